```python
import jax, jax.numpy as jnp
from jax import lax
import numpy as np

D_MODEL = 1024
BATCH = 32
SEQ = 2048
DEPTH = 2

N_BRANCH = 4
BRANCH_W = D_MODEL // 2
N_GROUPS = 4
GROUP_W = BRANCH_W // N_GROUPS
POOL_WINDOWS = (2, 4, 8, 16)
CONV_K = 31
SHORT_K = 3
CHUNK = 128
N_PIECES = 12
N_BRANCH_COLS = N_PIECES * BRANCH_W
IN_COLS = N_BRANCH_COLS + N_BRANCH * D_MODEL
RMS_EPS = 1e-6
LN_EPS = 1e-5

kernel_name = "hybrid_gated_parallel_mixers"


def rms_norm(x, g):
    xf = x.astype(jnp.float32)
    y = xf * lax.rsqrt(jnp.mean(xf * xf, axis=-1, keepdims=True) + RMS_EPS)
    return (y * g.astype(jnp.float32)).astype(x.dtype)


def layer_norm(x, g, b):
    xf = x.astype(jnp.float32)
    mu = jnp.mean(xf, axis=-1, keepdims=True)
    var = jnp.mean(jnp.square(xf - mu), axis=-1, keepdims=True)
    y = (xf - mu) * lax.rsqrt(var + LN_EPS)
    return (y * g.astype(jnp.float32) + b.astype(jnp.float32)).astype(x.dtype)


def causal_dwconv(x, w):
    k, c = w.shape
    return lax.conv_general_dilated(
        x, w[:, None, :].astype(x.dtype), window_strides=(1,), padding=[(k - 1, 0)],
        dimension_numbers=("NWC", "WIO", "NWC"), feature_group_count=c)


def pool_mixer(xa, pool_w, pool_scale):
    b, s, _ = xa.shape
    xf = xa.astype(jnp.float32)
    csum = jnp.cumsum(xf, axis=1)
    t = jnp.arange(1, s + 1, dtype=jnp.float32)
    groups = []
    for j, win in enumerate(POOL_WINDOWS):
        cj = csum[..., j * GROUP_W:(j + 1) * GROUP_W]
        prev = jnp.pad(cj, ((0, 0), (win, 0), (0, 0)))[:, :s]
        mean = (cj - prev) / jnp.minimum(t, float(win))[None, :, None]
        groups.append(mean - xf[..., j * GROUP_W:(j + 1) * GROUP_W])
    pooled = jnp.stack(groups, axis=2)
    mixed = jnp.einsum("bsgc,gcd->bsgd", pooled, pool_w.astype(jnp.float32))
    return (mixed.reshape(b, s, BRANCH_W) * pool_scale.astype(jnp.float32)).astype(xa.dtype)


def conformer_conv(a, gb, conv_w, conv_b, ln_g, ln_b):
    y = a * jax.nn.sigmoid(gb)
    y = causal_dwconv(y, conv_w) + conv_b.astype(y.dtype)
    return jax.nn.silu(layer_norm(y, ln_g, ln_b))


def spatial_gating(u, v, ln_g, ln_b, sgu_w, sgu_b):
    b, s, _ = u.shape
    v = layer_norm(v, ln_g, ln_b).reshape(b, s // CHUNK, CHUNK, N_GROUPS, GROUP_W)
    mask = jnp.tril(jnp.ones((CHUNK, CHUNK), dtype=v.dtype))
    ws = sgu_w.astype(v.dtype) * mask[None]
    sp = jnp.einsum("gts,bnsgc->bntgc", ws, v) + sgu_b.T.astype(v.dtype)[None, None, :, :, None]
    return u * sp.reshape(b, s, BRANCH_W)


def short_gated_conv(bg, cg, xs, sc_w):
    return bg * causal_dwconv(cg * xs, sc_w)


def _fwd_setup_inputs(seed: int = 0) -> dict:
    key = jax.random.key(seed)
    ks = jax.random.split(key, 20)
    f32 = jnp.float32
    nrm = lambda k, shape, scale: jax.random.normal(k, shape, f32) * scale
    return {
        "x": jax.random.normal(ks[0], (BATCH, SEQ, D_MODEL), f32),
        "norm_g": 1.0 + nrm(ks[1], (DEPTH, D_MODEL), 0.02),
        "w_in": nrm(ks[2], (DEPTH, D_MODEL, IN_COLS), D_MODEL ** -0.5),
        "pool_w": nrm(ks[3], (DEPTH, N_GROUPS, GROUP_W, GROUP_W), GROUP_W ** -0.5),
        "pool_scale": 1.0 + nrm(ks[4], (DEPTH, BRANCH_W), 0.02),
        "conv_w": nrm(ks[5], (DEPTH, CONV_K, BRANCH_W), CONV_K ** -0.5),
        "conv_b": nrm(ks[6], (DEPTH, BRANCH_W), 0.01),
        "conv_ln_g": 1.0 + nrm(ks[7], (DEPTH, BRANCH_W), 0.02),
        "conv_ln_b": nrm(ks[8], (DEPTH, BRANCH_W), 0.01),
        "sgu_ln_g": 1.0 + nrm(ks[9], (DEPTH, BRANCH_W), 0.02),
        "sgu_ln_b": nrm(ks[10], (DEPTH, BRANCH_W), 0.01),
        "sgu_w": nrm(ks[11], (DEPTH, N_GROUPS, CHUNK, CHUNK), CHUNK ** -0.5),
        "sgu_b": 1.0 + nrm(ks[12], (DEPTH, N_GROUPS, CHUNK), 0.01),
        "sc_w": nrm(ks[13], (DEPTH, SHORT_K, BRANCH_W), SHORT_K ** -0.5),
        "w_branch": nrm(ks[14], (DEPTH, N_BRANCH, BRANCH_W, D_MODEL), BRANCH_W ** -0.5),
        "w_o": nrm(ks[15], (DEPTH, D_MODEL, D_MODEL), D_MODEL ** -0.5),
        "final_g": 1.0 + nrm(ks[16], (D_MODEL,), 0.02),
    }


def _fwd_reference(x, norm_g, w_in, pool_w, pool_scale, conv_w, conv_b, conv_ln_g, conv_ln_b,
              sgu_ln_g, sgu_ln_b, sgu_w, sgu_b, sc_w, w_branch, w_o, final_g):
    b, s, d = x.shape
    for l in range(DEPTH):
        h = rms_norm(x, norm_g[l])
        proj = jnp.einsum("bsd,dk->bsk", h, w_in[l].astype(h.dtype))
        (p_x, p_gate, c_a, c_b, c_gate, g_u, g_v, g_gate,
         s_b, s_c, s_x, s_gate) = jnp.split(proj[..., :N_BRANCH_COLS], N_PIECES, axis=-1)
        merge_gates = jax.nn.sigmoid(proj[..., N_BRANCH_COLS:].reshape(b, s, N_BRANCH, d))

        z_pool = pool_mixer(p_x, pool_w[l], pool_scale[l]) * jax.nn.silu(p_gate)
        z_conv = conformer_conv(c_a, c_b, conv_w[l], conv_b[l], conv_ln_g[l], conv_ln_b[l]) * jax.nn.silu(c_gate)
        z_sgu = spatial_gating(g_u, g_v, sgu_ln_g[l], sgu_ln_b[l], sgu_w[l], sgu_b[l]) * jax.nn.silu(g_gate)
        z_sc = short_gated_conv(s_b, s_c, s_x, sc_w[l]) * jax.nn.silu(s_gate)

        z = jnp.stack([z_pool, z_conv, z_sgu, z_sc], axis=2)
        branch_out = jnp.einsum("bsnc,ncd->bsnd", z, w_branch[l].astype(z.dtype))
        merged = jnp.sum(merge_gates * branch_out, axis=2)
        x = x + jnp.einsum("bsd,de->bse", merged, w_o[l].astype(merged.dtype))
    return rms_norm(x, final_g)


import jax as _jax
import jax.numpy as _jnp

TWIN_FORMAT = 'train_step'
FWD_PARAMS = ['x', 'norm_g', 'w_in', 'pool_w', 'pool_scale', 'conv_w', 'conv_b', 'conv_ln_g', 'conv_ln_b', 'sgu_ln_g', 'sgu_ln_b', 'sgu_w', 'sgu_b', 'sc_w', 'w_branch', 'w_o', 'final_g']
TWIN_WEIGHTS = ['norm_g', 'w_in', 'pool_w', 'pool_scale', 'conv_w', 'conv_b', 'conv_ln_g', 'conv_ln_b', 'sgu_ln_g', 'sgu_ln_b', 'sgu_w', 'sgu_b', 'sc_w', 'w_branch', 'w_o', 'final_g']
TWIN_DIFF_INPUT = 'x'
TWIN_INPUTS = ['x', 'norm_g', 'w_in', 'pool_w', 'pool_scale', 'conv_w', 'conv_b', 'conv_ln_g', 'conv_ln_b', 'sgu_ln_g', 'sgu_ln_b', 'sgu_w', 'sgu_b', 'sc_w', 'w_branch', 'w_o', 'final_g', 'loss_target', 'm_norm_g', 'm_w_in', 'm_pool_w', 'm_pool_scale', 'm_conv_w', 'm_conv_b', 'm_conv_ln_g', 'm_conv_ln_b', 'm_sgu_ln_g', 'm_sgu_ln_b', 'm_sgu_w', 'm_sgu_b', 'm_sc_w', 'm_w_branch', 'm_w_o', 'm_final_g', 'v_norm_g', 'v_w_in', 'v_pool_w', 'v_pool_scale', 'v_conv_w', 'v_conv_b', 'v_conv_ln_g', 'v_conv_ln_b', 'v_sgu_ln_g', 'v_sgu_ln_b', 'v_sgu_w', 'v_sgu_b', 'v_sc_w', 'v_w_branch', 'v_w_o', 'v_final_g']
TWIN_OUTPUTS = ['loss', 'grad_x', 'grad_norm_g', 'grad_w_in', 'grad_pool_w', 'grad_pool_scale', 'grad_conv_w', 'grad_conv_b', 'grad_conv_ln_g', 'grad_conv_ln_b', 'grad_sgu_ln_g', 'grad_sgu_ln_b', 'grad_sgu_w', 'grad_sgu_b', 'grad_sc_w', 'grad_w_branch', 'grad_w_o', 'grad_final_g', 'delta_norm_g', 'delta_w_in', 'delta_pool_w', 'delta_pool_scale', 'delta_conv_w', 'delta_conv_b', 'delta_conv_ln_g', 'delta_conv_ln_b', 'delta_sgu_ln_g', 'delta_sgu_ln_b', 'delta_sgu_w', 'delta_sgu_b', 'delta_sc_w', 'delta_w_branch', 'delta_w_o', 'delta_final_g', 'new_m_norm_g', 'new_m_w_in', 'new_m_pool_w', 'new_m_pool_scale', 'new_m_conv_w', 'new_m_conv_b', 'new_m_conv_ln_g', 'new_m_conv_ln_b', 'new_m_sgu_ln_g', 'new_m_sgu_ln_b', 'new_m_sgu_w', 'new_m_sgu_b', 'new_m_sc_w', 'new_m_w_branch', 'new_m_w_o', 'new_m_final_g', 'new_v_norm_g', 'new_v_w_in', 'new_v_pool_w', 'new_v_pool_scale', 'new_v_conv_w', 'new_v_conv_b', 'new_v_conv_ln_g', 'new_v_conv_ln_b', 'new_v_sgu_ln_g', 'new_v_sgu_ln_b', 'new_v_sgu_w', 'new_v_sgu_b', 'new_v_sc_w', 'new_v_w_branch', 'new_v_w_o', 'new_v_final_g']
TWIN_LEAF_KINDS = {'loss': 'loss', 'grad_x': 'grad_x', 'grad_norm_g': 'grad_w', 'grad_w_in': 'grad_w', 'grad_pool_w': 'grad_w', 'grad_pool_scale': 'grad_w', 'grad_conv_w': 'grad_w', 'grad_conv_b': 'grad_w', 'grad_conv_ln_g': 'grad_w', 'grad_conv_ln_b': 'grad_w', 'grad_sgu_ln_g': 'grad_w', 'grad_sgu_ln_b': 'grad_w', 'grad_sgu_w': 'grad_w', 'grad_sgu_b': 'grad_w', 'grad_sc_w': 'grad_w', 'grad_w_branch': 'grad_w', 'grad_w_o': 'grad_w', 'grad_final_g': 'grad_w', 'delta_norm_g': 'delta_w', 'delta_w_in': 'delta_w', 'delta_pool_w': 'delta_w', 'delta_pool_scale': 'delta_w', 'delta_conv_w': 'delta_w', 'delta_conv_b': 'delta_w', 'delta_conv_ln_g': 'delta_w', 'delta_conv_ln_b': 'delta_w', 'delta_sgu_ln_g': 'delta_w', 'delta_sgu_ln_b': 'delta_w', 'delta_sgu_w': 'delta_w', 'delta_sgu_b': 'delta_w', 'delta_sc_w': 'delta_w', 'delta_w_branch': 'delta_w', 'delta_w_o': 'delta_w', 'delta_final_g': 'delta_w', 'new_m_norm_g': 'new_m', 'new_m_w_in': 'new_m', 'new_m_pool_w': 'new_m', 'new_m_pool_scale': 'new_m', 'new_m_conv_w': 'new_m', 'new_m_conv_b': 'new_m', 'new_m_conv_ln_g': 'new_m', 'new_m_conv_ln_b': 'new_m', 'new_m_sgu_ln_g': 'new_m', 'new_m_sgu_ln_b': 'new_m', 'new_m_sgu_w': 'new_m', 'new_m_sgu_b': 'new_m', 'new_m_sc_w': 'new_m', 'new_m_w_branch': 'new_m', 'new_m_w_o': 'new_m', 'new_m_final_g': 'new_m', 'new_v_norm_g': 'new_v', 'new_v_w_in': 'new_v', 'new_v_pool_w': 'new_v', 'new_v_pool_scale': 'new_v', 'new_v_conv_w': 'new_v', 'new_v_conv_b': 'new_v', 'new_v_conv_ln_g': 'new_v', 'new_v_conv_ln_b': 'new_v', 'new_v_sgu_ln_g': 'new_v', 'new_v_sgu_ln_b': 'new_v', 'new_v_sgu_w': 'new_v', 'new_v_sgu_b': 'new_v', 'new_v_sc_w': 'new_v', 'new_v_w_branch': 'new_v', 'new_v_w_o': 'new_v', 'new_v_final_g': 'new_v'}


def _forward(args):
    return _fwd_reference(*[args[k] for k in FWD_PARAMS])


def _output_shape():
    out = _jax.eval_shape(lambda: _forward(_fwd_setup_inputs(0)))
    return out.shape, out.dtype

N_MICROBATCH = 1
ADAM_LR = 0.001
ADAM_B1 = 0.9
ADAM_B2 = 0.999
ADAM_EPS = 1e-08
ADAM_WD = 0.01
ADAM_STEP = 10
PER_EXAMPLE_BATCH_AXIS = {'x': 0, 'loss_target': 0}
SHARED_INPUTS = []
_WEIGHT_DTYPES = {'norm_g': _jnp.float32, 'w_in': _jnp.float32, 'pool_w': _jnp.float32, 'pool_scale': _jnp.float32, 'conv_w': _jnp.float32, 'conv_b': _jnp.float32, 'conv_ln_g': _jnp.float32, 'conv_ln_b': _jnp.float32, 'sgu_ln_g': _jnp.float32, 'sgu_ln_b': _jnp.float32, 'sgu_w': _jnp.float32, 'sgu_b': _jnp.float32, 'sc_w': _jnp.float32, 'w_branch': _jnp.float32, 'w_o': _jnp.float32, 'final_g': _jnp.float32}
MOMENT_SCALE = {'norm_g': 2.509062e-01, 'w_in': 7.517900e-02, 'pool_w': 9.167991e-02, 'pool_scale': 9.051708e-02, 'conv_w': 6.368627e-02, 'conv_b': 1.362562e-01, 'conv_ln_g': 7.485702e-02, 'conv_ln_b': 6.499837e-02, 'sgu_ln_g': 7.025697e-02, 'sgu_ln_b': 6.802701e-02, 'sgu_w': 7.244047e-02, 'sgu_b': 1.017053e-01, 'sc_w': 1.038438e-01, 'w_branch': 6.930734e-02, 'w_o': 1.386425e-01, 'final_g': 6.407271e+01}


def _to_microbatches(a, axis):
    t = _jnp.moveaxis(a, axis, 0)
    t = t.reshape((N_MICROBATCH, t.shape[0] // N_MICROBATCH) + t.shape[1:])
    return _jnp.moveaxis(t, 1, axis + 1)


def setup_inputs(seed: int = 0) -> dict:
    inp = _fwd_setup_inputs(seed)
    key = _jax.random.fold_in(_jax.random.key(seed), 7919)
    shape, _ = _output_shape()
    out = dict(inp)
    out["loss_target"] = _jax.random.normal(_jax.random.fold_in(key, 0), shape, _jnp.float32)
    for i, name in enumerate(TWIN_WEIGHTS):
        w = inp[name].astype(_jnp.float32)
        if MOMENT_SCALE is None:
            s = _jnp.sqrt(_jnp.mean(_jnp.square(w)) + 1e-30)
        else:
            s = MOMENT_SCALE[name]
        km, kv = _jax.random.split(_jax.random.fold_in(key, i + 1))
        out[name] = w
        out["m_" + name] = s * _jax.random.normal(km, w.shape, _jnp.float32)
        out["v_" + name] = (s * s) * _jax.random.uniform(kv, w.shape, _jnp.float32, 0.5, 1.5)
    if N_MICROBATCH > 1:
        for name, axis in PER_EXAMPLE_BATCH_AXIS.items():
            out[name] = _to_microbatches(out[name], axis)
    return {'x': out['x'], 'norm_g': out['norm_g'], 'w_in': out['w_in'], 'pool_w': out['pool_w'], 'pool_scale': out['pool_scale'], 'conv_w': out['conv_w'], 'conv_b': out['conv_b'], 'conv_ln_g': out['conv_ln_g'], 'conv_ln_b': out['conv_ln_b'], 'sgu_ln_g': out['sgu_ln_g'], 'sgu_ln_b': out['sgu_ln_b'], 'sgu_w': out['sgu_w'], 'sgu_b': out['sgu_b'], 'sc_w': out['sc_w'], 'w_branch': out['w_branch'], 'w_o': out['w_o'], 'final_g': out['final_g'], 'loss_target': out['loss_target'], 'm_norm_g': out['m_norm_g'], 'm_w_in': out['m_w_in'], 'm_pool_w': out['m_pool_w'], 'm_pool_scale': out['m_pool_scale'], 'm_conv_w': out['m_conv_w'], 'm_conv_b': out['m_conv_b'], 'm_conv_ln_g': out['m_conv_ln_g'], 'm_conv_ln_b': out['m_conv_ln_b'], 'm_sgu_ln_g': out['m_sgu_ln_g'], 'm_sgu_ln_b': out['m_sgu_ln_b'], 'm_sgu_w': out['m_sgu_w'], 'm_sgu_b': out['m_sgu_b'], 'm_sc_w': out['m_sc_w'], 'm_w_branch': out['m_w_branch'], 'm_w_o': out['m_w_o'], 'm_final_g': out['m_final_g'], 'v_norm_g': out['v_norm_g'], 'v_w_in': out['v_w_in'], 'v_pool_w': out['v_pool_w'], 'v_pool_scale': out['v_pool_scale'], 'v_conv_w': out['v_conv_w'], 'v_conv_b': out['v_conv_b'], 'v_conv_ln_g': out['v_conv_ln_g'], 'v_conv_ln_b': out['v_conv_ln_b'], 'v_sgu_ln_g': out['v_sgu_ln_g'], 'v_sgu_ln_b': out['v_sgu_ln_b'], 'v_sgu_w': out['v_sgu_w'], 'v_sgu_b': out['v_sgu_b'], 'v_sc_w': out['v_sc_w'], 'v_w_branch': out['v_w_branch'], 'v_w_o': out['v_w_o'], 'v_final_g': out['v_final_g']}


def _loss(weights, diff, rest, loss_target):
    with _jax.named_scope("forward"):
        args = {**rest, TWIN_DIFF_INPUT: diff, **{k: w.astype(_WEIGHT_DTYPES[k]) for k, w in weights.items()}}
        y = _forward(args)
    with _jax.named_scope("loss_head"):
        err = _jnp.square(y.astype(_jnp.float32) - loss_target)
        return 0.5 * _jnp.sum(_jnp.mean(err, axis=-1)) if err.ndim else 0.5 * err


def _adamw(w, g, m, v):
    m = ADAM_B1 * m + (1.0 - ADAM_B1) * g
    v = ADAM_B2 * v + (1.0 - ADAM_B2) * _jnp.square(g)
    m_hat = m / (1.0 - ADAM_B1 ** ADAM_STEP)
    v_hat = v / (1.0 - ADAM_B2 ** ADAM_STEP)
    delta = -ADAM_LR * (m_hat / (_jnp.sqrt(v_hat) + ADAM_EPS) + ADAM_WD * w)
    return delta, m, v


def reference(x, norm_g, w_in, pool_w, pool_scale, conv_w, conv_b, conv_ln_g, conv_ln_b, sgu_ln_g, sgu_ln_b, sgu_w, sgu_b, sc_w, w_branch, w_o, final_g, loss_target, m_norm_g, m_w_in, m_pool_w, m_pool_scale, m_conv_w, m_conv_b, m_conv_ln_g, m_conv_ln_b, m_sgu_ln_g, m_sgu_ln_b, m_sgu_w, m_sgu_b, m_sc_w, m_w_branch, m_w_o, m_final_g, v_norm_g, v_w_in, v_pool_w, v_pool_scale, v_conv_w, v_conv_b, v_conv_ln_g, v_conv_ln_b, v_sgu_ln_g, v_sgu_ln_b, v_sgu_w, v_sgu_b, v_sc_w, v_w_branch, v_w_o, v_final_g):
    given = dict(x=x, norm_g=norm_g, w_in=w_in, pool_w=pool_w, pool_scale=pool_scale, conv_w=conv_w, conv_b=conv_b, conv_ln_g=conv_ln_g, conv_ln_b=conv_ln_b, sgu_ln_g=sgu_ln_g, sgu_ln_b=sgu_ln_b, sgu_w=sgu_w, sgu_b=sgu_b, sc_w=sc_w, w_branch=w_branch, w_o=w_o, final_g=final_g, loss_target=loss_target, m_norm_g=m_norm_g, m_w_in=m_w_in, m_pool_w=m_pool_w, m_pool_scale=m_pool_scale, m_conv_w=m_conv_w, m_conv_b=m_conv_b, m_conv_ln_g=m_conv_ln_g, m_conv_ln_b=m_conv_ln_b, m_sgu_ln_g=m_sgu_ln_g, m_sgu_ln_b=m_sgu_ln_b, m_sgu_w=m_sgu_w, m_sgu_b=m_sgu_b, m_sc_w=m_sc_w, m_w_branch=m_w_branch, m_w_o=m_w_o, m_final_g=m_final_g, v_norm_g=v_norm_g, v_w_in=v_w_in, v_pool_w=v_pool_w, v_pool_scale=v_pool_scale, v_conv_w=v_conv_w, v_conv_b=v_conv_b, v_conv_ln_g=v_conv_ln_g, v_conv_ln_b=v_conv_ln_b, v_sgu_ln_g=v_sgu_ln_g, v_sgu_ln_b=v_sgu_ln_b, v_sgu_w=v_sgu_w, v_sgu_b=v_sgu_b, v_sc_w=v_sc_w, v_w_branch=v_w_branch, v_w_o=v_w_o, v_final_g=v_final_g)
    weights = {n: given[n] for n in TWIN_WEIGHTS}
    shared = {n: given[n] for n in SHARED_INPUTS}
    per_example = {n: given[n] for n in ['x']}
    grad_fn = _jax.value_and_grad(_loss, argnums=(0, 1))

    def one_microbatch(ex, loss_target):
        ex = dict(ex)
        diff = ex.pop(TWIN_DIFF_INPUT)
        return grad_fn(weights, diff, {**shared, **ex}, loss_target)

    if N_MICROBATCH == 1:
        loss, (grad_w, grad_x) = one_microbatch(per_example, given["loss_target"])
    else:
        def body(carry, xs):
            loss_sum, grad_sum = carry
            l_k, (gw_k, gx_k) = one_microbatch(xs[0], xs[1])
            with _jax.named_scope("update"):
                return (loss_sum + l_k, _jax.tree.map(_jnp.add, grad_sum, gw_k)), gx_k

        init = (_jnp.zeros((), _jnp.float32), _jax.tree.map(_jnp.zeros_like, weights))
        (loss, grad_w), grad_x = _jax.lax.scan(body, init, (per_example, given["loss_target"]))
    with _jax.named_scope("update"):
        delta_w, new_m, new_v = {}, {}, {}
        for n in TWIN_WEIGHTS:
            delta_w[n], new_m[n], new_v[n] = _adamw(weights[n], grad_w[n], given["m_" + n], given["v_" + n])
    return (loss, grad_x, *[grad_w[n] for n in TWIN_WEIGHTS], *[delta_w[n] for n in TWIN_WEIGHTS],
            *[new_m[n] for n in TWIN_WEIGHTS], *[new_v[n] for n in TWIN_WEIGHTS])
```

```python
import functools

import jax
import jax.numpy as jnp
from jax import lax
from jax.experimental import pallas as pl
from jax.experimental.pallas import tpu as pltpu

F32 = jnp.float32
BF16 = jnp.bfloat16

D_MODEL = 1024
DEPTH = 2
N_BRANCH = 4
BRANCH_W = 512
N_GROUPS = 4
GROUP_W = 128
POOL_WINDOWS = (2, 4, 8, 16)
CONV_K = 31
SHORT_K = 3
CHUNK = 128
N_PIECES = 12
PIECE_COLS = N_PIECES * BRANCH_W
IN_COLS = PIECE_COLS + N_BRANCH * D_MODEL
N_CHIPS = 4
SHARD_COLS = IN_COLS // N_CHIPS
BR_SHARD = D_MODEL // N_CHIPS
GATE_BLOCK0 = PIECE_COLS // D_MODEL
RMS_EPS = 1e-6
LN_EPS = 1e-5
HALO = 32
CONV_ROWS = 32
SHORT_ROWS = 8

ADAM_LR = 0.001
ADAM_B1 = 0.9
ADAM_B2 = 0.999
ADAM_EPS = 1e-08
ADAM_WD = 0.01
ADAM_STEP = 10

VMEM_LIMIT = 52 * 1024 * 1024
MESH_ID = pl.DeviceIdType.MESH
ANY = pl.BlockSpec(memory_space=pl.ANY)
VMEM_WHOLE = pl.BlockSpec(memory_space=pltpu.VMEM)

(P_X, P_GATE, C_A, C_B, C_GATE, G_U, G_V, G_GATE, S_B, S_C, S_X, S_GATE) = range(N_PIECES)


def _params(*sem):
    return pltpu.CompilerParams(dimension_semantics=sem, vmem_limit_bytes=VMEM_LIMIT)


def _sigmoid(v):
    return jax.nn.sigmoid(v)


def _silu(v):
    return v * _sigmoid(v)


def _dsilu(v):
    s = _sigmoid(v)
    return s * (1.0 + v * (1.0 - s))


def _dot(a, b):
    return jnp.dot(a, b, preferred_element_type=F32)


def _dot_nt(a, b):
    return lax.dot_general(a, b, (((1,), (1,)), ((), ())), preferred_element_type=F32)


def _dot_tn(a, b):
    return lax.dot_general(a, b, (((0,), (0,)), ((), ())), preferred_element_type=F32)


def _rowsum(v):
    return jnp.sum(v, axis=0, keepdims=True)


def _lanemean(v):
    return jnp.mean(v, axis=-1, keepdims=True)


def _tile(n, want, mult=8):
    t = max(1, min(n, want))
    while n % t or (t % mult and t != n):
        t -= 1
    return t


def _inproj_fwd(x2, g_row, w_full, layer):
    n = x2.shape[0]
    tm = _tile(n, 512)

    def body(x_ref, g_ref, w_ref, p_ref, h_ref):
        @pl.when(pl.program_id(1) == 0)
        def _():
            xv = x_ref[...]
            r = lax.rsqrt(_lanemean(xv * xv) + RMS_EPS)
            h_ref[...] = (xv * r * g_ref[...]).astype(BF16)

        p_ref[...] = _dot(h_ref[...], w_ref[...]).astype(BF16)

    return pl.pallas_call(
        body,
        name=f"inproj_fwd_l{layer}",
        grid=(n // tm, N_CHIPS),
        in_specs=[
            pl.BlockSpec((tm, D_MODEL), lambda i, s: (i, 0)),
            pl.BlockSpec((1, D_MODEL), lambda i, s: (0, 0)),
            pl.BlockSpec((None, None, D_MODEL, SHARD_COLS), lambda i, s: (layer, s, 0, 0)),
        ],
        out_specs=[
            pl.BlockSpec((tm, SHARD_COLS), lambda i, s: (i, s)),
            pl.BlockSpec((tm, D_MODEL), lambda i, s: (i, 0)),
        ],
        out_shape=[jax.ShapeDtypeStruct((n, IN_COLS), BF16), jax.ShapeDtypeStruct((n, D_MODEL), BF16)],
        compiler_params=_params("arbitrary", "arbitrary"),
    )(x2, g_row, w_full)


def _layer_norm_parts(v, g, b):
    mu = _lanemean(v)
    d = v - mu
    rstd = lax.rsqrt(_lanemean(d * d) + LN_EPS)
    xh = d * rstd
    return xh, rstd, xh * g + b


def _tril_mask():
    r = lax.broadcasted_iota(jnp.int32, (CHUNK, CHUNK), 0)
    c = lax.broadcasted_iota(jnp.int32, (CHUNK, CHUNK), 1)
    return r >= c


def _lanes(refs4):
    return jnp.concatenate([refs4[g] for g in range(N_CHIPS)], axis=1)


def _mixer_weight_specs(layer):
    def whole(shape):
        nd = len(shape)
        return pl.BlockSpec((None,) + shape, lambda b, c: (layer,) + (0,) * nd)

    return [
        whole((N_GROUPS, GROUP_W, GROUP_W)),
        whole((1, BRANCH_W)),
        whole((N_CHIPS, CONV_ROWS, GROUP_W)),
        whole((1, BRANCH_W)),
        whole((1, BRANCH_W)),
        whole((1, BRANCH_W)),
        whole((1, BRANCH_W)),
        whole((1, BRANCH_W)),
        whole((N_GROUPS, CHUNK, CHUNK)),
        whole((CHUNK, BRANCH_W)),
        whole((N_CHIPS, SHORT_ROWS, GROUP_W)),
    ]


def _mixer_weight_args(mw):
    return [mw["pool_w"], mw["pool_scale"], mw["conv_w"], mw["conv_b"], mw["conv_ln_g"], mw["conv_ln_b"],
            mw["sgu_ln_g"], mw["sgu_ln_b"], mw["sgu_w"], mw["sgu_bias"], mw["sc_w"]]


def _mixers_fwd(proj3, mw, layer):
    nb, seq, _ = proj3.shape
    t_rows = _tile(seq, 256)
    nc = seq // t_rows
    hb = t_rows // HALO

    def body(cur_ref, halo_ref, pw_ref, ps_ref, cw_ref, cb_ref, clg_ref, clb_ref, slg_ref, slb_ref, sw_ref,
             sbias_ref, scw_ref, z_ref, ext_ref):
        c = pl.program_id(1)
        has_prev = c > 0
        row = lax.broadcasted_iota(jnp.int32, (t_rows, 1), 0)
        tpos = (c * t_rows + row + 1).astype(F32)

        def cur(k):
            return cur_ref[:, k * BRANCH_W:(k + 1) * BRANCH_W].astype(F32)

        def hal(k):
            return halo_ref[:, k * BRANCH_W:(k + 1) * BRANCH_W].astype(F32)

        def put_ext(slot, halo_val, cur_val):
            ext_ref[slot, 0:HALO, :] = jnp.where(has_prev, halo_val, 0.0)
            ext_ref[slot, HALO:HALO + t_rows, :] = cur_val

        px = cur(P_X)
        put_ext(0, hal(P_X), px)
        mixed = []
        for j, win in enumerate(POOL_WINDOWS):
            cols = slice(j * GROUP_W, (j + 1) * GROUP_W)
            s = px[:, cols]
            for i in range(1, win):
                s = s + ext_ref[0, pl.ds(HALO - i, t_rows), cols]
            pooled = s / jnp.minimum(tpos, float(win)) - px[:, cols]
            mixed.append(_dot(pooled.astype(BF16), pw_ref[j]))
        z_pool = jnp.concatenate(mixed, axis=1) * ps_ref[...] * _silu(cur(P_GATE))
        z_ref[:, 0:BRANCH_W] = z_pool.astype(BF16)

        put_ext(1, hal(C_A) * _sigmoid(hal(C_B)), cur(C_A) * _sigmoid(cur(C_B)))
        cw = _lanes(cw_ref)
        acc = jnp.broadcast_to(cb_ref[...], (t_rows, BRANCH_W))
        for k in range(CONV_K):
            acc = acc + cw[k:k + 1, :] * ext_ref[1, pl.ds(HALO - (CONV_K - 1) + k, t_rows), :]
        _, _, ln = _layer_norm_parts(acc, clg_ref[...], clb_ref[...])
        z_ref[:, BRANCH_W:2 * BRANCH_W] = (_silu(ln) * _silu(cur(C_GATE))).astype(BF16)

        _, _, v = _layer_norm_parts(cur(G_V), slg_ref[...], slb_ref[...])
        vb = v.astype(BF16)
        mask = _tril_mask()
        wt = [jnp.where(mask, sw_ref[g], 0.0).astype(BF16) for g in range(N_GROUPS)]
        sp_rows = []
        for sub in range(t_rows // CHUNK):
            rows = slice(sub * CHUNK, (sub + 1) * CHUNK)
            sp_rows.append(jnp.concatenate(
                [_dot(wt[g], vb[rows, g * GROUP_W:(g + 1) * GROUP_W]) for g in range(N_GROUPS)], axis=1)
                + sbias_ref[...])
        sp = jnp.concatenate(sp_rows, axis=0)
        z_ref[:, 2 * BRANCH_W:3 * BRANCH_W] = (cur(G_U) * sp * _silu(cur(G_GATE))).astype(BF16)

        put_ext(2, hal(S_C) * hal(S_X), cur(S_C) * cur(S_X))
        scw = _lanes(scw_ref)
        cv = jnp.zeros((t_rows, BRANCH_W), F32)
        for k in range(SHORT_K):
            cv = cv + scw[k:k + 1, :] * ext_ref[2, pl.ds(HALO - (SHORT_K - 1) + k, t_rows), :]
        z_ref[:, 3 * BRANCH_W:4 * BRANCH_W] = (cur(S_B) * cv * _silu(cur(S_GATE))).astype(BF16)

    return pl.pallas_call(
        body,
        name=f"mixers_fwd_l{layer}",
        grid=(nb, nc),
        in_specs=[
            pl.BlockSpec((None, t_rows, PIECE_COLS), lambda b, c: (b, c, 0)),
            pl.BlockSpec((None, HALO, PIECE_COLS), lambda b, c: (b, jnp.maximum(c * hb - 1, 0), 0)),
        ] + _mixer_weight_specs(layer),
        out_specs=pl.BlockSpec((None, t_rows, N_BRANCH * BRANCH_W), lambda b, c: (b, c, 0)),
        out_shape=jax.ShapeDtypeStruct((nb, seq, N_BRANCH * BRANCH_W), BF16),
        scratch_shapes=[pltpu.VMEM((3, HALO + t_rows, BRANCH_W), F32)],
        compiler_params=_params("arbitrary", "arbitrary"),
    )(proj3, proj3, *_mixer_weight_args(mw))


def _mixers_bwd(proj3, dz3, dproj3, mw, layer):
    nb, seq, _ = proj3.shape
    t_rows = _tile(seq, 256)
    nc = seq // t_rows
    hb = t_rows // HALO

    def body(cur_ref, halo_ref, dz_ref, pw_ref, ps_ref, cw_ref, cb_ref, clg_ref, clb_ref, slg_ref, slb_ref,
             sw_ref, sbias_ref, scw_ref, dp_in_ref, dp_ref, g_pw, g_ps, g_cw, g_cb, g_clg, g_clb, g_slg,
             g_slb, g_sw, g_sb, g_scw, ext_ref, nxt_ref, sb_acc):
        del dp_in_ref
        b = pl.program_id(0)
        r = pl.program_id(1)
        c = nc - 1 - r
        has_prev = c > 0
        row = lax.broadcasted_iota(jnp.int32, (t_rows, 1), 0)
        tpos = (c * t_rows + row + 1).astype(F32)

        @pl.when(jnp.logical_and(b == 0, r == 0))
        def _():
            for ref in (g_pw, g_ps, g_cw, g_cb, g_clg, g_clb, g_slg, g_slb, g_sw, g_scw, sb_acc):
                ref[...] = jnp.zeros_like(ref)

        @pl.when(r == 0)
        def _():
            nxt_ref[:, t_rows:t_rows + HALO, :] = jnp.zeros((3, HALO, BRANCH_W), F32)

        def cur(k):
            return cur_ref[:, k * BRANCH_W:(k + 1) * BRANCH_W].astype(F32)

        def hal(k):
            return halo_ref[:, k * BRANCH_W:(k + 1) * BRANCH_W].astype(F32)

        def dzp(k):
            return dz_ref[:, k * BRANCH_W:(k + 1) * BRANCH_W].astype(F32)

        def put_dp(k, val):
            dp_ref[:, k * BRANCH_W:(k + 1) * BRANCH_W] = val.astype(BF16)

        def put_ext(slot, halo_val, cur_val):
            ext_ref[slot, 0:HALO, :] = jnp.where(has_prev, halo_val, 0.0)
            ext_ref[slot, HALO:HALO + t_rows, :] = cur_val

        px = cur(P_X)
        put_ext(0, hal(P_X), px)
        pgate = cur(P_GATE)
        dz_pool = dzp(0)
        ps = ps_ref[...]
        pooled, mixed, cnts = [], [], []
        for j, win in enumerate(POOL_WINDOWS):
            cols = slice(j * GROUP_W, (j + 1) * GROUP_W)
            s = px[:, cols]
            for i in range(1, win):
                s = s + ext_ref[0, pl.ds(HALO - i, t_rows), cols]
            cnt = jnp.minimum(tpos, float(win))
            pj = s / cnt - px[:, cols]
            cnts.append(cnt)
            pooled.append(pj.astype(BF16))
            mixed.append(_dot(pooled[j], pw_ref[j]))
        mixed = jnp.concatenate(mixed, axis=1)
        put_dp(P_GATE, dz_pool * (mixed * ps) * _dsilu(pgate))
        d_out = dz_pool * _silu(pgate)
        g_ps[...] += _rowsum(d_out * mixed)
        d_mixed = (d_out * ps).astype(BF16)
        d_pooled = []
        for j in range(N_GROUPS):
            cols = slice(j * GROUP_W, (j + 1) * GROUP_W)
            g_pw[j] += _dot_tn(pooled[j], d_mixed[:, cols])
            dpj = _dot_nt(d_mixed[:, cols], pw_ref[j])
            d_pooled.append(dpj)
            nxt_ref[0, 0:t_rows, cols] = dpj / cnts[j]
        dpx = []
        for j, win in enumerate(POOL_WINDOWS):
            cols = slice(j * GROUP_W, (j + 1) * GROUP_W)
            s = nxt_ref[0, 0:t_rows, cols]
            for i in range(1, win):
                s = s + nxt_ref[0, pl.ds(i, t_rows), cols]
            dpx.append(s - d_pooled[j])
        put_dp(P_X, jnp.concatenate(dpx, axis=1))

        c_a = cur(C_A)
        sig_b = _sigmoid(cur(C_B))
        put_ext(1, hal(C_A) * _sigmoid(hal(C_B)), c_a * sig_b)
        cw = _lanes(cw_ref)
        acc = jnp.broadcast_to(cb_ref[...], (t_rows, BRANCH_W))
        for k in range(CONV_K):
            acc = acc + cw[k:k + 1, :] * ext_ref[1, pl.ds(HALO - (CONV_K - 1) + k, t_rows), :]
        xh, rstd, ln = _layer_norm_parts(acc, clg_ref[...], clb_ref[...])
        cgate = cur(C_GATE)
        dz_conv = dzp(1)
        put_dp(C_GATE, dz_conv * _silu(ln) * _dsilu(cgate))
        d_ln = dz_conv * _silu(cgate) * _dsilu(ln)
        g_clg[...] += _rowsum(d_ln * xh)
        g_clb[...] += _rowsum(d_ln)
        dxh = d_ln * clg_ref[...]
        dc = rstd * (dxh - _lanemean(dxh) - xh * _lanemean(dxh * xh))
        g_cb[...] += _rowsum(dc)
        nxt_ref[1, 0:t_rows, :] = dc
        for k in range(CONV_K):
            g_cw[k:k + 1, :] += _rowsum(dc * ext_ref[1, pl.ds(HALO - (CONV_K - 1) + k, t_rows), :])
        dyg = jnp.zeros((t_rows, BRANCH_W), F32)
        for i in range(CONV_K):
            dyg = dyg + cw[CONV_K - 1 - i:CONV_K - i, :] * nxt_ref[1, pl.ds(i, t_rows), :]
        put_dp(C_A, dyg * sig_b)
        put_dp(C_B, dyg * c_a * sig_b * (1.0 - sig_b))

        u = cur(G_U)
        ggate = cur(G_GATE)
        vxh, vrstd, v = _layer_norm_parts(cur(G_V), slg_ref[...], slb_ref[...])
        vb = v.astype(BF16)
        mask = _tril_mask()
        wt = [jnp.where(mask, sw_ref[g], 0.0) for g in range(N_GROUPS)]
        wt_b = [w.astype(BF16) for w in wt]
        wtt_b = [w.T.astype(BF16) for w in wt]
        dz_sgu = dzp(2)
        d_sgu = dz_sgu * _silu(ggate)
        d_sp = d_sgu * u
        d_spb = d_sp.astype(BF16)
        sp_rows, dv_rows = [], []
        for sub in range(t_rows // CHUNK):
            rows = slice(sub * CHUNK, (sub + 1) * CHUNK)
            sp_g, dv_g = [], []
            for g in range(N_GROUPS):
                cols = slice(g * GROUP_W, (g + 1) * GROUP_W)
                sp_g.append(_dot(wt_b[g], vb[rows, cols]))
                g_sw[g] += jnp.where(mask, _dot_nt(d_spb[rows, cols], vb[rows, cols]), 0.0)
                dv_g.append(_dot(wtt_b[g], d_spb[rows, cols]))
            sp_rows.append(jnp.concatenate(sp_g, axis=1) + sbias_ref[...])
            dv_rows.append(jnp.concatenate(dv_g, axis=1))
            sb_acc[...] += d_sp[rows, :]
        sp = jnp.concatenate(sp_rows, axis=0)
        dv = jnp.concatenate(dv_rows, axis=0)
        put_dp(G_GATE, dz_sgu * (u * sp) * _dsilu(ggate))
        put_dp(G_U, d_sgu * sp)
        g_slg[...] += _rowsum(dv * vxh)
        g_slb[...] += _rowsum(dv)
        dvx = dv * slg_ref[...]
        put_dp(G_V, vrstd * (dvx - _lanemean(dvx) - vxh * _lanemean(dvx * vxh)))

        s_b, s_c, s_x, sgate = cur(S_B), cur(S_C), cur(S_X), cur(S_GATE)
        put_ext(2, hal(S_C) * hal(S_X), s_c * s_x)
        scw = _lanes(scw_ref)
        cv = jnp.zeros((t_rows, BRANCH_W), F32)
        for k in range(SHORT_K):
            cv = cv + scw[k:k + 1, :] * ext_ref[2, pl.ds(HALO - (SHORT_K - 1) + k, t_rows), :]
        dz_sc = dzp(3)
        put_dp(S_GATE, dz_sc * (s_b * cv) * _dsilu(sgate))
        d_pre = dz_sc * _silu(sgate)
        put_dp(S_B, d_pre * cv)
        dcv = d_pre * s_b
        nxt_ref[2, 0:t_rows, :] = dcv
        for k in range(SHORT_K):
            g_scw[k:k + 1, :] += _rowsum(dcv * ext_ref[2, pl.ds(HALO - (SHORT_K - 1) + k, t_rows), :])
        du = jnp.zeros((t_rows, BRANCH_W), F32)
        for i in range(SHORT_K):
            du = du + scw[SHORT_K - 1 - i:SHORT_K - i, :] * nxt_ref[2, pl.ds(i, t_rows), :]
        put_dp(S_C, du * s_x)
        put_dp(S_X, du * s_c)

        nxt_ref[:, t_rows:t_rows + HALO, :] = nxt_ref[:, 0:HALO, :]

        @pl.when(jnp.logical_and(b == nb - 1, r == nc - 1))
        def _():
            lane = lax.broadcasted_iota(jnp.int32, (CHUNK, GROUP_W), 1)
            out = jnp.zeros((CHUNK, GROUP_W), F32)
            for g in range(N_GROUPS):
                col = jnp.sum(sb_acc[:, g * GROUP_W:(g + 1) * GROUP_W], axis=1, keepdims=True)
                out = jnp.where(lane == g, col, out)
            g_sb[...] = out

    def acc_spec(shape):
        nd = len(shape)
        return pl.BlockSpec(shape, lambda b, r: (0,) * nd)

    acc_shapes = [
        (N_GROUPS, GROUP_W, GROUP_W),
        (1, BRANCH_W),
        (CONV_ROWS, BRANCH_W),
        (1, BRANCH_W),
        (1, BRANCH_W),
        (1, BRANCH_W),
        (1, BRANCH_W),
        (1, BRANCH_W),
        (N_GROUPS, CHUNK, CHUNK),
        (CHUNK, GROUP_W),
        (SHORT_ROWS, BRANCH_W),
    ]
    outs = pl.pallas_call(
        body,
        name=f"mixers_bwd_l{layer}",
        grid=(nb, nc),
        in_specs=[
            pl.BlockSpec((None, t_rows, PIECE_COLS), lambda b, r: (b, nc - 1 - r, 0)),
            pl.BlockSpec((None, HALO, PIECE_COLS), lambda b, r: (b, jnp.maximum((nc - 1 - r) * hb - 1, 0), 0)),
            pl.BlockSpec((None, t_rows, N_BRANCH * BRANCH_W), lambda b, r: (b, nc - 1 - r, 0)),
        ] + _mixer_weight_specs(layer) + [ANY],
        out_specs=[pl.BlockSpec((None, t_rows, PIECE_COLS), lambda b, r: (b, nc - 1 - r, 0))]
        + [acc_spec(s) for s in acc_shapes],
        out_shape=[jax.ShapeDtypeStruct(dproj3.shape, BF16)] + [jax.ShapeDtypeStruct(s, F32) for s in acc_shapes],
        scratch_shapes=[
            pltpu.VMEM((3, HALO + t_rows, BRANCH_W), F32),
            pltpu.VMEM((3, t_rows + HALO, BRANCH_W), F32),
            pltpu.VMEM((CHUNK, BRANCH_W), F32),
        ],
        input_output_aliases={3 + 11: 0},
        compiler_params=_params("arbitrary", "arbitrary"),
    )(proj3, proj3, dz3, *_mixer_weight_args(mw), dproj3)
    dproj3 = outs[0]
    names = ["pool_w", "pool_scale", "conv_w", "conv_b", "conv_ln_g", "conv_ln_b", "sgu_ln_g", "sgu_ln_b",
             "sgu_w", "sgu_b", "sc_w"]
    return dproj3, dict(zip(names, outs[1:]))


def _merge_fwd(z2, proj2, x2, wbr_full, wo_full, layer):
    n = x2.shape[0]
    tm = _tile(n, 512)

    def body(z_ref, gate_ref, x_ref, wbr_ref, wo_ref, xn_ref, mg_ref, acc_ref):
        nbr = pl.program_id(1)
        zt = z_ref[...]
        bo = jnp.concatenate([_dot(zt, wbr_ref[s]) for s in range(N_CHIPS)], axis=1)
        contrib = _sigmoid(gate_ref[...].astype(F32)) * bo

        @pl.when(nbr == 0)
        def _():
            acc_ref[...] = contrib

        @pl.when(nbr > 0)
        def _():
            acc_ref[...] += contrib

        @pl.when(nbr == N_BRANCH - 1)
        def _():
            mg = acc_ref[...].astype(BF16)
            mg_ref[...] = mg
            xn_ref[...] = x_ref[...] + _dot(mg, wo_ref[...])

    return pl.pallas_call(
        body,
        name=f"merge_fwd_l{layer}",
        grid=(n // tm, N_BRANCH),
        in_specs=[
            pl.BlockSpec((tm, BRANCH_W), lambda i, b: (i, b)),
            pl.BlockSpec((tm, D_MODEL), lambda i, b: (i, GATE_BLOCK0 + b)),
            pl.BlockSpec((tm, D_MODEL), lambda i, b: (i, 0)),
            pl.BlockSpec((None, N_CHIPS, None, BRANCH_W, BR_SHARD), lambda i, b: (layer, 0, b, 0, 0)),
            pl.BlockSpec((None, D_MODEL, D_MODEL), lambda i, b: (layer, 0, 0)),
        ],
        out_specs=[
            pl.BlockSpec((tm, D_MODEL), lambda i, b: (i, 0)),
            pl.BlockSpec((tm, D_MODEL), lambda i, b: (i, 0)),
        ],
        out_shape=[jax.ShapeDtypeStruct((n, D_MODEL), F32), jax.ShapeDtypeStruct((n, D_MODEL), BF16)],
        scratch_shapes=[pltpu.VMEM((tm, D_MODEL), F32)],
        compiler_params=_params("arbitrary", "arbitrary"),
    )(z2, proj2, x2, wbr_full, wo_full)


def _merge_bwd(dout2, merged2, z2, proj2, wbr_full, wo_full, gwo_prev, gwbr_prev, layer):
    n = dout2.shape[0]
    tm = _tile(n, 512)
    nt = n // tm
    first_layer_call = gwo_prev is None

    def body(*refs):
        if first_layer_call:
            (do_ref, mg_ref, z_ref, gate_ref, wbr_ref, wo_ref,
             dz_ref, dg_ref, gwo_ref, gwbr_ref, dm_ref, awo_ref, awbr_ref) = refs
        else:
            (do_ref, mg_ref, z_ref, gate_ref, wbr_ref, wo_ref, _, _,
             dz_ref, dg_ref, gwo_ref, gwbr_ref, dm_ref, awo_ref, awbr_ref) = refs
        i = pl.program_id(0)
        nbr = pl.program_id(1)

        @pl.when(nbr == 0)
        def _():
            do_b = do_ref[...].astype(BF16)
            dm_ref[...] = _dot_nt(do_b, wo_ref[...])
            gw = _dot_tn(mg_ref[...], do_b)

            @pl.when(i == 0)
            def _():
                awo_ref[...] = gw

            @pl.when(i > 0)
            def _():
                awo_ref[...] += gw

        zt = z_ref[...]
        bo = jnp.concatenate([_dot(zt, wbr_ref[s]) for s in range(N_CHIPS)], axis=1)
        gt = _sigmoid(gate_ref[...].astype(F32))
        dm = dm_ref[...]
        dbo = (dm * gt).astype(BF16)
        dg_ref[...] = (dm * bo * gt * (1.0 - gt)).astype(BF16)
        dzt = jnp.zeros((tm, BRANCH_W), F32)
        for s in range(N_CHIPS):
            dbo_s = dbo[:, s * BR_SHARD:(s + 1) * BR_SHARD]
            dzt = dzt + _dot_nt(dbo_s, wbr_ref[s])
            gw = _dot_tn(zt, dbo_s)

            @pl.when(i == 0)
            def _():
                awbr_ref[s, nbr] = gw

            @pl.when(i > 0)
            def _():
                awbr_ref[s, nbr] += gw

        dz_ref[...] = dzt.astype(BF16)

        @pl.when(jnp.logical_and(i == nt - 1, nbr == N_BRANCH - 1))
        def _():
            pltpu.sync_copy(awo_ref, gwo_ref.at[layer])
            pltpu.sync_copy(awbr_ref, gwbr_ref.at[layer])

    in_specs = [
        pl.BlockSpec((tm, D_MODEL), lambda i, b: (i, 0)),
        pl.BlockSpec((tm, D_MODEL), lambda i, b: (i, 0)),
        pl.BlockSpec((tm, BRANCH_W), lambda i, b: (i, b)),
        pl.BlockSpec((tm, D_MODEL), lambda i, b: (i, GATE_BLOCK0 + b)),
        pl.BlockSpec((None, N_CHIPS, None, BRANCH_W, BR_SHARD), lambda i, b: (layer, 0, b, 0, 0)),
        pl.BlockSpec((None, D_MODEL, D_MODEL), lambda i, b: (layer, 0, 0)),
    ]
    args = [dout2, merged2, z2, proj2, wbr_full, wo_full]
    aliases = {}
    if not first_layer_call:
        in_specs += [ANY, ANY]
        args += [gwo_prev, gwbr_prev]
        aliases = {6: 2, 7: 3}
    return pl.pallas_call(
        body,
        name=f"merge_bwd_l{layer}",
        grid=(nt, N_BRANCH),
        in_specs=in_specs,
        out_specs=[
            pl.BlockSpec((tm, BRANCH_W), lambda i, b: (i, b)),
            pl.BlockSpec((tm, D_MODEL), lambda i, b: (i, GATE_BLOCK0 + b)),
            ANY,
            ANY,
        ],
        out_shape=[
            jax.ShapeDtypeStruct((n, N_BRANCH * BRANCH_W), BF16),
            jax.ShapeDtypeStruct((n, IN_COLS), BF16),
            jax.ShapeDtypeStruct((DEPTH, D_MODEL, D_MODEL), F32),
            jax.ShapeDtypeStruct((DEPTH, N_CHIPS, N_BRANCH, BRANCH_W, BR_SHARD), F32),
        ],
        scratch_shapes=[
            pltpu.VMEM((tm, D_MODEL), F32),
            pltpu.VMEM((D_MODEL, D_MODEL), F32),
            pltpu.VMEM((N_CHIPS, N_BRANCH, BRANCH_W, BR_SHARD), F32),
        ],
        input_output_aliases=aliases,
        compiler_params=_params("arbitrary", "arbitrary"),
    )(*args)


def _loss_head(x2, g_row, tgt2):
    n = x2.shape[0]
    tm = _tile(n, 512)

    def body(x_ref, g_ref, t_ref, dx_ref, loss_ref, dg_ref):
        @pl.when(pl.program_id(0) == 0)
        def _():
            loss_ref[...] = jnp.zeros_like(loss_ref)
            dg_ref[...] = jnp.zeros_like(dg_ref)

        xv = x_ref[...]
        g = g_ref[...]
        r = lax.rsqrt(_lanemean(xv * xv) + RMS_EPS)
        xh = xv * r
        err = xh * g - t_ref[...]
        loss_ref[...] += 0.5 * jnp.sum(_lanemean(err * err), axis=0, keepdims=True)
        dy = err * (1.0 / D_MODEL)
        dg_ref[...] += _rowsum(dy * xh)
        dxh = dy * g
        dx_ref[...] = r * (dxh - xh * _lanemean(dxh * xh))

    return pl.pallas_call(
        body,
        name="loss_head",
        grid=(n // tm,),
        in_specs=[
            pl.BlockSpec((tm, D_MODEL), lambda i: (i, 0)),
            pl.BlockSpec((1, D_MODEL), lambda i: (0, 0)),
            pl.BlockSpec((tm, D_MODEL), lambda i: (i, 0)),
        ],
        out_specs=[
            pl.BlockSpec((tm, D_MODEL), lambda i: (i, 0)),
            pl.BlockSpec((1, GROUP_W), lambda i: (0, 0)),
            pl.BlockSpec((1, D_MODEL), lambda i: (0, 0)),
        ],
        out_shape=[jax.ShapeDtypeStruct((n, D_MODEL), F32), jax.ShapeDtypeStruct((1, GROUP_W), F32),
                   jax.ShapeDtypeStruct((1, D_MODEL), F32)],
        compiler_params=_params("arbitrary"),
    )(x2, g_row, tgt2)


def _inproj_bwd_x(dproj2, w_full, x2, g_row, dout2, layer):
    n = x2.shape[0]
    tm = _tile(n, 512)

    def body(dp_ref, w_ref, x_ref, g_ref, do_ref, dx_ref, dng_ref, dh_ref):
        i = pl.program_id(0)
        s = pl.program_id(1)
        part = _dot_nt(dp_ref[...], w_ref[...])

        @pl.when(s == 0)
        def _():
            dh_ref[...] = part

        @pl.when(s > 0)
        def _():
            dh_ref[...] += part

        @pl.when(jnp.logical_and(i == 0, s == 0))
        def _():
            dng_ref[...] = jnp.zeros_like(dng_ref)

        @pl.when(s == N_CHIPS - 1)
        def _():
            xv = x_ref[...]
            r = lax.rsqrt(_lanemean(xv * xv) + RMS_EPS)
            xh = xv * r
            dh = dh_ref[...]
            dng_ref[...] += _rowsum(dh * xh)
            dxh = dh * g_ref[...]
            dx_ref[...] = do_ref[...] + r * (dxh - xh * _lanemean(dxh * xh))

    return pl.pallas_call(
        body,
        name=f"inproj_bwd_x_l{layer}",
        grid=(n // tm, N_CHIPS),
        in_specs=[
            pl.BlockSpec((tm, SHARD_COLS), lambda i, s: (i, s)),
            pl.BlockSpec((None, None, D_MODEL, SHARD_COLS), lambda i, s: (layer, s, 0, 0)),
            pl.BlockSpec((tm, D_MODEL), lambda i, s: (i, 0)),
            pl.BlockSpec((1, D_MODEL), lambda i, s: (0, 0)),
            pl.BlockSpec((tm, D_MODEL), lambda i, s: (i, 0)),
        ],
        out_specs=[
            pl.BlockSpec((tm, D_MODEL), lambda i, s: (i, 0)),
            pl.BlockSpec((1, D_MODEL), lambda i, s: (0, 0)),
        ],
        out_shape=[jax.ShapeDtypeStruct((n, D_MODEL), F32), jax.ShapeDtypeStruct((1, D_MODEL), F32)],
        scratch_shapes=[pltpu.VMEM((tm, D_MODEL), F32)],
        compiler_params=_params("arbitrary", "arbitrary"),
    )(dproj2, w_full, x2, g_row, dout2)


def _inproj_bwd_w(h2, dproj2, gwin_prev, layer):
    n = h2.shape[0]
    tm = _tile(n, 512)

    def body(*refs):
        h_ref, dp_ref = refs[0], refs[1]
        gw_ref = refs[-1]
        gw = _dot_tn(h_ref[...], dp_ref[...])

        @pl.when(pl.program_id(1) == 0)
        def _():
            gw_ref[...] = gw

        @pl.when(pl.program_id(1) > 0)
        def _():
            gw_ref[...] += gw

    in_specs = [
        pl.BlockSpec((tm, D_MODEL), lambda s, i: (i, 0)),
        pl.BlockSpec((tm, SHARD_COLS), lambda s, i: (i, s)),
    ]
    args = [h2, dproj2]
    aliases = {}
    if gwin_prev is not None:
        in_specs.append(ANY)
        args.append(gwin_prev)
        aliases = {2: 0}
    return pl.pallas_call(
        body,
        name=f"inproj_bwd_w_l{layer}",
        grid=(N_CHIPS, n // tm),
        in_specs=in_specs,
        out_specs=pl.BlockSpec((None, None, D_MODEL, SHARD_COLS), lambda s, i: (layer, s, 0, 0)),
        out_shape=jax.ShapeDtypeStruct((DEPTH, N_CHIPS, D_MODEL, SHARD_COLS), F32),
        input_output_aliases=aliases,
        compiler_params=_params("arbitrary", "arbitrary"),
    )(*args)


def _place():
    x, y, c = lax.axis_index("x"), lax.axis_index("y"), lax.axis_index("c")
    chip = 2 * x + y
    peers = [(1 - x, y), (x, 1 - y), (1 - x, 1 - y)]
    return x, y, c, chip, peers


def _remote(src, dst, send_sem, recv_sem, dev):
    return pltpu.make_async_remote_copy(src_ref=src, dst_ref=dst, send_sem=send_sem, recv_sem=recv_sem,
                                        device_id=dev, device_id_type=MESH_ID)


def _allgather_weights(shards):
    na = len(shards)

    def body(*refs):
        ins, outs = refs[:na], refs[na:2 * na]
        send_sems, recv_sems, fsend_sems, frecv_sems, loc_sems = refs[2 * na:]
        x, y, c, chip, peers = _place()
        sib = (x, y, 1 - c)
        locs = [pltpu.make_async_copy(ins[a], outs[a].at[:, chip], loc_sems.at[a]) for a in range(na)]
        for cp in locs:
            cp.start()
        pending = []
        for k, (px, py) in enumerate(peers):
            for a in range(na):
                cp = _remote(ins[a].at[c], outs[a].at[c, chip], send_sems.at[k * na + a], recv_sems.at[k * na + a],
                             (px, py, c))
                cp.start()
                pending.append(cp)
        for k, (px, py) in enumerate(peers):
            pchip = 2 * px + py
            for a in range(na):
                slab = outs[a].at[c, pchip]
                _remote(slab, slab, send_sems.at[k * na + a], recv_sems.at[k * na + a], (px, py, c)).wait_recv()
                cp = _remote(slab, slab, fsend_sems.at[k * na + a], frecv_sems.at[k * na + a], sib)
                cp.start()
                pending.append(cp)
        for k, (px, py) in enumerate(peers):
            pchip = 2 * px + py
            for a in range(na):
                slab = outs[a].at[1 - c, pchip]
                _remote(slab, slab, fsend_sems.at[k * na + a], frecv_sems.at[k * na + a], sib).wait_recv()
        for cp in pending:
            cp.wait_send()
        for cp in locs:
            cp.wait()

    return pl.pallas_call(
        body,
        name="allgather_weights",
        in_specs=[ANY] * na,
        out_specs=[ANY] * na,
        out_shape=[jax.ShapeDtypeStruct((DEPTH, N_CHIPS) + a.shape[1:], a.dtype) for a in shards],
        scratch_shapes=[pltpu.SemaphoreType.DMA((3 * na,)), pltpu.SemaphoreType.DMA((3 * na,)),
                        pltpu.SemaphoreType.DMA((3 * na,)), pltpu.SemaphoreType.DMA((3 * na,)),
                        pltpu.SemaphoreType.DMA((na,))],
    )(*shards)


def _swap_layers(grads):
    na = len(grads)

    def body(*refs):
        ins, outs = refs[:na], refs[na:2 * na]
        send_sems, recv_sems = refs[2 * na:]
        x, y, c, _, _ = _place()
        cps = [_remote(ins[a].at[1 - c], outs[a], send_sems.at[a], recv_sems.at[a], (x, y, 1 - c)) for a in range(na)]
        for cp in cps:
            cp.start()
        for cp in cps:
            cp.wait()

    return pl.pallas_call(
        body,
        name="swap_layers",
        in_specs=[ANY] * na,
        out_specs=[ANY] * na,
        out_shape=[jax.ShapeDtypeStruct(a.shape[1:], a.dtype) for a in grads],
        scratch_shapes=[pltpu.SemaphoreType.DMA((na,)), pltpu.SemaphoreType.DMA((na,))],
    )(*grads)


def _add_own_layer(grad, other, core):
    _, rows, cols = grad.shape
    tr = _tile(rows, max(8, (1 << 20) // (4 * cols) // 8 * 8))

    def body(core_ref, g_ref, o_ref, out_ref):
        del core_ref
        out_ref[...] = g_ref[...] + o_ref[...]

    return pl.pallas_call(
        body,
        name=f"add_own_layer_{rows}x{cols}",
        grid_spec=pltpu.PrefetchScalarGridSpec(
            num_scalar_prefetch=1,
            grid=(rows // tr,),
            in_specs=[
                pl.BlockSpec((None, tr, cols), lambda i, core_ref: (core_ref[0], i, 0)),
                pl.BlockSpec((tr, cols), lambda i, core_ref: (i, 0)),
            ],
            out_specs=pl.BlockSpec((tr, cols), lambda i, core_ref: (i, 0)),
        ),
        out_shape=jax.ShapeDtypeStruct((rows, cols), F32),
        compiler_params=_params("arbitrary"),
    )(core, grad, other)


def _scatter_chip_sums(sums):
    na = len(sums)

    def body(*refs):
        ins, outs = refs[:na], refs[na:2 * na]
        send_sems, recv_sems, fsend_sems, frecv_sems, loc_sems = refs[2 * na:]
        x, y, c, chip, peers = _place()
        sib = (x, y, 1 - c)
        locs = [pltpu.make_async_copy(ins[a].at[chip], outs[a].at[c, chip], loc_sems.at[a]) for a in range(na)]
        for cp in locs:
            cp.start()
        pending = []
        for a in range(na):
            cp = _remote(ins[a].at[chip], outs[a].at[c, chip], fsend_sems.at[3 * na + a], frecv_sems.at[3 * na + a], sib)
            cp.start()
            pending.append(cp)
        for k, (px, py) in enumerate(peers):
            pchip = 2 * px + py
            for a in range(na):
                cp = _remote(ins[a].at[pchip], outs[a].at[c, chip], send_sems.at[k * na + a],
                             recv_sems.at[k * na + a], (px, py, c))
                cp.start()
                pending.append(cp)
        for k, (px, py) in enumerate(peers):
            pchip = 2 * px + py
            for a in range(na):
                slab = outs[a].at[c, pchip]
                _remote(slab, slab, send_sems.at[k * na + a], recv_sems.at[k * na + a], (px, py, c)).wait_recv()
                cp = _remote(slab, slab, fsend_sems.at[k * na + a], frecv_sems.at[k * na + a], sib)
                cp.start()
                pending.append(cp)
        for a in range(na):
            slab = outs[a].at[1 - c, chip]
            _remote(slab, slab, fsend_sems.at[3 * na + a], frecv_sems.at[3 * na + a], sib).wait_recv()
        for k, (px, py) in enumerate(peers):
            pchip = 2 * px + py
            for a in range(na):
                slab = outs[a].at[1 - c, pchip]
                _remote(slab, slab, fsend_sems.at[k * na + a], frecv_sems.at[k * na + a], sib).wait_recv()
        for cp in pending:
            cp.wait_send()
        for cp in locs:
            cp.wait()

    return pl.pallas_call(
        body,
        name="scatter_chip_sums",
        in_specs=[ANY] * na,
        out_specs=[ANY] * na,
        out_shape=[jax.ShapeDtypeStruct((DEPTH,) + a.shape, a.dtype) for a in sums],
        scratch_shapes=[pltpu.SemaphoreType.DMA((3 * na,)), pltpu.SemaphoreType.DMA((3 * na,)),
                        pltpu.SemaphoreType.DMA((4 * na,)), pltpu.SemaphoreType.DMA((4 * na,)),
                        pltpu.SemaphoreType.DMA((na,))],
    )(*sums)


def _allreduce_small(packed):
    rows = packed.shape[0]

    def body(sg_ref, res_ref, sib_ref, cs_ref, rem_ref, s1_send, s1_recv, s2_send, s2_recv):
        x, y, c, chip, peers = _place()
        cp = _remote(sg_ref, sib_ref, s1_send, s1_recv, (x, y, 1 - c))
        cp.start()
        cp.wait()
        cs_ref[...] = sg_ref[...] + sib_ref[...]
        rem_ref[chip] = cs_ref[...]
        cps = [_remote(cs_ref, rem_ref.at[chip], s2_send.at[k], s2_recv.at[chip], (px, py, c))
               for k, (px, py) in enumerate(peers)]
        for cp in cps:
            cp.start()
        for k, (px, py) in enumerate(peers):
            pchip = 2 * px + py
            _remote(cs_ref, rem_ref.at[pchip], s2_send.at[k], s2_recv.at[pchip], (px, py, c)).wait_recv()
        for cp in cps:
            cp.wait_send()
        res_ref[...] = ((rem_ref[0] + rem_ref[1]) + rem_ref[2]) + rem_ref[3]

    return pl.pallas_call(
        body,
        name="allreduce_small",
        in_specs=[VMEM_WHOLE],
        out_specs=VMEM_WHOLE,
        out_shape=jax.ShapeDtypeStruct((rows, GROUP_W), F32),
        scratch_shapes=[
            pltpu.VMEM((rows, GROUP_W), F32),
            pltpu.VMEM((rows, GROUP_W), F32),
            pltpu.VMEM((N_CHIPS, rows, GROUP_W), F32),
            pltpu.SemaphoreType.DMA,
            pltpu.SemaphoreType.DMA,
            pltpu.SemaphoreType.DMA((3,)),
            pltpu.SemaphoreType.DMA((N_CHIPS,)),
        ],
        compiler_params=pltpu.CompilerParams(vmem_limit_bytes=VMEM_LIMIT),
    )(packed)


def _adamw_math(w, g, m, v):
    m = ADAM_B1 * m + (1.0 - ADAM_B1) * g
    v = ADAM_B2 * v + (1.0 - ADAM_B2) * jnp.square(g)
    m_hat = m / (1.0 - ADAM_B1 ** ADAM_STEP)
    v_hat = v / (1.0 - ADAM_B2 ** ADAM_STEP)
    delta = -ADAM_LR * (m_hat / (jnp.sqrt(v_hat) + ADAM_EPS) + ADAM_WD * w)
    return delta, m, v


def _adamw_sharded(w, m, v, parts, name):
    depth, rows, cols = w.shape
    tr = _tile(rows, max(8, (1 << 19) // (4 * cols) // 8 * 8))

    def body(w_ref, m_ref, v_ref, p_ref, g_out, d_out, m_out, v_out):
        g = ((p_ref[0] + p_ref[1]) + p_ref[2]) + p_ref[3]
        delta, m2, v2 = _adamw_math(w_ref[...], g, m_ref[...], v_ref[...])
        g_out[...] = g
        d_out[...] = delta
        m_out[...] = m2
        v_out[...] = v2

    spec = pl.BlockSpec((None, tr, cols), lambda l, i: (l, i, 0))
    return pl.pallas_call(
        body,
        name=f"adamw_{name}",
        grid=(depth, rows // tr),
        in_specs=[spec, spec, spec, pl.BlockSpec((None, N_CHIPS, tr, cols), lambda l, i: (l, 0, i, 0))],
        out_specs=[spec] * 4,
        out_shape=[jax.ShapeDtypeStruct(w.shape, F32)] * 4,
        compiler_params=_params("arbitrary", "arbitrary"),
    )(w, m, v, parts)


def _adamw_packed(w, m, v, g):
    rows = w.shape[0]
    tr = _tile(rows, rows // 2 if rows % 16 == 0 else rows)

    def body(w_ref, m_ref, v_ref, g_ref, d_out, m_out, v_out):
        delta, m2, v2 = _adamw_math(w_ref[...], g_ref[...], m_ref[...], v_ref[...])
        d_out[...] = delta
        m_out[...] = m2
        v_out[...] = v2

    spec = pl.BlockSpec((tr, GROUP_W), lambda i: (i, 0))
    return pl.pallas_call(
        body,
        name="adamw_small",
        grid=(rows // tr,),
        in_specs=[spec] * 4,
        out_specs=[spec] * 3,
        out_shape=[jax.ShapeDtypeStruct(w.shape, F32)] * 3,
        compiler_params=_params("arbitrary"),
    )(w, m, v, g)


SMALL = ["norm_g", "pool_w", "pool_scale", "conv_b", "conv_ln_g", "conv_ln_b", "sgu_ln_g", "sgu_ln_b", "sgu_w",
         "sgu_b", "final_g"]
SHARDED = ["w_in", "w_branch", "w_o", "conv_w", "sc_w"]
WEIGHTS = ["norm_g", "w_in", "pool_w", "pool_scale", "conv_w", "conv_b", "conv_ln_g", "conv_ln_b", "sgu_ln_g",
           "sgu_ln_b", "sgu_w", "sgu_b", "sc_w", "w_branch", "w_o", "final_g"]


def _pack_small(tree):
    return jnp.concatenate([tree[k].reshape(-1, GROUP_W) for k in SMALL], axis=0)


def _unpack_small(packed, like):
    out, r = {}, 0
    for k in SMALL:
        nr = like[k].size // GROUP_W
        out[k] = packed[r:r + nr].reshape(like[k].shape)
        r += nr
    return out


def _pad_rows(a, rows):
    return jnp.pad(a, ((0, 0), (0, rows - a.shape[1]), (0, 0)))


def _shard_major_rows(a):
    return a.reshape(a.shape[0], N_CHIPS, GROUP_W).transpose(1, 0, 2)


def kernel(x, norm_g, w_in, pool_w, pool_scale, conv_w, conv_b, conv_ln_g, conv_ln_b, sgu_ln_g, sgu_ln_b, sgu_w, sgu_b, sc_w, w_branch, w_o, final_g, loss_target, m_norm_g, m_w_in, m_pool_w, m_pool_scale, m_conv_w, m_conv_b, m_conv_ln_g, m_conv_ln_b, m_sgu_ln_g, m_sgu_ln_b, m_sgu_w, m_sgu_b, m_sc_w, m_w_branch, m_w_o, m_final_g, v_norm_g, v_w_in, v_pool_w, v_pool_scale, v_conv_w, v_conv_b, v_conv_ln_g, v_conv_ln_b, v_sgu_ln_g, v_sgu_ln_b, v_sgu_w, v_sgu_b, v_sc_w, v_w_branch, v_w_o, v_final_g):
    w = dict(norm_g=norm_g, w_in=w_in, pool_w=pool_w, pool_scale=pool_scale, conv_w=conv_w, conv_b=conv_b,
             conv_ln_g=conv_ln_g, conv_ln_b=conv_ln_b, sgu_ln_g=sgu_ln_g, sgu_ln_b=sgu_ln_b, sgu_w=sgu_w,
             sgu_b=sgu_b, sc_w=sc_w, w_branch=w_branch, w_o=w_o, final_g=final_g)
    mom = dict(norm_g=m_norm_g, w_in=m_w_in, pool_w=m_pool_w, pool_scale=m_pool_scale, conv_w=m_conv_w,
               conv_b=m_conv_b, conv_ln_g=m_conv_ln_g, conv_ln_b=m_conv_ln_b, sgu_ln_g=m_sgu_ln_g,
               sgu_ln_b=m_sgu_ln_b, sgu_w=m_sgu_w, sgu_b=m_sgu_b, sc_w=m_sc_w, w_branch=m_w_branch, w_o=m_w_o,
               final_g=m_final_g)
    var = dict(norm_g=v_norm_g, w_in=v_w_in, pool_w=v_pool_w, pool_scale=v_pool_scale, conv_w=v_conv_w,
               conv_b=v_conv_b, conv_ln_g=v_conv_ln_g, conv_ln_b=v_conv_ln_b, sgu_ln_g=v_sgu_ln_g,
               sgu_ln_b=v_sgu_ln_b, sgu_w=v_sgu_w, sgu_b=v_sgu_b, sc_w=v_sc_w, w_branch=v_w_branch, w_o=v_w_o,
               final_g=v_final_g)

    nb, seq, _ = x.shape
    n = nb * seq
    core = lax.axis_index("c").astype(jnp.int32).reshape(1)

    win_f, wbr_f, wo_f, cw_f, scw_f = _allgather_weights([
        w_in.astype(BF16), w_branch.astype(BF16), w_o.astype(BF16),
        _pad_rows(conv_w, CONV_ROWS), _pad_rows(sc_w, SHORT_ROWS)])

    def mixer_weights():
        bias = jnp.repeat(jnp.swapaxes(sgu_b, 1, 2), GROUP_W, axis=2)
        row = lambda a: a.reshape(DEPTH, 1, BRANCH_W)
        return dict(pool_w=pool_w.astype(BF16), pool_scale=row(pool_scale), conv_w=cw_f, conv_b=row(conv_b),
                    conv_ln_g=row(conv_ln_g), conv_ln_b=row(conv_ln_b), sgu_ln_g=row(sgu_ln_g),
                    sgu_ln_b=row(sgu_ln_b), sgu_w=sgu_w, sgu_bias=bias, sc_w=scw_f)

    mw = mixer_weights()
    wo_f = wo_f.reshape(DEPTH, D_MODEL, D_MODEL)

    xs, projs, hs, zs, mgs = [x.reshape(n, D_MODEL)], [], [], [], []
    for l in range(DEPTH):
        proj, h = _inproj_fwd(xs[l], norm_g[l:l + 1], win_f, l)
        z = _mixers_fwd(proj.reshape(nb, seq, IN_COLS), mw, l).reshape(n, N_BRANCH * BRANCH_W)
        x_new, merged = _merge_fwd(z, proj, xs[l], wbr_f, wo_f, l)
        projs.append(proj)
        hs.append(h)
        zs.append(z)
        mgs.append(merged)
        xs.append(x_new)

    dx, loss_part, g_final = _loss_head(xs[DEPTH], final_g.reshape(1, D_MODEL), loss_target.reshape(n, D_MODEL))
    loss = lax.psum(loss_part[0, 0], ("x", "y", "c"))

    gwin = gwo = gwbr = None
    g_small = {"final_g": g_final.reshape(D_MODEL)}
    per_layer = {}
    for l in reversed(range(DEPTH)):
        dz, dproj, gwo, gwbr = _merge_bwd(dx, mgs[l], zs[l], projs[l], wbr_f, wo_f, gwo, gwbr, l)
        dproj3, gm = _mixers_bwd(projs[l].reshape(nb, seq, IN_COLS), dz.reshape(nb, seq, N_BRANCH * BRANCH_W),
                                 dproj.reshape(nb, seq, IN_COLS), mw, l)
        dproj = dproj3.reshape(n, IN_COLS)
        dx, g_norm = _inproj_bwd_x(dproj, win_f, xs[l], norm_g[l:l + 1], dx, l)
        gwin = _inproj_bwd_w(hs[l], dproj, gwin, l)
        gm["norm_g"] = g_norm
        per_layer[l] = gm
    grad_x = dx.reshape(nb, seq, D_MODEL)

    def stack(k, shape):
        return jnp.stack([per_layer[l][k] for l in range(DEPTH)]).reshape(shape)

    g_small["norm_g"] = stack("norm_g", (DEPTH, D_MODEL))
    g_small["pool_w"] = stack("pool_w", pool_w.shape)
    g_small["sgu_w"] = stack("sgu_w", sgu_w.shape)
    g_small["sgu_b"] = jnp.swapaxes(stack("sgu_b", (DEPTH, CHUNK, GROUP_W))[:, :, :N_GROUPS], 1, 2)
    for k in ("pool_scale", "conv_b", "conv_ln_g", "conv_ln_b", "sgu_ln_g", "sgu_ln_b"):
        g_small[k] = stack(k, (DEPTH, BRANCH_W))
    gcw = jnp.stack([_shard_major_rows(per_layer[l]["conv_w"]) for l in range(DEPTH)])
    gscw = jnp.stack([_shard_major_rows(per_layer[l]["sc_w"]) for l in range(DEPTH)])

    partial = [gwin, gwbr, gwo.reshape(DEPTH, N_CHIPS, BR_SHARD, D_MODEL), gcw, gscw]
    others = _swap_layers(partial)
    sums = []
    for g, o in zip(partial, others):
        rows = g.size // (DEPTH * g.shape[-1])
        s = _add_own_layer(g.reshape(DEPTH, rows, g.shape[-1]), o.reshape(rows, g.shape[-1]), core)
        sums.append(s.reshape(o.shape))
    parts = _scatter_chip_sums(sums)

    grads, deltas, new_m, new_v = {}, {}, {}, {}

    def sharded_update(name, part, flat_shape, padded=None):
        def prep(a):
            if padded is not None:
                a = _pad_rows(a, padded)
            return a.reshape((DEPTH,) + flat_shape)

        res = _adamw_sharded(prep(w[name]), prep(mom[name]), prep(var[name]),
                             part.reshape((DEPTH, N_CHIPS) + flat_shape), name)
        shape = w[name].shape
        for tree, r in zip((grads, deltas, new_m, new_v), res):
            if padded is not None:
                r = r[:, :shape[1]]
            tree[name] = r.reshape(shape)

    sharded_update("w_in", parts[0], (D_MODEL, SHARD_COLS))
    sharded_update("w_branch", parts[1], (N_BRANCH * BRANCH_W, BR_SHARD))
    sharded_update("w_o", parts[2], (BR_SHARD, D_MODEL))
    sharded_update("conv_w", parts[3], (CONV_ROWS, GROUP_W), padded=CONV_ROWS)
    sharded_update("sc_w", parts[4], (SHORT_ROWS, GROUP_W), padded=SHORT_ROWS)

    g_packed = _allreduce_small(_pack_small(g_small))
    d_packed, m_packed, v_packed = _adamw_packed(_pack_small(w), _pack_small(mom), _pack_small(var), g_packed)
    for tree, packed in ((grads, g_packed), (deltas, d_packed), (new_m, m_packed), (new_v, v_packed)):
        tree.update(_unpack_small(packed, w))

    return (loss, grad_x, *[grads[k] for k in WEIGHTS], *[deltas[k] for k in WEIGHTS],
            *[new_m[k] for k in WEIGHTS], *[new_v[k] for k in WEIGHTS])
```

```python
import functools

import jax
import jax.numpy as jnp
from jax import lax
from jax.experimental import pallas as pl
from jax.experimental.pallas import tpu as pltpu

F32 = jnp.float32
BF16 = jnp.bfloat16

D_MODEL = 1024
DEPTH = 2
N_BRANCH = 4
BRANCH_W = 512
N_GROUPS = 4
GROUP_W = 128
POOL_WINDOWS = (2, 4, 8, 16)
CONV_K = 31
SHORT_K = 3
CHUNK = 128
N_PIECES = 12
PIECE_COLS = N_PIECES * BRANCH_W
IN_COLS = PIECE_COLS + N_BRANCH * D_MODEL
N_CHIPS = 4
SHARD_COLS = IN_COLS // N_CHIPS
BR_SHARD = D_MODEL // N_CHIPS
COL_TILE = SHARD_COLS // 2
N_COL_TILES = IN_COLS // COL_TILE
GATE_BLOCK0 = PIECE_COLS // D_MODEL
RMS_EPS = 1e-6
LN_EPS = 1e-5
HALO = 32
CONV_ROWS = 32
SHORT_ROWS = 8

ADAM_LR = 0.001
ADAM_B1 = 0.9
ADAM_B2 = 0.999
ADAM_EPS = 1e-08
ADAM_WD = 0.01
ADAM_STEP = 10

VMEM_LIMIT = 52 * 1024 * 1024
MESH_ID = pl.DeviceIdType.MESH
ANY = pl.BlockSpec(memory_space=pl.ANY)
VMEM_WHOLE = pl.BlockSpec(memory_space=pltpu.VMEM)

(P_X, P_GATE, C_A, C_B, C_GATE, G_U, G_V, G_GATE, S_B, S_C, S_X, S_GATE) = range(N_PIECES)


def _params(*sem):
    return pltpu.CompilerParams(dimension_semantics=sem, vmem_limit_bytes=VMEM_LIMIT)


def _sigmoid(v):
    return 0.5 * jnp.tanh(0.5 * v) + 0.5


def _silu(v):
    return v * _sigmoid(v)


def _dsilu(v):
    s = _sigmoid(v)
    return s * (1.0 + v * (1.0 - s))


def _dot(a, b):
    return jnp.dot(a, b, preferred_element_type=F32)


def _dot_nt(a, b):
    return lax.dot_general(a, b, (((1,), (1,)), ((), ())), preferred_element_type=F32)


def _dot_tn(a, b):
    return lax.dot_general(a, b, (((0,), (0,)), ((), ())), preferred_element_type=F32)


def _rowsum(v):
    return jnp.sum(v, axis=0, keepdims=True)


def _lanemean(v):
    return jnp.mean(v, axis=-1, keepdims=True)


def _tile(n, want, mult=8):
    t = max(1, min(n, want))
    while n % t or (t % mult and t != n):
        t -= 1
    return t


def _inproj_fwd(x2, g_row, w_full, layer):
    n = x2.shape[0]
    tm = _tile(n, 1024)

    def body(x_ref, g_ref, w_ref, p_ref, h_ref):
        @pl.when(pl.program_id(1) == 0)
        def _():
            xv = x_ref[...]
            r = lax.rsqrt(_lanemean(xv * xv) + RMS_EPS)
            h_ref[...] = (xv * r * g_ref[...]).astype(BF16)

        p_ref[...] = _dot(h_ref[...], w_ref[...]).astype(BF16)

    return pl.pallas_call(
        body,
        name=f"inproj_fwd_l{layer}",
        grid=(n // tm, N_COL_TILES),
        in_specs=[
            pl.BlockSpec((tm, D_MODEL), lambda i, j: (i, 0)),
            pl.BlockSpec((1, D_MODEL), lambda i, j: (0, 0)),
            pl.BlockSpec((None, None, D_MODEL, COL_TILE), lambda i, j: (layer, j // 2, 0, j % 2)),
        ],
        out_specs=[
            pl.BlockSpec((tm, COL_TILE), lambda i, j: (i, j)),
            pl.BlockSpec((tm, D_MODEL), lambda i, j: (i, 0)),
        ],
        out_shape=[jax.ShapeDtypeStruct((n, IN_COLS), BF16), jax.ShapeDtypeStruct((n, D_MODEL), BF16)],
        compiler_params=_params("arbitrary", "arbitrary"),
    )(x2, g_row, w_full)


def _layer_norm_parts(v, g, b):
    mu = _lanemean(v)
    d = v - mu
    rstd = lax.rsqrt(_lanemean(d * d) + LN_EPS)
    xh = d * rstd
    return xh, rstd, xh * g + b


def _window_sum(ref, slot, weight_row, n_taps, base, t_rows):
    total = None
    for b in range(8):
        group = None
        for a in range((base + n_taps - 1) // 8 + 1):
            k = 8 * a + b - base
            if 0 <= k < n_taps:
                term = weight_row(k) * ref[slot, 8 * a:8 * a + t_rows + 8, :]
                group = term if group is None else group + term
        if group is not None:
            part = group[b:b + t_rows, :]
            total = part if total is None else total + part
    return total


def _tap_grads(acc_ref, pad_ref, shift_ref, dy, ref, slot, n_taps, base, t_rows):
    pad_ref[0:8, :] = jnp.zeros((8, BRANCH_W), F32)
    pad_ref[8:8 + t_rows, :] = dy
    pad_ref[8 + t_rows:16 + t_rows, :] = jnp.zeros((8, BRANCH_W), F32)
    for b in range(8):
        taps = [(a, 8 * a + b - base) for a in range((base + n_taps - 1) // 8 + 1) if 0 <= 8 * a + b - base < n_taps]
        if not taps:
            continue
        shift_ref[...] = pad_ref[pl.ds(8 - b, t_rows + 8), :]
        for a, k in taps:
            prod = shift_ref[...] * ref[slot, 8 * a:8 * a + t_rows + 8, :]
            acc_ref[k] += prod.reshape((t_rows + 8) // 8, 8, BRANCH_W).sum(axis=0)


def _tril_mask():
    r = lax.broadcasted_iota(jnp.int32, (CHUNK, CHUNK), 0)
    c = lax.broadcasted_iota(jnp.int32, (CHUNK, CHUNK), 1)
    return r >= c


def _lanes(refs4):
    return jnp.concatenate([refs4[g] for g in range(N_CHIPS)], axis=1)


def _mixer_weight_specs(layer):
    def whole(shape):
        nd = len(shape)
        return pl.BlockSpec((None,) + shape, lambda b, c: (layer,) + (0,) * nd)

    return [
        whole((N_GROUPS, GROUP_W, GROUP_W)),
        whole((1, BRANCH_W)),
        whole((N_CHIPS, CONV_ROWS, GROUP_W)),
        whole((1, BRANCH_W)),
        whole((1, BRANCH_W)),
        whole((1, BRANCH_W)),
        whole((1, BRANCH_W)),
        whole((1, BRANCH_W)),
        whole((N_GROUPS, CHUNK, CHUNK)),
        whole((CHUNK, BRANCH_W)),
        whole((N_CHIPS, SHORT_ROWS, GROUP_W)),
    ]


def _mixer_weight_args(mw):
    return [mw["pool_w"], mw["pool_scale"], mw["conv_w"], mw["conv_b"], mw["conv_ln_g"], mw["conv_ln_b"],
            mw["sgu_ln_g"], mw["sgu_ln_b"], mw["sgu_w"], mw["sgu_bias"], mw["sc_w"]]


def _mixers_fwd(proj3, mw, layer):
    nb, seq, _ = proj3.shape
    t_rows = _tile(seq, 256)
    nc = seq // t_rows
    hb = t_rows // HALO

    def body(cur_ref, halo_ref, pw_ref, ps_ref, cw_ref, cb_ref, clg_ref, clb_ref, slg_ref, slb_ref, sw_ref,
             sbias_ref, scw_ref, z_ref, ext_ref):
        c = pl.program_id(1)
        has_prev = c > 0
        row = lax.broadcasted_iota(jnp.int32, (t_rows, 1), 0)
        tpos = (c * t_rows + row + 1).astype(F32)

        def cur(k):
            return cur_ref[:, k * BRANCH_W:(k + 1) * BRANCH_W].astype(F32)

        def hal(k):
            return halo_ref[:, k * BRANCH_W:(k + 1) * BRANCH_W].astype(F32)

        def put_ext(slot, halo_val, cur_val):
            ext_ref[slot, 0:HALO, :] = jnp.where(has_prev, halo_val, 0.0)
            ext_ref[slot, HALO:HALO + t_rows, :] = cur_val
            ext_ref[slot, HALO + t_rows:HALO + t_rows + 8, :] = jnp.zeros((8, BRANCH_W), F32)

        px = cur(P_X)
        put_ext(0, hal(P_X), px)
        mixed = []
        for j, win in enumerate(POOL_WINDOWS):
            cols = slice(j * GROUP_W, (j + 1) * GROUP_W)
            s = px[:, cols]
            for i in range(1, win):
                s = s + ext_ref[0, pl.ds(HALO - i, t_rows), cols]
            pooled = s / jnp.minimum(tpos, float(win)) - px[:, cols]
            mixed.append(_dot(pooled.astype(BF16), pw_ref[j]))
        z_pool = jnp.concatenate(mixed, axis=1) * ps_ref[...] * _silu(cur(P_GATE))
        z_ref[:, 0:BRANCH_W] = z_pool.astype(BF16)

        put_ext(1, hal(C_A) * _sigmoid(hal(C_B)), cur(C_A) * _sigmoid(cur(C_B)))
        cw = _lanes(cw_ref)
        acc = cb_ref[...] + _window_sum(ext_ref, 1, lambda k: cw[k:k + 1, :], CONV_K, HALO - (CONV_K - 1), t_rows)
        _, _, ln = _layer_norm_parts(acc, clg_ref[...], clb_ref[...])
        z_ref[:, BRANCH_W:2 * BRANCH_W] = (_silu(ln) * _silu(cur(C_GATE))).astype(BF16)

        _, _, v = _layer_norm_parts(cur(G_V), slg_ref[...], slb_ref[...])
        vb = v.astype(BF16)
        mask = _tril_mask()
        wt = [jnp.where(mask, sw_ref[g], 0.0).astype(BF16) for g in range(N_GROUPS)]
        sp_rows = []
        for sub in range(t_rows // CHUNK):
            rows = slice(sub * CHUNK, (sub + 1) * CHUNK)
            sp_rows.append(jnp.concatenate(
                [_dot(wt[g], vb[rows, g * GROUP_W:(g + 1) * GROUP_W]) for g in range(N_GROUPS)], axis=1)
                + sbias_ref[...])
        sp = jnp.concatenate(sp_rows, axis=0)
        z_ref[:, 2 * BRANCH_W:3 * BRANCH_W] = (cur(G_U) * sp * _silu(cur(G_GATE))).astype(BF16)

        put_ext(2, hal(S_C) * hal(S_X), cur(S_C) * cur(S_X))
        scw = _lanes(scw_ref)
        cv = _window_sum(ext_ref, 2, lambda k: scw[k:k + 1, :], SHORT_K, HALO - (SHORT_K - 1), t_rows)
        z_ref[:, 3 * BRANCH_W:4 * BRANCH_W] = (cur(S_B) * cv * _silu(cur(S_GATE))).astype(BF16)

    return pl.pallas_call(
        body,
        name=f"mixers_fwd_l{layer}",
        grid=(nb, nc),
        in_specs=[
            pl.BlockSpec((None, t_rows, PIECE_COLS), lambda b, c: (b, c, 0)),
            pl.BlockSpec((None, HALO, PIECE_COLS), lambda b, c: (b, jnp.maximum(c * hb - 1, 0), 0)),
        ] + _mixer_weight_specs(layer),
        out_specs=pl.BlockSpec((None, t_rows, N_BRANCH * BRANCH_W), lambda b, c: (b, c, 0)),
        out_shape=jax.ShapeDtypeStruct((nb, seq, N_BRANCH * BRANCH_W), BF16),
        scratch_shapes=[pltpu.VMEM((3, HALO + t_rows + 8, BRANCH_W), F32)],
        compiler_params=_params("arbitrary", "arbitrary"),
    )(proj3, proj3, *_mixer_weight_args(mw))


def _mixers_bwd(proj3, dz3, dproj3, mw, layer):
    nb, seq, _ = proj3.shape
    t_rows = _tile(seq, 256)
    nc = seq // t_rows
    hb = t_rows // HALO

    def body(cur_ref, halo_ref, dz_ref, pw_ref, ps_ref, cw_ref, cb_ref, clg_ref, clb_ref, slg_ref, slb_ref,
             sw_ref, sbias_ref, scw_ref, dp_in_ref, dp_ref, g_pw, g_ps, g_cw, g_cb, g_clg, g_clb, g_slg,
             g_slb, g_sw, g_sb, g_scw, ext_ref, nxt_ref, sb_acc, cw_acc, scw_acc, pad_ref, shift_ref):
        del dp_in_ref
        b = pl.program_id(0)
        r = pl.program_id(1)
        c = nc - 1 - r
        has_prev = c > 0
        row = lax.broadcasted_iota(jnp.int32, (t_rows, 1), 0)
        tpos = (c * t_rows + row + 1).astype(F32)

        @pl.when(jnp.logical_and(b == 0, r == 0))
        def _():
            for ref in (g_pw, g_ps, g_cb, g_clg, g_clb, g_slg, g_slb, g_sw, sb_acc, cw_acc, scw_acc):
                ref[...] = jnp.zeros_like(ref)

        @pl.when(r == 0)
        def _():
            nxt_ref[:, t_rows:t_rows + HALO, :] = jnp.zeros((3, HALO, BRANCH_W), F32)

        def cur(k):
            return cur_ref[:, k * BRANCH_W:(k + 1) * BRANCH_W].astype(F32)

        def hal(k):
            return halo_ref[:, k * BRANCH_W:(k + 1) * BRANCH_W].astype(F32)

        def dzp(k):
            return dz_ref[:, k * BRANCH_W:(k + 1) * BRANCH_W].astype(F32)

        def put_dp(k, val):
            dp_ref[:, k * BRANCH_W:(k + 1) * BRANCH_W] = val.astype(BF16)

        def put_ext(slot, halo_val, cur_val):
            ext_ref[slot, 0:HALO, :] = jnp.where(has_prev, halo_val, 0.0)
            ext_ref[slot, HALO:HALO + t_rows, :] = cur_val
            ext_ref[slot, HALO + t_rows:HALO + t_rows + 8, :] = jnp.zeros((8, BRANCH_W), F32)

        px = cur(P_X)
        put_ext(0, hal(P_X), px)
        pgate = cur(P_GATE)
        dz_pool = dzp(0)
        ps = ps_ref[...]
        pooled, mixed, cnts = [], [], []
        for j, win in enumerate(POOL_WINDOWS):
            cols = slice(j * GROUP_W, (j + 1) * GROUP_W)
            s = px[:, cols]
            for i in range(1, win):
                s = s + ext_ref[0, pl.ds(HALO - i, t_rows), cols]
            cnt = jnp.minimum(tpos, float(win))
            pj = s / cnt - px[:, cols]
            cnts.append(cnt)
            pooled.append(pj.astype(BF16))
            mixed.append(_dot(pooled[j], pw_ref[j]))
        mixed = jnp.concatenate(mixed, axis=1)
        put_dp(P_GATE, dz_pool * (mixed * ps) * _dsilu(pgate))
        d_out = dz_pool * _silu(pgate)
        g_ps[...] += _rowsum(d_out * mixed)
        d_mixed = (d_out * ps).astype(BF16)
        d_pooled = []
        for j in range(N_GROUPS):
            cols = slice(j * GROUP_W, (j + 1) * GROUP_W)
            g_pw[j] += _dot_tn(pooled[j], d_mixed[:, cols])
            dpj = _dot_nt(d_mixed[:, cols], pw_ref[j])
            d_pooled.append(dpj)
            nxt_ref[0, 0:t_rows, cols] = dpj / cnts[j]
        dpx = []
        for j, win in enumerate(POOL_WINDOWS):
            cols = slice(j * GROUP_W, (j + 1) * GROUP_W)
            s = nxt_ref[0, 0:t_rows, cols]
            for i in range(1, win):
                s = s + nxt_ref[0, pl.ds(i, t_rows), cols]
            dpx.append(s - d_pooled[j])
        put_dp(P_X, jnp.concatenate(dpx, axis=1))

        c_a = cur(C_A)
        sig_b = _sigmoid(cur(C_B))
        put_ext(1, hal(C_A) * _sigmoid(hal(C_B)), c_a * sig_b)
        cw = _lanes(cw_ref)
        acc = cb_ref[...] + _window_sum(ext_ref, 1, lambda k: cw[k:k + 1, :], CONV_K, HALO - (CONV_K - 1), t_rows)
        xh, rstd, ln = _layer_norm_parts(acc, clg_ref[...], clb_ref[...])
        cgate = cur(C_GATE)
        dz_conv = dzp(1)
        put_dp(C_GATE, dz_conv * _silu(ln) * _dsilu(cgate))
        d_ln = dz_conv * _silu(cgate) * _dsilu(ln)
        g_clg[...] += _rowsum(d_ln * xh)
        g_clb[...] += _rowsum(d_ln)
        dxh = d_ln * clg_ref[...]
        dc = rstd * (dxh - _lanemean(dxh) - xh * _lanemean(dxh * xh))
        g_cb[...] += _rowsum(dc)
        nxt_ref[1, 0:t_rows, :] = dc
        _tap_grads(cw_acc, pad_ref, shift_ref, dc, ext_ref, 1, CONV_K, HALO - (CONV_K - 1), t_rows)
        dyg = _window_sum(nxt_ref, 1, lambda i: cw[CONV_K - 1 - i:CONV_K - i, :], CONV_K, 0, t_rows)
        put_dp(C_A, dyg * sig_b)
        put_dp(C_B, dyg * c_a * sig_b * (1.0 - sig_b))

        u = cur(G_U)
        ggate = cur(G_GATE)
        vxh, vrstd, v = _layer_norm_parts(cur(G_V), slg_ref[...], slb_ref[...])
        vb = v.astype(BF16)
        mask = _tril_mask()
        wt = [jnp.where(mask, sw_ref[g], 0.0) for g in range(N_GROUPS)]
        wt_b = [w.astype(BF16) for w in wt]
        wtt_b = [w.T.astype(BF16) for w in wt]
        dz_sgu = dzp(2)
        d_sgu = dz_sgu * _silu(ggate)
        d_sp = d_sgu * u
        d_spb = d_sp.astype(BF16)
        sp_rows, dv_rows = [], []
        for sub in range(t_rows // CHUNK):
            rows = slice(sub * CHUNK, (sub + 1) * CHUNK)
            sp_g, dv_g = [], []
            for g in range(N_GROUPS):
                cols = slice(g * GROUP_W, (g + 1) * GROUP_W)
                sp_g.append(_dot(wt_b[g], vb[rows, cols]))
                g_sw[g] += jnp.where(mask, _dot_nt(d_spb[rows, cols], vb[rows, cols]), 0.0)
                dv_g.append(_dot(wtt_b[g], d_spb[rows, cols]))
            sp_rows.append(jnp.concatenate(sp_g, axis=1) + sbias_ref[...])
            dv_rows.append(jnp.concatenate(dv_g, axis=1))
            sb_acc[...] += d_sp[rows, :]
        sp = jnp.concatenate(sp_rows, axis=0)
        dv = jnp.concatenate(dv_rows, axis=0)
        put_dp(G_GATE, dz_sgu * (u * sp) * _dsilu(ggate))
        put_dp(G_U, d_sgu * sp)
        g_slg[...] += _rowsum(dv * vxh)
        g_slb[...] += _rowsum(dv)
        dvx = dv * slg_ref[...]
        put_dp(G_V, vrstd * (dvx - _lanemean(dvx) - vxh * _lanemean(dvx * vxh)))

        s_b, s_c, s_x, sgate = cur(S_B), cur(S_C), cur(S_X), cur(S_GATE)
        put_ext(2, hal(S_C) * hal(S_X), s_c * s_x)
        scw = _lanes(scw_ref)
        cv = _window_sum(ext_ref, 2, lambda k: scw[k:k + 1, :], SHORT_K, HALO - (SHORT_K - 1), t_rows)
        dz_sc = dzp(3)
        put_dp(S_GATE, dz_sc * (s_b * cv) * _dsilu(sgate))
        d_pre = dz_sc * _silu(sgate)
        put_dp(S_B, d_pre * cv)
        dcv = d_pre * s_b
        nxt_ref[2, 0:t_rows, :] = dcv
        _tap_grads(scw_acc, pad_ref, shift_ref, dcv, ext_ref, 2, SHORT_K, HALO - (SHORT_K - 1), t_rows)
        du = _window_sum(nxt_ref, 2, lambda i: scw[SHORT_K - 1 - i:SHORT_K - i, :], SHORT_K, 0, t_rows)
        put_dp(S_C, du * s_x)
        put_dp(S_X, du * s_c)

        nxt_ref[:, t_rows:t_rows + HALO, :] = nxt_ref[:, 0:HALO, :]

        @pl.when(jnp.logical_and(b == nb - 1, r == nc - 1))
        def _():
            lane = lax.broadcasted_iota(jnp.int32, (CHUNK, GROUP_W), 1)
            out = jnp.zeros((CHUNK, GROUP_W), F32)
            for g in range(N_GROUPS):
                col = jnp.sum(sb_acc[:, g * GROUP_W:(g + 1) * GROUP_W], axis=1, keepdims=True)
                out = jnp.where(lane == g, col, out)
            g_sb[...] = out
            g_cw[...] = jnp.sum(cw_acc[...], axis=1)
            g_scw[...] = jnp.sum(scw_acc[...], axis=1)

    def acc_spec(shape):
        nd = len(shape)
        return pl.BlockSpec(shape, lambda b, r: (0,) * nd)

    acc_shapes = [
        (N_GROUPS, GROUP_W, GROUP_W),
        (1, BRANCH_W),
        (CONV_ROWS, BRANCH_W),
        (1, BRANCH_W),
        (1, BRANCH_W),
        (1, BRANCH_W),
        (1, BRANCH_W),
        (1, BRANCH_W),
        (N_GROUPS, CHUNK, CHUNK),
        (CHUNK, GROUP_W),
        (SHORT_ROWS, BRANCH_W),
    ]
    outs = pl.pallas_call(
        body,
        name=f"mixers_bwd_l{layer}",
        grid=(nb, nc),
        in_specs=[
            pl.BlockSpec((None, t_rows, PIECE_COLS), lambda b, r: (b, nc - 1 - r, 0)),
            pl.BlockSpec((None, HALO, PIECE_COLS), lambda b, r: (b, jnp.maximum((nc - 1 - r) * hb - 1, 0), 0)),
            pl.BlockSpec((None, t_rows, N_BRANCH * BRANCH_W), lambda b, r: (b, nc - 1 - r, 0)),
        ] + _mixer_weight_specs(layer) + [ANY],
        out_specs=[pl.BlockSpec((None, t_rows, PIECE_COLS), lambda b, r: (b, nc - 1 - r, 0))]
        + [acc_spec(s) for s in acc_shapes],
        out_shape=[jax.ShapeDtypeStruct(dproj3.shape, BF16)] + [jax.ShapeDtypeStruct(s, F32) for s in acc_shapes],
        scratch_shapes=[
            pltpu.VMEM((3, HALO + t_rows + 8, BRANCH_W), F32),
            pltpu.VMEM((3, t_rows + HALO, BRANCH_W), F32),
            pltpu.VMEM((CHUNK, BRANCH_W), F32),
            pltpu.VMEM((CONV_ROWS, 8, BRANCH_W), F32),
            pltpu.VMEM((SHORT_ROWS, 8, BRANCH_W), F32),
            pltpu.VMEM((t_rows + 16, BRANCH_W), F32),
            pltpu.VMEM((t_rows + 8, BRANCH_W), F32),
        ],
        input_output_aliases={3 + 11: 0},
        compiler_params=_params("arbitrary", "arbitrary"),
    )(proj3, proj3, dz3, *_mixer_weight_args(mw), dproj3)
    dproj3 = outs[0]
    names = ["pool_w", "pool_scale", "conv_w", "conv_b", "conv_ln_g", "conv_ln_b", "sgu_ln_g", "sgu_ln_b",
             "sgu_w", "sgu_b", "sc_w"]
    return dproj3, dict(zip(names, outs[1:]))


def _merge_fwd(z2, proj2, x2, wbr_full, wo_full, layer):
    n = x2.shape[0]
    tm = _tile(n, 512)

    def body(z_ref, gate_ref, x_ref, wbr_ref, wo_ref, xn_ref, mg_ref, acc_ref):
        nbr = pl.program_id(1)
        zt = z_ref[...]
        bo = jnp.concatenate([_dot(zt, wbr_ref[s]) for s in range(N_CHIPS)], axis=1)
        contrib = _sigmoid(gate_ref[...].astype(F32)) * bo

        @pl.when(nbr == 0)
        def _():
            acc_ref[...] = contrib

        @pl.when(nbr > 0)
        def _():
            acc_ref[...] += contrib

        @pl.when(nbr == N_BRANCH - 1)
        def _():
            mg = acc_ref[...].astype(BF16)
            mg_ref[...] = mg
            xn_ref[...] = x_ref[...] + _dot(mg, wo_ref[...])

    return pl.pallas_call(
        body,
        name=f"merge_fwd_l{layer}",
        grid=(n // tm, N_BRANCH),
        in_specs=[
            pl.BlockSpec((tm, BRANCH_W), lambda i, b: (i, b)),
            pl.BlockSpec((tm, D_MODEL), lambda i, b: (i, GATE_BLOCK0 + b)),
            pl.BlockSpec((tm, D_MODEL), lambda i, b: (i, 0)),
            pl.BlockSpec((None, N_CHIPS, None, BRANCH_W, BR_SHARD), lambda i, b: (layer, 0, b, 0, 0)),
            pl.BlockSpec((None, D_MODEL, D_MODEL), lambda i, b: (layer, 0, 0)),
        ],
        out_specs=[
            pl.BlockSpec((tm, D_MODEL), lambda i, b: (i, 0)),
            pl.BlockSpec((tm, D_MODEL), lambda i, b: (i, 0)),
        ],
        out_shape=[jax.ShapeDtypeStruct((n, D_MODEL), F32), jax.ShapeDtypeStruct((n, D_MODEL), BF16)],
        scratch_shapes=[pltpu.VMEM((tm, D_MODEL), F32)],
        compiler_params=_params("arbitrary", "arbitrary"),
    )(z2, proj2, x2, wbr_full, wo_full)


def _merge_bwd(dout2, merged2, z2, proj2, wbr_full, wo_full, gwo_prev, gwbr_prev, layer):
    n = dout2.shape[0]
    tm = _tile(n, 512)
    nt = n // tm
    first_layer_call = gwo_prev is None

    def body(*refs):
        if first_layer_call:
            (do_ref, mg_ref, z_ref, gate_ref, wbr_ref, wo_ref,
             dz_ref, dg_ref, gwo_ref, gwbr_ref, dm_ref, awo_ref, awbr_ref) = refs
        else:
            (do_ref, mg_ref, z_ref, gate_ref, wbr_ref, wo_ref, _, _,
             dz_ref, dg_ref, gwo_ref, gwbr_ref, dm_ref, awo_ref, awbr_ref) = refs
        i = pl.program_id(0)
        nbr = pl.program_id(1)

        @pl.when(nbr == 0)
        def _():
            do_b = do_ref[...].astype(BF16)
            dm_ref[...] = _dot_nt(do_b, wo_ref[...])
            gw = _dot_tn(mg_ref[...], do_b)

            @pl.when(i == 0)
            def _():
                awo_ref[...] = gw

            @pl.when(i > 0)
            def _():
                awo_ref[...] += gw

        zt = z_ref[...]
        bo = jnp.concatenate([_dot(zt, wbr_ref[s]) for s in range(N_CHIPS)], axis=1)
        gt = _sigmoid(gate_ref[...].astype(F32))
        dm = dm_ref[...]
        dbo = (dm * gt).astype(BF16)
        dg_ref[...] = (dm * bo * gt * (1.0 - gt)).astype(BF16)
        dzt = jnp.zeros((tm, BRANCH_W), F32)
        for s in range(N_CHIPS):
            dbo_s = dbo[:, s * BR_SHARD:(s + 1) * BR_SHARD]
            dzt = dzt + _dot_nt(dbo_s, wbr_ref[s])
            gw = _dot_tn(zt, dbo_s)

            @pl.when(i == 0)
            def _():
                awbr_ref[s, nbr] = gw

            @pl.when(i > 0)
            def _():
                awbr_ref[s, nbr] += gw

        dz_ref[...] = dzt.astype(BF16)

        @pl.when(jnp.logical_and(i == nt - 1, nbr == N_BRANCH - 1))
        def _():
            pltpu.sync_copy(awo_ref, gwo_ref.at[layer])
            pltpu.sync_copy(awbr_ref, gwbr_ref.at[layer])

    in_specs = [
        pl.BlockSpec((tm, D_MODEL), lambda i, b: (i, 0)),
        pl.BlockSpec((tm, D_MODEL), lambda i, b: (i, 0)),
        pl.BlockSpec((tm, BRANCH_W), lambda i, b: (i, b)),
        pl.BlockSpec((tm, D_MODEL), lambda i, b: (i, GATE_BLOCK0 + b)),
        pl.BlockSpec((None, N_CHIPS, None, BRANCH_W, BR_SHARD), lambda i, b: (layer, 0, b, 0, 0)),
        pl.BlockSpec((None, D_MODEL, D_MODEL), lambda i, b: (layer, 0, 0)),
    ]
    args = [dout2, merged2, z2, proj2, wbr_full, wo_full]
    aliases = {}
    if not first_layer_call:
        in_specs += [ANY, ANY]
        args += [gwo_prev, gwbr_prev]
        aliases = {6: 2, 7: 3}
    return pl.pallas_call(
        body,
        name=f"merge_bwd_l{layer}",
        grid=(nt, N_BRANCH),
        in_specs=in_specs,
        out_specs=[
            pl.BlockSpec((tm, BRANCH_W), lambda i, b: (i, b)),
            pl.BlockSpec((tm, D_MODEL), lambda i, b: (i, GATE_BLOCK0 + b)),
            ANY,
            ANY,
        ],
        out_shape=[
            jax.ShapeDtypeStruct((n, N_BRANCH * BRANCH_W), BF16),
            jax.ShapeDtypeStruct((n, IN_COLS), BF16),
            jax.ShapeDtypeStruct((DEPTH, D_MODEL, D_MODEL), F32),
            jax.ShapeDtypeStruct((DEPTH, N_CHIPS, N_BRANCH, BRANCH_W, BR_SHARD), F32),
        ],
        scratch_shapes=[
            pltpu.VMEM((tm, D_MODEL), F32),
            pltpu.VMEM((D_MODEL, D_MODEL), F32),
            pltpu.VMEM((N_CHIPS, N_BRANCH, BRANCH_W, BR_SHARD), F32),
        ],
        input_output_aliases=aliases,
        compiler_params=_params("arbitrary", "arbitrary"),
    )(*args)


def _loss_head(x2, g_row, tgt2):
    n = x2.shape[0]
    tm = _tile(n, 512)

    def body(x_ref, g_ref, t_ref, dx_ref, loss_ref, dg_ref):
        @pl.when(pl.program_id(0) == 0)
        def _():
            loss_ref[...] = jnp.zeros_like(loss_ref)
            dg_ref[...] = jnp.zeros_like(dg_ref)

        xv = x_ref[...]
        g = g_ref[...]
        r = lax.rsqrt(_lanemean(xv * xv) + RMS_EPS)
        xh = xv * r
        err = xh * g - t_ref[...]
        loss_ref[...] += 0.5 * jnp.sum(_lanemean(err * err), axis=0, keepdims=True)
        dy = err * (1.0 / D_MODEL)
        dg_ref[...] += _rowsum(dy * xh)
        dxh = dy * g
        dx_ref[...] = r * (dxh - xh * _lanemean(dxh * xh))

    return pl.pallas_call(
        body,
        name="loss_head",
        grid=(n // tm,),
        in_specs=[
            pl.BlockSpec((tm, D_MODEL), lambda i: (i, 0)),
            pl.BlockSpec((1, D_MODEL), lambda i: (0, 0)),
            pl.BlockSpec((tm, D_MODEL), lambda i: (i, 0)),
        ],
        out_specs=[
            pl.BlockSpec((tm, D_MODEL), lambda i: (i, 0)),
            pl.BlockSpec((1, GROUP_W), lambda i: (0, 0)),
            pl.BlockSpec((1, D_MODEL), lambda i: (0, 0)),
        ],
        out_shape=[jax.ShapeDtypeStruct((n, D_MODEL), F32), jax.ShapeDtypeStruct((1, GROUP_W), F32),
                   jax.ShapeDtypeStruct((1, D_MODEL), F32)],
        compiler_params=_params("arbitrary"),
    )(x2, g_row, tgt2)


def _inproj_bwd_x(dproj2, w_full, x2, g_row, dout2, layer):
    n = x2.shape[0]
    tm = _tile(n, 1024)

    def body(dp_ref, w_ref, x_ref, g_ref, do_ref, dx_ref, dng_ref, dh_ref):
        i = pl.program_id(0)
        s = pl.program_id(1)
        part = _dot_nt(dp_ref[...], w_ref[...])

        @pl.when(s == 0)
        def _():
            dh_ref[...] = part

        @pl.when(s > 0)
        def _():
            dh_ref[...] += part

        @pl.when(jnp.logical_and(i == 0, s == 0))
        def _():
            dng_ref[...] = jnp.zeros_like(dng_ref)

        @pl.when(s == N_COL_TILES - 1)
        def _():
            xv = x_ref[...]
            r = lax.rsqrt(_lanemean(xv * xv) + RMS_EPS)
            xh = xv * r
            dh = dh_ref[...]
            dng_ref[...] += _rowsum(dh * xh)
            dxh = dh * g_ref[...]
            dx_ref[...] = do_ref[...] + r * (dxh - xh * _lanemean(dxh * xh))

    return pl.pallas_call(
        body,
        name=f"inproj_bwd_x_l{layer}",
        grid=(n // tm, N_COL_TILES),
        in_specs=[
            pl.BlockSpec((tm, COL_TILE), lambda i, s: (i, s)),
            pl.BlockSpec((None, None, D_MODEL, COL_TILE), lambda i, s: (layer, s // 2, 0, s % 2)),
            pl.BlockSpec((tm, D_MODEL), lambda i, s: (i, 0)),
            pl.BlockSpec((1, D_MODEL), lambda i, s: (0, 0)),
            pl.BlockSpec((tm, D_MODEL), lambda i, s: (i, 0)),
        ],
        out_specs=[
            pl.BlockSpec((tm, D_MODEL), lambda i, s: (i, 0)),
            pl.BlockSpec((1, D_MODEL), lambda i, s: (0, 0)),
        ],
        out_shape=[jax.ShapeDtypeStruct((n, D_MODEL), F32), jax.ShapeDtypeStruct((1, D_MODEL), F32)],
        scratch_shapes=[pltpu.VMEM((tm, D_MODEL), F32)],
        compiler_params=_params("arbitrary", "arbitrary"),
    )(dproj2, w_full, x2, g_row, dout2)


def _inproj_bwd_w(h2, dproj2, gwin_prev, layer):
    n = h2.shape[0]
    tm = _tile(n, 1024)

    def body(*refs):
        h_ref, dp_ref = refs[0], refs[1]
        gw_ref = refs[-1]
        gw = _dot_tn(h_ref[...], dp_ref[...])

        @pl.when(pl.program_id(1) == 0)
        def _():
            gw_ref[...] = gw

        @pl.when(pl.program_id(1) > 0)
        def _():
            gw_ref[...] += gw

    in_specs = [
        pl.BlockSpec((tm, D_MODEL), lambda s, i: (i, 0)),
        pl.BlockSpec((tm, COL_TILE), lambda s, i: (i, s)),
    ]
    args = [h2, dproj2]
    aliases = {}
    if gwin_prev is not None:
        in_specs.append(ANY)
        args.append(gwin_prev)
        aliases = {2: 0}
    return pl.pallas_call(
        body,
        name=f"inproj_bwd_w_l{layer}",
        grid=(N_COL_TILES, n // tm),
        in_specs=in_specs,
        out_specs=pl.BlockSpec((None, None, D_MODEL, COL_TILE), lambda s, i: (layer, s // 2, 0, s % 2)),
        out_shape=jax.ShapeDtypeStruct((DEPTH, N_CHIPS, D_MODEL, SHARD_COLS), F32),
        input_output_aliases=aliases,
        compiler_params=_params("arbitrary", "arbitrary"),
    )(*args)


def _place():
    x, y, c = lax.axis_index("x"), lax.axis_index("y"), lax.axis_index("c")
    chip = 2 * x + y
    peers = [(1 - x, y), (x, 1 - y), (1 - x, 1 - y)]
    return x, y, c, chip, peers


def _remote(src, dst, send_sem, recv_sem, dev):
    return pltpu.make_async_remote_copy(src_ref=src, dst_ref=dst, send_sem=send_sem, recv_sem=recv_sem,
                                        device_id=dev, device_id_type=MESH_ID)


def _allgather_weights(shards):
    na = len(shards)

    def body(*refs):
        ins, outs = refs[:na], refs[na:2 * na]
        send_sems, recv_sems, fsend_sems, frecv_sems, loc_sems = refs[2 * na:]
        x, y, c, chip, peers = _place()
        sib = (x, y, 1 - c)
        locs = [pltpu.make_async_copy(ins[a], outs[a].at[:, chip], loc_sems.at[a]) for a in range(na)]
        for cp in locs:
            cp.start()
        pending = []
        for k, (px, py) in enumerate(peers):
            for a in range(na):
                cp = _remote(ins[a].at[c], outs[a].at[c, chip], send_sems.at[k * na + a], recv_sems.at[k * na + a],
                             (px, py, c))
                cp.start()
                pending.append(cp)
        for k, (px, py) in enumerate(peers):
            pchip = 2 * px + py
            for a in range(na):
                slab = outs[a].at[c, pchip]
                _remote(slab, slab, send_sems.at[k * na + a], recv_sems.at[k * na + a], (px, py, c)).wait_recv()
                cp = _remote(slab, slab, fsend_sems.at[k * na + a], frecv_sems.at[k * na + a], sib)
                cp.start()
                pending.append(cp)
        for k, (px, py) in enumerate(peers):
            pchip = 2 * px + py
            for a in range(na):
                slab = outs[a].at[1 - c, pchip]
                _remote(slab, slab, fsend_sems.at[k * na + a], frecv_sems.at[k * na + a], sib).wait_recv()
        for cp in pending:
            cp.wait_send()
        for cp in locs:
            cp.wait()

    return pl.pallas_call(
        body,
        name="allgather_weights",
        in_specs=[ANY] * na,
        out_specs=[ANY] * na,
        out_shape=[jax.ShapeDtypeStruct((DEPTH, N_CHIPS) + a.shape[1:], a.dtype) for a in shards],
        scratch_shapes=[pltpu.SemaphoreType.DMA((3 * na,)), pltpu.SemaphoreType.DMA((3 * na,)),
                        pltpu.SemaphoreType.DMA((3 * na,)), pltpu.SemaphoreType.DMA((3 * na,)),
                        pltpu.SemaphoreType.DMA((na,))],
    )(*shards)


def _swap_layers(grads):
    na = len(grads)

    def body(*refs):
        ins, outs = refs[:na], refs[na:2 * na]
        send_sems, recv_sems = refs[2 * na:]
        x, y, c, _, _ = _place()
        cps = [_remote(ins[a].at[1 - c], outs[a], send_sems.at[a], recv_sems.at[a], (x, y, 1 - c)) for a in range(na)]
        for cp in cps:
            cp.start()
        for cp in cps:
            cp.wait()

    return pl.pallas_call(
        body,
        name="swap_layers",
        in_specs=[ANY] * na,
        out_specs=[ANY] * na,
        out_shape=[jax.ShapeDtypeStruct(a.shape[1:], a.dtype) for a in grads],
        scratch_shapes=[pltpu.SemaphoreType.DMA((na,)), pltpu.SemaphoreType.DMA((na,))],
    )(*grads)


def _add_own_layer(grad, other, core):
    _, rows, cols = grad.shape
    tr = _tile(rows, max(8, (1 << 20) // (4 * cols) // 8 * 8))

    def body(core_ref, g_ref, o_ref, out_ref):
        del core_ref
        out_ref[...] = g_ref[...] + o_ref[...]

    return pl.pallas_call(
        body,
        name=f"add_own_layer_{rows}x{cols}",
        grid_spec=pltpu.PrefetchScalarGridSpec(
            num_scalar_prefetch=1,
            grid=(rows // tr,),
            in_specs=[
                pl.BlockSpec((None, tr, cols), lambda i, core_ref: (core_ref[0], i, 0)),
                pl.BlockSpec((tr, cols), lambda i, core_ref: (i, 0)),
            ],
            out_specs=pl.BlockSpec((tr, cols), lambda i, core_ref: (i, 0)),
        ),
        out_shape=jax.ShapeDtypeStruct((rows, cols), F32),
        compiler_params=_params("arbitrary"),
    )(core, grad, other)


def _scatter_chip_sums(sums):
    na = len(sums)

    def body(*refs):
        ins, outs = refs[:na], refs[na:2 * na]
        send_sems, recv_sems, fsend_sems, frecv_sems, loc_sems = refs[2 * na:]
        x, y, c, chip, peers = _place()
        sib = (x, y, 1 - c)
        locs = [pltpu.make_async_copy(ins[a].at[chip], outs[a].at[c, chip], loc_sems.at[a]) for a in range(na)]
        for cp in locs:
            cp.start()
        pending = []
        for a in range(na):
            cp = _remote(ins[a].at[chip], outs[a].at[c, chip], fsend_sems.at[3 * na + a], frecv_sems.at[3 * na + a], sib)
            cp.start()
            pending.append(cp)
        for k, (px, py) in enumerate(peers):
            pchip = 2 * px + py
            for a in range(na):
                cp = _remote(ins[a].at[pchip], outs[a].at[c, chip], send_sems.at[k * na + a],
                             recv_sems.at[k * na + a], (px, py, c))
                cp.start()
                pending.append(cp)
        for k, (px, py) in enumerate(peers):
            pchip = 2 * px + py
            for a in range(na):
                slab = outs[a].at[c, pchip]
                _remote(slab, slab, send_sems.at[k * na + a], recv_sems.at[k * na + a], (px, py, c)).wait_recv()
                cp = _remote(slab, slab, fsend_sems.at[k * na + a], frecv_sems.at[k * na + a], sib)
                cp.start()
                pending.append(cp)
        for a in range(na):
            slab = outs[a].at[1 - c, chip]
            _remote(slab, slab, fsend_sems.at[3 * na + a], frecv_sems.at[3 * na + a], sib).wait_recv()
        for k, (px, py) in enumerate(peers):
            pchip = 2 * px + py
            for a in range(na):
                slab = outs[a].at[1 - c, pchip]
                _remote(slab, slab, fsend_sems.at[k * na + a], frecv_sems.at[k * na + a], sib).wait_recv()
        for cp in pending:
            cp.wait_send()
        for cp in locs:
            cp.wait()

    return pl.pallas_call(
        body,
        name="scatter_chip_sums",
        in_specs=[ANY] * na,
        out_specs=[ANY] * na,
        out_shape=[jax.ShapeDtypeStruct((DEPTH,) + a.shape, a.dtype) for a in sums],
        scratch_shapes=[pltpu.SemaphoreType.DMA((3 * na,)), pltpu.SemaphoreType.DMA((3 * na,)),
                        pltpu.SemaphoreType.DMA((4 * na,)), pltpu.SemaphoreType.DMA((4 * na,)),
                        pltpu.SemaphoreType.DMA((na,))],
    )(*sums)


def _allreduce_small(packed):
    rows = packed.shape[0]

    def body(sg_ref, res_ref, sib_ref, cs_ref, rem_ref, s1_send, s1_recv, s2_send, s2_recv):
        x, y, c, chip, peers = _place()
        cp = _remote(sg_ref, sib_ref, s1_send, s1_recv, (x, y, 1 - c))
        cp.start()
        cp.wait()
        cs_ref[...] = sg_ref[...] + sib_ref[...]
        rem_ref[chip] = cs_ref[...]
        cps = [_remote(cs_ref, rem_ref.at[chip], s2_send.at[k], s2_recv.at[chip], (px, py, c))
               for k, (px, py) in enumerate(peers)]
        for cp in cps:
            cp.start()
        for k, (px, py) in enumerate(peers):
            pchip = 2 * px + py
            _remote(cs_ref, rem_ref.at[pchip], s2_send.at[k], s2_recv.at[pchip], (px, py, c)).wait_recv()
        for cp in cps:
            cp.wait_send()
        res_ref[...] = ((rem_ref[0] + rem_ref[1]) + rem_ref[2]) + rem_ref[3]

    return pl.pallas_call(
        body,
        name="allreduce_small",
        in_specs=[VMEM_WHOLE],
        out_specs=VMEM_WHOLE,
        out_shape=jax.ShapeDtypeStruct((rows, GROUP_W), F32),
        scratch_shapes=[
            pltpu.VMEM((rows, GROUP_W), F32),
            pltpu.VMEM((rows, GROUP_W), F32),
            pltpu.VMEM((N_CHIPS, rows, GROUP_W), F32),
            pltpu.SemaphoreType.DMA,
            pltpu.SemaphoreType.DMA,
            pltpu.SemaphoreType.DMA((3,)),
            pltpu.SemaphoreType.DMA((N_CHIPS,)),
        ],
        compiler_params=pltpu.CompilerParams(vmem_limit_bytes=VMEM_LIMIT),
    )(packed)


def _adamw_math(w, g, m, v):
    m = ADAM_B1 * m + (1.0 - ADAM_B1) * g
    v = ADAM_B2 * v + (1.0 - ADAM_B2) * jnp.square(g)
    m_hat = m / (1.0 - ADAM_B1 ** ADAM_STEP)
    v_hat = v / (1.0 - ADAM_B2 ** ADAM_STEP)
    delta = -ADAM_LR * (m_hat / (jnp.sqrt(v_hat) + ADAM_EPS) + ADAM_WD * w)
    return delta, m, v


def _adamw_sharded(w, m, v, parts, name):
    depth, rows, cols = w.shape
    tr = _tile(rows, max(8, (1 << 19) // (4 * cols) // 8 * 8))

    def body(w_ref, m_ref, v_ref, p_ref, g_out, d_out, m_out, v_out):
        g = ((p_ref[0] + p_ref[1]) + p_ref[2]) + p_ref[3]
        delta, m2, v2 = _adamw_math(w_ref[...], g, m_ref[...], v_ref[...])
        g_out[...] = g
        d_out[...] = delta
        m_out[...] = m2
        v_out[...] = v2

    spec = pl.BlockSpec((None, tr, cols), lambda l, i: (l, i, 0))
    return pl.pallas_call(
        body,
        name=f"adamw_{name}",
        grid=(depth, rows // tr),
        in_specs=[spec, spec, spec, pl.BlockSpec((None, N_CHIPS, tr, cols), lambda l, i: (l, 0, i, 0))],
        out_specs=[spec] * 4,
        out_shape=[jax.ShapeDtypeStruct(w.shape, F32)] * 4,
        compiler_params=_params("arbitrary", "arbitrary"),
    )(w, m, v, parts)


def _adamw_packed(w, m, v, g):
    rows = w.shape[0]
    tr = _tile(rows, rows // 2 if rows % 16 == 0 else rows)

    def body(w_ref, m_ref, v_ref, g_ref, d_out, m_out, v_out):
        delta, m2, v2 = _adamw_math(w_ref[...], g_ref[...], m_ref[...], v_ref[...])
        d_out[...] = delta
        m_out[...] = m2
        v_out[...] = v2

    spec = pl.BlockSpec((tr, GROUP_W), lambda i: (i, 0))
    return pl.pallas_call(
        body,
        name="adamw_small",
        grid=(rows // tr,),
        in_specs=[spec] * 4,
        out_specs=[spec] * 3,
        out_shape=[jax.ShapeDtypeStruct(w.shape, F32)] * 3,
        compiler_params=_params("arbitrary"),
    )(w, m, v, g)


SMALL = ["norm_g", "pool_w", "pool_scale", "conv_b", "conv_ln_g", "conv_ln_b", "sgu_ln_g", "sgu_ln_b", "sgu_w",
         "sgu_b", "final_g"]
SHARDED = ["w_in", "w_branch", "w_o", "conv_w", "sc_w"]
WEIGHTS = ["norm_g", "w_in", "pool_w", "pool_scale", "conv_w", "conv_b", "conv_ln_g", "conv_ln_b", "sgu_ln_g",
           "sgu_ln_b", "sgu_w", "sgu_b", "sc_w", "w_branch", "w_o", "final_g"]


def _pack_small(tree):
    return jnp.concatenate([tree[k].reshape(-1, GROUP_W) for k in SMALL], axis=0)


def _unpack_small(packed, like):
    out, r = {}, 0
    for k in SMALL:
        nr = like[k].size // GROUP_W
        out[k] = packed[r:r + nr].reshape(like[k].shape)
        r += nr
    return out


def _pad_rows(a, rows):
    return jnp.pad(a, ((0, 0), (0, rows - a.shape[1]), (0, 0)))


def _shard_major_rows(a):
    return a.reshape(a.shape[0], N_CHIPS, GROUP_W).transpose(1, 0, 2)


def kernel(x, norm_g, w_in, pool_w, pool_scale, conv_w, conv_b, conv_ln_g, conv_ln_b, sgu_ln_g, sgu_ln_b, sgu_w, sgu_b, sc_w, w_branch, w_o, final_g, loss_target, m_norm_g, m_w_in, m_pool_w, m_pool_scale, m_conv_w, m_conv_b, m_conv_ln_g, m_conv_ln_b, m_sgu_ln_g, m_sgu_ln_b, m_sgu_w, m_sgu_b, m_sc_w, m_w_branch, m_w_o, m_final_g, v_norm_g, v_w_in, v_pool_w, v_pool_scale, v_conv_w, v_conv_b, v_conv_ln_g, v_conv_ln_b, v_sgu_ln_g, v_sgu_ln_b, v_sgu_w, v_sgu_b, v_sc_w, v_w_branch, v_w_o, v_final_g):
    w = dict(norm_g=norm_g, w_in=w_in, pool_w=pool_w, pool_scale=pool_scale, conv_w=conv_w, conv_b=conv_b,
             conv_ln_g=conv_ln_g, conv_ln_b=conv_ln_b, sgu_ln_g=sgu_ln_g, sgu_ln_b=sgu_ln_b, sgu_w=sgu_w,
             sgu_b=sgu_b, sc_w=sc_w, w_branch=w_branch, w_o=w_o, final_g=final_g)
    mom = dict(norm_g=m_norm_g, w_in=m_w_in, pool_w=m_pool_w, pool_scale=m_pool_scale, conv_w=m_conv_w,
               conv_b=m_conv_b, conv_ln_g=m_conv_ln_g, conv_ln_b=m_conv_ln_b, sgu_ln_g=m_sgu_ln_g,
               sgu_ln_b=m_sgu_ln_b, sgu_w=m_sgu_w, sgu_b=m_sgu_b, sc_w=m_sc_w, w_branch=m_w_branch, w_o=m_w_o,
               final_g=m_final_g)
    var = dict(norm_g=v_norm_g, w_in=v_w_in, pool_w=v_pool_w, pool_scale=v_pool_scale, conv_w=v_conv_w,
               conv_b=v_conv_b, conv_ln_g=v_conv_ln_g, conv_ln_b=v_conv_ln_b, sgu_ln_g=v_sgu_ln_g,
               sgu_ln_b=v_sgu_ln_b, sgu_w=v_sgu_w, sgu_b=v_sgu_b, sc_w=v_sc_w, w_branch=v_w_branch, w_o=v_w_o,
               final_g=v_final_g)

    nb, seq, _ = x.shape
    n = nb * seq
    core = lax.axis_index("c").astype(jnp.int32).reshape(1)

    win_f, wbr_f, wo_f, cw_f, scw_f = _allgather_weights([
        w_in.astype(BF16), w_branch.astype(BF16), w_o.astype(BF16),
        _pad_rows(conv_w, CONV_ROWS), _pad_rows(sc_w, SHORT_ROWS)])

    def mixer_weights():
        bias = jnp.repeat(jnp.swapaxes(sgu_b, 1, 2), GROUP_W, axis=2)
        row = lambda a: a.reshape(DEPTH, 1, BRANCH_W)
        return dict(pool_w=pool_w.astype(BF16), pool_scale=row(pool_scale), conv_w=cw_f, conv_b=row(conv_b),
                    conv_ln_g=row(conv_ln_g), conv_ln_b=row(conv_ln_b), sgu_ln_g=row(sgu_ln_g),
                    sgu_ln_b=row(sgu_ln_b), sgu_w=sgu_w, sgu_bias=bias, sc_w=scw_f)

    mw = mixer_weights()
    wo_f = wo_f.reshape(DEPTH, D_MODEL, D_MODEL)

    xs, projs, hs, zs, mgs = [x.reshape(n, D_MODEL)], [], [], [], []
    for l in range(DEPTH):
        proj, h = _inproj_fwd(xs[l], norm_g[l:l + 1], win_f, l)
        z = _mixers_fwd(proj.reshape(nb, seq, IN_COLS), mw, l).reshape(n, N_BRANCH * BRANCH_W)
        x_new, merged = _merge_fwd(z, proj, xs[l], wbr_f, wo_f, l)
        projs.append(proj)
        hs.append(h)
        zs.append(z)
        mgs.append(merged)
        xs.append(x_new)

    dx, loss_part, g_final = _loss_head(xs[DEPTH], final_g.reshape(1, D_MODEL), loss_target.reshape(n, D_MODEL))
    loss = lax.psum(loss_part[0, 0], ("x", "y", "c"))

    gwin = gwo = gwbr = None
    g_small = {"final_g": g_final.reshape(D_MODEL)}
    per_layer = {}
    for l in reversed(range(DEPTH)):
        dz, dproj, gwo, gwbr = _merge_bwd(dx, mgs[l], zs[l], projs[l], wbr_f, wo_f, gwo, gwbr, l)
        dproj3, gm = _mixers_bwd(projs[l].reshape(nb, seq, IN_COLS), dz.reshape(nb, seq, N_BRANCH * BRANCH_W),
                                 dproj.reshape(nb, seq, IN_COLS), mw, l)
        dproj = dproj3.reshape(n, IN_COLS)
        gwin = _inproj_bwd_w(hs[l], dproj, gwin, l)
        dx, g_norm = _inproj_bwd_x(dproj, win_f, xs[l], norm_g[l:l + 1], dx, l)
        gm["norm_g"] = g_norm
        per_layer[l] = gm
    grad_x = dx.reshape(nb, seq, D_MODEL)

    def stack(k, shape):
        return jnp.stack([per_layer[l][k] for l in range(DEPTH)]).reshape(shape)

    g_small["norm_g"] = stack("norm_g", (DEPTH, D_MODEL))
    g_small["pool_w"] = stack("pool_w", pool_w.shape)
    g_small["sgu_w"] = stack("sgu_w", sgu_w.shape)
    g_small["sgu_b"] = jnp.swapaxes(stack("sgu_b", (DEPTH, CHUNK, GROUP_W))[:, :, :N_GROUPS], 1, 2)
    for k in ("pool_scale", "conv_b", "conv_ln_g", "conv_ln_b", "sgu_ln_g", "sgu_ln_b"):
        g_small[k] = stack(k, (DEPTH, BRANCH_W))
    gcw = jnp.stack([_shard_major_rows(per_layer[l]["conv_w"]) for l in range(DEPTH)])
    gscw = jnp.stack([_shard_major_rows(per_layer[l]["sc_w"]) for l in range(DEPTH)])

    partial = [gwin, gwbr, gwo.reshape(DEPTH, N_CHIPS, BR_SHARD, D_MODEL), gcw, gscw]
    others = _swap_layers(partial)
    sums = []
    for g, o in zip(partial, others):
        rows = g.size // (DEPTH * g.shape[-1])
        s = _add_own_layer(g.reshape(DEPTH, rows, g.shape[-1]), o.reshape(rows, g.shape[-1]), core)
        sums.append(s.reshape(o.shape))
    parts = _scatter_chip_sums(sums)

    grads, deltas, new_m, new_v = {}, {}, {}, {}

    def sharded_update(name, part, flat_shape, padded=None):
        def prep(a):
            if padded is not None:
                a = _pad_rows(a, padded)
            return a.reshape((DEPTH,) + flat_shape)

        res = _adamw_sharded(prep(w[name]), prep(mom[name]), prep(var[name]),
                             part.reshape((DEPTH, N_CHIPS) + flat_shape), name)
        shape = w[name].shape
        for tree, r in zip((grads, deltas, new_m, new_v), res):
            if padded is not None:
                r = r[:, :shape[1]]
            tree[name] = r.reshape(shape)

    sharded_update("w_in", parts[0], (D_MODEL, SHARD_COLS))
    sharded_update("w_branch", parts[1], (N_BRANCH * BRANCH_W, BR_SHARD))
    sharded_update("w_o", parts[2], (BR_SHARD, D_MODEL))
    sharded_update("conv_w", parts[3], (CONV_ROWS, GROUP_W), padded=CONV_ROWS)
    sharded_update("sc_w", parts[4], (SHORT_ROWS, GROUP_W), padded=SHORT_ROWS)

    g_packed = _allreduce_small(_pack_small(g_small))
    d_packed, m_packed, v_packed = _adamw_packed(_pack_small(w), _pack_small(mom), _pack_small(var), g_packed)
    for tree, packed in ((grads, g_packed), (deltas, d_packed), (new_m, m_packed), (new_v, v_packed)):
        tree.update(_unpack_small(packed, w))

    return (loss, grad_x, *[grads[k] for k in WEIGHTS], *[deltas[k] for k in WEIGHTS],
            *[new_m[k] for k in WEIGHTS], *[new_v[k] for k in WEIGHTS])
```

```python
import functools

import jax
import jax.numpy as jnp
from jax import lax
from jax.experimental import pallas as pl
from jax.experimental.pallas import tpu as pltpu

F32 = jnp.float32
BF16 = jnp.bfloat16

D_MODEL = 1024
DEPTH = 2
N_BRANCH = 4
BRANCH_W = 512
N_GROUPS = 4
GROUP_W = 128
POOL_WINDOWS = (2, 4, 8, 16)
CONV_K = 31
SHORT_K = 3
CHUNK = 128
N_PIECES = 12
PIECE_COLS = N_PIECES * BRANCH_W
IN_COLS = PIECE_COLS + N_BRANCH * D_MODEL
N_CHIPS = 4
SHARD_COLS = IN_COLS // N_CHIPS
BR_SHARD = D_MODEL // N_CHIPS
COL_TILE = SHARD_COLS // 2
N_COL_TILES = IN_COLS // COL_TILE
GATE_BLOCK0 = PIECE_COLS // D_MODEL
RMS_EPS = 1e-6
LN_EPS = 1e-5
HALO = 32
CONV_ROWS = 32
SHORT_ROWS = 8
CS_ROWS = 64

ADAM_LR = 0.001
ADAM_B1 = 0.9
ADAM_B2 = 0.999
ADAM_EPS = 1e-08
ADAM_WD = 0.01
ADAM_STEP = 10

VMEM_LIMIT = 52 * 1024 * 1024
MESH_ID = pl.DeviceIdType.MESH
ANY = pl.BlockSpec(memory_space=pl.ANY)
VMEM_WHOLE = pl.BlockSpec(memory_space=pltpu.VMEM)

(P_X, P_GATE, C_A, C_B, C_GATE, G_U, G_V, G_GATE, S_B, S_C, S_X, S_GATE) = range(N_PIECES)


def _params(*sem):
    return pltpu.CompilerParams(dimension_semantics=sem, vmem_limit_bytes=VMEM_LIMIT)


def _sigmoid(v):
    return 0.5 * jnp.tanh(0.5 * v) + 0.5


def _silu(v):
    return v * _sigmoid(v)


def _dsilu(v):
    s = _sigmoid(v)
    return s * (1.0 + v * (1.0 - s))


def _dot(a, b):
    return jnp.dot(a, b, preferred_element_type=F32)


def _dot_nt(a, b):
    return lax.dot_general(a, b, (((1,), (1,)), ((), ())), preferred_element_type=F32)


def _dot_tn(a, b):
    return lax.dot_general(a, b, (((0,), (0,)), ((), ())), preferred_element_type=F32)


def _rowsum(v):
    return jnp.sum(v, axis=0, keepdims=True)


def _lanemean(v):
    return jnp.mean(v, axis=-1, keepdims=True)


def _tile(n, want, mult=8):
    t = max(1, min(n, want))
    while n % t or (t % mult and t != n):
        t -= 1
    return t


def _place():
    x, y, c = lax.axis_index("x"), lax.axis_index("y"), lax.axis_index("c")
    chip = 2 * x + y
    peers = [(1 - x, y), (x, 1 - y), (1 - x, 1 - y)]
    return x, y, c, chip, peers


def _remote(src, dst, send_sem, recv_sem, dev):
    return pltpu.make_async_remote_copy(src_ref=src, dst_ref=dst, send_sem=send_sem, recv_sem=recv_sem,
                                        device_id=dev, device_id_type=MESH_ID)


class _Exchange:
    def __init__(self, inputs, out_shapes, plan, n_remote, n_local=0, aliases=None):
        self.inputs = list(inputs)
        self.out_shapes = list(out_shapes)
        self.plan = plan
        self.n_remote = n_remote
        self.n_local = n_local
        self.aliases = dict(aliases or {})

    def copies(self, in_refs, out_refs, send_sems, recv_sems, loc_sems):
        remote, local = self.plan(in_refs, out_refs)
        assert len(remote) == self.n_remote and len(local) == self.n_local
        cps = [_remote(s, d, send_sems.at[t], recv_sems.at[t], dev) for t, (s, d, dev) in enumerate(remote)]
        cps += [pltpu.make_async_copy(s, d, loc_sems.at[t]) for t, (s, d) in enumerate(local)]
        return cps

    def sem_shapes(self):
        return [pltpu.SemaphoreType.DMA((max(self.n_remote, 1),)), pltpu.SemaphoreType.DMA((max(self.n_remote, 1),)),
                pltpu.SemaphoreType.DMA((max(self.n_local, 1),))]


def _exchange_call(name, ex):
    n_ci = len(ex.inputs)

    def body(*refs):
        cins, couts = refs[:n_ci], refs[n_ci:n_ci + len(ex.out_shapes)]
        cps = ex.copies(cins, couts, *refs[n_ci + len(ex.out_shapes):])
        for cp in cps:
            cp.start()
        for cp in cps:
            cp.wait()

    return pl.pallas_call(
        body,
        name=name,
        in_specs=[ANY] * n_ci,
        out_specs=[ANY] * len(ex.out_shapes),
        out_shape=ex.out_shapes,
        scratch_shapes=ex.sem_shapes(),
        input_output_aliases=ex.aliases,
    )(*ex.inputs)


def _host_call(body, *, name, grid, in_specs, out_specs, out_shape, args, scratch_shapes=(), aliases=None, ride=None):
    n_in, n_out, n_scr = len(in_specs), len(out_specs), len(scratch_shapes)
    sem = ("arbitrary",) * len(grid)
    aliases = dict(aliases or {})
    if ride is None:
        outs = pl.pallas_call(body, name=name, grid=grid, in_specs=list(in_specs), out_specs=list(out_specs),
                              out_shape=list(out_shape), scratch_shapes=list(scratch_shapes),
                              input_output_aliases=aliases, compiler_params=_params(*sem))(*args)
        return list(outs), []
    n_ci, n_co = len(ride.inputs), len(ride.out_shapes)

    def full_body(*refs):
        ins, cins = refs[:n_in], refs[n_in:n_in + n_ci]
        o0 = n_in + n_ci
        outs, couts = refs[o0:o0 + n_out], refs[o0 + n_out:o0 + n_out + n_co]
        s0 = o0 + n_out + n_co
        scr, sems = refs[s0:s0 + n_scr], refs[s0 + n_scr:]
        first = functools.reduce(jnp.logical_and, [pl.program_id(d) == 0 for d in range(len(grid))])
        last = functools.reduce(jnp.logical_and, [pl.program_id(d) == grid[d] - 1 for d in range(len(grid))])

        @pl.when(first)
        def _():
            for cp in ride.copies(cins, couts, *sems):
                cp.start()

        body(*ins, *outs, *scr)

        @pl.when(last)
        def _():
            for cp in ride.copies(cins, couts, *sems):
                cp.wait()

    for ci, co in ride.aliases.items():
        aliases[n_in + ci] = n_out + co
    outs = pl.pallas_call(
        full_body, name=name, grid=grid, in_specs=list(in_specs) + [ANY] * n_ci,
        out_specs=list(out_specs) + [ANY] * n_co, out_shape=list(out_shape) + ride.out_shapes,
        scratch_shapes=list(scratch_shapes) + ride.sem_shapes(), input_output_aliases=aliases,
        compiler_params=_params(*sem))(*args, *ride.inputs)
    return list(outs[:n_out]), list(outs[n_out:])


def _allgather_layer0(shards):
    na = len(shards)

    def body(*refs):
        ins, outs = refs[:na], refs[na:2 * na]
        send_sems, recv_sems, fsend_sems, frecv_sems, loc_sems = refs[2 * na:]
        x, y, c, chip, peers = _place()
        sib = (x, y, 1 - c)
        locs = [pltpu.make_async_copy(ins[a].at[0], outs[a].at[chip], loc_sems.at[a]) for a in range(na)]
        for cp in locs:
            cp.start()
        pending = []
        for k, (px, py) in enumerate(peers):
            for a in range(na):
                cp = _remote(ins[a].at[0, c], outs[a].at[chip, c], send_sems.at[k * na + a],
                             recv_sems.at[k * na + a], (px, py, c))
                cp.start()
                pending.append(cp)
        for k, (px, py) in enumerate(peers):
            pchip = 2 * px + py
            for a in range(na):
                slab = outs[a].at[pchip, c]
                _remote(slab, slab, send_sems.at[k * na + a], recv_sems.at[k * na + a], (px, py, c)).wait_recv()
                cp = _remote(slab, slab, fsend_sems.at[k * na + a], frecv_sems.at[k * na + a], sib)
                cp.start()
                pending.append(cp)
        for k, (px, py) in enumerate(peers):
            pchip = 2 * px + py
            for a in range(na):
                slab = outs[a].at[pchip, 1 - c]
                _remote(slab, slab, fsend_sems.at[k * na + a], frecv_sems.at[k * na + a], sib).wait_recv()
        for cp in pending:
            cp.wait_send()
        for cp in locs:
            cp.wait()

    return pl.pallas_call(
        body,
        name="allgather_layer0",
        in_specs=[ANY] * na,
        out_specs=[ANY] * na,
        out_shape=[jax.ShapeDtypeStruct((N_CHIPS,) + a.shape[1:], a.dtype) for a in shards],
        scratch_shapes=[pltpu.SemaphoreType.DMA((3 * na,))] * 4 + [pltpu.SemaphoreType.DMA((na,))],
    )(*shards)


def _gather_layer1_first_half(win_s):
    def plan(cins, couts):
        _, _, c, chip, peers = _place()
        (src,), (dst,) = cins, couts
        return ([(src.at[1, 0], dst.at[chip, 0], (px, py, c)) for px, py in peers],
                [(src.at[1], dst.at[chip])])

    return _Exchange([win_s], [jax.ShapeDtypeStruct((N_CHIPS,) + win_s.shape[1:], win_s.dtype)], plan, 3, 1)


def _gather_layer1_second_half(win_s, win_f1):
    def plan(cins, couts):
        _, _, c, chip, peers = _place()
        src, (dst,) = cins[0], couts
        return [(src.at[1, 1], dst.at[chip, 1], (px, py, c)) for px, py in peers], []

    return _Exchange([win_s, win_f1], [jax.ShapeDtypeStruct(win_f1.shape, win_f1.dtype)], plan, 3, 0, aliases={1: 0})


def _gather_layer1_rest(shards):
    na = len(shards)

    def plan(cins, couts):
        _, _, c, chip, peers = _place()
        remote = [(cins[a].at[1], couts[a].at[chip], (px, py, c)) for a in range(na) for px, py in peers]
        return remote, [(cins[a].at[1], couts[a].at[chip]) for a in range(na)]

    return _Exchange(shards, [jax.ShapeDtypeStruct((N_CHIPS,) + a.shape[1:], a.dtype) for a in shards], plan,
                     3 * na, na)


def _swap_halves(grads):
    def plan(cins, couts):
        x, y, c, _, _ = _place()
        return [(g.at[:, 1 - c], r, (x, y, 1 - c)) for g, r in zip(cins, couts)], []

    return _Exchange(grads, [jax.ShapeDtypeStruct(g.shape[:1] + g.shape[2:], g.dtype) for g in grads], plan,
                     len(grads))


def _scatter_chip_sums(sums):
    na = len(sums)

    def plan(cins, couts):
        _, _, c, chip, peers = _place()
        remote = [(cins[a].at[2 * px + py], couts[a].at[chip, c], (px, py, c)) for a in range(na) for px, py in peers]
        return remote, [(cins[a].at[chip], couts[a].at[chip, c]) for a in range(na)]

    return _Exchange(sums, [jax.ShapeDtypeStruct((N_CHIPS, 2) + s.shape[1:], s.dtype) for s in sums], plan,
                     3 * na, na)


def _share_halves(parts):
    na = len(parts)

    def plan(cins, couts):
        x, y, c, _, _ = _place()
        return [(p.at[:, c], p.at[:, c], (x, y, 1 - c)) for p in couts], []

    return _Exchange(parts, [jax.ShapeDtypeStruct(p.shape, p.dtype) for p in parts], plan, na, 0,
                     aliases={a: a for a in range(na)})


def _allreduce_small(packed):
    rows = packed.shape[0]

    def body(sg_ref, res_ref, sib_ref, cs_ref, rem_ref, s1_send, s1_recv, s2_send, s2_recv):
        x, y, c, chip, peers = _place()
        cp = _remote(sg_ref, sib_ref, s1_send, s1_recv, (x, y, 1 - c))
        cp.start()
        cp.wait()
        cs_ref[...] = sg_ref[...] + sib_ref[...]
        rem_ref[chip] = cs_ref[...]
        cps = [_remote(cs_ref, rem_ref.at[chip], s2_send.at[k], s2_recv.at[chip], (px, py, c))
               for k, (px, py) in enumerate(peers)]
        for cp in cps:
            cp.start()
        for k, (px, py) in enumerate(peers):
            pchip = 2 * px + py
            _remote(cs_ref, rem_ref.at[pchip], s2_send.at[k], s2_recv.at[pchip], (px, py, c)).wait_recv()
        for cp in cps:
            cp.wait_send()
        res_ref[...] = ((rem_ref[0] + rem_ref[1]) + rem_ref[2]) + rem_ref[3]

    return pl.pallas_call(
        body,
        name="allreduce_small",
        in_specs=[VMEM_WHOLE],
        out_specs=VMEM_WHOLE,
        out_shape=jax.ShapeDtypeStruct((rows, GROUP_W), F32),
        scratch_shapes=[
            pltpu.VMEM((rows, GROUP_W), F32),
            pltpu.VMEM((rows, GROUP_W), F32),
            pltpu.VMEM((N_CHIPS, rows, GROUP_W), F32),
            pltpu.SemaphoreType.DMA,
            pltpu.SemaphoreType.DMA,
            pltpu.SemaphoreType.DMA((3,)),
            pltpu.SemaphoreType.DMA((N_CHIPS,)),
        ],
        compiler_params=pltpu.CompilerParams(vmem_limit_bytes=VMEM_LIMIT),
    )(packed)


def _add_halves(grad, other, core, name):
    n_shards, _, rows, cols = grad.shape
    tr = _tile(rows, max(16, (1 << 20) // (4 * cols) // 16 * 16), mult=16)

    def body(core_ref, g_ref, o_ref, out_ref):
        del core_ref
        out_ref[...] = (g_ref[...] + o_ref[...]).astype(BF16)

    return pl.pallas_call(
        body,
        name=name,
        grid_spec=pltpu.PrefetchScalarGridSpec(
            num_scalar_prefetch=1,
            grid=(n_shards, rows // tr),
            in_specs=[
                pl.BlockSpec((None, None, tr, cols), lambda s, i, core_ref: (s, core_ref[0], i, 0)),
                pl.BlockSpec((None, tr, cols), lambda s, i, core_ref: (s, i, 0)),
            ],
            out_specs=pl.BlockSpec((None, tr, cols), lambda s, i, core_ref: (s, i, 0)),
        ),
        out_shape=jax.ShapeDtypeStruct((n_shards, rows, cols), BF16),
        compiler_params=_params("arbitrary", "arbitrary"),
    )(core, grad, other)


def _inproj_fwd(x2, g_row, w_l, layer, ride=None):
    n = x2.shape[0]
    tm = _tile(n, 1024)

    def body(x_ref, g_ref, w_ref, p_ref, h_ref):
        @pl.when(pl.program_id(1) == 0)
        def _():
            xv = x_ref[...]
            r = lax.rsqrt(_lanemean(xv * xv) + RMS_EPS)
            h_ref[...] = (xv * r * g_ref[...]).astype(BF16)

        p_ref[...] = _dot(h_ref[...], w_ref[...]).astype(BF16)

    return _host_call(
        body,
        name=f"inproj_fwd_l{layer}",
        grid=(n // tm, N_COL_TILES),
        in_specs=[
            pl.BlockSpec((tm, D_MODEL), lambda i, j: (i, 0)),
            pl.BlockSpec((1, D_MODEL), lambda i, j: (0, 0)),
            pl.BlockSpec((None, D_MODEL, COL_TILE), lambda i, j: (j // 2, 0, j % 2)),
        ],
        out_specs=[
            pl.BlockSpec((tm, COL_TILE), lambda i, j: (i, j)),
            pl.BlockSpec((tm, D_MODEL), lambda i, j: (i, 0)),
        ],
        out_shape=[jax.ShapeDtypeStruct((n, IN_COLS), BF16), jax.ShapeDtypeStruct((n, D_MODEL), BF16)],
        args=[x2, g_row, w_l],
        ride=ride,
    )


def _layer_norm_parts(v, g, b):
    mu = _lanemean(v)
    d = v - mu
    rstd = lax.rsqrt(_lanemean(d * d) + LN_EPS)
    xh = d * rstd
    return xh, rstd, xh * g + b


def _window_sum(ref, slot, weight_row, n_taps, base, t_rows):
    total = None
    for b in range(8):
        group = None
        for a in range((base + n_taps - 1) // 8 + 1):
            k = 8 * a + b - base
            if 0 <= k < n_taps:
                term = weight_row(k) * ref[slot, 8 * a:8 * a + t_rows + 8, :]
                group = term if group is None else group + term
        if group is not None:
            part = group[b:b + t_rows, :]
            total = part if total is None else total + part
    return total


def _tap_grads(acc_ref, pad_ref, shift_ref, dy, ref, slot, n_taps, base, t_rows):
    pad_ref[0:8, :] = jnp.zeros((8, BRANCH_W), F32)
    pad_ref[8:8 + t_rows, :] = dy
    pad_ref[8 + t_rows:16 + t_rows, :] = jnp.zeros((8, BRANCH_W), F32)
    for b in range(8):
        taps = [(a, 8 * a + b - base) for a in range((base + n_taps - 1) // 8 + 1) if 0 <= 8 * a + b - base < n_taps]
        if not taps:
            continue
        shift_ref[...] = pad_ref[pl.ds(8 - b, t_rows + 8), :]
        for a, k in taps:
            prod = shift_ref[...] * ref[slot, 8 * a:8 * a + t_rows + 8, :]
            acc_ref[k] += prod.reshape((t_rows + 8) // 8, 8, BRANCH_W).sum(axis=0)


def _tril_mask():
    r = lax.broadcasted_iota(jnp.int32, (CHUNK, CHUNK), 0)
    c = lax.broadcasted_iota(jnp.int32, (CHUNK, CHUNK), 1)
    return r >= c


def _lanes(refs4):
    return jnp.concatenate([refs4[g] for g in range(N_CHIPS)], axis=1)


def _mixer_weight_specs():
    def whole(shape):
        nd = len(shape)
        return pl.BlockSpec(shape, lambda b, c: (0,) * nd)

    return [
        whole((N_GROUPS, GROUP_W, GROUP_W)),
        whole((1, BRANCH_W)),
        whole((N_CHIPS, CONV_ROWS, GROUP_W)),
        whole((1, BRANCH_W)),
        whole((1, BRANCH_W)),
        whole((1, BRANCH_W)),
        whole((1, BRANCH_W)),
        whole((1, BRANCH_W)),
        whole((N_GROUPS, CHUNK, CHUNK)),
        whole((CHUNK, BRANCH_W)),
        whole((N_CHIPS, SHORT_ROWS, GROUP_W)),
    ]


def _mixer_weight_args(mw):
    return [mw["pool_w"], mw["pool_scale"], mw["conv_w"], mw["conv_b"], mw["conv_ln_g"], mw["conv_ln_b"],
            mw["sgu_ln_g"], mw["sgu_ln_b"], mw["sgu_w"], mw["sgu_bias"], mw["sc_w"]]


def _mixers_fwd(proj3, mw, layer, ride=None):
    nb, seq, _ = proj3.shape
    t_rows = _tile(seq, 256)
    nc = seq // t_rows
    hb = t_rows // HALO

    def body(cur_ref, halo_ref, pw_ref, ps_ref, cw_ref, cb_ref, clg_ref, clb_ref, slg_ref, slb_ref, sw_ref,
             sbias_ref, scw_ref, z_ref, ext_ref):
        c = pl.program_id(1)
        has_prev = c > 0
        row = lax.broadcasted_iota(jnp.int32, (t_rows, 1), 0)
        tpos = (c * t_rows + row + 1).astype(F32)

        def cur(k):
            return cur_ref[:, k * BRANCH_W:(k + 1) * BRANCH_W].astype(F32)

        def hal(k):
            return halo_ref[:, k * BRANCH_W:(k + 1) * BRANCH_W].astype(F32)

        def put_ext(slot, halo_val, cur_val):
            ext_ref[slot, 0:HALO, :] = jnp.where(has_prev, halo_val, 0.0)
            ext_ref[slot, HALO:HALO + t_rows, :] = cur_val
            ext_ref[slot, HALO + t_rows:HALO + t_rows + 8, :] = jnp.zeros((8, BRANCH_W), F32)

        px = cur(P_X)
        put_ext(0, hal(P_X), px)
        mixed = []
        for j, win in enumerate(POOL_WINDOWS):
            cols = slice(j * GROUP_W, (j + 1) * GROUP_W)
            s = px[:, cols]
            for i in range(1, win):
                s = s + ext_ref[0, pl.ds(HALO - i, t_rows), cols]
            pooled = s / jnp.minimum(tpos, float(win)) - px[:, cols]
            mixed.append(_dot(pooled.astype(BF16), pw_ref[j]))
        z_pool = jnp.concatenate(mixed, axis=1) * ps_ref[...] * _silu(cur(P_GATE))
        z_ref[:, 0:BRANCH_W] = z_pool.astype(BF16)

        put_ext(1, hal(C_A) * _sigmoid(hal(C_B)), cur(C_A) * _sigmoid(cur(C_B)))
        cw = _lanes(cw_ref)
        acc = cb_ref[...] + _window_sum(ext_ref, 1, lambda k: cw[k:k + 1, :], CONV_K, HALO - (CONV_K - 1), t_rows)
        _, _, ln = _layer_norm_parts(acc, clg_ref[...], clb_ref[...])
        z_ref[:, BRANCH_W:2 * BRANCH_W] = (_silu(ln) * _silu(cur(C_GATE))).astype(BF16)

        _, _, v = _layer_norm_parts(cur(G_V), slg_ref[...], slb_ref[...])
        vb = v.astype(BF16)
        mask = _tril_mask()
        wt = [jnp.where(mask, sw_ref[g], 0.0).astype(BF16) for g in range(N_GROUPS)]
        sp_rows = []
        for sub in range(t_rows // CHUNK):
            rows = slice(sub * CHUNK, (sub + 1) * CHUNK)
            sp_rows.append(jnp.concatenate(
                [_dot(wt[g], vb[rows, g * GROUP_W:(g + 1) * GROUP_W]) for g in range(N_GROUPS)], axis=1)
                + sbias_ref[...])
        sp = jnp.concatenate(sp_rows, axis=0)
        z_ref[:, 2 * BRANCH_W:3 * BRANCH_W] = (cur(G_U) * sp * _silu(cur(G_GATE))).astype(BF16)

        put_ext(2, hal(S_C) * hal(S_X), cur(S_C) * cur(S_X))
        scw = _lanes(scw_ref)
        cv = _window_sum(ext_ref, 2, lambda k: scw[k:k + 1, :], SHORT_K, HALO - (SHORT_K - 1), t_rows)
        z_ref[:, 3 * BRANCH_W:4 * BRANCH_W] = (cur(S_B) * cv * _silu(cur(S_GATE))).astype(BF16)

    return _host_call(
        body,
        name=f"mixers_fwd_l{layer}",
        grid=(nb, nc),
        in_specs=[
            pl.BlockSpec((None, t_rows, PIECE_COLS), lambda b, c: (b, c, 0)),
            pl.BlockSpec((None, HALO, PIECE_COLS), lambda b, c: (b, jnp.maximum(c * hb - 1, 0), 0)),
        ] + _mixer_weight_specs(),
        out_specs=[pl.BlockSpec((None, t_rows, N_BRANCH * BRANCH_W), lambda b, c: (b, c, 0))],
        out_shape=[jax.ShapeDtypeStruct((nb, seq, N_BRANCH * BRANCH_W), BF16)],
        scratch_shapes=[pltpu.VMEM((3, HALO + t_rows + 8, BRANCH_W), F32)],
        args=[proj3, proj3, *_mixer_weight_args(mw)],
        ride=ride,
    )


MIXER_GRADS = ["pool_w", "pool_scale", "conv_w", "conv_b", "conv_ln_g", "conv_ln_b", "sgu_ln_g", "sgu_ln_b", "sgu_w",
               "sgu_b", "sc_w"]


def _mixers_bwd(proj3, dz3, dproj3, mw, layer, ride=None):
    nb, seq, _ = proj3.shape
    t_rows = _tile(seq, 256)
    nc = seq // t_rows
    hb = t_rows // HALO

    def body(cur_ref, halo_ref, dz_ref, pw_ref, ps_ref, cw_ref, cb_ref, clg_ref, clb_ref, slg_ref, slb_ref,
             sw_ref, sbias_ref, scw_ref, dp_in_ref, dp_ref, g_pw, g_ps, g_cw, g_cb, g_clg, g_clb, g_slg,
             g_slb, g_sw, g_sb, g_scw, ext_ref, nxt_ref, sb_acc, cw_acc, scw_acc, pad_ref, shift_ref):
        del dp_in_ref
        b = pl.program_id(0)
        r = pl.program_id(1)
        c = nc - 1 - r
        has_prev = c > 0
        row = lax.broadcasted_iota(jnp.int32, (t_rows, 1), 0)
        tpos = (c * t_rows + row + 1).astype(F32)

        @pl.when(jnp.logical_and(b == 0, r == 0))
        def _():
            for ref in (g_pw, g_ps, g_cb, g_clg, g_clb, g_slg, g_slb, g_sw, sb_acc, cw_acc, scw_acc):
                ref[...] = jnp.zeros_like(ref)

        @pl.when(r == 0)
        def _():
            nxt_ref[:, t_rows:t_rows + HALO, :] = jnp.zeros((3, HALO, BRANCH_W), F32)

        def cur(k):
            return cur_ref[:, k * BRANCH_W:(k + 1) * BRANCH_W].astype(F32)

        def hal(k):
            return halo_ref[:, k * BRANCH_W:(k + 1) * BRANCH_W].astype(F32)

        def dzp(k):
            return dz_ref[:, k * BRANCH_W:(k + 1) * BRANCH_W].astype(F32)

        def put_dp(k, val):
            dp_ref[:, k * BRANCH_W:(k + 1) * BRANCH_W] = val.astype(BF16)

        def put_ext(slot, halo_val, cur_val):
            ext_ref[slot, 0:HALO, :] = jnp.where(has_prev, halo_val, 0.0)
            ext_ref[slot, HALO:HALO + t_rows, :] = cur_val
            ext_ref[slot, HALO + t_rows:HALO + t_rows + 8, :] = jnp.zeros((8, BRANCH_W), F32)

        px = cur(P_X)
        put_ext(0, hal(P_X), px)
        pgate = cur(P_GATE)
        dz_pool = dzp(0)
        ps = ps_ref[...]
        pooled, mixed, cnts = [], [], []
        for j, win in enumerate(POOL_WINDOWS):
            cols = slice(j * GROUP_W, (j + 1) * GROUP_W)
            s = px[:, cols]
            for i in range(1, win):
                s = s + ext_ref[0, pl.ds(HALO - i, t_rows), cols]
            cnt = jnp.minimum(tpos, float(win))
            pj = s / cnt - px[:, cols]
            cnts.append(cnt)
            pooled.append(pj.astype(BF16))
            mixed.append(_dot(pooled[j], pw_ref[j]))
        mixed = jnp.concatenate(mixed, axis=1)
        put_dp(P_GATE, dz_pool * (mixed * ps) * _dsilu(pgate))
        d_out = dz_pool * _silu(pgate)
        g_ps[...] += _rowsum(d_out * mixed)
        d_mixed = (d_out * ps).astype(BF16)
        d_pooled = []
        for j in range(N_GROUPS):
            cols = slice(j * GROUP_W, (j + 1) * GROUP_W)
            g_pw[j] += _dot_tn(pooled[j], d_mixed[:, cols])
            dpj = _dot_nt(d_mixed[:, cols], pw_ref[j])
            d_pooled.append(dpj)
            nxt_ref[0, 0:t_rows, cols] = dpj / cnts[j]
        dpx = []
        for j, win in enumerate(POOL_WINDOWS):
            cols = slice(j * GROUP_W, (j + 1) * GROUP_W)
            s = nxt_ref[0, 0:t_rows, cols]
            for i in range(1, win):
                s = s + nxt_ref[0, pl.ds(i, t_rows), cols]
            dpx.append(s - d_pooled[j])
        put_dp(P_X, jnp.concatenate(dpx, axis=1))

        c_a = cur(C_A)
        sig_b = _sigmoid(cur(C_B))
        put_ext(1, hal(C_A) * _sigmoid(hal(C_B)), c_a * sig_b)
        cw = _lanes(cw_ref)
        acc = cb_ref[...] + _window_sum(ext_ref, 1, lambda k: cw[k:k + 1, :], CONV_K, HALO - (CONV_K - 1), t_rows)
        xh, rstd, ln = _layer_norm_parts(acc, clg_ref[...], clb_ref[...])
        cgate = cur(C_GATE)
        dz_conv = dzp(1)
        put_dp(C_GATE, dz_conv * _silu(ln) * _dsilu(cgate))
        d_ln = dz_conv * _silu(cgate) * _dsilu(ln)
        g_clg[...] += _rowsum(d_ln * xh)
        g_clb[...] += _rowsum(d_ln)
        dxh = d_ln * clg_ref[...]
        dc = rstd * (dxh - _lanemean(dxh) - xh * _lanemean(dxh * xh))
        g_cb[...] += _rowsum(dc)
        nxt_ref[1, 0:t_rows, :] = dc
        _tap_grads(cw_acc, pad_ref, shift_ref, dc, ext_ref, 1, CONV_K, HALO - (CONV_K - 1), t_rows)
        dyg = _window_sum(nxt_ref, 1, lambda i: cw[CONV_K - 1 - i:CONV_K - i, :], CONV_K, 0, t_rows)
        put_dp(C_A, dyg * sig_b)
        put_dp(C_B, dyg * c_a * sig_b * (1.0 - sig_b))

        u = cur(G_U)
        ggate = cur(G_GATE)
        vxh, vrstd, v = _layer_norm_parts(cur(G_V), slg_ref[...], slb_ref[...])
        vb = v.astype(BF16)
        mask = _tril_mask()
        wt = [jnp.where(mask, sw_ref[g], 0.0) for g in range(N_GROUPS)]
        wt_b = [w.astype(BF16) for w in wt]
        wtt_b = [w.T.astype(BF16) for w in wt]
        dz_sgu = dzp(2)
        d_sgu = dz_sgu * _silu(ggate)
        d_sp = d_sgu * u
        d_spb = d_sp.astype(BF16)
        sp_rows, dv_rows = [], []
        for sub in range(t_rows // CHUNK):
            rows = slice(sub * CHUNK, (sub + 1) * CHUNK)
            sp_g, dv_g = [], []
            for g in range(N_GROUPS):
                cols = slice(g * GROUP_W, (g + 1) * GROUP_W)
                sp_g.append(_dot(wt_b[g], vb[rows, cols]))
                g_sw[g] += jnp.where(mask, _dot_nt(d_spb[rows, cols], vb[rows, cols]), 0.0)
                dv_g.append(_dot(wtt_b[g], d_spb[rows, cols]))
            sp_rows.append(jnp.concatenate(sp_g, axis=1) + sbias_ref[...])
            dv_rows.append(jnp.concatenate(dv_g, axis=1))
            sb_acc[...] += d_sp[rows, :]
        sp = jnp.concatenate(sp_rows, axis=0)
        dv = jnp.concatenate(dv_rows, axis=0)
        put_dp(G_GATE, dz_sgu * (u * sp) * _dsilu(ggate))
        put_dp(G_U, d_sgu * sp)
        g_slg[...] += _rowsum(dv * vxh)
        g_slb[...] += _rowsum(dv)
        dvx = dv * slg_ref[...]
        put_dp(G_V, vrstd * (dvx - _lanemean(dvx) - vxh * _lanemean(dvx * vxh)))

        s_b, s_c, s_x, sgate = cur(S_B), cur(S_C), cur(S_X), cur(S_GATE)
        put_ext(2, hal(S_C) * hal(S_X), s_c * s_x)
        scw = _lanes(scw_ref)
        cv = _window_sum(ext_ref, 2, lambda k: scw[k:k + 1, :], SHORT_K, HALO - (SHORT_K - 1), t_rows)
        dz_sc = dzp(3)
        put_dp(S_GATE, dz_sc * (s_b * cv) * _dsilu(sgate))
        d_pre = dz_sc * _silu(sgate)
        put_dp(S_B, d_pre * cv)
        dcv = d_pre * s_b
        nxt_ref[2, 0:t_rows, :] = dcv
        _tap_grads(scw_acc, pad_ref, shift_ref, dcv, ext_ref, 2, SHORT_K, HALO - (SHORT_K - 1), t_rows)
        du = _window_sum(nxt_ref, 2, lambda i: scw[SHORT_K - 1 - i:SHORT_K - i, :], SHORT_K, 0, t_rows)
        put_dp(S_C, du * s_x)
        put_dp(S_X, du * s_c)

        nxt_ref[:, t_rows:t_rows + HALO, :] = nxt_ref[:, 0:HALO, :]

        @pl.when(jnp.logical_and(b == nb - 1, r == nc - 1))
        def _():
            lane = lax.broadcasted_iota(jnp.int32, (CHUNK, GROUP_W), 1)
            out = jnp.zeros((CHUNK, GROUP_W), F32)
            for g in range(N_GROUPS):
                col = jnp.sum(sb_acc[:, g * GROUP_W:(g + 1) * GROUP_W], axis=1, keepdims=True)
                out = jnp.where(lane == g, col, out)
            g_sb[...] = out
            g_cw[...] = jnp.sum(cw_acc[...], axis=1)
            g_scw[...] = jnp.sum(scw_acc[...], axis=1)

    def acc_spec(shape):
        nd = len(shape)
        return pl.BlockSpec(shape, lambda b, r: (0,) * nd)

    acc_shapes = [
        (N_GROUPS, GROUP_W, GROUP_W),
        (1, BRANCH_W),
        (CONV_ROWS, BRANCH_W),
        (1, BRANCH_W),
        (1, BRANCH_W),
        (1, BRANCH_W),
        (1, BRANCH_W),
        (1, BRANCH_W),
        (N_GROUPS, CHUNK, CHUNK),
        (CHUNK, GROUP_W),
        (SHORT_ROWS, BRANCH_W),
    ]
    outs, rode = _host_call(
        body,
        name=f"mixers_bwd_l{layer}",
        grid=(nb, nc),
        in_specs=[
            pl.BlockSpec((None, t_rows, PIECE_COLS), lambda b, r: (b, nc - 1 - r, 0)),
            pl.BlockSpec((None, HALO, PIECE_COLS), lambda b, r: (b, jnp.maximum((nc - 1 - r) * hb - 1, 0), 0)),
            pl.BlockSpec((None, t_rows, N_BRANCH * BRANCH_W), lambda b, r: (b, nc - 1 - r, 0)),
        ] + _mixer_weight_specs() + [ANY],
        out_specs=[pl.BlockSpec((None, t_rows, PIECE_COLS), lambda b, r: (b, nc - 1 - r, 0))]
        + [acc_spec(s) for s in acc_shapes],
        out_shape=[jax.ShapeDtypeStruct(dproj3.shape, BF16)] + [jax.ShapeDtypeStruct(s, F32) for s in acc_shapes],
        scratch_shapes=[
            pltpu.VMEM((3, HALO + t_rows + 8, BRANCH_W), F32),
            pltpu.VMEM((3, t_rows + HALO, BRANCH_W), F32),
            pltpu.VMEM((CHUNK, BRANCH_W), F32),
            pltpu.VMEM((CONV_ROWS, 8, BRANCH_W), F32),
            pltpu.VMEM((SHORT_ROWS, 8, BRANCH_W), F32),
            pltpu.VMEM((t_rows + 16, BRANCH_W), F32),
            pltpu.VMEM((t_rows + 8, BRANCH_W), F32),
        ],
        aliases={3 + 11: 0},
        args=[proj3, proj3, dz3, *_mixer_weight_args(mw), dproj3],
        ride=ride,
    )
    return outs[0], dict(zip(MIXER_GRADS, outs[1:])), rode


def _merge_fwd(z2, proj2, x2, wbr_l, wo_l, layer, ride=None):
    n = x2.shape[0]
    tm = _tile(n, 512)

    def body(z_ref, gate_ref, x_ref, wbr_ref, wo_ref, xn_ref, mg_ref, acc_ref):
        nbr = pl.program_id(1)
        zt = z_ref[...]
        bo = jnp.concatenate([_dot(zt, wbr_ref[s]) for s in range(N_CHIPS)], axis=1)
        contrib = _sigmoid(gate_ref[...].astype(F32)) * bo

        @pl.when(nbr == 0)
        def _():
            acc_ref[...] = contrib

        @pl.when(nbr > 0)
        def _():
            acc_ref[...] += contrib

        @pl.when(nbr == N_BRANCH - 1)
        def _():
            mg = acc_ref[...].astype(BF16)
            mg_ref[...] = mg
            xn_ref[...] = x_ref[...] + _dot(mg, wo_ref[...])

    return _host_call(
        body,
        name=f"merge_fwd_l{layer}",
        grid=(n // tm, N_BRANCH),
        in_specs=[
            pl.BlockSpec((tm, BRANCH_W), lambda i, b: (i, b)),
            pl.BlockSpec((tm, D_MODEL), lambda i, b: (i, GATE_BLOCK0 + b)),
            pl.BlockSpec((tm, D_MODEL), lambda i, b: (i, 0)),
            pl.BlockSpec((N_CHIPS, None, BRANCH_W, BR_SHARD), lambda i, b: (0, b, 0, 0)),
            pl.BlockSpec((D_MODEL, D_MODEL), lambda i, b: (0, 0)),
        ],
        out_specs=[
            pl.BlockSpec((tm, D_MODEL), lambda i, b: (i, 0)),
            pl.BlockSpec((tm, D_MODEL), lambda i, b: (i, 0)),
        ],
        out_shape=[jax.ShapeDtypeStruct((n, D_MODEL), F32), jax.ShapeDtypeStruct((n, D_MODEL), BF16)],
        scratch_shapes=[pltpu.VMEM((tm, D_MODEL), F32)],
        args=[z2, proj2, x2, wbr_l, wo_l],
        ride=ride,
    )


def _merge_bwd(dout2, merged2, z2, proj2, wbr_l, wo_l, layer, ride=None):
    n = dout2.shape[0]
    tm = _tile(n, 512)
    nt = n // tm

    def body(do_ref, mg_ref, z_ref, gate_ref, wbr_ref, wo_ref, dz_ref, dg_ref, gwo_ref, gwbr_ref, dm_ref,
             awo_ref, awbr_ref):
        i = pl.program_id(0)
        nbr = pl.program_id(1)

        @pl.when(nbr == 0)
        def _():
            do_b = do_ref[...].astype(BF16)
            dm_ref[...] = _dot_nt(do_b, wo_ref[...])
            gw = _dot_tn(mg_ref[...], do_b)

            @pl.when(i == 0)
            def _():
                awo_ref[...] = gw

            @pl.when(i > 0)
            def _():
                awo_ref[...] += gw

        zt = z_ref[...]
        bo = jnp.concatenate([_dot(zt, wbr_ref[s]) for s in range(N_CHIPS)], axis=1)
        gt = _sigmoid(gate_ref[...].astype(F32))
        dm = dm_ref[...]
        dbo = (dm * gt).astype(BF16)
        dg_ref[...] = (dm * bo * gt * (1.0 - gt)).astype(BF16)
        dzt = jnp.zeros((tm, BRANCH_W), F32)
        for s in range(N_CHIPS):
            dbo_s = dbo[:, s * BR_SHARD:(s + 1) * BR_SHARD]
            dzt = dzt + _dot_nt(dbo_s, wbr_ref[s])
            gw = _dot_tn(zt, dbo_s)

            @pl.when(i == 0)
            def _():
                awbr_ref[s, nbr] = gw

            @pl.when(i > 0)
            def _():
                awbr_ref[s, nbr] += gw

        dz_ref[...] = dzt.astype(BF16)

        @pl.when(jnp.logical_and(i == nt - 1, nbr == N_BRANCH - 1))
        def _():
            pltpu.sync_copy(awo_ref, gwo_ref)
            pltpu.sync_copy(awbr_ref, gwbr_ref)

    return _host_call(
        body,
        name=f"merge_bwd_l{layer}",
        grid=(nt, N_BRANCH),
        in_specs=[
            pl.BlockSpec((tm, D_MODEL), lambda i, b: (i, 0)),
            pl.BlockSpec((tm, D_MODEL), lambda i, b: (i, 0)),
            pl.BlockSpec((tm, BRANCH_W), lambda i, b: (i, b)),
            pl.BlockSpec((tm, D_MODEL), lambda i, b: (i, GATE_BLOCK0 + b)),
            pl.BlockSpec((N_CHIPS, None, BRANCH_W, BR_SHARD), lambda i, b: (0, b, 0, 0)),
            pl.BlockSpec((D_MODEL, D_MODEL), lambda i, b: (0, 0)),
        ],
        out_specs=[
            pl.BlockSpec((tm, BRANCH_W), lambda i, b: (i, b)),
            pl.BlockSpec((tm, D_MODEL), lambda i, b: (i, GATE_BLOCK0 + b)),
            ANY,
            ANY,
        ],
        out_shape=[
            jax.ShapeDtypeStruct((n, N_BRANCH * BRANCH_W), BF16),
            jax.ShapeDtypeStruct((n, IN_COLS), BF16),
            jax.ShapeDtypeStruct((D_MODEL, D_MODEL), F32),
            jax.ShapeDtypeStruct((N_CHIPS, N_BRANCH, BRANCH_W, BR_SHARD), F32),
        ],
        scratch_shapes=[
            pltpu.VMEM((tm, D_MODEL), F32),
            pltpu.VMEM((D_MODEL, D_MODEL), F32),
            pltpu.VMEM((N_CHIPS, N_BRANCH, BRANCH_W, BR_SHARD), F32),
        ],
        args=[dout2, merged2, z2, proj2, wbr_l, wo_l],
        ride=ride,
    )


def _loss_head(x2, g_row, tgt2):
    n = x2.shape[0]
    tm = _tile(n, 512)

    def body(x_ref, g_ref, t_ref, dx_ref, loss_ref, dg_ref):
        @pl.when(pl.program_id(0) == 0)
        def _():
            loss_ref[...] = jnp.zeros_like(loss_ref)
            dg_ref[...] = jnp.zeros_like(dg_ref)

        xv = x_ref[...]
        g = g_ref[...]
        r = lax.rsqrt(_lanemean(xv * xv) + RMS_EPS)
        xh = xv * r
        err = xh * g - t_ref[...]
        loss_ref[...] += 0.5 * jnp.sum(_lanemean(err * err), axis=0, keepdims=True)
        dy = err * (1.0 / D_MODEL)
        dg_ref[...] += _rowsum(dy * xh)
        dxh = dy * g
        dx_ref[...] = r * (dxh - xh * _lanemean(dxh * xh))

    return pl.pallas_call(
        body,
        name="loss_head",
        grid=(n // tm,),
        in_specs=[
            pl.BlockSpec((tm, D_MODEL), lambda i: (i, 0)),
            pl.BlockSpec((1, D_MODEL), lambda i: (0, 0)),
            pl.BlockSpec((tm, D_MODEL), lambda i: (i, 0)),
        ],
        out_specs=[
            pl.BlockSpec((tm, D_MODEL), lambda i: (i, 0)),
            pl.BlockSpec((1, GROUP_W), lambda i: (0, 0)),
            pl.BlockSpec((1, D_MODEL), lambda i: (0, 0)),
        ],
        out_shape=[jax.ShapeDtypeStruct((n, D_MODEL), F32), jax.ShapeDtypeStruct((1, GROUP_W), F32),
                   jax.ShapeDtypeStruct((1, D_MODEL), F32)],
        compiler_params=_params("arbitrary"),
    )(x2, g_row, tgt2)


def _inproj_bwd_x(dproj2, w_l, x2, g_row, dout2, layer, ride=None):
    n = x2.shape[0]
    tm = _tile(n, 1024)

    def body(dp_ref, w_ref, x_ref, g_ref, do_ref, dx_ref, dng_ref, dh_ref):
        i = pl.program_id(0)
        s = pl.program_id(1)
        part = _dot_nt(dp_ref[...], w_ref[...])

        @pl.when(s == 0)
        def _():
            dh_ref[...] = part

        @pl.when(s > 0)
        def _():
            dh_ref[...] += part

        @pl.when(jnp.logical_and(i == 0, s == 0))
        def _():
            dng_ref[...] = jnp.zeros_like(dng_ref)

        @pl.when(s == N_COL_TILES - 1)
        def _():
            xv = x_ref[...]
            r = lax.rsqrt(_lanemean(xv * xv) + RMS_EPS)
            xh = xv * r
            dh = dh_ref[...]
            dng_ref[...] += _rowsum(dh * xh)
            dxh = dh * g_ref[...]
            dx_ref[...] = do_ref[...] + r * (dxh - xh * _lanemean(dxh * xh))

    return _host_call(
        body,
        name=f"inproj_bwd_x_l{layer}",
        grid=(n // tm, N_COL_TILES),
        in_specs=[
            pl.BlockSpec((tm, COL_TILE), lambda i, s: (i, s)),
            pl.BlockSpec((None, D_MODEL, COL_TILE), lambda i, s: (s // 2, 0, s % 2)),
            pl.BlockSpec((tm, D_MODEL), lambda i, s: (i, 0)),
            pl.BlockSpec((1, D_MODEL), lambda i, s: (0, 0)),
            pl.BlockSpec((tm, D_MODEL), lambda i, s: (i, 0)),
        ],
        out_specs=[
            pl.BlockSpec((tm, D_MODEL), lambda i, s: (i, 0)),
            pl.BlockSpec((1, D_MODEL), lambda i, s: (0, 0)),
        ],
        out_shape=[jax.ShapeDtypeStruct((n, D_MODEL), F32), jax.ShapeDtypeStruct((1, D_MODEL), F32)],
        scratch_shapes=[pltpu.VMEM((tm, D_MODEL), F32)],
        args=[dproj2, w_l, x2, g_row, dout2],
        ride=ride,
    )


def _inproj_bwd_w(h2, dproj2, layer):
    n = h2.shape[0]
    tm = _tile(n, 1024)

    def body(h_ref, dp_ref, gw_ref):
        gw = _dot_tn(h_ref[...], dp_ref[...])

        @pl.when(pl.program_id(1) == 0)
        def _():
            gw_ref[...] = gw

        @pl.when(pl.program_id(1) > 0)
        def _():
            gw_ref[...] += gw

    outs, _ = _host_call(
        body,
        name=f"inproj_bwd_w_l{layer}",
        grid=(N_COL_TILES, n // tm),
        in_specs=[
            pl.BlockSpec((tm, D_MODEL), lambda s, i: (i, 0)),
            pl.BlockSpec((tm, COL_TILE), lambda s, i: (i, s)),
        ],
        out_specs=[pl.BlockSpec((None, D_MODEL, COL_TILE), lambda s, i: (s // 2, 0, s % 2))],
        out_shape=[jax.ShapeDtypeStruct((N_CHIPS, D_MODEL, SHARD_COLS), F32)],
        args=[h2, dproj2],
    )
    return outs[0]


def _adamw_math(w, g, m, v):
    m = ADAM_B1 * m + (1.0 - ADAM_B1) * g
    v = ADAM_B2 * v + (1.0 - ADAM_B2) * jnp.square(g)
    m_hat = m / (1.0 - ADAM_B1 ** ADAM_STEP)
    v_hat = v / (1.0 - ADAM_B2 ** ADAM_STEP)
    delta = -ADAM_LR * (m_hat / (jnp.sqrt(v_hat) + ADAM_EPS) + ADAM_WD * w)
    return delta, m, v


def _adamw_sharded(w, m, v, parts, name):
    depth, rows, cols = w.shape
    tr = _tile(rows, max(16, (1 << 19) // (4 * cols) // 16 * 16), mult=16)

    def body(w_ref, m_ref, v_ref, p0_ref, p1_ref, g_out, d_out, m_out, v_out):
        def update(p_ref):
            g = ((p_ref[0].astype(F32) + p_ref[1].astype(F32)) + p_ref[2].astype(F32)) + p_ref[3].astype(F32)
            delta, m2, v2 = _adamw_math(w_ref[...], g, m_ref[...], v_ref[...])
            g_out[...] = g
            d_out[...] = delta
            m_out[...] = m2
            v_out[...] = v2

        @pl.when(pl.program_id(0) == 0)
        def _():
            update(p0_ref)

        @pl.when(pl.program_id(0) == 1)
        def _():
            update(p1_ref)

    spec = pl.BlockSpec((None, tr, cols), lambda l, i: (l, i, 0))

    def part_spec(layer):
        return pl.BlockSpec((N_CHIPS, tr, cols), lambda l, i: (0, jnp.where(l == layer, i, 0), 0))

    return pl.pallas_call(
        body,
        name=f"adamw_{name}",
        grid=(depth, rows // tr),
        in_specs=[spec, spec, spec, part_spec(0), part_spec(1)],
        out_specs=[spec] * 4,
        out_shape=[jax.ShapeDtypeStruct(w.shape, F32)] * 4,
        compiler_params=_params("arbitrary", "arbitrary"),
    )(w, m, v, parts[0], parts[1])


def _adamw_packed(w, m, v, g):
    rows = w.shape[0]
    tr = _tile(rows, rows // 2 if rows % 16 == 0 else rows)

    def body(w_ref, m_ref, v_ref, g_ref, d_out, m_out, v_out):
        delta, m2, v2 = _adamw_math(w_ref[...], g_ref[...], m_ref[...], v_ref[...])
        d_out[...] = delta
        m_out[...] = m2
        v_out[...] = v2

    spec = pl.BlockSpec((tr, GROUP_W), lambda i: (i, 0))
    return pl.pallas_call(
        body,
        name="adamw_small",
        grid=(rows // tr,),
        in_specs=[spec] * 4,
        out_specs=[spec] * 3,
        out_shape=[jax.ShapeDtypeStruct(w.shape, F32)] * 3,
        compiler_params=_params("arbitrary"),
    )(w, m, v, g)


SMALL = ["norm_g", "pool_w", "pool_scale", "conv_b", "conv_ln_g", "conv_ln_b", "sgu_ln_g", "sgu_ln_b", "sgu_w",
         "sgu_b", "final_g"]
WEIGHTS = ["norm_g", "w_in", "pool_w", "pool_scale", "conv_w", "conv_b", "conv_ln_g", "conv_ln_b", "sgu_ln_g",
           "sgu_ln_b", "sgu_w", "sgu_b", "sc_w", "w_branch", "w_o", "final_g"]
HALF_SHAPES = [(D_MODEL // 2, SHARD_COLS), (N_BRANCH * BRANCH_W // 2, BR_SHARD), (BR_SHARD // 2, D_MODEL),
               (CS_ROWS // 2, GROUP_W)]


def _pack_small(tree):
    return jnp.concatenate([tree[k].reshape(-1, GROUP_W) for k in SMALL], axis=0)


def _unpack_small(packed, like):
    out, r = {}, 0
    for k in SMALL:
        nr = like[k].size // GROUP_W
        out[k] = packed[r:r + nr].reshape(like[k].shape)
        r += nr
    return out


def _pad_rows(a, rows):
    pad = [(0, 0)] * a.ndim
    pad[-2] = (0, rows - a.shape[-2])
    return jnp.pad(a, pad)


def _pack_cs(conv, short):
    return jnp.concatenate([_pad_rows(conv, CS_ROWS // 2), _pad_rows(short, CS_ROWS // 2)], axis=-2)


def _shard_major_rows(a):
    return a.reshape(a.shape[0], N_CHIPS, GROUP_W).transpose(1, 0, 2)


def kernel(x, norm_g, w_in, pool_w, pool_scale, conv_w, conv_b, conv_ln_g, conv_ln_b, sgu_ln_g, sgu_ln_b, sgu_w, sgu_b, sc_w, w_branch, w_o, final_g, loss_target, m_norm_g, m_w_in, m_pool_w, m_pool_scale, m_conv_w, m_conv_b, m_conv_ln_g, m_conv_ln_b, m_sgu_ln_g, m_sgu_ln_b, m_sgu_w, m_sgu_b, m_sc_w, m_w_branch, m_w_o, m_final_g, v_norm_g, v_w_in, v_pool_w, v_pool_scale, v_conv_w, v_conv_b, v_conv_ln_g, v_conv_ln_b, v_sgu_ln_g, v_sgu_ln_b, v_sgu_w, v_sgu_b, v_sc_w, v_w_branch, v_w_o, v_final_g):
    w = dict(norm_g=norm_g, w_in=w_in, pool_w=pool_w, pool_scale=pool_scale, conv_w=conv_w, conv_b=conv_b,
             conv_ln_g=conv_ln_g, conv_ln_b=conv_ln_b, sgu_ln_g=sgu_ln_g, sgu_ln_b=sgu_ln_b, sgu_w=sgu_w,
             sgu_b=sgu_b, sc_w=sc_w, w_branch=w_branch, w_o=w_o, final_g=final_g)
    mom = dict(norm_g=m_norm_g, w_in=m_w_in, pool_w=m_pool_w, pool_scale=m_pool_scale, conv_w=m_conv_w,
               conv_b=m_conv_b, conv_ln_g=m_conv_ln_g, conv_ln_b=m_conv_ln_b, sgu_ln_g=m_sgu_ln_g,
               sgu_ln_b=m_sgu_ln_b, sgu_w=m_sgu_w, sgu_b=m_sgu_b, sc_w=m_sc_w, w_branch=m_w_branch, w_o=m_w_o,
               final_g=m_final_g)
    var = dict(norm_g=v_norm_g, w_in=v_w_in, pool_w=v_pool_w, pool_scale=v_pool_scale, conv_w=v_conv_w,
               conv_b=v_conv_b, conv_ln_g=v_conv_ln_g, conv_ln_b=v_conv_ln_b, sgu_ln_g=v_sgu_ln_g,
               sgu_ln_b=v_sgu_ln_b, sgu_w=v_sgu_w, sgu_b=v_sgu_b, sc_w=v_sc_w, w_branch=v_w_branch, w_o=v_w_o,
               final_g=v_final_g)

    nb, seq, _ = x.shape
    n = nb * seq
    core = lax.axis_index("c").astype(jnp.int32).reshape(1)

    win_s = w_in.astype(BF16).reshape(DEPTH, 2, D_MODEL // 2, SHARD_COLS)
    wbr_s = w_branch.astype(BF16)
    wo_s = w_o.astype(BF16)
    cs_s = _pack_cs(conv_w, sc_w)

    def halves(a, shape):
        return a.reshape((DEPTH, 2) + shape)

    gathered0 = _allgather_layer0([win_s] + [halves(a, s) for a, s in zip((wbr_s, wo_s, cs_s), HALF_SHAPES[1:])])

    def layer_weights(win_f, wbr_f, wo_f, cs_f):
        cs_f = cs_f.reshape(N_CHIPS, CS_ROWS, GROUP_W)
        return dict(win=win_f.reshape(N_CHIPS, D_MODEL, SHARD_COLS),
                    wbr=wbr_f.reshape(N_CHIPS, N_BRANCH, BRANCH_W, BR_SHARD),
                    wo=wo_f.reshape(D_MODEL, D_MODEL),
                    conv_w=cs_f[:, :CONV_ROWS], sc_w=cs_f[:, CS_ROWS // 2:CS_ROWS // 2 + SHORT_ROWS])

    def mixer_weights(l, gathered):
        row = lambda a: a[l].reshape(1, BRANCH_W)
        bias = jnp.repeat(jnp.swapaxes(sgu_b[l], 0, 1), GROUP_W, axis=1)
        return dict(pool_w=pool_w[l].astype(BF16), pool_scale=row(pool_scale), conv_w=gathered["conv_w"],
                    conv_b=row(conv_b), conv_ln_g=row(conv_ln_g), conv_ln_b=row(conv_ln_b), sgu_ln_g=row(sgu_ln_g),
                    sgu_ln_b=row(sgu_ln_b), sgu_w=sgu_w[l], sgu_bias=bias, sc_w=gathered["sc_w"])

    lw = [layer_weights(*gathered0), None]
    mw = [mixer_weights(0, lw[0]), None]

    xs, projs, hs, zs, mgs = [x.reshape(n, D_MODEL)], [], [], [], []
    for l in range(DEPTH):
        first = l == 0
        (proj, h), rode = _inproj_fwd(xs[l], norm_g[l:l + 1], lw[l]["win"], l,
                                      ride=_gather_layer1_first_half(win_s) if first else None)
        if first:
            win_f1 = rode[0]
        (z3,), rode = _mixers_fwd(proj.reshape(nb, seq, IN_COLS), mw[l], l,
                                  ride=_gather_layer1_second_half(win_s, win_f1) if first else None)
        if first:
            win_f1 = rode[0]
        z = z3.reshape(n, N_BRANCH * BRANCH_W)
        (x_new, merged), rode = _merge_fwd(z, proj, xs[l], lw[l]["wbr"], lw[l]["wo"], l,
                                           ride=_gather_layer1_rest([wbr_s, wo_s, cs_s]) if first else None)
        if first:
            lw[1] = layer_weights(win_f1, *rode)
            mw[1] = mixer_weights(1, lw[1])
        projs.append(proj)
        hs.append(h)
        zs.append(z)
        mgs.append(merged)
        xs.append(x_new)

    dx, loss_part, g_final = _loss_head(xs[DEPTH], final_g.reshape(1, D_MODEL), loss_target.reshape(n, D_MODEL))
    loss = lax.psum(loss_part[0, 0], ("x", "y", "c"))

    def as_halves(gwin, gwbr, gwo, gcs):
        flat = [gwin, gwbr, gwo, gcs]
        return [g.reshape((N_CHIPS, 2) + s) for g, s in zip(flat, HALF_SHAPES)]

    def add_halves(grads, others, l):
        return [_add_halves(g, o, core, f"add_halves_l{l}_{i}") for i, (g, o) in enumerate(zip(grads, others))]

    per_layer, parts = {}, [None] * DEPTH
    sums1 = scattered1 = None
    for l in reversed(range(DEPTH)):
        last = l == 0
        (dz, dproj, gwo, gwbr), rode = _merge_bwd(dx, mgs[l], zs[l], projs[l], lw[l]["wbr"], lw[l]["wo"], l,
                                                  ride=_scatter_chip_sums(sums1) if last else None)
        if last:
            scattered1 = rode
        dproj3, gm, rode = _mixers_bwd(projs[l].reshape(nb, seq, IN_COLS), dz.reshape(nb, seq, N_BRANCH * BRANCH_W),
                                       dproj.reshape(nb, seq, IN_COLS), mw[l], l,
                                       ride=_share_halves(scattered1) if last else None)
        if last:
            parts[1] = rode
        dproj = dproj3.reshape(n, IN_COLS)
        gwin = _inproj_bwd_w(hs[l], dproj, l)
        gcs = _pack_cs(_shard_major_rows(gm["conv_w"]), _shard_major_rows(gm["sc_w"]))
        grads_l = as_halves(gwin, gwbr, gwo, gcs)
        (dx, g_norm), others = _inproj_bwd_x(dproj, lw[l]["win"], xs[l], norm_g[l:l + 1], dx, l,
                                             ride=_swap_halves(grads_l))
        gm["norm_g"] = g_norm
        per_layer[l] = gm
        sums = add_halves(grads_l, others, l)
        if last:
            scattered0 = _exchange_call("scatter_chip_sums_l0", _scatter_chip_sums(sums))
            parts[0] = _exchange_call("share_halves_l0", _share_halves(scattered0))
        else:
            sums1 = sums
    grad_x = dx.reshape(nb, seq, D_MODEL)

    grads, deltas, new_m, new_v = {}, {}, {}, {}

    def flat(a, i):
        rows, cols = 2 * HALF_SHAPES[i][0], HALF_SHAPES[i][1]
        return a.reshape(a.shape[0], rows, cols)

    def sharded_update(i, name, tree_of):
        res = _adamw_sharded(flat(tree_of(w), i), flat(tree_of(mom), i), flat(tree_of(var), i),
                             [flat(parts[l][i], i) for l in range(DEPTH)], name)
        return res

    for i, name in enumerate(["w_in", "w_branch", "w_o"]):
        res = sharded_update(i, name, lambda t: t[name])
        for tree, r in zip((grads, deltas, new_m, new_v), res):
            tree[name] = r.reshape(w[name].shape)
    res = sharded_update(3, "conv_sc", lambda t: _pack_cs(t["conv_w"], t["sc_w"]))
    for tree, r in zip((grads, deltas, new_m, new_v), res):
        tree["conv_w"] = r[:, :CONV_K]
        tree["sc_w"] = r[:, CS_ROWS // 2:CS_ROWS // 2 + SHORT_K]

    def stack(k, shape):
        return jnp.stack([per_layer[l][k] for l in range(DEPTH)]).reshape(shape)

    g_small = {"final_g": g_final.reshape(D_MODEL), "norm_g": stack("norm_g", (DEPTH, D_MODEL)),
               "pool_w": stack("pool_w", pool_w.shape), "sgu_w": stack("sgu_w", sgu_w.shape),
               "sgu_b": jnp.swapaxes(stack("sgu_b", (DEPTH, CHUNK, GROUP_W))[:, :, :N_GROUPS], 1, 2)}
    for k in ("pool_scale", "conv_b", "conv_ln_g", "conv_ln_b", "sgu_ln_g", "sgu_ln_b"):
        g_small[k] = stack(k, (DEPTH, BRANCH_W))
    g_packed = _allreduce_small(_pack_small(g_small))
    d_packed, m_packed, v_packed = _adamw_packed(_pack_small(w), _pack_small(mom), _pack_small(var), g_packed)
    for tree, packed in ((grads, g_packed), (deltas, d_packed), (new_m, m_packed), (new_v, v_packed)):
        tree.update(_unpack_small(packed, w))

    return (loss, grad_x, *[grads[k] for k in WEIGHTS], *[deltas[k] for k in WEIGHTS],
            *[new_m[k] for k in WEIGHTS], *[new_v[k] for k in WEIGHTS])
```

```python
import functools

import jax
import jax.numpy as jnp
from jax import lax
from jax.experimental import pallas as pl
from jax.experimental.pallas import tpu as pltpu

F32 = jnp.float32
BF16 = jnp.bfloat16

D_MODEL = 1024
DEPTH = 2
N_BRANCH = 4
BRANCH_W = 512
N_GROUPS = 4
GROUP_W = 128
POOL_WINDOWS = (2, 4, 8, 16)
CONV_K = 31
SHORT_K = 3
CHUNK = 128
N_PIECES = 12
PIECE_COLS = N_PIECES * BRANCH_W
IN_COLS = PIECE_COLS + N_BRANCH * D_MODEL
N_CHIPS = 4
SHARD_COLS = IN_COLS // N_CHIPS
BR_SHARD = D_MODEL // N_CHIPS
COL_TILE = SHARD_COLS // 2
N_COL_TILES = IN_COLS // COL_TILE
GATE_BLOCK0 = PIECE_COLS // D_MODEL
RMS_EPS = 1e-6
LN_EPS = 1e-5
HALO = 32
CONV_ROWS = 32
SHORT_ROWS = 8
CS_ROWS = 64

ADAM_LR = 0.001
ADAM_B1 = 0.9
ADAM_B2 = 0.999
ADAM_EPS = 1e-08
ADAM_WD = 0.01
ADAM_STEP = 10

VMEM_LIMIT = 52 * 1024 * 1024
MESH_ID = pl.DeviceIdType.MESH
ANY = pl.BlockSpec(memory_space=pl.ANY)
VMEM_WHOLE = pl.BlockSpec(memory_space=pltpu.VMEM)

(P_X, P_GATE, C_A, C_B, C_GATE, G_U, G_V, G_GATE, S_B, S_C, S_X, S_GATE) = range(N_PIECES)


def _params(*sem):
    return pltpu.CompilerParams(dimension_semantics=sem, vmem_limit_bytes=VMEM_LIMIT)


def _sigmoid(v):
    return 0.5 * jnp.tanh(0.5 * v) + 0.5


def _silu(v):
    return v * _sigmoid(v)


def _dsilu(v):
    s = _sigmoid(v)
    return s * (1.0 + v * (1.0 - s))


def _dot(a, b):
    return jnp.dot(a, b, preferred_element_type=F32)


def _dot_nt(a, b):
    return lax.dot_general(a, b, (((1,), (1,)), ((), ())), preferred_element_type=F32)


def _dot_tn(a, b):
    return lax.dot_general(a, b, (((0,), (0,)), ((), ())), preferred_element_type=F32)


def _rowsum(v):
    return jnp.sum(v, axis=0, keepdims=True)


def _lanemean(v):
    return jnp.mean(v, axis=-1, keepdims=True)


def _tile(n, want, mult=8):
    t = max(1, min(n, want))
    while n % t or (t % mult and t != n):
        t -= 1
    return t


def _place():
    x, y, c = lax.axis_index("x"), lax.axis_index("y"), lax.axis_index("c")
    chip = 2 * x + y
    peers = [(1 - x, y), (x, 1 - y), (1 - x, 1 - y)]
    return x, y, c, chip, peers


def _remote(src, dst, send_sem, recv_sem, dev):
    return pltpu.make_async_remote_copy(src_ref=src, dst_ref=dst, send_sem=send_sem, recv_sem=recv_sem,
                                        device_id=dev, device_id_type=MESH_ID)


class _Exchange:
    def __init__(self, inputs, out_shapes, plan, n_remote, n_local=0, aliases=None):
        self.inputs = list(inputs)
        self.out_shapes = list(out_shapes)
        self.plan = plan
        self.n_remote = n_remote
        self.n_local = n_local
        self.aliases = dict(aliases or {})

    def copies(self, in_refs, out_refs, send_sems, recv_sems, loc_sems):
        remote, local = self.plan(in_refs, out_refs)
        assert len(remote) == self.n_remote and len(local) == self.n_local
        cps = [_remote(s, d, send_sems.at[t], recv_sems.at[t], dev) for t, (s, d, dev) in enumerate(remote)]
        cps += [pltpu.make_async_copy(s, d, loc_sems.at[t]) for t, (s, d) in enumerate(local)]
        return cps

    def sem_shapes(self):
        return [pltpu.SemaphoreType.DMA((max(self.n_remote, 1),)), pltpu.SemaphoreType.DMA((max(self.n_remote, 1),)),
                pltpu.SemaphoreType.DMA((max(self.n_local, 1),))]


def _exchange_call(name, ex):
    n_ci = len(ex.inputs)

    def body(*refs):
        cins, couts = refs[:n_ci], refs[n_ci:n_ci + len(ex.out_shapes)]
        cps = ex.copies(cins, couts, *refs[n_ci + len(ex.out_shapes):])
        for cp in cps:
            cp.start()
        for cp in cps:
            cp.wait()

    return pl.pallas_call(
        body,
        name=name,
        in_specs=[ANY] * n_ci,
        out_specs=[ANY] * len(ex.out_shapes),
        out_shape=ex.out_shapes,
        scratch_shapes=ex.sem_shapes(),
        input_output_aliases=ex.aliases,
    )(*ex.inputs)


def _host_call(body, *, name, grid, in_specs, out_specs, out_shape, args, scratch_shapes=(), aliases=None, ride=None):
    n_in, n_out, n_scr = len(in_specs), len(out_specs), len(scratch_shapes)
    sem = ("arbitrary",) * len(grid)
    aliases = dict(aliases or {})
    if ride is None:
        outs = pl.pallas_call(body, name=name, grid=grid, in_specs=list(in_specs), out_specs=list(out_specs),
                              out_shape=list(out_shape), scratch_shapes=list(scratch_shapes),
                              input_output_aliases=aliases, compiler_params=_params(*sem))(*args)
        return list(outs), []
    n_ci, n_co = len(ride.inputs), len(ride.out_shapes)

    def full_body(*refs):
        ins, cins = refs[:n_in], refs[n_in:n_in + n_ci]
        o0 = n_in + n_ci
        outs, couts = refs[o0:o0 + n_out], refs[o0 + n_out:o0 + n_out + n_co]
        s0 = o0 + n_out + n_co
        scr, sems = refs[s0:s0 + n_scr], refs[s0 + n_scr:]
        first = functools.reduce(jnp.logical_and, [pl.program_id(d) == 0 for d in range(len(grid))])
        last = functools.reduce(jnp.logical_and, [pl.program_id(d) == grid[d] - 1 for d in range(len(grid))])

        @pl.when(first)
        def _():
            for cp in ride.copies(cins, couts, *sems):
                cp.start()

        body(*ins, *outs, *scr)

        @pl.when(last)
        def _():
            for cp in ride.copies(cins, couts, *sems):
                cp.wait()

    for ci, co in ride.aliases.items():
        aliases[n_in + ci] = n_out + co
    outs = pl.pallas_call(
        full_body, name=name, grid=grid, in_specs=list(in_specs) + [ANY] * n_ci,
        out_specs=list(out_specs) + [ANY] * n_co, out_shape=list(out_shape) + ride.out_shapes,
        scratch_shapes=list(scratch_shapes) + ride.sem_shapes(), input_output_aliases=aliases,
        compiler_params=_params(*sem))(*args, *ride.inputs)
    return list(outs[:n_out]), list(outs[n_out:])


def _allgather_layer0(shards):
    na = len(shards)

    def body(*refs):
        ins, outs = refs[:na], refs[na:2 * na]
        send_sems, recv_sems, fsend_sems, frecv_sems, loc_sems = refs[2 * na:]
        x, y, c, chip, peers = _place()
        sib = (x, y, 1 - c)
        locs = [pltpu.make_async_copy(ins[a].at[0], outs[a].at[chip], loc_sems.at[a]) for a in range(na)]
        for cp in locs:
            cp.start()
        pending = []
        for k, (px, py) in enumerate(peers):
            for a in range(na):
                cp = _remote(ins[a].at[0, c], outs[a].at[chip, c], send_sems.at[k * na + a],
                             recv_sems.at[k * na + a], (px, py, c))
                cp.start()
                pending.append(cp)
        for k, (px, py) in enumerate(peers):
            pchip = 2 * px + py
            for a in range(na):
                slab = outs[a].at[pchip, c]
                _remote(slab, slab, send_sems.at[k * na + a], recv_sems.at[k * na + a], (px, py, c)).wait_recv()
                cp = _remote(slab, slab, fsend_sems.at[k * na + a], frecv_sems.at[k * na + a], sib)
                cp.start()
                pending.append(cp)
        for k, (px, py) in enumerate(peers):
            pchip = 2 * px + py
            for a in range(na):
                slab = outs[a].at[pchip, 1 - c]
                _remote(slab, slab, fsend_sems.at[k * na + a], frecv_sems.at[k * na + a], sib).wait_recv()
        for cp in pending:
            cp.wait_send()
        for cp in locs:
            cp.wait()

    return pl.pallas_call(
        body,
        name="allgather_layer0",
        in_specs=[ANY] * na,
        out_specs=[ANY] * na,
        out_shape=[jax.ShapeDtypeStruct((N_CHIPS,) + a.shape[1:], a.dtype) for a in shards],
        scratch_shapes=[pltpu.SemaphoreType.DMA((3 * na,))] * 4 + [pltpu.SemaphoreType.DMA((na,))],
    )(*shards)


def _gather_layer1_first_half(win_s):
    def plan(cins, couts):
        _, _, c, chip, peers = _place()
        (src,), (dst,) = cins, couts
        return ([(src.at[1, 0], dst.at[chip, 0], (px, py, c)) for px, py in peers],
                [(src.at[1], dst.at[chip])])

    return _Exchange([win_s], [jax.ShapeDtypeStruct((N_CHIPS,) + win_s.shape[1:], win_s.dtype)], plan, 3, 1)


def _gather_layer1_second_half(win_s, win_f1):
    def plan(cins, couts):
        _, _, c, chip, peers = _place()
        src, (dst,) = cins[0], couts
        return [(src.at[1, 1], dst.at[chip, 1], (px, py, c)) for px, py in peers], []

    return _Exchange([win_s, win_f1], [jax.ShapeDtypeStruct(win_f1.shape, win_f1.dtype)], plan, 3, 0, aliases={1: 0})


def _gather_layer1_rest(shards):
    na = len(shards)

    def plan(cins, couts):
        _, _, c, chip, peers = _place()
        remote = [(cins[a].at[1], couts[a].at[chip], (px, py, c)) for a in range(na) for px, py in peers]
        return remote, [(cins[a].at[1], couts[a].at[chip]) for a in range(na)]

    return _Exchange(shards, [jax.ShapeDtypeStruct((N_CHIPS,) + a.shape[1:], a.dtype) for a in shards], plan,
                     3 * na, na)


def _swap_halves(grads):
    def plan(cins, couts):
        x, y, c, _, _ = _place()
        return [(g.at[:, 1 - c], r, (x, y, 1 - c)) for g, r in zip(cins, couts)], []

    return _Exchange(grads, [jax.ShapeDtypeStruct(g.shape[:1] + g.shape[2:], g.dtype) for g in grads], plan,
                     len(grads))


def _scatter_chip_sums(sums):
    na = len(sums)

    def plan(cins, couts):
        _, _, c, chip, peers = _place()
        remote = [(cins[a].at[2 * px + py], couts[a].at[chip, c], (px, py, c)) for a in range(na) for px, py in peers]
        return remote, [(cins[a].at[chip], couts[a].at[chip, c]) for a in range(na)]

    return _Exchange(sums, [jax.ShapeDtypeStruct((N_CHIPS, 2) + s.shape[1:], s.dtype) for s in sums], plan,
                     3 * na, na)


def _share_halves(parts):
    na = len(parts)

    def plan(cins, couts):
        x, y, c, _, _ = _place()
        return [(p.at[:, c], p.at[:, c], (x, y, 1 - c)) for p in couts], []

    return _Exchange(parts, [jax.ShapeDtypeStruct(p.shape, p.dtype) for p in parts], plan, na, 0,
                     aliases={a: a for a in range(na)})


def _allreduce_small(packed):
    rows = packed.shape[0]

    def body(sg_ref, res_ref, sib_ref, cs_ref, rem_ref, s1_send, s1_recv, s2_send, s2_recv):
        x, y, c, chip, peers = _place()
        cp = _remote(sg_ref, sib_ref, s1_send, s1_recv, (x, y, 1 - c))
        cp.start()
        cp.wait()
        cs_ref[...] = sg_ref[...] + sib_ref[...]
        rem_ref[chip] = cs_ref[...]
        cps = [_remote(cs_ref, rem_ref.at[chip], s2_send.at[k], s2_recv.at[chip], (px, py, c))
               for k, (px, py) in enumerate(peers)]
        for cp in cps:
            cp.start()
        for k, (px, py) in enumerate(peers):
            pchip = 2 * px + py
            _remote(cs_ref, rem_ref.at[pchip], s2_send.at[k], s2_recv.at[pchip], (px, py, c)).wait_recv()
        for cp in cps:
            cp.wait_send()
        res_ref[...] = ((rem_ref[0] + rem_ref[1]) + rem_ref[2]) + rem_ref[3]

    return pl.pallas_call(
        body,
        name="allreduce_small",
        in_specs=[VMEM_WHOLE],
        out_specs=VMEM_WHOLE,
        out_shape=jax.ShapeDtypeStruct((rows, GROUP_W), F32),
        scratch_shapes=[
            pltpu.VMEM((rows, GROUP_W), F32),
            pltpu.VMEM((rows, GROUP_W), F32),
            pltpu.VMEM((N_CHIPS, rows, GROUP_W), F32),
            pltpu.SemaphoreType.DMA,
            pltpu.SemaphoreType.DMA,
            pltpu.SemaphoreType.DMA((3,)),
            pltpu.SemaphoreType.DMA((N_CHIPS,)),
        ],
        compiler_params=pltpu.CompilerParams(vmem_limit_bytes=VMEM_LIMIT),
    )(packed)


def _add_halves(grad, other, core, name):
    n_shards, _, rows, cols = grad.shape
    tr = _tile(rows, max(16, (1 << 20) // (4 * cols) // 16 * 16), mult=16)

    def body(core_ref, g_ref, o_ref, out_ref):
        del core_ref
        out_ref[...] = (g_ref[...] + o_ref[...]).astype(BF16)

    return pl.pallas_call(
        body,
        name=name,
        grid_spec=pltpu.PrefetchScalarGridSpec(
            num_scalar_prefetch=1,
            grid=(n_shards, rows // tr),
            in_specs=[
                pl.BlockSpec((None, None, tr, cols), lambda s, i, core_ref: (s, core_ref[0], i, 0)),
                pl.BlockSpec((None, tr, cols), lambda s, i, core_ref: (s, i, 0)),
            ],
            out_specs=pl.BlockSpec((None, tr, cols), lambda s, i, core_ref: (s, i, 0)),
        ),
        out_shape=jax.ShapeDtypeStruct((n_shards, rows, cols), BF16),
        compiler_params=_params("arbitrary", "arbitrary"),
    )(core, grad, other)


def _inproj_fwd(x2, g_row, w_l, layer, ride=None):
    n = x2.shape[0]
    tm = _tile(n, 1024)

    def body(x_ref, g_ref, w_ref, p_ref, h_ref):
        @pl.when(pl.program_id(1) == 0)
        def _():
            xv = x_ref[...]
            r = lax.rsqrt(_lanemean(xv * xv) + RMS_EPS)
            h_ref[...] = (xv * r * g_ref[...]).astype(BF16)

        p_ref[...] = _dot(h_ref[...], w_ref[...]).astype(BF16)

    return _host_call(
        body,
        name=f"inproj_fwd_l{layer}",
        grid=(n // tm, N_COL_TILES),
        in_specs=[
            pl.BlockSpec((tm, D_MODEL), lambda i, j: (i, 0)),
            pl.BlockSpec((1, D_MODEL), lambda i, j: (0, 0)),
            pl.BlockSpec((None, D_MODEL, COL_TILE), lambda i, j: (j // 2, 0, j % 2)),
        ],
        out_specs=[
            pl.BlockSpec((tm, COL_TILE), lambda i, j: (i, j)),
            pl.BlockSpec((tm, D_MODEL), lambda i, j: (i, 0)),
        ],
        out_shape=[jax.ShapeDtypeStruct((n, IN_COLS), BF16), jax.ShapeDtypeStruct((n, D_MODEL), BF16)],
        args=[x2, g_row, w_l],
        ride=ride,
    )


def _layer_norm_parts(v, g, b):
    mu = _lanemean(v)
    d = v - mu
    rstd = lax.rsqrt(_lanemean(d * d) + LN_EPS)
    xh = d * rstd
    return xh, rstd, xh * g + b


def _window_sum(ref, slot, weight_row, n_taps, base, t_rows):
    total = None
    for b in range(8):
        group = None
        for a in range((base + n_taps - 1) // 8 + 1):
            k = 8 * a + b - base
            if 0 <= k < n_taps:
                term = weight_row(k) * ref[slot, 8 * a:8 * a + t_rows + 8, :]
                group = term if group is None else group + term
        if group is not None:
            part = group[b:b + t_rows, :]
            total = part if total is None else total + part
    return total


def _tap_grads(acc_ref, pad_ref, shift_ref, dy, ref, slot, n_taps, base, t_rows):
    pad_ref[0:8, :] = jnp.zeros((8, BRANCH_W), F32)
    pad_ref[8:8 + t_rows, :] = dy
    pad_ref[8 + t_rows:16 + t_rows, :] = jnp.zeros((8, BRANCH_W), F32)
    for b in range(8):
        taps = [(a, 8 * a + b - base) for a in range((base + n_taps - 1) // 8 + 1) if 0 <= 8 * a + b - base < n_taps]
        if not taps:
            continue
        shift_ref[...] = pad_ref[pl.ds(8 - b, t_rows + 8), :]
        for a, k in taps:
            prod = shift_ref[...] * ref[slot, 8 * a:8 * a + t_rows + 8, :]
            acc_ref[k] += prod.reshape((t_rows + 8) // 8, 8, BRANCH_W).sum(axis=0)


def _tril_mask():
    r = lax.broadcasted_iota(jnp.int32, (CHUNK, CHUNK), 0)
    c = lax.broadcasted_iota(jnp.int32, (CHUNK, CHUNK), 1)
    return r >= c


def _lanes(refs4):
    return jnp.concatenate([refs4[g] for g in range(N_CHIPS)], axis=1)


def _mixer_weight_specs():
    def whole(shape):
        nd = len(shape)
        return pl.BlockSpec(shape, lambda b, c: (0,) * nd)

    return [
        whole((N_GROUPS, GROUP_W, GROUP_W)),
        whole((1, BRANCH_W)),
        whole((N_CHIPS, CONV_ROWS, GROUP_W)),
        whole((1, BRANCH_W)),
        whole((1, BRANCH_W)),
        whole((1, BRANCH_W)),
        whole((1, BRANCH_W)),
        whole((1, BRANCH_W)),
        whole((N_GROUPS, CHUNK, CHUNK)),
        whole((CHUNK, BRANCH_W)),
        whole((N_CHIPS, SHORT_ROWS, GROUP_W)),
    ]


def _mixer_weight_args(mw):
    return [mw["pool_w"], mw["pool_scale"], mw["conv_w"], mw["conv_b"], mw["conv_ln_g"], mw["conv_ln_b"],
            mw["sgu_ln_g"], mw["sgu_ln_b"], mw["sgu_w"], mw["sgu_bias"], mw["sc_w"]]


def _mixers_fwd(proj3, mw, layer, ride=None):
    nb, seq, _ = proj3.shape
    t_rows = _tile(seq, 256)
    nc = seq // t_rows
    hb = t_rows // HALO

    def body(cur_ref, halo_ref, pw_ref, ps_ref, cw_ref, cb_ref, clg_ref, clb_ref, slg_ref, slb_ref, sw_ref,
             sbias_ref, scw_ref, z_ref, cacc_ref, ext_ref):
        c = pl.program_id(1)
        has_prev = c > 0
        row = lax.broadcasted_iota(jnp.int32, (t_rows, 1), 0)
        tpos = (c * t_rows + row + 1).astype(F32)

        def cur(k):
            return cur_ref[:, k * BRANCH_W:(k + 1) * BRANCH_W].astype(F32)

        def hal(k):
            return halo_ref[:, k * BRANCH_W:(k + 1) * BRANCH_W].astype(F32)

        def put_ext(slot, halo_val, cur_val):
            ext_ref[slot, 0:HALO, :] = jnp.where(has_prev, halo_val, 0.0)
            ext_ref[slot, HALO:HALO + t_rows, :] = cur_val
            ext_ref[slot, HALO + t_rows:HALO + t_rows + 8, :] = jnp.zeros((8, BRANCH_W), F32)

        px = cur(P_X)
        put_ext(0, hal(P_X), px)
        mixed = []
        for j, win in enumerate(POOL_WINDOWS):
            cols = slice(j * GROUP_W, (j + 1) * GROUP_W)
            s = px[:, cols]
            for i in range(1, win):
                s = s + ext_ref[0, pl.ds(HALO - i, t_rows), cols]
            pooled = s / jnp.minimum(tpos, float(win)) - px[:, cols]
            mixed.append(_dot(pooled.astype(BF16), pw_ref[j]))
        z_pool = jnp.concatenate(mixed, axis=1) * ps_ref[...] * _silu(cur(P_GATE))
        z_ref[:, 0:BRANCH_W] = z_pool.astype(BF16)

        put_ext(1, hal(C_A) * _sigmoid(hal(C_B)), cur(C_A) * _sigmoid(cur(C_B)))
        cw = _lanes(cw_ref)
        acc = cb_ref[...] + _window_sum(ext_ref, 1, lambda k: cw[k:k + 1, :], CONV_K, HALO - (CONV_K - 1), t_rows)
        cacc_ref[...] = acc
        _, _, ln = _layer_norm_parts(acc, clg_ref[...], clb_ref[...])
        z_ref[:, BRANCH_W:2 * BRANCH_W] = (_silu(ln) * _silu(cur(C_GATE))).astype(BF16)

        _, _, v = _layer_norm_parts(cur(G_V), slg_ref[...], slb_ref[...])
        vb = v.astype(BF16)
        mask = _tril_mask()
        wt = [jnp.where(mask, sw_ref[g], 0.0).astype(BF16) for g in range(N_GROUPS)]
        sp_rows = []
        for sub in range(t_rows // CHUNK):
            rows = slice(sub * CHUNK, (sub + 1) * CHUNK)
            sp_rows.append(jnp.concatenate(
                [_dot(wt[g], vb[rows, g * GROUP_W:(g + 1) * GROUP_W]) for g in range(N_GROUPS)], axis=1)
                + sbias_ref[...])
        sp = jnp.concatenate(sp_rows, axis=0)
        z_ref[:, 2 * BRANCH_W:3 * BRANCH_W] = (cur(G_U) * sp * _silu(cur(G_GATE))).astype(BF16)

        put_ext(2, hal(S_C) * hal(S_X), cur(S_C) * cur(S_X))
        scw = _lanes(scw_ref)
        cv = _window_sum(ext_ref, 2, lambda k: scw[k:k + 1, :], SHORT_K, HALO - (SHORT_K - 1), t_rows)
        z_ref[:, 3 * BRANCH_W:4 * BRANCH_W] = (cur(S_B) * cv * _silu(cur(S_GATE))).astype(BF16)

    return _host_call(
        body,
        name=f"mixers_fwd_l{layer}",
        grid=(nb, nc),
        in_specs=[
            pl.BlockSpec((None, t_rows, PIECE_COLS), lambda b, c: (b, c, 0)),
            pl.BlockSpec((None, HALO, PIECE_COLS), lambda b, c: (b, jnp.maximum(c * hb - 1, 0), 0)),
        ] + _mixer_weight_specs(),
        out_specs=[pl.BlockSpec((None, t_rows, N_BRANCH * BRANCH_W), lambda b, c: (b, c, 0)),
                   pl.BlockSpec((None, t_rows, BRANCH_W), lambda b, c: (b, c, 0))],
        out_shape=[jax.ShapeDtypeStruct((nb, seq, N_BRANCH * BRANCH_W), BF16),
                   jax.ShapeDtypeStruct((nb, seq, BRANCH_W), F32)],
        scratch_shapes=[pltpu.VMEM((3, HALO + t_rows + 8, BRANCH_W), F32)],
        args=[proj3, proj3, *_mixer_weight_args(mw)],
        ride=ride,
    )


MIXER_GRADS = ["pool_w", "pool_scale", "conv_w", "conv_b", "conv_ln_g", "conv_ln_b", "sgu_ln_g", "sgu_ln_b", "sgu_w",
               "sgu_b", "sc_w"]


def _mixers_bwd(proj3, dz3, cacc3, dproj3, mw, layer, ride=None):
    nb, seq, _ = proj3.shape
    t_rows = _tile(seq, 256)
    nc = seq // t_rows
    hb = t_rows // HALO

    def body(cur_ref, halo_ref, dz_ref, cacc_ref, pw_ref, ps_ref, cw_ref, cb_ref, clg_ref, clb_ref, slg_ref, slb_ref,
             sw_ref, sbias_ref, scw_ref, dp_in_ref, dp_ref, g_pw, g_ps, g_cw, g_cb, g_clg, g_clb, g_slg,
             g_slb, g_sw, g_sb, g_scw, ext_ref, nxt_ref, sb_acc, cw_acc, scw_acc, pad_ref, shift_ref):
        del dp_in_ref
        b = pl.program_id(0)
        r = pl.program_id(1)
        c = nc - 1 - r
        has_prev = c > 0
        row = lax.broadcasted_iota(jnp.int32, (t_rows, 1), 0)
        tpos = (c * t_rows + row + 1).astype(F32)

        @pl.when(jnp.logical_and(b == 0, r == 0))
        def _():
            for ref in (g_pw, g_ps, g_cb, g_clg, g_clb, g_slg, g_slb, g_sw, sb_acc, cw_acc, scw_acc):
                ref[...] = jnp.zeros_like(ref)

        @pl.when(r == 0)
        def _():
            nxt_ref[:, t_rows:t_rows + HALO, :] = jnp.zeros((3, HALO, BRANCH_W), F32)

        def cur(k):
            return cur_ref[:, k * BRANCH_W:(k + 1) * BRANCH_W].astype(F32)

        def hal(k):
            return halo_ref[:, k * BRANCH_W:(k + 1) * BRANCH_W].astype(F32)

        def dzp(k):
            return dz_ref[:, k * BRANCH_W:(k + 1) * BRANCH_W].astype(F32)

        def put_dp(k, val):
            dp_ref[:, k * BRANCH_W:(k + 1) * BRANCH_W] = val.astype(BF16)

        def put_ext(slot, halo_val, cur_val):
            ext_ref[slot, 0:HALO, :] = jnp.where(has_prev, halo_val, 0.0)
            ext_ref[slot, HALO:HALO + t_rows, :] = cur_val
            ext_ref[slot, HALO + t_rows:HALO + t_rows + 8, :] = jnp.zeros((8, BRANCH_W), F32)

        px = cur(P_X)
        put_ext(0, hal(P_X), px)
        pgate = cur(P_GATE)
        dz_pool = dzp(0)
        ps = ps_ref[...]
        pooled, mixed, cnts = [], [], []
        for j, win in enumerate(POOL_WINDOWS):
            cols = slice(j * GROUP_W, (j + 1) * GROUP_W)
            s = px[:, cols]
            for i in range(1, win):
                s = s + ext_ref[0, pl.ds(HALO - i, t_rows), cols]
            cnt = jnp.minimum(tpos, float(win))
            pj = s / cnt - px[:, cols]
            cnts.append(cnt)
            pooled.append(pj.astype(BF16))
            mixed.append(_dot(pooled[j], pw_ref[j]))
        mixed = jnp.concatenate(mixed, axis=1)
        put_dp(P_GATE, dz_pool * (mixed * ps) * _dsilu(pgate))
        d_out = dz_pool * _silu(pgate)
        g_ps[...] += _rowsum(d_out * mixed)
        d_mixed = (d_out * ps).astype(BF16)
        d_pooled = []
        for j in range(N_GROUPS):
            cols = slice(j * GROUP_W, (j + 1) * GROUP_W)
            g_pw[j] += _dot_tn(pooled[j], d_mixed[:, cols])
            dpj = _dot_nt(d_mixed[:, cols], pw_ref[j])
            d_pooled.append(dpj)
            nxt_ref[0, 0:t_rows, cols] = dpj / cnts[j]
        dpx = []
        for j, win in enumerate(POOL_WINDOWS):
            cols = slice(j * GROUP_W, (j + 1) * GROUP_W)
            s = nxt_ref[0, 0:t_rows, cols]
            for i in range(1, win):
                s = s + nxt_ref[0, pl.ds(i, t_rows), cols]
            dpx.append(s - d_pooled[j])
        put_dp(P_X, jnp.concatenate(dpx, axis=1))

        c_a = cur(C_A)
        sig_b = _sigmoid(cur(C_B))
        put_ext(1, hal(C_A) * _sigmoid(hal(C_B)), c_a * sig_b)
        cw = _lanes(cw_ref)
        xh, rstd, ln = _layer_norm_parts(cacc_ref[...], clg_ref[...], clb_ref[...])
        cgate = cur(C_GATE)
        dz_conv = dzp(1)
        put_dp(C_GATE, dz_conv * _silu(ln) * _dsilu(cgate))
        d_ln = dz_conv * _silu(cgate) * _dsilu(ln)
        g_clg[...] += _rowsum(d_ln * xh)
        g_clb[...] += _rowsum(d_ln)
        dxh = d_ln * clg_ref[...]
        dc = rstd * (dxh - _lanemean(dxh) - xh * _lanemean(dxh * xh))
        g_cb[...] += _rowsum(dc)
        nxt_ref[1, 0:t_rows, :] = dc
        _tap_grads(cw_acc, pad_ref, shift_ref, dc, ext_ref, 1, CONV_K, HALO - (CONV_K - 1), t_rows)
        dyg = _window_sum(nxt_ref, 1, lambda i: cw[CONV_K - 1 - i:CONV_K - i, :], CONV_K, 0, t_rows)
        put_dp(C_A, dyg * sig_b)
        put_dp(C_B, dyg * c_a * sig_b * (1.0 - sig_b))

        u = cur(G_U)
        ggate = cur(G_GATE)
        vxh, vrstd, v = _layer_norm_parts(cur(G_V), slg_ref[...], slb_ref[...])
        vb = v.astype(BF16)
        mask = _tril_mask()
        wt = [jnp.where(mask, sw_ref[g], 0.0) for g in range(N_GROUPS)]
        wt_b = [w.astype(BF16) for w in wt]
        wtt_b = [w.T.astype(BF16) for w in wt]
        dz_sgu = dzp(2)
        d_sgu = dz_sgu * _silu(ggate)
        d_sp = d_sgu * u
        d_spb = d_sp.astype(BF16)
        sp_rows, dv_rows = [], []
        for sub in range(t_rows // CHUNK):
            rows = slice(sub * CHUNK, (sub + 1) * CHUNK)
            sp_g, dv_g = [], []
            for g in range(N_GROUPS):
                cols = slice(g * GROUP_W, (g + 1) * GROUP_W)
                sp_g.append(_dot(wt_b[g], vb[rows, cols]))
                g_sw[g] += jnp.where(mask, _dot_nt(d_spb[rows, cols], vb[rows, cols]), 0.0)
                dv_g.append(_dot(wtt_b[g], d_spb[rows, cols]))
            sp_rows.append(jnp.concatenate(sp_g, axis=1) + sbias_ref[...])
            dv_rows.append(jnp.concatenate(dv_g, axis=1))
            sb_acc[...] += d_sp[rows, :]
        sp = jnp.concatenate(sp_rows, axis=0)
        dv = jnp.concatenate(dv_rows, axis=0)
        put_dp(G_GATE, dz_sgu * (u * sp) * _dsilu(ggate))
        put_dp(G_U, d_sgu * sp)
        g_slg[...] += _rowsum(dv * vxh)
        g_slb[...] += _rowsum(dv)
        dvx = dv * slg_ref[...]
        put_dp(G_V, vrstd * (dvx - _lanemean(dvx) - vxh * _lanemean(dvx * vxh)))

        s_b, s_c, s_x, sgate = cur(S_B), cur(S_C), cur(S_X), cur(S_GATE)
        put_ext(2, hal(S_C) * hal(S_X), s_c * s_x)
        scw = _lanes(scw_ref)
        cv = _window_sum(ext_ref, 2, lambda k: scw[k:k + 1, :], SHORT_K, HALO - (SHORT_K - 1), t_rows)
        dz_sc = dzp(3)
        put_dp(S_GATE, dz_sc * (s_b * cv) * _dsilu(sgate))
        d_pre = dz_sc * _silu(sgate)
        put_dp(S_B, d_pre * cv)
        dcv = d_pre * s_b
        nxt_ref[2, 0:t_rows, :] = dcv
        _tap_grads(scw_acc, pad_ref, shift_ref, dcv, ext_ref, 2, SHORT_K, HALO - (SHORT_K - 1), t_rows)
        du = _window_sum(nxt_ref, 2, lambda i: scw[SHORT_K - 1 - i:SHORT_K - i, :], SHORT_K, 0, t_rows)
        put_dp(S_C, du * s_x)
        put_dp(S_X, du * s_c)

        nxt_ref[:, t_rows:t_rows + HALO, :] = nxt_ref[:, 0:HALO, :]

        @pl.when(jnp.logical_and(b == nb - 1, r == nc - 1))
        def _():
            lane = lax.broadcasted_iota(jnp.int32, (CHUNK, GROUP_W), 1)
            out = jnp.zeros((CHUNK, GROUP_W), F32)
            for g in range(N_GROUPS):
                col = jnp.sum(sb_acc[:, g * GROUP_W:(g + 1) * GROUP_W], axis=1, keepdims=True)
                out = jnp.where(lane == g, col, out)
            g_sb[...] = out
            g_cw[...] = jnp.sum(cw_acc[...], axis=1)
            g_scw[...] = jnp.sum(scw_acc[...], axis=1)

    def acc_spec(shape):
        nd = len(shape)
        return pl.BlockSpec(shape, lambda b, r: (0,) * nd)

    acc_shapes = [
        (N_GROUPS, GROUP_W, GROUP_W),
        (1, BRANCH_W),
        (CONV_ROWS, BRANCH_W),
        (1, BRANCH_W),
        (1, BRANCH_W),
        (1, BRANCH_W),
        (1, BRANCH_W),
        (1, BRANCH_W),
        (N_GROUPS, CHUNK, CHUNK),
        (CHUNK, GROUP_W),
        (SHORT_ROWS, BRANCH_W),
    ]
    outs, rode = _host_call(
        body,
        name=f"mixers_bwd_l{layer}",
        grid=(nb, nc),
        in_specs=[
            pl.BlockSpec((None, t_rows, PIECE_COLS), lambda b, r: (b, nc - 1 - r, 0)),
            pl.BlockSpec((None, HALO, PIECE_COLS), lambda b, r: (b, jnp.maximum((nc - 1 - r) * hb - 1, 0), 0)),
            pl.BlockSpec((None, t_rows, N_BRANCH * BRANCH_W), lambda b, r: (b, nc - 1 - r, 0)),
            pl.BlockSpec((None, t_rows, BRANCH_W), lambda b, r: (b, nc - 1 - r, 0)),
        ] + _mixer_weight_specs() + [ANY],
        out_specs=[pl.BlockSpec((None, t_rows, PIECE_COLS), lambda b, r: (b, nc - 1 - r, 0))]
        + [acc_spec(s) for s in acc_shapes],
        out_shape=[jax.ShapeDtypeStruct(dproj3.shape, BF16)] + [jax.ShapeDtypeStruct(s, F32) for s in acc_shapes],
        scratch_shapes=[
            pltpu.VMEM((3, HALO + t_rows + 8, BRANCH_W), F32),
            pltpu.VMEM((3, t_rows + HALO, BRANCH_W), F32),
            pltpu.VMEM((CHUNK, BRANCH_W), F32),
            pltpu.VMEM((CONV_ROWS, 8, BRANCH_W), F32),
            pltpu.VMEM((SHORT_ROWS, 8, BRANCH_W), F32),
            pltpu.VMEM((t_rows + 16, BRANCH_W), F32),
            pltpu.VMEM((t_rows + 8, BRANCH_W), F32),
        ],
        aliases={4 + 11: 0},
        args=[proj3, proj3, dz3, cacc3, *_mixer_weight_args(mw), dproj3],
        ride=ride,
    )
    return outs[0], dict(zip(MIXER_GRADS, outs[1:])), rode


def _merge_fwd(z2, proj2, x2, wbr_l, wo_l, layer, ride=None):
    n = x2.shape[0]
    tm = _tile(n, 1024)

    def body(z_ref, gate_ref, x_ref, wbr_ref, wo_ref, xn_ref, mg_ref, acc_ref):
        nbr = pl.program_id(1)
        bo = _dot(z_ref[...], wbr_ref[nbr])
        contrib = _sigmoid(gate_ref[...].astype(F32)) * bo

        @pl.when(nbr == 0)
        def _():
            acc_ref[...] = contrib

        @pl.when(nbr > 0)
        def _():
            acc_ref[...] += contrib

        @pl.when(nbr == N_BRANCH - 1)
        def _():
            mg = acc_ref[...].astype(BF16)
            mg_ref[...] = mg
            xn_ref[...] = x_ref[...] + _dot(mg, wo_ref[...])

    return _host_call(
        body,
        name=f"merge_fwd_l{layer}",
        grid=(n // tm, N_BRANCH),
        in_specs=[
            pl.BlockSpec((tm, BRANCH_W), lambda i, b: (i, b)),
            pl.BlockSpec((tm, D_MODEL), lambda i, b: (i, GATE_BLOCK0 + b)),
            pl.BlockSpec((tm, D_MODEL), lambda i, b: (i, 0)),
            pl.BlockSpec((N_BRANCH, BRANCH_W, D_MODEL), lambda i, b: (0, 0, 0)),
            pl.BlockSpec((D_MODEL, D_MODEL), lambda i, b: (0, 0)),
        ],
        out_specs=[
            pl.BlockSpec((tm, D_MODEL), lambda i, b: (i, 0)),
            pl.BlockSpec((tm, D_MODEL), lambda i, b: (i, 0)),
        ],
        out_shape=[jax.ShapeDtypeStruct((n, D_MODEL), F32), jax.ShapeDtypeStruct((n, D_MODEL), BF16)],
        scratch_shapes=[pltpu.VMEM((tm, D_MODEL), F32)],
        args=[z2, proj2, x2, wbr_l, wo_l],
        ride=ride,
    )


def _merge_bwd(dout2, merged2, z2, proj2, wbr_l, wo_l, layer, ride=None):
    n = dout2.shape[0]
    tm = _tile(n, 512)
    nt = n // tm

    def body(do_ref, mg_ref, z_ref, gate_ref, wbr_ref, wo_ref, dz_ref, dg_ref, gwo_ref, gwbr_ref, dm_ref,
             awo_ref, awbr_ref):
        i = pl.program_id(0)
        nbr = pl.program_id(1)

        @pl.when(nbr == 0)
        def _():
            do_b = do_ref[...].astype(BF16)
            dm_ref[...] = _dot_nt(do_b, wo_ref[...])
            gw = _dot_tn(mg_ref[...], do_b)

            @pl.when(i == 0)
            def _():
                awo_ref[...] = gw

            @pl.when(i > 0)
            def _():
                awo_ref[...] += gw

        zt = z_ref[...]
        wbr = wbr_ref[nbr]
        bo = _dot(zt, wbr)
        gt = _sigmoid(gate_ref[...].astype(F32))
        dm = dm_ref[...]
        dbo = (dm * gt).astype(BF16)
        dg_ref[...] = (dm * bo * gt * (1.0 - gt)).astype(BF16)
        dz_ref[...] = _dot_nt(dbo, wbr).astype(BF16)
        gw = _dot_tn(zt, dbo)

        @pl.when(i == 0)
        def _():
            awbr_ref[nbr] = gw

        @pl.when(i > 0)
        def _():
            awbr_ref[nbr] += gw

        @pl.when(jnp.logical_and(i == nt - 1, nbr == N_BRANCH - 1))
        def _():
            pltpu.sync_copy(awo_ref, gwo_ref)
            pltpu.sync_copy(awbr_ref, gwbr_ref)

    return _host_call(
        body,
        name=f"merge_bwd_l{layer}",
        grid=(nt, N_BRANCH),
        in_specs=[
            pl.BlockSpec((tm, D_MODEL), lambda i, b: (i, 0)),
            pl.BlockSpec((tm, D_MODEL), lambda i, b: (i, 0)),
            pl.BlockSpec((tm, BRANCH_W), lambda i, b: (i, b)),
            pl.BlockSpec((tm, D_MODEL), lambda i, b: (i, GATE_BLOCK0 + b)),
            pl.BlockSpec((N_BRANCH, BRANCH_W, D_MODEL), lambda i, b: (0, 0, 0)),
            pl.BlockSpec((D_MODEL, D_MODEL), lambda i, b: (0, 0)),
        ],
        out_specs=[
            pl.BlockSpec((tm, BRANCH_W), lambda i, b: (i, b)),
            pl.BlockSpec((tm, D_MODEL), lambda i, b: (i, GATE_BLOCK0 + b)),
            ANY,
            ANY,
        ],
        out_shape=[
            jax.ShapeDtypeStruct((n, N_BRANCH * BRANCH_W), BF16),
            jax.ShapeDtypeStruct((n, IN_COLS), BF16),
            jax.ShapeDtypeStruct((D_MODEL, D_MODEL), F32),
            jax.ShapeDtypeStruct((N_BRANCH, BRANCH_W, D_MODEL), F32),
        ],
        scratch_shapes=[
            pltpu.VMEM((tm, D_MODEL), F32),
            pltpu.VMEM((D_MODEL, D_MODEL), F32),
            pltpu.VMEM((N_BRANCH, BRANCH_W, D_MODEL), F32),
        ],
        args=[dout2, merged2, z2, proj2, wbr_l, wo_l],
        ride=ride,
    )


def _loss_head(x2, g_row, tgt2):
    n = x2.shape[0]
    tm = _tile(n, 512)

    def body(x_ref, g_ref, t_ref, dx_ref, loss_ref, dg_ref):
        @pl.when(pl.program_id(0) == 0)
        def _():
            loss_ref[...] = jnp.zeros_like(loss_ref)
            dg_ref[...] = jnp.zeros_like(dg_ref)

        xv = x_ref[...]
        g = g_ref[...]
        r = lax.rsqrt(_lanemean(xv * xv) + RMS_EPS)
        xh = xv * r
        err = xh * g - t_ref[...]
        loss_ref[...] += 0.5 * jnp.sum(_lanemean(err * err), axis=0, keepdims=True)
        dy = err * (1.0 / D_MODEL)
        dg_ref[...] += _rowsum(dy * xh)
        dxh = dy * g
        dx_ref[...] = r * (dxh - xh * _lanemean(dxh * xh))

    return pl.pallas_call(
        body,
        name="loss_head",
        grid=(n // tm,),
        in_specs=[
            pl.BlockSpec((tm, D_MODEL), lambda i: (i, 0)),
            pl.BlockSpec((1, D_MODEL), lambda i: (0, 0)),
            pl.BlockSpec((tm, D_MODEL), lambda i: (i, 0)),
        ],
        out_specs=[
            pl.BlockSpec((tm, D_MODEL), lambda i: (i, 0)),
            pl.BlockSpec((1, GROUP_W), lambda i: (0, 0)),
            pl.BlockSpec((1, D_MODEL), lambda i: (0, 0)),
        ],
        out_shape=[jax.ShapeDtypeStruct((n, D_MODEL), F32), jax.ShapeDtypeStruct((1, GROUP_W), F32),
                   jax.ShapeDtypeStruct((1, D_MODEL), F32)],
        compiler_params=_params("arbitrary"),
    )(x2, g_row, tgt2)


def _inproj_bwd_x(dproj2, w_l, x2, g_row, dout2, layer, ride=None):
    n = x2.shape[0]
    tm = _tile(n, 1024)

    def body(dp_ref, w_ref, x_ref, g_ref, do_ref, dx_ref, dng_ref, dh_ref):
        i = pl.program_id(0)
        s = pl.program_id(1)
        part = _dot_nt(dp_ref[...], w_ref[...])

        @pl.when(s == 0)
        def _():
            dh_ref[...] = part

        @pl.when(s > 0)
        def _():
            dh_ref[...] += part

        @pl.when(jnp.logical_and(i == 0, s == 0))
        def _():
            dng_ref[...] = jnp.zeros_like(dng_ref)

        @pl.when(s == N_COL_TILES - 1)
        def _():
            xv = x_ref[...]
            r = lax.rsqrt(_lanemean(xv * xv) + RMS_EPS)
            xh = xv * r
            dh = dh_ref[...]
            dng_ref[...] += _rowsum(dh * xh)
            dxh = dh * g_ref[...]
            dx_ref[...] = do_ref[...] + r * (dxh - xh * _lanemean(dxh * xh))

    return _host_call(
        body,
        name=f"inproj_bwd_x_l{layer}",
        grid=(n // tm, N_COL_TILES),
        in_specs=[
            pl.BlockSpec((tm, COL_TILE), lambda i, s: (i, s)),
            pl.BlockSpec((None, D_MODEL, COL_TILE), lambda i, s: (s // 2, 0, s % 2)),
            pl.BlockSpec((tm, D_MODEL), lambda i, s: (i, 0)),
            pl.BlockSpec((1, D_MODEL), lambda i, s: (0, 0)),
            pl.BlockSpec((tm, D_MODEL), lambda i, s: (i, 0)),
        ],
        out_specs=[
            pl.BlockSpec((tm, D_MODEL), lambda i, s: (i, 0)),
            pl.BlockSpec((1, D_MODEL), lambda i, s: (0, 0)),
        ],
        out_shape=[jax.ShapeDtypeStruct((n, D_MODEL), F32), jax.ShapeDtypeStruct((1, D_MODEL), F32)],
        scratch_shapes=[pltpu.VMEM((tm, D_MODEL), F32)],
        args=[dproj2, w_l, x2, g_row, dout2],
        ride=ride,
    )


def _inproj_bwd_w(h2, dproj2, layer, ride=None):
    n = h2.shape[0]
    tm = _tile(n, 1024)

    def body(h_ref, dp_ref, gw_ref):
        gw = _dot_tn(h_ref[...], dp_ref[...])

        @pl.when(pl.program_id(1) == 0)
        def _():
            gw_ref[...] = gw

        @pl.when(pl.program_id(1) > 0)
        def _():
            gw_ref[...] += gw

    outs, rode = _host_call(
        body,
        name=f"inproj_bwd_w_l{layer}",
        grid=(N_COL_TILES, n // tm),
        in_specs=[
            pl.BlockSpec((tm, D_MODEL), lambda s, i: (i, 0)),
            pl.BlockSpec((tm, COL_TILE), lambda s, i: (i, s)),
        ],
        out_specs=[pl.BlockSpec((None, D_MODEL, COL_TILE), lambda s, i: (s // 2, 0, s % 2))],
        out_shape=[jax.ShapeDtypeStruct((N_CHIPS, D_MODEL, SHARD_COLS), F32)],
        args=[h2, dproj2],
        ride=ride,
    )
    return outs[0], rode


def _adamw_math(w, g, m, v):
    m = ADAM_B1 * m + (1.0 - ADAM_B1) * g
    v = ADAM_B2 * v + (1.0 - ADAM_B2) * jnp.square(g)
    m_hat = m / (1.0 - ADAM_B1 ** ADAM_STEP)
    v_hat = v / (1.0 - ADAM_B2 ** ADAM_STEP)
    delta = -ADAM_LR * (m_hat / (jnp.sqrt(v_hat) + ADAM_EPS) + ADAM_WD * w)
    return delta, m, v


def _adamw_sharded(w, m, v, parts, name):
    depth, rows, cols = w.shape
    tr = _tile(rows, max(16, (1 << 19) // (4 * cols) // 16 * 16), mult=16)

    def body(w_ref, m_ref, v_ref, p0_ref, p1_ref, g_out, d_out, m_out, v_out):
        def update(p_ref):
            g = ((p_ref[0].astype(F32) + p_ref[1].astype(F32)) + p_ref[2].astype(F32)) + p_ref[3].astype(F32)
            delta, m2, v2 = _adamw_math(w_ref[...], g, m_ref[...], v_ref[...])
            g_out[...] = g
            d_out[...] = delta
            m_out[...] = m2
            v_out[...] = v2

        @pl.when(pl.program_id(0) == 0)
        def _():
            update(p0_ref)

        @pl.when(pl.program_id(0) == 1)
        def _():
            update(p1_ref)

    spec = pl.BlockSpec((None, tr, cols), lambda l, i: (l, i, 0))

    def part_spec(layer):
        return pl.BlockSpec((N_CHIPS, tr, cols), lambda l, i: (0, jnp.where(l == layer, i, 0), 0))

    return pl.pallas_call(
        body,
        name=f"adamw_{name}",
        grid=(depth, rows // tr),
        in_specs=[spec, spec, spec, part_spec(0), part_spec(1)],
        out_specs=[spec] * 4,
        out_shape=[jax.ShapeDtypeStruct(w.shape, F32)] * 4,
        compiler_params=_params("arbitrary", "arbitrary"),
    )(w, m, v, parts[0], parts[1])


def _adamw_packed(w, m, v, g):
    rows = w.shape[0]
    tr = _tile(rows, rows // 2 if rows % 16 == 0 else rows)

    def body(w_ref, m_ref, v_ref, g_ref, d_out, m_out, v_out):
        delta, m2, v2 = _adamw_math(w_ref[...], g_ref[...], m_ref[...], v_ref[...])
        d_out[...] = delta
        m_out[...] = m2
        v_out[...] = v2

    spec = pl.BlockSpec((tr, GROUP_W), lambda i: (i, 0))
    return pl.pallas_call(
        body,
        name="adamw_small",
        grid=(rows // tr,),
        in_specs=[spec] * 4,
        out_specs=[spec] * 3,
        out_shape=[jax.ShapeDtypeStruct(w.shape, F32)] * 3,
        compiler_params=_params("arbitrary"),
    )(w, m, v, g)


SMALL = ["norm_g", "pool_w", "pool_scale", "conv_b", "conv_ln_g", "conv_ln_b", "sgu_ln_g", "sgu_ln_b", "sgu_w",
         "sgu_b", "final_g"]
WEIGHTS = ["norm_g", "w_in", "pool_w", "pool_scale", "conv_w", "conv_b", "conv_ln_g", "conv_ln_b", "sgu_ln_g",
           "sgu_ln_b", "sgu_w", "sgu_b", "sc_w", "w_branch", "w_o", "final_g"]
HALF_SHAPES = [(D_MODEL // 2, SHARD_COLS), (N_BRANCH * BRANCH_W // 2, BR_SHARD), (BR_SHARD // 2, D_MODEL),
               (CS_ROWS // 2, GROUP_W)]


def _pack_small(tree):
    return jnp.concatenate([tree[k].reshape(-1, GROUP_W) for k in SMALL], axis=0)


def _unpack_small(packed, like):
    out, r = {}, 0
    for k in SMALL:
        nr = like[k].size // GROUP_W
        out[k] = packed[r:r + nr].reshape(like[k].shape)
        r += nr
    return out


def _pad_rows(a, rows):
    pad = [(0, 0)] * a.ndim
    pad[-2] = (0, rows - a.shape[-2])
    return jnp.pad(a, pad)


def _pack_cs(conv, short):
    return jnp.concatenate([_pad_rows(conv, CS_ROWS // 2), _pad_rows(short, CS_ROWS // 2)], axis=-2)


def _shard_major_rows(a):
    return a.reshape(a.shape[0], N_CHIPS, GROUP_W).transpose(1, 0, 2)


def kernel(x, norm_g, w_in, pool_w, pool_scale, conv_w, conv_b, conv_ln_g, conv_ln_b, sgu_ln_g, sgu_ln_b, sgu_w, sgu_b, sc_w, w_branch, w_o, final_g, loss_target, m_norm_g, m_w_in, m_pool_w, m_pool_scale, m_conv_w, m_conv_b, m_conv_ln_g, m_conv_ln_b, m_sgu_ln_g, m_sgu_ln_b, m_sgu_w, m_sgu_b, m_sc_w, m_w_branch, m_w_o, m_final_g, v_norm_g, v_w_in, v_pool_w, v_pool_scale, v_conv_w, v_conv_b, v_conv_ln_g, v_conv_ln_b, v_sgu_ln_g, v_sgu_ln_b, v_sgu_w, v_sgu_b, v_sc_w, v_w_branch, v_w_o, v_final_g):
    w = dict(norm_g=norm_g, w_in=w_in, pool_w=pool_w, pool_scale=pool_scale, conv_w=conv_w, conv_b=conv_b,
             conv_ln_g=conv_ln_g, conv_ln_b=conv_ln_b, sgu_ln_g=sgu_ln_g, sgu_ln_b=sgu_ln_b, sgu_w=sgu_w,
             sgu_b=sgu_b, sc_w=sc_w, w_branch=w_branch, w_o=w_o, final_g=final_g)
    mom = dict(norm_g=m_norm_g, w_in=m_w_in, pool_w=m_pool_w, pool_scale=m_pool_scale, conv_w=m_conv_w,
               conv_b=m_conv_b, conv_ln_g=m_conv_ln_g, conv_ln_b=m_conv_ln_b, sgu_ln_g=m_sgu_ln_g,
               sgu_ln_b=m_sgu_ln_b, sgu_w=m_sgu_w, sgu_b=m_sgu_b, sc_w=m_sc_w, w_branch=m_w_branch, w_o=m_w_o,
               final_g=m_final_g)
    var = dict(norm_g=v_norm_g, w_in=v_w_in, pool_w=v_pool_w, pool_scale=v_pool_scale, conv_w=v_conv_w,
               conv_b=v_conv_b, conv_ln_g=v_conv_ln_g, conv_ln_b=v_conv_ln_b, sgu_ln_g=v_sgu_ln_g,
               sgu_ln_b=v_sgu_ln_b, sgu_w=v_sgu_w, sgu_b=v_sgu_b, sc_w=v_sc_w, w_branch=v_w_branch, w_o=v_w_o,
               final_g=v_final_g)

    nb, seq, _ = x.shape
    n = nb * seq
    core = lax.axis_index("c").astype(jnp.int32).reshape(1)

    win_s = w_in.astype(BF16).reshape(DEPTH, 2, D_MODEL // 2, SHARD_COLS)
    wbr_s = w_branch.astype(BF16)
    wo_s = w_o.astype(BF16)
    cs_s = _pack_cs(conv_w, sc_w)

    def halves(a, shape):
        return a.reshape((DEPTH, 2) + shape)

    gathered0 = _allgather_layer0([win_s] + [halves(a, s) for a, s in zip((wbr_s, wo_s, cs_s), HALF_SHAPES[1:])])

    def layer_weights(win_f, wbr_f, wo_f, cs_f):
        cs_f = cs_f.reshape(N_CHIPS, CS_ROWS, GROUP_W)
        return dict(win=win_f.reshape(N_CHIPS, D_MODEL, SHARD_COLS),
                    wbr=wbr_f.reshape(N_CHIPS, N_BRANCH, BRANCH_W, BR_SHARD).transpose(1, 2, 0, 3).reshape(
                        N_BRANCH, BRANCH_W, D_MODEL),
                    wo=wo_f.reshape(D_MODEL, D_MODEL),
                    conv_w=cs_f[:, :CONV_ROWS], sc_w=cs_f[:, CS_ROWS // 2:CS_ROWS // 2 + SHORT_ROWS])

    def mixer_weights(l, gathered):
        row = lambda a: a[l].reshape(1, BRANCH_W)
        bias = jnp.repeat(jnp.swapaxes(sgu_b[l], 0, 1), GROUP_W, axis=1)
        return dict(pool_w=pool_w[l].astype(BF16), pool_scale=row(pool_scale), conv_w=gathered["conv_w"],
                    conv_b=row(conv_b), conv_ln_g=row(conv_ln_g), conv_ln_b=row(conv_ln_b), sgu_ln_g=row(sgu_ln_g),
                    sgu_ln_b=row(sgu_ln_b), sgu_w=sgu_w[l], sgu_bias=bias, sc_w=gathered["sc_w"])

    lw = [layer_weights(*gathered0), None]
    mw = [mixer_weights(0, lw[0]), None]

    xs, projs, hs, zs, mgs, caccs = [x.reshape(n, D_MODEL)], [], [], [], [], []
    for l in range(DEPTH):
        first = l == 0
        (proj, h), rode = _inproj_fwd(xs[l], norm_g[l:l + 1], lw[l]["win"], l,
                                      ride=_gather_layer1_first_half(win_s) if first else None)
        if first:
            win_f1 = rode[0]
        (z3, cacc), rode = _mixers_fwd(proj.reshape(nb, seq, IN_COLS), mw[l], l,
                                       ride=_gather_layer1_second_half(win_s, win_f1) if first else None)
        caccs.append(cacc)
        if first:
            win_f1 = rode[0]
        z = z3.reshape(n, N_BRANCH * BRANCH_W)
        (x_new, merged), rode = _merge_fwd(z, proj, xs[l], lw[l]["wbr"], lw[l]["wo"], l,
                                           ride=_gather_layer1_rest([wbr_s, wo_s, cs_s]) if first else None)
        if first:
            lw[1] = layer_weights(win_f1, *rode)
            mw[1] = mixer_weights(1, lw[1])
        projs.append(proj)
        hs.append(h)
        zs.append(z)
        mgs.append(merged)
        xs.append(x_new)

    dx, loss_part, g_final = _loss_head(xs[DEPTH], final_g.reshape(1, D_MODEL), loss_target.reshape(n, D_MODEL))
    loss = lax.psum(loss_part[0, 0], ("x", "y", "c"))

    def chip_major(gwbr):
        return gwbr.reshape(N_BRANCH, BRANCH_W, N_CHIPS, BR_SHARD).transpose(2, 0, 1, 3)

    def as_halves(arrays, first_index=0):
        return [g.reshape((N_CHIPS, 2) + s) for g, s in zip(arrays, HALF_SHAPES[first_index:])]

    def add_halves(grads, others, l, first_index=0):
        return [_add_halves(g, o, core, f"add_halves_l{l}_{first_index + i}")
                for i, (g, o) in enumerate(zip(grads, others))]

    per_layer, parts = {}, [None] * DEPTH
    (dz, dproj, gwo, gwbr), _ = _merge_bwd(dx, mgs[1], zs[1], projs[1], lw[1]["wbr"], lw[1]["wo"], 1)
    dproj3, gm, _ = _mixers_bwd(projs[1].reshape(nb, seq, IN_COLS), dz.reshape(nb, seq, N_BRANCH * BRANCH_W), caccs[1],
                                dproj.reshape(nb, seq, IN_COLS), mw[1], 1)
    dproj = dproj3.reshape(n, IN_COLS)
    gwin, _ = _inproj_bwd_w(hs[1], dproj, 1)
    gcs = _pack_cs(_shard_major_rows(gm["conv_w"]), _shard_major_rows(gm["sc_w"]))
    grads1 = as_halves([gwin, chip_major(gwbr), gwo, gcs])
    (dx, g_norm), others1 = _inproj_bwd_x(dproj, lw[1]["win"], xs[1], norm_g[1:2], dx, 1, ride=_swap_halves(grads1))
    gm["norm_g"] = g_norm
    per_layer[1] = gm
    sums1 = add_halves(grads1, others1, 1)
    (dz, dproj, gwo, gwbr), scattered1 = _merge_bwd(dx, mgs[0], zs[0], projs[0], lw[0]["wbr"], lw[0]["wo"], 0,
                                                    ride=_scatter_chip_sums(sums1))
    dproj3, gm, parts[1] = _mixers_bwd(projs[0].reshape(nb, seq, IN_COLS), dz.reshape(nb, seq, N_BRANCH * BRANCH_W),
                                       caccs[0], dproj.reshape(nb, seq, IN_COLS), mw[0], 0,
                                       ride=_share_halves(scattered1))
    dproj = dproj3.reshape(n, IN_COLS)
    gcs = _pack_cs(_shard_major_rows(gm["conv_w"]), _shard_major_rows(gm["sc_w"]))
    small0 = as_halves([chip_major(gwbr), gwo, gcs], first_index=1)
    gwin, others_small0 = _inproj_bwd_w(hs[0], dproj, 0, ride=_swap_halves(small0))
    (gwin_h,) = as_halves([gwin])
    (other_win0,) = _exchange_call("swap_halves_l0_w_in", _swap_halves([gwin_h]))
    sums0 = add_halves([gwin_h], [other_win0], 0) + add_halves(small0, others_small0, 0, first_index=1)
    (dx, g_norm), scattered0 = _inproj_bwd_x(dproj, lw[0]["win"], xs[0], norm_g[0:1], dx, 0,
                                             ride=_scatter_chip_sums(sums0))
    gm["norm_g"] = g_norm
    per_layer[0] = gm
    parts[0] = _exchange_call("share_halves_l0", _share_halves(scattered0))
    grad_x = dx.reshape(nb, seq, D_MODEL)

    grads, deltas, new_m, new_v = {}, {}, {}, {}

    def flat(a, i):
        rows, cols = 2 * HALF_SHAPES[i][0], HALF_SHAPES[i][1]
        return a.reshape(a.shape[0], rows, cols)

    def sharded_update(i, name, tree_of):
        res = _adamw_sharded(flat(tree_of(w), i), flat(tree_of(mom), i), flat(tree_of(var), i),
                             [flat(parts[l][i], i) for l in range(DEPTH)], name)
        return res

    for i, name in enumerate(["w_in", "w_branch", "w_o"]):
        res = sharded_update(i, name, lambda t: t[name])
        for tree, r in zip((grads, deltas, new_m, new_v), res):
            tree[name] = r.reshape(w[name].shape)
    res = sharded_update(3, "conv_sc", lambda t: _pack_cs(t["conv_w"], t["sc_w"]))
    for tree, r in zip((grads, deltas, new_m, new_v), res):
        tree["conv_w"] = r[:, :CONV_K]
        tree["sc_w"] = r[:, CS_ROWS // 2:CS_ROWS // 2 + SHORT_K]

    def stack(k, shape):
        return jnp.stack([per_layer[l][k] for l in range(DEPTH)]).reshape(shape)

    g_small = {"final_g": g_final.reshape(D_MODEL), "norm_g": stack("norm_g", (DEPTH, D_MODEL)),
               "pool_w": stack("pool_w", pool_w.shape), "sgu_w": stack("sgu_w", sgu_w.shape),
               "sgu_b": jnp.swapaxes(stack("sgu_b", (DEPTH, CHUNK, GROUP_W))[:, :, :N_GROUPS], 1, 2)}
    for k in ("pool_scale", "conv_b", "conv_ln_g", "conv_ln_b", "sgu_ln_g", "sgu_ln_b"):
        g_small[k] = stack(k, (DEPTH, BRANCH_W))
    g_packed = _allreduce_small(_pack_small(g_small))
    d_packed, m_packed, v_packed = _adamw_packed(_pack_small(w), _pack_small(mom), _pack_small(var), g_packed)
    for tree, packed in ((grads, g_packed), (deltas, d_packed), (new_m, m_packed), (new_v, v_packed)):
        tree.update(_unpack_small(packed, w))

    return (loss, grad_x, *[grads[k] for k in WEIGHTS], *[deltas[k] for k in WEIGHTS],
            *[new_m[k] for k in WEIGHTS], *[new_v[k] for k in WEIGHTS])
```

```python
import functools

import jax
import jax.numpy as jnp
from jax import lax
from jax.experimental import pallas as pl
from jax.experimental.pallas import tpu as pltpu

F32 = jnp.float32
BF16 = jnp.bfloat16

D_MODEL = 1024
DEPTH = 2
N_BRANCH = 4
BRANCH_W = 512
N_GROUPS = 4
GROUP_W = 128
POOL_WINDOWS = (2, 4, 8, 16)
CONV_K = 31
SHORT_K = 3
CHUNK = 128
N_PIECES = 12
PIECE_COLS = N_PIECES * BRANCH_W
IN_COLS = PIECE_COLS + N_BRANCH * D_MODEL
N_CHIPS = 4
SHARD_COLS = IN_COLS // N_CHIPS
BR_SHARD = D_MODEL // N_CHIPS
COL_TILE = SHARD_COLS // 2
N_COL_TILES = IN_COLS // COL_TILE
GATE_BLOCK0 = PIECE_COLS // D_MODEL
RMS_EPS = 1e-6
LN_EPS = 1e-5
HALO = 32
CONV_ROWS = 32
SHORT_ROWS = 8
CS_ROWS = 64

ADAM_LR = 0.001
ADAM_B1 = 0.9
ADAM_B2 = 0.999
ADAM_EPS = 1e-08
ADAM_WD = 0.01
ADAM_STEP = 10

VMEM_LIMIT = 52 * 1024 * 1024
MESH_ID = pl.DeviceIdType.MESH
ANY = pl.BlockSpec(memory_space=pl.ANY)
VMEM_WHOLE = pl.BlockSpec(memory_space=pltpu.VMEM)

(P_X, P_GATE, C_A, C_B, C_GATE, G_U, G_V, G_GATE, S_B, S_C, S_X, S_GATE) = range(N_PIECES)


def _params(*sem):
    return pltpu.CompilerParams(dimension_semantics=sem, vmem_limit_bytes=VMEM_LIMIT)


def _sigmoid(v):
    return 0.5 * jnp.tanh(0.5 * v) + 0.5


def _silu(v):
    return v * _sigmoid(v)


def _silu_pair(v):
    s = _sigmoid(v)
    return v * s, s * (1.0 + v * (1.0 - s))


def _dot(a, b):
    return jnp.dot(a, b, preferred_element_type=F32)


def _dot_nt(a, b):
    return lax.dot_general(a, b, (((1,), (1,)), ((), ())), preferred_element_type=F32)


def _dot_tn(a, b):
    return lax.dot_general(a, b, (((0,), (0,)), ((), ())), preferred_element_type=F32)


def _rowsum(v):
    return jnp.sum(v, axis=0, keepdims=True)


def _lanemean(v):
    return jnp.mean(v, axis=-1, keepdims=True)


def _tile(n, want, mult=8):
    t = max(1, min(n, want))
    while n % t or (t % mult and t != n):
        t -= 1
    return t


def _place():
    x, y, c = lax.axis_index("x"), lax.axis_index("y"), lax.axis_index("c")
    chip = 2 * x + y
    peers = [(1 - x, y), (x, 1 - y), (1 - x, 1 - y)]
    return x, y, c, chip, peers


def _remote(src, dst, send_sem, recv_sem, dev):
    return pltpu.make_async_remote_copy(src_ref=src, dst_ref=dst, send_sem=send_sem, recv_sem=recv_sem,
                                        device_id=dev, device_id_type=MESH_ID)


class _Exchange:
    def __init__(self, inputs, out_shapes, plan, n_remote, n_local=0, aliases=None):
        self.inputs = list(inputs)
        self.out_shapes = list(out_shapes)
        self.plan = plan
        self.n_remote = n_remote
        self.n_local = n_local
        self.aliases = dict(aliases or {})

    def copies(self, in_refs, out_refs, send_sems, recv_sems, loc_sems):
        remote, local = self.plan(in_refs, out_refs)
        assert len(remote) == self.n_remote and len(local) == self.n_local
        cps = [_remote(s, d, send_sems.at[t], recv_sems.at[t], dev) for t, (s, d, dev) in enumerate(remote)]
        cps += [pltpu.make_async_copy(s, d, loc_sems.at[t]) for t, (s, d) in enumerate(local)]
        return cps

    def sem_shapes(self):
        return [pltpu.SemaphoreType.DMA((max(self.n_remote, 1),)), pltpu.SemaphoreType.DMA((max(self.n_remote, 1),)),
                pltpu.SemaphoreType.DMA((max(self.n_local, 1),))]


def _exchange_call(name, ex):
    n_ci = len(ex.inputs)

    def body(*refs):
        cins, couts = refs[:n_ci], refs[n_ci:n_ci + len(ex.out_shapes)]
        cps = ex.copies(cins, couts, *refs[n_ci + len(ex.out_shapes):])
        for cp in cps:
            cp.start()
        for cp in cps:
            cp.wait()

    return pl.pallas_call(
        body,
        name=name,
        in_specs=[ANY] * n_ci,
        out_specs=[ANY] * len(ex.out_shapes),
        out_shape=ex.out_shapes,
        scratch_shapes=ex.sem_shapes(),
        input_output_aliases=ex.aliases,
    )(*ex.inputs)


def _host_call(body, *, name, grid, in_specs, out_specs, out_shape, args, scratch_shapes=(), aliases=None, ride=None):
    n_in, n_out, n_scr = len(in_specs), len(out_specs), len(scratch_shapes)
    sem = ("arbitrary",) * len(grid)
    aliases = dict(aliases or {})
    if ride is None:
        outs = pl.pallas_call(body, name=name, grid=grid, in_specs=list(in_specs), out_specs=list(out_specs),
                              out_shape=list(out_shape), scratch_shapes=list(scratch_shapes),
                              input_output_aliases=aliases, compiler_params=_params(*sem))(*args)
        return list(outs), []
    n_ci, n_co = len(ride.inputs), len(ride.out_shapes)

    def full_body(*refs):
        ins, cins = refs[:n_in], refs[n_in:n_in + n_ci]
        o0 = n_in + n_ci
        outs, couts = refs[o0:o0 + n_out], refs[o0 + n_out:o0 + n_out + n_co]
        s0 = o0 + n_out + n_co
        scr, sems = refs[s0:s0 + n_scr], refs[s0 + n_scr:]
        first = functools.reduce(jnp.logical_and, [pl.program_id(d) == 0 for d in range(len(grid))])
        last = functools.reduce(jnp.logical_and, [pl.program_id(d) == grid[d] - 1 for d in range(len(grid))])

        @pl.when(first)
        def _():
            for cp in ride.copies(cins, couts, *sems):
                cp.start()

        body(*ins, *outs, *scr)

        @pl.when(last)
        def _():
            for cp in ride.copies(cins, couts, *sems):
                cp.wait()

    for ci, co in ride.aliases.items():
        aliases[n_in + ci] = n_out + co
    outs = pl.pallas_call(
        full_body, name=name, grid=grid, in_specs=list(in_specs) + [ANY] * n_ci,
        out_specs=list(out_specs) + [ANY] * n_co, out_shape=list(out_shape) + ride.out_shapes,
        scratch_shapes=list(scratch_shapes) + ride.sem_shapes(), input_output_aliases=aliases,
        compiler_params=_params(*sem))(*args, *ride.inputs)
    return list(outs[:n_out]), list(outs[n_out:])


def _allgather_layer0(shards):
    na = len(shards)

    def body(*refs):
        ins, outs = refs[:na], refs[na:2 * na]
        send_sems, recv_sems, fsend_sems, frecv_sems, loc_sems = refs[2 * na:]
        x, y, c, chip, peers = _place()
        sib = (x, y, 1 - c)
        locs = [pltpu.make_async_copy(ins[a].at[0], outs[a].at[chip], loc_sems.at[a]) for a in range(na)]
        for cp in locs:
            cp.start()
        pending = []
        for k, (px, py) in enumerate(peers):
            for a in range(na):
                cp = _remote(ins[a].at[0, c], outs[a].at[chip, c], send_sems.at[k * na + a],
                             recv_sems.at[k * na + a], (px, py, c))
                cp.start()
                pending.append(cp)
        for k, (px, py) in enumerate(peers):
            pchip = 2 * px + py
            for a in range(na):
                slab = outs[a].at[pchip, c]
                _remote(slab, slab, send_sems.at[k * na + a], recv_sems.at[k * na + a], (px, py, c)).wait_recv()
                cp = _remote(slab, slab, fsend_sems.at[k * na + a], frecv_sems.at[k * na + a], sib)
                cp.start()
                pending.append(cp)
        for k, (px, py) in enumerate(peers):
            pchip = 2 * px + py
            for a in range(na):
                slab = outs[a].at[pchip, 1 - c]
                _remote(slab, slab, fsend_sems.at[k * na + a], frecv_sems.at[k * na + a], sib).wait_recv()
        for cp in pending:
            cp.wait_send()
        for cp in locs:
            cp.wait()

    return pl.pallas_call(
        body,
        name="allgather_layer0",
        in_specs=[ANY] * na,
        out_specs=[ANY] * na,
        out_shape=[jax.ShapeDtypeStruct((N_CHIPS,) + a.shape[1:], a.dtype) for a in shards],
        scratch_shapes=[pltpu.SemaphoreType.DMA((3 * na,))] * 4 + [pltpu.SemaphoreType.DMA((na,))],
    )(*shards)


def _gather_layer1_first_half(win_s):
    def plan(cins, couts):
        _, _, c, chip, peers = _place()
        (src,), (dst,) = cins, couts
        return ([(src.at[1, 0], dst.at[chip, 0], (px, py, c)) for px, py in peers],
                [(src.at[1], dst.at[chip])])

    return _Exchange([win_s], [jax.ShapeDtypeStruct((N_CHIPS,) + win_s.shape[1:], win_s.dtype)], plan, 3, 1)


def _gather_layer1_second_half(win_s, win_f1):
    def plan(cins, couts):
        _, _, c, chip, peers = _place()
        src, (dst,) = cins[0], couts
        return [(src.at[1, 1], dst.at[chip, 1], (px, py, c)) for px, py in peers], []

    return _Exchange([win_s, win_f1], [jax.ShapeDtypeStruct(win_f1.shape, win_f1.dtype)], plan, 3, 0, aliases={1: 0})


def _gather_whole(shards, layer):
    na = len(shards)

    def plan(cins, couts):
        _, _, c, chip, peers = _place()
        remote = [(cins[a].at[layer], couts[a].at[chip], (px, py, c)) for a in range(na) for px, py in peers]
        return remote, [(cins[a].at[layer], couts[a].at[chip]) for a in range(na)]

    return _Exchange(shards, [jax.ShapeDtypeStruct((N_CHIPS,) + a.shape[1:], a.dtype) for a in shards], plan,
                     3 * na, na)


def _join(a, b):
    n_in, n_out = len(a.inputs), len(a.out_shapes)

    def plan(cins, couts):
        remote_a, local_a = a.plan(cins[:n_in], couts[:n_out])
        remote_b, local_b = b.plan(cins[n_in:], couts[n_out:])
        return remote_a + remote_b, local_a + local_b

    aliases = dict(a.aliases)
    aliases.update({n_in + i: n_out + o for i, o in b.aliases.items()})
    return _Exchange(a.inputs + b.inputs, a.out_shapes + b.out_shapes, plan, a.n_remote + b.n_remote,
                     a.n_local + b.n_local, aliases)


def _swap_halves(grads):
    def plan(cins, couts):
        x, y, c, _, _ = _place()
        return [(g.at[:, 1 - c], r, (x, y, 1 - c)) for g, r in zip(cins, couts)], []

    return _Exchange(grads, [jax.ShapeDtypeStruct(g.shape[:1] + g.shape[2:], g.dtype) for g in grads], plan,
                     len(grads))


def _scatter_chip_sums(sums):
    na = len(sums)

    def plan(cins, couts):
        _, _, c, chip, peers = _place()
        remote = [(cins[a].at[2 * px + py], couts[a].at[chip, c], (px, py, c)) for a in range(na) for px, py in peers]
        return remote, [(cins[a].at[chip], couts[a].at[chip, c]) for a in range(na)]

    return _Exchange(sums, [jax.ShapeDtypeStruct((N_CHIPS, 2) + s.shape[1:], s.dtype) for s in sums], plan,
                     3 * na, na)


def _share_halves(parts):
    na = len(parts)

    def plan(cins, couts):
        x, y, c, _, _ = _place()
        return [(p.at[:, c], p.at[:, c], (x, y, 1 - c)) for p in couts], []

    return _Exchange(parts, [jax.ShapeDtypeStruct(p.shape, p.dtype) for p in parts], plan, na, 0,
                     aliases={a: a for a in range(na)})


def _allreduce_small(packed, ride):
    rows = packed.shape[0]
    n_ci, n_co = len(ride.inputs), len(ride.out_shapes)

    def body(*refs):
        sg_ref, cins = refs[0], refs[1:1 + n_ci]
        res_ref, couts = refs[1 + n_ci], refs[2 + n_ci:2 + n_ci + n_co]
        sib_ref, cs_ref, rem_ref, s1_send, s1_recv, s2_send, s2_recv = refs[2 + n_ci + n_co:9 + n_ci + n_co]
        riding = ride.copies(cins, couts, *refs[9 + n_ci + n_co:])
        for cp in riding:
            cp.start()
        x, y, c, chip, peers = _place()
        cp = _remote(sg_ref, sib_ref, s1_send, s1_recv, (x, y, 1 - c))
        cp.start()
        cp.wait()
        cs_ref[...] = sg_ref[...] + sib_ref[...]
        rem_ref[chip] = cs_ref[...]
        cps = [_remote(cs_ref, rem_ref.at[chip], s2_send.at[k], s2_recv.at[chip], (px, py, c))
               for k, (px, py) in enumerate(peers)]
        for cp in cps:
            cp.start()
        for k, (px, py) in enumerate(peers):
            pchip = 2 * px + py
            _remote(cs_ref, rem_ref.at[pchip], s2_send.at[k], s2_recv.at[pchip], (px, py, c)).wait_recv()
        for cp in cps:
            cp.wait_send()
        res_ref[...] = ((rem_ref[0] + rem_ref[1]) + rem_ref[2]) + rem_ref[3]
        for cp in riding:
            cp.wait()

    outs = pl.pallas_call(
        body,
        name="allreduce_small",
        in_specs=[VMEM_WHOLE] + [ANY] * n_ci,
        out_specs=[VMEM_WHOLE] + [ANY] * n_co,
        out_shape=[jax.ShapeDtypeStruct((rows, GROUP_W), F32)] + ride.out_shapes,
        scratch_shapes=[
            pltpu.VMEM((rows, GROUP_W), F32),
            pltpu.VMEM((rows, GROUP_W), F32),
            pltpu.VMEM((N_CHIPS, rows, GROUP_W), F32),
            pltpu.SemaphoreType.DMA,
            pltpu.SemaphoreType.DMA,
            pltpu.SemaphoreType.DMA((3,)),
            pltpu.SemaphoreType.DMA((N_CHIPS,)),
        ] + ride.sem_shapes(),
        input_output_aliases={1 + i: 1 + o for i, o in ride.aliases.items()},
        compiler_params=pltpu.CompilerParams(vmem_limit_bytes=VMEM_LIMIT),
    )(packed, *ride.inputs)
    return outs[0], list(outs[1:])


def _add_halves(grad, other, core, name):
    n_shards, _, rows, cols = grad.shape
    tr = _tile(rows, max(16, (1 << 20) // (4 * cols) // 16 * 16), mult=16)

    def body(core_ref, g_ref, o_ref, out_ref):
        del core_ref
        out_ref[...] = (g_ref[...] + o_ref[...]).astype(BF16)

    return pl.pallas_call(
        body,
        name=name,
        grid_spec=pltpu.PrefetchScalarGridSpec(
            num_scalar_prefetch=1,
            grid=(n_shards, rows // tr),
            in_specs=[
                pl.BlockSpec((None, None, tr, cols), lambda s, i, core_ref: (s, core_ref[0], i, 0)),
                pl.BlockSpec((None, tr, cols), lambda s, i, core_ref: (s, i, 0)),
            ],
            out_specs=pl.BlockSpec((None, tr, cols), lambda s, i, core_ref: (s, i, 0)),
        ),
        out_shape=jax.ShapeDtypeStruct((n_shards, rows, cols), BF16),
        compiler_params=_params("arbitrary", "arbitrary"),
    )(core, grad, other)


def _inproj_fwd(x2, g_row, w_l, layer, ride=None):
    n = x2.shape[0]
    tm = _tile(n, 1024)

    def body(x_ref, g_ref, w_ref, p_ref, h_ref):
        @pl.when(pl.program_id(1) == 0)
        def _():
            xv = x_ref[...]
            r = lax.rsqrt(_lanemean(xv * xv) + RMS_EPS)
            h_ref[...] = (xv * r * g_ref[...]).astype(BF16)

        p_ref[...] = _dot(h_ref[...], w_ref[...]).astype(BF16)

    return _host_call(
        body,
        name=f"inproj_fwd_l{layer}",
        grid=(n // tm, N_COL_TILES),
        in_specs=[
            pl.BlockSpec((tm, D_MODEL), lambda i, j: (i, 0)),
            pl.BlockSpec((1, D_MODEL), lambda i, j: (0, 0)),
            pl.BlockSpec((None, D_MODEL, COL_TILE), lambda i, j: (j // 2, 0, j % 2)),
        ],
        out_specs=[
            pl.BlockSpec((tm, COL_TILE), lambda i, j: (i, j)),
            pl.BlockSpec((tm, D_MODEL), lambda i, j: (i, 0)),
        ],
        out_shape=[jax.ShapeDtypeStruct((n, IN_COLS), BF16), jax.ShapeDtypeStruct((n, D_MODEL), BF16)],
        args=[x2, g_row, w_l],
        ride=ride,
    )


def _layer_norm_parts(v, g, b):
    mu = _lanemean(v)
    d = v - mu
    rstd = lax.rsqrt(_lanemean(d * d) + LN_EPS)
    xh = d * rstd
    return xh, rstd, xh * g + b


def _window_sum(ref, slot, weight_row, n_taps, base, t_rows):
    total = None
    for b in range(8):
        group = None
        for a in range((base + n_taps - 1) // 8 + 1):
            k = 8 * a + b - base
            if 0 <= k < n_taps:
                term = weight_row(k) * ref[slot, 8 * a:8 * a + t_rows + 8, :]
                group = term if group is None else group + term
        if group is not None:
            part = group[b:b + t_rows, :]
            total = part if total is None else total + part
    return total


def _tap_grads(acc_ref, pad_ref, shift_ref, dy, ref, slot, n_taps, base, t_rows):
    pad_ref[0:8, :] = jnp.zeros((8, BRANCH_W), F32)
    pad_ref[8:8 + t_rows, :] = dy
    pad_ref[8 + t_rows:16 + t_rows, :] = jnp.zeros((8, BRANCH_W), F32)
    for b in range(8):
        taps = [(a, 8 * a + b - base) for a in range((base + n_taps - 1) // 8 + 1) if 0 <= 8 * a + b - base < n_taps]
        if not taps:
            continue
        shift_ref[...] = pad_ref[pl.ds(8 - b, t_rows + 8), :]
        for a, k in taps:
            prod = shift_ref[...] * ref[slot, 8 * a:8 * a + t_rows + 8, :]
            acc_ref[k] += prod.reshape((t_rows + 8) // 8, 8, BRANCH_W).sum(axis=0)


def _tril_mask():
    r = lax.broadcasted_iota(jnp.int32, (CHUNK, CHUNK), 0)
    c = lax.broadcasted_iota(jnp.int32, (CHUNK, CHUNK), 1)
    return r >= c


def _lanes(refs4):
    return jnp.concatenate([refs4[g] for g in range(N_CHIPS)], axis=1)


def _mixer_weight_specs():
    def whole(shape):
        nd = len(shape)
        return pl.BlockSpec(shape, lambda b, c: (0,) * nd)

    return [
        whole((N_GROUPS, GROUP_W, GROUP_W)),
        whole((1, BRANCH_W)),
        whole((N_CHIPS, CONV_ROWS, GROUP_W)),
        whole((1, BRANCH_W)),
        whole((1, BRANCH_W)),
        whole((1, BRANCH_W)),
        whole((1, BRANCH_W)),
        whole((1, BRANCH_W)),
        whole((N_GROUPS, CHUNK, CHUNK)),
        whole((CHUNK, BRANCH_W)),
        whole((N_CHIPS, SHORT_ROWS, GROUP_W)),
    ]


def _mixer_weight_args(mw):
    return [mw["pool_w"], mw["pool_scale"], mw["conv_w"], mw["conv_b"], mw["conv_ln_g"], mw["conv_ln_b"],
            mw["sgu_ln_g"], mw["sgu_ln_b"], mw["sgu_w"], mw["sgu_bias"], mw["sc_w"]]


def _mixers_fwd(proj3, mw, layer, ride=None):
    nb, seq, _ = proj3.shape
    t_rows = _tile(seq, 256)
    nc = seq // t_rows
    hb = t_rows // HALO

    def body(cur_ref, halo_ref, pw_ref, ps_ref, cw_ref, cb_ref, clg_ref, clb_ref, slg_ref, slb_ref, sw_ref,
             sbias_ref, scw_ref, z_ref, cacc_ref, ext_ref):
        c = pl.program_id(1)
        has_prev = c > 0
        row = lax.broadcasted_iota(jnp.int32, (t_rows, 1), 0)
        tpos = (c * t_rows + row + 1).astype(F32)

        def cur(k):
            return cur_ref[:, k * BRANCH_W:(k + 1) * BRANCH_W].astype(F32)

        def hal(k):
            return halo_ref[:, k * BRANCH_W:(k + 1) * BRANCH_W].astype(F32)

        def put_ext(slot, halo_val, cur_val):
            ext_ref[slot, 0:HALO, :] = jnp.where(has_prev, halo_val, 0.0)
            ext_ref[slot, HALO:HALO + t_rows, :] = cur_val
            ext_ref[slot, HALO + t_rows:HALO + t_rows + 8, :] = jnp.zeros((8, BRANCH_W), F32)

        px = cur(P_X)
        put_ext(0, hal(P_X), px)
        mixed = []
        for j, win in enumerate(POOL_WINDOWS):
            cols = slice(j * GROUP_W, (j + 1) * GROUP_W)
            s = px[:, cols]
            for i in range(1, win):
                s = s + ext_ref[0, pl.ds(HALO - i, t_rows), cols]
            pooled = s / jnp.minimum(tpos, float(win)) - px[:, cols]
            mixed.append(_dot(pooled.astype(BF16), pw_ref[j]))
        z_pool = jnp.concatenate(mixed, axis=1) * ps_ref[...] * _silu(cur(P_GATE))
        z_ref[:, 0:BRANCH_W] = z_pool.astype(BF16)

        put_ext(1, hal(C_A) * _sigmoid(hal(C_B)), cur(C_A) * _sigmoid(cur(C_B)))
        cw = _lanes(cw_ref)
        acc = cb_ref[...] + _window_sum(ext_ref, 1, lambda k: cw[k:k + 1, :], CONV_K, HALO - (CONV_K - 1), t_rows)
        cacc_ref[...] = acc
        _, _, ln = _layer_norm_parts(acc, clg_ref[...], clb_ref[...])
        z_ref[:, BRANCH_W:2 * BRANCH_W] = (_silu(ln) * _silu(cur(C_GATE))).astype(BF16)

        _, _, v = _layer_norm_parts(cur(G_V), slg_ref[...], slb_ref[...])
        vb = v.astype(BF16)
        mask = _tril_mask()
        wt = [jnp.where(mask, sw_ref[g], 0.0).astype(BF16) for g in range(N_GROUPS)]
        sp_rows = []
        for sub in range(t_rows // CHUNK):
            rows = slice(sub * CHUNK, (sub + 1) * CHUNK)
            sp_rows.append(jnp.concatenate(
                [_dot(wt[g], vb[rows, g * GROUP_W:(g + 1) * GROUP_W]) for g in range(N_GROUPS)], axis=1)
                + sbias_ref[...])
        sp = jnp.concatenate(sp_rows, axis=0)
        z_ref[:, 2 * BRANCH_W:3 * BRANCH_W] = (cur(G_U) * sp * _silu(cur(G_GATE))).astype(BF16)

        put_ext(2, hal(S_C) * hal(S_X), cur(S_C) * cur(S_X))
        scw = _lanes(scw_ref)
        cv = _window_sum(ext_ref, 2, lambda k: scw[k:k + 1, :], SHORT_K, HALO - (SHORT_K - 1), t_rows)
        z_ref[:, 3 * BRANCH_W:4 * BRANCH_W] = (cur(S_B) * cv * _silu(cur(S_GATE))).astype(BF16)

    return _host_call(
        body,
        name=f"mixers_fwd_l{layer}",
        grid=(nb, nc),
        in_specs=[
            pl.BlockSpec((None, t_rows, PIECE_COLS), lambda b, c: (b, c, 0)),
            pl.BlockSpec((None, HALO, PIECE_COLS), lambda b, c: (b, jnp.maximum(c * hb - 1, 0), 0)),
        ] + _mixer_weight_specs(),
        out_specs=[pl.BlockSpec((None, t_rows, N_BRANCH * BRANCH_W), lambda b, c: (b, c, 0)),
                   pl.BlockSpec((None, t_rows, BRANCH_W), lambda b, c: (b, c, 0))],
        out_shape=[jax.ShapeDtypeStruct((nb, seq, N_BRANCH * BRANCH_W), BF16),
                   jax.ShapeDtypeStruct((nb, seq, BRANCH_W), F32)],
        scratch_shapes=[pltpu.VMEM((3, HALO + t_rows + 8, BRANCH_W), F32)],
        args=[proj3, proj3, *_mixer_weight_args(mw)],
        ride=ride,
    )


MIXER_GRADS = ["pool_w", "pool_scale", "conv_w", "conv_b", "conv_ln_g", "conv_ln_b", "sgu_ln_g", "sgu_ln_b", "sgu_w",
               "sgu_b", "sc_w"]


def _mixers_bwd(proj3, dz3, cacc3, dproj3, mw, layer, ride=None):
    nb, seq, _ = proj3.shape
    t_rows = _tile(seq, 256)
    nc = seq // t_rows
    hb = t_rows // HALO

    def body(cur_ref, halo_ref, dz_ref, cacc_ref, pw_ref, ps_ref, cw_ref, cb_ref, clg_ref, clb_ref, slg_ref, slb_ref,
             sw_ref, sbias_ref, scw_ref, dp_in_ref, dp_ref, g_pw, g_ps, g_cw, g_cb, g_clg, g_clb, g_slg,
             g_slb, g_sw, g_sb, g_scw, ext_ref, nxt_ref, sb_acc, cw_acc, scw_acc, pad_ref, shift_ref):
        del dp_in_ref
        b = pl.program_id(0)
        r = pl.program_id(1)
        c = nc - 1 - r
        has_prev = c > 0
        row = lax.broadcasted_iota(jnp.int32, (t_rows, 1), 0)
        tpos = (c * t_rows + row + 1).astype(F32)

        @pl.when(jnp.logical_and(b == 0, r == 0))
        def _():
            for ref in (g_pw, g_ps, g_cb, g_clg, g_clb, g_slg, g_slb, g_sw, sb_acc, cw_acc, scw_acc):
                ref[...] = jnp.zeros_like(ref)

        @pl.when(r == 0)
        def _():
            nxt_ref[:, t_rows:t_rows + HALO, :] = jnp.zeros((3, HALO, BRANCH_W), F32)

        def cur(k):
            return cur_ref[:, k * BRANCH_W:(k + 1) * BRANCH_W].astype(F32)

        def hal(k):
            return halo_ref[:, k * BRANCH_W:(k + 1) * BRANCH_W].astype(F32)

        def dzp(k):
            return dz_ref[:, k * BRANCH_W:(k + 1) * BRANCH_W].astype(F32)

        def put_dp(k, val):
            dp_ref[:, k * BRANCH_W:(k + 1) * BRANCH_W] = val.astype(BF16)

        def put_ext(slot, halo_val, cur_val):
            ext_ref[slot, 0:HALO, :] = jnp.where(has_prev, halo_val, 0.0)
            ext_ref[slot, HALO:HALO + t_rows, :] = cur_val
            ext_ref[slot, HALO + t_rows:HALO + t_rows + 8, :] = jnp.zeros((8, BRANCH_W), F32)

        px = cur(P_X)
        put_ext(0, hal(P_X), px)
        pgate = cur(P_GATE)
        dz_pool = dzp(0)
        ps = ps_ref[...]
        pooled, mixed, cnts = [], [], []
        for j, win in enumerate(POOL_WINDOWS):
            cols = slice(j * GROUP_W, (j + 1) * GROUP_W)
            s = px[:, cols]
            for i in range(1, win):
                s = s + ext_ref[0, pl.ds(HALO - i, t_rows), cols]
            cnt = jnp.minimum(tpos, float(win))
            pj = s / cnt - px[:, cols]
            cnts.append(cnt)
            pooled.append(pj.astype(BF16))
            mixed.append(_dot(pooled[j], pw_ref[j]))
        mixed = jnp.concatenate(mixed, axis=1)
        pg_silu, pg_dsilu = _silu_pair(pgate)
        put_dp(P_GATE, dz_pool * (mixed * ps) * pg_dsilu)
        d_out = dz_pool * pg_silu
        g_ps[...] += _rowsum(d_out * mixed)
        d_mixed = (d_out * ps).astype(BF16)
        d_pooled = []
        for j in range(N_GROUPS):
            cols = slice(j * GROUP_W, (j + 1) * GROUP_W)
            g_pw[j] += _dot_tn(pooled[j], d_mixed[:, cols])
            dpj = _dot_nt(d_mixed[:, cols], pw_ref[j])
            d_pooled.append(dpj)
            nxt_ref[0, 0:t_rows, cols] = dpj / cnts[j]
        dpx = []
        for j, win in enumerate(POOL_WINDOWS):
            cols = slice(j * GROUP_W, (j + 1) * GROUP_W)
            s = nxt_ref[0, 0:t_rows, cols]
            for i in range(1, win):
                s = s + nxt_ref[0, pl.ds(i, t_rows), cols]
            dpx.append(s - d_pooled[j])
        put_dp(P_X, jnp.concatenate(dpx, axis=1))

        c_a = cur(C_A)
        sig_b = _sigmoid(cur(C_B))
        put_ext(1, hal(C_A) * _sigmoid(hal(C_B)), c_a * sig_b)
        cw = _lanes(cw_ref)
        xh, rstd, ln = _layer_norm_parts(cacc_ref[...], clg_ref[...], clb_ref[...])
        cgate = cur(C_GATE)
        dz_conv = dzp(1)
        ln_silu, ln_dsilu = _silu_pair(ln)
        cg_silu, cg_dsilu = _silu_pair(cgate)
        put_dp(C_GATE, dz_conv * ln_silu * cg_dsilu)
        d_ln = dz_conv * cg_silu * ln_dsilu
        g_clg[...] += _rowsum(d_ln * xh)
        g_clb[...] += _rowsum(d_ln)
        dxh = d_ln * clg_ref[...]
        dc = rstd * (dxh - _lanemean(dxh) - xh * _lanemean(dxh * xh))
        g_cb[...] += _rowsum(dc)
        nxt_ref[1, 0:t_rows, :] = dc
        _tap_grads(cw_acc, pad_ref, shift_ref, dc, ext_ref, 1, CONV_K, HALO - (CONV_K - 1), t_rows)
        dyg = _window_sum(nxt_ref, 1, lambda i: cw[CONV_K - 1 - i:CONV_K - i, :], CONV_K, 0, t_rows)
        put_dp(C_A, dyg * sig_b)
        put_dp(C_B, dyg * c_a * sig_b * (1.0 - sig_b))

        u = cur(G_U)
        ggate = cur(G_GATE)
        vxh, vrstd, v = _layer_norm_parts(cur(G_V), slg_ref[...], slb_ref[...])
        vb = v.astype(BF16)
        mask = _tril_mask()
        wt = [jnp.where(mask, sw_ref[g], 0.0) for g in range(N_GROUPS)]
        wt_b = [w.astype(BF16) for w in wt]
        wtt_b = [w.T.astype(BF16) for w in wt]
        dz_sgu = dzp(2)
        gg_silu, gg_dsilu = _silu_pair(ggate)
        d_sgu = dz_sgu * gg_silu
        d_sp = d_sgu * u
        d_spb = d_sp.astype(BF16)
        sp_rows, dv_rows = [], []
        for sub in range(t_rows // CHUNK):
            rows = slice(sub * CHUNK, (sub + 1) * CHUNK)
            sp_g, dv_g = [], []
            for g in range(N_GROUPS):
                cols = slice(g * GROUP_W, (g + 1) * GROUP_W)
                sp_g.append(_dot(wt_b[g], vb[rows, cols]))
                g_sw[g] += jnp.where(mask, _dot_nt(d_spb[rows, cols], vb[rows, cols]), 0.0)
                dv_g.append(_dot(wtt_b[g], d_spb[rows, cols]))
            sp_rows.append(jnp.concatenate(sp_g, axis=1) + sbias_ref[...])
            dv_rows.append(jnp.concatenate(dv_g, axis=1))
            sb_acc[...] += d_sp[rows, :]
        sp = jnp.concatenate(sp_rows, axis=0)
        dv = jnp.concatenate(dv_rows, axis=0)
        put_dp(G_GATE, dz_sgu * (u * sp) * gg_dsilu)
        put_dp(G_U, d_sgu * sp)
        g_slg[...] += _rowsum(dv * vxh)
        g_slb[...] += _rowsum(dv)
        dvx = dv * slg_ref[...]
        put_dp(G_V, vrstd * (dvx - _lanemean(dvx) - vxh * _lanemean(dvx * vxh)))

        s_b, s_c, s_x, sgate = cur(S_B), cur(S_C), cur(S_X), cur(S_GATE)
        put_ext(2, hal(S_C) * hal(S_X), s_c * s_x)
        scw = _lanes(scw_ref)
        cv = _window_sum(ext_ref, 2, lambda k: scw[k:k + 1, :], SHORT_K, HALO - (SHORT_K - 1), t_rows)
        dz_sc = dzp(3)
        sg_silu, sg_dsilu = _silu_pair(sgate)
        put_dp(S_GATE, dz_sc * (s_b * cv) * sg_dsilu)
        d_pre = dz_sc * sg_silu
        put_dp(S_B, d_pre * cv)
        dcv = d_pre * s_b
        nxt_ref[2, 0:t_rows, :] = dcv
        _tap_grads(scw_acc, pad_ref, shift_ref, dcv, ext_ref, 2, SHORT_K, HALO - (SHORT_K - 1), t_rows)
        du = _window_sum(nxt_ref, 2, lambda i: scw[SHORT_K - 1 - i:SHORT_K - i, :], SHORT_K, 0, t_rows)
        put_dp(S_C, du * s_x)
        put_dp(S_X, du * s_c)

        nxt_ref[:, t_rows:t_rows + HALO, :] = nxt_ref[:, 0:HALO, :]

        @pl.when(jnp.logical_and(b == nb - 1, r == nc - 1))
        def _():
            lane = lax.broadcasted_iota(jnp.int32, (CHUNK, GROUP_W), 1)
            out = jnp.zeros((CHUNK, GROUP_W), F32)
            for g in range(N_GROUPS):
                col = jnp.sum(sb_acc[:, g * GROUP_W:(g + 1) * GROUP_W], axis=1, keepdims=True)
                out = jnp.where(lane == g, col, out)
            g_sb[...] = out
            g_cw[...] = jnp.sum(cw_acc[...], axis=1)
            g_scw[...] = jnp.sum(scw_acc[...], axis=1)

    def acc_spec(shape):
        nd = len(shape)
        return pl.BlockSpec(shape, lambda b, r: (0,) * nd)

    acc_shapes = [
        (N_GROUPS, GROUP_W, GROUP_W),
        (1, BRANCH_W),
        (CONV_ROWS, BRANCH_W),
        (1, BRANCH_W),
        (1, BRANCH_W),
        (1, BRANCH_W),
        (1, BRANCH_W),
        (1, BRANCH_W),
        (N_GROUPS, CHUNK, CHUNK),
        (CHUNK, GROUP_W),
        (SHORT_ROWS, BRANCH_W),
    ]
    outs, rode = _host_call(
        body,
        name=f"mixers_bwd_l{layer}",
        grid=(nb, nc),
        in_specs=[
            pl.BlockSpec((None, t_rows, PIECE_COLS), lambda b, r: (b, nc - 1 - r, 0)),
            pl.BlockSpec((None, HALO, PIECE_COLS), lambda b, r: (b, jnp.maximum((nc - 1 - r) * hb - 1, 0), 0)),
            pl.BlockSpec((None, t_rows, N_BRANCH * BRANCH_W), lambda b, r: (b, nc - 1 - r, 0)),
            pl.BlockSpec((None, t_rows, BRANCH_W), lambda b, r: (b, nc - 1 - r, 0)),
        ] + _mixer_weight_specs() + [ANY],
        out_specs=[pl.BlockSpec((None, t_rows, PIECE_COLS), lambda b, r: (b, nc - 1 - r, 0))]
        + [acc_spec(s) for s in acc_shapes],
        out_shape=[jax.ShapeDtypeStruct(dproj3.shape, BF16)] + [jax.ShapeDtypeStruct(s, F32) for s in acc_shapes],
        scratch_shapes=[
            pltpu.VMEM((3, HALO + t_rows + 8, BRANCH_W), F32),
            pltpu.VMEM((3, t_rows + HALO, BRANCH_W), F32),
            pltpu.VMEM((CHUNK, BRANCH_W), F32),
            pltpu.VMEM((CONV_ROWS, 8, BRANCH_W), F32),
            pltpu.VMEM((SHORT_ROWS, 8, BRANCH_W), F32),
            pltpu.VMEM((t_rows + 16, BRANCH_W), F32),
            pltpu.VMEM((t_rows + 8, BRANCH_W), F32),
        ],
        aliases={4 + 11: 0},
        args=[proj3, proj3, dz3, cacc3, *_mixer_weight_args(mw), dproj3],
        ride=ride,
    )
    return outs[0], dict(zip(MIXER_GRADS, outs[1:])), rode


def _merge_fwd(z2, proj2, x2, wbr_l, wo_l, layer, ride=None):
    n = x2.shape[0]
    tm = _tile(n, 1024)

    def body(z_ref, gate_ref, x_ref, wbr_ref, wo_ref, xn_ref, mg_ref, acc_ref):
        nbr = pl.program_id(1)
        bo = _dot(z_ref[...], wbr_ref[nbr])
        contrib = _sigmoid(gate_ref[...].astype(F32)) * bo

        @pl.when(nbr == 0)
        def _():
            acc_ref[...] = contrib

        @pl.when(nbr > 0)
        def _():
            acc_ref[...] += contrib

        @pl.when(nbr == N_BRANCH - 1)
        def _():
            mg = acc_ref[...].astype(BF16)
            mg_ref[...] = mg
            xn_ref[...] = x_ref[...] + _dot(mg, wo_ref[...])

    return _host_call(
        body,
        name=f"merge_fwd_l{layer}",
        grid=(n // tm, N_BRANCH),
        in_specs=[
            pl.BlockSpec((tm, BRANCH_W), lambda i, b: (i, b)),
            pl.BlockSpec((tm, D_MODEL), lambda i, b: (i, GATE_BLOCK0 + b)),
            pl.BlockSpec((tm, D_MODEL), lambda i, b: (i, 0)),
            pl.BlockSpec((N_BRANCH, BRANCH_W, D_MODEL), lambda i, b: (0, 0, 0)),
            pl.BlockSpec((D_MODEL, D_MODEL), lambda i, b: (0, 0)),
        ],
        out_specs=[
            pl.BlockSpec((tm, D_MODEL), lambda i, b: (i, 0)),
            pl.BlockSpec((tm, D_MODEL), lambda i, b: (i, 0)),
        ],
        out_shape=[jax.ShapeDtypeStruct((n, D_MODEL), F32), jax.ShapeDtypeStruct((n, D_MODEL), BF16)],
        scratch_shapes=[pltpu.VMEM((tm, D_MODEL), F32)],
        args=[z2, proj2, x2, wbr_l, wo_l],
        ride=ride,
    )


def _merge_bwd(dout2, merged2, z2, proj2, wbr_l, wo_l, layer, ride=None):
    n = dout2.shape[0]
    tm = _tile(n, 512)
    nt = n // tm

    def body(do_ref, mg_ref, z_ref, gate_ref, wbr_ref, wo_ref, dz_ref, dg_ref, gwo_ref, gwbr_ref, dm_ref,
             awo_ref, awbr_ref):
        i = pl.program_id(0)
        nbr = pl.program_id(1)

        @pl.when(nbr == 0)
        def _():
            do_b = do_ref[...].astype(BF16)
            dm_ref[...] = _dot_nt(do_b, wo_ref[...])
            gw = _dot_tn(mg_ref[...], do_b)

            @pl.when(i == 0)
            def _():
                awo_ref[...] = gw

            @pl.when(i > 0)
            def _():
                awo_ref[...] += gw

        zt = z_ref[...]
        wbr = wbr_ref[nbr]
        bo = _dot(zt, wbr)
        gt = _sigmoid(gate_ref[...].astype(F32))
        dm = dm_ref[...]
        dbo = (dm * gt).astype(BF16)
        dg_ref[...] = (dm * bo * gt * (1.0 - gt)).astype(BF16)
        dz_ref[...] = _dot_nt(dbo, wbr).astype(BF16)
        gw = _dot_tn(zt, dbo)

        @pl.when(i == 0)
        def _():
            awbr_ref[nbr] = gw

        @pl.when(i > 0)
        def _():
            awbr_ref[nbr] += gw

        @pl.when(jnp.logical_and(i == nt - 1, nbr == N_BRANCH - 1))
        def _():
            pltpu.sync_copy(awo_ref, gwo_ref)
            pltpu.sync_copy(awbr_ref, gwbr_ref)

    return _host_call(
        body,
        name=f"merge_bwd_l{layer}",
        grid=(nt, N_BRANCH),
        in_specs=[
            pl.BlockSpec((tm, D_MODEL), lambda i, b: (i, 0)),
            pl.BlockSpec((tm, D_MODEL), lambda i, b: (i, 0)),
            pl.BlockSpec((tm, BRANCH_W), lambda i, b: (i, b)),
            pl.BlockSpec((tm, D_MODEL), lambda i, b: (i, GATE_BLOCK0 + b)),
            pl.BlockSpec((N_BRANCH, BRANCH_W, D_MODEL), lambda i, b: (0, 0, 0)),
            pl.BlockSpec((D_MODEL, D_MODEL), lambda i, b: (0, 0)),
        ],
        out_specs=[
            pl.BlockSpec((tm, BRANCH_W), lambda i, b: (i, b)),
            pl.BlockSpec((tm, D_MODEL), lambda i, b: (i, GATE_BLOCK0 + b)),
            ANY,
            ANY,
        ],
        out_shape=[
            jax.ShapeDtypeStruct((n, N_BRANCH * BRANCH_W), BF16),
            jax.ShapeDtypeStruct((n, IN_COLS), BF16),
            jax.ShapeDtypeStruct((D_MODEL, D_MODEL), F32),
            jax.ShapeDtypeStruct((N_BRANCH, BRANCH_W, D_MODEL), F32),
        ],
        scratch_shapes=[
            pltpu.VMEM((tm, D_MODEL), F32),
            pltpu.VMEM((D_MODEL, D_MODEL), F32),
            pltpu.VMEM((N_BRANCH, BRANCH_W, D_MODEL), F32),
        ],
        args=[dout2, merged2, z2, proj2, wbr_l, wo_l],
        ride=ride,
    )


def _loss_head(x2, g_row, tgt2):
    n = x2.shape[0]
    tm = _tile(n, 512)

    def body(x_ref, g_ref, t_ref, dx_ref, loss_ref, dg_ref):
        @pl.when(pl.program_id(0) == 0)
        def _():
            loss_ref[...] = jnp.zeros_like(loss_ref)
            dg_ref[...] = jnp.zeros_like(dg_ref)

        xv = x_ref[...]
        g = g_ref[...]
        r = lax.rsqrt(_lanemean(xv * xv) + RMS_EPS)
        xh = xv * r
        err = xh * g - t_ref[...]
        loss_ref[...] += 0.5 * jnp.sum(_lanemean(err * err), axis=0, keepdims=True)
        dy = err * (1.0 / D_MODEL)
        dg_ref[...] += _rowsum(dy * xh)
        dxh = dy * g
        dx_ref[...] = r * (dxh - xh * _lanemean(dxh * xh))

    return pl.pallas_call(
        body,
        name="loss_head",
        grid=(n // tm,),
        in_specs=[
            pl.BlockSpec((tm, D_MODEL), lambda i: (i, 0)),
            pl.BlockSpec((1, D_MODEL), lambda i: (0, 0)),
            pl.BlockSpec((tm, D_MODEL), lambda i: (i, 0)),
        ],
        out_specs=[
            pl.BlockSpec((tm, D_MODEL), lambda i: (i, 0)),
            pl.BlockSpec((1, GROUP_W), lambda i: (0, 0)),
            pl.BlockSpec((1, D_MODEL), lambda i: (0, 0)),
        ],
        out_shape=[jax.ShapeDtypeStruct((n, D_MODEL), F32), jax.ShapeDtypeStruct((1, GROUP_W), F32),
                   jax.ShapeDtypeStruct((1, D_MODEL), F32)],
        compiler_params=_params("arbitrary"),
    )(x2, g_row, tgt2)


def _inproj_bwd_x(dproj2, w_l, x2, g_row, dout2, layer, ride=None):
    n = x2.shape[0]
    tm = _tile(n, 1024)

    def body(dp_ref, w_ref, x_ref, g_ref, do_ref, dx_ref, dng_ref, dh_ref):
        i = pl.program_id(0)
        s = pl.program_id(1)
        part = _dot_nt(dp_ref[...], w_ref[...])

        @pl.when(s == 0)
        def _():
            dh_ref[...] = part

        @pl.when(s > 0)
        def _():
            dh_ref[...] += part

        @pl.when(jnp.logical_and(i == 0, s == 0))
        def _():
            dng_ref[...] = jnp.zeros_like(dng_ref)

        @pl.when(s == N_COL_TILES - 1)
        def _():
            xv = x_ref[...]
            r = lax.rsqrt(_lanemean(xv * xv) + RMS_EPS)
            xh = xv * r
            dh = dh_ref[...]
            dng_ref[...] += _rowsum(dh * xh)
            dxh = dh * g_ref[...]
            dx_ref[...] = do_ref[...] + r * (dxh - xh * _lanemean(dxh * xh))

    return _host_call(
        body,
        name=f"inproj_bwd_x_l{layer}",
        grid=(n // tm, N_COL_TILES),
        in_specs=[
            pl.BlockSpec((tm, COL_TILE), lambda i, s: (i, s)),
            pl.BlockSpec((None, D_MODEL, COL_TILE), lambda i, s: (s // 2, 0, s % 2)),
            pl.BlockSpec((tm, D_MODEL), lambda i, s: (i, 0)),
            pl.BlockSpec((1, D_MODEL), lambda i, s: (0, 0)),
            pl.BlockSpec((tm, D_MODEL), lambda i, s: (i, 0)),
        ],
        out_specs=[
            pl.BlockSpec((tm, D_MODEL), lambda i, s: (i, 0)),
            pl.BlockSpec((1, D_MODEL), lambda i, s: (0, 0)),
        ],
        out_shape=[jax.ShapeDtypeStruct((n, D_MODEL), F32), jax.ShapeDtypeStruct((1, D_MODEL), F32)],
        scratch_shapes=[pltpu.VMEM((tm, D_MODEL), F32)],
        args=[dproj2, w_l, x2, g_row, dout2],
        ride=ride,
    )


def _inproj_bwd_w(h2, dproj2, layer, ride=None):
    n = h2.shape[0]
    tm = _tile(n, 1024)

    def body(h_ref, dp_ref, gw_ref):
        gw = _dot_tn(h_ref[...], dp_ref[...])

        @pl.when(pl.program_id(1) == 0)
        def _():
            gw_ref[...] = gw

        @pl.when(pl.program_id(1) > 0)
        def _():
            gw_ref[...] += gw

    outs, rode = _host_call(
        body,
        name=f"inproj_bwd_w_l{layer}",
        grid=(N_COL_TILES, n // tm),
        in_specs=[
            pl.BlockSpec((tm, D_MODEL), lambda s, i: (i, 0)),
            pl.BlockSpec((tm, COL_TILE), lambda s, i: (i, s)),
        ],
        out_specs=[pl.BlockSpec((None, D_MODEL, COL_TILE), lambda s, i: (s // 2, 0, s % 2))],
        out_shape=[jax.ShapeDtypeStruct((N_CHIPS, D_MODEL, SHARD_COLS), F32)],
        args=[h2, dproj2],
        ride=ride,
    )
    return outs[0], rode


def _adamw_math(w, g, m, v):
    m = ADAM_B1 * m + (1.0 - ADAM_B1) * g
    v = ADAM_B2 * v + (1.0 - ADAM_B2) * jnp.square(g)
    m_hat = m / (1.0 - ADAM_B1 ** ADAM_STEP)
    v_hat = v / (1.0 - ADAM_B2 ** ADAM_STEP)
    delta = -ADAM_LR * (m_hat / (jnp.sqrt(v_hat) + ADAM_EPS) + ADAM_WD * w)
    return delta, m, v


def _adamw_sharded(w, m, v, part, layer, prev, name, ride=None):
    _, rows, cols = w.shape
    tr = _tile(rows, max(16, (1 << 19) // (4 * cols) // 16 * 16), mult=16)

    def body(w_ref, m_ref, v_ref, p_ref, *rest):
        g_out, d_out, m_out, v_out = rest[-4:]
        g = ((p_ref[0].astype(F32) + p_ref[1].astype(F32)) + p_ref[2].astype(F32)) + p_ref[3].astype(F32)
        delta, m2, v2 = _adamw_math(w_ref[...], g, m_ref[...], v_ref[...])
        g_out[...] = g
        d_out[...] = delta
        m_out[...] = m2
        v_out[...] = v2

    spec = pl.BlockSpec((None, tr, cols), lambda i: (layer, i, 0))
    return _host_call(
        body,
        name=f"adamw_{name}_l{layer}",
        grid=(rows // tr,),
        in_specs=[spec, spec, spec, pl.BlockSpec((N_CHIPS, tr, cols), lambda i: (0, i, 0))]
        + ([ANY] * 4 if prev else []),
        out_specs=[spec] * 4,
        out_shape=[jax.ShapeDtypeStruct(w.shape, F32)] * 4,
        args=[w, m, v, part] + (list(prev) if prev else []),
        aliases={4 + k: k for k in range(4)} if prev else {},
        ride=ride,
    )


def _adamw_packed(w, m, v, g):
    rows = w.shape[0]
    tr = _tile(rows, rows // 2 if rows % 16 == 0 else rows)

    def body(w_ref, m_ref, v_ref, g_ref, d_out, m_out, v_out):
        delta, m2, v2 = _adamw_math(w_ref[...], g_ref[...], m_ref[...], v_ref[...])
        d_out[...] = delta
        m_out[...] = m2
        v_out[...] = v2

    spec = pl.BlockSpec((tr, GROUP_W), lambda i: (i, 0))
    return pl.pallas_call(
        body,
        name="adamw_small",
        grid=(rows // tr,),
        in_specs=[spec] * 4,
        out_specs=[spec] * 3,
        out_shape=[jax.ShapeDtypeStruct(w.shape, F32)] * 3,
        compiler_params=_params("arbitrary"),
    )(w, m, v, g)


SMALL = ["norm_g", "pool_w", "pool_scale", "conv_b", "conv_ln_g", "conv_ln_b", "sgu_ln_g", "sgu_ln_b", "sgu_w",
         "sgu_b", "final_g"]
WEIGHTS = ["norm_g", "w_in", "pool_w", "pool_scale", "conv_w", "conv_b", "conv_ln_g", "conv_ln_b", "sgu_ln_g",
           "sgu_ln_b", "sgu_w", "sgu_b", "sc_w", "w_branch", "w_o", "final_g"]
HALF_SHAPES = [(D_MODEL // 2, SHARD_COLS), (N_BRANCH * BRANCH_W // 2, BR_SHARD), (BR_SHARD // 2, D_MODEL),
               (CS_ROWS // 2, GROUP_W)]


def _pack_small(tree):
    return jnp.concatenate([tree[k].reshape(-1, GROUP_W) for k in SMALL], axis=0)


def _unpack_small(packed, like):
    out, r = {}, 0
    for k in SMALL:
        nr = like[k].size // GROUP_W
        out[k] = packed[r:r + nr].reshape(like[k].shape)
        r += nr
    return out


def _pad_rows(a, rows):
    pad = [(0, 0)] * a.ndim
    pad[-2] = (0, rows - a.shape[-2])
    return jnp.pad(a, pad)


def _pack_cs(conv, short):
    return jnp.concatenate([_pad_rows(conv, CS_ROWS // 2), _pad_rows(short, CS_ROWS // 2)], axis=-2)


def _shard_major_rows(a):
    return a.reshape(a.shape[0], N_CHIPS, GROUP_W).transpose(1, 0, 2)


def kernel(x, norm_g, w_in, pool_w, pool_scale, conv_w, conv_b, conv_ln_g, conv_ln_b, sgu_ln_g, sgu_ln_b, sgu_w, sgu_b, sc_w, w_branch, w_o, final_g, loss_target, m_norm_g, m_w_in, m_pool_w, m_pool_scale, m_conv_w, m_conv_b, m_conv_ln_g, m_conv_ln_b, m_sgu_ln_g, m_sgu_ln_b, m_sgu_w, m_sgu_b, m_sc_w, m_w_branch, m_w_o, m_final_g, v_norm_g, v_w_in, v_pool_w, v_pool_scale, v_conv_w, v_conv_b, v_conv_ln_g, v_conv_ln_b, v_sgu_ln_g, v_sgu_ln_b, v_sgu_w, v_sgu_b, v_sc_w, v_w_branch, v_w_o, v_final_g):
    w = dict(norm_g=norm_g, w_in=w_in, pool_w=pool_w, pool_scale=pool_scale, conv_w=conv_w, conv_b=conv_b,
             conv_ln_g=conv_ln_g, conv_ln_b=conv_ln_b, sgu_ln_g=sgu_ln_g, sgu_ln_b=sgu_ln_b, sgu_w=sgu_w,
             sgu_b=sgu_b, sc_w=sc_w, w_branch=w_branch, w_o=w_o, final_g=final_g)
    mom = dict(norm_g=m_norm_g, w_in=m_w_in, pool_w=m_pool_w, pool_scale=m_pool_scale, conv_w=m_conv_w,
               conv_b=m_conv_b, conv_ln_g=m_conv_ln_g, conv_ln_b=m_conv_ln_b, sgu_ln_g=m_sgu_ln_g,
               sgu_ln_b=m_sgu_ln_b, sgu_w=m_sgu_w, sgu_b=m_sgu_b, sc_w=m_sc_w, w_branch=m_w_branch, w_o=m_w_o,
               final_g=m_final_g)
    var = dict(norm_g=v_norm_g, w_in=v_w_in, pool_w=v_pool_w, pool_scale=v_pool_scale, conv_w=v_conv_w,
               conv_b=v_conv_b, conv_ln_g=v_conv_ln_g, conv_ln_b=v_conv_ln_b, sgu_ln_g=v_sgu_ln_g,
               sgu_ln_b=v_sgu_ln_b, sgu_w=v_sgu_w, sgu_b=v_sgu_b, sc_w=v_sc_w, w_branch=v_w_branch, w_o=v_w_o,
               final_g=v_final_g)

    nb, seq, _ = x.shape
    n = nb * seq
    core = lax.axis_index("c").astype(jnp.int32).reshape(1)

    win_s = w_in.astype(BF16).reshape(DEPTH, 2, D_MODEL // 2, SHARD_COLS)
    wbr_s = w_branch.astype(BF16)
    wo_s = w_o.astype(BF16)
    cs_s = _pack_cs(conv_w, sc_w)

    (win_f0,) = _allgather_layer0([win_s])

    def layer_weights(win_f, wbr_f, wo_f, cs_f):
        cs_f = cs_f.reshape(N_CHIPS, CS_ROWS, GROUP_W)
        return dict(win=win_f.reshape(N_CHIPS, D_MODEL, SHARD_COLS),
                    wbr=wbr_f.reshape(N_CHIPS, N_BRANCH, BRANCH_W, BR_SHARD).transpose(1, 2, 0, 3).reshape(
                        N_BRANCH, BRANCH_W, D_MODEL),
                    wo=wo_f.reshape(D_MODEL, D_MODEL),
                    conv_w=cs_f[:, :CONV_ROWS], sc_w=cs_f[:, CS_ROWS // 2:CS_ROWS // 2 + SHORT_ROWS])

    def mixer_weights(l, gathered):
        row = lambda a: a[l].reshape(1, BRANCH_W)
        bias = jnp.repeat(jnp.swapaxes(sgu_b[l], 0, 1), GROUP_W, axis=1)
        return dict(pool_w=pool_w[l].astype(BF16), pool_scale=row(pool_scale), conv_w=gathered["conv_w"],
                    conv_b=row(conv_b), conv_ln_g=row(conv_ln_g), conv_ln_b=row(conv_ln_b), sgu_ln_g=row(sgu_ln_g),
                    sgu_ln_b=row(sgu_ln_b), sgu_w=sgu_w[l], sgu_bias=bias, sc_w=gathered["sc_w"])

    lw, mw = [None, None], [None, None]

    xs, projs, hs, zs, mgs, caccs = [x.reshape(n, D_MODEL)], [], [], [], [], []
    for l in range(DEPTH):
        first = l == 0
        win_l = win_f0.reshape(N_CHIPS, D_MODEL, SHARD_COLS) if first else lw[1]["win"]
        ride = _join(_gather_whole([wbr_s, wo_s, cs_s], 0), _gather_layer1_first_half(win_s)) if first else None
        (proj, h), rode = _inproj_fwd(xs[l], norm_g[l:l + 1], win_l, l, ride=ride)
        if first:
            lw[0] = layer_weights(win_f0, *rode[:3])
            mw[0] = mixer_weights(0, lw[0])
            win_f1 = rode[3]
        (z3, cacc), rode = _mixers_fwd(proj.reshape(nb, seq, IN_COLS), mw[l], l,
                                       ride=_gather_layer1_second_half(win_s, win_f1) if first else None)
        caccs.append(cacc)
        if first:
            win_f1 = rode[0]
        z = z3.reshape(n, N_BRANCH * BRANCH_W)
        (x_new, merged), rode = _merge_fwd(z, proj, xs[l], lw[l]["wbr"], lw[l]["wo"], l,
                                           ride=_gather_whole([wbr_s, wo_s, cs_s], 1) if first else None)
        if first:
            lw[1] = layer_weights(win_f1, *rode)
            mw[1] = mixer_weights(1, lw[1])
        projs.append(proj)
        hs.append(h)
        zs.append(z)
        mgs.append(merged)
        xs.append(x_new)

    dx, loss_part, g_final = _loss_head(xs[DEPTH], final_g.reshape(1, D_MODEL), loss_target.reshape(n, D_MODEL))
    loss = lax.psum(loss_part[0, 0], ("x", "y", "c"))

    def chip_major(gwbr):
        return gwbr.reshape(N_BRANCH, BRANCH_W, N_CHIPS, BR_SHARD).transpose(2, 0, 1, 3)

    def as_halves(arrays, first_index=0):
        return [g.reshape((N_CHIPS, 2) + s) for g, s in zip(arrays, HALF_SHAPES[first_index:])]

    def add_halves(grads, others, l, first_index=0):
        return [_add_halves(g, o, core, f"add_halves_l{l}_{first_index + i}")
                for i, (g, o) in enumerate(zip(grads, others))]

    per_layer, parts = {}, [None] * DEPTH
    (dz, dproj, gwo, gwbr), _ = _merge_bwd(dx, mgs[1], zs[1], projs[1], lw[1]["wbr"], lw[1]["wo"], 1)
    dproj3, gm, _ = _mixers_bwd(projs[1].reshape(nb, seq, IN_COLS), dz.reshape(nb, seq, N_BRANCH * BRANCH_W), caccs[1],
                                dproj.reshape(nb, seq, IN_COLS), mw[1], 1)
    dproj = dproj3.reshape(n, IN_COLS)
    gwin, _ = _inproj_bwd_w(hs[1], dproj, 1)
    gcs = _pack_cs(_shard_major_rows(gm["conv_w"]), _shard_major_rows(gm["sc_w"]))
    grads1 = as_halves([gwin, chip_major(gwbr), gwo, gcs])
    (dx, g_norm), others1 = _inproj_bwd_x(dproj, lw[1]["win"], xs[1], norm_g[1:2], dx, 1, ride=_swap_halves(grads1))
    gm["norm_g"] = g_norm
    per_layer[1] = gm
    sums1 = add_halves(grads1, others1, 1)
    (dz, dproj, gwo, gwbr), scattered1 = _merge_bwd(dx, mgs[0], zs[0], projs[0], lw[0]["wbr"], lw[0]["wo"], 0,
                                                    ride=_scatter_chip_sums(sums1))
    dproj3, gm, parts[1] = _mixers_bwd(projs[0].reshape(nb, seq, IN_COLS), dz.reshape(nb, seq, N_BRANCH * BRANCH_W),
                                       caccs[0], dproj.reshape(nb, seq, IN_COLS), mw[0], 0,
                                       ride=_share_halves(scattered1))
    dproj = dproj3.reshape(n, IN_COLS)
    gcs = _pack_cs(_shard_major_rows(gm["conv_w"]), _shard_major_rows(gm["sc_w"]))
    small0 = as_halves([chip_major(gwbr), gwo, gcs], first_index=1)
    gwin, others_small0 = _inproj_bwd_w(hs[0], dproj, 0, ride=_swap_halves(small0))
    (gwin_h,) = as_halves([gwin])

    def flat(a, i):
        rows, cols = 2 * HALF_SHAPES[i][0], HALF_SHAPES[i][1]
        return a.reshape(a.shape[0], rows, cols)

    names4 = ["w_in", "w_branch", "w_o", "conv_sc"]

    def packed4(t):
        return [t["w_in"], t["w_branch"], t["w_o"], _pack_cs(t["conv_w"], t["sc_w"])]

    w4, m4, v4 = ([flat(a, i) for i, a in enumerate(packed4(t))] for t in (w, mom, var))

    def sharded_update(i, layer, prev, ride=None):
        return _adamw_sharded(w4[i], m4[i], v4[i], flat(parts[layer][i], i), layer, prev, names4[i], ride=ride)

    updated = []
    for i in range(4):
        res, rode = sharded_update(i, 1, None, ride=_swap_halves([gwin_h]) if i == 0 else None)
        if i == 0:
            (other_win0,) = rode
        updated.append(res)

    sums0 = add_halves([gwin_h], [other_win0], 0) + add_halves(small0, others_small0, 0, first_index=1)
    (dx, g_norm), scattered0 = _inproj_bwd_x(dproj, lw[0]["win"], xs[0], norm_g[0:1], dx, 0,
                                             ride=_scatter_chip_sums(sums0))
    gm["norm_g"] = g_norm
    per_layer[0] = gm
    grad_x = dx.reshape(nb, seq, D_MODEL)

    def stack(k, shape):
        return jnp.stack([per_layer[l][k] for l in range(DEPTH)]).reshape(shape)

    g_small = {"final_g": g_final.reshape(D_MODEL), "norm_g": stack("norm_g", (DEPTH, D_MODEL)),
               "pool_w": stack("pool_w", pool_w.shape), "sgu_w": stack("sgu_w", sgu_w.shape),
               "sgu_b": jnp.swapaxes(stack("sgu_b", (DEPTH, CHUNK, GROUP_W))[:, :, :N_GROUPS], 1, 2)}
    for k in ("pool_scale", "conv_b", "conv_ln_g", "conv_ln_b", "sgu_ln_g", "sgu_ln_b"):
        g_small[k] = stack(k, (DEPTH, BRANCH_W))
    g_packed, parts[0] = _allreduce_small(_pack_small(g_small), _share_halves(scattered0))
    d_packed, m_packed, v_packed = _adamw_packed(_pack_small(w), _pack_small(mom), _pack_small(var), g_packed)
    grads, deltas, new_m, new_v = {}, {}, {}, {}
    for tree, packed in ((grads, g_packed), (deltas, d_packed), (new_m, m_packed), (new_v, v_packed)):
        tree.update(_unpack_small(packed, w))

    for i, name in enumerate(names4):
        res, _ = sharded_update(i, 0, updated[i])
        for tree, r in zip((grads, deltas, new_m, new_v), res):
            if name == "conv_sc":
                tree["conv_w"] = r[:, :CONV_K]
                tree["sc_w"] = r[:, CS_ROWS // 2:CS_ROWS // 2 + SHORT_K]
            else:
                tree[name] = r.reshape(w[name].shape)

    return (loss, grad_x, *[grads[k] for k in WEIGHTS], *[deltas[k] for k in WEIGHTS],
            *[new_m[k] for k in WEIGHTS], *[new_v[k] for k in WEIGHTS])
```

```python
import functools

import jax
import jax.numpy as jnp
from jax import lax
from jax.experimental import pallas as pl
from jax.experimental.pallas import tpu as pltpu

F32 = jnp.float32
BF16 = jnp.bfloat16

D_MODEL = 1024
DEPTH = 2
N_BRANCH = 4
BRANCH_W = 512
N_GROUPS = 4
GROUP_W = 128
POOL_WINDOWS = (2, 4, 8, 16)
CONV_K = 31
SHORT_K = 3
CHUNK = 128
N_PIECES = 12
PIECE_COLS = N_PIECES * BRANCH_W
IN_COLS = PIECE_COLS + N_BRANCH * D_MODEL
N_CHIPS = 4
SHARD_COLS = IN_COLS // N_CHIPS
BR_SHARD = D_MODEL // N_CHIPS
COL_TILE = SHARD_COLS // 2
N_COL_TILES = IN_COLS // COL_TILE
GATE_BLOCK0 = PIECE_COLS // D_MODEL
RMS_EPS = 1e-6
LN_EPS = 1e-5
HALO = 32
CONV_ROWS = 32
SHORT_ROWS = 8
CS_ROWS = 64

ADAM_LR = 0.001
ADAM_B1 = 0.9
ADAM_B2 = 0.999
ADAM_EPS = 1e-08
ADAM_WD = 0.01
ADAM_STEP = 10

VMEM_LIMIT = 52 * 1024 * 1024
MESH_ID = pl.DeviceIdType.MESH
ANY = pl.BlockSpec(memory_space=pl.ANY)
VMEM_WHOLE = pl.BlockSpec(memory_space=pltpu.VMEM)

(P_X, P_GATE, C_A, C_B, C_GATE, G_U, G_V, G_GATE, S_B, S_C, S_X, S_GATE) = range(N_PIECES)


def _params(*sem):
    return pltpu.CompilerParams(dimension_semantics=sem, vmem_limit_bytes=VMEM_LIMIT)


def _sigmoid(v):
    return 0.5 * jnp.tanh(0.5 * v) + 0.5


def _silu(v):
    return v * _sigmoid(v)


def _silu_pair(v):
    s = _sigmoid(v)
    return v * s, s * (1.0 + v * (1.0 - s))


def _dot(a, b):
    return jnp.dot(a, b, preferred_element_type=F32)


def _dot_nt(a, b):
    return lax.dot_general(a, b, (((1,), (1,)), ((), ())), preferred_element_type=F32)


def _dot_tn(a, b):
    return lax.dot_general(a, b, (((0,), (0,)), ((), ())), preferred_element_type=F32)


def _rowsum(v):
    return jnp.sum(v, axis=0, keepdims=True)


def _lanemean(v):
    return jnp.mean(v, axis=-1, keepdims=True)


def _tile(n, want, mult=8):
    t = max(1, min(n, want))
    while n % t or (t % mult and t != n):
        t -= 1
    return t


def _place():
    x, y, c = lax.axis_index("x"), lax.axis_index("y"), lax.axis_index("c")
    chip = 2 * x + y
    peers = [(1 - x, y), (x, 1 - y), (1 - x, 1 - y)]
    return x, y, c, chip, peers


def _remote(src, dst, send_sem, recv_sem, dev):
    return pltpu.make_async_remote_copy(src_ref=src, dst_ref=dst, send_sem=send_sem, recv_sem=recv_sem,
                                        device_id=dev, device_id_type=MESH_ID)


class _Exchange:
    def __init__(self, inputs, out_shapes, plan, n_remote, n_local=0, aliases=None):
        self.inputs = list(inputs)
        self.out_shapes = list(out_shapes)
        self.plan = plan
        self.n_remote = n_remote
        self.n_local = n_local
        self.aliases = dict(aliases or {})

    def copies(self, in_refs, out_refs, send_sems, recv_sems, loc_sems):
        remote, local = self.plan(in_refs, out_refs)
        assert len(remote) == self.n_remote and len(local) == self.n_local
        cps = [_remote(s, d, send_sems.at[t], recv_sems.at[t], dev) for t, (s, d, dev) in enumerate(remote)]
        cps += [pltpu.make_async_copy(s, d, loc_sems.at[t]) for t, (s, d) in enumerate(local)]
        return cps

    def sem_shapes(self):
        return [pltpu.SemaphoreType.DMA((max(self.n_remote, 1),)), pltpu.SemaphoreType.DMA((max(self.n_remote, 1),)),
                pltpu.SemaphoreType.DMA((max(self.n_local, 1),))]


def _exchange_call(name, ex):
    n_ci = len(ex.inputs)

    def body(*refs):
        cins, couts = refs[:n_ci], refs[n_ci:n_ci + len(ex.out_shapes)]
        cps = ex.copies(cins, couts, *refs[n_ci + len(ex.out_shapes):])
        for cp in cps:
            cp.start()
        for cp in cps:
            cp.wait()

    return pl.pallas_call(
        body,
        name=name,
        in_specs=[ANY] * n_ci,
        out_specs=[ANY] * len(ex.out_shapes),
        out_shape=ex.out_shapes,
        scratch_shapes=ex.sem_shapes(),
        input_output_aliases=ex.aliases,
    )(*ex.inputs)


def _host_call(body, *, name, grid, in_specs, out_specs, out_shape, args, scratch_shapes=(), aliases=None, ride=None):
    n_in, n_out, n_scr = len(in_specs), len(out_specs), len(scratch_shapes)
    sem = ("arbitrary",) * len(grid)
    aliases = dict(aliases or {})
    if ride is None:
        outs = pl.pallas_call(body, name=name, grid=grid, in_specs=list(in_specs), out_specs=list(out_specs),
                              out_shape=list(out_shape), scratch_shapes=list(scratch_shapes),
                              input_output_aliases=aliases, compiler_params=_params(*sem))(*args)
        return list(outs), []
    n_ci, n_co = len(ride.inputs), len(ride.out_shapes)

    def full_body(*refs):
        ins, cins = refs[:n_in], refs[n_in:n_in + n_ci]
        o0 = n_in + n_ci
        outs, couts = refs[o0:o0 + n_out], refs[o0 + n_out:o0 + n_out + n_co]
        s0 = o0 + n_out + n_co
        scr, sems = refs[s0:s0 + n_scr], refs[s0 + n_scr:]
        first = functools.reduce(jnp.logical_and, [pl.program_id(d) == 0 for d in range(len(grid))])
        last = functools.reduce(jnp.logical_and, [pl.program_id(d) == grid[d] - 1 for d in range(len(grid))])

        @pl.when(first)
        def _():
            for cp in ride.copies(cins, couts, *sems):
                cp.start()

        body(*ins, *outs, *scr)

        @pl.when(last)
        def _():
            for cp in ride.copies(cins, couts, *sems):
                cp.wait()

    for ci, co in ride.aliases.items():
        aliases[n_in + ci] = n_out + co
    outs = pl.pallas_call(
        full_body, name=name, grid=grid, in_specs=list(in_specs) + [ANY] * n_ci,
        out_specs=list(out_specs) + [ANY] * n_co, out_shape=list(out_shape) + ride.out_shapes,
        scratch_shapes=list(scratch_shapes) + ride.sem_shapes(), input_output_aliases=aliases,
        compiler_params=_params(*sem))(*args, *ride.inputs)
    return list(outs[:n_out]), list(outs[n_out:])


def _allgather_layer0(shards):
    na = len(shards)

    def body(*refs):
        ins, outs = refs[:na], refs[na:2 * na]
        send_sems, recv_sems, fsend_sems, frecv_sems, loc_sems = refs[2 * na:]
        x, y, c, chip, peers = _place()
        sib = (x, y, 1 - c)
        locs = [pltpu.make_async_copy(ins[a].at[0], outs[a].at[chip], loc_sems.at[a]) for a in range(na)]
        for cp in locs:
            cp.start()
        pending = []
        for k, (px, py) in enumerate(peers):
            for a in range(na):
                cp = _remote(ins[a].at[0, c], outs[a].at[chip, c], send_sems.at[k * na + a],
                             recv_sems.at[k * na + a], (px, py, c))
                cp.start()
                pending.append(cp)
        for k, (px, py) in enumerate(peers):
            pchip = 2 * px + py
            for a in range(na):
                slab = outs[a].at[pchip, c]
                _remote(slab, slab, send_sems.at[k * na + a], recv_sems.at[k * na + a], (px, py, c)).wait_recv()
                cp = _remote(slab, slab, fsend_sems.at[k * na + a], frecv_sems.at[k * na + a], sib)
                cp.start()
                pending.append(cp)
        for k, (px, py) in enumerate(peers):
            pchip = 2 * px + py
            for a in range(na):
                slab = outs[a].at[pchip, 1 - c]
                _remote(slab, slab, fsend_sems.at[k * na + a], frecv_sems.at[k * na + a], sib).wait_recv()
        for cp in pending:
            cp.wait_send()
        for cp in locs:
            cp.wait()

    return pl.pallas_call(
        body,
        name="allgather_layer0",
        in_specs=[ANY] * na,
        out_specs=[ANY] * na,
        out_shape=[jax.ShapeDtypeStruct((N_CHIPS,) + a.shape[1:], a.dtype) for a in shards],
        scratch_shapes=[pltpu.SemaphoreType.DMA((3 * na,))] * 4 + [pltpu.SemaphoreType.DMA((na,))],
    )(*shards)


def _gather_layer1_first_half(win_s):
    def plan(cins, couts):
        _, _, c, chip, peers = _place()
        (src,), (dst,) = cins, couts
        return ([(src.at[1, 0], dst.at[chip, 0], (px, py, c)) for px, py in peers],
                [(src.at[1], dst.at[chip])])

    return _Exchange([win_s], [jax.ShapeDtypeStruct((N_CHIPS,) + win_s.shape[1:], win_s.dtype)], plan, 3, 1)


def _gather_layer1_second_half(win_s, win_f1):
    def plan(cins, couts):
        _, _, c, chip, peers = _place()
        src, (dst,) = cins[0], couts
        return [(src.at[1, 1], dst.at[chip, 1], (px, py, c)) for px, py in peers], []

    return _Exchange([win_s, win_f1], [jax.ShapeDtypeStruct(win_f1.shape, win_f1.dtype)], plan, 3, 0, aliases={1: 0})


def _gather_whole(shards, layer):
    na = len(shards)

    def plan(cins, couts):
        _, _, c, chip, peers = _place()
        remote = [(cins[a].at[layer], couts[a].at[chip], (px, py, c)) for a in range(na) for px, py in peers]
        return remote, [(cins[a].at[layer], couts[a].at[chip]) for a in range(na)]

    return _Exchange(shards, [jax.ShapeDtypeStruct((N_CHIPS,) + a.shape[1:], a.dtype) for a in shards], plan,
                     3 * na, na)


def _join(a, b):
    n_in, n_out = len(a.inputs), len(a.out_shapes)

    def plan(cins, couts):
        remote_a, local_a = a.plan(cins[:n_in], couts[:n_out])
        remote_b, local_b = b.plan(cins[n_in:], couts[n_out:])
        return remote_a + remote_b, local_a + local_b

    aliases = dict(a.aliases)
    aliases.update({n_in + i: n_out + o for i, o in b.aliases.items()})
    return _Exchange(a.inputs + b.inputs, a.out_shapes + b.out_shapes, plan, a.n_remote + b.n_remote,
                     a.n_local + b.n_local, aliases)


def _swap_halves(grads):
    def plan(cins, couts):
        x, y, c, _, _ = _place()
        return [(g.at[:, 1 - c], r, (x, y, 1 - c)) for g, r in zip(cins, couts)], []

    return _Exchange(grads, [jax.ShapeDtypeStruct(g.shape[:1] + g.shape[2:], g.dtype) for g in grads], plan,
                     len(grads))


def _scatter_chip_sums(sums):
    na = len(sums)

    def plan(cins, couts):
        _, _, c, chip, peers = _place()
        remote = [(cins[a].at[2 * px + py], couts[a].at[chip, c], (px, py, c)) for a in range(na) for px, py in peers]
        return remote, [(cins[a].at[chip], couts[a].at[chip, c]) for a in range(na)]

    return _Exchange(sums, [jax.ShapeDtypeStruct((N_CHIPS, 2) + s.shape[1:], s.dtype) for s in sums], plan,
                     3 * na, na)


def _share_halves(parts):
    na = len(parts)

    def plan(cins, couts):
        x, y, c, _, _ = _place()
        return [(p.at[:, c], p.at[:, c], (x, y, 1 - c)) for p in couts], []

    return _Exchange(parts, [jax.ShapeDtypeStruct(p.shape, p.dtype) for p in parts], plan, na, 0,
                     aliases={a: a for a in range(na)})


def _allreduce_small(packed, ride):
    rows = packed.shape[0]
    n_ci, n_co = len(ride.inputs), len(ride.out_shapes)

    def body(*refs):
        sg_ref, cins = refs[0], refs[1:1 + n_ci]
        res_ref, couts = refs[1 + n_ci], refs[2 + n_ci:2 + n_ci + n_co]
        sib_ref, cs_ref, rem_ref, s1_send, s1_recv, s2_send, s2_recv = refs[2 + n_ci + n_co:9 + n_ci + n_co]
        riding = ride.copies(cins, couts, *refs[9 + n_ci + n_co:])
        for cp in riding:
            cp.start()
        x, y, c, chip, peers = _place()
        cp = _remote(sg_ref, sib_ref, s1_send, s1_recv, (x, y, 1 - c))
        cp.start()
        cp.wait()
        cs_ref[...] = sg_ref[...] + sib_ref[...]
        rem_ref[chip] = cs_ref[...]
        cps = [_remote(cs_ref, rem_ref.at[chip], s2_send.at[k], s2_recv.at[chip], (px, py, c))
               for k, (px, py) in enumerate(peers)]
        for cp in cps:
            cp.start()
        for k, (px, py) in enumerate(peers):
            pchip = 2 * px + py
            _remote(cs_ref, rem_ref.at[pchip], s2_send.at[k], s2_recv.at[pchip], (px, py, c)).wait_recv()
        for cp in cps:
            cp.wait_send()
        res_ref[...] = ((rem_ref[0] + rem_ref[1]) + rem_ref[2]) + rem_ref[3]
        for cp in riding:
            cp.wait()

    outs = pl.pallas_call(
        body,
        name="allreduce_small",
        in_specs=[VMEM_WHOLE] + [ANY] * n_ci,
        out_specs=[VMEM_WHOLE] + [ANY] * n_co,
        out_shape=[jax.ShapeDtypeStruct((rows, GROUP_W), F32)] + ride.out_shapes,
        scratch_shapes=[
            pltpu.VMEM((rows, GROUP_W), F32),
            pltpu.VMEM((rows, GROUP_W), F32),
            pltpu.VMEM((N_CHIPS, rows, GROUP_W), F32),
            pltpu.SemaphoreType.DMA,
            pltpu.SemaphoreType.DMA,
            pltpu.SemaphoreType.DMA((3,)),
            pltpu.SemaphoreType.DMA((N_CHIPS,)),
        ] + ride.sem_shapes(),
        input_output_aliases={1 + i: 1 + o for i, o in ride.aliases.items()},
        compiler_params=pltpu.CompilerParams(vmem_limit_bytes=VMEM_LIMIT),
    )(packed, *ride.inputs)
    return outs[0], list(outs[1:])


def _add_halves(grad, other, core, name):
    n_shards, _, rows, cols = grad.shape
    tr = _tile(rows, max(16, (1 << 20) // (4 * cols) // 16 * 16), mult=16)

    def body(core_ref, g_ref, o_ref, out_ref):
        del core_ref
        out_ref[...] = (g_ref[...] + o_ref[...]).astype(BF16)

    return pl.pallas_call(
        body,
        name=name,
        grid_spec=pltpu.PrefetchScalarGridSpec(
            num_scalar_prefetch=1,
            grid=(n_shards, rows // tr),
            in_specs=[
                pl.BlockSpec((None, None, tr, cols), lambda s, i, core_ref: (s, core_ref[0], i, 0)),
                pl.BlockSpec((None, tr, cols), lambda s, i, core_ref: (s, i, 0)),
            ],
            out_specs=pl.BlockSpec((None, tr, cols), lambda s, i, core_ref: (s, i, 0)),
        ),
        out_shape=jax.ShapeDtypeStruct((n_shards, rows, cols), BF16),
        compiler_params=_params("arbitrary", "arbitrary"),
    )(core, grad, other)


def _inproj_fwd(x2, g_row, w_l, layer, ride=None):
    n = x2.shape[0]
    tm = _tile(n, 1024)

    def body(x_ref, g_ref, w_ref, p_ref, h_ref):
        @pl.when(pl.program_id(1) == 0)
        def _():
            xv = x_ref[...]
            r = lax.rsqrt(_lanemean(xv * xv) + RMS_EPS)
            h_ref[...] = (xv * r * g_ref[...]).astype(BF16)

        p_ref[...] = _dot(h_ref[...], w_ref[...]).astype(BF16)

    return _host_call(
        body,
        name=f"inproj_fwd_l{layer}",
        grid=(n // tm, N_COL_TILES),
        in_specs=[
            pl.BlockSpec((tm, D_MODEL), lambda i, j: (i, 0)),
            pl.BlockSpec((1, D_MODEL), lambda i, j: (0, 0)),
            pl.BlockSpec((None, D_MODEL, COL_TILE), lambda i, j: (j // 2, 0, j % 2)),
        ],
        out_specs=[
            pl.BlockSpec((tm, COL_TILE), lambda i, j: (i, j)),
            pl.BlockSpec((tm, D_MODEL), lambda i, j: (i, 0)),
        ],
        out_shape=[jax.ShapeDtypeStruct((n, IN_COLS), BF16), jax.ShapeDtypeStruct((n, D_MODEL), BF16)],
        args=[x2, g_row, w_l],
        ride=ride,
    )


def _layer_norm_parts(v, g, b):
    mu = _lanemean(v)
    d = v - mu
    rstd = lax.rsqrt(_lanemean(d * d) + LN_EPS)
    xh = d * rstd
    return xh, rstd, xh * g + b


def _window_sum(ref, slot, weight_row, n_taps, base, t_rows):
    total = None
    for b in range(8):
        group = None
        for a in range((base + n_taps - 1) // 8 + 1):
            k = 8 * a + b - base
            if 0 <= k < n_taps:
                term = weight_row(k) * ref[slot, 8 * a:8 * a + t_rows + 8, :]
                group = term if group is None else group + term
        if group is not None:
            part = group[b:b + t_rows, :]
            total = part if total is None else total + part
    return total


def _tap_grads(acc_ref, pad_ref, shift_ref, dy, ref, slot, n_taps, base, t_rows):
    pad_ref[0:8, :] = jnp.zeros((8, BRANCH_W), F32)
    pad_ref[8:8 + t_rows, :] = dy
    pad_ref[8 + t_rows:16 + t_rows, :] = jnp.zeros((8, BRANCH_W), F32)
    for b in range(8):
        taps = [(a, 8 * a + b - base) for a in range((base + n_taps - 1) // 8 + 1) if 0 <= 8 * a + b - base < n_taps]
        if not taps:
            continue
        shift_ref[...] = pad_ref[pl.ds(8 - b, t_rows + 8), :]
        for a, k in taps:
            prod = shift_ref[...] * ref[slot, 8 * a:8 * a + t_rows + 8, :]
            acc_ref[k] += prod.reshape((t_rows + 8) // 8, 8, BRANCH_W).sum(axis=0)


def _tril_mask():
    r = lax.broadcasted_iota(jnp.int32, (CHUNK, CHUNK), 0)
    c = lax.broadcasted_iota(jnp.int32, (CHUNK, CHUNK), 1)
    return r >= c


def _lanes(refs4):
    return jnp.concatenate([refs4[g] for g in range(N_CHIPS)], axis=1)


def _mixer_weight_specs():
    def whole(shape):
        nd = len(shape)
        return pl.BlockSpec(shape, lambda b, c: (0,) * nd)

    return [
        whole((N_GROUPS, GROUP_W, GROUP_W)),
        whole((1, BRANCH_W)),
        whole((N_CHIPS, CONV_ROWS, GROUP_W)),
        whole((1, BRANCH_W)),
        whole((1, BRANCH_W)),
        whole((1, BRANCH_W)),
        whole((1, BRANCH_W)),
        whole((1, BRANCH_W)),
        whole((N_GROUPS, CHUNK, CHUNK)),
        whole((CHUNK, BRANCH_W)),
        whole((N_CHIPS, SHORT_ROWS, GROUP_W)),
    ]


def _mixer_weight_args(mw):
    return [mw["pool_w"], mw["pool_scale"], mw["conv_w"], mw["conv_b"], mw["conv_ln_g"], mw["conv_ln_b"],
            mw["sgu_ln_g"], mw["sgu_ln_b"], mw["sgu_w"], mw["sgu_bias"], mw["sc_w"]]


def _mixers_fwd(proj3, mw, layer, ride=None):
    nb, seq, _ = proj3.shape
    t_rows = _tile(seq, 512)
    nc = seq // t_rows
    hb = t_rows // HALO

    def body(cur_ref, halo_ref, pw_ref, ps_ref, cw_ref, cb_ref, clg_ref, clb_ref, slg_ref, slb_ref, sw_ref,
             sbias_ref, scw_ref, z_ref, cacc_ref, ext_ref):
        c = pl.program_id(1)
        has_prev = c > 0
        row = lax.broadcasted_iota(jnp.int32, (t_rows, 1), 0)
        tpos = (c * t_rows + row + 1).astype(F32)

        def cur(k):
            return cur_ref[:, k * BRANCH_W:(k + 1) * BRANCH_W].astype(F32)

        def hal(k):
            return halo_ref[:, k * BRANCH_W:(k + 1) * BRANCH_W].astype(F32)

        def put_ext(slot, halo_val, cur_val):
            ext_ref[slot, 0:HALO, :] = jnp.where(has_prev, halo_val, 0.0)
            ext_ref[slot, HALO:HALO + t_rows, :] = cur_val
            ext_ref[slot, HALO + t_rows:HALO + t_rows + 8, :] = jnp.zeros((8, BRANCH_W), F32)

        px = cur(P_X)
        put_ext(0, hal(P_X), px)
        mixed = []
        for j, win in enumerate(POOL_WINDOWS):
            cols = slice(j * GROUP_W, (j + 1) * GROUP_W)
            s = px[:, cols]
            for i in range(1, win):
                s = s + ext_ref[0, pl.ds(HALO - i, t_rows), cols]
            pooled = s / jnp.minimum(tpos, float(win)) - px[:, cols]
            mixed.append(_dot(pooled.astype(BF16), pw_ref[j]))
        z_pool = jnp.concatenate(mixed, axis=1) * ps_ref[...] * _silu(cur(P_GATE))
        z_ref[:, 0:BRANCH_W] = z_pool.astype(BF16)

        put_ext(1, hal(C_A) * _sigmoid(hal(C_B)), cur(C_A) * _sigmoid(cur(C_B)))
        cw = _lanes(cw_ref)
        acc = cb_ref[...] + _window_sum(ext_ref, 1, lambda k: cw[k:k + 1, :], CONV_K, HALO - (CONV_K - 1), t_rows)
        cacc_ref[...] = acc
        _, _, ln = _layer_norm_parts(acc, clg_ref[...], clb_ref[...])
        z_ref[:, BRANCH_W:2 * BRANCH_W] = (_silu(ln) * _silu(cur(C_GATE))).astype(BF16)

        _, _, v = _layer_norm_parts(cur(G_V), slg_ref[...], slb_ref[...])
        vb = v.astype(BF16)
        mask = _tril_mask()
        wt = [jnp.where(mask, sw_ref[g], 0.0).astype(BF16) for g in range(N_GROUPS)]
        sp_rows = []
        for sub in range(t_rows // CHUNK):
            rows = slice(sub * CHUNK, (sub + 1) * CHUNK)
            sp_rows.append(jnp.concatenate(
                [_dot(wt[g], vb[rows, g * GROUP_W:(g + 1) * GROUP_W]) for g in range(N_GROUPS)], axis=1)
                + sbias_ref[...])
        sp = jnp.concatenate(sp_rows, axis=0)
        z_ref[:, 2 * BRANCH_W:3 * BRANCH_W] = (cur(G_U) * sp * _silu(cur(G_GATE))).astype(BF16)

        put_ext(2, hal(S_C) * hal(S_X), cur(S_C) * cur(S_X))
        scw = _lanes(scw_ref)
        cv = _window_sum(ext_ref, 2, lambda k: scw[k:k + 1, :], SHORT_K, HALO - (SHORT_K - 1), t_rows)
        z_ref[:, 3 * BRANCH_W:4 * BRANCH_W] = (cur(S_B) * cv * _silu(cur(S_GATE))).astype(BF16)

    return _host_call(
        body,
        name=f"mixers_fwd_l{layer}",
        grid=(nb, nc),
        in_specs=[
            pl.BlockSpec((None, t_rows, PIECE_COLS), lambda b, c: (b, c, 0)),
            pl.BlockSpec((None, HALO, PIECE_COLS), lambda b, c: (b, jnp.maximum(c * hb - 1, 0), 0)),
        ] + _mixer_weight_specs(),
        out_specs=[pl.BlockSpec((None, t_rows, N_BRANCH * BRANCH_W), lambda b, c: (b, c, 0)),
                   pl.BlockSpec((None, t_rows, BRANCH_W), lambda b, c: (b, c, 0))],
        out_shape=[jax.ShapeDtypeStruct((nb, seq, N_BRANCH * BRANCH_W), BF16),
                   jax.ShapeDtypeStruct((nb, seq, BRANCH_W), F32)],
        scratch_shapes=[pltpu.VMEM((3, HALO + t_rows + 8, BRANCH_W), F32)],
        args=[proj3, proj3, *_mixer_weight_args(mw)],
        ride=ride,
    )


MIXER_GRADS = ["pool_w", "pool_scale", "conv_w", "conv_b", "conv_ln_g", "conv_ln_b", "sgu_ln_g", "sgu_ln_b", "sgu_w",
               "sgu_b", "sc_w"]


def _mixers_bwd(proj3, dz3, cacc3, dproj3, mw, layer, ride=None):
    nb, seq, _ = proj3.shape
    t_rows = _tile(seq, 512)
    nc = seq // t_rows
    hb = t_rows // HALO

    def body(cur_ref, halo_ref, dz_ref, cacc_ref, pw_ref, ps_ref, cw_ref, cb_ref, clg_ref, clb_ref, slg_ref, slb_ref,
             sw_ref, sbias_ref, scw_ref, dp_in_ref, dp_ref, g_pw, g_ps, g_cw, g_cb, g_clg, g_clb, g_slg,
             g_slb, g_sw, g_sb, g_scw, ext_ref, nxt_ref, sb_acc, cw_acc, scw_acc, pad_ref, shift_ref):
        del dp_in_ref
        b = pl.program_id(0)
        r = pl.program_id(1)
        c = nc - 1 - r
        has_prev = c > 0
        row = lax.broadcasted_iota(jnp.int32, (t_rows, 1), 0)
        tpos = (c * t_rows + row + 1).astype(F32)

        @pl.when(jnp.logical_and(b == 0, r == 0))
        def _():
            for ref in (g_pw, g_ps, g_cb, g_clg, g_clb, g_slg, g_slb, g_sw, sb_acc, cw_acc, scw_acc):
                ref[...] = jnp.zeros_like(ref)

        @pl.when(r == 0)
        def _():
            nxt_ref[:, t_rows:t_rows + HALO, :] = jnp.zeros((3, HALO, BRANCH_W), F32)

        def cur(k):
            return cur_ref[:, k * BRANCH_W:(k + 1) * BRANCH_W].astype(F32)

        def hal(k):
            return halo_ref[:, k * BRANCH_W:(k + 1) * BRANCH_W].astype(F32)

        def dzp(k):
            return dz_ref[:, k * BRANCH_W:(k + 1) * BRANCH_W].astype(F32)

        def put_dp(k, val):
            dp_ref[:, k * BRANCH_W:(k + 1) * BRANCH_W] = val.astype(BF16)

        def put_ext(slot, halo_val, cur_val):
            ext_ref[slot, 0:HALO, :] = jnp.where(has_prev, halo_val, 0.0)
            ext_ref[slot, HALO:HALO + t_rows, :] = cur_val
            ext_ref[slot, HALO + t_rows:HALO + t_rows + 8, :] = jnp.zeros((8, BRANCH_W), F32)

        px = cur(P_X)
        put_ext(0, hal(P_X), px)
        pgate = cur(P_GATE)
        dz_pool = dzp(0)
        ps = ps_ref[...]
        pooled, mixed, cnts = [], [], []
        for j, win in enumerate(POOL_WINDOWS):
            cols = slice(j * GROUP_W, (j + 1) * GROUP_W)
            s = px[:, cols]
            for i in range(1, win):
                s = s + ext_ref[0, pl.ds(HALO - i, t_rows), cols]
            cnt = jnp.minimum(tpos, float(win))
            pj = s / cnt - px[:, cols]
            cnts.append(cnt)
            pooled.append(pj.astype(BF16))
            mixed.append(_dot(pooled[j], pw_ref[j]))
        mixed = jnp.concatenate(mixed, axis=1)
        pg_silu, pg_dsilu = _silu_pair(pgate)
        put_dp(P_GATE, dz_pool * (mixed * ps) * pg_dsilu)
        d_out = dz_pool * pg_silu
        g_ps[...] += _rowsum(d_out * mixed)
        d_mixed = (d_out * ps).astype(BF16)
        d_pooled = []
        for j in range(N_GROUPS):
            cols = slice(j * GROUP_W, (j + 1) * GROUP_W)
            g_pw[j] += _dot_tn(pooled[j], d_mixed[:, cols])
            dpj = _dot_nt(d_mixed[:, cols], pw_ref[j])
            d_pooled.append(dpj)
            nxt_ref[0, 0:t_rows, cols] = dpj / cnts[j]
        dpx = []
        for j, win in enumerate(POOL_WINDOWS):
            cols = slice(j * GROUP_W, (j + 1) * GROUP_W)
            s = nxt_ref[0, 0:t_rows, cols]
            for i in range(1, win):
                s = s + nxt_ref[0, pl.ds(i, t_rows), cols]
            dpx.append(s - d_pooled[j])
        put_dp(P_X, jnp.concatenate(dpx, axis=1))

        c_a = cur(C_A)
        sig_b = _sigmoid(cur(C_B))
        put_ext(1, hal(C_A) * _sigmoid(hal(C_B)), c_a * sig_b)
        cw = _lanes(cw_ref)
        xh, rstd, ln = _layer_norm_parts(cacc_ref[...], clg_ref[...], clb_ref[...])
        cgate = cur(C_GATE)
        dz_conv = dzp(1)
        ln_silu, ln_dsilu = _silu_pair(ln)
        cg_silu, cg_dsilu = _silu_pair(cgate)
        put_dp(C_GATE, dz_conv * ln_silu * cg_dsilu)
        d_ln = dz_conv * cg_silu * ln_dsilu
        g_clg[...] += _rowsum(d_ln * xh)
        g_clb[...] += _rowsum(d_ln)
        dxh = d_ln * clg_ref[...]
        dc = rstd * (dxh - _lanemean(dxh) - xh * _lanemean(dxh * xh))
        g_cb[...] += _rowsum(dc)
        nxt_ref[1, 0:t_rows, :] = dc
        _tap_grads(cw_acc, pad_ref, shift_ref, dc, ext_ref, 1, CONV_K, HALO - (CONV_K - 1), t_rows)
        dyg = _window_sum(nxt_ref, 1, lambda i: cw[CONV_K - 1 - i:CONV_K - i, :], CONV_K, 0, t_rows)
        put_dp(C_A, dyg * sig_b)
        put_dp(C_B, dyg * c_a * sig_b * (1.0 - sig_b))

        u = cur(G_U)
        ggate = cur(G_GATE)
        vxh, vrstd, v = _layer_norm_parts(cur(G_V), slg_ref[...], slb_ref[...])
        vb = v.astype(BF16)
        mask = _tril_mask()
        wt = [jnp.where(mask, sw_ref[g], 0.0) for g in range(N_GROUPS)]
        wt_b = [w.astype(BF16) for w in wt]
        wtt_b = [w.T.astype(BF16) for w in wt]
        dz_sgu = dzp(2)
        gg_silu, gg_dsilu = _silu_pair(ggate)
        d_sgu = dz_sgu * gg_silu
        d_sp = d_sgu * u
        d_spb = d_sp.astype(BF16)
        sp_rows, dv_rows = [], []
        for sub in range(t_rows // CHUNK):
            rows = slice(sub * CHUNK, (sub + 1) * CHUNK)
            sp_g, dv_g = [], []
            for g in range(N_GROUPS):
                cols = slice(g * GROUP_W, (g + 1) * GROUP_W)
                sp_g.append(_dot(wt_b[g], vb[rows, cols]))
                g_sw[g] += jnp.where(mask, _dot_nt(d_spb[rows, cols], vb[rows, cols]), 0.0)
                dv_g.append(_dot(wtt_b[g], d_spb[rows, cols]))
            sp_rows.append(jnp.concatenate(sp_g, axis=1) + sbias_ref[...])
            dv_rows.append(jnp.concatenate(dv_g, axis=1))
            sb_acc[...] += d_sp[rows, :]
        sp = jnp.concatenate(sp_rows, axis=0)
        dv = jnp.concatenate(dv_rows, axis=0)
        put_dp(G_GATE, dz_sgu * (u * sp) * gg_dsilu)
        put_dp(G_U, d_sgu * sp)
        g_slg[...] += _rowsum(dv * vxh)
        g_slb[...] += _rowsum(dv)
        dvx = dv * slg_ref[...]
        put_dp(G_V, vrstd * (dvx - _lanemean(dvx) - vxh * _lanemean(dvx * vxh)))

        s_b, s_c, s_x, sgate = cur(S_B), cur(S_C), cur(S_X), cur(S_GATE)
        put_ext(2, hal(S_C) * hal(S_X), s_c * s_x)
        scw = _lanes(scw_ref)
        cv = _window_sum(ext_ref, 2, lambda k: scw[k:k + 1, :], SHORT_K, HALO - (SHORT_K - 1), t_rows)
        dz_sc = dzp(3)
        sg_silu, sg_dsilu = _silu_pair(sgate)
        put_dp(S_GATE, dz_sc * (s_b * cv) * sg_dsilu)
        d_pre = dz_sc * sg_silu
        put_dp(S_B, d_pre * cv)
        dcv = d_pre * s_b
        nxt_ref[2, 0:t_rows, :] = dcv
        _tap_grads(scw_acc, pad_ref, shift_ref, dcv, ext_ref, 2, SHORT_K, HALO - (SHORT_K - 1), t_rows)
        du = _window_sum(nxt_ref, 2, lambda i: scw[SHORT_K - 1 - i:SHORT_K - i, :], SHORT_K, 0, t_rows)
        put_dp(S_C, du * s_x)
        put_dp(S_X, du * s_c)

        nxt_ref[:, t_rows:t_rows + HALO, :] = nxt_ref[:, 0:HALO, :]

        @pl.when(jnp.logical_and(b == nb - 1, r == nc - 1))
        def _():
            lane = lax.broadcasted_iota(jnp.int32, (CHUNK, GROUP_W), 1)
            out = jnp.zeros((CHUNK, GROUP_W), F32)
            for g in range(N_GROUPS):
                col = jnp.sum(sb_acc[:, g * GROUP_W:(g + 1) * GROUP_W], axis=1, keepdims=True)
                out = jnp.where(lane == g, col, out)
            g_sb[...] = out
            g_cw[...] = jnp.sum(cw_acc[...], axis=1)
            g_scw[...] = jnp.sum(scw_acc[...], axis=1)

    def acc_spec(shape):
        nd = len(shape)
        return pl.BlockSpec(shape, lambda b, r: (0,) * nd)

    acc_shapes = [
        (N_GROUPS, GROUP_W, GROUP_W),
        (1, BRANCH_W),
        (CONV_ROWS, BRANCH_W),
        (1, BRANCH_W),
        (1, BRANCH_W),
        (1, BRANCH_W),
        (1, BRANCH_W),
        (1, BRANCH_W),
        (N_GROUPS, CHUNK, CHUNK),
        (CHUNK, GROUP_W),
        (SHORT_ROWS, BRANCH_W),
    ]
    outs, rode = _host_call(
        body,
        name=f"mixers_bwd_l{layer}",
        grid=(nb, nc),
        in_specs=[
            pl.BlockSpec((None, t_rows, PIECE_COLS), lambda b, r: (b, nc - 1 - r, 0)),
            pl.BlockSpec((None, HALO, PIECE_COLS), lambda b, r: (b, jnp.maximum((nc - 1 - r) * hb - 1, 0), 0)),
            pl.BlockSpec((None, t_rows, N_BRANCH * BRANCH_W), lambda b, r: (b, nc - 1 - r, 0)),
            pl.BlockSpec((None, t_rows, BRANCH_W), lambda b, r: (b, nc - 1 - r, 0)),
        ] + _mixer_weight_specs() + [ANY],
        out_specs=[pl.BlockSpec((None, t_rows, PIECE_COLS), lambda b, r: (b, nc - 1 - r, 0))]
        + [acc_spec(s) for s in acc_shapes],
        out_shape=[jax.ShapeDtypeStruct(dproj3.shape, BF16)] + [jax.ShapeDtypeStruct(s, F32) for s in acc_shapes],
        scratch_shapes=[
            pltpu.VMEM((3, HALO + t_rows + 8, BRANCH_W), F32),
            pltpu.VMEM((3, t_rows + HALO, BRANCH_W), F32),
            pltpu.VMEM((CHUNK, BRANCH_W), F32),
            pltpu.VMEM((CONV_ROWS, 8, BRANCH_W), F32),
            pltpu.VMEM((SHORT_ROWS, 8, BRANCH_W), F32),
            pltpu.VMEM((t_rows + 16, BRANCH_W), F32),
            pltpu.VMEM((t_rows + 8, BRANCH_W), F32),
        ],
        aliases={4 + 11: 0},
        args=[proj3, proj3, dz3, cacc3, *_mixer_weight_args(mw), dproj3],
        ride=ride,
    )
    return outs[0], dict(zip(MIXER_GRADS, outs[1:])), rode


def _merge_fwd(z2, proj2, x2, wbr_l, wo_l, layer, ride=None):
    n = x2.shape[0]
    tm = _tile(n, 1024)

    def body(z_ref, gate_ref, x_ref, wbr_ref, wo_ref, xn_ref, mg_ref, acc_ref):
        nbr = pl.program_id(1)
        bo = _dot(z_ref[...], wbr_ref[nbr])
        contrib = _sigmoid(gate_ref[...].astype(F32)) * bo

        @pl.when(nbr == 0)
        def _():
            acc_ref[...] = contrib

        @pl.when(nbr > 0)
        def _():
            acc_ref[...] += contrib

        @pl.when(nbr == N_BRANCH - 1)
        def _():
            mg = acc_ref[...].astype(BF16)
            mg_ref[...] = mg
            xn_ref[...] = x_ref[...] + _dot(mg, wo_ref[...])

    return _host_call(
        body,
        name=f"merge_fwd_l{layer}",
        grid=(n // tm, N_BRANCH),
        in_specs=[
            pl.BlockSpec((tm, BRANCH_W), lambda i, b: (i, b)),
            pl.BlockSpec((tm, D_MODEL), lambda i, b: (i, GATE_BLOCK0 + b)),
            pl.BlockSpec((tm, D_MODEL), lambda i, b: (i, 0)),
            pl.BlockSpec((N_BRANCH, BRANCH_W, D_MODEL), lambda i, b: (0, 0, 0)),
            pl.BlockSpec((D_MODEL, D_MODEL), lambda i, b: (0, 0)),
        ],
        out_specs=[
            pl.BlockSpec((tm, D_MODEL), lambda i, b: (i, 0)),
            pl.BlockSpec((tm, D_MODEL), lambda i, b: (i, 0)),
        ],
        out_shape=[jax.ShapeDtypeStruct((n, D_MODEL), F32), jax.ShapeDtypeStruct((n, D_MODEL), BF16)],
        scratch_shapes=[pltpu.VMEM((tm, D_MODEL), F32)],
        args=[z2, proj2, x2, wbr_l, wo_l],
        ride=ride,
    )


def _merge_bwd(dout2, merged2, z2, proj2, wbr_l, wo_l, layer, ride=None):
    n = dout2.shape[0]
    tm = _tile(n, 512)
    nt = n // tm

    def body(do_ref, mg_ref, z_ref, gate_ref, wbr_ref, wo_ref, dz_ref, dg_ref, gwo_ref, gwbr_ref, dm_ref,
             awo_ref, awbr_ref):
        i = pl.program_id(0)
        nbr = pl.program_id(1)

        @pl.when(nbr == 0)
        def _():
            do_b = do_ref[...].astype(BF16)
            dm_ref[...] = _dot_nt(do_b, wo_ref[...])
            gw = _dot_tn(mg_ref[...], do_b)

            @pl.when(i == 0)
            def _():
                awo_ref[...] = gw

            @pl.when(i > 0)
            def _():
                awo_ref[...] += gw

        zt = z_ref[...]
        wbr = wbr_ref[nbr]
        bo = _dot(zt, wbr)
        gt = _sigmoid(gate_ref[...].astype(F32))
        dm = dm_ref[...]
        dbo = (dm * gt).astype(BF16)
        dg_ref[...] = (dm * bo * gt * (1.0 - gt)).astype(BF16)
        dz_ref[...] = _dot_nt(dbo, wbr).astype(BF16)
        gw = _dot_tn(zt, dbo)

        @pl.when(i == 0)
        def _():
            awbr_ref[nbr] = gw

        @pl.when(i > 0)
        def _():
            awbr_ref[nbr] += gw

        @pl.when(jnp.logical_and(i == nt - 1, nbr == N_BRANCH - 1))
        def _():
            pltpu.sync_copy(awo_ref, gwo_ref)
            pltpu.sync_copy(awbr_ref, gwbr_ref)

    return _host_call(
        body,
        name=f"merge_bwd_l{layer}",
        grid=(nt, N_BRANCH),
        in_specs=[
            pl.BlockSpec((tm, D_MODEL), lambda i, b: (i, 0)),
            pl.BlockSpec((tm, D_MODEL), lambda i, b: (i, 0)),
            pl.BlockSpec((tm, BRANCH_W), lambda i, b: (i, b)),
            pl.BlockSpec((tm, D_MODEL), lambda i, b: (i, GATE_BLOCK0 + b)),
            pl.BlockSpec((N_BRANCH, BRANCH_W, D_MODEL), lambda i, b: (0, 0, 0)),
            pl.BlockSpec((D_MODEL, D_MODEL), lambda i, b: (0, 0)),
        ],
        out_specs=[
            pl.BlockSpec((tm, BRANCH_W), lambda i, b: (i, b)),
            pl.BlockSpec((tm, D_MODEL), lambda i, b: (i, GATE_BLOCK0 + b)),
            ANY,
            ANY,
        ],
        out_shape=[
            jax.ShapeDtypeStruct((n, N_BRANCH * BRANCH_W), BF16),
            jax.ShapeDtypeStruct((n, IN_COLS), BF16),
            jax.ShapeDtypeStruct((D_MODEL, D_MODEL), F32),
            jax.ShapeDtypeStruct((N_BRANCH, BRANCH_W, D_MODEL), F32),
        ],
        scratch_shapes=[
            pltpu.VMEM((tm, D_MODEL), F32),
            pltpu.VMEM((D_MODEL, D_MODEL), F32),
            pltpu.VMEM((N_BRANCH, BRANCH_W, D_MODEL), F32),
        ],
        args=[dout2, merged2, z2, proj2, wbr_l, wo_l],
        ride=ride,
    )


def _loss_head(x2, g_row, tgt2):
    n = x2.shape[0]
    tm = _tile(n, 512)

    def body(x_ref, g_ref, t_ref, dx_ref, loss_ref, dg_ref):
        @pl.when(pl.program_id(0) == 0)
        def _():
            loss_ref[...] = jnp.zeros_like(loss_ref)
            dg_ref[...] = jnp.zeros_like(dg_ref)

        xv = x_ref[...]
        g = g_ref[...]
        r = lax.rsqrt(_lanemean(xv * xv) + RMS_EPS)
        xh = xv * r
        err = xh * g - t_ref[...]
        loss_ref[...] += 0.5 * jnp.sum(_lanemean(err * err), axis=0, keepdims=True)
        dy = err * (1.0 / D_MODEL)
        dg_ref[...] += _rowsum(dy * xh)
        dxh = dy * g
        dx_ref[...] = r * (dxh - xh * _lanemean(dxh * xh))

    return pl.pallas_call(
        body,
        name="loss_head",
        grid=(n // tm,),
        in_specs=[
            pl.BlockSpec((tm, D_MODEL), lambda i: (i, 0)),
            pl.BlockSpec((1, D_MODEL), lambda i: (0, 0)),
            pl.BlockSpec((tm, D_MODEL), lambda i: (i, 0)),
        ],
        out_specs=[
            pl.BlockSpec((tm, D_MODEL), lambda i: (i, 0)),
            pl.BlockSpec((1, GROUP_W), lambda i: (0, 0)),
            pl.BlockSpec((1, D_MODEL), lambda i: (0, 0)),
        ],
        out_shape=[jax.ShapeDtypeStruct((n, D_MODEL), F32), jax.ShapeDtypeStruct((1, GROUP_W), F32),
                   jax.ShapeDtypeStruct((1, D_MODEL), F32)],
        compiler_params=_params("arbitrary"),
    )(x2, g_row, tgt2)


def _inproj_bwd_x(dproj2, w_l, x2, g_row, dout2, layer, ride=None):
    n = x2.shape[0]
    tm = _tile(n, 1024)

    def body(dp_ref, w_ref, x_ref, g_ref, do_ref, dx_ref, dng_ref, dh_ref):
        i = pl.program_id(0)
        s = pl.program_id(1)
        part = _dot_nt(dp_ref[...], w_ref[...])

        @pl.when(s == 0)
        def _():
            dh_ref[...] = part

        @pl.when(s > 0)
        def _():
            dh_ref[...] += part

        @pl.when(jnp.logical_and(i == 0, s == 0))
        def _():
            dng_ref[...] = jnp.zeros_like(dng_ref)

        @pl.when(s == N_COL_TILES - 1)
        def _():
            xv = x_ref[...]
            r = lax.rsqrt(_lanemean(xv * xv) + RMS_EPS)
            xh = xv * r
            dh = dh_ref[...]
            dng_ref[...] += _rowsum(dh * xh)
            dxh = dh * g_ref[...]
            dx_ref[...] = do_ref[...] + r * (dxh - xh * _lanemean(dxh * xh))

    return _host_call(
        body,
        name=f"inproj_bwd_x_l{layer}",
        grid=(n // tm, N_COL_TILES),
        in_specs=[
            pl.BlockSpec((tm, COL_TILE), lambda i, s: (i, s)),
            pl.BlockSpec((None, D_MODEL, COL_TILE), lambda i, s: (s // 2, 0, s % 2)),
            pl.BlockSpec((tm, D_MODEL), lambda i, s: (i, 0)),
            pl.BlockSpec((1, D_MODEL), lambda i, s: (0, 0)),
            pl.BlockSpec((tm, D_MODEL), lambda i, s: (i, 0)),
        ],
        out_specs=[
            pl.BlockSpec((tm, D_MODEL), lambda i, s: (i, 0)),
            pl.BlockSpec((1, D_MODEL), lambda i, s: (0, 0)),
        ],
        out_shape=[jax.ShapeDtypeStruct((n, D_MODEL), F32), jax.ShapeDtypeStruct((1, D_MODEL), F32)],
        scratch_shapes=[pltpu.VMEM((tm, D_MODEL), F32)],
        args=[dproj2, w_l, x2, g_row, dout2],
        ride=ride,
    )


def _inproj_bwd_w(h2, dproj2, layer, ride=None):
    n = h2.shape[0]
    tm = _tile(n, 2048)

    def body(h_ref, dp_ref, gw_ref):
        gw = _dot_tn(h_ref[...], dp_ref[...])

        @pl.when(pl.program_id(1) == 0)
        def _():
            gw_ref[...] = gw

        @pl.when(pl.program_id(1) > 0)
        def _():
            gw_ref[...] += gw

    outs, rode = _host_call(
        body,
        name=f"inproj_bwd_w_l{layer}",
        grid=(N_COL_TILES, n // tm),
        in_specs=[
            pl.BlockSpec((tm, D_MODEL), lambda s, i: (i, 0)),
            pl.BlockSpec((tm, COL_TILE), lambda s, i: (i, s)),
        ],
        out_specs=[pl.BlockSpec((None, D_MODEL, COL_TILE), lambda s, i: (s // 2, 0, s % 2))],
        out_shape=[jax.ShapeDtypeStruct((N_CHIPS, D_MODEL, SHARD_COLS), F32)],
        args=[h2, dproj2],
        ride=ride,
    )
    return outs[0], rode


def _adamw_math(w, g, m, v):
    m = ADAM_B1 * m + (1.0 - ADAM_B1) * g
    v = ADAM_B2 * v + (1.0 - ADAM_B2) * jnp.square(g)
    m_hat = m / (1.0 - ADAM_B1 ** ADAM_STEP)
    v_hat = v / (1.0 - ADAM_B2 ** ADAM_STEP)
    delta = -ADAM_LR * (m_hat / (jnp.sqrt(v_hat) + ADAM_EPS) + ADAM_WD * w)
    return delta, m, v


def _adamw_sharded(w, m, v, part, layer, prev, name, ride=None):
    _, rows, cols = w.shape
    tr = _tile(rows, max(16, (1 << 19) // (4 * cols) // 16 * 16), mult=16)

    def body(w_ref, m_ref, v_ref, p_ref, *rest):
        g_out, d_out, m_out, v_out = rest[-4:]
        g = ((p_ref[0].astype(F32) + p_ref[1].astype(F32)) + p_ref[2].astype(F32)) + p_ref[3].astype(F32)
        delta, m2, v2 = _adamw_math(w_ref[...], g, m_ref[...], v_ref[...])
        g_out[...] = g
        d_out[...] = delta
        m_out[...] = m2
        v_out[...] = v2

    spec = pl.BlockSpec((None, tr, cols), lambda i: (layer, i, 0))
    return _host_call(
        body,
        name=f"adamw_{name}_l{layer}",
        grid=(rows // tr,),
        in_specs=[spec, spec, spec, pl.BlockSpec((N_CHIPS, tr, cols), lambda i: (0, i, 0))]
        + ([ANY] * 4 if prev else []),
        out_specs=[spec] * 4,
        out_shape=[jax.ShapeDtypeStruct(w.shape, F32)] * 4,
        args=[w, m, v, part] + (list(prev) if prev else []),
        aliases={4 + k: k for k in range(4)} if prev else {},
        ride=ride,
    )


def _adamw_packed(w, m, v, g):
    rows = w.shape[0]
    tr = _tile(rows, rows // 2 if rows % 16 == 0 else rows)

    def body(w_ref, m_ref, v_ref, g_ref, d_out, m_out, v_out):
        delta, m2, v2 = _adamw_math(w_ref[...], g_ref[...], m_ref[...], v_ref[...])
        d_out[...] = delta
        m_out[...] = m2
        v_out[...] = v2

    spec = pl.BlockSpec((tr, GROUP_W), lambda i: (i, 0))
    return pl.pallas_call(
        body,
        name="adamw_small",
        grid=(rows // tr,),
        in_specs=[spec] * 4,
        out_specs=[spec] * 3,
        out_shape=[jax.ShapeDtypeStruct(w.shape, F32)] * 3,
        compiler_params=_params("arbitrary"),
    )(w, m, v, g)


SMALL = ["norm_g", "pool_w", "pool_scale", "conv_b", "conv_ln_g", "conv_ln_b", "sgu_ln_g", "sgu_ln_b", "sgu_w",
         "sgu_b", "final_g"]
WEIGHTS = ["norm_g", "w_in", "pool_w", "pool_scale", "conv_w", "conv_b", "conv_ln_g", "conv_ln_b", "sgu_ln_g",
           "sgu_ln_b", "sgu_w", "sgu_b", "sc_w", "w_branch", "w_o", "final_g"]
HALF_SHAPES = [(D_MODEL // 2, SHARD_COLS), (N_BRANCH * BRANCH_W // 2, BR_SHARD), (BR_SHARD // 2, D_MODEL),
               (CS_ROWS // 2, GROUP_W)]


def _pack_small(tree):
    return jnp.concatenate([tree[k].reshape(-1, GROUP_W) for k in SMALL], axis=0)


def _unpack_small(packed, like):
    out, r = {}, 0
    for k in SMALL:
        nr = like[k].size // GROUP_W
        out[k] = packed[r:r + nr].reshape(like[k].shape)
        r += nr
    return out


def _pad_rows(a, rows):
    pad = [(0, 0)] * a.ndim
    pad[-2] = (0, rows - a.shape[-2])
    return jnp.pad(a, pad)


def _pack_cs(conv, short):
    return jnp.concatenate([_pad_rows(conv, CS_ROWS // 2), _pad_rows(short, CS_ROWS // 2)], axis=-2)


def _shard_major_rows(a):
    return a.reshape(a.shape[0], N_CHIPS, GROUP_W).transpose(1, 0, 2)


def kernel(x, norm_g, w_in, pool_w, pool_scale, conv_w, conv_b, conv_ln_g, conv_ln_b, sgu_ln_g, sgu_ln_b, sgu_w, sgu_b, sc_w, w_branch, w_o, final_g, loss_target, m_norm_g, m_w_in, m_pool_w, m_pool_scale, m_conv_w, m_conv_b, m_conv_ln_g, m_conv_ln_b, m_sgu_ln_g, m_sgu_ln_b, m_sgu_w, m_sgu_b, m_sc_w, m_w_branch, m_w_o, m_final_g, v_norm_g, v_w_in, v_pool_w, v_pool_scale, v_conv_w, v_conv_b, v_conv_ln_g, v_conv_ln_b, v_sgu_ln_g, v_sgu_ln_b, v_sgu_w, v_sgu_b, v_sc_w, v_w_branch, v_w_o, v_final_g):
    w = dict(norm_g=norm_g, w_in=w_in, pool_w=pool_w, pool_scale=pool_scale, conv_w=conv_w, conv_b=conv_b,
             conv_ln_g=conv_ln_g, conv_ln_b=conv_ln_b, sgu_ln_g=sgu_ln_g, sgu_ln_b=sgu_ln_b, sgu_w=sgu_w,
             sgu_b=sgu_b, sc_w=sc_w, w_branch=w_branch, w_o=w_o, final_g=final_g)
    mom = dict(norm_g=m_norm_g, w_in=m_w_in, pool_w=m_pool_w, pool_scale=m_pool_scale, conv_w=m_conv_w,
               conv_b=m_conv_b, conv_ln_g=m_conv_ln_g, conv_ln_b=m_conv_ln_b, sgu_ln_g=m_sgu_ln_g,
               sgu_ln_b=m_sgu_ln_b, sgu_w=m_sgu_w, sgu_b=m_sgu_b, sc_w=m_sc_w, w_branch=m_w_branch, w_o=m_w_o,
               final_g=m_final_g)
    var = dict(norm_g=v_norm_g, w_in=v_w_in, pool_w=v_pool_w, pool_scale=v_pool_scale, conv_w=v_conv_w,
               conv_b=v_conv_b, conv_ln_g=v_conv_ln_g, conv_ln_b=v_conv_ln_b, sgu_ln_g=v_sgu_ln_g,
               sgu_ln_b=v_sgu_ln_b, sgu_w=v_sgu_w, sgu_b=v_sgu_b, sc_w=v_sc_w, w_branch=v_w_branch, w_o=v_w_o,
               final_g=v_final_g)

    nb, seq, _ = x.shape
    n = nb * seq
    core = lax.axis_index("c").astype(jnp.int32).reshape(1)

    win_s = w_in.astype(BF16).reshape(DEPTH, 2, D_MODEL // 2, SHARD_COLS)
    wbr_s = w_branch.astype(BF16)
    wo_s = w_o.astype(BF16)
    cs_s = _pack_cs(conv_w, sc_w)

    (win_f0,) = _allgather_layer0([win_s])

    def layer_weights(win_f, wbr_f, wo_f, cs_f):
        cs_f = cs_f.reshape(N_CHIPS, CS_ROWS, GROUP_W)
        return dict(win=win_f.reshape(N_CHIPS, D_MODEL, SHARD_COLS),
                    wbr=wbr_f.reshape(N_CHIPS, N_BRANCH, BRANCH_W, BR_SHARD).transpose(1, 2, 0, 3).reshape(
                        N_BRANCH, BRANCH_W, D_MODEL),
                    wo=wo_f.reshape(D_MODEL, D_MODEL),
                    conv_w=cs_f[:, :CONV_ROWS], sc_w=cs_f[:, CS_ROWS // 2:CS_ROWS // 2 + SHORT_ROWS])

    def mixer_weights(l, gathered):
        row = lambda a: a[l].reshape(1, BRANCH_W)
        bias = jnp.repeat(jnp.swapaxes(sgu_b[l], 0, 1), GROUP_W, axis=1)
        return dict(pool_w=pool_w[l].astype(BF16), pool_scale=row(pool_scale), conv_w=gathered["conv_w"],
                    conv_b=row(conv_b), conv_ln_g=row(conv_ln_g), conv_ln_b=row(conv_ln_b), sgu_ln_g=row(sgu_ln_g),
                    sgu_ln_b=row(sgu_ln_b), sgu_w=sgu_w[l], sgu_bias=bias, sc_w=gathered["sc_w"])

    lw, mw = [None, None], [None, None]

    xs, projs, hs, zs, mgs, caccs = [x.reshape(n, D_MODEL)], [], [], [], [], []
    for l in range(DEPTH):
        first = l == 0
        win_l = win_f0.reshape(N_CHIPS, D_MODEL, SHARD_COLS) if first else lw[1]["win"]
        ride = _join(_gather_whole([wbr_s, wo_s, cs_s], 0), _gather_layer1_first_half(win_s)) if first else None
        (proj, h), rode = _inproj_fwd(xs[l], norm_g[l:l + 1], win_l, l, ride=ride)
        if first:
            lw[0] = layer_weights(win_f0, *rode[:3])
            mw[0] = mixer_weights(0, lw[0])
            win_f1 = rode[3]
        (z3, cacc), rode = _mixers_fwd(proj.reshape(nb, seq, IN_COLS), mw[l], l,
                                       ride=_gather_layer1_second_half(win_s, win_f1) if first else None)
        caccs.append(cacc)
        if first:
            win_f1 = rode[0]
        z = z3.reshape(n, N_BRANCH * BRANCH_W)
        (x_new, merged), rode = _merge_fwd(z, proj, xs[l], lw[l]["wbr"], lw[l]["wo"], l,
                                           ride=_gather_whole([wbr_s, wo_s, cs_s], 1) if first else None)
        if first:
            lw[1] = layer_weights(win_f1, *rode)
            mw[1] = mixer_weights(1, lw[1])
        projs.append(proj)
        hs.append(h)
        zs.append(z)
        mgs.append(merged)
        xs.append(x_new)

    dx, loss_part, g_final = _loss_head(xs[DEPTH], final_g.reshape(1, D_MODEL), loss_target.reshape(n, D_MODEL))
    loss = lax.psum(loss_part[0, 0], ("x", "y", "c"))

    def chip_major(gwbr):
        return gwbr.reshape(N_BRANCH, BRANCH_W, N_CHIPS, BR_SHARD).transpose(2, 0, 1, 3)

    def as_halves(arrays, first_index=0):
        return [g.reshape((N_CHIPS, 2) + s) for g, s in zip(arrays, HALF_SHAPES[first_index:])]

    def add_halves(grads, others, l, first_index=0):
        return [_add_halves(g, o, core, f"add_halves_l{l}_{first_index + i}")
                for i, (g, o) in enumerate(zip(grads, others))]

    per_layer, parts = {}, [None] * DEPTH
    (dz, dproj, gwo, gwbr), _ = _merge_bwd(dx, mgs[1], zs[1], projs[1], lw[1]["wbr"], lw[1]["wo"], 1)
    dproj3, gm, _ = _mixers_bwd(projs[1].reshape(nb, seq, IN_COLS), dz.reshape(nb, seq, N_BRANCH * BRANCH_W), caccs[1],
                                dproj.reshape(nb, seq, IN_COLS), mw[1], 1)
    dproj = dproj3.reshape(n, IN_COLS)
    gwin, _ = _inproj_bwd_w(hs[1], dproj, 1)
    gcs = _pack_cs(_shard_major_rows(gm["conv_w"]), _shard_major_rows(gm["sc_w"]))
    grads1 = as_halves([gwin, chip_major(gwbr), gwo, gcs])
    (dx, g_norm), others1 = _inproj_bwd_x(dproj, lw[1]["win"], xs[1], norm_g[1:2], dx, 1, ride=_swap_halves(grads1))
    gm["norm_g"] = g_norm
    per_layer[1] = gm
    sums1 = add_halves(grads1, others1, 1)
    (dz, dproj, gwo, gwbr), scattered1 = _merge_bwd(dx, mgs[0], zs[0], projs[0], lw[0]["wbr"], lw[0]["wo"], 0,
                                                    ride=_scatter_chip_sums(sums1))
    dproj3, gm, parts[1] = _mixers_bwd(projs[0].reshape(nb, seq, IN_COLS), dz.reshape(nb, seq, N_BRANCH * BRANCH_W),
                                       caccs[0], dproj.reshape(nb, seq, IN_COLS), mw[0], 0,
                                       ride=_share_halves(scattered1))
    dproj = dproj3.reshape(n, IN_COLS)
    gcs = _pack_cs(_shard_major_rows(gm["conv_w"]), _shard_major_rows(gm["sc_w"]))
    small0 = as_halves([chip_major(gwbr), gwo, gcs], first_index=1)
    gwin, others_small0 = _inproj_bwd_w(hs[0], dproj, 0, ride=_swap_halves(small0))
    (gwin_h,) = as_halves([gwin])

    def flat(a, i):
        rows, cols = 2 * HALF_SHAPES[i][0], HALF_SHAPES[i][1]
        return a.reshape(a.shape[0], rows, cols)

    names4 = ["w_in", "w_branch", "w_o", "conv_sc"]

    def packed4(t):
        return [t["w_in"], t["w_branch"], t["w_o"], _pack_cs(t["conv_w"], t["sc_w"])]

    w4, m4, v4 = ([flat(a, i) for i, a in enumerate(packed4(t))] for t in (w, mom, var))

    def sharded_update(i, layer, prev, ride=None):
        return _adamw_sharded(w4[i], m4[i], v4[i], flat(parts[layer][i], i), layer, prev, names4[i], ride=ride)

    updated = [sharded_update(i, 1, None)[0] for i in range(4)]

    (other_win0,) = _exchange_call("swap_halves_l0_w_in", _swap_halves([gwin_h]))
    sums0 = add_halves([gwin_h], [other_win0], 0) + add_halves(small0, others_small0, 0, first_index=1)
    (dx, g_norm), scattered0 = _inproj_bwd_x(dproj, lw[0]["win"], xs[0], norm_g[0:1], dx, 0,
                                             ride=_scatter_chip_sums(sums0))
    gm["norm_g"] = g_norm
    per_layer[0] = gm
    grad_x = dx.reshape(nb, seq, D_MODEL)

    def stack(k, shape):
        return jnp.stack([per_layer[l][k] for l in range(DEPTH)]).reshape(shape)

    g_small = {"final_g": g_final.reshape(D_MODEL), "norm_g": stack("norm_g", (DEPTH, D_MODEL)),
               "pool_w": stack("pool_w", pool_w.shape), "sgu_w": stack("sgu_w", sgu_w.shape),
               "sgu_b": jnp.swapaxes(stack("sgu_b", (DEPTH, CHUNK, GROUP_W))[:, :, :N_GROUPS], 1, 2)}
    for k in ("pool_scale", "conv_b", "conv_ln_g", "conv_ln_b", "sgu_ln_g", "sgu_ln_b"):
        g_small[k] = stack(k, (DEPTH, BRANCH_W))
    g_packed, parts[0] = _allreduce_small(_pack_small(g_small), _share_halves(scattered0))
    d_packed, m_packed, v_packed = _adamw_packed(_pack_small(w), _pack_small(mom), _pack_small(var), g_packed)
    grads, deltas, new_m, new_v = {}, {}, {}, {}
    for tree, packed in ((grads, g_packed), (deltas, d_packed), (new_m, m_packed), (new_v, v_packed)):
        tree.update(_unpack_small(packed, w))

    for i, name in enumerate(names4):
        res, _ = sharded_update(i, 0, updated[i])
        for tree, r in zip((grads, deltas, new_m, new_v), res):
            if name == "conv_sc":
                tree["conv_w"] = r[:, :CONV_K]
                tree["sc_w"] = r[:, CS_ROWS // 2:CS_ROWS // 2 + SHORT_K]
            else:
                tree[name] = r.reshape(w[name].shape)

    return (loss, grad_x, *[grads[k] for k in WEIGHTS], *[deltas[k] for k in WEIGHTS],
            *[new_m[k] for k in WEIGHTS], *[new_v[k] for k in WEIGHTS])
```

```python
import functools

import jax
import jax.numpy as jnp
from jax import lax
from jax.experimental import pallas as pl
from jax.experimental.pallas import tpu as pltpu

F32 = jnp.float32
BF16 = jnp.bfloat16

D_MODEL = 1024
DEPTH = 2
N_BRANCH = 4
BRANCH_W = 512
N_GROUPS = 4
GROUP_W = 128
POOL_WINDOWS = (2, 4, 8, 16)
CONV_K = 31
SHORT_K = 3
CHUNK = 128
N_PIECES = 12
PIECE_COLS = N_PIECES * BRANCH_W
IN_COLS = PIECE_COLS + N_BRANCH * D_MODEL
N_CHIPS = 4
SHARD_COLS = IN_COLS // N_CHIPS
BR_SHARD = D_MODEL // N_CHIPS
COL_TILE = SHARD_COLS // 2
N_COL_TILES = IN_COLS // COL_TILE
GATE_BLOCK0 = PIECE_COLS // D_MODEL
RMS_EPS = 1e-6
LN_EPS = 1e-5
HALO = 32
CONV_ROWS = 32
SHORT_ROWS = 8
CS_ROWS = 64

ADAM_LR = 0.001
ADAM_B1 = 0.9
ADAM_B2 = 0.999
ADAM_EPS = 1e-08
ADAM_WD = 0.01
ADAM_STEP = 10

VMEM_LIMIT = 52 * 1024 * 1024
MESH_ID = pl.DeviceIdType.MESH
ANY = pl.BlockSpec(memory_space=pl.ANY)
VMEM_WHOLE = pl.BlockSpec(memory_space=pltpu.VMEM)

(P_X, P_GATE, C_A, C_B, C_GATE, G_U, G_V, G_GATE, S_B, S_C, S_X, S_GATE) = range(N_PIECES)


def _params(*sem):
    return pltpu.CompilerParams(dimension_semantics=sem, vmem_limit_bytes=VMEM_LIMIT)


def _sigmoid(v):
    return 0.5 * jnp.tanh(0.5 * v) + 0.5


def _silu(v):
    return v * _sigmoid(v)


def _silu_pair(v):
    s = _sigmoid(v)
    return v * s, s * (1.0 + v * (1.0 - s))


def _dot(a, b):
    return jnp.dot(a, b, preferred_element_type=F32)


def _dot_nt(a, b):
    return lax.dot_general(a, b, (((1,), (1,)), ((), ())), preferred_element_type=F32)


def _dot_tn(a, b):
    return lax.dot_general(a, b, (((0,), (0,)), ((), ())), preferred_element_type=F32)


def _rowsum(v):
    return jnp.sum(v, axis=0, keepdims=True)


def _lanemean(v):
    return jnp.mean(v, axis=-1, keepdims=True)


def _tile(n, want, mult=8):
    t = max(1, min(n, want))
    while n % t or (t % mult and t != n):
        t -= 1
    return t


def _place():
    x, y, c = lax.axis_index("x"), lax.axis_index("y"), lax.axis_index("c")
    chip = 2 * x + y
    peers = [(1 - x, y), (x, 1 - y), (1 - x, 1 - y)]
    return x, y, c, chip, peers


def _remote(src, dst, send_sem, recv_sem, dev):
    return pltpu.make_async_remote_copy(src_ref=src, dst_ref=dst, send_sem=send_sem, recv_sem=recv_sem,
                                        device_id=dev, device_id_type=MESH_ID)


class _Exchange:
    def __init__(self, inputs, out_shapes, plan, n_remote, n_local=0, aliases=None):
        self.inputs = list(inputs)
        self.out_shapes = list(out_shapes)
        self.plan = plan
        self.n_remote = n_remote
        self.n_local = n_local
        self.aliases = dict(aliases or {})

    def copies(self, in_refs, out_refs, send_sems, recv_sems, loc_sems):
        remote, local = self.plan(in_refs, out_refs)
        assert len(remote) == self.n_remote and len(local) == self.n_local
        cps = [_remote(s, d, send_sems.at[t], recv_sems.at[t], dev) for t, (s, d, dev) in enumerate(remote)]
        cps += [pltpu.make_async_copy(s, d, loc_sems.at[t]) for t, (s, d) in enumerate(local)]
        return cps

    def sem_shapes(self):
        return [pltpu.SemaphoreType.DMA((max(self.n_remote, 1),)), pltpu.SemaphoreType.DMA((max(self.n_remote, 1),)),
                pltpu.SemaphoreType.DMA((max(self.n_local, 1),))]


def _exchange_call(name, ex):
    n_ci = len(ex.inputs)

    def body(*refs):
        cins, couts = refs[:n_ci], refs[n_ci:n_ci + len(ex.out_shapes)]
        cps = ex.copies(cins, couts, *refs[n_ci + len(ex.out_shapes):])
        for cp in cps:
            cp.start()
        for cp in cps:
            cp.wait()

    return pl.pallas_call(
        body,
        name=name,
        in_specs=[ANY] * n_ci,
        out_specs=[ANY] * len(ex.out_shapes),
        out_shape=ex.out_shapes,
        scratch_shapes=ex.sem_shapes(),
        input_output_aliases=ex.aliases,
    )(*ex.inputs)


def _host_call(body, *, name, grid, in_specs, out_specs, out_shape, args, scratch_shapes=(), aliases=None, ride=None):
    n_in, n_out, n_scr = len(in_specs), len(out_specs), len(scratch_shapes)
    sem = ("arbitrary",) * len(grid)
    aliases = dict(aliases or {})
    if ride is None:
        outs = pl.pallas_call(body, name=name, grid=grid, in_specs=list(in_specs), out_specs=list(out_specs),
                              out_shape=list(out_shape), scratch_shapes=list(scratch_shapes),
                              input_output_aliases=aliases, compiler_params=_params(*sem))(*args)
        return list(outs), []
    n_ci, n_co = len(ride.inputs), len(ride.out_shapes)

    def full_body(*refs):
        ins, cins = refs[:n_in], refs[n_in:n_in + n_ci]
        o0 = n_in + n_ci
        outs, couts = refs[o0:o0 + n_out], refs[o0 + n_out:o0 + n_out + n_co]
        s0 = o0 + n_out + n_co
        scr, sems = refs[s0:s0 + n_scr], refs[s0 + n_scr:]
        first = functools.reduce(jnp.logical_and, [pl.program_id(d) == 0 for d in range(len(grid))])
        last = functools.reduce(jnp.logical_and, [pl.program_id(d) == grid[d] - 1 for d in range(len(grid))])

        @pl.when(first)
        def _():
            for cp in ride.copies(cins, couts, *sems):
                cp.start()

        body(*ins, *outs, *scr)

        @pl.when(last)
        def _():
            for cp in ride.copies(cins, couts, *sems):
                cp.wait()

    for ci, co in ride.aliases.items():
        aliases[n_in + ci] = n_out + co
    outs = pl.pallas_call(
        full_body, name=name, grid=grid, in_specs=list(in_specs) + [ANY] * n_ci,
        out_specs=list(out_specs) + [ANY] * n_co, out_shape=list(out_shape) + ride.out_shapes,
        scratch_shapes=list(scratch_shapes) + ride.sem_shapes(), input_output_aliases=aliases,
        compiler_params=_params(*sem))(*args, *ride.inputs)
    return list(outs[:n_out]), list(outs[n_out:])


def _allgather_layer0(shards):
    na = len(shards)

    def body(*refs):
        ins, outs = refs[:na], refs[na:2 * na]
        send_sems, recv_sems, fsend_sems, frecv_sems, loc_sems = refs[2 * na:]
        x, y, c, chip, peers = _place()
        sib = (x, y, 1 - c)
        locs = [pltpu.make_async_copy(ins[a].at[0], outs[a].at[chip], loc_sems.at[a]) for a in range(na)]
        for cp in locs:
            cp.start()
        pending = []
        for k, (px, py) in enumerate(peers):
            for a in range(na):
                cp = _remote(ins[a].at[0, c], outs[a].at[chip, c], send_sems.at[k * na + a],
                             recv_sems.at[k * na + a], (px, py, c))
                cp.start()
                pending.append(cp)
        for k, (px, py) in enumerate(peers):
            pchip = 2 * px + py
            for a in range(na):
                slab = outs[a].at[pchip, c]
                _remote(slab, slab, send_sems.at[k * na + a], recv_sems.at[k * na + a], (px, py, c)).wait_recv()
                cp = _remote(slab, slab, fsend_sems.at[k * na + a], frecv_sems.at[k * na + a], sib)
                cp.start()
                pending.append(cp)
        for k, (px, py) in enumerate(peers):
            pchip = 2 * px + py
            for a in range(na):
                slab = outs[a].at[pchip, 1 - c]
                _remote(slab, slab, fsend_sems.at[k * na + a], frecv_sems.at[k * na + a], sib).wait_recv()
        for cp in pending:
            cp.wait_send()
        for cp in locs:
            cp.wait()

    return pl.pallas_call(
        body,
        name="allgather_layer0",
        in_specs=[ANY] * na,
        out_specs=[ANY] * na,
        out_shape=[jax.ShapeDtypeStruct((N_CHIPS,) + a.shape[1:], a.dtype) for a in shards],
        scratch_shapes=[pltpu.SemaphoreType.DMA((3 * na,))] * 4 + [pltpu.SemaphoreType.DMA((na,))],
    )(*shards)


def _gather_layer1_first_half(win_s):
    def plan(cins, couts):
        _, _, c, chip, peers = _place()
        (src,), (dst,) = cins, couts
        return ([(src.at[1, 0], dst.at[chip, 0], (px, py, c)) for px, py in peers],
                [(src.at[1], dst.at[chip])])

    return _Exchange([win_s], [jax.ShapeDtypeStruct((N_CHIPS,) + win_s.shape[1:], win_s.dtype)], plan, 3, 1)


def _gather_layer1_second_half(win_s, win_f1):
    def plan(cins, couts):
        _, _, c, chip, peers = _place()
        src, (dst,) = cins[0], couts
        return [(src.at[1, 1], dst.at[chip, 1], (px, py, c)) for px, py in peers], []

    return _Exchange([win_s, win_f1], [jax.ShapeDtypeStruct(win_f1.shape, win_f1.dtype)], plan, 3, 0, aliases={1: 0})


def _gather_whole(shards, layer):
    na = len(shards)

    def plan(cins, couts):
        _, _, c, chip, peers = _place()
        remote = [(cins[a].at[layer], couts[a].at[chip], (px, py, c)) for a in range(na) for px, py in peers]
        return remote, [(cins[a].at[layer], couts[a].at[chip]) for a in range(na)]

    return _Exchange(shards, [jax.ShapeDtypeStruct((N_CHIPS,) + a.shape[1:], a.dtype) for a in shards], plan,
                     3 * na, na)


def _join(a, b):
    n_in, n_out = len(a.inputs), len(a.out_shapes)

    def plan(cins, couts):
        remote_a, local_a = a.plan(cins[:n_in], couts[:n_out])
        remote_b, local_b = b.plan(cins[n_in:], couts[n_out:])
        return remote_a + remote_b, local_a + local_b

    aliases = dict(a.aliases)
    aliases.update({n_in + i: n_out + o for i, o in b.aliases.items()})
    return _Exchange(a.inputs + b.inputs, a.out_shapes + b.out_shapes, plan, a.n_remote + b.n_remote,
                     a.n_local + b.n_local, aliases)


def _swap_halves(grads):
    def plan(cins, couts):
        x, y, c, _, _ = _place()
        return [(g.at[:, 1 - c], r, (x, y, 1 - c)) for g, r in zip(cins, couts)], []

    return _Exchange(grads, [jax.ShapeDtypeStruct(g.shape[:1] + g.shape[2:], g.dtype) for g in grads], plan,
                     len(grads))


def _scatter_chip_sums(sums):
    na = len(sums)

    def plan(cins, couts):
        _, _, c, chip, peers = _place()
        remote = [(cins[a].at[2 * px + py], couts[a].at[chip, c], (px, py, c)) for a in range(na) for px, py in peers]
        return remote, [(cins[a].at[chip], couts[a].at[chip, c]) for a in range(na)]

    return _Exchange(sums, [jax.ShapeDtypeStruct((N_CHIPS, 2) + s.shape[1:], s.dtype) for s in sums], plan,
                     3 * na, na)


def _share_halves(parts):
    na = len(parts)

    def plan(cins, couts):
        x, y, c, _, _ = _place()
        return [(p.at[:, c], p.at[:, c], (x, y, 1 - c)) for p in couts], []

    return _Exchange(parts, [jax.ShapeDtypeStruct(p.shape, p.dtype) for p in parts], plan, na, 0,
                     aliases={a: a for a in range(na)})


def _allreduce_small(packed, ride):
    rows = packed.shape[0]
    n_ci, n_co = len(ride.inputs), len(ride.out_shapes)

    def body(*refs):
        sg_ref, cins = refs[0], refs[1:1 + n_ci]
        res_ref, couts = refs[1 + n_ci], refs[2 + n_ci:2 + n_ci + n_co]
        sib_ref, cs_ref, rem_ref, s1_send, s1_recv, s2_send, s2_recv = refs[2 + n_ci + n_co:9 + n_ci + n_co]
        riding = ride.copies(cins, couts, *refs[9 + n_ci + n_co:])
        for cp in riding:
            cp.start()
        x, y, c, chip, peers = _place()
        cp = _remote(sg_ref, sib_ref, s1_send, s1_recv, (x, y, 1 - c))
        cp.start()
        cp.wait()
        cs_ref[...] = sg_ref[...] + sib_ref[...]
        rem_ref[chip] = cs_ref[...]
        cps = [_remote(cs_ref, rem_ref.at[chip], s2_send.at[k], s2_recv.at[chip], (px, py, c))
               for k, (px, py) in enumerate(peers)]
        for cp in cps:
            cp.start()
        for k, (px, py) in enumerate(peers):
            pchip = 2 * px + py
            _remote(cs_ref, rem_ref.at[pchip], s2_send.at[k], s2_recv.at[pchip], (px, py, c)).wait_recv()
        for cp in cps:
            cp.wait_send()
        res_ref[...] = ((rem_ref[0] + rem_ref[1]) + rem_ref[2]) + rem_ref[3]
        for cp in riding:
            cp.wait()

    outs = pl.pallas_call(
        body,
        name="allreduce_small",
        in_specs=[VMEM_WHOLE] + [ANY] * n_ci,
        out_specs=[VMEM_WHOLE] + [ANY] * n_co,
        out_shape=[jax.ShapeDtypeStruct((rows, GROUP_W), F32)] + ride.out_shapes,
        scratch_shapes=[
            pltpu.VMEM((rows, GROUP_W), F32),
            pltpu.VMEM((rows, GROUP_W), F32),
            pltpu.VMEM((N_CHIPS, rows, GROUP_W), F32),
            pltpu.SemaphoreType.DMA,
            pltpu.SemaphoreType.DMA,
            pltpu.SemaphoreType.DMA((3,)),
            pltpu.SemaphoreType.DMA((N_CHIPS,)),
        ] + ride.sem_shapes(),
        input_output_aliases={1 + i: 1 + o for i, o in ride.aliases.items()},
        compiler_params=pltpu.CompilerParams(vmem_limit_bytes=VMEM_LIMIT),
    )(packed, *ride.inputs)
    return outs[0], list(outs[1:])


def _swap_whole(packed):
    def plan(cins, couts):
        x, y, c, _, _ = _place()
        return [(cins[0], couts[0], (x, y, 1 - c))], []

    return _Exchange([packed], [jax.ShapeDtypeStruct(packed.shape, packed.dtype)], plan, 1)


def _spread_chip_sums(chip_sum):
    def plan(cins, couts):
        _, _, c, chip, peers = _place()
        return ([(cins[0], couts[0].at[chip], (px, py, c)) for px, py in peers], [(cins[0], couts[0].at[chip])])

    return _Exchange([chip_sum], [jax.ShapeDtypeStruct((N_CHIPS,) + chip_sum.shape, chip_sum.dtype)], plan, 3, 1)


def _add2(a, b, name):
    def body(a_ref, b_ref, o_ref):
        o_ref[...] = a_ref[...] + b_ref[...]

    return pl.pallas_call(body, name=name, in_specs=[VMEM_WHOLE] * 2, out_specs=VMEM_WHOLE,
                          out_shape=jax.ShapeDtypeStruct(a.shape, F32))(a, b)


def _sum_chips(parts, name):
    def body(p_ref, o_ref):
        o_ref[...] = ((p_ref[0] + p_ref[1]) + p_ref[2]) + p_ref[3]

    return pl.pallas_call(body, name=name, in_specs=[VMEM_WHOLE], out_specs=VMEM_WHOLE,
                          out_shape=jax.ShapeDtypeStruct(parts.shape[1:], F32))(parts)


def _add_halves(grad, other, core, name):
    n_shards, _, rows, cols = grad.shape
    tr = _tile(rows, max(16, (1 << 22) // (4 * cols) // 16 * 16), mult=16)

    def body(core_ref, g_ref, o_ref, out_ref):
        del core_ref
        out_ref[...] = (g_ref[...] + o_ref[...]).astype(BF16)

    return pl.pallas_call(
        body,
        name=name,
        grid_spec=pltpu.PrefetchScalarGridSpec(
            num_scalar_prefetch=1,
            grid=(n_shards, rows // tr),
            in_specs=[
                pl.BlockSpec((None, None, tr, cols), lambda s, i, core_ref: (s, core_ref[0], i, 0)),
                pl.BlockSpec((None, tr, cols), lambda s, i, core_ref: (s, i, 0)),
            ],
            out_specs=pl.BlockSpec((None, tr, cols), lambda s, i, core_ref: (s, i, 0)),
        ),
        out_shape=jax.ShapeDtypeStruct((n_shards, rows, cols), BF16),
        compiler_params=_params("arbitrary", "arbitrary"),
    )(core, grad, other)


def _inproj_fwd(x2, g_row, w_l, layer, ride=None):
    n = x2.shape[0]
    tm = _tile(n, 2048)

    def body(x_ref, g_ref, w_ref, p_ref, h_ref):
        @pl.when(pl.program_id(1) == 0)
        def _():
            xv = x_ref[...]
            r = lax.rsqrt(_lanemean(xv * xv) + RMS_EPS)
            h_ref[...] = (xv * r * g_ref[...]).astype(BF16)

        p_ref[...] = _dot(h_ref[...], w_ref[...]).astype(BF16)

    return _host_call(
        body,
        name=f"inproj_fwd_l{layer}",
        grid=(n // tm, N_COL_TILES),
        in_specs=[
            pl.BlockSpec((tm, D_MODEL), lambda i, j: (i, 0)),
            pl.BlockSpec((1, D_MODEL), lambda i, j: (0, 0)),
            pl.BlockSpec((None, D_MODEL, COL_TILE), lambda i, j: (j // 2, 0, j % 2)),
        ],
        out_specs=[
            pl.BlockSpec((tm, COL_TILE), lambda i, j: (i, j)),
            pl.BlockSpec((tm, D_MODEL), lambda i, j: (i, 0)),
        ],
        out_shape=[jax.ShapeDtypeStruct((n, IN_COLS), BF16), jax.ShapeDtypeStruct((n, D_MODEL), BF16)],
        args=[x2, g_row, w_l],
        ride=ride,
    )


def _layer_norm_parts(v, g, b):
    mu = _lanemean(v)
    d = v - mu
    rstd = lax.rsqrt(_lanemean(d * d) + LN_EPS)
    xh = d * rstd
    return xh, rstd, xh * g + b


def _window_sum(ref, slot, weight_row, n_taps, base, t_rows):
    total = None
    for b in range(8):
        group = None
        for a in range((base + n_taps - 1) // 8 + 1):
            k = 8 * a + b - base
            if 0 <= k < n_taps:
                term = weight_row(k) * ref[slot, 8 * a:8 * a + t_rows + 8, :]
                group = term if group is None else group + term
        if group is not None:
            part = group[b:b + t_rows, :]
            total = part if total is None else total + part
    return total


def _tap_grads(acc_ref, pad_ref, shift_ref, dy, ref, slot, n_taps, base, t_rows):
    pad_ref[0:8, :] = jnp.zeros((8, BRANCH_W), F32)
    pad_ref[8:8 + t_rows, :] = dy
    pad_ref[8 + t_rows:16 + t_rows, :] = jnp.zeros((8, BRANCH_W), F32)
    for b in range(8):
        taps = [(a, 8 * a + b - base) for a in range((base + n_taps - 1) // 8 + 1) if 0 <= 8 * a + b - base < n_taps]
        if not taps:
            continue
        shift_ref[...] = pad_ref[pl.ds(8 - b, t_rows + 8), :]
        for a, k in taps:
            prod = shift_ref[...] * ref[slot, 8 * a:8 * a + t_rows + 8, :]
            acc_ref[k] += prod.reshape((t_rows + 8) // 8, 8, BRANCH_W).sum(axis=0)


def _tril_mask():
    r = lax.broadcasted_iota(jnp.int32, (CHUNK, CHUNK), 0)
    c = lax.broadcasted_iota(jnp.int32, (CHUNK, CHUNK), 1)
    return r >= c


def _lanes(refs4):
    return jnp.concatenate([refs4[g] for g in range(N_CHIPS)], axis=1)


def _mixer_weight_specs():
    def whole(shape):
        nd = len(shape)
        return pl.BlockSpec(shape, lambda b, c: (0,) * nd)

    return [
        whole((N_GROUPS, GROUP_W, GROUP_W)),
        whole((1, BRANCH_W)),
        whole((N_CHIPS, CONV_ROWS, GROUP_W)),
        whole((1, BRANCH_W)),
        whole((1, BRANCH_W)),
        whole((1, BRANCH_W)),
        whole((1, BRANCH_W)),
        whole((1, BRANCH_W)),
        whole((N_GROUPS, CHUNK, CHUNK)),
        whole((CHUNK, BRANCH_W)),
        whole((N_CHIPS, SHORT_ROWS, GROUP_W)),
    ]


def _mixer_weight_args(mw):
    return [mw["pool_w"], mw["pool_scale"], mw["conv_w"], mw["conv_b"], mw["conv_ln_g"], mw["conv_ln_b"],
            mw["sgu_ln_g"], mw["sgu_ln_b"], mw["sgu_w"], mw["sgu_bias"], mw["sc_w"]]


def _mixers_fwd(proj3, mw, layer, ride=None):
    nb, seq, _ = proj3.shape
    t_rows = _tile(seq, 512)
    nc = seq // t_rows
    hb = t_rows // HALO

    def body(cur_ref, halo_ref, pw_ref, ps_ref, cw_ref, cb_ref, clg_ref, clb_ref, slg_ref, slb_ref, sw_ref,
             sbias_ref, scw_ref, z_ref, cacc_ref, ext_ref):
        c = pl.program_id(1)
        has_prev = c > 0
        row = lax.broadcasted_iota(jnp.int32, (t_rows, 1), 0)
        tpos = (c * t_rows + row + 1).astype(F32)

        def cur(k):
            return cur_ref[:, k * BRANCH_W:(k + 1) * BRANCH_W].astype(F32)

        def hal(k):
            return halo_ref[:, k * BRANCH_W:(k + 1) * BRANCH_W].astype(F32)

        def put_ext(slot, halo_val, cur_val):
            ext_ref[slot, 0:HALO, :] = jnp.where(has_prev, halo_val, 0.0)
            ext_ref[slot, HALO:HALO + t_rows, :] = cur_val
            ext_ref[slot, HALO + t_rows:HALO + t_rows + 8, :] = jnp.zeros((8, BRANCH_W), F32)

        px = cur(P_X)
        put_ext(0, hal(P_X), px)
        mixed = []
        for j, win in enumerate(POOL_WINDOWS):
            cols = slice(j * GROUP_W, (j + 1) * GROUP_W)
            s = px[:, cols]
            for i in range(1, win):
                s = s + ext_ref[0, pl.ds(HALO - i, t_rows), cols]
            pooled = s / jnp.minimum(tpos, float(win)) - px[:, cols]
            mixed.append(_dot(pooled.astype(BF16), pw_ref[j]))
        z_pool = jnp.concatenate(mixed, axis=1) * ps_ref[...] * _silu(cur(P_GATE))
        z_ref[:, 0:BRANCH_W] = z_pool.astype(BF16)

        put_ext(1, hal(C_A) * _sigmoid(hal(C_B)), cur(C_A) * _sigmoid(cur(C_B)))
        cw = _lanes(cw_ref)
        acc = cb_ref[...] + _window_sum(ext_ref, 1, lambda k: cw[k:k + 1, :], CONV_K, HALO - (CONV_K - 1), t_rows)
        cacc_ref[...] = acc
        _, _, ln = _layer_norm_parts(acc, clg_ref[...], clb_ref[...])
        z_ref[:, BRANCH_W:2 * BRANCH_W] = (_silu(ln) * _silu(cur(C_GATE))).astype(BF16)

        _, _, v = _layer_norm_parts(cur(G_V), slg_ref[...], slb_ref[...])
        vb = v.astype(BF16)
        mask = _tril_mask()
        wt = [jnp.where(mask, sw_ref[g], 0.0).astype(BF16) for g in range(N_GROUPS)]
        sp_rows = []
        for sub in range(t_rows // CHUNK):
            rows = slice(sub * CHUNK, (sub + 1) * CHUNK)
            sp_rows.append(jnp.concatenate(
                [_dot(wt[g], vb[rows, g * GROUP_W:(g + 1) * GROUP_W]) for g in range(N_GROUPS)], axis=1)
                + sbias_ref[...])
        sp = jnp.concatenate(sp_rows, axis=0)
        z_ref[:, 2 * BRANCH_W:3 * BRANCH_W] = (cur(G_U) * sp * _silu(cur(G_GATE))).astype(BF16)

        put_ext(2, hal(S_C) * hal(S_X), cur(S_C) * cur(S_X))
        scw = _lanes(scw_ref)
        cv = _window_sum(ext_ref, 2, lambda k: scw[k:k + 1, :], SHORT_K, HALO - (SHORT_K - 1), t_rows)
        z_ref[:, 3 * BRANCH_W:4 * BRANCH_W] = (cur(S_B) * cv * _silu(cur(S_GATE))).astype(BF16)

    return _host_call(
        body,
        name=f"mixers_fwd_l{layer}",
        grid=(nb, nc),
        in_specs=[
            pl.BlockSpec((None, t_rows, PIECE_COLS), lambda b, c: (b, c, 0)),
            pl.BlockSpec((None, HALO, PIECE_COLS), lambda b, c: (b, jnp.maximum(c * hb - 1, 0), 0)),
        ] + _mixer_weight_specs(),
        out_specs=[pl.BlockSpec((None, t_rows, N_BRANCH * BRANCH_W), lambda b, c: (b, c, 0)),
                   pl.BlockSpec((None, t_rows, BRANCH_W), lambda b, c: (b, c, 0))],
        out_shape=[jax.ShapeDtypeStruct((nb, seq, N_BRANCH * BRANCH_W), BF16),
                   jax.ShapeDtypeStruct((nb, seq, BRANCH_W), F32)],
        scratch_shapes=[pltpu.VMEM((3, HALO + t_rows + 8, BRANCH_W), F32)],
        args=[proj3, proj3, *_mixer_weight_args(mw)],
        ride=ride,
    )


MIXER_GRADS = ["pool_w", "pool_scale", "conv_w", "conv_b", "conv_ln_g", "conv_ln_b", "sgu_ln_g", "sgu_ln_b", "sgu_w",
               "sgu_b", "sc_w"]


def _mixers_bwd(proj3, dz3, cacc3, dproj3, mw, layer, ride=None):
    nb, seq, _ = proj3.shape
    t_rows = _tile(seq, 512)
    nc = seq // t_rows
    hb = t_rows // HALO

    def body(cur_ref, halo_ref, dz_ref, cacc_ref, pw_ref, ps_ref, cw_ref, cb_ref, clg_ref, clb_ref, slg_ref, slb_ref,
             sw_ref, sbias_ref, scw_ref, dp_in_ref, dp_ref, g_pw, g_ps, g_cw, g_cb, g_clg, g_clb, g_slg,
             g_slb, g_sw, g_sb, g_scw, ext_ref, nxt_ref, sb_acc, cw_acc, scw_acc, pad_ref, shift_ref):
        del dp_in_ref
        b = pl.program_id(0)
        r = pl.program_id(1)
        c = nc - 1 - r
        has_prev = c > 0
        row = lax.broadcasted_iota(jnp.int32, (t_rows, 1), 0)
        tpos = (c * t_rows + row + 1).astype(F32)

        @pl.when(jnp.logical_and(b == 0, r == 0))
        def _():
            for ref in (g_pw, g_ps, g_cb, g_clg, g_clb, g_slg, g_slb, g_sw, sb_acc, cw_acc, scw_acc):
                ref[...] = jnp.zeros_like(ref)

        @pl.when(r == 0)
        def _():
            nxt_ref[:, t_rows:t_rows + HALO, :] = jnp.zeros((3, HALO, BRANCH_W), F32)

        def cur(k):
            return cur_ref[:, k * BRANCH_W:(k + 1) * BRANCH_W].astype(F32)

        def hal(k):
            return halo_ref[:, k * BRANCH_W:(k + 1) * BRANCH_W].astype(F32)

        def dzp(k):
            return dz_ref[:, k * BRANCH_W:(k + 1) * BRANCH_W].astype(F32)

        def put_dp(k, val):
            dp_ref[:, k * BRANCH_W:(k + 1) * BRANCH_W] = val.astype(BF16)

        def put_ext(slot, halo_val, cur_val):
            ext_ref[slot, 0:HALO, :] = jnp.where(has_prev, halo_val, 0.0)
            ext_ref[slot, HALO:HALO + t_rows, :] = cur_val
            ext_ref[slot, HALO + t_rows:HALO + t_rows + 8, :] = jnp.zeros((8, BRANCH_W), F32)

        px = cur(P_X)
        put_ext(0, hal(P_X), px)
        pgate = cur(P_GATE)
        dz_pool = dzp(0)
        ps = ps_ref[...]
        pooled, mixed, cnts = [], [], []
        for j, win in enumerate(POOL_WINDOWS):
            cols = slice(j * GROUP_W, (j + 1) * GROUP_W)
            s = px[:, cols]
            for i in range(1, win):
                s = s + ext_ref[0, pl.ds(HALO - i, t_rows), cols]
            cnt = jnp.minimum(tpos, float(win))
            pj = s / cnt - px[:, cols]
            cnts.append(cnt)
            pooled.append(pj.astype(BF16))
            mixed.append(_dot(pooled[j], pw_ref[j]))
        mixed = jnp.concatenate(mixed, axis=1)
        pg_silu, pg_dsilu = _silu_pair(pgate)
        put_dp(P_GATE, dz_pool * (mixed * ps) * pg_dsilu)
        d_out = dz_pool * pg_silu
        g_ps[...] += _rowsum(d_out * mixed)
        d_mixed = (d_out * ps).astype(BF16)
        d_pooled = []
        for j in range(N_GROUPS):
            cols = slice(j * GROUP_W, (j + 1) * GROUP_W)
            g_pw[j] += _dot_tn(pooled[j], d_mixed[:, cols])
            dpj = _dot_nt(d_mixed[:, cols], pw_ref[j])
            d_pooled.append(dpj)
            nxt_ref[0, 0:t_rows, cols] = dpj / cnts[j]
        dpx = []
        for j, win in enumerate(POOL_WINDOWS):
            cols = slice(j * GROUP_W, (j + 1) * GROUP_W)
            s = nxt_ref[0, 0:t_rows, cols]
            for i in range(1, win):
                s = s + nxt_ref[0, pl.ds(i, t_rows), cols]
            dpx.append(s - d_pooled[j])
        put_dp(P_X, jnp.concatenate(dpx, axis=1))

        c_a = cur(C_A)
        sig_b = _sigmoid(cur(C_B))
        put_ext(1, hal(C_A) * _sigmoid(hal(C_B)), c_a * sig_b)
        cw = _lanes(cw_ref)
        xh, rstd, ln = _layer_norm_parts(cacc_ref[...], clg_ref[...], clb_ref[...])
        cgate = cur(C_GATE)
        dz_conv = dzp(1)
        ln_silu, ln_dsilu = _silu_pair(ln)
        cg_silu, cg_dsilu = _silu_pair(cgate)
        put_dp(C_GATE, dz_conv * ln_silu * cg_dsilu)
        d_ln = dz_conv * cg_silu * ln_dsilu
        g_clg[...] += _rowsum(d_ln * xh)
        g_clb[...] += _rowsum(d_ln)
        dxh = d_ln * clg_ref[...]
        dc = rstd * (dxh - _lanemean(dxh) - xh * _lanemean(dxh * xh))
        g_cb[...] += _rowsum(dc)
        nxt_ref[1, 0:t_rows, :] = dc
        _tap_grads(cw_acc, pad_ref, shift_ref, dc, ext_ref, 1, CONV_K, HALO - (CONV_K - 1), t_rows)
        dyg = _window_sum(nxt_ref, 1, lambda i: cw[CONV_K - 1 - i:CONV_K - i, :], CONV_K, 0, t_rows)
        put_dp(C_A, dyg * sig_b)
        put_dp(C_B, dyg * c_a * sig_b * (1.0 - sig_b))

        u = cur(G_U)
        ggate = cur(G_GATE)
        vxh, vrstd, v = _layer_norm_parts(cur(G_V), slg_ref[...], slb_ref[...])
        vb = v.astype(BF16)
        mask = _tril_mask()
        wt = [jnp.where(mask, sw_ref[g], 0.0) for g in range(N_GROUPS)]
        wt_b = [w.astype(BF16) for w in wt]
        wtt_b = [w.T.astype(BF16) for w in wt]
        dz_sgu = dzp(2)
        gg_silu, gg_dsilu = _silu_pair(ggate)
        d_sgu = dz_sgu * gg_silu
        d_sp = d_sgu * u
        d_spb = d_sp.astype(BF16)
        sp_rows, dv_rows = [], []
        for sub in range(t_rows // CHUNK):
            rows = slice(sub * CHUNK, (sub + 1) * CHUNK)
            sp_g, dv_g = [], []
            for g in range(N_GROUPS):
                cols = slice(g * GROUP_W, (g + 1) * GROUP_W)
                sp_g.append(_dot(wt_b[g], vb[rows, cols]))
                g_sw[g] += jnp.where(mask, _dot_nt(d_spb[rows, cols], vb[rows, cols]), 0.0)
                dv_g.append(_dot(wtt_b[g], d_spb[rows, cols]))
            sp_rows.append(jnp.concatenate(sp_g, axis=1) + sbias_ref[...])
            dv_rows.append(jnp.concatenate(dv_g, axis=1))
            sb_acc[...] += d_sp[rows, :]
        sp = jnp.concatenate(sp_rows, axis=0)
        dv = jnp.concatenate(dv_rows, axis=0)
        put_dp(G_GATE, dz_sgu * (u * sp) * gg_dsilu)
        put_dp(G_U, d_sgu * sp)
        g_slg[...] += _rowsum(dv * vxh)
        g_slb[...] += _rowsum(dv)
        dvx = dv * slg_ref[...]
        put_dp(G_V, vrstd * (dvx - _lanemean(dvx) - vxh * _lanemean(dvx * vxh)))

        s_b, s_c, s_x, sgate = cur(S_B), cur(S_C), cur(S_X), cur(S_GATE)
        put_ext(2, hal(S_C) * hal(S_X), s_c * s_x)
        scw = _lanes(scw_ref)
        cv = _window_sum(ext_ref, 2, lambda k: scw[k:k + 1, :], SHORT_K, HALO - (SHORT_K - 1), t_rows)
        dz_sc = dzp(3)
        sg_silu, sg_dsilu = _silu_pair(sgate)
        put_dp(S_GATE, dz_sc * (s_b * cv) * sg_dsilu)
        d_pre = dz_sc * sg_silu
        put_dp(S_B, d_pre * cv)
        dcv = d_pre * s_b
        nxt_ref[2, 0:t_rows, :] = dcv
        _tap_grads(scw_acc, pad_ref, shift_ref, dcv, ext_ref, 2, SHORT_K, HALO - (SHORT_K - 1), t_rows)
        du = _window_sum(nxt_ref, 2, lambda i: scw[SHORT_K - 1 - i:SHORT_K - i, :], SHORT_K, 0, t_rows)
        put_dp(S_C, du * s_x)
        put_dp(S_X, du * s_c)

        nxt_ref[:, t_rows:t_rows + HALO, :] = nxt_ref[:, 0:HALO, :]

        @pl.when(jnp.logical_and(b == nb - 1, r == nc - 1))
        def _():
            lane = lax.broadcasted_iota(jnp.int32, (CHUNK, GROUP_W), 1)
            out = jnp.zeros((CHUNK, GROUP_W), F32)
            for g in range(N_GROUPS):
                col = jnp.sum(sb_acc[:, g * GROUP_W:(g + 1) * GROUP_W], axis=1, keepdims=True)
                out = jnp.where(lane == g, col, out)
            g_sb[...] = out
            g_cw[...] = jnp.sum(cw_acc[...], axis=1)
            g_scw[...] = jnp.sum(scw_acc[...], axis=1)

    def acc_spec(shape):
        nd = len(shape)
        return pl.BlockSpec(shape, lambda b, r: (0,) * nd)

    acc_shapes = [
        (N_GROUPS, GROUP_W, GROUP_W),
        (1, BRANCH_W),
        (CONV_ROWS, BRANCH_W),
        (1, BRANCH_W),
        (1, BRANCH_W),
        (1, BRANCH_W),
        (1, BRANCH_W),
        (1, BRANCH_W),
        (N_GROUPS, CHUNK, CHUNK),
        (CHUNK, GROUP_W),
        (SHORT_ROWS, BRANCH_W),
    ]
    outs, rode = _host_call(
        body,
        name=f"mixers_bwd_l{layer}",
        grid=(nb, nc),
        in_specs=[
            pl.BlockSpec((None, t_rows, PIECE_COLS), lambda b, r: (b, nc - 1 - r, 0)),
            pl.BlockSpec((None, HALO, PIECE_COLS), lambda b, r: (b, jnp.maximum((nc - 1 - r) * hb - 1, 0), 0)),
            pl.BlockSpec((None, t_rows, N_BRANCH * BRANCH_W), lambda b, r: (b, nc - 1 - r, 0)),
            pl.BlockSpec((None, t_rows, BRANCH_W), lambda b, r: (b, nc - 1 - r, 0)),
        ] + _mixer_weight_specs() + [ANY],
        out_specs=[pl.BlockSpec((None, t_rows, PIECE_COLS), lambda b, r: (b, nc - 1 - r, 0))]
        + [acc_spec(s) for s in acc_shapes],
        out_shape=[jax.ShapeDtypeStruct(dproj3.shape, BF16)] + [jax.ShapeDtypeStruct(s, F32) for s in acc_shapes],
        scratch_shapes=[
            pltpu.VMEM((3, HALO + t_rows + 8, BRANCH_W), F32),
            pltpu.VMEM((3, t_rows + HALO, BRANCH_W), F32),
            pltpu.VMEM((CHUNK, BRANCH_W), F32),
            pltpu.VMEM((CONV_ROWS, 8, BRANCH_W), F32),
            pltpu.VMEM((SHORT_ROWS, 8, BRANCH_W), F32),
            pltpu.VMEM((t_rows + 16, BRANCH_W), F32),
            pltpu.VMEM((t_rows + 8, BRANCH_W), F32),
        ],
        aliases={4 + 11: 0},
        args=[proj3, proj3, dz3, cacc3, *_mixer_weight_args(mw), dproj3],
        ride=ride,
    )
    return outs[0], dict(zip(MIXER_GRADS, outs[1:])), rode


def _merge_fwd(z2, proj2, x2, wbr_l, wo_l, layer, ride=None):
    n = x2.shape[0]
    tm = _tile(n, 1024)

    def body(z_ref, gate_ref, x_ref, wbr_ref, wo_ref, xn_ref, mg_ref, acc_ref):
        nbr = pl.program_id(1)
        bo = _dot(z_ref[...], wbr_ref[nbr])
        contrib = _sigmoid(gate_ref[...].astype(F32)) * bo

        @pl.when(nbr == 0)
        def _():
            acc_ref[...] = contrib

        @pl.when(nbr > 0)
        def _():
            acc_ref[...] += contrib

        @pl.when(nbr == N_BRANCH - 1)
        def _():
            mg = acc_ref[...].astype(BF16)
            mg_ref[...] = mg
            xn_ref[...] = x_ref[...] + _dot(mg, wo_ref[...])

    return _host_call(
        body,
        name=f"merge_fwd_l{layer}",
        grid=(n // tm, N_BRANCH),
        in_specs=[
            pl.BlockSpec((tm, BRANCH_W), lambda i, b: (i, b)),
            pl.BlockSpec((tm, D_MODEL), lambda i, b: (i, GATE_BLOCK0 + b)),
            pl.BlockSpec((tm, D_MODEL), lambda i, b: (i, 0)),
            pl.BlockSpec((N_BRANCH, BRANCH_W, D_MODEL), lambda i, b: (0, 0, 0)),
            pl.BlockSpec((D_MODEL, D_MODEL), lambda i, b: (0, 0)),
        ],
        out_specs=[
            pl.BlockSpec((tm, D_MODEL), lambda i, b: (i, 0)),
            pl.BlockSpec((tm, D_MODEL), lambda i, b: (i, 0)),
        ],
        out_shape=[jax.ShapeDtypeStruct((n, D_MODEL), F32), jax.ShapeDtypeStruct((n, D_MODEL), BF16)],
        scratch_shapes=[pltpu.VMEM((tm, D_MODEL), F32)],
        args=[z2, proj2, x2, wbr_l, wo_l],
        ride=ride,
    )


def _merge_bwd(dout2, merged2, z2, proj2, wbr_l, wo_l, layer, ride=None):
    n = dout2.shape[0]
    tm = _tile(n, 512)
    nt = n // tm

    def body(do_ref, mg_ref, z_ref, gate_ref, wbr_ref, wo_ref, dz_ref, dg_ref, gwo_ref, gwbr_ref, dm_ref,
             awo_ref, awbr_ref):
        i = pl.program_id(0)
        nbr = pl.program_id(1)

        @pl.when(nbr == 0)
        def _():
            do_b = do_ref[...].astype(BF16)
            dm_ref[...] = _dot_nt(do_b, wo_ref[...])
            gw = _dot_tn(mg_ref[...], do_b)

            @pl.when(i == 0)
            def _():
                awo_ref[...] = gw

            @pl.when(i > 0)
            def _():
                awo_ref[...] += gw

        zt = z_ref[...]
        wbr = wbr_ref[nbr]
        bo = _dot(zt, wbr)
        gt = _sigmoid(gate_ref[...].astype(F32))
        dm = dm_ref[...]
        dbo = (dm * gt).astype(BF16)
        dg_ref[...] = (dm * bo * gt * (1.0 - gt)).astype(BF16)
        dz_ref[...] = _dot_nt(dbo, wbr).astype(BF16)
        gw = _dot_tn(zt, dbo)

        @pl.when(i == 0)
        def _():
            awbr_ref[nbr] = gw

        @pl.when(i > 0)
        def _():
            awbr_ref[nbr] += gw

        @pl.when(jnp.logical_and(i == nt - 1, nbr == N_BRANCH - 1))
        def _():
            pltpu.sync_copy(awo_ref, gwo_ref)
            pltpu.sync_copy(awbr_ref, gwbr_ref)

    return _host_call(
        body,
        name=f"merge_bwd_l{layer}",
        grid=(nt, N_BRANCH),
        in_specs=[
            pl.BlockSpec((tm, D_MODEL), lambda i, b: (i, 0)),
            pl.BlockSpec((tm, D_MODEL), lambda i, b: (i, 0)),
            pl.BlockSpec((tm, BRANCH_W), lambda i, b: (i, b)),
            pl.BlockSpec((tm, D_MODEL), lambda i, b: (i, GATE_BLOCK0 + b)),
            pl.BlockSpec((N_BRANCH, BRANCH_W, D_MODEL), lambda i, b: (0, 0, 0)),
            pl.BlockSpec((D_MODEL, D_MODEL), lambda i, b: (0, 0)),
        ],
        out_specs=[
            pl.BlockSpec((tm, BRANCH_W), lambda i, b: (i, b)),
            pl.BlockSpec((tm, D_MODEL), lambda i, b: (i, GATE_BLOCK0 + b)),
            ANY,
            ANY,
        ],
        out_shape=[
            jax.ShapeDtypeStruct((n, N_BRANCH * BRANCH_W), BF16),
            jax.ShapeDtypeStruct((n, IN_COLS), BF16),
            jax.ShapeDtypeStruct((D_MODEL, D_MODEL), F32),
            jax.ShapeDtypeStruct((N_BRANCH, BRANCH_W, D_MODEL), F32),
        ],
        scratch_shapes=[
            pltpu.VMEM((tm, D_MODEL), F32),
            pltpu.VMEM((D_MODEL, D_MODEL), F32),
            pltpu.VMEM((N_BRANCH, BRANCH_W, D_MODEL), F32),
        ],
        args=[dout2, merged2, z2, proj2, wbr_l, wo_l],
        ride=ride,
    )


def _loss_head(x2, g_row, tgt2):
    n = x2.shape[0]
    tm = _tile(n, 512)

    def body(x_ref, g_ref, t_ref, dx_ref, loss_ref, dg_ref):
        @pl.when(pl.program_id(0) == 0)
        def _():
            loss_ref[...] = jnp.zeros_like(loss_ref)
            dg_ref[...] = jnp.zeros_like(dg_ref)

        xv = x_ref[...]
        g = g_ref[...]
        r = lax.rsqrt(_lanemean(xv * xv) + RMS_EPS)
        xh = xv * r
        err = xh * g - t_ref[...]
        loss_ref[...] += 0.5 * jnp.sum(_lanemean(err * err), axis=0, keepdims=True)
        dy = err * (1.0 / D_MODEL)
        dg_ref[...] += _rowsum(dy * xh)
        dxh = dy * g
        dx_ref[...] = r * (dxh - xh * _lanemean(dxh * xh))

    return pl.pallas_call(
        body,
        name="loss_head",
        grid=(n // tm,),
        in_specs=[
            pl.BlockSpec((tm, D_MODEL), lambda i: (i, 0)),
            pl.BlockSpec((1, D_MODEL), lambda i: (0, 0)),
            pl.BlockSpec((tm, D_MODEL), lambda i: (i, 0)),
        ],
        out_specs=[
            pl.BlockSpec((tm, D_MODEL), lambda i: (i, 0)),
            pl.BlockSpec((1, GROUP_W), lambda i: (0, 0)),
            pl.BlockSpec((1, D_MODEL), lambda i: (0, 0)),
        ],
        out_shape=[jax.ShapeDtypeStruct((n, D_MODEL), F32), jax.ShapeDtypeStruct((1, GROUP_W), F32),
                   jax.ShapeDtypeStruct((1, D_MODEL), F32)],
        compiler_params=_params("arbitrary"),
    )(x2, g_row, tgt2)


def _inproj_bwd_x(dproj2, w_l, x2, g_row, dout2, layer, ride=None):
    n = x2.shape[0]
    tm = _tile(n, 1024)

    def body(dp_ref, w_ref, x_ref, g_ref, do_ref, dx_ref, dng_ref, dh_ref):
        i = pl.program_id(0)
        s = pl.program_id(1)
        part = _dot_nt(dp_ref[...], w_ref[...])

        @pl.when(s == 0)
        def _():
            dh_ref[...] = part

        @pl.when(s > 0)
        def _():
            dh_ref[...] += part

        @pl.when(jnp.logical_and(i == 0, s == 0))
        def _():
            dng_ref[...] = jnp.zeros_like(dng_ref)

        @pl.when(s == N_COL_TILES - 1)
        def _():
            xv = x_ref[...]
            r = lax.rsqrt(_lanemean(xv * xv) + RMS_EPS)
            xh = xv * r
            dh = dh_ref[...]
            dng_ref[...] += _rowsum(dh * xh)
            dxh = dh * g_ref[...]
            dx_ref[...] = do_ref[...] + r * (dxh - xh * _lanemean(dxh * xh))

    return _host_call(
        body,
        name=f"inproj_bwd_x_l{layer}",
        grid=(n // tm, N_COL_TILES),
        in_specs=[
            pl.BlockSpec((tm, COL_TILE), lambda i, s: (i, s)),
            pl.BlockSpec((None, D_MODEL, COL_TILE), lambda i, s: (s // 2, 0, s % 2)),
            pl.BlockSpec((tm, D_MODEL), lambda i, s: (i, 0)),
            pl.BlockSpec((1, D_MODEL), lambda i, s: (0, 0)),
            pl.BlockSpec((tm, D_MODEL), lambda i, s: (i, 0)),
        ],
        out_specs=[
            pl.BlockSpec((tm, D_MODEL), lambda i, s: (i, 0)),
            pl.BlockSpec((1, D_MODEL), lambda i, s: (0, 0)),
        ],
        out_shape=[jax.ShapeDtypeStruct((n, D_MODEL), F32), jax.ShapeDtypeStruct((1, D_MODEL), F32)],
        scratch_shapes=[pltpu.VMEM((tm, D_MODEL), F32)],
        args=[dproj2, w_l, x2, g_row, dout2],
        ride=ride,
    )


def _inproj_bwd_w(h2, dproj2, layer, ride=None):
    n = h2.shape[0]
    tm = _tile(n, 2048)

    def body(h_ref, dp_ref, gw_ref):
        gw = _dot_tn(h_ref[...], dp_ref[...])

        @pl.when(pl.program_id(1) == 0)
        def _():
            gw_ref[...] = gw

        @pl.when(pl.program_id(1) > 0)
        def _():
            gw_ref[...] += gw

    outs, rode = _host_call(
        body,
        name=f"inproj_bwd_w_l{layer}",
        grid=(N_COL_TILES, n // tm),
        in_specs=[
            pl.BlockSpec((tm, D_MODEL), lambda s, i: (i, 0)),
            pl.BlockSpec((tm, COL_TILE), lambda s, i: (i, s)),
        ],
        out_specs=[pl.BlockSpec((None, D_MODEL, COL_TILE), lambda s, i: (s // 2, 0, s % 2))],
        out_shape=[jax.ShapeDtypeStruct((N_CHIPS, D_MODEL, SHARD_COLS), F32)],
        args=[h2, dproj2],
        ride=ride,
    )
    return outs[0], rode


def _adamw_math(w, g, m, v):
    m = ADAM_B1 * m + (1.0 - ADAM_B1) * g
    v = ADAM_B2 * v + (1.0 - ADAM_B2) * jnp.square(g)
    m_hat = m / (1.0 - ADAM_B1 ** ADAM_STEP)
    v_hat = v / (1.0 - ADAM_B2 ** ADAM_STEP)
    delta = -ADAM_LR * (m_hat / (jnp.sqrt(v_hat) + ADAM_EPS) + ADAM_WD * w)
    return delta, m, v


def _adamw_sharded(w, m, v, part, layer, prev, name, ride=None):
    _, rows, cols = w.shape
    tr = _tile(rows, max(16, (1 << 21) // (4 * cols) // 16 * 16), mult=16)

    def body(w_ref, m_ref, v_ref, p_ref, *rest):
        g_out, d_out, m_out, v_out = rest[-4:]
        g = ((p_ref[0].astype(F32) + p_ref[1].astype(F32)) + p_ref[2].astype(F32)) + p_ref[3].astype(F32)
        delta, m2, v2 = _adamw_math(w_ref[...], g, m_ref[...], v_ref[...])
        g_out[...] = g
        d_out[...] = delta
        m_out[...] = m2
        v_out[...] = v2

    spec = pl.BlockSpec((None, tr, cols), lambda i: (layer, i, 0))
    return _host_call(
        body,
        name=f"adamw_{name}_l{layer}",
        grid=(rows // tr,),
        in_specs=[spec, spec, spec, pl.BlockSpec((N_CHIPS, tr, cols), lambda i: (0, i, 0))]
        + ([ANY] * 4 if prev else []),
        out_specs=[spec] * 4,
        out_shape=[jax.ShapeDtypeStruct(w.shape, F32)] * 4,
        args=[w, m, v, part] + (list(prev) if prev else []),
        aliases={4 + k: k for k in range(4)} if prev else {},
        ride=ride,
    )


def _adamw_packed(w, m, v, g):
    rows = w.shape[0]
    tr = _tile(rows, rows // 2 if rows % 16 == 0 else rows)

    def body(w_ref, m_ref, v_ref, g_ref, d_out, m_out, v_out):
        delta, m2, v2 = _adamw_math(w_ref[...], g_ref[...], m_ref[...], v_ref[...])
        d_out[...] = delta
        m_out[...] = m2
        v_out[...] = v2

    spec = pl.BlockSpec((tr, GROUP_W), lambda i: (i, 0))
    return pl.pallas_call(
        body,
        name="adamw_small",
        grid=(rows // tr,),
        in_specs=[spec] * 4,
        out_specs=[spec] * 3,
        out_shape=[jax.ShapeDtypeStruct(w.shape, F32)] * 3,
        compiler_params=_params("arbitrary"),
    )(w, m, v, g)


SMALL = ["norm_g", "pool_w", "pool_scale", "conv_b", "conv_ln_g", "conv_ln_b", "sgu_ln_g", "sgu_ln_b", "sgu_w",
         "sgu_b", "final_g"]
WEIGHTS = ["norm_g", "w_in", "pool_w", "pool_scale", "conv_w", "conv_b", "conv_ln_g", "conv_ln_b", "sgu_ln_g",
           "sgu_ln_b", "sgu_w", "sgu_b", "sc_w", "w_branch", "w_o", "final_g"]
HALF_SHAPES = [(D_MODEL // 2, SHARD_COLS), (N_BRANCH * BRANCH_W // 2, BR_SHARD), (BR_SHARD // 2, D_MODEL),
               (CS_ROWS // 2, GROUP_W)]


def _pack_small(tree):
    return jnp.concatenate([tree[k].reshape(-1, GROUP_W) for k in SMALL], axis=0)


def _unpack_small(packed, like):
    out, r = {}, 0
    for k in SMALL:
        nr = like[k].size // GROUP_W
        out[k] = packed[r:r + nr].reshape(like[k].shape)
        r += nr
    return out


def _pad_rows(a, rows):
    pad = [(0, 0)] * a.ndim
    pad[-2] = (0, rows - a.shape[-2])
    return jnp.pad(a, pad)


def _pack_cs(conv, short):
    return jnp.concatenate([_pad_rows(conv, CS_ROWS // 2), _pad_rows(short, CS_ROWS // 2)], axis=-2)


def _shard_major_rows(a):
    return a.reshape(a.shape[0], N_CHIPS, GROUP_W).transpose(1, 0, 2)


def kernel(x, norm_g, w_in, pool_w, pool_scale, conv_w, conv_b, conv_ln_g, conv_ln_b, sgu_ln_g, sgu_ln_b, sgu_w, sgu_b, sc_w, w_branch, w_o, final_g, loss_target, m_norm_g, m_w_in, m_pool_w, m_pool_scale, m_conv_w, m_conv_b, m_conv_ln_g, m_conv_ln_b, m_sgu_ln_g, m_sgu_ln_b, m_sgu_w, m_sgu_b, m_sc_w, m_w_branch, m_w_o, m_final_g, v_norm_g, v_w_in, v_pool_w, v_pool_scale, v_conv_w, v_conv_b, v_conv_ln_g, v_conv_ln_b, v_sgu_ln_g, v_sgu_ln_b, v_sgu_w, v_sgu_b, v_sc_w, v_w_branch, v_w_o, v_final_g):
    w = dict(norm_g=norm_g, w_in=w_in, pool_w=pool_w, pool_scale=pool_scale, conv_w=conv_w, conv_b=conv_b,
             conv_ln_g=conv_ln_g, conv_ln_b=conv_ln_b, sgu_ln_g=sgu_ln_g, sgu_ln_b=sgu_ln_b, sgu_w=sgu_w,
             sgu_b=sgu_b, sc_w=sc_w, w_branch=w_branch, w_o=w_o, final_g=final_g)
    mom = dict(norm_g=m_norm_g, w_in=m_w_in, pool_w=m_pool_w, pool_scale=m_pool_scale, conv_w=m_conv_w,
               conv_b=m_conv_b, conv_ln_g=m_conv_ln_g, conv_ln_b=m_conv_ln_b, sgu_ln_g=m_sgu_ln_g,
               sgu_ln_b=m_sgu_ln_b, sgu_w=m_sgu_w, sgu_b=m_sgu_b, sc_w=m_sc_w, w_branch=m_w_branch, w_o=m_w_o,
               final_g=m_final_g)
    var = dict(norm_g=v_norm_g, w_in=v_w_in, pool_w=v_pool_w, pool_scale=v_pool_scale, conv_w=v_conv_w,
               conv_b=v_conv_b, conv_ln_g=v_conv_ln_g, conv_ln_b=v_conv_ln_b, sgu_ln_g=v_sgu_ln_g,
               sgu_ln_b=v_sgu_ln_b, sgu_w=v_sgu_w, sgu_b=v_sgu_b, sc_w=v_sc_w, w_branch=v_w_branch, w_o=v_w_o,
               final_g=v_final_g)

    nb, seq, _ = x.shape
    n = nb * seq
    core = lax.axis_index("c").astype(jnp.int32).reshape(1)

    win_s = w_in.astype(BF16).reshape(DEPTH, 2, D_MODEL // 2, SHARD_COLS)
    wbr_s = w_branch.astype(BF16)
    wo_s = w_o.astype(BF16)
    cs_s = _pack_cs(conv_w, sc_w)

    (win_f0,) = _allgather_layer0([win_s])

    def layer_weights(win_f, wbr_f, wo_f, cs_f):
        cs_f = cs_f.reshape(N_CHIPS, CS_ROWS, GROUP_W)
        return dict(win=win_f.reshape(N_CHIPS, D_MODEL, SHARD_COLS),
                    wbr=wbr_f.reshape(N_CHIPS, N_BRANCH, BRANCH_W, BR_SHARD).transpose(1, 2, 0, 3).reshape(
                        N_BRANCH, BRANCH_W, D_MODEL),
                    wo=wo_f.reshape(D_MODEL, D_MODEL),
                    conv_w=cs_f[:, :CONV_ROWS], sc_w=cs_f[:, CS_ROWS // 2:CS_ROWS // 2 + SHORT_ROWS])

    def mixer_weights(l, gathered):
        row = lambda a: a[l].reshape(1, BRANCH_W)
        bias = jnp.repeat(jnp.swapaxes(sgu_b[l], 0, 1), GROUP_W, axis=1)
        return dict(pool_w=pool_w[l].astype(BF16), pool_scale=row(pool_scale), conv_w=gathered["conv_w"],
                    conv_b=row(conv_b), conv_ln_g=row(conv_ln_g), conv_ln_b=row(conv_ln_b), sgu_ln_g=row(sgu_ln_g),
                    sgu_ln_b=row(sgu_ln_b), sgu_w=sgu_w[l], sgu_bias=bias, sc_w=gathered["sc_w"])

    lw, mw = [None, None], [None, None]

    xs, projs, hs, zs, mgs, caccs = [x.reshape(n, D_MODEL)], [], [], [], [], []
    for l in range(DEPTH):
        first = l == 0
        win_l = win_f0.reshape(N_CHIPS, D_MODEL, SHARD_COLS) if first else lw[1]["win"]
        ride = _join(_gather_whole([wbr_s, wo_s, cs_s], 0), _gather_layer1_first_half(win_s)) if first else None
        (proj, h), rode = _inproj_fwd(xs[l], norm_g[l:l + 1], win_l, l, ride=ride)
        if first:
            lw[0] = layer_weights(win_f0, *rode[:3])
            mw[0] = mixer_weights(0, lw[0])
            win_f1 = rode[3]
        (z3, cacc), rode = _mixers_fwd(proj.reshape(nb, seq, IN_COLS), mw[l], l,
                                       ride=_gather_layer1_second_half(win_s, win_f1) if first else None)
        caccs.append(cacc)
        if first:
            win_f1 = rode[0]
        z = z3.reshape(n, N_BRANCH * BRANCH_W)
        (x_new, merged), rode = _merge_fwd(z, proj, xs[l], lw[l]["wbr"], lw[l]["wo"], l,
                                           ride=_gather_whole([wbr_s, wo_s, cs_s], 1) if first else None)
        if first:
            lw[1] = layer_weights(win_f1, *rode)
            mw[1] = mixer_weights(1, lw[1])
        projs.append(proj)
        hs.append(h)
        zs.append(z)
        mgs.append(merged)
        xs.append(x_new)

    dx, loss_part, g_final = _loss_head(xs[DEPTH], final_g.reshape(1, D_MODEL), loss_target.reshape(n, D_MODEL))
    loss = lax.psum(loss_part[0, 0], ("x", "y", "c"))

    def chip_major(gwbr):
        return gwbr.reshape(N_BRANCH, BRANCH_W, N_CHIPS, BR_SHARD).transpose(2, 0, 1, 3)

    def as_halves(arrays, first_index=0):
        return [g.reshape((N_CHIPS, 2) + s) for g, s in zip(arrays, HALF_SHAPES[first_index:])]

    def add_halves(grads, others, l, first_index=0):
        return [_add_halves(g, o, core, f"add_halves_l{l}_{first_index + i}")
                for i, (g, o) in enumerate(zip(grads, others))]

    per_layer, parts = {}, [None] * DEPTH
    (dz, dproj, gwo, gwbr), _ = _merge_bwd(dx, mgs[1], zs[1], projs[1], lw[1]["wbr"], lw[1]["wo"], 1)
    dproj3, gm, _ = _mixers_bwd(projs[1].reshape(nb, seq, IN_COLS), dz.reshape(nb, seq, N_BRANCH * BRANCH_W), caccs[1],
                                dproj.reshape(nb, seq, IN_COLS), mw[1], 1)
    dproj = dproj3.reshape(n, IN_COLS)
    gwin, _ = _inproj_bwd_w(hs[1], dproj, 1)
    gcs = _pack_cs(_shard_major_rows(gm["conv_w"]), _shard_major_rows(gm["sc_w"]))
    grads1 = as_halves([gwin, chip_major(gwbr), gwo, gcs])
    (dx, g_norm), others1 = _inproj_bwd_x(dproj, lw[1]["win"], xs[1], norm_g[1:2], dx, 1, ride=_swap_halves(grads1))
    gm["norm_g"] = g_norm
    per_layer[1] = gm
    sums1 = add_halves(grads1, others1, 1)
    (dz, dproj, gwo, gwbr), scattered1 = _merge_bwd(dx, mgs[0], zs[0], projs[0], lw[0]["wbr"], lw[0]["wo"], 0,
                                                    ride=_scatter_chip_sums(sums1))
    dproj3, gm, parts[1] = _mixers_bwd(projs[0].reshape(nb, seq, IN_COLS), dz.reshape(nb, seq, N_BRANCH * BRANCH_W),
                                       caccs[0], dproj.reshape(nb, seq, IN_COLS), mw[0], 0,
                                       ride=_share_halves(scattered1))
    dproj = dproj3.reshape(n, IN_COLS)
    gcs = _pack_cs(_shard_major_rows(gm["conv_w"]), _shard_major_rows(gm["sc_w"]))
    small0 = as_halves([chip_major(gwbr), gwo, gcs], first_index=1)

    per_layer[0] = dict(gm, norm_g=jnp.zeros((1, D_MODEL), F32))

    def stack(k, shape):
        return jnp.stack([per_layer[l][k] for l in range(DEPTH)]).reshape(shape)

    g_small = {"final_g": g_final.reshape(D_MODEL), "norm_g": stack("norm_g", (DEPTH, D_MODEL)),
               "pool_w": stack("pool_w", pool_w.shape), "sgu_w": stack("sgu_w", sgu_w.shape),
               "sgu_b": jnp.swapaxes(stack("sgu_b", (DEPTH, CHUNK, GROUP_W))[:, :, :N_GROUPS], 1, 2)}
    for k in ("pool_scale", "conv_b", "conv_ln_g", "conv_ln_b", "sgu_ln_g", "sgu_ln_b"):
        g_small[k] = stack(k, (DEPTH, BRANCH_W))
    small_part = _pack_small(g_small)

    gwin, rode = _inproj_bwd_w(hs[0], dproj, 0, ride=_join(_swap_halves(small0), _swap_whole(small_part)))
    others_small0, small_other = rode[:3], rode[3]
    small_chip = _add2(small_part, small_other, "add_small_sibling")
    (gwin_h,) = as_halves([gwin])

    def flat(a, i):
        rows, cols = 2 * HALF_SHAPES[i][0], HALF_SHAPES[i][1]
        return a.reshape(a.shape[0], rows, cols)

    names4 = ["w_in", "w_branch", "w_o", "conv_sc"]

    def packed4(t):
        return [t["w_in"], t["w_branch"], t["w_o"], _pack_cs(t["conv_w"], t["sc_w"])]

    w4, m4, v4 = ([flat(a, i) for i, a in enumerate(packed4(t))] for t in (w, mom, var))

    def sharded_update(i, layer, prev, ride=None):
        return _adamw_sharded(w4[i], m4[i], v4[i], flat(parts[layer][i], i), layer, prev, names4[i], ride=ride)

    updated = [sharded_update(i, 1, None)[0] for i in range(4)]

    (other_win0,) = _exchange_call("swap_halves_l0_w_in", _swap_halves([gwin_h]))
    sums0 = add_halves([gwin_h], [other_win0], 0) + add_halves(small0, others_small0, 0, first_index=1)
    (dx, g_norm0), rode = _inproj_bwd_x(dproj, lw[0]["win"], xs[0], norm_g[0:1], dx, 0,
                                        ride=_join(_scatter_chip_sums(sums0), _spread_chip_sums(small_chip)))
    scattered0, small_chips = rode[:4], rode[4]
    grad_x = dx.reshape(nb, seq, D_MODEL)

    g_norm0, parts[0] = _allreduce_small(g_norm0.reshape(D_MODEL // GROUP_W, GROUP_W), _share_halves(scattered0))
    g_packed = lax.dynamic_update_slice(_sum_chips(small_chips, "sum_small_chips"), g_norm0, (0, 0))
    d_packed, m_packed, v_packed = _adamw_packed(_pack_small(w), _pack_small(mom), _pack_small(var), g_packed)
    grads, deltas, new_m, new_v = {}, {}, {}, {}
    for tree, packed in ((grads, g_packed), (deltas, d_packed), (new_m, m_packed), (new_v, v_packed)):
        tree.update(_unpack_small(packed, w))

    for i, name in enumerate(names4):
        res, _ = sharded_update(i, 0, updated[i])
        for tree, r in zip((grads, deltas, new_m, new_v), res):
            if name == "conv_sc":
                tree["conv_w"] = r[:, :CONV_K]
                tree["sc_w"] = r[:, CS_ROWS // 2:CS_ROWS // 2 + SHORT_K]
            else:
                tree[name] = r.reshape(w[name].shape)

    return (loss, grad_x, *[grads[k] for k in WEIGHTS], *[deltas[k] for k in WEIGHTS],
            *[new_m[k] for k in WEIGHTS], *[new_v[k] for k in WEIGHTS])
```

```python
import functools

import jax
import jax.numpy as jnp
from jax import lax
from jax.experimental import pallas as pl
from jax.experimental.pallas import tpu as pltpu

F32 = jnp.float32
BF16 = jnp.bfloat16

D_MODEL = 1024
DEPTH = 2
N_BRANCH = 4
BRANCH_W = 512
N_GROUPS = 4
GROUP_W = 128
POOL_WINDOWS = (2, 4, 8, 16)
CONV_K = 31
SHORT_K = 3
CHUNK = 128
N_PIECES = 12
PIECE_COLS = N_PIECES * BRANCH_W
IN_COLS = PIECE_COLS + N_BRANCH * D_MODEL
N_CHIPS = 4
SHARD_COLS = IN_COLS // N_CHIPS
BR_SHARD = D_MODEL // N_CHIPS
COL_TILE = SHARD_COLS // 2
N_COL_TILES = IN_COLS // COL_TILE
GATE_BLOCK0 = PIECE_COLS // D_MODEL
RMS_EPS = 1e-6
LN_EPS = 1e-5
HALO = 32
CONV_ROWS = 32
SHORT_ROWS = 8
CS_ROWS = 64

ADAM_LR = 0.001
ADAM_B1 = 0.9
ADAM_B2 = 0.999
ADAM_EPS = 1e-08
ADAM_WD = 0.01
ADAM_STEP = 10

VMEM_LIMIT = 60 * 1024 * 1024
MESH_ID = pl.DeviceIdType.MESH
ANY = pl.BlockSpec(memory_space=pl.ANY)
VMEM_WHOLE = pl.BlockSpec(memory_space=pltpu.VMEM)

(P_X, P_GATE, C_A, C_B, C_GATE, G_U, G_V, G_GATE, S_B, S_C, S_X, S_GATE) = range(N_PIECES)


def _params(*sem):
    return pltpu.CompilerParams(dimension_semantics=sem, vmem_limit_bytes=VMEM_LIMIT)


def _sigmoid(v):
    return 0.5 * jnp.tanh(0.5 * v) + 0.5


def _silu(v):
    return v * _sigmoid(v)


def _silu_pair(v):
    s = _sigmoid(v)
    return v * s, s * (1.0 + v * (1.0 - s))


def _dot(a, b):
    return jnp.dot(a, b, preferred_element_type=F32)


def _dot_nt(a, b):
    return lax.dot_general(a, b, (((1,), (1,)), ((), ())), preferred_element_type=F32)


def _dot_tn(a, b):
    return lax.dot_general(a, b, (((0,), (0,)), ((), ())), preferred_element_type=F32)


def _rowsum(v):
    return jnp.sum(v, axis=0, keepdims=True)


def _lanemean(v):
    return jnp.mean(v, axis=-1, keepdims=True)


def _tile(n, want, mult=8):
    t = max(1, min(n, want))
    while n % t or (t % mult and t != n):
        t -= 1
    return t


def _place():
    x, y, c = lax.axis_index("x"), lax.axis_index("y"), lax.axis_index("c")
    chip = 2 * x + y
    peers = [(1 - x, y), (x, 1 - y), (1 - x, 1 - y)]
    return x, y, c, chip, peers


def _remote(src, dst, send_sem, recv_sem, dev):
    return pltpu.make_async_remote_copy(src_ref=src, dst_ref=dst, send_sem=send_sem, recv_sem=recv_sem,
                                        device_id=dev, device_id_type=MESH_ID)


class _Exchange:
    def __init__(self, inputs, out_shapes, plan, n_remote, n_local=0, aliases=None):
        self.inputs = list(inputs)
        self.out_shapes = list(out_shapes)
        self.plan = plan
        self.n_remote = n_remote
        self.n_local = n_local
        self.aliases = dict(aliases or {})

    def copies(self, in_refs, out_refs, send_sems, recv_sems, loc_sems):
        remote, local = self.plan(in_refs, out_refs)
        assert len(remote) == self.n_remote and len(local) == self.n_local
        cps = [_remote(s, d, send_sems.at[t], recv_sems.at[t], dev) for t, (s, d, dev) in enumerate(remote)]
        cps += [pltpu.make_async_copy(s, d, loc_sems.at[t]) for t, (s, d) in enumerate(local)]
        return cps

    def sem_shapes(self):
        return [pltpu.SemaphoreType.DMA((max(self.n_remote, 1),)), pltpu.SemaphoreType.DMA((max(self.n_remote, 1),)),
                pltpu.SemaphoreType.DMA((max(self.n_local, 1),))]


def _exchange_call(name, ex):
    n_ci = len(ex.inputs)

    def body(*refs):
        cins, couts = refs[:n_ci], refs[n_ci:n_ci + len(ex.out_shapes)]
        cps = ex.copies(cins, couts, *refs[n_ci + len(ex.out_shapes):])
        for cp in cps:
            cp.start()
        for cp in cps:
            cp.wait()

    return pl.pallas_call(
        body,
        name=name,
        in_specs=[ANY] * n_ci,
        out_specs=[ANY] * len(ex.out_shapes),
        out_shape=ex.out_shapes,
        scratch_shapes=ex.sem_shapes(),
        input_output_aliases=ex.aliases,
    )(*ex.inputs)


def _host_call(body, *, name, grid, in_specs, out_specs, out_shape, args, scratch_shapes=(), aliases=None, ride=None):
    n_in, n_out, n_scr = len(in_specs), len(out_specs), len(scratch_shapes)
    sem = ("arbitrary",) * len(grid)
    aliases = dict(aliases or {})
    if ride is None:
        outs = pl.pallas_call(body, name=name, grid=grid, in_specs=list(in_specs), out_specs=list(out_specs),
                              out_shape=list(out_shape), scratch_shapes=list(scratch_shapes),
                              input_output_aliases=aliases, compiler_params=_params(*sem))(*args)
        return list(outs), []
    n_ci, n_co = len(ride.inputs), len(ride.out_shapes)

    def full_body(*refs):
        ins, cins = refs[:n_in], refs[n_in:n_in + n_ci]
        o0 = n_in + n_ci
        outs, couts = refs[o0:o0 + n_out], refs[o0 + n_out:o0 + n_out + n_co]
        s0 = o0 + n_out + n_co
        scr, sems = refs[s0:s0 + n_scr], refs[s0 + n_scr:]
        first = functools.reduce(jnp.logical_and, [pl.program_id(d) == 0 for d in range(len(grid))])
        last = functools.reduce(jnp.logical_and, [pl.program_id(d) == grid[d] - 1 for d in range(len(grid))])

        @pl.when(first)
        def _():
            for cp in ride.copies(cins, couts, *sems):
                cp.start()

        body(*ins, *outs, *scr)

        @pl.when(last)
        def _():
            for cp in ride.copies(cins, couts, *sems):
                cp.wait()

    for ci, co in ride.aliases.items():
        aliases[n_in + ci] = n_out + co
    outs = pl.pallas_call(
        full_body, name=name, grid=grid, in_specs=list(in_specs) + [ANY] * n_ci,
        out_specs=list(out_specs) + [ANY] * n_co, out_shape=list(out_shape) + ride.out_shapes,
        scratch_shapes=list(scratch_shapes) + ride.sem_shapes(), input_output_aliases=aliases,
        compiler_params=_params(*sem))(*args, *ride.inputs)
    return list(outs[:n_out]), list(outs[n_out:])


def _allgather_layer0(shards):
    na = len(shards)

    def body(*refs):
        ins, outs = refs[:na], refs[na:2 * na]
        send_sems, recv_sems, fsend_sems, frecv_sems, loc_sems = refs[2 * na:]
        x, y, c, chip, peers = _place()
        sib = (x, y, 1 - c)
        locs = [pltpu.make_async_copy(ins[a].at[0], outs[a].at[chip], loc_sems.at[a]) for a in range(na)]
        for cp in locs:
            cp.start()
        pending = []
        for k, (px, py) in enumerate(peers):
            for a in range(na):
                cp = _remote(ins[a].at[0, c], outs[a].at[chip, c], send_sems.at[k * na + a],
                             recv_sems.at[k * na + a], (px, py, c))
                cp.start()
                pending.append(cp)
        for k, (px, py) in enumerate(peers):
            pchip = 2 * px + py
            for a in range(na):
                slab = outs[a].at[pchip, c]
                _remote(slab, slab, send_sems.at[k * na + a], recv_sems.at[k * na + a], (px, py, c)).wait_recv()
                cp = _remote(slab, slab, fsend_sems.at[k * na + a], frecv_sems.at[k * na + a], sib)
                cp.start()
                pending.append(cp)
        for k, (px, py) in enumerate(peers):
            pchip = 2 * px + py
            for a in range(na):
                slab = outs[a].at[pchip, 1 - c]
                _remote(slab, slab, fsend_sems.at[k * na + a], frecv_sems.at[k * na + a], sib).wait_recv()
        for cp in pending:
            cp.wait_send()
        for cp in locs:
            cp.wait()

    return pl.pallas_call(
        body,
        name="allgather_layer0",
        in_specs=[ANY] * na,
        out_specs=[ANY] * na,
        out_shape=[jax.ShapeDtypeStruct((N_CHIPS,) + a.shape[1:], a.dtype) for a in shards],
        scratch_shapes=[pltpu.SemaphoreType.DMA((3 * na,))] * 4 + [pltpu.SemaphoreType.DMA((na,))],
    )(*shards)


def _gather_layer1_first_half(win_s):
    def plan(cins, couts):
        _, _, c, chip, peers = _place()
        (src,), (dst,) = cins, couts
        return ([(src.at[1, 0], dst.at[chip, 0], (px, py, c)) for px, py in peers],
                [(src.at[1], dst.at[chip])])

    return _Exchange([win_s], [jax.ShapeDtypeStruct((N_CHIPS,) + win_s.shape[1:], win_s.dtype)], plan, 3, 1)


def _gather_layer1_second_half(win_s, win_f1):
    def plan(cins, couts):
        _, _, c, chip, peers = _place()
        src, (dst,) = cins[0], couts
        return [(src.at[1, 1], dst.at[chip, 1], (px, py, c)) for px, py in peers], []

    return _Exchange([win_s, win_f1], [jax.ShapeDtypeStruct(win_f1.shape, win_f1.dtype)], plan, 3, 0, aliases={1: 0})


def _gather_whole(shards, layer):
    na = len(shards)

    def plan(cins, couts):
        _, _, c, chip, peers = _place()
        remote = [(cins[a].at[layer], couts[a].at[chip], (px, py, c)) for a in range(na) for px, py in peers]
        return remote, [(cins[a].at[layer], couts[a].at[chip]) for a in range(na)]

    return _Exchange(shards, [jax.ShapeDtypeStruct((N_CHIPS,) + a.shape[1:], a.dtype) for a in shards], plan,
                     3 * na, na)


def _join(a, b):
    n_in, n_out = len(a.inputs), len(a.out_shapes)

    def plan(cins, couts):
        remote_a, local_a = a.plan(cins[:n_in], couts[:n_out])
        remote_b, local_b = b.plan(cins[n_in:], couts[n_out:])
        return remote_a + remote_b, local_a + local_b

    aliases = dict(a.aliases)
    aliases.update({n_in + i: n_out + o for i, o in b.aliases.items()})
    return _Exchange(a.inputs + b.inputs, a.out_shapes + b.out_shapes, plan, a.n_remote + b.n_remote,
                     a.n_local + b.n_local, aliases)


def _swap_halves(grads):
    def plan(cins, couts):
        x, y, c, _, _ = _place()
        return [(g.at[:, 1 - c], r, (x, y, 1 - c)) for g, r in zip(cins, couts)], []

    return _Exchange(grads, [jax.ShapeDtypeStruct(g.shape[:1] + g.shape[2:], g.dtype) for g in grads], plan,
                     len(grads))


def _scatter_chip_sums(sums):
    na = len(sums)

    def plan(cins, couts):
        _, _, c, chip, peers = _place()
        remote = [(cins[a].at[2 * px + py], couts[a].at[chip, c], (px, py, c)) for a in range(na) for px, py in peers]
        return remote, [(cins[a].at[chip], couts[a].at[chip, c]) for a in range(na)]

    return _Exchange(sums, [jax.ShapeDtypeStruct((N_CHIPS, 2) + s.shape[1:], s.dtype) for s in sums], plan,
                     3 * na, na)


def _share_halves(parts):
    na = len(parts)

    def plan(cins, couts):
        x, y, c, _, _ = _place()
        return [(p.at[:, c], p.at[:, c], (x, y, 1 - c)) for p in couts], []

    return _Exchange(parts, [jax.ShapeDtypeStruct(p.shape, p.dtype) for p in parts], plan, na, 0,
                     aliases={a: a for a in range(na)})


def _allreduce_small(packed, ride):
    rows = packed.shape[0]
    n_ci, n_co = len(ride.inputs), len(ride.out_shapes)

    def body(*refs):
        sg_ref, cins = refs[0], refs[1:1 + n_ci]
        res_ref, couts = refs[1 + n_ci], refs[2 + n_ci:2 + n_ci + n_co]
        sib_ref, cs_ref, rem_ref, s1_send, s1_recv, s2_send, s2_recv = refs[2 + n_ci + n_co:9 + n_ci + n_co]
        riding = ride.copies(cins, couts, *refs[9 + n_ci + n_co:])
        for cp in riding:
            cp.start()
        x, y, c, chip, peers = _place()
        cp = _remote(sg_ref, sib_ref, s1_send, s1_recv, (x, y, 1 - c))
        cp.start()
        cp.wait()
        cs_ref[...] = sg_ref[...] + sib_ref[...]
        rem_ref[chip] = cs_ref[...]
        cps = [_remote(cs_ref, rem_ref.at[chip], s2_send.at[k], s2_recv.at[chip], (px, py, c))
               for k, (px, py) in enumerate(peers)]
        for cp in cps:
            cp.start()
        for k, (px, py) in enumerate(peers):
            pchip = 2 * px + py
            _remote(cs_ref, rem_ref.at[pchip], s2_send.at[k], s2_recv.at[pchip], (px, py, c)).wait_recv()
        for cp in cps:
            cp.wait_send()
        res_ref[...] = ((rem_ref[0] + rem_ref[1]) + rem_ref[2]) + rem_ref[3]
        for cp in riding:
            cp.wait()

    outs = pl.pallas_call(
        body,
        name="allreduce_small",
        in_specs=[VMEM_WHOLE] + [ANY] * n_ci,
        out_specs=[VMEM_WHOLE] + [ANY] * n_co,
        out_shape=[jax.ShapeDtypeStruct((rows, GROUP_W), F32)] + ride.out_shapes,
        scratch_shapes=[
            pltpu.VMEM((rows, GROUP_W), F32),
            pltpu.VMEM((rows, GROUP_W), F32),
            pltpu.VMEM((N_CHIPS, rows, GROUP_W), F32),
            pltpu.SemaphoreType.DMA,
            pltpu.SemaphoreType.DMA,
            pltpu.SemaphoreType.DMA((3,)),
            pltpu.SemaphoreType.DMA((N_CHIPS,)),
        ] + ride.sem_shapes(),
        input_output_aliases={1 + i: 1 + o for i, o in ride.aliases.items()},
        compiler_params=pltpu.CompilerParams(vmem_limit_bytes=VMEM_LIMIT),
    )(packed, *ride.inputs)
    return outs[0], list(outs[1:])


def _swap_whole(packed):
    def plan(cins, couts):
        x, y, c, _, _ = _place()
        return [(cins[0], couts[0], (x, y, 1 - c))], []

    return _Exchange([packed], [jax.ShapeDtypeStruct(packed.shape, packed.dtype)], plan, 1)


def _spread_chip_sums(chip_sum):
    def plan(cins, couts):
        _, _, c, chip, peers = _place()
        return ([(cins[0], couts[0].at[chip], (px, py, c)) for px, py in peers], [(cins[0], couts[0].at[chip])])

    return _Exchange([chip_sum], [jax.ShapeDtypeStruct((N_CHIPS,) + chip_sum.shape, chip_sum.dtype)], plan, 3, 1)


def _add2(a, b, name):
    def body(a_ref, b_ref, o_ref):
        o_ref[...] = a_ref[...] + b_ref[...]

    return pl.pallas_call(body, name=name, in_specs=[VMEM_WHOLE] * 2, out_specs=VMEM_WHOLE,
                          out_shape=jax.ShapeDtypeStruct(a.shape, F32))(a, b)


def _sum_chips(parts, name):
    def body(p_ref, o_ref):
        o_ref[...] = ((p_ref[0] + p_ref[1]) + p_ref[2]) + p_ref[3]

    return pl.pallas_call(body, name=name, in_specs=[VMEM_WHOLE], out_specs=VMEM_WHOLE,
                          out_shape=jax.ShapeDtypeStruct(parts.shape[1:], F32))(parts)


def _add_halves(grad, other, core, name):
    n_shards, _, rows, cols = grad.shape
    tr = _tile(rows, max(16, (1 << 22) // (4 * cols) // 16 * 16), mult=16)

    def body(core_ref, g_ref, o_ref, out_ref):
        del core_ref
        out_ref[...] = (g_ref[...] + o_ref[...]).astype(BF16)

    return pl.pallas_call(
        body,
        name=name,
        grid_spec=pltpu.PrefetchScalarGridSpec(
            num_scalar_prefetch=1,
            grid=(n_shards, rows // tr),
            in_specs=[
                pl.BlockSpec((None, None, tr, cols), lambda s, i, core_ref: (s, core_ref[0], i, 0)),
                pl.BlockSpec((None, tr, cols), lambda s, i, core_ref: (s, i, 0)),
            ],
            out_specs=pl.BlockSpec((None, tr, cols), lambda s, i, core_ref: (s, i, 0)),
        ),
        out_shape=jax.ShapeDtypeStruct((n_shards, rows, cols), BF16),
        compiler_params=_params("arbitrary", "arbitrary"),
    )(core, grad, other)


def _inproj_fwd(x2, g_row, w_l, layer, ride=None):
    n = x2.shape[0]
    tm = _tile(n, 2048)

    def body(x_ref, g_ref, w_ref, p_ref, h_ref):
        @pl.when(pl.program_id(1) == 0)
        def _():
            xv = x_ref[...]
            r = lax.rsqrt(_lanemean(xv * xv) + RMS_EPS)
            h_ref[...] = (xv * r * g_ref[...]).astype(BF16)

        p_ref[...] = _dot(h_ref[...], w_ref[...]).astype(BF16)

    return _host_call(
        body,
        name=f"inproj_fwd_l{layer}",
        grid=(n // tm, N_COL_TILES),
        in_specs=[
            pl.BlockSpec((tm, D_MODEL), lambda i, j: (i, 0)),
            pl.BlockSpec((1, D_MODEL), lambda i, j: (0, 0)),
            pl.BlockSpec((None, D_MODEL, COL_TILE), lambda i, j: (j // 2, 0, j % 2)),
        ],
        out_specs=[
            pl.BlockSpec((tm, COL_TILE), lambda i, j: (i, j)),
            pl.BlockSpec((tm, D_MODEL), lambda i, j: (i, 0)),
        ],
        out_shape=[jax.ShapeDtypeStruct((n, IN_COLS), BF16), jax.ShapeDtypeStruct((n, D_MODEL), BF16)],
        args=[x2, g_row, w_l],
        ride=ride,
    )


def _layer_norm_parts(v, g, b):
    mu = _lanemean(v)
    d = v - mu
    rstd = lax.rsqrt(_lanemean(d * d) + LN_EPS)
    xh = d * rstd
    return xh, rstd, xh * g + b


def _window_sum(ref, slot, weight_row, n_taps, base, t_rows):
    total = None
    for b in range(8):
        group = None
        for a in range((base + n_taps - 1) // 8 + 1):
            k = 8 * a + b - base
            if 0 <= k < n_taps:
                term = weight_row(k) * ref[slot, 8 * a:8 * a + t_rows + 8, :]
                group = term if group is None else group + term
        if group is not None:
            part = group[b:b + t_rows, :]
            total = part if total is None else total + part
    return total


def _tap_grads(acc_ref, pad_ref, shift_ref, dy, ref, slot, n_taps, base, t_rows):
    pad_ref[0:8, :] = jnp.zeros((8, BRANCH_W), F32)
    pad_ref[8:8 + t_rows, :] = dy
    pad_ref[8 + t_rows:16 + t_rows, :] = jnp.zeros((8, BRANCH_W), F32)
    for b in range(8):
        taps = [(a, 8 * a + b - base) for a in range((base + n_taps - 1) // 8 + 1) if 0 <= 8 * a + b - base < n_taps]
        if not taps:
            continue
        shift_ref[...] = pad_ref[pl.ds(8 - b, t_rows + 8), :]
        for a, k in taps:
            prod = shift_ref[...] * ref[slot, 8 * a:8 * a + t_rows + 8, :]
            acc_ref[k] += prod.reshape((t_rows + 8) // 8, 8, BRANCH_W).sum(axis=0)


def _tril_mask():
    r = lax.broadcasted_iota(jnp.int32, (CHUNK, CHUNK), 0)
    c = lax.broadcasted_iota(jnp.int32, (CHUNK, CHUNK), 1)
    return r >= c


def _lanes(refs4):
    return jnp.concatenate([refs4[g] for g in range(N_CHIPS)], axis=1)


def _mixer_weight_specs():
    def whole(shape):
        nd = len(shape)
        return pl.BlockSpec(shape, lambda b, c: (0,) * nd)

    return [
        whole((N_GROUPS, GROUP_W, GROUP_W)),
        whole((1, BRANCH_W)),
        whole((N_CHIPS, CONV_ROWS, GROUP_W)),
        whole((1, BRANCH_W)),
        whole((1, BRANCH_W)),
        whole((1, BRANCH_W)),
        whole((1, BRANCH_W)),
        whole((1, BRANCH_W)),
        whole((N_GROUPS, CHUNK, CHUNK)),
        whole((CHUNK, BRANCH_W)),
        whole((N_CHIPS, SHORT_ROWS, GROUP_W)),
    ]


def _mixer_weight_args(mw):
    return [mw["pool_w"], mw["pool_scale"], mw["conv_w"], mw["conv_b"], mw["conv_ln_g"], mw["conv_ln_b"],
            mw["sgu_ln_g"], mw["sgu_ln_b"], mw["sgu_w"], mw["sgu_bias"], mw["sc_w"]]


def _mixers_fwd(proj3, mw, layer, ride=None):
    nb, seq, _ = proj3.shape
    t_rows = _tile(seq, 512)
    nc = seq // t_rows
    hb = t_rows // HALO

    def body(cur_ref, halo_ref, pw_ref, ps_ref, cw_ref, cb_ref, clg_ref, clb_ref, slg_ref, slb_ref, sw_ref,
             sbias_ref, scw_ref, z_ref, cacc_ref, ext_ref):
        c = pl.program_id(1)
        has_prev = c > 0
        row = lax.broadcasted_iota(jnp.int32, (t_rows, 1), 0)
        tpos = (c * t_rows + row + 1).astype(F32)

        def cur(k):
            return cur_ref[:, k * BRANCH_W:(k + 1) * BRANCH_W].astype(F32)

        def hal(k):
            return halo_ref[:, k * BRANCH_W:(k + 1) * BRANCH_W].astype(F32)

        def put_ext(slot, halo_val, cur_val):
            ext_ref[slot, 0:HALO, :] = jnp.where(has_prev, halo_val, 0.0)
            ext_ref[slot, HALO:HALO + t_rows, :] = cur_val
            ext_ref[slot, HALO + t_rows:HALO + t_rows + 8, :] = jnp.zeros((8, BRANCH_W), F32)

        px = cur(P_X)
        put_ext(0, hal(P_X), px)
        mixed = []
        for j, win in enumerate(POOL_WINDOWS):
            cols = slice(j * GROUP_W, (j + 1) * GROUP_W)
            s = px[:, cols]
            for i in range(1, win):
                s = s + ext_ref[0, pl.ds(HALO - i, t_rows), cols]
            pooled = s / jnp.minimum(tpos, float(win)) - px[:, cols]
            mixed.append(_dot(pooled.astype(BF16), pw_ref[j]))
        z_pool = jnp.concatenate(mixed, axis=1) * ps_ref[...] * _silu(cur(P_GATE))
        z_ref[:, 0:BRANCH_W] = z_pool.astype(BF16)

        put_ext(1, hal(C_A) * _sigmoid(hal(C_B)), cur(C_A) * _sigmoid(cur(C_B)))
        cw = _lanes(cw_ref)
        acc = cb_ref[...] + _window_sum(ext_ref, 1, lambda k: cw[k:k + 1, :], CONV_K, HALO - (CONV_K - 1), t_rows)
        cacc_ref[...] = acc
        _, _, ln = _layer_norm_parts(acc, clg_ref[...], clb_ref[...])
        z_ref[:, BRANCH_W:2 * BRANCH_W] = (_silu(ln) * _silu(cur(C_GATE))).astype(BF16)

        _, _, v = _layer_norm_parts(cur(G_V), slg_ref[...], slb_ref[...])
        vb = v.astype(BF16)
        mask = _tril_mask()
        wt = [jnp.where(mask, sw_ref[g], 0.0).astype(BF16) for g in range(N_GROUPS)]
        sp_rows = []
        for sub in range(t_rows // CHUNK):
            rows = slice(sub * CHUNK, (sub + 1) * CHUNK)
            sp_rows.append(jnp.concatenate(
                [_dot(wt[g], vb[rows, g * GROUP_W:(g + 1) * GROUP_W]) for g in range(N_GROUPS)], axis=1)
                + sbias_ref[...])
        sp = jnp.concatenate(sp_rows, axis=0)
        z_ref[:, 2 * BRANCH_W:3 * BRANCH_W] = (cur(G_U) * sp * _silu(cur(G_GATE))).astype(BF16)

        put_ext(2, hal(S_C) * hal(S_X), cur(S_C) * cur(S_X))
        scw = _lanes(scw_ref)
        cv = _window_sum(ext_ref, 2, lambda k: scw[k:k + 1, :], SHORT_K, HALO - (SHORT_K - 1), t_rows)
        z_ref[:, 3 * BRANCH_W:4 * BRANCH_W] = (cur(S_B) * cv * _silu(cur(S_GATE))).astype(BF16)

    return _host_call(
        body,
        name=f"mixers_fwd_l{layer}",
        grid=(nb, nc),
        in_specs=[
            pl.BlockSpec((None, t_rows, PIECE_COLS), lambda b, c: (b, c, 0)),
            pl.BlockSpec((None, HALO, PIECE_COLS), lambda b, c: (b, jnp.maximum(c * hb - 1, 0), 0)),
        ] + _mixer_weight_specs(),
        out_specs=[pl.BlockSpec((None, t_rows, N_BRANCH * BRANCH_W), lambda b, c: (b, c, 0)),
                   pl.BlockSpec((None, t_rows, BRANCH_W), lambda b, c: (b, c, 0))],
        out_shape=[jax.ShapeDtypeStruct((nb, seq, N_BRANCH * BRANCH_W), BF16),
                   jax.ShapeDtypeStruct((nb, seq, BRANCH_W), F32)],
        scratch_shapes=[pltpu.VMEM((3, HALO + t_rows + 8, BRANCH_W), F32)],
        args=[proj3, proj3, *_mixer_weight_args(mw)],
        ride=ride,
    )


MIXER_GRADS = ["pool_w", "pool_scale", "conv_w", "conv_b", "conv_ln_g", "conv_ln_b", "sgu_ln_g", "sgu_ln_b", "sgu_w",
               "sgu_b", "sc_w"]


def _mixers_bwd(proj3, dz3, cacc3, dproj3, mw, layer, ride=None):
    nb, seq, _ = proj3.shape
    t_rows = _tile(seq, 512)
    nc = seq // t_rows
    hb = t_rows // HALO

    def body(cur_ref, halo_ref, dz_ref, cacc_ref, pw_ref, ps_ref, cw_ref, cb_ref, clg_ref, clb_ref, slg_ref, slb_ref,
             sw_ref, sbias_ref, scw_ref, dp_in_ref, dp_ref, g_pw, g_ps, g_cw, g_cb, g_clg, g_clb, g_slg,
             g_slb, g_sw, g_sb, g_scw, ext_ref, nxt_ref, sb_acc, cw_acc, scw_acc, pad_ref, shift_ref):
        del dp_in_ref
        b = pl.program_id(0)
        r = pl.program_id(1)
        c = nc - 1 - r
        has_prev = c > 0
        row = lax.broadcasted_iota(jnp.int32, (t_rows, 1), 0)
        tpos = (c * t_rows + row + 1).astype(F32)

        @pl.when(jnp.logical_and(b == 0, r == 0))
        def _():
            for ref in (g_pw, g_ps, g_cb, g_clg, g_clb, g_slg, g_slb, g_sw, sb_acc, cw_acc, scw_acc):
                ref[...] = jnp.zeros_like(ref)

        @pl.when(r == 0)
        def _():
            nxt_ref[:, t_rows:t_rows + HALO, :] = jnp.zeros((3, HALO, BRANCH_W), F32)

        def cur(k):
            return cur_ref[:, k * BRANCH_W:(k + 1) * BRANCH_W].astype(F32)

        def hal(k):
            return halo_ref[:, k * BRANCH_W:(k + 1) * BRANCH_W].astype(F32)

        def dzp(k):
            return dz_ref[:, k * BRANCH_W:(k + 1) * BRANCH_W].astype(F32)

        def put_dp(k, val):
            dp_ref[:, k * BRANCH_W:(k + 1) * BRANCH_W] = val.astype(BF16)

        def put_ext(slot, halo_val, cur_val):
            ext_ref[slot, 0:HALO, :] = jnp.where(has_prev, halo_val, 0.0)
            ext_ref[slot, HALO:HALO + t_rows, :] = cur_val
            ext_ref[slot, HALO + t_rows:HALO + t_rows + 8, :] = jnp.zeros((8, BRANCH_W), F32)

        px = cur(P_X)
        put_ext(0, hal(P_X), px)
        pgate = cur(P_GATE)
        dz_pool = dzp(0)
        ps = ps_ref[...]
        pooled, mixed, cnts = [], [], []
        for j, win in enumerate(POOL_WINDOWS):
            cols = slice(j * GROUP_W, (j + 1) * GROUP_W)
            s = px[:, cols]
            for i in range(1, win):
                s = s + ext_ref[0, pl.ds(HALO - i, t_rows), cols]
            cnt = jnp.minimum(tpos, float(win))
            pj = s / cnt - px[:, cols]
            cnts.append(cnt)
            pooled.append(pj.astype(BF16))
            mixed.append(_dot(pooled[j], pw_ref[j]))
        mixed = jnp.concatenate(mixed, axis=1)
        pg_silu, pg_dsilu = _silu_pair(pgate)
        put_dp(P_GATE, dz_pool * (mixed * ps) * pg_dsilu)
        d_out = dz_pool * pg_silu
        g_ps[...] += _rowsum(d_out * mixed)
        d_mixed = (d_out * ps).astype(BF16)
        d_pooled = []
        for j in range(N_GROUPS):
            cols = slice(j * GROUP_W, (j + 1) * GROUP_W)
            g_pw[j] += _dot_tn(pooled[j], d_mixed[:, cols])
            dpj = _dot_nt(d_mixed[:, cols], pw_ref[j])
            d_pooled.append(dpj)
            nxt_ref[0, 0:t_rows, cols] = dpj / cnts[j]
        dpx = []
        for j, win in enumerate(POOL_WINDOWS):
            cols = slice(j * GROUP_W, (j + 1) * GROUP_W)
            s = nxt_ref[0, 0:t_rows, cols]
            for i in range(1, win):
                s = s + nxt_ref[0, pl.ds(i, t_rows), cols]
            dpx.append(s - d_pooled[j])
        put_dp(P_X, jnp.concatenate(dpx, axis=1))

        c_a = cur(C_A)
        sig_b = _sigmoid(cur(C_B))
        put_ext(1, hal(C_A) * _sigmoid(hal(C_B)), c_a * sig_b)
        cw = _lanes(cw_ref)
        xh, rstd, ln = _layer_norm_parts(cacc_ref[...], clg_ref[...], clb_ref[...])
        cgate = cur(C_GATE)
        dz_conv = dzp(1)
        ln_silu, ln_dsilu = _silu_pair(ln)
        cg_silu, cg_dsilu = _silu_pair(cgate)
        put_dp(C_GATE, dz_conv * ln_silu * cg_dsilu)
        d_ln = dz_conv * cg_silu * ln_dsilu
        g_clg[...] += _rowsum(d_ln * xh)
        g_clb[...] += _rowsum(d_ln)
        dxh = d_ln * clg_ref[...]
        dc = rstd * (dxh - _lanemean(dxh) - xh * _lanemean(dxh * xh))
        g_cb[...] += _rowsum(dc)
        nxt_ref[1, 0:t_rows, :] = dc
        _tap_grads(cw_acc, pad_ref, shift_ref, dc, ext_ref, 1, CONV_K, HALO - (CONV_K - 1), t_rows)
        dyg = _window_sum(nxt_ref, 1, lambda i: cw[CONV_K - 1 - i:CONV_K - i, :], CONV_K, 0, t_rows)
        put_dp(C_A, dyg * sig_b)
        put_dp(C_B, dyg * c_a * sig_b * (1.0 - sig_b))

        u = cur(G_U)
        ggate = cur(G_GATE)
        vxh, vrstd, v = _layer_norm_parts(cur(G_V), slg_ref[...], slb_ref[...])
        vb = v.astype(BF16)
        mask = _tril_mask()
        wt = [jnp.where(mask, sw_ref[g], 0.0) for g in range(N_GROUPS)]
        wt_b = [w.astype(BF16) for w in wt]
        wtt_b = [w.T.astype(BF16) for w in wt]
        dz_sgu = dzp(2)
        gg_silu, gg_dsilu = _silu_pair(ggate)
        d_sgu = dz_sgu * gg_silu
        d_sp = d_sgu * u
        d_spb = d_sp.astype(BF16)
        sp_rows, dv_rows = [], []
        for sub in range(t_rows // CHUNK):
            rows = slice(sub * CHUNK, (sub + 1) * CHUNK)
            sp_g, dv_g = [], []
            for g in range(N_GROUPS):
                cols = slice(g * GROUP_W, (g + 1) * GROUP_W)
                sp_g.append(_dot(wt_b[g], vb[rows, cols]))
                g_sw[g] += jnp.where(mask, _dot_nt(d_spb[rows, cols], vb[rows, cols]), 0.0)
                dv_g.append(_dot(wtt_b[g], d_spb[rows, cols]))
            sp_rows.append(jnp.concatenate(sp_g, axis=1) + sbias_ref[...])
            dv_rows.append(jnp.concatenate(dv_g, axis=1))
            sb_acc[...] += d_sp[rows, :]
        sp = jnp.concatenate(sp_rows, axis=0)
        dv = jnp.concatenate(dv_rows, axis=0)
        put_dp(G_GATE, dz_sgu * (u * sp) * gg_dsilu)
        put_dp(G_U, d_sgu * sp)
        g_slg[...] += _rowsum(dv * vxh)
        g_slb[...] += _rowsum(dv)
        dvx = dv * slg_ref[...]
        put_dp(G_V, vrstd * (dvx - _lanemean(dvx) - vxh * _lanemean(dvx * vxh)))

        s_b, s_c, s_x, sgate = cur(S_B), cur(S_C), cur(S_X), cur(S_GATE)
        put_ext(2, hal(S_C) * hal(S_X), s_c * s_x)
        scw = _lanes(scw_ref)
        cv = _window_sum(ext_ref, 2, lambda k: scw[k:k + 1, :], SHORT_K, HALO - (SHORT_K - 1), t_rows)
        dz_sc = dzp(3)
        sg_silu, sg_dsilu = _silu_pair(sgate)
        put_dp(S_GATE, dz_sc * (s_b * cv) * sg_dsilu)
        d_pre = dz_sc * sg_silu
        put_dp(S_B, d_pre * cv)
        dcv = d_pre * s_b
        nxt_ref[2, 0:t_rows, :] = dcv
        _tap_grads(scw_acc, pad_ref, shift_ref, dcv, ext_ref, 2, SHORT_K, HALO - (SHORT_K - 1), t_rows)
        du = _window_sum(nxt_ref, 2, lambda i: scw[SHORT_K - 1 - i:SHORT_K - i, :], SHORT_K, 0, t_rows)
        put_dp(S_C, du * s_x)
        put_dp(S_X, du * s_c)

        nxt_ref[:, t_rows:t_rows + HALO, :] = nxt_ref[:, 0:HALO, :]

        @pl.when(jnp.logical_and(b == nb - 1, r == nc - 1))
        def _():
            lane = lax.broadcasted_iota(jnp.int32, (CHUNK, GROUP_W), 1)
            out = jnp.zeros((CHUNK, GROUP_W), F32)
            for g in range(N_GROUPS):
                col = jnp.sum(sb_acc[:, g * GROUP_W:(g + 1) * GROUP_W], axis=1, keepdims=True)
                out = jnp.where(lane == g, col, out)
            g_sb[...] = out
            g_cw[...] = jnp.sum(cw_acc[...], axis=1)
            g_scw[...] = jnp.sum(scw_acc[...], axis=1)

    def acc_spec(shape):
        nd = len(shape)
        return pl.BlockSpec(shape, lambda b, r: (0,) * nd)

    acc_shapes = [
        (N_GROUPS, GROUP_W, GROUP_W),
        (1, BRANCH_W),
        (CONV_ROWS, BRANCH_W),
        (1, BRANCH_W),
        (1, BRANCH_W),
        (1, BRANCH_W),
        (1, BRANCH_W),
        (1, BRANCH_W),
        (N_GROUPS, CHUNK, CHUNK),
        (CHUNK, GROUP_W),
        (SHORT_ROWS, BRANCH_W),
    ]
    outs, rode = _host_call(
        body,
        name=f"mixers_bwd_l{layer}",
        grid=(nb, nc),
        in_specs=[
            pl.BlockSpec((None, t_rows, PIECE_COLS), lambda b, r: (b, nc - 1 - r, 0)),
            pl.BlockSpec((None, HALO, PIECE_COLS), lambda b, r: (b, jnp.maximum((nc - 1 - r) * hb - 1, 0), 0)),
            pl.BlockSpec((None, t_rows, N_BRANCH * BRANCH_W), lambda b, r: (b, nc - 1 - r, 0)),
            pl.BlockSpec((None, t_rows, BRANCH_W), lambda b, r: (b, nc - 1 - r, 0)),
        ] + _mixer_weight_specs() + [ANY],
        out_specs=[pl.BlockSpec((None, t_rows, PIECE_COLS), lambda b, r: (b, nc - 1 - r, 0))]
        + [acc_spec(s) for s in acc_shapes],
        out_shape=[jax.ShapeDtypeStruct(dproj3.shape, BF16)] + [jax.ShapeDtypeStruct(s, F32) for s in acc_shapes],
        scratch_shapes=[
            pltpu.VMEM((3, HALO + t_rows + 8, BRANCH_W), F32),
            pltpu.VMEM((3, t_rows + HALO, BRANCH_W), F32),
            pltpu.VMEM((CHUNK, BRANCH_W), F32),
            pltpu.VMEM((CONV_ROWS, 8, BRANCH_W), F32),
            pltpu.VMEM((SHORT_ROWS, 8, BRANCH_W), F32),
            pltpu.VMEM((t_rows + 16, BRANCH_W), F32),
            pltpu.VMEM((t_rows + 8, BRANCH_W), F32),
        ],
        aliases={4 + 11: 0},
        args=[proj3, proj3, dz3, cacc3, *_mixer_weight_args(mw), dproj3],
        ride=ride,
    )
    return outs[0], dict(zip(MIXER_GRADS, outs[1:])), rode


def _merge_fwd(z2, proj2, x2, wbr_l, wo_l, layer, ride=None):
    n = x2.shape[0]
    tm = _tile(n, 1024)

    def body(z_ref, gate_ref, x_ref, wbr_ref, wo_ref, xn_ref, mg_ref, acc_ref):
        nbr = pl.program_id(1)
        bo = _dot(z_ref[...], wbr_ref[nbr])
        contrib = _sigmoid(gate_ref[...].astype(F32)) * bo

        @pl.when(nbr == 0)
        def _():
            acc_ref[...] = contrib

        @pl.when(nbr > 0)
        def _():
            acc_ref[...] += contrib

        @pl.when(nbr == N_BRANCH - 1)
        def _():
            mg = acc_ref[...].astype(BF16)
            mg_ref[...] = mg
            xn_ref[...] = x_ref[...] + _dot(mg, wo_ref[...])

    return _host_call(
        body,
        name=f"merge_fwd_l{layer}",
        grid=(n // tm, N_BRANCH),
        in_specs=[
            pl.BlockSpec((tm, BRANCH_W), lambda i, b: (i, b)),
            pl.BlockSpec((tm, D_MODEL), lambda i, b: (i, GATE_BLOCK0 + b)),
            pl.BlockSpec((tm, D_MODEL), lambda i, b: (i, 0)),
            pl.BlockSpec((N_BRANCH, BRANCH_W, D_MODEL), lambda i, b: (0, 0, 0)),
            pl.BlockSpec((D_MODEL, D_MODEL), lambda i, b: (0, 0)),
        ],
        out_specs=[
            pl.BlockSpec((tm, D_MODEL), lambda i, b: (i, 0)),
            pl.BlockSpec((tm, D_MODEL), lambda i, b: (i, 0)),
        ],
        out_shape=[jax.ShapeDtypeStruct((n, D_MODEL), F32), jax.ShapeDtypeStruct((n, D_MODEL), BF16)],
        scratch_shapes=[pltpu.VMEM((tm, D_MODEL), F32)],
        args=[z2, proj2, x2, wbr_l, wo_l],
        ride=ride,
    )


def _merge_bwd(dout2, merged2, z2, proj2, wbr_l, wo_l, layer, ride=None):
    n = dout2.shape[0]
    tm = _tile(n, 1024)
    nt = n // tm

    def body(do_ref, mg_ref, z_ref, gate_ref, wbr_ref, wo_ref, dz_ref, dg_ref, gwo_ref, gwbr_ref, dm_ref,
             awo_ref, awbr_ref):
        i = pl.program_id(0)
        nbr = pl.program_id(1)

        @pl.when(nbr == 0)
        def _():
            do_b = do_ref[...].astype(BF16)
            dm_ref[...] = _dot_nt(do_b, wo_ref[...])
            gw = _dot_tn(mg_ref[...], do_b)

            @pl.when(i == 0)
            def _():
                awo_ref[...] = gw

            @pl.when(i > 0)
            def _():
                awo_ref[...] += gw

        zt = z_ref[...]
        wbr = wbr_ref[nbr]
        bo = _dot(zt, wbr)
        gt = _sigmoid(gate_ref[...].astype(F32))
        dm = dm_ref[...]
        dbo = (dm * gt).astype(BF16)
        dg_ref[...] = (dm * bo * gt * (1.0 - gt)).astype(BF16)
        dz_ref[...] = _dot_nt(dbo, wbr).astype(BF16)
        gw = _dot_tn(zt, dbo)

        @pl.when(i == 0)
        def _():
            awbr_ref[nbr] = gw

        @pl.when(i > 0)
        def _():
            awbr_ref[nbr] += gw

        @pl.when(jnp.logical_and(i == nt - 1, nbr == N_BRANCH - 1))
        def _():
            pltpu.sync_copy(awo_ref, gwo_ref)
            pltpu.sync_copy(awbr_ref, gwbr_ref)

    return _host_call(
        body,
        name=f"merge_bwd_l{layer}",
        grid=(nt, N_BRANCH),
        in_specs=[
            pl.BlockSpec((tm, D_MODEL), lambda i, b: (i, 0)),
            pl.BlockSpec((tm, D_MODEL), lambda i, b: (i, 0)),
            pl.BlockSpec((tm, BRANCH_W), lambda i, b: (i, b)),
            pl.BlockSpec((tm, D_MODEL), lambda i, b: (i, GATE_BLOCK0 + b)),
            pl.BlockSpec((N_BRANCH, BRANCH_W, D_MODEL), lambda i, b: (0, 0, 0)),
            pl.BlockSpec((D_MODEL, D_MODEL), lambda i, b: (0, 0)),
        ],
        out_specs=[
            pl.BlockSpec((tm, BRANCH_W), lambda i, b: (i, b)),
            pl.BlockSpec((tm, D_MODEL), lambda i, b: (i, GATE_BLOCK0 + b)),
            ANY,
            ANY,
        ],
        out_shape=[
            jax.ShapeDtypeStruct((n, N_BRANCH * BRANCH_W), BF16),
            jax.ShapeDtypeStruct((n, IN_COLS), BF16),
            jax.ShapeDtypeStruct((D_MODEL, D_MODEL), F32),
            jax.ShapeDtypeStruct((N_BRANCH, BRANCH_W, D_MODEL), F32),
        ],
        scratch_shapes=[
            pltpu.VMEM((tm, D_MODEL), F32),
            pltpu.VMEM((D_MODEL, D_MODEL), F32),
            pltpu.VMEM((N_BRANCH, BRANCH_W, D_MODEL), F32),
        ],
        args=[dout2, merged2, z2, proj2, wbr_l, wo_l],
        ride=ride,
    )


def _loss_head(x2, g_row, tgt2):
    n = x2.shape[0]
    tm = _tile(n, 512)

    def body(x_ref, g_ref, t_ref, dx_ref, loss_ref, dg_ref):
        @pl.when(pl.program_id(0) == 0)
        def _():
            loss_ref[...] = jnp.zeros_like(loss_ref)
            dg_ref[...] = jnp.zeros_like(dg_ref)

        xv = x_ref[...]
        g = g_ref[...]
        r = lax.rsqrt(_lanemean(xv * xv) + RMS_EPS)
        xh = xv * r
        err = xh * g - t_ref[...]
        loss_ref[...] += 0.5 * jnp.sum(_lanemean(err * err), axis=0, keepdims=True)
        dy = err * (1.0 / D_MODEL)
        dg_ref[...] += _rowsum(dy * xh)
        dxh = dy * g
        dx_ref[...] = r * (dxh - xh * _lanemean(dxh * xh))

    return pl.pallas_call(
        body,
        name="loss_head",
        grid=(n // tm,),
        in_specs=[
            pl.BlockSpec((tm, D_MODEL), lambda i: (i, 0)),
            pl.BlockSpec((1, D_MODEL), lambda i: (0, 0)),
            pl.BlockSpec((tm, D_MODEL), lambda i: (i, 0)),
        ],
        out_specs=[
            pl.BlockSpec((tm, D_MODEL), lambda i: (i, 0)),
            pl.BlockSpec((1, GROUP_W), lambda i: (0, 0)),
            pl.BlockSpec((1, D_MODEL), lambda i: (0, 0)),
        ],
        out_shape=[jax.ShapeDtypeStruct((n, D_MODEL), F32), jax.ShapeDtypeStruct((1, GROUP_W), F32),
                   jax.ShapeDtypeStruct((1, D_MODEL), F32)],
        compiler_params=_params("arbitrary"),
    )(x2, g_row, tgt2)


def _inproj_bwd_x(dproj2, w_l, x2, g_row, dout2, layer, ride=None):
    n = x2.shape[0]
    tm = _tile(n, 1024)

    def body(dp_ref, w_ref, x_ref, g_ref, do_ref, dx_ref, dng_ref, dh_ref):
        i = pl.program_id(0)
        s = pl.program_id(1)
        part = _dot_nt(dp_ref[...], w_ref[...])

        @pl.when(s == 0)
        def _():
            dh_ref[...] = part

        @pl.when(s > 0)
        def _():
            dh_ref[...] += part

        @pl.when(jnp.logical_and(i == 0, s == 0))
        def _():
            dng_ref[...] = jnp.zeros_like(dng_ref)

        @pl.when(s == N_CHIPS - 1)
        def _():
            xv = x_ref[...]
            r = lax.rsqrt(_lanemean(xv * xv) + RMS_EPS)
            xh = xv * r
            dh = dh_ref[...]
            dng_ref[...] += _rowsum(dh * xh)
            dxh = dh * g_ref[...]
            dx_ref[...] = do_ref[...] + r * (dxh - xh * _lanemean(dxh * xh))

    return _host_call(
        body,
        name=f"inproj_bwd_x_l{layer}",
        grid=(n // tm, N_CHIPS),
        in_specs=[
            pl.BlockSpec((tm, SHARD_COLS), lambda i, s: (i, s)),
            pl.BlockSpec((None, D_MODEL, SHARD_COLS), lambda i, s: (s, 0, 0)),
            pl.BlockSpec((tm, D_MODEL), lambda i, s: (i, 0)),
            pl.BlockSpec((1, D_MODEL), lambda i, s: (0, 0)),
            pl.BlockSpec((tm, D_MODEL), lambda i, s: (i, 0)),
        ],
        out_specs=[
            pl.BlockSpec((tm, D_MODEL), lambda i, s: (i, 0)),
            pl.BlockSpec((1, D_MODEL), lambda i, s: (0, 0)),
        ],
        out_shape=[jax.ShapeDtypeStruct((n, D_MODEL), F32), jax.ShapeDtypeStruct((1, D_MODEL), F32)],
        scratch_shapes=[pltpu.VMEM((tm, D_MODEL), F32)],
        args=[dproj2, w_l, x2, g_row, dout2],
        ride=ride,
    )


def _inproj_bwd_w(h2, dproj2, layer, ride=None):
    n = h2.shape[0]
    tm = _tile(n, 2048)

    def body(h_ref, dp_ref, gw_ref):
        gw = _dot_tn(h_ref[...], dp_ref[...])

        @pl.when(pl.program_id(1) == 0)
        def _():
            gw_ref[...] = gw

        @pl.when(pl.program_id(1) > 0)
        def _():
            gw_ref[...] += gw

    outs, rode = _host_call(
        body,
        name=f"inproj_bwd_w_l{layer}",
        grid=(N_COL_TILES, n // tm),
        in_specs=[
            pl.BlockSpec((tm, D_MODEL), lambda s, i: (i, 0)),
            pl.BlockSpec((tm, COL_TILE), lambda s, i: (i, s)),
        ],
        out_specs=[pl.BlockSpec((None, D_MODEL, COL_TILE), lambda s, i: (s // 2, 0, s % 2))],
        out_shape=[jax.ShapeDtypeStruct((N_CHIPS, D_MODEL, SHARD_COLS), F32)],
        args=[h2, dproj2],
        ride=ride,
    )
    return outs[0], rode


def _adamw_math(w, g, m, v):
    m = ADAM_B1 * m + (1.0 - ADAM_B1) * g
    v = ADAM_B2 * v + (1.0 - ADAM_B2) * jnp.square(g)
    m_hat = m / (1.0 - ADAM_B1 ** ADAM_STEP)
    v_hat = v / (1.0 - ADAM_B2 ** ADAM_STEP)
    delta = -ADAM_LR * (m_hat / (jnp.sqrt(v_hat) + ADAM_EPS) + ADAM_WD * w)
    return delta, m, v


def _adamw_sharded(w, m, v, part, layer, prev, name, ride=None):
    _, rows, cols = w.shape
    tr = _tile(rows, max(16, (1 << 21) // (4 * cols) // 16 * 16), mult=16)

    def body(w_ref, m_ref, v_ref, p_ref, *rest):
        g_out, d_out, m_out, v_out = rest[-4:]
        g = ((p_ref[0].astype(F32) + p_ref[1].astype(F32)) + p_ref[2].astype(F32)) + p_ref[3].astype(F32)
        delta, m2, v2 = _adamw_math(w_ref[...], g, m_ref[...], v_ref[...])
        g_out[...] = g
        d_out[...] = delta
        m_out[...] = m2
        v_out[...] = v2

    spec = pl.BlockSpec((None, tr, cols), lambda i: (layer, i, 0))
    return _host_call(
        body,
        name=f"adamw_{name}_l{layer}",
        grid=(rows // tr,),
        in_specs=[spec, spec, spec, pl.BlockSpec((N_CHIPS, tr, cols), lambda i: (0, i, 0))]
        + ([ANY] * 4 if prev else []),
        out_specs=[spec] * 4,
        out_shape=[jax.ShapeDtypeStruct(w.shape, F32)] * 4,
        args=[w, m, v, part] + (list(prev) if prev else []),
        aliases={4 + k: k for k in range(4)} if prev else {},
        ride=ride,
    )


def _adamw_packed(w, m, v, g):
    rows = w.shape[0]
    tr = _tile(rows, rows // 2 if rows % 16 == 0 else rows)

    def body(w_ref, m_ref, v_ref, g_ref, d_out, m_out, v_out):
        delta, m2, v2 = _adamw_math(w_ref[...], g_ref[...], m_ref[...], v_ref[...])
        d_out[...] = delta
        m_out[...] = m2
        v_out[...] = v2

    spec = pl.BlockSpec((tr, GROUP_W), lambda i: (i, 0))
    return pl.pallas_call(
        body,
        name="adamw_small",
        grid=(rows // tr,),
        in_specs=[spec] * 4,
        out_specs=[spec] * 3,
        out_shape=[jax.ShapeDtypeStruct(w.shape, F32)] * 3,
        compiler_params=_params("arbitrary"),
    )(w, m, v, g)


SMALL = ["norm_g", "pool_w", "pool_scale", "conv_b", "conv_ln_g", "conv_ln_b", "sgu_ln_g", "sgu_ln_b", "sgu_w",
         "sgu_b", "final_g"]
WEIGHTS = ["norm_g", "w_in", "pool_w", "pool_scale", "conv_w", "conv_b", "conv_ln_g", "conv_ln_b", "sgu_ln_g",
           "sgu_ln_b", "sgu_w", "sgu_b", "sc_w", "w_branch", "w_o", "final_g"]
HALF_SHAPES = [(D_MODEL // 2, SHARD_COLS), (N_BRANCH * BRANCH_W // 2, BR_SHARD), (BR_SHARD // 2, D_MODEL),
               (CS_ROWS // 2, GROUP_W)]


def _pack_small(tree, last_rows=None):
    tail = jnp.zeros((8, GROUP_W), F32) if last_rows is None else last_rows
    return jnp.concatenate([tree[k].reshape(-1, GROUP_W) for k in SMALL] + [tail], axis=0)


def _unpack_small(packed, like):
    out, r = {}, 0
    for k in SMALL:
        nr = like[k].size // GROUP_W
        out[k] = packed[r:r + nr].reshape(like[k].shape)
        r += nr
    return out


def _pad_rows(a, rows):
    pad = [(0, 0)] * a.ndim
    pad[-2] = (0, rows - a.shape[-2])
    return jnp.pad(a, pad)


def _pack_cs(conv, short):
    return jnp.concatenate([_pad_rows(conv, CS_ROWS // 2), _pad_rows(short, CS_ROWS // 2)], axis=-2)


def _shard_major_rows(a):
    return a.reshape(a.shape[0], N_CHIPS, GROUP_W).transpose(1, 0, 2)


def kernel(x, norm_g, w_in, pool_w, pool_scale, conv_w, conv_b, conv_ln_g, conv_ln_b, sgu_ln_g, sgu_ln_b, sgu_w, sgu_b, sc_w, w_branch, w_o, final_g, loss_target, m_norm_g, m_w_in, m_pool_w, m_pool_scale, m_conv_w, m_conv_b, m_conv_ln_g, m_conv_ln_b, m_sgu_ln_g, m_sgu_ln_b, m_sgu_w, m_sgu_b, m_sc_w, m_w_branch, m_w_o, m_final_g, v_norm_g, v_w_in, v_pool_w, v_pool_scale, v_conv_w, v_conv_b, v_conv_ln_g, v_conv_ln_b, v_sgu_ln_g, v_sgu_ln_b, v_sgu_w, v_sgu_b, v_sc_w, v_w_branch, v_w_o, v_final_g):
    w = dict(norm_g=norm_g, w_in=w_in, pool_w=pool_w, pool_scale=pool_scale, conv_w=conv_w, conv_b=conv_b,
             conv_ln_g=conv_ln_g, conv_ln_b=conv_ln_b, sgu_ln_g=sgu_ln_g, sgu_ln_b=sgu_ln_b, sgu_w=sgu_w,
             sgu_b=sgu_b, sc_w=sc_w, w_branch=w_branch, w_o=w_o, final_g=final_g)
    mom = dict(norm_g=m_norm_g, w_in=m_w_in, pool_w=m_pool_w, pool_scale=m_pool_scale, conv_w=m_conv_w,
               conv_b=m_conv_b, conv_ln_g=m_conv_ln_g, conv_ln_b=m_conv_ln_b, sgu_ln_g=m_sgu_ln_g,
               sgu_ln_b=m_sgu_ln_b, sgu_w=m_sgu_w, sgu_b=m_sgu_b, sc_w=m_sc_w, w_branch=m_w_branch, w_o=m_w_o,
               final_g=m_final_g)
    var = dict(norm_g=v_norm_g, w_in=v_w_in, pool_w=v_pool_w, pool_scale=v_pool_scale, conv_w=v_conv_w,
               conv_b=v_conv_b, conv_ln_g=v_conv_ln_g, conv_ln_b=v_conv_ln_b, sgu_ln_g=v_sgu_ln_g,
               sgu_ln_b=v_sgu_ln_b, sgu_w=v_sgu_w, sgu_b=v_sgu_b, sc_w=v_sc_w, w_branch=v_w_branch, w_o=v_w_o,
               final_g=v_final_g)

    nb, seq, _ = x.shape
    n = nb * seq
    core = lax.axis_index("c").astype(jnp.int32).reshape(1)

    win_s = w_in.astype(BF16).reshape(DEPTH, 2, D_MODEL // 2, SHARD_COLS)
    wbr_s = w_branch.astype(BF16)
    wo_s = w_o.astype(BF16)
    cs_s = _pack_cs(conv_w, sc_w)

    (win_f0,) = _allgather_layer0([win_s])

    def layer_weights(win_f, wbr_f, wo_f, cs_f):
        cs_f = cs_f.reshape(N_CHIPS, CS_ROWS, GROUP_W)
        return dict(win=win_f.reshape(N_CHIPS, D_MODEL, SHARD_COLS),
                    wbr=wbr_f.reshape(N_CHIPS, N_BRANCH, BRANCH_W, BR_SHARD).transpose(1, 2, 0, 3).reshape(
                        N_BRANCH, BRANCH_W, D_MODEL),
                    wo=wo_f.reshape(D_MODEL, D_MODEL),
                    conv_w=cs_f[:, :CONV_ROWS], sc_w=cs_f[:, CS_ROWS // 2:CS_ROWS // 2 + SHORT_ROWS])

    def mixer_weights(l, gathered):
        row = lambda a: a[l].reshape(1, BRANCH_W)
        bias = jnp.repeat(jnp.swapaxes(sgu_b[l], 0, 1), GROUP_W, axis=1)
        return dict(pool_w=pool_w[l].astype(BF16), pool_scale=row(pool_scale), conv_w=gathered["conv_w"],
                    conv_b=row(conv_b), conv_ln_g=row(conv_ln_g), conv_ln_b=row(conv_ln_b), sgu_ln_g=row(sgu_ln_g),
                    sgu_ln_b=row(sgu_ln_b), sgu_w=sgu_w[l], sgu_bias=bias, sc_w=gathered["sc_w"])

    lw, mw = [None, None], [None, None]

    xs, projs, hs, zs, mgs, caccs = [x.reshape(n, D_MODEL)], [], [], [], [], []
    for l in range(DEPTH):
        first = l == 0
        win_l = win_f0.reshape(N_CHIPS, D_MODEL, SHARD_COLS) if first else lw[1]["win"]
        ride = _join(_gather_whole([wbr_s, wo_s, cs_s], 0), _gather_layer1_first_half(win_s)) if first else None
        (proj, h), rode = _inproj_fwd(xs[l], norm_g[l:l + 1], win_l, l, ride=ride)
        if first:
            lw[0] = layer_weights(win_f0, *rode[:3])
            mw[0] = mixer_weights(0, lw[0])
            win_f1 = rode[3]
        (z3, cacc), rode = _mixers_fwd(proj.reshape(nb, seq, IN_COLS), mw[l], l,
                                       ride=_gather_layer1_second_half(win_s, win_f1) if first else None)
        caccs.append(cacc)
        if first:
            win_f1 = rode[0]
        z = z3.reshape(n, N_BRANCH * BRANCH_W)
        (x_new, merged), rode = _merge_fwd(z, proj, xs[l], lw[l]["wbr"], lw[l]["wo"], l,
                                           ride=_gather_whole([wbr_s, wo_s, cs_s], 1) if first else None)
        if first:
            lw[1] = layer_weights(win_f1, *rode)
            mw[1] = mixer_weights(1, lw[1])
        projs.append(proj)
        hs.append(h)
        zs.append(z)
        mgs.append(merged)
        xs.append(x_new)

    dx, loss_part, g_final = _loss_head(xs[DEPTH], final_g.reshape(1, D_MODEL), loss_target.reshape(n, D_MODEL))

    def chip_major(gwbr):
        return gwbr.reshape(N_BRANCH, BRANCH_W, N_CHIPS, BR_SHARD).transpose(2, 0, 1, 3)

    def as_halves(arrays, first_index=0):
        return [g.reshape((N_CHIPS, 2) + s) for g, s in zip(arrays, HALF_SHAPES[first_index:])]

    def add_halves(grads, others, l, first_index=0):
        return [_add_halves(g, o, core, f"add_halves_l{l}_{first_index + i}")
                for i, (g, o) in enumerate(zip(grads, others))]

    per_layer, parts = {}, [None] * DEPTH
    (dz, dproj, gwo, gwbr), _ = _merge_bwd(dx, mgs[1], zs[1], projs[1], lw[1]["wbr"], lw[1]["wo"], 1)
    dproj3, gm, _ = _mixers_bwd(projs[1].reshape(nb, seq, IN_COLS), dz.reshape(nb, seq, N_BRANCH * BRANCH_W), caccs[1],
                                dproj.reshape(nb, seq, IN_COLS), mw[1], 1)
    dproj = dproj3.reshape(n, IN_COLS)
    gwin, _ = _inproj_bwd_w(hs[1], dproj, 1)
    gcs = _pack_cs(_shard_major_rows(gm["conv_w"]), _shard_major_rows(gm["sc_w"]))
    grads1 = as_halves([gwin, chip_major(gwbr), gwo, gcs])
    (dx, g_norm), others1 = _inproj_bwd_x(dproj, lw[1]["win"], xs[1], norm_g[1:2], dx, 1, ride=_swap_halves(grads1))
    gm["norm_g"] = g_norm
    per_layer[1] = gm
    sums1 = add_halves(grads1, others1, 1)
    (dz, dproj, gwo, gwbr), scattered1 = _merge_bwd(dx, mgs[0], zs[0], projs[0], lw[0]["wbr"], lw[0]["wo"], 0,
                                                    ride=_scatter_chip_sums(sums1))
    dproj3, gm, parts[1] = _mixers_bwd(projs[0].reshape(nb, seq, IN_COLS), dz.reshape(nb, seq, N_BRANCH * BRANCH_W),
                                       caccs[0], dproj.reshape(nb, seq, IN_COLS), mw[0], 0,
                                       ride=_share_halves(scattered1))
    dproj = dproj3.reshape(n, IN_COLS)
    gcs = _pack_cs(_shard_major_rows(gm["conv_w"]), _shard_major_rows(gm["sc_w"]))
    small0 = as_halves([chip_major(gwbr), gwo, gcs], first_index=1)

    per_layer[0] = dict(gm, norm_g=jnp.zeros((1, D_MODEL), F32))

    def stack(k, shape):
        return jnp.stack([per_layer[l][k] for l in range(DEPTH)]).reshape(shape)

    g_small = {"final_g": g_final.reshape(D_MODEL), "norm_g": stack("norm_g", (DEPTH, D_MODEL)),
               "pool_w": stack("pool_w", pool_w.shape), "sgu_w": stack("sgu_w", sgu_w.shape),
               "sgu_b": jnp.swapaxes(stack("sgu_b", (DEPTH, CHUNK, GROUP_W))[:, :, :N_GROUPS], 1, 2)}
    for k in ("pool_scale", "conv_b", "conv_ln_g", "conv_ln_b", "sgu_ln_g", "sgu_ln_b"):
        g_small[k] = stack(k, (DEPTH, BRANCH_W))
    small_part = _pack_small(g_small, jnp.broadcast_to(loss_part, (8, GROUP_W)))

    gwin, rode = _inproj_bwd_w(hs[0], dproj, 0, ride=_join(_swap_halves(small0), _swap_whole(small_part)))
    others_small0, small_other = rode[:3], rode[3]
    small_chip = _add2(small_part, small_other, "add_small_sibling")
    (gwin_h,) = as_halves([gwin])

    def flat(a, i):
        rows, cols = 2 * HALF_SHAPES[i][0], HALF_SHAPES[i][1]
        return a.reshape(a.shape[0], rows, cols)

    names4 = ["w_in", "w_branch", "w_o", "conv_sc"]

    def packed4(t):
        return [t["w_in"], t["w_branch"], t["w_o"], _pack_cs(t["conv_w"], t["sc_w"])]

    w4, m4, v4 = ([flat(a, i) for i, a in enumerate(packed4(t))] for t in (w, mom, var))

    def sharded_update(i, layer, prev, ride=None):
        return _adamw_sharded(w4[i], m4[i], v4[i], flat(parts[layer][i], i), layer, prev, names4[i], ride=ride)

    updated = [sharded_update(i, 1, None)[0] for i in range(4)]

    (other_win0,) = _exchange_call("swap_halves_l0_w_in", _swap_halves([gwin_h]))
    sums0 = add_halves([gwin_h], [other_win0], 0) + add_halves(small0, others_small0, 0, first_index=1)
    (dx, g_norm0), rode = _inproj_bwd_x(dproj, lw[0]["win"], xs[0], norm_g[0:1], dx, 0,
                                        ride=_join(_scatter_chip_sums(sums0), _spread_chip_sums(small_chip)))
    scattered0, small_chips = rode[:4], rode[4]
    grad_x = dx.reshape(nb, seq, D_MODEL)

    g_norm0, parts[0] = _allreduce_small(g_norm0.reshape(D_MODEL // GROUP_W, GROUP_W), _share_halves(scattered0))
    g_packed = lax.dynamic_update_slice(_sum_chips(small_chips, "sum_small_chips"), g_norm0, (0, 0))
    loss = g_packed[g_packed.shape[0] - 8, 0]
    d_packed, m_packed, v_packed = _adamw_packed(_pack_small(w), _pack_small(mom), _pack_small(var), g_packed)
    grads, deltas, new_m, new_v = {}, {}, {}, {}
    for tree, packed in ((grads, g_packed), (deltas, d_packed), (new_m, m_packed), (new_v, v_packed)):
        tree.update(_unpack_small(packed, w))

    for i, name in enumerate(names4):
        res, _ = sharded_update(i, 0, updated[i])
        for tree, r in zip((grads, deltas, new_m, new_v), res):
            if name == "conv_sc":
                tree["conv_w"] = r[:, :CONV_K]
                tree["sc_w"] = r[:, CS_ROWS // 2:CS_ROWS // 2 + SHORT_K]
            else:
                tree[name] = r.reshape(w[name].shape)

    return (loss, grad_x, *[grads[k] for k in WEIGHTS], *[deltas[k] for k in WEIGHTS],
            *[new_m[k] for k in WEIGHTS], *[new_v[k] for k in WEIGHTS])
```

```python
import functools

import jax
import jax.numpy as jnp
from jax import lax
from jax.experimental import pallas as pl
from jax.experimental.pallas import tpu as pltpu

F32 = jnp.float32
BF16 = jnp.bfloat16

D_MODEL = 1024
DEPTH = 2
N_BRANCH = 4
BRANCH_W = 512
N_GROUPS = 4
GROUP_W = 128
POOL_WINDOWS = (2, 4, 8, 16)
CONV_K = 31
SHORT_K = 3
CHUNK = 128
N_PIECES = 12
PIECE_COLS = N_PIECES * BRANCH_W
IN_COLS = PIECE_COLS + N_BRANCH * D_MODEL
N_CHIPS = 4
SHARD_COLS = IN_COLS // N_CHIPS
BR_SHARD = D_MODEL // N_CHIPS
COL_TILE = SHARD_COLS // 2
N_COL_TILES = IN_COLS // COL_TILE
GATE_BLOCK0 = PIECE_COLS // D_MODEL
RMS_EPS = 1e-6
LN_EPS = 1e-5
HALO = 32
CONV_ROWS = 32
SHORT_ROWS = 8
CS_ROWS = 64

ADAM_LR = 0.001
ADAM_B1 = 0.9
ADAM_B2 = 0.999
ADAM_EPS = 1e-08
ADAM_WD = 0.01
ADAM_STEP = 10

VMEM_LIMIT = 60 * 1024 * 1024
MESH_ID = pl.DeviceIdType.MESH
ANY = pl.BlockSpec(memory_space=pl.ANY)
VMEM_WHOLE = pl.BlockSpec(memory_space=pltpu.VMEM)

(P_X, P_GATE, C_A, C_B, C_GATE, G_U, G_V, G_GATE, S_B, S_C, S_X, S_GATE) = range(N_PIECES)


def _params(*sem):
    return pltpu.CompilerParams(dimension_semantics=sem, vmem_limit_bytes=VMEM_LIMIT)


def _sigmoid(v):
    return 0.5 * jnp.tanh(0.5 * v) + 0.5


def _silu(v):
    return v * _sigmoid(v)


def _silu_pair(v):
    s = _sigmoid(v)
    return v * s, s * (1.0 + v * (1.0 - s))


def _dot(a, b):
    return jnp.dot(a, b, preferred_element_type=F32)


def _dot_nt(a, b):
    return lax.dot_general(a, b, (((1,), (1,)), ((), ())), preferred_element_type=F32)


def _dot_tn(a, b):
    return lax.dot_general(a, b, (((0,), (0,)), ((), ())), preferred_element_type=F32)


def _rowsum(v):
    return jnp.sum(v, axis=0, keepdims=True)


def _lanemean(v):
    return jnp.mean(v, axis=-1, keepdims=True)


def _tile(n, want, mult=8):
    t = max(1, min(n, want))
    while n % t or (t % mult and t != n):
        t -= 1
    return t


def _place():
    x, y, c = lax.axis_index("x"), lax.axis_index("y"), lax.axis_index("c")
    chip = 2 * x + y
    peers = [(1 - x, y), (x, 1 - y), (1 - x, 1 - y)]
    return x, y, c, chip, peers


def _remote(src, dst, send_sem, recv_sem, dev):
    return pltpu.make_async_remote_copy(src_ref=src, dst_ref=dst, send_sem=send_sem, recv_sem=recv_sem,
                                        device_id=dev, device_id_type=MESH_ID)


class _Exchange:
    def __init__(self, inputs, out_shapes, plan, n_remote, n_local=0, aliases=None):
        self.inputs = list(inputs)
        self.out_shapes = list(out_shapes)
        self.plan = plan
        self.n_remote = n_remote
        self.n_local = n_local
        self.aliases = dict(aliases or {})

    def copies(self, in_refs, out_refs, send_sems, recv_sems, loc_sems):
        remote, local = self.plan(in_refs, out_refs)
        assert len(remote) == self.n_remote and len(local) == self.n_local
        cps = [_remote(s, d, send_sems.at[t], recv_sems.at[t], dev) for t, (s, d, dev) in enumerate(remote)]
        cps += [pltpu.make_async_copy(s, d, loc_sems.at[t]) for t, (s, d) in enumerate(local)]
        return cps

    def sem_shapes(self):
        return [pltpu.SemaphoreType.DMA((max(self.n_remote, 1),)), pltpu.SemaphoreType.DMA((max(self.n_remote, 1),)),
                pltpu.SemaphoreType.DMA((max(self.n_local, 1),))]


def _exchange_call(name, ex):
    n_ci = len(ex.inputs)

    def body(*refs):
        cins, couts = refs[:n_ci], refs[n_ci:n_ci + len(ex.out_shapes)]
        cps = ex.copies(cins, couts, *refs[n_ci + len(ex.out_shapes):])
        for cp in cps:
            cp.start()
        for cp in cps:
            cp.wait()

    return pl.pallas_call(
        body,
        name=name,
        in_specs=[ANY] * n_ci,
        out_specs=[ANY] * len(ex.out_shapes),
        out_shape=ex.out_shapes,
        scratch_shapes=ex.sem_shapes(),
        input_output_aliases=ex.aliases,
    )(*ex.inputs)


def _host_call(body, *, name, grid, in_specs, out_specs, out_shape, args, scratch_shapes=(), aliases=None, ride=None,
               prefetch=()):
    n_pre, n_in, n_out, n_scr = len(prefetch), len(in_specs), len(out_specs), len(scratch_shapes)
    sem = ("arbitrary",) * len(grid)
    aliases = {n_pre + i: o for i, o in (aliases or {}).items()}
    n_ci, n_co = (len(ride.inputs), len(ride.out_shapes)) if ride else (0, 0)

    def full_body(*refs):
        pre, refs = refs[:n_pre], refs[n_pre:]
        ins, cins = refs[:n_in], refs[n_in:n_in + n_ci]
        o0 = n_in + n_ci
        outs, couts = refs[o0:o0 + n_out], refs[o0 + n_out:o0 + n_out + n_co]
        s0 = o0 + n_out + n_co
        scr, sems = refs[s0:s0 + n_scr], refs[s0 + n_scr:]
        if ride is None:
            body(*pre, *ins, *outs, *scr)
            return
        first = functools.reduce(jnp.logical_and, [pl.program_id(d) == 0 for d in range(len(grid))])
        last = functools.reduce(jnp.logical_and, [pl.program_id(d) == grid[d] - 1 for d in range(len(grid))])

        @pl.when(first)
        def _():
            for cp in ride.copies(cins, couts, *sems):
                cp.start()

        body(*pre, *ins, *outs, *scr)

        @pl.when(last)
        def _():
            for cp in ride.copies(cins, couts, *sems):
                cp.wait()

    if ride:
        for ci, co in ride.aliases.items():
            aliases[n_pre + n_in + ci] = n_out + co
    all_in = list(in_specs) + [ANY] * n_ci
    all_out = list(out_specs) + [ANY] * n_co
    all_scratch = list(scratch_shapes) + (ride.sem_shapes() if ride else [])
    grid_args = dict(grid=grid, in_specs=all_in, out_specs=all_out, scratch_shapes=all_scratch)
    if n_pre:
        grid_args = dict(grid_spec=pltpu.PrefetchScalarGridSpec(num_scalar_prefetch=n_pre, **grid_args))
    outs = pl.pallas_call(
        full_body, name=name, out_shape=list(out_shape) + (ride.out_shapes if ride else []),
        input_output_aliases=aliases, compiler_params=_params(*sem), **grid_args,
    )(*prefetch, *args, *(ride.inputs if ride else []))
    return list(outs[:n_out]), list(outs[n_out:])


def _gather_layer0_halves(win_s):
    def plan(cins, couts):
        _, _, c, chip, peers = _place()
        (src,), (dst,) = cins, couts
        return ([(src.at[0, c], dst.at[chip, c], (px, py, c)) for px, py in peers], [(src.at[0], dst.at[chip])])

    return _Exchange([win_s], [jax.ShapeDtypeStruct((N_CHIPS,) + win_s.shape[1:], win_s.dtype)], plan, 3, 1)


def _forward_layer0_halves(win_f0):
    def plan(cins, couts):
        x, y, c, _, peers = _place()
        (dst,) = couts
        return [(dst.at[2 * px + py, c], dst.at[2 * px + py, c], (x, y, 1 - c)) for px, py in peers], []

    return _Exchange([win_f0], [jax.ShapeDtypeStruct(win_f0.shape, win_f0.dtype)], plan, 3, 0, aliases={0: 0})


def _gather_layer1_first_half(win_s):
    def plan(cins, couts):
        _, _, c, chip, peers = _place()
        (src,), (dst,) = cins, couts
        return ([(src.at[1, 0], dst.at[chip, 0], (px, py, c)) for px, py in peers],
                [(src.at[1], dst.at[chip])])

    return _Exchange([win_s], [jax.ShapeDtypeStruct((N_CHIPS,) + win_s.shape[1:], win_s.dtype)], plan, 3, 1)


def _gather_layer1_second_half(win_s, win_f1):
    def plan(cins, couts):
        _, _, c, chip, peers = _place()
        src, (dst,) = cins[0], couts
        return [(src.at[1, 1], dst.at[chip, 1], (px, py, c)) for px, py in peers], []

    return _Exchange([win_s, win_f1], [jax.ShapeDtypeStruct(win_f1.shape, win_f1.dtype)], plan, 3, 0, aliases={1: 0})


def _gather_whole(shards, layer):
    na = len(shards)

    def plan(cins, couts):
        _, _, c, chip, peers = _place()
        remote = [(cins[a].at[layer], couts[a].at[chip], (px, py, c)) for a in range(na) for px, py in peers]
        return remote, [(cins[a].at[layer], couts[a].at[chip]) for a in range(na)]

    return _Exchange(shards, [jax.ShapeDtypeStruct((N_CHIPS,) + a.shape[1:], a.dtype) for a in shards], plan,
                     3 * na, na)


def _join(a, b):
    n_in, n_out = len(a.inputs), len(a.out_shapes)

    def plan(cins, couts):
        remote_a, local_a = a.plan(cins[:n_in], couts[:n_out])
        remote_b, local_b = b.plan(cins[n_in:], couts[n_out:])
        return remote_a + remote_b, local_a + local_b

    aliases = dict(a.aliases)
    aliases.update({n_in + i: n_out + o for i, o in b.aliases.items()})
    return _Exchange(a.inputs + b.inputs, a.out_shapes + b.out_shapes, plan, a.n_remote + b.n_remote,
                     a.n_local + b.n_local, aliases)


def _swap_halves(grads):
    def plan(cins, couts):
        x, y, c, _, _ = _place()
        return [(g.at[:, 1 - c], r, (x, y, 1 - c)) for g, r in zip(cins, couts)], []

    return _Exchange(grads, [jax.ShapeDtypeStruct(g.shape[:1] + g.shape[2:], g.dtype) for g in grads], plan,
                     len(grads))


def _scatter_chip_sums(sums):
    na = len(sums)

    def plan(cins, couts):
        _, _, c, chip, peers = _place()
        remote = [(cins[a].at[2 * px + py], couts[a].at[chip, c], (px, py, c)) for a in range(na) for px, py in peers]
        return remote, [(cins[a].at[chip], couts[a].at[chip, c]) for a in range(na)]

    return _Exchange(sums, [jax.ShapeDtypeStruct((N_CHIPS, 2) + s.shape[1:], s.dtype) for s in sums], plan,
                     3 * na, na)


def _share_halves(parts):
    na = len(parts)

    def plan(cins, couts):
        x, y, c, _, _ = _place()
        return [(p.at[:, c], p.at[:, c], (x, y, 1 - c)) for p in couts], []

    return _Exchange(parts, [jax.ShapeDtypeStruct(p.shape, p.dtype) for p in parts], plan, na, 0,
                     aliases={a: a for a in range(na)})


def _allreduce_small(packed, ride):
    rows = packed.shape[0]
    n_ci, n_co = len(ride.inputs), len(ride.out_shapes)

    def body(*refs):
        sg_ref, cins = refs[0], refs[1:1 + n_ci]
        res_ref, couts = refs[1 + n_ci], refs[2 + n_ci:2 + n_ci + n_co]
        sib_ref, cs_ref, rem_ref, s1_send, s1_recv, s2_send, s2_recv = refs[2 + n_ci + n_co:9 + n_ci + n_co]
        riding = ride.copies(cins, couts, *refs[9 + n_ci + n_co:])
        for cp in riding:
            cp.start()
        x, y, c, chip, peers = _place()
        cp = _remote(sg_ref, sib_ref, s1_send, s1_recv, (x, y, 1 - c))
        cp.start()
        cp.wait()
        cs_ref[...] = sg_ref[...] + sib_ref[...]
        rem_ref[chip] = cs_ref[...]
        cps = [_remote(cs_ref, rem_ref.at[chip], s2_send.at[k], s2_recv.at[chip], (px, py, c))
               for k, (px, py) in enumerate(peers)]
        for cp in cps:
            cp.start()
        for k, (px, py) in enumerate(peers):
            pchip = 2 * px + py
            _remote(cs_ref, rem_ref.at[pchip], s2_send.at[k], s2_recv.at[pchip], (px, py, c)).wait_recv()
        for cp in cps:
            cp.wait_send()
        res_ref[...] = ((rem_ref[0] + rem_ref[1]) + rem_ref[2]) + rem_ref[3]
        for cp in riding:
            cp.wait()

    outs = pl.pallas_call(
        body,
        name="allreduce_small",
        in_specs=[VMEM_WHOLE] + [ANY] * n_ci,
        out_specs=[VMEM_WHOLE] + [ANY] * n_co,
        out_shape=[jax.ShapeDtypeStruct((rows, GROUP_W), F32)] + ride.out_shapes,
        scratch_shapes=[
            pltpu.VMEM((rows, GROUP_W), F32),
            pltpu.VMEM((rows, GROUP_W), F32),
            pltpu.VMEM((N_CHIPS, rows, GROUP_W), F32),
            pltpu.SemaphoreType.DMA,
            pltpu.SemaphoreType.DMA,
            pltpu.SemaphoreType.DMA((3,)),
            pltpu.SemaphoreType.DMA((N_CHIPS,)),
        ] + ride.sem_shapes(),
        input_output_aliases={1 + i: 1 + o for i, o in ride.aliases.items()},
        compiler_params=pltpu.CompilerParams(vmem_limit_bytes=VMEM_LIMIT),
    )(packed, *ride.inputs)
    return outs[0], list(outs[1:])


def _swap_whole(packed):
    def plan(cins, couts):
        x, y, c, _, _ = _place()
        return [(cins[0], couts[0], (x, y, 1 - c))], []

    return _Exchange([packed], [jax.ShapeDtypeStruct(packed.shape, packed.dtype)], plan, 1)


def _spread_chip_sums(chip_sum):
    def plan(cins, couts):
        _, _, c, chip, peers = _place()
        return ([(cins[0], couts[0].at[chip], (px, py, c)) for px, py in peers], [(cins[0], couts[0].at[chip])])

    return _Exchange([chip_sum], [jax.ShapeDtypeStruct((N_CHIPS,) + chip_sum.shape, chip_sum.dtype)], plan, 3, 1)


def _add2(a, b, name):
    def body(a_ref, b_ref, o_ref):
        o_ref[...] = a_ref[...] + b_ref[...]

    return pl.pallas_call(body, name=name, in_specs=[VMEM_WHOLE] * 2, out_specs=VMEM_WHOLE,
                          out_shape=jax.ShapeDtypeStruct(a.shape, F32))(a, b)


def _sum_chips(parts, name):
    def body(p_ref, o_ref):
        o_ref[...] = ((p_ref[0] + p_ref[1]) + p_ref[2]) + p_ref[3]

    return pl.pallas_call(body, name=name, in_specs=[VMEM_WHOLE], out_specs=VMEM_WHOLE,
                          out_shape=jax.ShapeDtypeStruct(parts.shape[1:], F32))(parts)


def _add_halves(grad, other, core, name):
    n_shards, _, rows, cols = grad.shape
    tr = _tile(rows, max(16, (1 << 22) // (4 * cols) // 16 * 16), mult=16)

    def body(core_ref, g_ref, o_ref, out_ref):
        del core_ref
        out_ref[...] = (g_ref[...] + o_ref[...]).astype(BF16)

    return pl.pallas_call(
        body,
        name=name,
        grid_spec=pltpu.PrefetchScalarGridSpec(
            num_scalar_prefetch=1,
            grid=(n_shards, rows // tr),
            in_specs=[
                pl.BlockSpec((None, None, tr, cols), lambda s, i, core_ref: (s, core_ref[0], i, 0)),
                pl.BlockSpec((None, tr, cols), lambda s, i, core_ref: (s, i, 0)),
            ],
            out_specs=pl.BlockSpec((None, tr, cols), lambda s, i, core_ref: (s, i, 0)),
        ),
        out_shape=jax.ShapeDtypeStruct((n_shards, rows, cols), BF16),
        compiler_params=_params("arbitrary", "arbitrary"),
    )(core, grad, other)


def _inproj_fwd(x2, g_row, w_l, layer, ride=None):
    n = x2.shape[0]
    tm = _tile(n, 2048)

    def body(x_ref, g_ref, w_ref, p_ref, h_ref):
        @pl.when(pl.program_id(1) == 0)
        def _():
            xv = x_ref[...]
            r = lax.rsqrt(_lanemean(xv * xv) + RMS_EPS)
            h_ref[...] = (xv * r * g_ref[...]).astype(BF16)

        p_ref[...] = _dot(h_ref[...], w_ref[...]).astype(BF16)

    return _host_call(
        body,
        name=f"inproj_fwd_l{layer}",
        grid=(n // tm, N_COL_TILES),
        in_specs=[
            pl.BlockSpec((tm, D_MODEL), lambda i, j: (i, 0)),
            pl.BlockSpec((1, D_MODEL), lambda i, j: (0, 0)),
            pl.BlockSpec((None, D_MODEL, COL_TILE), lambda i, j: (j // 2, 0, j % 2)),
        ],
        out_specs=[
            pl.BlockSpec((tm, COL_TILE), lambda i, j: (i, j)),
            pl.BlockSpec((tm, D_MODEL), lambda i, j: (i, 0)),
        ],
        out_shape=[jax.ShapeDtypeStruct((n, IN_COLS), BF16), jax.ShapeDtypeStruct((n, D_MODEL), BF16)],
        args=[x2, g_row, w_l],
        ride=ride,
    )


def _inproj_fwd_own(x2, g_row, w_own, place, ride):
    n = x2.shape[0]
    tm = _tile(n, 2048)

    def body(place_ref, x_ref, g_ref, w_ref, p_ref, h_ref):
        del place_ref

        @pl.when(pl.program_id(1) == 0)
        def _():
            xv = x_ref[...]
            r = lax.rsqrt(_lanemean(xv * xv) + RMS_EPS)
            h_ref[...] = (xv * r * g_ref[...]).astype(BF16)

        p_ref[...] = _dot(h_ref[...], w_ref[...]).astype(BF16)

    return _host_call(
        body,
        name="inproj_fwd_own_l0",
        grid=(n // tm, 2),
        in_specs=[
            pl.BlockSpec((tm, D_MODEL), lambda i, j, place_ref: (i, 0)),
            pl.BlockSpec((1, D_MODEL), lambda i, j, place_ref: (0, 0)),
            pl.BlockSpec((D_MODEL, COL_TILE), lambda i, j, place_ref: (0, j)),
        ],
        out_specs=[
            pl.BlockSpec((tm, COL_TILE), lambda i, j, place_ref: (i, 2 * place_ref[0] + j)),
            pl.BlockSpec((tm, D_MODEL), lambda i, j, place_ref: (i, 0)),
        ],
        out_shape=[jax.ShapeDtypeStruct((n, IN_COLS), BF16), jax.ShapeDtypeStruct((n, D_MODEL), BF16)],
        args=[x2, g_row, w_own],
        ride=ride,
        prefetch=[place],
    )


def _inproj_fwd_rest(h2, w_l, proj_prev, place, ride):
    n = h2.shape[0]
    tm = _tile(n, 2048)

    def shard_of(j, place_ref):
        k = j // 2
        return place_ref[0] ^ jnp.where(k == 0, 2, jnp.where(k == 1, 1, 3))

    def body(place_ref, h_ref, w_ref, prev_ref, p_ref):
        del place_ref, prev_ref
        p_ref[...] = _dot(h_ref[...], w_ref[...]).astype(BF16)

    return _host_call(
        body,
        name="inproj_fwd_rest_l0",
        grid=(n // tm, 2 * (N_CHIPS - 1)),
        in_specs=[
            pl.BlockSpec((tm, D_MODEL), lambda i, j, place_ref: (i, 0)),
            pl.BlockSpec((None, D_MODEL, COL_TILE), lambda i, j, place_ref: (shard_of(j, place_ref), 0, j % 2)),
            ANY,
        ],
        out_specs=[pl.BlockSpec((tm, COL_TILE), lambda i, j, place_ref: (i, 2 * shard_of(j, place_ref) + j % 2))],
        out_shape=[jax.ShapeDtypeStruct(proj_prev.shape, BF16)],
        args=[h2, w_l, proj_prev],
        aliases={2: 0},
        ride=ride,
        prefetch=[place],
    )


def _layer_norm_parts(v, g, b):
    mu = _lanemean(v)
    d = v - mu
    rstd = lax.rsqrt(_lanemean(d * d) + LN_EPS)
    xh = d * rstd
    return xh, rstd, xh * g + b


def _window_sum(ref, slot, weight_row, n_taps, base, t_rows):
    total = None
    for b in range(8):
        group = None
        for a in range((base + n_taps - 1) // 8 + 1):
            k = 8 * a + b - base
            if 0 <= k < n_taps:
                term = weight_row(k) * ref[slot, 8 * a:8 * a + t_rows + 8, :]
                group = term if group is None else group + term
        if group is not None:
            part = group[b:b + t_rows, :]
            total = part if total is None else total + part
    return total


def _tap_grads(acc_ref, pad_ref, shift_ref, dy, ref, slot, n_taps, base, t_rows):
    pad_ref[0:8, :] = jnp.zeros((8, BRANCH_W), F32)
    pad_ref[8:8 + t_rows, :] = dy
    pad_ref[8 + t_rows:16 + t_rows, :] = jnp.zeros((8, BRANCH_W), F32)
    for b in range(8):
        taps = [(a, 8 * a + b - base) for a in range((base + n_taps - 1) // 8 + 1) if 0 <= 8 * a + b - base < n_taps]
        if not taps:
            continue
        shift_ref[...] = pad_ref[pl.ds(8 - b, t_rows + 8), :]
        for a, k in taps:
            prod = shift_ref[...] * ref[slot, 8 * a:8 * a + t_rows + 8, :]
            acc_ref[k] += prod.reshape((t_rows + 8) // 8, 8, BRANCH_W).sum(axis=0)


def _tril_mask():
    r = lax.broadcasted_iota(jnp.int32, (CHUNK, CHUNK), 0)
    c = lax.broadcasted_iota(jnp.int32, (CHUNK, CHUNK), 1)
    return r >= c


def _lanes(refs4):
    return jnp.concatenate([refs4[g] for g in range(N_CHIPS)], axis=1)


def _mixer_weight_specs():
    def whole(shape):
        nd = len(shape)
        return pl.BlockSpec(shape, lambda b, c: (0,) * nd)

    return [
        whole((N_GROUPS, GROUP_W, GROUP_W)),
        whole((1, BRANCH_W)),
        whole((N_CHIPS, CONV_ROWS, GROUP_W)),
        whole((1, BRANCH_W)),
        whole((1, BRANCH_W)),
        whole((1, BRANCH_W)),
        whole((1, BRANCH_W)),
        whole((1, BRANCH_W)),
        whole((N_GROUPS, CHUNK, CHUNK)),
        whole((CHUNK, BRANCH_W)),
        whole((N_CHIPS, SHORT_ROWS, GROUP_W)),
    ]


def _mixer_weight_args(mw):
    return [mw["pool_w"], mw["pool_scale"], mw["conv_w"], mw["conv_b"], mw["conv_ln_g"], mw["conv_ln_b"],
            mw["sgu_ln_g"], mw["sgu_ln_b"], mw["sgu_w"], mw["sgu_bias"], mw["sc_w"]]


def _mixers_fwd(proj3, mw, layer, ride=None):
    nb, seq, _ = proj3.shape
    t_rows = _tile(seq, 512)
    nc = seq // t_rows
    hb = t_rows // HALO

    def body(cur_ref, halo_ref, pw_ref, ps_ref, cw_ref, cb_ref, clg_ref, clb_ref, slg_ref, slb_ref, sw_ref,
             sbias_ref, scw_ref, z_ref, cacc_ref, ext_ref):
        c = pl.program_id(1)
        has_prev = c > 0
        row = lax.broadcasted_iota(jnp.int32, (t_rows, 1), 0)
        tpos = (c * t_rows + row + 1).astype(F32)

        def cur(k):
            return cur_ref[:, k * BRANCH_W:(k + 1) * BRANCH_W].astype(F32)

        def hal(k):
            return halo_ref[:, k * BRANCH_W:(k + 1) * BRANCH_W].astype(F32)

        def put_ext(slot, halo_val, cur_val):
            ext_ref[slot, 0:HALO, :] = jnp.where(has_prev, halo_val, 0.0)
            ext_ref[slot, HALO:HALO + t_rows, :] = cur_val
            ext_ref[slot, HALO + t_rows:HALO + t_rows + 8, :] = jnp.zeros((8, BRANCH_W), F32)

        px = cur(P_X)
        put_ext(0, hal(P_X), px)
        mixed = []
        for j, win in enumerate(POOL_WINDOWS):
            cols = slice(j * GROUP_W, (j + 1) * GROUP_W)
            s = px[:, cols]
            for i in range(1, win):
                s = s + ext_ref[0, pl.ds(HALO - i, t_rows), cols]
            pooled = s / jnp.minimum(tpos, float(win)) - px[:, cols]
            mixed.append(_dot(pooled.astype(BF16), pw_ref[j]))
        z_pool = jnp.concatenate(mixed, axis=1) * ps_ref[...] * _silu(cur(P_GATE))
        z_ref[:, 0:BRANCH_W] = z_pool.astype(BF16)

        put_ext(1, hal(C_A) * _sigmoid(hal(C_B)), cur(C_A) * _sigmoid(cur(C_B)))
        cw = _lanes(cw_ref)
        acc = cb_ref[...] + _window_sum(ext_ref, 1, lambda k: cw[k:k + 1, :], CONV_K, HALO - (CONV_K - 1), t_rows)
        cacc_ref[...] = acc
        _, _, ln = _layer_norm_parts(acc, clg_ref[...], clb_ref[...])
        z_ref[:, BRANCH_W:2 * BRANCH_W] = (_silu(ln) * _silu(cur(C_GATE))).astype(BF16)

        _, _, v = _layer_norm_parts(cur(G_V), slg_ref[...], slb_ref[...])
        vb = v.astype(BF16)
        mask = _tril_mask()
        wt = [jnp.where(mask, sw_ref[g], 0.0).astype(BF16) for g in range(N_GROUPS)]
        sp_rows = []
        for sub in range(t_rows // CHUNK):
            rows = slice(sub * CHUNK, (sub + 1) * CHUNK)
            sp_rows.append(jnp.concatenate(
                [_dot(wt[g], vb[rows, g * GROUP_W:(g + 1) * GROUP_W]) for g in range(N_GROUPS)], axis=1)
                + sbias_ref[...])
        sp = jnp.concatenate(sp_rows, axis=0)
        z_ref[:, 2 * BRANCH_W:3 * BRANCH_W] = (cur(G_U) * sp * _silu(cur(G_GATE))).astype(BF16)

        put_ext(2, hal(S_C) * hal(S_X), cur(S_C) * cur(S_X))
        scw = _lanes(scw_ref)
        cv = _window_sum(ext_ref, 2, lambda k: scw[k:k + 1, :], SHORT_K, HALO - (SHORT_K - 1), t_rows)
        z_ref[:, 3 * BRANCH_W:4 * BRANCH_W] = (cur(S_B) * cv * _silu(cur(S_GATE))).astype(BF16)

    return _host_call(
        body,
        name=f"mixers_fwd_l{layer}",
        grid=(nb, nc),
        in_specs=[
            pl.BlockSpec((None, t_rows, PIECE_COLS), lambda b, c: (b, c, 0)),
            pl.BlockSpec((None, HALO, PIECE_COLS), lambda b, c: (b, jnp.maximum(c * hb - 1, 0), 0)),
        ] + _mixer_weight_specs(),
        out_specs=[pl.BlockSpec((None, t_rows, N_BRANCH * BRANCH_W), lambda b, c: (b, c, 0)),
                   pl.BlockSpec((None, t_rows, BRANCH_W), lambda b, c: (b, c, 0))],
        out_shape=[jax.ShapeDtypeStruct((nb, seq, N_BRANCH * BRANCH_W), BF16),
                   jax.ShapeDtypeStruct((nb, seq, BRANCH_W), F32)],
        scratch_shapes=[pltpu.VMEM((3, HALO + t_rows + 8, BRANCH_W), F32)],
        args=[proj3, proj3, *_mixer_weight_args(mw)],
        ride=ride,
    )


MIXER_GRADS = ["pool_w", "pool_scale", "conv_w", "conv_b", "conv_ln_g", "conv_ln_b", "sgu_ln_g", "sgu_ln_b", "sgu_w",
               "sgu_b", "sc_w"]


def _mixers_bwd(proj3, dz3, cacc3, dproj3, mw, layer, ride=None):
    nb, seq, _ = proj3.shape
    t_rows = _tile(seq, 512)
    nc = seq // t_rows
    hb = t_rows // HALO

    def body(cur_ref, halo_ref, dz_ref, cacc_ref, pw_ref, ps_ref, cw_ref, cb_ref, clg_ref, clb_ref, slg_ref, slb_ref,
             sw_ref, sbias_ref, scw_ref, dp_in_ref, dp_ref, g_pw, g_ps, g_cw, g_cb, g_clg, g_clb, g_slg,
             g_slb, g_sw, g_sb, g_scw, ext_ref, nxt_ref, sb_acc, cw_acc, scw_acc, pad_ref, shift_ref):
        del dp_in_ref
        b = pl.program_id(0)
        r = pl.program_id(1)
        c = nc - 1 - r
        has_prev = c > 0
        row = lax.broadcasted_iota(jnp.int32, (t_rows, 1), 0)
        tpos = (c * t_rows + row + 1).astype(F32)

        @pl.when(jnp.logical_and(b == 0, r == 0))
        def _():
            for ref in (g_pw, g_ps, g_cb, g_clg, g_clb, g_slg, g_slb, g_sw, sb_acc, cw_acc, scw_acc):
                ref[...] = jnp.zeros_like(ref)

        @pl.when(r == 0)
        def _():
            nxt_ref[:, t_rows:t_rows + HALO, :] = jnp.zeros((3, HALO, BRANCH_W), F32)

        def cur(k):
            return cur_ref[:, k * BRANCH_W:(k + 1) * BRANCH_W].astype(F32)

        def hal(k):
            return halo_ref[:, k * BRANCH_W:(k + 1) * BRANCH_W].astype(F32)

        def dzp(k):
            return dz_ref[:, k * BRANCH_W:(k + 1) * BRANCH_W].astype(F32)

        def put_dp(k, val):
            dp_ref[:, k * BRANCH_W:(k + 1) * BRANCH_W] = val.astype(BF16)

        def put_ext(slot, halo_val, cur_val):
            ext_ref[slot, 0:HALO, :] = jnp.where(has_prev, halo_val, 0.0)
            ext_ref[slot, HALO:HALO + t_rows, :] = cur_val
            ext_ref[slot, HALO + t_rows:HALO + t_rows + 8, :] = jnp.zeros((8, BRANCH_W), F32)

        px = cur(P_X)
        put_ext(0, hal(P_X), px)
        pgate = cur(P_GATE)
        dz_pool = dzp(0)
        ps = ps_ref[...]
        pooled, mixed, cnts = [], [], []
        for j, win in enumerate(POOL_WINDOWS):
            cols = slice(j * GROUP_W, (j + 1) * GROUP_W)
            s = px[:, cols]
            for i in range(1, win):
                s = s + ext_ref[0, pl.ds(HALO - i, t_rows), cols]
            cnt = jnp.minimum(tpos, float(win))
            pj = s / cnt - px[:, cols]
            cnts.append(cnt)
            pooled.append(pj.astype(BF16))
            mixed.append(_dot(pooled[j], pw_ref[j]))
        mixed = jnp.concatenate(mixed, axis=1)
        pg_silu, pg_dsilu = _silu_pair(pgate)
        put_dp(P_GATE, dz_pool * (mixed * ps) * pg_dsilu)
        d_out = dz_pool * pg_silu
        g_ps[...] += _rowsum(d_out * mixed)
        d_mixed = (d_out * ps).astype(BF16)
        d_pooled = []
        for j in range(N_GROUPS):
            cols = slice(j * GROUP_W, (j + 1) * GROUP_W)
            g_pw[j] += _dot_tn(pooled[j], d_mixed[:, cols])
            dpj = _dot_nt(d_mixed[:, cols], pw_ref[j])
            d_pooled.append(dpj)
            nxt_ref[0, 0:t_rows, cols] = dpj / cnts[j]
        dpx = []
        for j, win in enumerate(POOL_WINDOWS):
            cols = slice(j * GROUP_W, (j + 1) * GROUP_W)
            s = nxt_ref[0, 0:t_rows, cols]
            for i in range(1, win):
                s = s + nxt_ref[0, pl.ds(i, t_rows), cols]
            dpx.append(s - d_pooled[j])
        put_dp(P_X, jnp.concatenate(dpx, axis=1))

        c_a = cur(C_A)
        sig_b = _sigmoid(cur(C_B))
        put_ext(1, hal(C_A) * _sigmoid(hal(C_B)), c_a * sig_b)
        cw = _lanes(cw_ref)
        xh, rstd, ln = _layer_norm_parts(cacc_ref[...], clg_ref[...], clb_ref[...])
        cgate = cur(C_GATE)
        dz_conv = dzp(1)
        ln_silu, ln_dsilu = _silu_pair(ln)
        cg_silu, cg_dsilu = _silu_pair(cgate)
        put_dp(C_GATE, dz_conv * ln_silu * cg_dsilu)
        d_ln = dz_conv * cg_silu * ln_dsilu
        g_clg[...] += _rowsum(d_ln * xh)
        g_clb[...] += _rowsum(d_ln)
        dxh = d_ln * clg_ref[...]
        dc = rstd * (dxh - _lanemean(dxh) - xh * _lanemean(dxh * xh))
        g_cb[...] += _rowsum(dc)
        nxt_ref[1, 0:t_rows, :] = dc
        _tap_grads(cw_acc, pad_ref, shift_ref, dc, ext_ref, 1, CONV_K, HALO - (CONV_K - 1), t_rows)
        dyg = _window_sum(nxt_ref, 1, lambda i: cw[CONV_K - 1 - i:CONV_K - i, :], CONV_K, 0, t_rows)
        put_dp(C_A, dyg * sig_b)
        put_dp(C_B, dyg * c_a * sig_b * (1.0 - sig_b))

        u = cur(G_U)
        ggate = cur(G_GATE)
        vxh, vrstd, v = _layer_norm_parts(cur(G_V), slg_ref[...], slb_ref[...])
        vb = v.astype(BF16)
        mask = _tril_mask()
        wt = [jnp.where(mask, sw_ref[g], 0.0) for g in range(N_GROUPS)]
        wt_b = [w.astype(BF16) for w in wt]
        wtt_b = [w.T.astype(BF16) for w in wt]
        dz_sgu = dzp(2)
        gg_silu, gg_dsilu = _silu_pair(ggate)
        d_sgu = dz_sgu * gg_silu
        d_sp = d_sgu * u
        d_spb = d_sp.astype(BF16)
        sp_rows, dv_rows = [], []
        for sub in range(t_rows // CHUNK):
            rows = slice(sub * CHUNK, (sub + 1) * CHUNK)
            sp_g, dv_g = [], []
            for g in range(N_GROUPS):
                cols = slice(g * GROUP_W, (g + 1) * GROUP_W)
                sp_g.append(_dot(wt_b[g], vb[rows, cols]))
                g_sw[g] += jnp.where(mask, _dot_nt(d_spb[rows, cols], vb[rows, cols]), 0.0)
                dv_g.append(_dot(wtt_b[g], d_spb[rows, cols]))
            sp_rows.append(jnp.concatenate(sp_g, axis=1) + sbias_ref[...])
            dv_rows.append(jnp.concatenate(dv_g, axis=1))
            sb_acc[...] += d_sp[rows, :]
        sp = jnp.concatenate(sp_rows, axis=0)
        dv = jnp.concatenate(dv_rows, axis=0)
        put_dp(G_GATE, dz_sgu * (u * sp) * gg_dsilu)
        put_dp(G_U, d_sgu * sp)
        g_slg[...] += _rowsum(dv * vxh)
        g_slb[...] += _rowsum(dv)
        dvx = dv * slg_ref[...]
        put_dp(G_V, vrstd * (dvx - _lanemean(dvx) - vxh * _lanemean(dvx * vxh)))

        s_b, s_c, s_x, sgate = cur(S_B), cur(S_C), cur(S_X), cur(S_GATE)
        put_ext(2, hal(S_C) * hal(S_X), s_c * s_x)
        scw = _lanes(scw_ref)
        cv = _window_sum(ext_ref, 2, lambda k: scw[k:k + 1, :], SHORT_K, HALO - (SHORT_K - 1), t_rows)
        dz_sc = dzp(3)
        sg_silu, sg_dsilu = _silu_pair(sgate)
        put_dp(S_GATE, dz_sc * (s_b * cv) * sg_dsilu)
        d_pre = dz_sc * sg_silu
        put_dp(S_B, d_pre * cv)
        dcv = d_pre * s_b
        nxt_ref[2, 0:t_rows, :] = dcv
        _tap_grads(scw_acc, pad_ref, shift_ref, dcv, ext_ref, 2, SHORT_K, HALO - (SHORT_K - 1), t_rows)
        du = _window_sum(nxt_ref, 2, lambda i: scw[SHORT_K - 1 - i:SHORT_K - i, :], SHORT_K, 0, t_rows)
        put_dp(S_C, du * s_x)
        put_dp(S_X, du * s_c)

        nxt_ref[:, t_rows:t_rows + HALO, :] = nxt_ref[:, 0:HALO, :]

        @pl.when(jnp.logical_and(b == nb - 1, r == nc - 1))
        def _():
            lane = lax.broadcasted_iota(jnp.int32, (CHUNK, GROUP_W), 1)
            out = jnp.zeros((CHUNK, GROUP_W), F32)
            for g in range(N_GROUPS):
                col = jnp.sum(sb_acc[:, g * GROUP_W:(g + 1) * GROUP_W], axis=1, keepdims=True)
                out = jnp.where(lane == g, col, out)
            g_sb[...] = out
            g_cw[...] = jnp.sum(cw_acc[...], axis=1)
            g_scw[...] = jnp.sum(scw_acc[...], axis=1)

    def acc_spec(shape):
        nd = len(shape)
        return pl.BlockSpec(shape, lambda b, r: (0,) * nd)

    acc_shapes = [
        (N_GROUPS, GROUP_W, GROUP_W),
        (1, BRANCH_W),
        (CONV_ROWS, BRANCH_W),
        (1, BRANCH_W),
        (1, BRANCH_W),
        (1, BRANCH_W),
        (1, BRANCH_W),
        (1, BRANCH_W),
        (N_GROUPS, CHUNK, CHUNK),
        (CHUNK, GROUP_W),
        (SHORT_ROWS, BRANCH_W),
    ]
    outs, rode = _host_call(
        body,
        name=f"mixers_bwd_l{layer}",
        grid=(nb, nc),
        in_specs=[
            pl.BlockSpec((None, t_rows, PIECE_COLS), lambda b, r: (b, nc - 1 - r, 0)),
            pl.BlockSpec((None, HALO, PIECE_COLS), lambda b, r: (b, jnp.maximum((nc - 1 - r) * hb - 1, 0), 0)),
            pl.BlockSpec((None, t_rows, N_BRANCH * BRANCH_W), lambda b, r: (b, nc - 1 - r, 0)),
            pl.BlockSpec((None, t_rows, BRANCH_W), lambda b, r: (b, nc - 1 - r, 0)),
        ] + _mixer_weight_specs() + [ANY],
        out_specs=[pl.BlockSpec((None, t_rows, PIECE_COLS), lambda b, r: (b, nc - 1 - r, 0))]
        + [acc_spec(s) for s in acc_shapes],
        out_shape=[jax.ShapeDtypeStruct(dproj3.shape, BF16)] + [jax.ShapeDtypeStruct(s, F32) for s in acc_shapes],
        scratch_shapes=[
            pltpu.VMEM((3, HALO + t_rows + 8, BRANCH_W), F32),
            pltpu.VMEM((3, t_rows + HALO, BRANCH_W), F32),
            pltpu.VMEM((CHUNK, BRANCH_W), F32),
            pltpu.VMEM((CONV_ROWS, 8, BRANCH_W), F32),
            pltpu.VMEM((SHORT_ROWS, 8, BRANCH_W), F32),
            pltpu.VMEM((t_rows + 16, BRANCH_W), F32),
            pltpu.VMEM((t_rows + 8, BRANCH_W), F32),
        ],
        aliases={4 + 11: 0},
        args=[proj3, proj3, dz3, cacc3, *_mixer_weight_args(mw), dproj3],
        ride=ride,
    )
    return outs[0], dict(zip(MIXER_GRADS, outs[1:])), rode


def _merge_fwd(z2, proj2, x2, wbr_l, wo_l, layer, ride=None):
    n = x2.shape[0]
    tm = _tile(n, 1024)

    def body(z_ref, gate_ref, x_ref, wbr_ref, wo_ref, xn_ref, mg_ref, acc_ref):
        nbr = pl.program_id(1)
        bo = _dot(z_ref[...], wbr_ref[nbr])
        contrib = _sigmoid(gate_ref[...].astype(F32)) * bo

        @pl.when(nbr == 0)
        def _():
            acc_ref[...] = contrib

        @pl.when(nbr > 0)
        def _():
            acc_ref[...] += contrib

        @pl.when(nbr == N_BRANCH - 1)
        def _():
            mg = acc_ref[...].astype(BF16)
            mg_ref[...] = mg
            xn_ref[...] = x_ref[...] + _dot(mg, wo_ref[...])

    return _host_call(
        body,
        name=f"merge_fwd_l{layer}",
        grid=(n // tm, N_BRANCH),
        in_specs=[
            pl.BlockSpec((tm, BRANCH_W), lambda i, b: (i, b)),
            pl.BlockSpec((tm, D_MODEL), lambda i, b: (i, GATE_BLOCK0 + b)),
            pl.BlockSpec((tm, D_MODEL), lambda i, b: (i, 0)),
            pl.BlockSpec((N_BRANCH, BRANCH_W, D_MODEL), lambda i, b: (0, 0, 0)),
            pl.BlockSpec((D_MODEL, D_MODEL), lambda i, b: (0, 0)),
        ],
        out_specs=[
            pl.BlockSpec((tm, D_MODEL), lambda i, b: (i, 0)),
            pl.BlockSpec((tm, D_MODEL), lambda i, b: (i, 0)),
        ],
        out_shape=[jax.ShapeDtypeStruct((n, D_MODEL), F32), jax.ShapeDtypeStruct((n, D_MODEL), BF16)],
        scratch_shapes=[pltpu.VMEM((tm, D_MODEL), F32)],
        args=[z2, proj2, x2, wbr_l, wo_l],
        ride=ride,
    )


def _merge_bwd(dout2, merged2, z2, proj2, wbr_l, wo_l, layer, ride=None):
    n = dout2.shape[0]
    tm = _tile(n, 1024)
    nt = n // tm

    def body(do_ref, mg_ref, z_ref, gate_ref, wbr_ref, wo_ref, dz_ref, dg_ref, gwo_ref, gwbr_ref, dm_ref,
             awo_ref, awbr_ref):
        i = pl.program_id(0)
        nbr = pl.program_id(1)

        @pl.when(nbr == 0)
        def _():
            do_b = do_ref[...].astype(BF16)
            dm_ref[...] = _dot_nt(do_b, wo_ref[...])
            gw = _dot_tn(mg_ref[...], do_b)

            @pl.when(i == 0)
            def _():
                awo_ref[...] = gw

            @pl.when(i > 0)
            def _():
                awo_ref[...] += gw

        zt = z_ref[...]
        wbr = wbr_ref[nbr]
        bo = _dot(zt, wbr)
        gt = _sigmoid(gate_ref[...].astype(F32))
        dm = dm_ref[...]
        dbo = (dm * gt).astype(BF16)
        dg_ref[...] = (dm * bo * gt * (1.0 - gt)).astype(BF16)
        dz_ref[...] = _dot_nt(dbo, wbr).astype(BF16)
        gw = _dot_tn(zt, dbo)

        @pl.when(i == 0)
        def _():
            awbr_ref[nbr] = gw

        @pl.when(i > 0)
        def _():
            awbr_ref[nbr] += gw

        @pl.when(jnp.logical_and(i == nt - 1, nbr == N_BRANCH - 1))
        def _():
            pltpu.sync_copy(awo_ref, gwo_ref)
            pltpu.sync_copy(awbr_ref, gwbr_ref)

    return _host_call(
        body,
        name=f"merge_bwd_l{layer}",
        grid=(nt, N_BRANCH),
        in_specs=[
            pl.BlockSpec((tm, D_MODEL), lambda i, b: (i, 0)),
            pl.BlockSpec((tm, D_MODEL), lambda i, b: (i, 0)),
            pl.BlockSpec((tm, BRANCH_W), lambda i, b: (i, b)),
            pl.BlockSpec((tm, D_MODEL), lambda i, b: (i, GATE_BLOCK0 + b)),
            pl.BlockSpec((N_BRANCH, BRANCH_W, D_MODEL), lambda i, b: (0, 0, 0)),
            pl.BlockSpec((D_MODEL, D_MODEL), lambda i, b: (0, 0)),
        ],
        out_specs=[
            pl.BlockSpec((tm, BRANCH_W), lambda i, b: (i, b)),
            pl.BlockSpec((tm, D_MODEL), lambda i, b: (i, GATE_BLOCK0 + b)),
            ANY,
            ANY,
        ],
        out_shape=[
            jax.ShapeDtypeStruct((n, N_BRANCH * BRANCH_W), BF16),
            jax.ShapeDtypeStruct((n, IN_COLS), BF16),
            jax.ShapeDtypeStruct((D_MODEL, D_MODEL), F32),
            jax.ShapeDtypeStruct((N_BRANCH, BRANCH_W, D_MODEL), F32),
        ],
        scratch_shapes=[
            pltpu.VMEM((tm, D_MODEL), F32),
            pltpu.VMEM((D_MODEL, D_MODEL), F32),
            pltpu.VMEM((N_BRANCH, BRANCH_W, D_MODEL), F32),
        ],
        args=[dout2, merged2, z2, proj2, wbr_l, wo_l],
        ride=ride,
    )


def _loss_head(x2, g_row, tgt2):
    n = x2.shape[0]
    tm = _tile(n, 512)

    def body(x_ref, g_ref, t_ref, dx_ref, loss_ref, dg_ref):
        @pl.when(pl.program_id(0) == 0)
        def _():
            loss_ref[...] = jnp.zeros_like(loss_ref)
            dg_ref[...] = jnp.zeros_like(dg_ref)

        xv = x_ref[...]
        g = g_ref[...]
        r = lax.rsqrt(_lanemean(xv * xv) + RMS_EPS)
        xh = xv * r
        err = xh * g - t_ref[...]
        loss_ref[...] += 0.5 * jnp.sum(_lanemean(err * err), axis=0, keepdims=True)
        dy = err * (1.0 / D_MODEL)
        dg_ref[...] += _rowsum(dy * xh)
        dxh = dy * g
        dx_ref[...] = r * (dxh - xh * _lanemean(dxh * xh))

    return pl.pallas_call(
        body,
        name="loss_head",
        grid=(n // tm,),
        in_specs=[
            pl.BlockSpec((tm, D_MODEL), lambda i: (i, 0)),
            pl.BlockSpec((1, D_MODEL), lambda i: (0, 0)),
            pl.BlockSpec((tm, D_MODEL), lambda i: (i, 0)),
        ],
        out_specs=[
            pl.BlockSpec((tm, D_MODEL), lambda i: (i, 0)),
            pl.BlockSpec((1, GROUP_W), lambda i: (0, 0)),
            pl.BlockSpec((1, D_MODEL), lambda i: (0, 0)),
        ],
        out_shape=[jax.ShapeDtypeStruct((n, D_MODEL), F32), jax.ShapeDtypeStruct((1, GROUP_W), F32),
                   jax.ShapeDtypeStruct((1, D_MODEL), F32)],
        compiler_params=_params("arbitrary"),
    )(x2, g_row, tgt2)


def _inproj_bwd_x(dproj2, w_l, x2, g_row, dout2, layer, ride=None):
    n = x2.shape[0]
    tm = _tile(n, 1024)

    def body(dp_ref, w_ref, x_ref, g_ref, do_ref, dx_ref, dng_ref, dh_ref):
        i = pl.program_id(0)
        s = pl.program_id(1)
        part = _dot_nt(dp_ref[...], w_ref[...])

        @pl.when(s == 0)
        def _():
            dh_ref[...] = part

        @pl.when(s > 0)
        def _():
            dh_ref[...] += part

        @pl.when(jnp.logical_and(i == 0, s == 0))
        def _():
            dng_ref[...] = jnp.zeros_like(dng_ref)

        @pl.when(s == N_CHIPS - 1)
        def _():
            xv = x_ref[...]
            r = lax.rsqrt(_lanemean(xv * xv) + RMS_EPS)
            xh = xv * r
            dh = dh_ref[...]
            dng_ref[...] += _rowsum(dh * xh)
            dxh = dh * g_ref[...]
            dx_ref[...] = do_ref[...] + r * (dxh - xh * _lanemean(dxh * xh))

    return _host_call(
        body,
        name=f"inproj_bwd_x_l{layer}",
        grid=(n // tm, N_CHIPS),
        in_specs=[
            pl.BlockSpec((tm, SHARD_COLS), lambda i, s: (i, s)),
            pl.BlockSpec((None, D_MODEL, SHARD_COLS), lambda i, s: (s, 0, 0)),
            pl.BlockSpec((tm, D_MODEL), lambda i, s: (i, 0)),
            pl.BlockSpec((1, D_MODEL), lambda i, s: (0, 0)),
            pl.BlockSpec((tm, D_MODEL), lambda i, s: (i, 0)),
        ],
        out_specs=[
            pl.BlockSpec((tm, D_MODEL), lambda i, s: (i, 0)),
            pl.BlockSpec((1, D_MODEL), lambda i, s: (0, 0)),
        ],
        out_shape=[jax.ShapeDtypeStruct((n, D_MODEL), F32), jax.ShapeDtypeStruct((1, D_MODEL), F32)],
        scratch_shapes=[pltpu.VMEM((tm, D_MODEL), F32)],
        args=[dproj2, w_l, x2, g_row, dout2],
        ride=ride,
    )


def _inproj_bwd_w(h2, dproj2, layer, ride=None):
    n = h2.shape[0]
    tm = _tile(n, 2048)

    def body(h_ref, dp_ref, gw_ref):
        gw = _dot_tn(h_ref[...], dp_ref[...])

        @pl.when(pl.program_id(1) == 0)
        def _():
            gw_ref[...] = gw

        @pl.when(pl.program_id(1) > 0)
        def _():
            gw_ref[...] += gw

    outs, rode = _host_call(
        body,
        name=f"inproj_bwd_w_l{layer}",
        grid=(N_COL_TILES, n // tm),
        in_specs=[
            pl.BlockSpec((tm, D_MODEL), lambda s, i: (i, 0)),
            pl.BlockSpec((tm, COL_TILE), lambda s, i: (i, s)),
        ],
        out_specs=[pl.BlockSpec((None, D_MODEL, COL_TILE), lambda s, i: (s // 2, 0, s % 2))],
        out_shape=[jax.ShapeDtypeStruct((N_CHIPS, D_MODEL, SHARD_COLS), F32)],
        args=[h2, dproj2],
        ride=ride,
    )
    return outs[0], rode


def _adamw_math(w, g, m, v):
    m = ADAM_B1 * m + (1.0 - ADAM_B1) * g
    v = ADAM_B2 * v + (1.0 - ADAM_B2) * jnp.square(g)
    m_hat = m / (1.0 - ADAM_B1 ** ADAM_STEP)
    v_hat = v / (1.0 - ADAM_B2 ** ADAM_STEP)
    delta = -ADAM_LR * (m_hat / (jnp.sqrt(v_hat) + ADAM_EPS) + ADAM_WD * w)
    return delta, m, v


def _adamw_sharded(w, m, v, part, layer, prev, name, ride=None):
    _, rows, cols = w.shape
    tr = _tile(rows, max(16, (1 << 21) // (4 * cols) // 16 * 16), mult=16)

    def body(w_ref, m_ref, v_ref, p_ref, *rest):
        g_out, d_out, m_out, v_out = rest[-4:]
        g = ((p_ref[0].astype(F32) + p_ref[1].astype(F32)) + p_ref[2].astype(F32)) + p_ref[3].astype(F32)
        delta, m2, v2 = _adamw_math(w_ref[...], g, m_ref[...], v_ref[...])
        g_out[...] = g
        d_out[...] = delta
        m_out[...] = m2
        v_out[...] = v2

    spec = pl.BlockSpec((None, tr, cols), lambda i: (layer, i, 0))
    return _host_call(
        body,
        name=f"adamw_{name}_l{layer}",
        grid=(rows // tr,),
        in_specs=[spec, spec, spec, pl.BlockSpec((N_CHIPS, tr, cols), lambda i: (0, i, 0))]
        + ([ANY] * 4 if prev else []),
        out_specs=[spec] * 4,
        out_shape=[jax.ShapeDtypeStruct(w.shape, F32)] * 4,
        args=[w, m, v, part] + (list(prev) if prev else []),
        aliases={4 + k: k for k in range(4)} if prev else {},
        ride=ride,
    )


def _adamw_packed(w, m, v, g):
    rows = w.shape[0]
    tr = _tile(rows, rows // 2 if rows % 16 == 0 else rows)

    def body(w_ref, m_ref, v_ref, g_ref, d_out, m_out, v_out):
        delta, m2, v2 = _adamw_math(w_ref[...], g_ref[...], m_ref[...], v_ref[...])
        d_out[...] = delta
        m_out[...] = m2
        v_out[...] = v2

    spec = pl.BlockSpec((tr, GROUP_W), lambda i: (i, 0))
    return pl.pallas_call(
        body,
        name="adamw_small",
        grid=(rows // tr,),
        in_specs=[spec] * 4,
        out_specs=[spec] * 3,
        out_shape=[jax.ShapeDtypeStruct(w.shape, F32)] * 3,
        compiler_params=_params("arbitrary"),
    )(w, m, v, g)


SMALL = ["norm_g", "pool_w", "pool_scale", "conv_b", "conv_ln_g", "conv_ln_b", "sgu_ln_g", "sgu_ln_b", "sgu_w",
         "sgu_b", "final_g"]
WEIGHTS = ["norm_g", "w_in", "pool_w", "pool_scale", "conv_w", "conv_b", "conv_ln_g", "conv_ln_b", "sgu_ln_g",
           "sgu_ln_b", "sgu_w", "sgu_b", "sc_w", "w_branch", "w_o", "final_g"]
HALF_SHAPES = [(D_MODEL // 2, SHARD_COLS), (N_BRANCH * BRANCH_W // 2, BR_SHARD), (BR_SHARD // 2, D_MODEL),
               (CS_ROWS // 2, GROUP_W)]


def _pack_small(tree, last_rows=None):
    tail = jnp.zeros((8, GROUP_W), F32) if last_rows is None else last_rows
    return jnp.concatenate([tree[k].reshape(-1, GROUP_W) for k in SMALL] + [tail], axis=0)


def _unpack_small(packed, like):
    out, r = {}, 0
    for k in SMALL:
        nr = like[k].size // GROUP_W
        out[k] = packed[r:r + nr].reshape(like[k].shape)
        r += nr
    return out


def _pad_rows(a, rows):
    pad = [(0, 0)] * a.ndim
    pad[-2] = (0, rows - a.shape[-2])
    return jnp.pad(a, pad)


def _pack_cs(conv, short):
    return jnp.concatenate([_pad_rows(conv, CS_ROWS // 2), _pad_rows(short, CS_ROWS // 2)], axis=-2)


def _shard_major_rows(a):
    return a.reshape(a.shape[0], N_CHIPS, GROUP_W).transpose(1, 0, 2)


def kernel(x, norm_g, w_in, pool_w, pool_scale, conv_w, conv_b, conv_ln_g, conv_ln_b, sgu_ln_g, sgu_ln_b, sgu_w, sgu_b, sc_w, w_branch, w_o, final_g, loss_target, m_norm_g, m_w_in, m_pool_w, m_pool_scale, m_conv_w, m_conv_b, m_conv_ln_g, m_conv_ln_b, m_sgu_ln_g, m_sgu_ln_b, m_sgu_w, m_sgu_b, m_sc_w, m_w_branch, m_w_o, m_final_g, v_norm_g, v_w_in, v_pool_w, v_pool_scale, v_conv_w, v_conv_b, v_conv_ln_g, v_conv_ln_b, v_sgu_ln_g, v_sgu_ln_b, v_sgu_w, v_sgu_b, v_sc_w, v_w_branch, v_w_o, v_final_g):
    w = dict(norm_g=norm_g, w_in=w_in, pool_w=pool_w, pool_scale=pool_scale, conv_w=conv_w, conv_b=conv_b,
             conv_ln_g=conv_ln_g, conv_ln_b=conv_ln_b, sgu_ln_g=sgu_ln_g, sgu_ln_b=sgu_ln_b, sgu_w=sgu_w,
             sgu_b=sgu_b, sc_w=sc_w, w_branch=w_branch, w_o=w_o, final_g=final_g)
    mom = dict(norm_g=m_norm_g, w_in=m_w_in, pool_w=m_pool_w, pool_scale=m_pool_scale, conv_w=m_conv_w,
               conv_b=m_conv_b, conv_ln_g=m_conv_ln_g, conv_ln_b=m_conv_ln_b, sgu_ln_g=m_sgu_ln_g,
               sgu_ln_b=m_sgu_ln_b, sgu_w=m_sgu_w, sgu_b=m_sgu_b, sc_w=m_sc_w, w_branch=m_w_branch, w_o=m_w_o,
               final_g=m_final_g)
    var = dict(norm_g=v_norm_g, w_in=v_w_in, pool_w=v_pool_w, pool_scale=v_pool_scale, conv_w=v_conv_w,
               conv_b=v_conv_b, conv_ln_g=v_conv_ln_g, conv_ln_b=v_conv_ln_b, sgu_ln_g=v_sgu_ln_g,
               sgu_ln_b=v_sgu_ln_b, sgu_w=v_sgu_w, sgu_b=v_sgu_b, sc_w=v_sc_w, w_branch=v_w_branch, w_o=v_w_o,
               final_g=v_final_g)

    nb, seq, _ = x.shape
    n = nb * seq
    core = lax.axis_index("c").astype(jnp.int32).reshape(1)

    win_s = w_in.astype(BF16).reshape(DEPTH, 2, D_MODEL // 2, SHARD_COLS)
    wbr_s = w_branch.astype(BF16)
    wo_s = w_o.astype(BF16)
    cs_s = _pack_cs(conv_w, sc_w)

    place = jnp.stack([2 * lax.axis_index("x") + lax.axis_index("y"), lax.axis_index("c")]).astype(jnp.int32)

    def layer_weights(win_f, wbr_f, wo_f, cs_f):
        cs_f = cs_f.reshape(N_CHIPS, CS_ROWS, GROUP_W)
        return dict(win=win_f.reshape(N_CHIPS, D_MODEL, SHARD_COLS),
                    wbr=wbr_f.reshape(N_CHIPS, N_BRANCH, BRANCH_W, BR_SHARD).transpose(1, 2, 0, 3).reshape(
                        N_BRANCH, BRANCH_W, D_MODEL),
                    wo=wo_f.reshape(D_MODEL, D_MODEL),
                    conv_w=cs_f[:, :CONV_ROWS], sc_w=cs_f[:, CS_ROWS // 2:CS_ROWS // 2 + SHORT_ROWS])

    def mixer_weights(l, gathered):
        row = lambda a: a[l].reshape(1, BRANCH_W)
        bias = jnp.repeat(jnp.swapaxes(sgu_b[l], 0, 1), GROUP_W, axis=1)
        return dict(pool_w=pool_w[l].astype(BF16), pool_scale=row(pool_scale), conv_w=gathered["conv_w"],
                    conv_b=row(conv_b), conv_ln_g=row(conv_ln_g), conv_ln_b=row(conv_ln_b), sgu_ln_g=row(sgu_ln_g),
                    sgu_ln_b=row(sgu_ln_b), sgu_w=sgu_w[l], sgu_bias=bias, sc_w=gathered["sc_w"])

    lw, mw = [None, None], [None, None]

    xs, projs, hs, zs, mgs, caccs = [x.reshape(n, D_MODEL)], [], [], [], [], []
    for l in range(DEPTH):
        first = l == 0
        if first:
            (proj, h), (win_f0,) = _inproj_fwd_own(xs[0], norm_g[0:1], win_s[0].reshape(D_MODEL, SHARD_COLS), place,
                                                   ride=_gather_layer0_halves(win_s))
            (win_f0,) = _exchange_call("forward_layer0_halves", _forward_layer0_halves(win_f0))
            ride = _join(_gather_whole([wbr_s, wo_s, cs_s], 0), _gather_layer1_first_half(win_s))
            (proj,), rode = _inproj_fwd_rest(h, win_f0.reshape(N_CHIPS, D_MODEL, SHARD_COLS), proj, place, ride=ride)
            lw[0] = layer_weights(win_f0, *rode[:3])
            mw[0] = mixer_weights(0, lw[0])
            win_f1 = rode[3]
        else:
            (proj, h), _ = _inproj_fwd(xs[l], norm_g[l:l + 1], lw[1]["win"], l)
        (z3, cacc), rode = _mixers_fwd(proj.reshape(nb, seq, IN_COLS), mw[l], l,
                                       ride=_gather_layer1_second_half(win_s, win_f1) if first else None)
        caccs.append(cacc)
        if first:
            win_f1 = rode[0]
        z = z3.reshape(n, N_BRANCH * BRANCH_W)
        (x_new, merged), rode = _merge_fwd(z, proj, xs[l], lw[l]["wbr"], lw[l]["wo"], l,
                                           ride=_gather_whole([wbr_s, wo_s, cs_s], 1) if first else None)
        if first:
            lw[1] = layer_weights(win_f1, *rode)
            mw[1] = mixer_weights(1, lw[1])
        projs.append(proj)
        hs.append(h)
        zs.append(z)
        mgs.append(merged)
        xs.append(x_new)

    dx, loss_part, g_final = _loss_head(xs[DEPTH], final_g.reshape(1, D_MODEL), loss_target.reshape(n, D_MODEL))

    def chip_major(gwbr):
        return gwbr.reshape(N_BRANCH, BRANCH_W, N_CHIPS, BR_SHARD).transpose(2, 0, 1, 3)

    def as_halves(arrays, first_index=0):
        return [g.reshape((N_CHIPS, 2) + s) for g, s in zip(arrays, HALF_SHAPES[first_index:])]

    def add_halves(grads, others, l, first_index=0):
        return [_add_halves(g, o, core, f"add_halves_l{l}_{first_index + i}")
                for i, (g, o) in enumerate(zip(grads, others))]

    per_layer, parts = {}, [None] * DEPTH
    (dz, dproj, gwo, gwbr), _ = _merge_bwd(dx, mgs[1], zs[1], projs[1], lw[1]["wbr"], lw[1]["wo"], 1)
    dproj3, gm, _ = _mixers_bwd(projs[1].reshape(nb, seq, IN_COLS), dz.reshape(nb, seq, N_BRANCH * BRANCH_W), caccs[1],
                                dproj.reshape(nb, seq, IN_COLS), mw[1], 1)
    dproj = dproj3.reshape(n, IN_COLS)
    gwin, _ = _inproj_bwd_w(hs[1], dproj, 1)
    gcs = _pack_cs(_shard_major_rows(gm["conv_w"]), _shard_major_rows(gm["sc_w"]))
    grads1 = as_halves([gwin, chip_major(gwbr), gwo, gcs])
    (dx, g_norm), others1 = _inproj_bwd_x(dproj, lw[1]["win"], xs[1], norm_g[1:2], dx, 1, ride=_swap_halves(grads1))
    gm["norm_g"] = g_norm
    per_layer[1] = gm
    sums1 = add_halves(grads1, others1, 1)
    (dz, dproj, gwo, gwbr), scattered1 = _merge_bwd(dx, mgs[0], zs[0], projs[0], lw[0]["wbr"], lw[0]["wo"], 0,
                                                    ride=_scatter_chip_sums(sums1))
    dproj3, gm, parts[1] = _mixers_bwd(projs[0].reshape(nb, seq, IN_COLS), dz.reshape(nb, seq, N_BRANCH * BRANCH_W),
                                       caccs[0], dproj.reshape(nb, seq, IN_COLS), mw[0], 0,
                                       ride=_share_halves(scattered1))
    dproj = dproj3.reshape(n, IN_COLS)
    gcs = _pack_cs(_shard_major_rows(gm["conv_w"]), _shard_major_rows(gm["sc_w"]))
    small0 = as_halves([chip_major(gwbr), gwo, gcs], first_index=1)

    per_layer[0] = dict(gm, norm_g=jnp.zeros((1, D_MODEL), F32))

    def stack(k, shape):
        return jnp.stack([per_layer[l][k] for l in range(DEPTH)]).reshape(shape)

    g_small = {"final_g": g_final.reshape(D_MODEL), "norm_g": stack("norm_g", (DEPTH, D_MODEL)),
               "pool_w": stack("pool_w", pool_w.shape), "sgu_w": stack("sgu_w", sgu_w.shape),
               "sgu_b": jnp.swapaxes(stack("sgu_b", (DEPTH, CHUNK, GROUP_W))[:, :, :N_GROUPS], 1, 2)}
    for k in ("pool_scale", "conv_b", "conv_ln_g", "conv_ln_b", "sgu_ln_g", "sgu_ln_b"):
        g_small[k] = stack(k, (DEPTH, BRANCH_W))
    small_part = _pack_small(g_small, jnp.broadcast_to(loss_part, (8, GROUP_W)))

    gwin, rode = _inproj_bwd_w(hs[0], dproj, 0, ride=_join(_swap_halves(small0), _swap_whole(small_part)))
    others_small0, small_other = rode[:3], rode[3]
    small_chip = _add2(small_part, small_other, "add_small_sibling")
    (gwin_h,) = as_halves([gwin])

    def flat(a, i):
        rows, cols = 2 * HALF_SHAPES[i][0], HALF_SHAPES[i][1]
        return a.reshape(a.shape[0], rows, cols)

    names4 = ["w_in", "w_branch", "w_o", "conv_sc"]

    def packed4(t):
        return [t["w_in"], t["w_branch"], t["w_o"], _pack_cs(t["conv_w"], t["sc_w"])]

    w4, m4, v4 = ([flat(a, i) for i, a in enumerate(packed4(t))] for t in (w, mom, var))

    def sharded_update(i, layer, prev, ride=None):
        return _adamw_sharded(w4[i], m4[i], v4[i], flat(parts[layer][i], i), layer, prev, names4[i], ride=ride)

    updated = [sharded_update(i, 1, None)[0] for i in range(4)]

    (other_win0,) = _exchange_call("swap_halves_l0_w_in", _swap_halves([gwin_h]))
    sums0 = add_halves([gwin_h], [other_win0], 0) + add_halves(small0, others_small0, 0, first_index=1)
    (dx, g_norm0), rode = _inproj_bwd_x(dproj, lw[0]["win"], xs[0], norm_g[0:1], dx, 0,
                                        ride=_join(_scatter_chip_sums(sums0), _spread_chip_sums(small_chip)))
    scattered0, small_chips = rode[:4], rode[4]
    grad_x = dx.reshape(nb, seq, D_MODEL)

    g_norm0, parts[0] = _allreduce_small(g_norm0.reshape(D_MODEL // GROUP_W, GROUP_W), _share_halves(scattered0))
    g_packed = lax.dynamic_update_slice(_sum_chips(small_chips, "sum_small_chips"), g_norm0, (0, 0))
    loss = g_packed[g_packed.shape[0] - 8, 0]
    d_packed, m_packed, v_packed = _adamw_packed(_pack_small(w), _pack_small(mom), _pack_small(var), g_packed)
    grads, deltas, new_m, new_v = {}, {}, {}, {}
    for tree, packed in ((grads, g_packed), (deltas, d_packed), (new_m, m_packed), (new_v, v_packed)):
        tree.update(_unpack_small(packed, w))

    for i, name in enumerate(names4):
        res, _ = sharded_update(i, 0, updated[i])
        for tree, r in zip((grads, deltas, new_m, new_v), res):
            if name == "conv_sc":
                tree["conv_w"] = r[:, :CONV_K]
                tree["sc_w"] = r[:, CS_ROWS // 2:CS_ROWS // 2 + SHORT_K]
            else:
                tree[name] = r.reshape(w[name].shape)

    return (loss, grad_x, *[grads[k] for k in WEIGHTS], *[deltas[k] for k in WEIGHTS],
            *[new_m[k] for k in WEIGHTS], *[new_v[k] for k in WEIGHTS])
```

```python
import functools

import jax
import jax.numpy as jnp
from jax import lax
from jax.experimental import pallas as pl
from jax.experimental.pallas import tpu as pltpu

F32 = jnp.float32
BF16 = jnp.bfloat16

D_MODEL = 1024
DEPTH = 2
N_BRANCH = 4
BRANCH_W = 512
N_GROUPS = 4
GROUP_W = 128
POOL_WINDOWS = (2, 4, 8, 16)
CONV_K = 31
SHORT_K = 3
CHUNK = 128
N_PIECES = 12
PIECE_COLS = N_PIECES * BRANCH_W
IN_COLS = PIECE_COLS + N_BRANCH * D_MODEL
N_CHIPS = 4
SHARD_COLS = IN_COLS // N_CHIPS
BR_SHARD = D_MODEL // N_CHIPS
COL_TILE = SHARD_COLS // 2
N_COL_TILES = IN_COLS // COL_TILE
GATE_BLOCK0 = PIECE_COLS // D_MODEL
RMS_EPS = 1e-6
LN_EPS = 1e-5
HALO = 32
CONV_ROWS = 32
SHORT_ROWS = 8
CS_ROWS = 64

ADAM_LR = 0.001
ADAM_B1 = 0.9
ADAM_B2 = 0.999
ADAM_EPS = 1e-08
ADAM_WD = 0.01
ADAM_STEP = 10

VMEM_LIMIT = 60 * 1024 * 1024
MESH_ID = pl.DeviceIdType.MESH
ANY = pl.BlockSpec(memory_space=pl.ANY)
VMEM_WHOLE = pl.BlockSpec(memory_space=pltpu.VMEM)

(P_X, P_GATE, C_A, C_B, C_GATE, G_U, G_V, G_GATE, S_B, S_C, S_X, S_GATE) = range(N_PIECES)


def _params(*sem):
    return pltpu.CompilerParams(dimension_semantics=sem, vmem_limit_bytes=VMEM_LIMIT)


def _sigmoid(v):
    return 0.5 * jnp.tanh(0.5 * v) + 0.5


def _silu(v):
    return v * _sigmoid(v)


def _silu_pair(v):
    s = _sigmoid(v)
    return v * s, s * (1.0 + v * (1.0 - s))


def _dot(a, b):
    return jnp.dot(a, b, preferred_element_type=F32)


def _dot_nt(a, b):
    return lax.dot_general(a, b, (((1,), (1,)), ((), ())), preferred_element_type=F32)


def _dot_tn(a, b):
    return lax.dot_general(a, b, (((0,), (0,)), ((), ())), preferred_element_type=F32)


def _rowsum(v):
    return jnp.sum(v, axis=0, keepdims=True)


def _lanemean(v):
    return jnp.mean(v, axis=-1, keepdims=True)


def _tile(n, want, mult=8):
    t = max(1, min(n, want))
    while n % t or (t % mult and t != n):
        t -= 1
    return t


def _place():
    x, y, c = lax.axis_index("x"), lax.axis_index("y"), lax.axis_index("c")
    chip = 2 * x + y
    peers = [(1 - x, y), (x, 1 - y), (1 - x, 1 - y)]
    return x, y, c, chip, peers


def _remote(src, dst, send_sem, recv_sem, dev):
    return pltpu.make_async_remote_copy(src_ref=src, dst_ref=dst, send_sem=send_sem, recv_sem=recv_sem,
                                        device_id=dev, device_id_type=MESH_ID)


class _Exchange:
    def __init__(self, inputs, out_shapes, plan, n_remote, n_local=0, aliases=None):
        self.inputs = list(inputs)
        self.out_shapes = list(out_shapes)
        self.plan = plan
        self.n_remote = n_remote
        self.n_local = n_local
        self.aliases = dict(aliases or {})

    def copies(self, in_refs, out_refs, send_sems, recv_sems, loc_sems):
        remote, local = self.plan(in_refs, out_refs)
        assert len(remote) == self.n_remote and len(local) == self.n_local
        cps = [_remote(s, d, send_sems.at[t], recv_sems.at[t], dev) for t, (s, d, dev) in enumerate(remote)]
        cps += [pltpu.make_async_copy(s, d, loc_sems.at[t]) for t, (s, d) in enumerate(local)]
        return cps

    def sem_shapes(self):
        return [pltpu.SemaphoreType.DMA((max(self.n_remote, 1),)), pltpu.SemaphoreType.DMA((max(self.n_remote, 1),)),
                pltpu.SemaphoreType.DMA((max(self.n_local, 1),))]


def _exchange_call(name, ex):
    n_ci = len(ex.inputs)

    def body(*refs):
        cins, couts = refs[:n_ci], refs[n_ci:n_ci + len(ex.out_shapes)]
        cps = ex.copies(cins, couts, *refs[n_ci + len(ex.out_shapes):])
        for cp in cps:
            cp.start()
        for cp in cps:
            cp.wait()

    return pl.pallas_call(
        body,
        name=name,
        in_specs=[ANY] * n_ci,
        out_specs=[ANY] * len(ex.out_shapes),
        out_shape=ex.out_shapes,
        scratch_shapes=ex.sem_shapes(),
        input_output_aliases=ex.aliases,
    )(*ex.inputs)


def _host_call(body, *, name, grid, in_specs, out_specs, out_shape, args, scratch_shapes=(), aliases=None, ride=None):
    n_in, n_out, n_scr = len(in_specs), len(out_specs), len(scratch_shapes)
    sem = ("arbitrary",) * len(grid)
    aliases = dict(aliases or {})
    if ride is None:
        outs = pl.pallas_call(body, name=name, grid=grid, in_specs=list(in_specs), out_specs=list(out_specs),
                              out_shape=list(out_shape), scratch_shapes=list(scratch_shapes),
                              input_output_aliases=aliases, compiler_params=_params(*sem))(*args)
        return list(outs), []
    n_ci, n_co = len(ride.inputs), len(ride.out_shapes)

    def full_body(*refs):
        ins, cins = refs[:n_in], refs[n_in:n_in + n_ci]
        o0 = n_in + n_ci
        outs, couts = refs[o0:o0 + n_out], refs[o0 + n_out:o0 + n_out + n_co]
        s0 = o0 + n_out + n_co
        scr, sems = refs[s0:s0 + n_scr], refs[s0 + n_scr:]
        first = functools.reduce(jnp.logical_and, [pl.program_id(d) == 0 for d in range(len(grid))])
        last = functools.reduce(jnp.logical_and, [pl.program_id(d) == grid[d] - 1 for d in range(len(grid))])

        @pl.when(first)
        def _():
            for cp in ride.copies(cins, couts, *sems):
                cp.start()

        body(*ins, *outs, *scr)

        @pl.when(last)
        def _():
            for cp in ride.copies(cins, couts, *sems):
                cp.wait()

    for ci, co in ride.aliases.items():
        aliases[n_in + ci] = n_out + co
    outs = pl.pallas_call(
        full_body, name=name, grid=grid, in_specs=list(in_specs) + [ANY] * n_ci,
        out_specs=list(out_specs) + [ANY] * n_co, out_shape=list(out_shape) + ride.out_shapes,
        scratch_shapes=list(scratch_shapes) + ride.sem_shapes(), input_output_aliases=aliases,
        compiler_params=_params(*sem))(*args, *ride.inputs)
    return list(outs[:n_out]), list(outs[n_out:])


def _allgather_layer0(shards):
    na = len(shards)

    def body(*refs):
        ins, outs = refs[:na], refs[na:2 * na]
        send_sems, recv_sems, fsend_sems, frecv_sems, loc_sems = refs[2 * na:]
        x, y, c, chip, peers = _place()
        sib = (x, y, 1 - c)
        locs = [pltpu.make_async_copy(ins[a].at[0], outs[a].at[chip], loc_sems.at[a]) for a in range(na)]
        for cp in locs:
            cp.start()
        pending = []
        for k, (px, py) in enumerate(peers):
            for a in range(na):
                cp = _remote(ins[a].at[0, c], outs[a].at[chip, c], send_sems.at[k * na + a],
                             recv_sems.at[k * na + a], (px, py, c))
                cp.start()
                pending.append(cp)
        for k, (px, py) in enumerate(peers):
            pchip = 2 * px + py
            for a in range(na):
                slab = outs[a].at[pchip, c]
                _remote(slab, slab, send_sems.at[k * na + a], recv_sems.at[k * na + a], (px, py, c)).wait_recv()
                cp = _remote(slab, slab, fsend_sems.at[k * na + a], frecv_sems.at[k * na + a], sib)
                cp.start()
                pending.append(cp)
        for k, (px, py) in enumerate(peers):
            pchip = 2 * px + py
            for a in range(na):
                slab = outs[a].at[pchip, 1 - c]
                _remote(slab, slab, fsend_sems.at[k * na + a], frecv_sems.at[k * na + a], sib).wait_recv()
        for cp in pending:
            cp.wait_send()
        for cp in locs:
            cp.wait()

    return pl.pallas_call(
        body,
        name="allgather_layer0",
        in_specs=[ANY] * na,
        out_specs=[ANY] * na,
        out_shape=[jax.ShapeDtypeStruct((N_CHIPS,) + a.shape[1:], a.dtype) for a in shards],
        scratch_shapes=[pltpu.SemaphoreType.DMA((3 * na,))] * 4 + [pltpu.SemaphoreType.DMA((na,))],
    )(*shards)


def _gather_layer1_first_half(win_s):
    def plan(cins, couts):
        _, _, c, chip, peers = _place()
        (src,), (dst,) = cins, couts
        return ([(src.at[1, 0], dst.at[chip, 0], (px, py, c)) for px, py in peers],
                [(src.at[1], dst.at[chip])])

    return _Exchange([win_s], [jax.ShapeDtypeStruct((N_CHIPS,) + win_s.shape[1:], win_s.dtype)], plan, 3, 1)


def _gather_layer1_second_half(win_s, win_f1):
    def plan(cins, couts):
        _, _, c, chip, peers = _place()
        src, (dst,) = cins[0], couts
        return [(src.at[1, 1], dst.at[chip, 1], (px, py, c)) for px, py in peers], []

    return _Exchange([win_s, win_f1], [jax.ShapeDtypeStruct(win_f1.shape, win_f1.dtype)], plan, 3, 0, aliases={1: 0})


def _gather_whole(shards, layer):
    na = len(shards)

    def plan(cins, couts):
        _, _, c, chip, peers = _place()
        remote = [(cins[a].at[layer], couts[a].at[chip], (px, py, c)) for a in range(na) for px, py in peers]
        return remote, [(cins[a].at[layer], couts[a].at[chip]) for a in range(na)]

    return _Exchange(shards, [jax.ShapeDtypeStruct((N_CHIPS,) + a.shape[1:], a.dtype) for a in shards], plan,
                     3 * na, na)


def _join(a, b):
    n_in, n_out = len(a.inputs), len(a.out_shapes)

    def plan(cins, couts):
        remote_a, local_a = a.plan(cins[:n_in], couts[:n_out])
        remote_b, local_b = b.plan(cins[n_in:], couts[n_out:])
        return remote_a + remote_b, local_a + local_b

    aliases = dict(a.aliases)
    aliases.update({n_in + i: n_out + o for i, o in b.aliases.items()})
    return _Exchange(a.inputs + b.inputs, a.out_shapes + b.out_shapes, plan, a.n_remote + b.n_remote,
                     a.n_local + b.n_local, aliases)


def _swap_halves(grads):
    def plan(cins, couts):
        x, y, c, _, _ = _place()
        return [(g.at[:, 1 - c], r, (x, y, 1 - c)) for g, r in zip(cins, couts)], []

    return _Exchange(grads, [jax.ShapeDtypeStruct(g.shape[:1] + g.shape[2:], g.dtype) for g in grads], plan,
                     len(grads))


def _scatter_chip_sums(sums):
    na = len(sums)

    def plan(cins, couts):
        _, _, c, chip, peers = _place()
        remote = [(cins[a].at[2 * px + py], couts[a].at[chip, c], (px, py, c)) for a in range(na) for px, py in peers]
        return remote, [(cins[a].at[chip], couts[a].at[chip, c]) for a in range(na)]

    return _Exchange(sums, [jax.ShapeDtypeStruct((N_CHIPS, 2) + s.shape[1:], s.dtype) for s in sums], plan,
                     3 * na, na)


def _share_halves(parts):
    na = len(parts)

    def plan(cins, couts):
        x, y, c, _, _ = _place()
        return [(p.at[:, c], p.at[:, c], (x, y, 1 - c)) for p in couts], []

    return _Exchange(parts, [jax.ShapeDtypeStruct(p.shape, p.dtype) for p in parts], plan, na, 0,
                     aliases={a: a for a in range(na)})


def _allreduce_small(packed, ride):
    rows = packed.shape[0]
    n_ci, n_co = len(ride.inputs), len(ride.out_shapes)

    def body(*refs):
        sg_ref, cins = refs[0], refs[1:1 + n_ci]
        res_ref, couts = refs[1 + n_ci], refs[2 + n_ci:2 + n_ci + n_co]
        sib_ref, cs_ref, rem_ref, s1_send, s1_recv, s2_send, s2_recv = refs[2 + n_ci + n_co:9 + n_ci + n_co]
        riding = ride.copies(cins, couts, *refs[9 + n_ci + n_co:])
        for cp in riding:
            cp.start()
        x, y, c, chip, peers = _place()
        cp = _remote(sg_ref, sib_ref, s1_send, s1_recv, (x, y, 1 - c))
        cp.start()
        cp.wait()
        cs_ref[...] = sg_ref[...] + sib_ref[...]
        rem_ref[chip] = cs_ref[...]
        cps = [_remote(cs_ref, rem_ref.at[chip], s2_send.at[k], s2_recv.at[chip], (px, py, c))
               for k, (px, py) in enumerate(peers)]
        for cp in cps:
            cp.start()
        for k, (px, py) in enumerate(peers):
            pchip = 2 * px + py
            _remote(cs_ref, rem_ref.at[pchip], s2_send.at[k], s2_recv.at[pchip], (px, py, c)).wait_recv()
        for cp in cps:
            cp.wait_send()
        res_ref[...] = ((rem_ref[0] + rem_ref[1]) + rem_ref[2]) + rem_ref[3]
        for cp in riding:
            cp.wait()

    outs = pl.pallas_call(
        body,
        name="allreduce_small",
        in_specs=[VMEM_WHOLE] + [ANY] * n_ci,
        out_specs=[VMEM_WHOLE] + [ANY] * n_co,
        out_shape=[jax.ShapeDtypeStruct((rows, GROUP_W), F32)] + ride.out_shapes,
        scratch_shapes=[
            pltpu.VMEM((rows, GROUP_W), F32),
            pltpu.VMEM((rows, GROUP_W), F32),
            pltpu.VMEM((N_CHIPS, rows, GROUP_W), F32),
            pltpu.SemaphoreType.DMA,
            pltpu.SemaphoreType.DMA,
            pltpu.SemaphoreType.DMA((3,)),
            pltpu.SemaphoreType.DMA((N_CHIPS,)),
        ] + ride.sem_shapes(),
        input_output_aliases={1 + i: 1 + o for i, o in ride.aliases.items()},
        compiler_params=pltpu.CompilerParams(vmem_limit_bytes=VMEM_LIMIT),
    )(packed, *ride.inputs)
    return outs[0], list(outs[1:])


def _swap_whole(packed):
    def plan(cins, couts):
        x, y, c, _, _ = _place()
        return [(cins[0], couts[0], (x, y, 1 - c))], []

    return _Exchange([packed], [jax.ShapeDtypeStruct(packed.shape, packed.dtype)], plan, 1)


def _spread_chip_sums(chip_sum):
    def plan(cins, couts):
        _, _, c, chip, peers = _place()
        return ([(cins[0], couts[0].at[chip], (px, py, c)) for px, py in peers], [(cins[0], couts[0].at[chip])])

    return _Exchange([chip_sum], [jax.ShapeDtypeStruct((N_CHIPS,) + chip_sum.shape, chip_sum.dtype)], plan, 3, 1)


def _add2(a, b, name):
    def body(a_ref, b_ref, o_ref):
        o_ref[...] = a_ref[...] + b_ref[...]

    return pl.pallas_call(body, name=name, in_specs=[VMEM_WHOLE] * 2, out_specs=VMEM_WHOLE,
                          out_shape=jax.ShapeDtypeStruct(a.shape, F32))(a, b)


def _sum_chips(parts, name):
    def body(p_ref, o_ref):
        o_ref[...] = ((p_ref[0] + p_ref[1]) + p_ref[2]) + p_ref[3]

    return pl.pallas_call(body, name=name, in_specs=[VMEM_WHOLE], out_specs=VMEM_WHOLE,
                          out_shape=jax.ShapeDtypeStruct(parts.shape[1:], F32))(parts)


def _add_halves(grad, other, core, name):
    n_shards, _, rows, cols = grad.shape
    tr = _tile(rows, max(16, (1 << 22) // (4 * cols) // 16 * 16), mult=16)

    def body(core_ref, g_ref, o_ref, out_ref):
        del core_ref
        out_ref[...] = (g_ref[...] + o_ref[...]).astype(BF16)

    return pl.pallas_call(
        body,
        name=name,
        grid_spec=pltpu.PrefetchScalarGridSpec(
            num_scalar_prefetch=1,
            grid=(n_shards, rows // tr),
            in_specs=[
                pl.BlockSpec((None, None, tr, cols), lambda s, i, core_ref: (s, core_ref[0], i, 0)),
                pl.BlockSpec((None, tr, cols), lambda s, i, core_ref: (s, i, 0)),
            ],
            out_specs=pl.BlockSpec((None, tr, cols), lambda s, i, core_ref: (s, i, 0)),
        ),
        out_shape=jax.ShapeDtypeStruct((n_shards, rows, cols), BF16),
        compiler_params=_params("arbitrary", "arbitrary"),
    )(core, grad, other)


def _inproj_fwd(x2, g_row, w_l, layer, ride=None):
    n = x2.shape[0]
    tm = _tile(n, 2048)

    def body(x_ref, g_ref, w_ref, p_ref, h_ref):
        @pl.when(pl.program_id(1) == 0)
        def _():
            xv = x_ref[...]
            r = lax.rsqrt(_lanemean(xv * xv) + RMS_EPS)
            h_ref[...] = (xv * r * g_ref[...]).astype(BF16)

        p_ref[...] = _dot(h_ref[...], w_ref[...]).astype(BF16)

    return _host_call(
        body,
        name=f"inproj_fwd_l{layer}",
        grid=(n // tm, N_COL_TILES),
        in_specs=[
            pl.BlockSpec((tm, D_MODEL), lambda i, j: (i, 0)),
            pl.BlockSpec((1, D_MODEL), lambda i, j: (0, 0)),
            pl.BlockSpec((None, D_MODEL, COL_TILE), lambda i, j: (j // 2, 0, j % 2)),
        ],
        out_specs=[
            pl.BlockSpec((tm, COL_TILE), lambda i, j: (i, j)),
            pl.BlockSpec((tm, D_MODEL), lambda i, j: (i, 0)),
        ],
        out_shape=[jax.ShapeDtypeStruct((n, IN_COLS), BF16), jax.ShapeDtypeStruct((n, D_MODEL), BF16)],
        args=[x2, g_row, w_l],
        ride=ride,
    )


def _layer_norm_parts(v, g, b):
    mu = _lanemean(v)
    d = v - mu
    rstd = lax.rsqrt(_lanemean(d * d) + LN_EPS)
    xh = d * rstd
    return xh, rstd, xh * g + b


def _window_sum(ref, slot, weight_row, n_taps, base, t_rows):
    total = None
    for b in range(8):
        group = None
        for a in range((base + n_taps - 1) // 8 + 1):
            k = 8 * a + b - base
            if 0 <= k < n_taps:
                term = weight_row(k) * ref[slot, 8 * a:8 * a + t_rows + 8, :]
                group = term if group is None else group + term
        if group is not None:
            part = group[b:b + t_rows, :]
            total = part if total is None else total + part
    return total


def _tap_grads(acc_ref, pad_ref, shift_ref, dy, ref, slot, n_taps, base, t_rows):
    pad_ref[0:8, :] = jnp.zeros((8, BRANCH_W), F32)
    pad_ref[8:8 + t_rows, :] = dy
    pad_ref[8 + t_rows:16 + t_rows, :] = jnp.zeros((8, BRANCH_W), F32)
    for b in range(8):
        taps = [(a, 8 * a + b - base) for a in range((base + n_taps - 1) // 8 + 1) if 0 <= 8 * a + b - base < n_taps]
        if not taps:
            continue
        shift_ref[...] = pad_ref[pl.ds(8 - b, t_rows + 8), :]
        for a, k in taps:
            prod = shift_ref[...] * ref[slot, 8 * a:8 * a + t_rows + 8, :]
            acc_ref[k] += prod.reshape((t_rows + 8) // 8, 8, BRANCH_W).sum(axis=0)


def _trailing_window_sums(ext_ref, pool_ref, t_rows):
    lo, n = HALO - 16, t_rows + 16
    pool_ref[0:16, :] = jnp.zeros((16, BRANCH_W), F32)
    sums = []
    for j, win in enumerate(POOL_WINDOWS):
        cols = slice(j * GROUP_W, (j + 1) * GROUP_W)
        v = ext_ref[0, lo:lo + n, cols] + ext_ref[0, pl.ds(lo - 1, n), cols]
        width = 2
        while width < win:
            pool_ref[16:16 + n, cols] = v
            v = v + pool_ref[pl.ds(16 - width, n), cols]
            width *= 2
        sums.append(v[16:, :])
    return sums


def _leading_window_sums(nxt_ref, pool_ref, t_rows):
    n = t_rows + 16
    pool_ref[n:n + 16, :] = jnp.zeros((16, BRANCH_W), F32)
    sums = []
    for j, win in enumerate(POOL_WINDOWS):
        cols = slice(j * GROUP_W, (j + 1) * GROUP_W)
        v = nxt_ref[0, 0:n, cols] + nxt_ref[0, pl.ds(1, n), cols]
        width = 2
        while width < win:
            pool_ref[0:n, cols] = v
            v = v + pool_ref[pl.ds(width, n), cols]
            width *= 2
        sums.append(v[0:t_rows, :])
    return sums


def _tril_mask():
    r = lax.broadcasted_iota(jnp.int32, (CHUNK, CHUNK), 0)
    c = lax.broadcasted_iota(jnp.int32, (CHUNK, CHUNK), 1)
    return r >= c


def _lanes(refs4):
    return jnp.concatenate([refs4[g] for g in range(N_CHIPS)], axis=1)


def _mixer_weight_specs():
    def whole(shape):
        nd = len(shape)
        return pl.BlockSpec(shape, lambda b, c: (0,) * nd)

    return [
        whole((N_GROUPS, GROUP_W, GROUP_W)),
        whole((1, BRANCH_W)),
        whole((N_CHIPS, CONV_ROWS, GROUP_W)),
        whole((1, BRANCH_W)),
        whole((1, BRANCH_W)),
        whole((1, BRANCH_W)),
        whole((1, BRANCH_W)),
        whole((1, BRANCH_W)),
        whole((N_GROUPS, CHUNK, CHUNK)),
        whole((CHUNK, BRANCH_W)),
        whole((N_CHIPS, SHORT_ROWS, GROUP_W)),
    ]


def _mixer_weight_args(mw):
    return [mw["pool_w"], mw["pool_scale"], mw["conv_w"], mw["conv_b"], mw["conv_ln_g"], mw["conv_ln_b"],
            mw["sgu_ln_g"], mw["sgu_ln_b"], mw["sgu_w"], mw["sgu_bias"], mw["sc_w"]]


def _mixers_fwd(proj3, mw, layer, ride=None):
    nb, seq, _ = proj3.shape
    t_rows = _tile(seq, 512)
    nc = seq // t_rows
    hb = t_rows // HALO

    def body(cur_ref, halo_ref, pw_ref, ps_ref, cw_ref, cb_ref, clg_ref, clb_ref, slg_ref, slb_ref, sw_ref,
             sbias_ref, scw_ref, z_ref, cacc_ref, ext_ref, pool_ref):
        c = pl.program_id(1)
        has_prev = c > 0
        row = lax.broadcasted_iota(jnp.int32, (t_rows, 1), 0)
        tpos = (c * t_rows + row + 1).astype(F32)

        def cur(k):
            return cur_ref[:, k * BRANCH_W:(k + 1) * BRANCH_W].astype(F32)

        def hal(k):
            return halo_ref[:, k * BRANCH_W:(k + 1) * BRANCH_W].astype(F32)

        def put_ext(slot, halo_val, cur_val):
            ext_ref[slot, 0:HALO, :] = jnp.where(has_prev, halo_val, 0.0)
            ext_ref[slot, HALO:HALO + t_rows, :] = cur_val
            ext_ref[slot, HALO + t_rows:HALO + t_rows + 8, :] = jnp.zeros((8, BRANCH_W), F32)

        px = cur(P_X)
        put_ext(0, hal(P_X), px)
        mixed = []
        window_sums = _trailing_window_sums(ext_ref, pool_ref, t_rows)
        for j, win in enumerate(POOL_WINDOWS):
            cols = slice(j * GROUP_W, (j + 1) * GROUP_W)
            pooled = window_sums[j] / jnp.minimum(tpos, float(win)) - px[:, cols]
            mixed.append(_dot(pooled.astype(BF16), pw_ref[j]))
        z_pool = jnp.concatenate(mixed, axis=1) * ps_ref[...] * _silu(cur(P_GATE))
        z_ref[:, 0:BRANCH_W] = z_pool.astype(BF16)

        put_ext(1, hal(C_A) * _sigmoid(hal(C_B)), cur(C_A) * _sigmoid(cur(C_B)))
        cw = _lanes(cw_ref)
        acc = cb_ref[...] + _window_sum(ext_ref, 1, lambda k: cw[k:k + 1, :], CONV_K, HALO - (CONV_K - 1), t_rows)
        cacc_ref[...] = acc
        _, _, ln = _layer_norm_parts(acc, clg_ref[...], clb_ref[...])
        z_ref[:, BRANCH_W:2 * BRANCH_W] = (_silu(ln) * _silu(cur(C_GATE))).astype(BF16)

        _, _, v = _layer_norm_parts(cur(G_V), slg_ref[...], slb_ref[...])
        vb = v.astype(BF16)
        mask = _tril_mask()
        wt = [jnp.where(mask, sw_ref[g], 0.0).astype(BF16) for g in range(N_GROUPS)]
        sp_rows = []
        for sub in range(t_rows // CHUNK):
            rows = slice(sub * CHUNK, (sub + 1) * CHUNK)
            sp_rows.append(jnp.concatenate(
                [_dot(wt[g], vb[rows, g * GROUP_W:(g + 1) * GROUP_W]) for g in range(N_GROUPS)], axis=1)
                + sbias_ref[...])
        sp = jnp.concatenate(sp_rows, axis=0)
        z_ref[:, 2 * BRANCH_W:3 * BRANCH_W] = (cur(G_U) * sp * _silu(cur(G_GATE))).astype(BF16)

        put_ext(2, hal(S_C) * hal(S_X), cur(S_C) * cur(S_X))
        scw = _lanes(scw_ref)
        cv = _window_sum(ext_ref, 2, lambda k: scw[k:k + 1, :], SHORT_K, HALO - (SHORT_K - 1), t_rows)
        z_ref[:, 3 * BRANCH_W:4 * BRANCH_W] = (cur(S_B) * cv * _silu(cur(S_GATE))).astype(BF16)

    return _host_call(
        body,
        name=f"mixers_fwd_l{layer}",
        grid=(nb, nc),
        in_specs=[
            pl.BlockSpec((None, t_rows, PIECE_COLS), lambda b, c: (b, c, 0)),
            pl.BlockSpec((None, HALO, PIECE_COLS), lambda b, c: (b, jnp.maximum(c * hb - 1, 0), 0)),
        ] + _mixer_weight_specs(),
        out_specs=[pl.BlockSpec((None, t_rows, N_BRANCH * BRANCH_W), lambda b, c: (b, c, 0)),
                   pl.BlockSpec((None, t_rows, BRANCH_W), lambda b, c: (b, c, 0))],
        out_shape=[jax.ShapeDtypeStruct((nb, seq, N_BRANCH * BRANCH_W), BF16),
                   jax.ShapeDtypeStruct((nb, seq, BRANCH_W), F32)],
        scratch_shapes=[pltpu.VMEM((3, HALO + t_rows + 8, BRANCH_W), F32),
                        pltpu.VMEM((t_rows + 32, BRANCH_W), F32)],
        args=[proj3, proj3, *_mixer_weight_args(mw)],
        ride=ride,
    )


MIXER_GRADS = ["pool_w", "pool_scale", "conv_w", "conv_b", "conv_ln_g", "conv_ln_b", "sgu_ln_g", "sgu_ln_b", "sgu_w",
               "sgu_b", "sc_w"]


def _mixers_bwd(proj3, dz3, cacc3, dproj3, mw, layer, ride=None):
    nb, seq, _ = proj3.shape
    t_rows = _tile(seq, 512)
    nc = seq // t_rows
    hb = t_rows // HALO

    def body(cur_ref, halo_ref, dz_ref, cacc_ref, pw_ref, ps_ref, cw_ref, cb_ref, clg_ref, clb_ref, slg_ref, slb_ref,
             sw_ref, sbias_ref, scw_ref, dp_in_ref, dp_ref, g_pw, g_ps, g_cw, g_cb, g_clg, g_clb, g_slg,
             g_slb, g_sw, g_sb, g_scw, ext_ref, nxt_ref, sb_acc, cw_acc, scw_acc, pad_ref, shift_ref,
             pool_ref):
        del dp_in_ref
        b = pl.program_id(0)
        r = pl.program_id(1)
        c = nc - 1 - r
        has_prev = c > 0
        row = lax.broadcasted_iota(jnp.int32, (t_rows, 1), 0)
        tpos = (c * t_rows + row + 1).astype(F32)

        @pl.when(jnp.logical_and(b == 0, r == 0))
        def _():
            for ref in (g_pw, g_ps, g_cb, g_clg, g_clb, g_slg, g_slb, g_sw, sb_acc, cw_acc, scw_acc):
                ref[...] = jnp.zeros_like(ref)

        @pl.when(r == 0)
        def _():
            nxt_ref[:, t_rows:t_rows + HALO, :] = jnp.zeros((3, HALO, BRANCH_W), F32)

        def cur(k):
            return cur_ref[:, k * BRANCH_W:(k + 1) * BRANCH_W].astype(F32)

        def hal(k):
            return halo_ref[:, k * BRANCH_W:(k + 1) * BRANCH_W].astype(F32)

        def dzp(k):
            return dz_ref[:, k * BRANCH_W:(k + 1) * BRANCH_W].astype(F32)

        def put_dp(k, val):
            dp_ref[:, k * BRANCH_W:(k + 1) * BRANCH_W] = val.astype(BF16)

        def put_ext(slot, halo_val, cur_val):
            ext_ref[slot, 0:HALO, :] = jnp.where(has_prev, halo_val, 0.0)
            ext_ref[slot, HALO:HALO + t_rows, :] = cur_val
            ext_ref[slot, HALO + t_rows:HALO + t_rows + 8, :] = jnp.zeros((8, BRANCH_W), F32)

        px = cur(P_X)
        put_ext(0, hal(P_X), px)
        pgate = cur(P_GATE)
        dz_pool = dzp(0)
        ps = ps_ref[...]
        pooled, mixed, cnts = [], [], []
        window_sums = _trailing_window_sums(ext_ref, pool_ref, t_rows)
        for j, win in enumerate(POOL_WINDOWS):
            cols = slice(j * GROUP_W, (j + 1) * GROUP_W)
            cnt = jnp.minimum(tpos, float(win))
            pj = window_sums[j] / cnt - px[:, cols]
            cnts.append(cnt)
            pooled.append(pj.astype(BF16))
            mixed.append(_dot(pooled[j], pw_ref[j]))
        mixed = jnp.concatenate(mixed, axis=1)
        pg_silu, pg_dsilu = _silu_pair(pgate)
        put_dp(P_GATE, dz_pool * (mixed * ps) * pg_dsilu)
        d_out = dz_pool * pg_silu
        g_ps[...] += _rowsum(d_out * mixed)
        d_mixed = (d_out * ps).astype(BF16)
        d_pooled = []
        for j in range(N_GROUPS):
            cols = slice(j * GROUP_W, (j + 1) * GROUP_W)
            g_pw[j] += _dot_tn(pooled[j], d_mixed[:, cols])
            dpj = _dot_nt(d_mixed[:, cols], pw_ref[j])
            d_pooled.append(dpj)
            nxt_ref[0, 0:t_rows, cols] = dpj / cnts[j]
        lead_sums = _leading_window_sums(nxt_ref, pool_ref, t_rows)
        put_dp(P_X, jnp.concatenate([lead_sums[j] - d_pooled[j] for j in range(N_GROUPS)], axis=1))

        c_a = cur(C_A)
        sig_b = _sigmoid(cur(C_B))
        put_ext(1, hal(C_A) * _sigmoid(hal(C_B)), c_a * sig_b)
        cw = _lanes(cw_ref)
        xh, rstd, ln = _layer_norm_parts(cacc_ref[...], clg_ref[...], clb_ref[...])
        cgate = cur(C_GATE)
        dz_conv = dzp(1)
        ln_silu, ln_dsilu = _silu_pair(ln)
        cg_silu, cg_dsilu = _silu_pair(cgate)
        put_dp(C_GATE, dz_conv * ln_silu * cg_dsilu)
        d_ln = dz_conv * cg_silu * ln_dsilu
        g_clg[...] += _rowsum(d_ln * xh)
        g_clb[...] += _rowsum(d_ln)
        dxh = d_ln * clg_ref[...]
        dc = rstd * (dxh - _lanemean(dxh) - xh * _lanemean(dxh * xh))
        g_cb[...] += _rowsum(dc)
        nxt_ref[1, 0:t_rows, :] = dc
        _tap_grads(cw_acc, pad_ref, shift_ref, dc, ext_ref, 1, CONV_K, HALO - (CONV_K - 1), t_rows)
        dyg = _window_sum(nxt_ref, 1, lambda i: cw[CONV_K - 1 - i:CONV_K - i, :], CONV_K, 0, t_rows)
        put_dp(C_A, dyg * sig_b)
        put_dp(C_B, dyg * c_a * sig_b * (1.0 - sig_b))

        u = cur(G_U)
        ggate = cur(G_GATE)
        vxh, vrstd, v = _layer_norm_parts(cur(G_V), slg_ref[...], slb_ref[...])
        vb = v.astype(BF16)
        mask = _tril_mask()
        wt = [jnp.where(mask, sw_ref[g], 0.0) for g in range(N_GROUPS)]
        wt_b = [w.astype(BF16) for w in wt]
        wtt_b = [w.T.astype(BF16) for w in wt]
        dz_sgu = dzp(2)
        gg_silu, gg_dsilu = _silu_pair(ggate)
        d_sgu = dz_sgu * gg_silu
        d_sp = d_sgu * u
        d_spb = d_sp.astype(BF16)
        sp_rows, dv_rows = [], []
        for sub in range(t_rows // CHUNK):
            rows = slice(sub * CHUNK, (sub + 1) * CHUNK)
            sp_g, dv_g = [], []
            for g in range(N_GROUPS):
                cols = slice(g * GROUP_W, (g + 1) * GROUP_W)
                sp_g.append(_dot(wt_b[g], vb[rows, cols]))
                g_sw[g] += jnp.where(mask, _dot_nt(d_spb[rows, cols], vb[rows, cols]), 0.0)
                dv_g.append(_dot(wtt_b[g], d_spb[rows, cols]))
            sp_rows.append(jnp.concatenate(sp_g, axis=1) + sbias_ref[...])
            dv_rows.append(jnp.concatenate(dv_g, axis=1))
            sb_acc[...] += d_sp[rows, :]
        sp = jnp.concatenate(sp_rows, axis=0)
        dv = jnp.concatenate(dv_rows, axis=0)
        put_dp(G_GATE, dz_sgu * (u * sp) * gg_dsilu)
        put_dp(G_U, d_sgu * sp)
        g_slg[...] += _rowsum(dv * vxh)
        g_slb[...] += _rowsum(dv)
        dvx = dv * slg_ref[...]
        put_dp(G_V, vrstd * (dvx - _lanemean(dvx) - vxh * _lanemean(dvx * vxh)))

        s_b, s_c, s_x, sgate = cur(S_B), cur(S_C), cur(S_X), cur(S_GATE)
        put_ext(2, hal(S_C) * hal(S_X), s_c * s_x)
        scw = _lanes(scw_ref)
        cv = _window_sum(ext_ref, 2, lambda k: scw[k:k + 1, :], SHORT_K, HALO - (SHORT_K - 1), t_rows)
        dz_sc = dzp(3)
        sg_silu, sg_dsilu = _silu_pair(sgate)
        put_dp(S_GATE, dz_sc * (s_b * cv) * sg_dsilu)
        d_pre = dz_sc * sg_silu
        put_dp(S_B, d_pre * cv)
        dcv = d_pre * s_b
        nxt_ref[2, 0:t_rows, :] = dcv
        _tap_grads(scw_acc, pad_ref, shift_ref, dcv, ext_ref, 2, SHORT_K, HALO - (SHORT_K - 1), t_rows)
        du = _window_sum(nxt_ref, 2, lambda i: scw[SHORT_K - 1 - i:SHORT_K - i, :], SHORT_K, 0, t_rows)
        put_dp(S_C, du * s_x)
        put_dp(S_X, du * s_c)

        nxt_ref[:, t_rows:t_rows + HALO, :] = nxt_ref[:, 0:HALO, :]

        @pl.when(jnp.logical_and(b == nb - 1, r == nc - 1))
        def _():
            lane = lax.broadcasted_iota(jnp.int32, (CHUNK, GROUP_W), 1)
            out = jnp.zeros((CHUNK, GROUP_W), F32)
            for g in range(N_GROUPS):
                col = jnp.sum(sb_acc[:, g * GROUP_W:(g + 1) * GROUP_W], axis=1, keepdims=True)
                out = jnp.where(lane == g, col, out)
            g_sb[...] = out
            g_cw[...] = jnp.sum(cw_acc[...], axis=1)
            g_scw[...] = jnp.sum(scw_acc[...], axis=1)

    def acc_spec(shape):
        nd = len(shape)
        return pl.BlockSpec(shape, lambda b, r: (0,) * nd)

    acc_shapes = [
        (N_GROUPS, GROUP_W, GROUP_W),
        (1, BRANCH_W),
        (CONV_ROWS, BRANCH_W),
        (1, BRANCH_W),
        (1, BRANCH_W),
        (1, BRANCH_W),
        (1, BRANCH_W),
        (1, BRANCH_W),
        (N_GROUPS, CHUNK, CHUNK),
        (CHUNK, GROUP_W),
        (SHORT_ROWS, BRANCH_W),
    ]
    outs, rode = _host_call(
        body,
        name=f"mixers_bwd_l{layer}",
        grid=(nb, nc),
        in_specs=[
            pl.BlockSpec((None, t_rows, PIECE_COLS), lambda b, r: (b, nc - 1 - r, 0)),
            pl.BlockSpec((None, HALO, PIECE_COLS), lambda b, r: (b, jnp.maximum((nc - 1 - r) * hb - 1, 0), 0)),
            pl.BlockSpec((None, t_rows, N_BRANCH * BRANCH_W), lambda b, r: (b, nc - 1 - r, 0)),
            pl.BlockSpec((None, t_rows, BRANCH_W), lambda b, r: (b, nc - 1 - r, 0)),
        ] + _mixer_weight_specs() + [ANY],
        out_specs=[pl.BlockSpec((None, t_rows, PIECE_COLS), lambda b, r: (b, nc - 1 - r, 0))]
        + [acc_spec(s) for s in acc_shapes],
        out_shape=[jax.ShapeDtypeStruct(dproj3.shape, BF16)] + [jax.ShapeDtypeStruct(s, F32) for s in acc_shapes],
        scratch_shapes=[
            pltpu.VMEM((3, HALO + t_rows + 8, BRANCH_W), F32),
            pltpu.VMEM((3, t_rows + HALO, BRANCH_W), F32),
            pltpu.VMEM((CHUNK, BRANCH_W), F32),
            pltpu.VMEM((CONV_ROWS, 8, BRANCH_W), F32),
            pltpu.VMEM((SHORT_ROWS, 8, BRANCH_W), F32),
            pltpu.VMEM((t_rows + 16, BRANCH_W), F32),
            pltpu.VMEM((t_rows + 8, BRANCH_W), F32),
            pltpu.VMEM((t_rows + 32, BRANCH_W), F32),
        ],
        aliases={4 + 11: 0},
        args=[proj3, proj3, dz3, cacc3, *_mixer_weight_args(mw), dproj3],
        ride=ride,
    )
    return outs[0], dict(zip(MIXER_GRADS, outs[1:])), rode


def _merge_fwd(z2, proj2, x2, wbr_l, wo_l, layer, ride=None):
    n = x2.shape[0]
    tm = _tile(n, 1024)

    def body(z_ref, gate_ref, x_ref, wbr_ref, wo_ref, xn_ref, mg_ref, acc_ref):
        nbr = pl.program_id(1)
        bo = _dot(z_ref[...], wbr_ref[nbr])
        contrib = _sigmoid(gate_ref[...].astype(F32)) * bo

        @pl.when(nbr == 0)
        def _():
            acc_ref[...] = contrib

        @pl.when(nbr > 0)
        def _():
            acc_ref[...] += contrib

        @pl.when(nbr == N_BRANCH - 1)
        def _():
            mg = acc_ref[...].astype(BF16)
            mg_ref[...] = mg
            xn_ref[...] = x_ref[...] + _dot(mg, wo_ref[...])

    return _host_call(
        body,
        name=f"merge_fwd_l{layer}",
        grid=(n // tm, N_BRANCH),
        in_specs=[
            pl.BlockSpec((tm, BRANCH_W), lambda i, b: (i, b)),
            pl.BlockSpec((tm, D_MODEL), lambda i, b: (i, GATE_BLOCK0 + b)),
            pl.BlockSpec((tm, D_MODEL), lambda i, b: (i, 0)),
            pl.BlockSpec((N_BRANCH, BRANCH_W, D_MODEL), lambda i, b: (0, 0, 0)),
            pl.BlockSpec((D_MODEL, D_MODEL), lambda i, b: (0, 0)),
        ],
        out_specs=[
            pl.BlockSpec((tm, D_MODEL), lambda i, b: (i, 0)),
            pl.BlockSpec((tm, D_MODEL), lambda i, b: (i, 0)),
        ],
        out_shape=[jax.ShapeDtypeStruct((n, D_MODEL), F32), jax.ShapeDtypeStruct((n, D_MODEL), BF16)],
        scratch_shapes=[pltpu.VMEM((tm, D_MODEL), F32)],
        args=[z2, proj2, x2, wbr_l, wo_l],
        ride=ride,
    )


def _merge_fwd_head(z2, proj2, x2, wbr_l, wo_l, g_row, tgt2, layer):
    n = x2.shape[0]
    tm = _tile(n, 1024)

    def body(z_ref, gate_ref, x_ref, wbr_ref, wo_ref, g_ref, t_ref, dx_ref, mg_ref, loss_ref, dg_ref, acc_ref):
        i = pl.program_id(0)
        nbr = pl.program_id(1)
        bo = _dot(z_ref[...], wbr_ref[nbr])
        contrib = _sigmoid(gate_ref[...].astype(F32)) * bo

        @pl.when(jnp.logical_and(i == 0, nbr == 0))
        def _():
            loss_ref[...] = jnp.zeros_like(loss_ref)
            dg_ref[...] = jnp.zeros_like(dg_ref)

        @pl.when(nbr == 0)
        def _():
            acc_ref[...] = contrib

        @pl.when(nbr > 0)
        def _():
            acc_ref[...] += contrib

        @pl.when(nbr == N_BRANCH - 1)
        def _():
            mg = acc_ref[...].astype(BF16)
            mg_ref[...] = mg
            xv = x_ref[...] + _dot(mg, wo_ref[...])
            g = g_ref[...]
            r = lax.rsqrt(_lanemean(xv * xv) + RMS_EPS)
            xh = xv * r
            err = xh * g - t_ref[...]
            loss_ref[...] += 0.5 * jnp.sum(_lanemean(err * err), axis=0, keepdims=True)
            dy = err * (1.0 / D_MODEL)
            dg_ref[...] += _rowsum(dy * xh)
            dxh = dy * g
            dx_ref[...] = r * (dxh - xh * _lanemean(dxh * xh))

    outs, _ = _host_call(
        body,
        name=f"merge_fwd_head_l{layer}",
        grid=(n // tm, N_BRANCH),
        in_specs=[
            pl.BlockSpec((tm, BRANCH_W), lambda i, b: (i, b)),
            pl.BlockSpec((tm, D_MODEL), lambda i, b: (i, GATE_BLOCK0 + b)),
            pl.BlockSpec((tm, D_MODEL), lambda i, b: (i, 0)),
            pl.BlockSpec((N_BRANCH, BRANCH_W, D_MODEL), lambda i, b: (0, 0, 0)),
            pl.BlockSpec((D_MODEL, D_MODEL), lambda i, b: (0, 0)),
            pl.BlockSpec((1, D_MODEL), lambda i, b: (0, 0)),
            pl.BlockSpec((tm, D_MODEL), lambda i, b: (i, 0)),
        ],
        out_specs=[
            pl.BlockSpec((tm, D_MODEL), lambda i, b: (i, 0)),
            pl.BlockSpec((tm, D_MODEL), lambda i, b: (i, 0)),
            pl.BlockSpec((1, GROUP_W), lambda i, b: (0, 0)),
            pl.BlockSpec((1, D_MODEL), lambda i, b: (0, 0)),
        ],
        out_shape=[jax.ShapeDtypeStruct((n, D_MODEL), F32), jax.ShapeDtypeStruct((n, D_MODEL), BF16),
                   jax.ShapeDtypeStruct((1, GROUP_W), F32), jax.ShapeDtypeStruct((1, D_MODEL), F32)],
        scratch_shapes=[pltpu.VMEM((tm, D_MODEL), F32)],
        args=[z2, proj2, x2, wbr_l, wo_l, g_row, tgt2],
    )
    return outs


def _merge_bwd(dout2, merged2, z2, proj2, wbr_l, wo_l, layer, ride=None):
    n = dout2.shape[0]
    tm = _tile(n, 1024)
    nt = n // tm

    def body(do_ref, mg_ref, z_ref, gate_ref, wbr_ref, wo_ref, dz_ref, dg_ref, gwo_ref, gwbr_ref, dm_ref,
             awo_ref, awbr_ref):
        i = pl.program_id(0)
        nbr = pl.program_id(1)

        @pl.when(nbr == 0)
        def _():
            do_b = do_ref[...].astype(BF16)
            dm_ref[...] = _dot_nt(do_b, wo_ref[...])
            gw = _dot_tn(mg_ref[...], do_b)

            @pl.when(i == 0)
            def _():
                awo_ref[...] = gw

            @pl.when(i > 0)
            def _():
                awo_ref[...] += gw

        zt = z_ref[...]
        wbr = wbr_ref[nbr]
        bo = _dot(zt, wbr)
        gt = _sigmoid(gate_ref[...].astype(F32))
        dm = dm_ref[...]
        dbo = (dm * gt).astype(BF16)
        dg_ref[...] = (dm * bo * gt * (1.0 - gt)).astype(BF16)
        dz_ref[...] = _dot_nt(dbo, wbr).astype(BF16)
        gw = _dot_tn(zt, dbo)

        @pl.when(i == 0)
        def _():
            awbr_ref[nbr] = gw

        @pl.when(i > 0)
        def _():
            awbr_ref[nbr] += gw

        @pl.when(jnp.logical_and(i == nt - 1, nbr == N_BRANCH - 1))
        def _():
            pltpu.sync_copy(awo_ref, gwo_ref)
            pltpu.sync_copy(awbr_ref, gwbr_ref)

    return _host_call(
        body,
        name=f"merge_bwd_l{layer}",
        grid=(nt, N_BRANCH),
        in_specs=[
            pl.BlockSpec((tm, D_MODEL), lambda i, b: (i, 0)),
            pl.BlockSpec((tm, D_MODEL), lambda i, b: (i, 0)),
            pl.BlockSpec((tm, BRANCH_W), lambda i, b: (i, b)),
            pl.BlockSpec((tm, D_MODEL), lambda i, b: (i, GATE_BLOCK0 + b)),
            pl.BlockSpec((N_BRANCH, BRANCH_W, D_MODEL), lambda i, b: (0, 0, 0)),
            pl.BlockSpec((D_MODEL, D_MODEL), lambda i, b: (0, 0)),
        ],
        out_specs=[
            pl.BlockSpec((tm, BRANCH_W), lambda i, b: (i, b)),
            pl.BlockSpec((tm, D_MODEL), lambda i, b: (i, GATE_BLOCK0 + b)),
            ANY,
            ANY,
        ],
        out_shape=[
            jax.ShapeDtypeStruct((n, N_BRANCH * BRANCH_W), BF16),
            jax.ShapeDtypeStruct((n, IN_COLS), BF16),
            jax.ShapeDtypeStruct((D_MODEL, D_MODEL), F32),
            jax.ShapeDtypeStruct((N_BRANCH, BRANCH_W, D_MODEL), F32),
        ],
        scratch_shapes=[
            pltpu.VMEM((tm, D_MODEL), F32),
            pltpu.VMEM((D_MODEL, D_MODEL), F32),
            pltpu.VMEM((N_BRANCH, BRANCH_W, D_MODEL), F32),
        ],
        args=[dout2, merged2, z2, proj2, wbr_l, wo_l],
        ride=ride,
    )


def _inproj_bwd_x(dproj2, w_l, x2, g_row, dout2, layer, ride=None):
    n = x2.shape[0]
    tm = _tile(n, 1024)

    def body(dp_ref, w_ref, x_ref, g_ref, do_ref, dx_ref, dng_ref, dh_ref):
        i = pl.program_id(0)
        s = pl.program_id(1)
        part = _dot_nt(dp_ref[...], w_ref[...])

        @pl.when(s == 0)
        def _():
            dh_ref[...] = part

        @pl.when(s > 0)
        def _():
            dh_ref[...] += part

        @pl.when(jnp.logical_and(i == 0, s == 0))
        def _():
            dng_ref[...] = jnp.zeros_like(dng_ref)

        @pl.when(s == N_CHIPS - 1)
        def _():
            xv = x_ref[...]
            r = lax.rsqrt(_lanemean(xv * xv) + RMS_EPS)
            xh = xv * r
            dh = dh_ref[...]
            dng_ref[...] += _rowsum(dh * xh)
            dxh = dh * g_ref[...]
            dx_ref[...] = do_ref[...] + r * (dxh - xh * _lanemean(dxh * xh))

    return _host_call(
        body,
        name=f"inproj_bwd_x_l{layer}",
        grid=(n // tm, N_CHIPS),
        in_specs=[
            pl.BlockSpec((tm, SHARD_COLS), lambda i, s: (i, s)),
            pl.BlockSpec((None, D_MODEL, SHARD_COLS), lambda i, s: (s, 0, 0)),
            pl.BlockSpec((tm, D_MODEL), lambda i, s: (i, 0)),
            pl.BlockSpec((1, D_MODEL), lambda i, s: (0, 0)),
            pl.BlockSpec((tm, D_MODEL), lambda i, s: (i, 0)),
        ],
        out_specs=[
            pl.BlockSpec((tm, D_MODEL), lambda i, s: (i, 0)),
            pl.BlockSpec((1, D_MODEL), lambda i, s: (0, 0)),
        ],
        out_shape=[jax.ShapeDtypeStruct((n, D_MODEL), F32), jax.ShapeDtypeStruct((1, D_MODEL), F32)],
        scratch_shapes=[pltpu.VMEM((tm, D_MODEL), F32)],
        args=[dproj2, w_l, x2, g_row, dout2],
        ride=ride,
    )


def _inproj_bwd_w(h2, dproj2, layer, ride=None):
    n = h2.shape[0]
    tm = _tile(n, 2048)

    def body(h_ref, dp_ref, gw_ref):
        gw = _dot_tn(h_ref[...], dp_ref[...])

        @pl.when(pl.program_id(1) == 0)
        def _():
            gw_ref[...] = gw

        @pl.when(pl.program_id(1) > 0)
        def _():
            gw_ref[...] += gw

    outs, rode = _host_call(
        body,
        name=f"inproj_bwd_w_l{layer}",
        grid=(N_COL_TILES, n // tm),
        in_specs=[
            pl.BlockSpec((tm, D_MODEL), lambda s, i: (i, 0)),
            pl.BlockSpec((tm, COL_TILE), lambda s, i: (i, s)),
        ],
        out_specs=[pl.BlockSpec((None, D_MODEL, COL_TILE), lambda s, i: (s // 2, 0, s % 2))],
        out_shape=[jax.ShapeDtypeStruct((N_CHIPS, D_MODEL, SHARD_COLS), F32)],
        args=[h2, dproj2],
        ride=ride,
    )
    return outs[0], rode


def _adamw_math(w, g, m, v):
    m = ADAM_B1 * m + (1.0 - ADAM_B1) * g
    v = ADAM_B2 * v + (1.0 - ADAM_B2) * jnp.square(g)
    m_hat = m / (1.0 - ADAM_B1 ** ADAM_STEP)
    v_hat = v / (1.0 - ADAM_B2 ** ADAM_STEP)
    delta = -ADAM_LR * (m_hat / (jnp.sqrt(v_hat) + ADAM_EPS) + ADAM_WD * w)
    return delta, m, v


def _adamw_sharded(w, m, v, part, layer, prev, name, ride=None):
    _, rows, cols = w.shape
    tr = _tile(rows, max(16, (1 << 21) // (4 * cols) // 16 * 16), mult=16)

    def body(w_ref, m_ref, v_ref, p_ref, *rest):
        g_out, d_out, m_out, v_out = rest[-4:]
        g = ((p_ref[0].astype(F32) + p_ref[1].astype(F32)) + p_ref[2].astype(F32)) + p_ref[3].astype(F32)
        delta, m2, v2 = _adamw_math(w_ref[...], g, m_ref[...], v_ref[...])
        g_out[...] = g
        d_out[...] = delta
        m_out[...] = m2
        v_out[...] = v2

    spec = pl.BlockSpec((None, tr, cols), lambda i: (layer, i, 0))
    return _host_call(
        body,
        name=f"adamw_{name}_l{layer}",
        grid=(rows // tr,),
        in_specs=[spec, spec, spec, pl.BlockSpec((N_CHIPS, tr, cols), lambda i: (0, i, 0))]
        + ([ANY] * 4 if prev else []),
        out_specs=[spec] * 4,
        out_shape=[jax.ShapeDtypeStruct(w.shape, F32)] * 4,
        args=[w, m, v, part] + (list(prev) if prev else []),
        aliases={4 + k: k for k in range(4)} if prev else {},
        ride=ride,
    )


def _adamw_packed(w, m, v, g):
    rows = w.shape[0]
    tr = _tile(rows, rows // 2 if rows % 16 == 0 else rows)

    def body(w_ref, m_ref, v_ref, g_ref, d_out, m_out, v_out):
        delta, m2, v2 = _adamw_math(w_ref[...], g_ref[...], m_ref[...], v_ref[...])
        d_out[...] = delta
        m_out[...] = m2
        v_out[...] = v2

    spec = pl.BlockSpec((tr, GROUP_W), lambda i: (i, 0))
    return pl.pallas_call(
        body,
        name="adamw_small",
        grid=(rows // tr,),
        in_specs=[spec] * 4,
        out_specs=[spec] * 3,
        out_shape=[jax.ShapeDtypeStruct(w.shape, F32)] * 3,
        compiler_params=_params("arbitrary"),
    )(w, m, v, g)


SMALL = ["norm_g", "pool_w", "pool_scale", "conv_b", "conv_ln_g", "conv_ln_b", "sgu_ln_g", "sgu_ln_b", "sgu_w",
         "sgu_b", "final_g"]
WEIGHTS = ["norm_g", "w_in", "pool_w", "pool_scale", "conv_w", "conv_b", "conv_ln_g", "conv_ln_b", "sgu_ln_g",
           "sgu_ln_b", "sgu_w", "sgu_b", "sc_w", "w_branch", "w_o", "final_g"]
HALF_SHAPES = [(D_MODEL // 2, SHARD_COLS), (N_BRANCH * BRANCH_W // 2, BR_SHARD), (BR_SHARD // 2, D_MODEL),
               (CS_ROWS // 2, GROUP_W)]


def _pack_small(tree, last_rows=None):
    tail = jnp.zeros((8, GROUP_W), F32) if last_rows is None else last_rows
    return jnp.concatenate([tree[k].reshape(-1, GROUP_W) for k in SMALL] + [tail], axis=0)


def _unpack_small(packed, like):
    out, r = {}, 0
    for k in SMALL:
        nr = like[k].size // GROUP_W
        out[k] = packed[r:r + nr].reshape(like[k].shape)
        r += nr
    return out


def _pad_rows(a, rows):
    pad = [(0, 0)] * a.ndim
    pad[-2] = (0, rows - a.shape[-2])
    return jnp.pad(a, pad)


def _pack_cs(conv, short):
    return jnp.concatenate([_pad_rows(conv, CS_ROWS // 2), _pad_rows(short, CS_ROWS // 2)], axis=-2)


def _shard_major_rows(a):
    return a.reshape(a.shape[0], N_CHIPS, GROUP_W).transpose(1, 0, 2)


def kernel(x, norm_g, w_in, pool_w, pool_scale, conv_w, conv_b, conv_ln_g, conv_ln_b, sgu_ln_g, sgu_ln_b, sgu_w, sgu_b, sc_w, w_branch, w_o, final_g, loss_target, m_norm_g, m_w_in, m_pool_w, m_pool_scale, m_conv_w, m_conv_b, m_conv_ln_g, m_conv_ln_b, m_sgu_ln_g, m_sgu_ln_b, m_sgu_w, m_sgu_b, m_sc_w, m_w_branch, m_w_o, m_final_g, v_norm_g, v_w_in, v_pool_w, v_pool_scale, v_conv_w, v_conv_b, v_conv_ln_g, v_conv_ln_b, v_sgu_ln_g, v_sgu_ln_b, v_sgu_w, v_sgu_b, v_sc_w, v_w_branch, v_w_o, v_final_g):
    w = dict(norm_g=norm_g, w_in=w_in, pool_w=pool_w, pool_scale=pool_scale, conv_w=conv_w, conv_b=conv_b,
             conv_ln_g=conv_ln_g, conv_ln_b=conv_ln_b, sgu_ln_g=sgu_ln_g, sgu_ln_b=sgu_ln_b, sgu_w=sgu_w,
             sgu_b=sgu_b, sc_w=sc_w, w_branch=w_branch, w_o=w_o, final_g=final_g)
    mom = dict(norm_g=m_norm_g, w_in=m_w_in, pool_w=m_pool_w, pool_scale=m_pool_scale, conv_w=m_conv_w,
               conv_b=m_conv_b, conv_ln_g=m_conv_ln_g, conv_ln_b=m_conv_ln_b, sgu_ln_g=m_sgu_ln_g,
               sgu_ln_b=m_sgu_ln_b, sgu_w=m_sgu_w, sgu_b=m_sgu_b, sc_w=m_sc_w, w_branch=m_w_branch, w_o=m_w_o,
               final_g=m_final_g)
    var = dict(norm_g=v_norm_g, w_in=v_w_in, pool_w=v_pool_w, pool_scale=v_pool_scale, conv_w=v_conv_w,
               conv_b=v_conv_b, conv_ln_g=v_conv_ln_g, conv_ln_b=v_conv_ln_b, sgu_ln_g=v_sgu_ln_g,
               sgu_ln_b=v_sgu_ln_b, sgu_w=v_sgu_w, sgu_b=v_sgu_b, sc_w=v_sc_w, w_branch=v_w_branch, w_o=v_w_o,
               final_g=v_final_g)

    nb, seq, _ = x.shape
    n = nb * seq
    core = lax.axis_index("c").astype(jnp.int32).reshape(1)

    win_s = w_in.astype(BF16).reshape(DEPTH, 2, D_MODEL // 2, SHARD_COLS)
    wbr_s = w_branch.astype(BF16)
    wo_s = w_o.astype(BF16)
    cs_s = _pack_cs(conv_w, sc_w)

    (win_f0,) = _allgather_layer0([win_s])

    def layer_weights(win_f, wbr_f, wo_f, cs_f):
        cs_f = cs_f.reshape(N_CHIPS, CS_ROWS, GROUP_W)
        return dict(win=win_f.reshape(N_CHIPS, D_MODEL, SHARD_COLS),
                    wbr=wbr_f.reshape(N_CHIPS, N_BRANCH, BRANCH_W, BR_SHARD).transpose(1, 2, 0, 3).reshape(
                        N_BRANCH, BRANCH_W, D_MODEL),
                    wo=wo_f.reshape(D_MODEL, D_MODEL),
                    conv_w=cs_f[:, :CONV_ROWS], sc_w=cs_f[:, CS_ROWS // 2:CS_ROWS // 2 + SHORT_ROWS])

    def mixer_weights(l, gathered):
        row = lambda a: a[l].reshape(1, BRANCH_W)
        bias = jnp.repeat(jnp.swapaxes(sgu_b[l], 0, 1), GROUP_W, axis=1)
        return dict(pool_w=pool_w[l].astype(BF16), pool_scale=row(pool_scale), conv_w=gathered["conv_w"],
                    conv_b=row(conv_b), conv_ln_g=row(conv_ln_g), conv_ln_b=row(conv_ln_b), sgu_ln_g=row(sgu_ln_g),
                    sgu_ln_b=row(sgu_ln_b), sgu_w=sgu_w[l], sgu_bias=bias, sc_w=gathered["sc_w"])

    lw, mw = [None, None], [None, None]

    xs, projs, hs, zs, mgs, caccs = [x.reshape(n, D_MODEL)], [], [], [], [], []
    for l in range(DEPTH):
        first = l == 0
        win_l = win_f0.reshape(N_CHIPS, D_MODEL, SHARD_COLS) if first else lw[1]["win"]
        ride = _join(_gather_whole([wbr_s, wo_s, cs_s], 0), _gather_layer1_first_half(win_s)) if first else None
        (proj, h), rode = _inproj_fwd(xs[l], norm_g[l:l + 1], win_l, l, ride=ride)
        if first:
            lw[0] = layer_weights(win_f0, *rode[:3])
            mw[0] = mixer_weights(0, lw[0])
            win_f1 = rode[3]
        (z3, cacc), rode = _mixers_fwd(proj.reshape(nb, seq, IN_COLS), mw[l], l,
                                       ride=_gather_layer1_second_half(win_s, win_f1) if first else None)
        caccs.append(cacc)
        if first:
            win_f1 = rode[0]
        z = z3.reshape(n, N_BRANCH * BRANCH_W)
        if first:
            (x_new, merged), rode = _merge_fwd(z, proj, xs[l], lw[l]["wbr"], lw[l]["wo"], l,
                                               ride=_gather_whole([wbr_s, wo_s, cs_s], 1))
            lw[1] = layer_weights(win_f1, *rode)
            mw[1] = mixer_weights(1, lw[1])
            xs.append(x_new)
        else:
            dx, merged, loss_part, g_final = _merge_fwd_head(
                z, proj, xs[l], lw[l]["wbr"], lw[l]["wo"], final_g.reshape(1, D_MODEL),
                loss_target.reshape(n, D_MODEL), l)
        projs.append(proj)
        hs.append(h)
        zs.append(z)
        mgs.append(merged)

    def chip_major(gwbr):
        return gwbr.reshape(N_BRANCH, BRANCH_W, N_CHIPS, BR_SHARD).transpose(2, 0, 1, 3)

    def as_halves(arrays, first_index=0):
        return [g.reshape((N_CHIPS, 2) + s) for g, s in zip(arrays, HALF_SHAPES[first_index:])]

    def add_halves(grads, others, l, first_index=0):
        return [_add_halves(g, o, core, f"add_halves_l{l}_{first_index + i}")
                for i, (g, o) in enumerate(zip(grads, others))]

    per_layer, parts = {}, [None] * DEPTH
    (dz, dproj, gwo, gwbr), _ = _merge_bwd(dx, mgs[1], zs[1], projs[1], lw[1]["wbr"], lw[1]["wo"], 1)
    dproj3, gm, _ = _mixers_bwd(projs[1].reshape(nb, seq, IN_COLS), dz.reshape(nb, seq, N_BRANCH * BRANCH_W), caccs[1],
                                dproj.reshape(nb, seq, IN_COLS), mw[1], 1)
    dproj = dproj3.reshape(n, IN_COLS)
    gwin, _ = _inproj_bwd_w(hs[1], dproj, 1)
    gcs = _pack_cs(_shard_major_rows(gm["conv_w"]), _shard_major_rows(gm["sc_w"]))
    grads1 = as_halves([gwin, chip_major(gwbr), gwo, gcs])
    (dx, g_norm), others1 = _inproj_bwd_x(dproj, lw[1]["win"], xs[1], norm_g[1:2], dx, 1, ride=_swap_halves(grads1))
    gm["norm_g"] = g_norm
    per_layer[1] = gm
    sums1 = add_halves(grads1, others1, 1)
    (dz, dproj, gwo, gwbr), scattered1 = _merge_bwd(dx, mgs[0], zs[0], projs[0], lw[0]["wbr"], lw[0]["wo"], 0,
                                                    ride=_scatter_chip_sums(sums1))
    dproj3, gm, parts[1] = _mixers_bwd(projs[0].reshape(nb, seq, IN_COLS), dz.reshape(nb, seq, N_BRANCH * BRANCH_W),
                                       caccs[0], dproj.reshape(nb, seq, IN_COLS), mw[0], 0,
                                       ride=_share_halves(scattered1))
    dproj = dproj3.reshape(n, IN_COLS)
    gcs = _pack_cs(_shard_major_rows(gm["conv_w"]), _shard_major_rows(gm["sc_w"]))
    small0 = as_halves([chip_major(gwbr), gwo, gcs], first_index=1)

    per_layer[0] = dict(gm, norm_g=jnp.zeros((1, D_MODEL), F32))

    def stack(k, shape):
        return jnp.stack([per_layer[l][k] for l in range(DEPTH)]).reshape(shape)

    g_small = {"final_g": g_final.reshape(D_MODEL), "norm_g": stack("norm_g", (DEPTH, D_MODEL)),
               "pool_w": stack("pool_w", pool_w.shape), "sgu_w": stack("sgu_w", sgu_w.shape),
               "sgu_b": jnp.swapaxes(stack("sgu_b", (DEPTH, CHUNK, GROUP_W))[:, :, :N_GROUPS], 1, 2)}
    for k in ("pool_scale", "conv_b", "conv_ln_g", "conv_ln_b", "sgu_ln_g", "sgu_ln_b"):
        g_small[k] = stack(k, (DEPTH, BRANCH_W))
    small_part = _pack_small(g_small, jnp.broadcast_to(loss_part, (8, GROUP_W)))

    gwin, rode = _inproj_bwd_w(hs[0], dproj, 0, ride=_join(_swap_halves(small0), _swap_whole(small_part)))
    others_small0, small_other = rode[:3], rode[3]
    small_chip = _add2(small_part, small_other, "add_small_sibling")
    (gwin_h,) = as_halves([gwin])

    def flat(a, i):
        rows, cols = 2 * HALF_SHAPES[i][0], HALF_SHAPES[i][1]
        return a.reshape(a.shape[0], rows, cols)

    names4 = ["w_in", "w_branch", "w_o", "conv_sc"]

    def packed4(t):
        return [t["w_in"], t["w_branch"], t["w_o"], _pack_cs(t["conv_w"], t["sc_w"])]

    w4, m4, v4 = ([flat(a, i) for i, a in enumerate(packed4(t))] for t in (w, mom, var))

    def sharded_update(i, layer, prev, ride=None):
        return _adamw_sharded(w4[i], m4[i], v4[i], flat(parts[layer][i], i), layer, prev, names4[i], ride=ride)

    updated = [sharded_update(i, 1, None)[0] for i in range(4)]

    (other_win0,) = _exchange_call("swap_halves_l0_w_in", _swap_halves([gwin_h]))
    sums0 = add_halves([gwin_h], [other_win0], 0) + add_halves(small0, others_small0, 0, first_index=1)
    (dx, g_norm0), rode = _inproj_bwd_x(dproj, lw[0]["win"], xs[0], norm_g[0:1], dx, 0,
                                        ride=_join(_scatter_chip_sums(sums0), _spread_chip_sums(small_chip)))
    scattered0, small_chips = rode[:4], rode[4]
    grad_x = dx.reshape(nb, seq, D_MODEL)

    g_norm0, parts[0] = _allreduce_small(g_norm0.reshape(D_MODEL // GROUP_W, GROUP_W), _share_halves(scattered0))
    g_packed = lax.dynamic_update_slice(_sum_chips(small_chips, "sum_small_chips"), g_norm0, (0, 0))
    loss = g_packed[g_packed.shape[0] - 8, 0]
    d_packed, m_packed, v_packed = _adamw_packed(_pack_small(w), _pack_small(mom), _pack_small(var), g_packed)
    grads, deltas, new_m, new_v = {}, {}, {}, {}
    for tree, packed in ((grads, g_packed), (deltas, d_packed), (new_m, m_packed), (new_v, v_packed)):
        tree.update(_unpack_small(packed, w))

    for i, name in enumerate(names4):
        res, _ = sharded_update(i, 0, updated[i])
        for tree, r in zip((grads, deltas, new_m, new_v), res):
            if name == "conv_sc":
                tree["conv_w"] = r[:, :CONV_K]
                tree["sc_w"] = r[:, CS_ROWS // 2:CS_ROWS // 2 + SHORT_K]
            else:
                tree[name] = r.reshape(w[name].shape)

    return (loss, grad_x, *[grads[k] for k in WEIGHTS], *[deltas[k] for k in WEIGHTS],
            *[new_m[k] for k in WEIGHTS], *[new_v[k] for k in WEIGHTS])
```

```python
import functools

import jax
import jax.numpy as jnp
from jax import lax
from jax.experimental import pallas as pl
from jax.experimental.pallas import tpu as pltpu

F32 = jnp.float32
BF16 = jnp.bfloat16

D_MODEL = 1024
DEPTH = 2
N_BRANCH = 4
BRANCH_W = 512
N_GROUPS = 4
GROUP_W = 128
POOL_WINDOWS = (2, 4, 8, 16)
CONV_K = 31
SHORT_K = 3
CHUNK = 128
N_PIECES = 12
PIECE_COLS = N_PIECES * BRANCH_W
IN_COLS = PIECE_COLS + N_BRANCH * D_MODEL
N_CHIPS = 4
SHARD_COLS = IN_COLS // N_CHIPS
BR_SHARD = D_MODEL // N_CHIPS
COL_TILE = SHARD_COLS // 2
N_COL_TILES = IN_COLS // COL_TILE
GATE_BLOCK0 = PIECE_COLS // D_MODEL
RMS_EPS = 1e-6
LN_EPS = 1e-5
HALO = 32
CONV_ROWS = 32
SHORT_ROWS = 8
CS_ROWS = 64

ADAM_LR = 0.001
ADAM_B1 = 0.9
ADAM_B2 = 0.999
ADAM_EPS = 1e-08
ADAM_WD = 0.01
ADAM_STEP = 10

VMEM_LIMIT = 60 * 1024 * 1024
MESH_ID = pl.DeviceIdType.MESH
ANY = pl.BlockSpec(memory_space=pl.ANY)
VMEM_WHOLE = pl.BlockSpec(memory_space=pltpu.VMEM)

(P_X, P_GATE, C_A, C_B, C_GATE, G_U, G_V, G_GATE, S_B, S_C, S_X, S_GATE) = range(N_PIECES)


def _params(*sem):
    return pltpu.CompilerParams(dimension_semantics=sem, vmem_limit_bytes=VMEM_LIMIT)


def _sigmoid(v):
    return 0.5 * jnp.tanh(0.5 * v) + 0.5


def _silu(v):
    return v * _sigmoid(v)


def _silu_pair(v):
    s = _sigmoid(v)
    return v * s, s * (1.0 + v * (1.0 - s))


def _dot(a, b):
    return jnp.dot(a, b, preferred_element_type=F32)


def _dot_nt(a, b):
    return lax.dot_general(a, b, (((1,), (1,)), ((), ())), preferred_element_type=F32)


def _dot_tn(a, b):
    return lax.dot_general(a, b, (((0,), (0,)), ((), ())), preferred_element_type=F32)


def _rowsum(v):
    return jnp.sum(v, axis=0, keepdims=True)


def _lanemean(v):
    return jnp.mean(v, axis=-1, keepdims=True)


def _tile(n, want, mult=8):
    t = max(1, min(n, want))
    while n % t or (t % mult and t != n):
        t -= 1
    return t


def _place():
    x, y, c = lax.axis_index("x"), lax.axis_index("y"), lax.axis_index("c")
    chip = 2 * x + y
    peers = [(1 - x, y), (x, 1 - y), (1 - x, 1 - y)]
    return x, y, c, chip, peers


def _remote(src, dst, send_sem, recv_sem, dev):
    return pltpu.make_async_remote_copy(src_ref=src, dst_ref=dst, send_sem=send_sem, recv_sem=recv_sem,
                                        device_id=dev, device_id_type=MESH_ID)


class _Exchange:
    def __init__(self, inputs, out_shapes, plan, n_remote, n_local=0, aliases=None):
        self.inputs = list(inputs)
        self.out_shapes = list(out_shapes)
        self.plan = plan
        self.n_remote = n_remote
        self.n_local = n_local
        self.aliases = dict(aliases or {})

    def copies(self, in_refs, out_refs, send_sems, recv_sems, loc_sems):
        remote, local = self.plan(in_refs, out_refs)
        assert len(remote) == self.n_remote and len(local) == self.n_local
        cps = [_remote(s, d, send_sems.at[t], recv_sems.at[t], dev) for t, (s, d, dev) in enumerate(remote)]
        cps += [pltpu.make_async_copy(s, d, loc_sems.at[t]) for t, (s, d) in enumerate(local)]
        return cps

    def sem_shapes(self):
        return [pltpu.SemaphoreType.DMA((max(self.n_remote, 1),)), pltpu.SemaphoreType.DMA((max(self.n_remote, 1),)),
                pltpu.SemaphoreType.DMA((max(self.n_local, 1),))]


def _exchange_call(name, ex):
    n_ci = len(ex.inputs)

    def body(*refs):
        cins, couts = refs[:n_ci], refs[n_ci:n_ci + len(ex.out_shapes)]
        cps = ex.copies(cins, couts, *refs[n_ci + len(ex.out_shapes):])
        for cp in cps:
            cp.start()
        for cp in cps:
            cp.wait()

    return pl.pallas_call(
        body,
        name=name,
        in_specs=[ANY] * n_ci,
        out_specs=[ANY] * len(ex.out_shapes),
        out_shape=ex.out_shapes,
        scratch_shapes=ex.sem_shapes(),
        input_output_aliases=ex.aliases,
    )(*ex.inputs)


def _host_call(body, *, name, grid, in_specs, out_specs, out_shape, args, scratch_shapes=(), aliases=None, ride=None):
    n_in, n_out, n_scr = len(in_specs), len(out_specs), len(scratch_shapes)
    sem = ("arbitrary",) * len(grid)
    aliases = dict(aliases or {})
    if ride is None:
        outs = pl.pallas_call(body, name=name, grid=grid, in_specs=list(in_specs), out_specs=list(out_specs),
                              out_shape=list(out_shape), scratch_shapes=list(scratch_shapes),
                              input_output_aliases=aliases, compiler_params=_params(*sem))(*args)
        return list(outs), []
    n_ci, n_co = len(ride.inputs), len(ride.out_shapes)

    def full_body(*refs):
        ins, cins = refs[:n_in], refs[n_in:n_in + n_ci]
        o0 = n_in + n_ci
        outs, couts = refs[o0:o0 + n_out], refs[o0 + n_out:o0 + n_out + n_co]
        s0 = o0 + n_out + n_co
        scr, sems = refs[s0:s0 + n_scr], refs[s0 + n_scr:]
        first = functools.reduce(jnp.logical_and, [pl.program_id(d) == 0 for d in range(len(grid))])
        last = functools.reduce(jnp.logical_and, [pl.program_id(d) == grid[d] - 1 for d in range(len(grid))])

        @pl.when(first)
        def _():
            for cp in ride.copies(cins, couts, *sems):
                cp.start()

        body(*ins, *outs, *scr)

        @pl.when(last)
        def _():
            for cp in ride.copies(cins, couts, *sems):
                cp.wait()

    for ci, co in ride.aliases.items():
        aliases[n_in + ci] = n_out + co
    outs = pl.pallas_call(
        full_body, name=name, grid=grid, in_specs=list(in_specs) + [ANY] * n_ci,
        out_specs=list(out_specs) + [ANY] * n_co, out_shape=list(out_shape) + ride.out_shapes,
        scratch_shapes=list(scratch_shapes) + ride.sem_shapes(), input_output_aliases=aliases,
        compiler_params=_params(*sem))(*args, *ride.inputs)
    return list(outs[:n_out]), list(outs[n_out:])


def _allgather_layer0(shards):
    na = len(shards)

    def body(*refs):
        ins, outs = refs[:na], refs[na:2 * na]
        send_sems, recv_sems, fsend_sems, frecv_sems, loc_sems = refs[2 * na:]
        x, y, c, chip, peers = _place()
        sib = (x, y, 1 - c)
        locs = [pltpu.make_async_copy(ins[a].at[0], outs[a].at[chip], loc_sems.at[a]) for a in range(na)]
        for cp in locs:
            cp.start()
        pending = []
        for k, (px, py) in enumerate(peers):
            for a in range(na):
                cp = _remote(ins[a].at[0, c], outs[a].at[chip, c], send_sems.at[k * na + a],
                             recv_sems.at[k * na + a], (px, py, c))
                cp.start()
                pending.append(cp)
        for k, (px, py) in enumerate(peers):
            pchip = 2 * px + py
            for a in range(na):
                slab = outs[a].at[pchip, c]
                _remote(slab, slab, send_sems.at[k * na + a], recv_sems.at[k * na + a], (px, py, c)).wait_recv()
                cp = _remote(slab, slab, fsend_sems.at[k * na + a], frecv_sems.at[k * na + a], sib)
                cp.start()
                pending.append(cp)
        for k, (px, py) in enumerate(peers):
            pchip = 2 * px + py
            for a in range(na):
                slab = outs[a].at[pchip, 1 - c]
                _remote(slab, slab, fsend_sems.at[k * na + a], frecv_sems.at[k * na + a], sib).wait_recv()
        for cp in pending:
            cp.wait_send()
        for cp in locs:
            cp.wait()

    return pl.pallas_call(
        body,
        name="allgather_layer0",
        in_specs=[ANY] * na,
        out_specs=[ANY] * na,
        out_shape=[jax.ShapeDtypeStruct((N_CHIPS,) + a.shape[1:], a.dtype) for a in shards],
        scratch_shapes=[pltpu.SemaphoreType.DMA((3 * na,))] * 4 + [pltpu.SemaphoreType.DMA((na,))],
    )(*shards)


def _gather_layer1_first_half(win_s):
    def plan(cins, couts):
        _, _, c, chip, peers = _place()
        (src,), (dst,) = cins, couts
        return ([(src.at[1, 0], dst.at[chip, 0], (px, py, c)) for px, py in peers],
                [(src.at[1], dst.at[chip])])

    return _Exchange([win_s], [jax.ShapeDtypeStruct((N_CHIPS,) + win_s.shape[1:], win_s.dtype)], plan, 3, 1)


def _gather_layer1_second_half(win_s, win_f1):
    def plan(cins, couts):
        _, _, c, chip, peers = _place()
        src, (dst,) = cins[0], couts
        return [(src.at[1, 1], dst.at[chip, 1], (px, py, c)) for px, py in peers], []

    return _Exchange([win_s, win_f1], [jax.ShapeDtypeStruct(win_f1.shape, win_f1.dtype)], plan, 3, 0, aliases={1: 0})


def _gather_whole(shards, layer):
    na = len(shards)

    def plan(cins, couts):
        _, _, c, chip, peers = _place()
        remote = [(cins[a].at[layer], couts[a].at[chip], (px, py, c)) for a in range(na) for px, py in peers]
        return remote, [(cins[a].at[layer], couts[a].at[chip]) for a in range(na)]

    return _Exchange(shards, [jax.ShapeDtypeStruct((N_CHIPS,) + a.shape[1:], a.dtype) for a in shards], plan,
                     3 * na, na)


def _join(a, b):
    n_in, n_out = len(a.inputs), len(a.out_shapes)

    def plan(cins, couts):
        remote_a, local_a = a.plan(cins[:n_in], couts[:n_out])
        remote_b, local_b = b.plan(cins[n_in:], couts[n_out:])
        return remote_a + remote_b, local_a + local_b

    aliases = dict(a.aliases)
    aliases.update({n_in + i: n_out + o for i, o in b.aliases.items()})
    return _Exchange(a.inputs + b.inputs, a.out_shapes + b.out_shapes, plan, a.n_remote + b.n_remote,
                     a.n_local + b.n_local, aliases)


def _swap_halves(grads):
    def plan(cins, couts):
        x, y, c, _, _ = _place()
        return [(g.at[:, 1 - c], r, (x, y, 1 - c)) for g, r in zip(cins, couts)], []

    return _Exchange(grads, [jax.ShapeDtypeStruct(g.shape[:1] + g.shape[2:], g.dtype) for g in grads], plan,
                     len(grads))


def _scatter_chip_sums(sums):
    na = len(sums)

    def plan(cins, couts):
        _, _, c, chip, peers = _place()
        remote = [(cins[a].at[2 * px + py], couts[a].at[chip, c], (px, py, c)) for a in range(na) for px, py in peers]
        return remote, [(cins[a].at[chip], couts[a].at[chip, c]) for a in range(na)]

    return _Exchange(sums, [jax.ShapeDtypeStruct((N_CHIPS, 2) + s.shape[1:], s.dtype) for s in sums], plan,
                     3 * na, na)


def _share_halves(parts):
    na = len(parts)

    def plan(cins, couts):
        x, y, c, _, _ = _place()
        return [(p.at[:, c], p.at[:, c], (x, y, 1 - c)) for p in couts], []

    return _Exchange(parts, [jax.ShapeDtypeStruct(p.shape, p.dtype) for p in parts], plan, na, 0,
                     aliases={a: a for a in range(na)})


def _allreduce_small(packed, ride):
    rows = packed.shape[0]
    n_ci, n_co = len(ride.inputs), len(ride.out_shapes)

    def body(*refs):
        sg_ref, cins = refs[0], refs[1:1 + n_ci]
        res_ref, couts = refs[1 + n_ci], refs[2 + n_ci:2 + n_ci + n_co]
        sib_ref, cs_ref, rem_ref, s1_send, s1_recv, s2_send, s2_recv = refs[2 + n_ci + n_co:9 + n_ci + n_co]
        riding = ride.copies(cins, couts, *refs[9 + n_ci + n_co:])
        for cp in riding:
            cp.start()
        x, y, c, chip, peers = _place()
        cp = _remote(sg_ref, sib_ref, s1_send, s1_recv, (x, y, 1 - c))
        cp.start()
        cp.wait()
        cs_ref[...] = sg_ref[...] + sib_ref[...]
        rem_ref[chip] = cs_ref[...]
        cps = [_remote(cs_ref, rem_ref.at[chip], s2_send.at[k], s2_recv.at[chip], (px, py, c))
               for k, (px, py) in enumerate(peers)]
        for cp in cps:
            cp.start()
        for k, (px, py) in enumerate(peers):
            pchip = 2 * px + py
            _remote(cs_ref, rem_ref.at[pchip], s2_send.at[k], s2_recv.at[pchip], (px, py, c)).wait_recv()
        for cp in cps:
            cp.wait_send()
        res_ref[...] = ((rem_ref[0] + rem_ref[1]) + rem_ref[2]) + rem_ref[3]
        for cp in riding:
            cp.wait()

    outs = pl.pallas_call(
        body,
        name="allreduce_small",
        in_specs=[VMEM_WHOLE] + [ANY] * n_ci,
        out_specs=[VMEM_WHOLE] + [ANY] * n_co,
        out_shape=[jax.ShapeDtypeStruct((rows, GROUP_W), F32)] + ride.out_shapes,
        scratch_shapes=[
            pltpu.VMEM((rows, GROUP_W), F32),
            pltpu.VMEM((rows, GROUP_W), F32),
            pltpu.VMEM((N_CHIPS, rows, GROUP_W), F32),
            pltpu.SemaphoreType.DMA,
            pltpu.SemaphoreType.DMA,
            pltpu.SemaphoreType.DMA((3,)),
            pltpu.SemaphoreType.DMA((N_CHIPS,)),
        ] + ride.sem_shapes(),
        input_output_aliases={1 + i: 1 + o for i, o in ride.aliases.items()},
        compiler_params=pltpu.CompilerParams(vmem_limit_bytes=VMEM_LIMIT),
    )(packed, *ride.inputs)
    return outs[0], list(outs[1:])


HBM_SPEC = pl.BlockSpec(memory_space=pltpu.HBM)
SEM_SPEC = pl.BlockSpec(memory_space=pltpu.SEMAPHORE)
SIDE_EFFECT = pltpu.SideEffectType.DATAFLOW_SIDE_EFFECTING


def _swap_halves_start(grad, name):
    land_shape = grad.shape[:1] + grad.shape[2:]

    def body(g_ref, land_ref, send_sem, recv_sem, g_thru, land_thru, token):
        del g_thru, land_thru
        x, y, c, _, _ = _place()
        _remote(g_ref.at[:, 1 - c], land_ref, send_sem, recv_sem, (x, y, 1 - c)).start()
        token[...] = jnp.zeros_like(token)

    return pl.pallas_call(
        body,
        name=name + "_start",
        out_shape=(pltpu.SemaphoreType.DMA(()), pltpu.SemaphoreType.DMA(()), pltpu.HBM(grad.shape, grad.dtype),
                   pltpu.HBM(land_shape, grad.dtype), jax.ShapeDtypeStruct((8, GROUP_W), F32)),
        in_specs=(HBM_SPEC, HBM_SPEC),
        out_specs=(SEM_SPEC, SEM_SPEC, HBM_SPEC, HBM_SPEC, VMEM_WHOLE),
        input_output_aliases={0: 2, 1: 3},
        compiler_params=pltpu.CompilerParams(has_side_effects=SIDE_EFFECT),
    )(pltpu.with_memory_space_constraint(grad, pltpu.HBM),
      pltpu.with_memory_space_constraint(lax.empty(land_shape, grad.dtype), pltpu.HBM))


def _swap_halves_wait(send_sem, recv_sem, g_thru, land_thru, after, name):
    def body(g_ref, land_ref, send_sem, recv_sem, after_ref, g_out, got_ref):
        del after_ref, g_out, got_ref
        x, y, c, _, _ = _place()
        cp = _remote(g_ref.at[:, 1 - c], land_ref, send_sem, recv_sem, (x, y, 1 - c))
        cp.wait_send()
        cp.wait_recv()

    return pl.pallas_call(
        body,
        name=name + "_wait",
        out_shape=(pltpu.HBM(g_thru.shape, g_thru.dtype), pltpu.HBM(land_thru.shape, land_thru.dtype)),
        in_specs=(HBM_SPEC, HBM_SPEC, SEM_SPEC, SEM_SPEC, ANY),
        out_specs=(HBM_SPEC, HBM_SPEC),
        input_output_aliases={0: 0, 1: 1},
        compiler_params=pltpu.CompilerParams(has_side_effects=SIDE_EFFECT),
    )(g_thru, land_thru, send_sem, recv_sem, after)


def _swap_whole(packed):
    def plan(cins, couts):
        x, y, c, _, _ = _place()
        return [(cins[0], couts[0], (x, y, 1 - c))], []

    return _Exchange([packed], [jax.ShapeDtypeStruct(packed.shape, packed.dtype)], plan, 1)


def _spread_chip_sums(chip_sum):
    def plan(cins, couts):
        _, _, c, chip, peers = _place()
        return ([(cins[0], couts[0].at[chip], (px, py, c)) for px, py in peers], [(cins[0], couts[0].at[chip])])

    return _Exchange([chip_sum], [jax.ShapeDtypeStruct((N_CHIPS,) + chip_sum.shape, chip_sum.dtype)], plan, 3, 1)


def _add2(a, b, name):
    def body(a_ref, b_ref, o_ref):
        o_ref[...] = a_ref[...] + b_ref[...]

    return pl.pallas_call(body, name=name, in_specs=[VMEM_WHOLE] * 2, out_specs=VMEM_WHOLE,
                          out_shape=jax.ShapeDtypeStruct(a.shape, F32))(a, b)


def _sum_chips(parts, name):
    def body(p_ref, o_ref):
        o_ref[...] = ((p_ref[0] + p_ref[1]) + p_ref[2]) + p_ref[3]

    return pl.pallas_call(body, name=name, in_specs=[VMEM_WHOLE], out_specs=VMEM_WHOLE,
                          out_shape=jax.ShapeDtypeStruct(parts.shape[1:], F32))(parts)


def _add_halves(grad, other, core, name):
    n_shards, _, rows, cols = grad.shape
    tr = _tile(rows, max(16, (1 << 22) // (4 * cols) // 16 * 16), mult=16)

    def body(core_ref, g_ref, o_ref, out_ref):
        del core_ref
        out_ref[...] = (g_ref[...] + o_ref[...]).astype(BF16)

    return pl.pallas_call(
        body,
        name=name,
        grid_spec=pltpu.PrefetchScalarGridSpec(
            num_scalar_prefetch=1,
            grid=(n_shards, rows // tr),
            in_specs=[
                pl.BlockSpec((None, None, tr, cols), lambda s, i, core_ref: (s, core_ref[0], i, 0)),
                pl.BlockSpec((None, tr, cols), lambda s, i, core_ref: (s, i, 0)),
            ],
            out_specs=pl.BlockSpec((None, tr, cols), lambda s, i, core_ref: (s, i, 0)),
        ),
        out_shape=jax.ShapeDtypeStruct((n_shards, rows, cols), BF16),
        compiler_params=_params("arbitrary", "arbitrary"),
    )(core, grad, other)


def _inproj_fwd(x2, g_row, w_l, layer, ride=None):
    n = x2.shape[0]
    tm = _tile(n, 2048)

    def body(x_ref, g_ref, w_ref, p_ref, h_ref):
        @pl.when(pl.program_id(1) == 0)
        def _():
            xv = x_ref[...]
            r = lax.rsqrt(_lanemean(xv * xv) + RMS_EPS)
            h_ref[...] = (xv * r * g_ref[...]).astype(BF16)

        p_ref[...] = _dot(h_ref[...], w_ref[...]).astype(BF16)

    return _host_call(
        body,
        name=f"inproj_fwd_l{layer}",
        grid=(n // tm, N_COL_TILES),
        in_specs=[
            pl.BlockSpec((tm, D_MODEL), lambda i, j: (i, 0)),
            pl.BlockSpec((1, D_MODEL), lambda i, j: (0, 0)),
            pl.BlockSpec((None, D_MODEL, COL_TILE), lambda i, j: (j // 2, 0, j % 2)),
        ],
        out_specs=[
            pl.BlockSpec((tm, COL_TILE), lambda i, j: (i, j)),
            pl.BlockSpec((tm, D_MODEL), lambda i, j: (i, 0)),
        ],
        out_shape=[jax.ShapeDtypeStruct((n, IN_COLS), BF16), jax.ShapeDtypeStruct((n, D_MODEL), BF16)],
        args=[x2, g_row, w_l],
        ride=ride,
    )


def _layer_norm_parts(v, g, b):
    mu = _lanemean(v)
    d = v - mu
    rstd = lax.rsqrt(_lanemean(d * d) + LN_EPS)
    xh = d * rstd
    return xh, rstd, xh * g + b


def _window_sum(ref, slot, weight_row, n_taps, base, t_rows):
    total = None
    for b in range(8):
        group = None
        for a in range((base + n_taps - 1) // 8 + 1):
            k = 8 * a + b - base
            if 0 <= k < n_taps:
                term = weight_row(k) * ref[slot, 8 * a:8 * a + t_rows + 8, :]
                group = term if group is None else group + term
        if group is not None:
            part = group[b:b + t_rows, :]
            total = part if total is None else total + part
    return total


def _tap_grads(acc_ref, pad_ref, shift_ref, dy, ref, slot, n_taps, base, t_rows):
    pad_ref[0:8, :] = jnp.zeros((8, BRANCH_W), F32)
    pad_ref[8:8 + t_rows, :] = dy
    pad_ref[8 + t_rows:16 + t_rows, :] = jnp.zeros((8, BRANCH_W), F32)
    for b in range(8):
        taps = [(a, 8 * a + b - base) for a in range((base + n_taps - 1) // 8 + 1) if 0 <= 8 * a + b - base < n_taps]
        if not taps:
            continue
        shift_ref[...] = pad_ref[pl.ds(8 - b, t_rows + 8), :]
        for a, k in taps:
            prod = shift_ref[...] * ref[slot, 8 * a:8 * a + t_rows + 8, :]
            acc_ref[k] += prod.reshape((t_rows + 8) // 8, 8, BRANCH_W).sum(axis=0)


def _trailing_window_sums(ext_ref, pool_ref, t_rows):
    lo, n = HALO - 16, t_rows + 16
    pool_ref[0:16, :] = jnp.zeros((16, BRANCH_W), F32)
    sums = []
    for j, win in enumerate(POOL_WINDOWS):
        cols = slice(j * GROUP_W, (j + 1) * GROUP_W)
        v = ext_ref[0, lo:lo + n, cols] + ext_ref[0, pl.ds(lo - 1, n), cols]
        width = 2
        while width < win:
            pool_ref[16:16 + n, cols] = v
            v = v + pool_ref[pl.ds(16 - width, n), cols]
            width *= 2
        sums.append(v[16:, :])
    return sums


def _leading_window_sums(nxt_ref, pool_ref, t_rows):
    n = t_rows + 16
    pool_ref[n:n + 16, :] = jnp.zeros((16, BRANCH_W), F32)
    sums = []
    for j, win in enumerate(POOL_WINDOWS):
        cols = slice(j * GROUP_W, (j + 1) * GROUP_W)
        v = nxt_ref[0, 0:n, cols] + nxt_ref[0, pl.ds(1, n), cols]
        width = 2
        while width < win:
            pool_ref[0:n, cols] = v
            v = v + pool_ref[pl.ds(width, n), cols]
            width *= 2
        sums.append(v[0:t_rows, :])
    return sums


def _tril_mask():
    r = lax.broadcasted_iota(jnp.int32, (CHUNK, CHUNK), 0)
    c = lax.broadcasted_iota(jnp.int32, (CHUNK, CHUNK), 1)
    return r >= c


def _lanes(refs4):
    return jnp.concatenate([refs4[g] for g in range(N_CHIPS)], axis=1)


def _mixer_weight_specs():
    def whole(shape):
        nd = len(shape)
        return pl.BlockSpec(shape, lambda b, c: (0,) * nd)

    return [
        whole((N_GROUPS, GROUP_W, GROUP_W)),
        whole((1, BRANCH_W)),
        whole((N_CHIPS, CONV_ROWS, GROUP_W)),
        whole((1, BRANCH_W)),
        whole((1, BRANCH_W)),
        whole((1, BRANCH_W)),
        whole((1, BRANCH_W)),
        whole((1, BRANCH_W)),
        whole((N_GROUPS, CHUNK, CHUNK)),
        whole((CHUNK, BRANCH_W)),
        whole((N_CHIPS, SHORT_ROWS, GROUP_W)),
    ]


def _mixer_weight_args(mw):
    return [mw["pool_w"], mw["pool_scale"], mw["conv_w"], mw["conv_b"], mw["conv_ln_g"], mw["conv_ln_b"],
            mw["sgu_ln_g"], mw["sgu_ln_b"], mw["sgu_w"], mw["sgu_bias"], mw["sc_w"]]


def _mixers_fwd(proj3, mw, layer, ride=None):
    nb, seq, _ = proj3.shape
    t_rows = _tile(seq, 512)
    nc = seq // t_rows
    hb = t_rows // HALO

    def body(cur_ref, halo_ref, pw_ref, ps_ref, cw_ref, cb_ref, clg_ref, clb_ref, slg_ref, slb_ref, sw_ref,
             sbias_ref, scw_ref, z_ref, cacc_ref, ext_ref, pool_ref):
        c = pl.program_id(1)
        has_prev = c > 0
        row = lax.broadcasted_iota(jnp.int32, (t_rows, 1), 0)
        tpos = (c * t_rows + row + 1).astype(F32)

        def cur(k):
            return cur_ref[:, k * BRANCH_W:(k + 1) * BRANCH_W].astype(F32)

        def hal(k):
            return halo_ref[:, k * BRANCH_W:(k + 1) * BRANCH_W].astype(F32)

        def put_ext(slot, halo_val, cur_val):
            ext_ref[slot, 0:HALO, :] = jnp.where(has_prev, halo_val, 0.0)
            ext_ref[slot, HALO:HALO + t_rows, :] = cur_val
            ext_ref[slot, HALO + t_rows:HALO + t_rows + 8, :] = jnp.zeros((8, BRANCH_W), F32)

        px = cur(P_X)
        put_ext(0, hal(P_X), px)
        mixed = []
        window_sums = _trailing_window_sums(ext_ref, pool_ref, t_rows)
        for j, win in enumerate(POOL_WINDOWS):
            cols = slice(j * GROUP_W, (j + 1) * GROUP_W)
            pooled = window_sums[j] / jnp.minimum(tpos, float(win)) - px[:, cols]
            mixed.append(_dot(pooled.astype(BF16), pw_ref[j]))
        z_pool = jnp.concatenate(mixed, axis=1) * ps_ref[...] * _silu(cur(P_GATE))
        z_ref[:, 0:BRANCH_W] = z_pool.astype(BF16)

        put_ext(1, hal(C_A) * _sigmoid(hal(C_B)), cur(C_A) * _sigmoid(cur(C_B)))
        cw = _lanes(cw_ref)
        acc = cb_ref[...] + _window_sum(ext_ref, 1, lambda k: cw[k:k + 1, :], CONV_K, HALO - (CONV_K - 1), t_rows)
        cacc_ref[...] = acc
        _, _, ln = _layer_norm_parts(acc, clg_ref[...], clb_ref[...])
        z_ref[:, BRANCH_W:2 * BRANCH_W] = (_silu(ln) * _silu(cur(C_GATE))).astype(BF16)

        _, _, v = _layer_norm_parts(cur(G_V), slg_ref[...], slb_ref[...])
        vb = v.astype(BF16)
        mask = _tril_mask()
        wt = [jnp.where(mask, sw_ref[g], 0.0).astype(BF16) for g in range(N_GROUPS)]
        sp_rows = []
        for sub in range(t_rows // CHUNK):
            rows = slice(sub * CHUNK, (sub + 1) * CHUNK)
            sp_rows.append(jnp.concatenate(
                [_dot(wt[g], vb[rows, g * GROUP_W:(g + 1) * GROUP_W]) for g in range(N_GROUPS)], axis=1)
                + sbias_ref[...])
        sp = jnp.concatenate(sp_rows, axis=0)
        z_ref[:, 2 * BRANCH_W:3 * BRANCH_W] = (cur(G_U) * sp * _silu(cur(G_GATE))).astype(BF16)

        put_ext(2, hal(S_C) * hal(S_X), cur(S_C) * cur(S_X))
        scw = _lanes(scw_ref)
        cv = _window_sum(ext_ref, 2, lambda k: scw[k:k + 1, :], SHORT_K, HALO - (SHORT_K - 1), t_rows)
        z_ref[:, 3 * BRANCH_W:4 * BRANCH_W] = (cur(S_B) * cv * _silu(cur(S_GATE))).astype(BF16)

    return _host_call(
        body,
        name=f"mixers_fwd_l{layer}",
        grid=(nb, nc),
        in_specs=[
            pl.BlockSpec((None, t_rows, PIECE_COLS), lambda b, c: (b, c, 0)),
            pl.BlockSpec((None, HALO, PIECE_COLS), lambda b, c: (b, jnp.maximum(c * hb - 1, 0), 0)),
        ] + _mixer_weight_specs(),
        out_specs=[pl.BlockSpec((None, t_rows, N_BRANCH * BRANCH_W), lambda b, c: (b, c, 0)),
                   pl.BlockSpec((None, t_rows, BRANCH_W), lambda b, c: (b, c, 0))],
        out_shape=[jax.ShapeDtypeStruct((nb, seq, N_BRANCH * BRANCH_W), BF16),
                   jax.ShapeDtypeStruct((nb, seq, BRANCH_W), F32)],
        scratch_shapes=[pltpu.VMEM((3, HALO + t_rows + 8, BRANCH_W), F32),
                        pltpu.VMEM((t_rows + 32, BRANCH_W), F32)],
        args=[proj3, proj3, *_mixer_weight_args(mw)],
        ride=ride,
    )


MIXER_GRADS = ["pool_w", "pool_scale", "conv_w", "conv_b", "conv_ln_g", "conv_ln_b", "sgu_ln_g", "sgu_ln_b", "sgu_w",
               "sgu_b", "sc_w"]


def _mixers_bwd(proj3, dz3, cacc3, dproj3, mw, layer, ride=None):
    nb, seq, _ = proj3.shape
    t_rows = _tile(seq, 512)
    nc = seq // t_rows
    hb = t_rows // HALO

    def body(cur_ref, halo_ref, dz_ref, cacc_ref, pw_ref, ps_ref, cw_ref, cb_ref, clg_ref, clb_ref, slg_ref, slb_ref,
             sw_ref, sbias_ref, scw_ref, dp_in_ref, dp_ref, g_pw, g_ps, g_cw, g_cb, g_clg, g_clb, g_slg,
             g_slb, g_sw, g_sb, g_scw, ext_ref, nxt_ref, sb_acc, cw_acc, scw_acc, pad_ref, shift_ref,
             pool_ref):
        del dp_in_ref
        b = pl.program_id(0)
        r = pl.program_id(1)
        c = nc - 1 - r
        has_prev = c > 0
        row = lax.broadcasted_iota(jnp.int32, (t_rows, 1), 0)
        tpos = (c * t_rows + row + 1).astype(F32)

        @pl.when(jnp.logical_and(b == 0, r == 0))
        def _():
            for ref in (g_pw, g_ps, g_cb, g_clg, g_clb, g_slg, g_slb, g_sw, sb_acc, cw_acc, scw_acc):
                ref[...] = jnp.zeros_like(ref)

        @pl.when(r == 0)
        def _():
            nxt_ref[:, t_rows:t_rows + HALO, :] = jnp.zeros((3, HALO, BRANCH_W), F32)

        def cur(k):
            return cur_ref[:, k * BRANCH_W:(k + 1) * BRANCH_W].astype(F32)

        def hal(k):
            return halo_ref[:, k * BRANCH_W:(k + 1) * BRANCH_W].astype(F32)

        def dzp(k):
            return dz_ref[:, k * BRANCH_W:(k + 1) * BRANCH_W].astype(F32)

        def put_dp(k, val):
            dp_ref[:, k * BRANCH_W:(k + 1) * BRANCH_W] = val.astype(BF16)

        def put_ext(slot, halo_val, cur_val):
            ext_ref[slot, 0:HALO, :] = jnp.where(has_prev, halo_val, 0.0)
            ext_ref[slot, HALO:HALO + t_rows, :] = cur_val
            ext_ref[slot, HALO + t_rows:HALO + t_rows + 8, :] = jnp.zeros((8, BRANCH_W), F32)

        px = cur(P_X)
        put_ext(0, hal(P_X), px)
        pgate = cur(P_GATE)
        dz_pool = dzp(0)
        ps = ps_ref[...]
        pooled, mixed, cnts = [], [], []
        window_sums = _trailing_window_sums(ext_ref, pool_ref, t_rows)
        for j, win in enumerate(POOL_WINDOWS):
            cols = slice(j * GROUP_W, (j + 1) * GROUP_W)
            cnt = jnp.minimum(tpos, float(win))
            pj = window_sums[j] / cnt - px[:, cols]
            cnts.append(cnt)
            pooled.append(pj.astype(BF16))
            mixed.append(_dot(pooled[j], pw_ref[j]))
        mixed = jnp.concatenate(mixed, axis=1)
        pg_silu, pg_dsilu = _silu_pair(pgate)
        put_dp(P_GATE, dz_pool * (mixed * ps) * pg_dsilu)
        d_out = dz_pool * pg_silu
        g_ps[...] += _rowsum(d_out * mixed)
        d_mixed = (d_out * ps).astype(BF16)
        d_pooled = []
        for j in range(N_GROUPS):
            cols = slice(j * GROUP_W, (j + 1) * GROUP_W)
            g_pw[j] += _dot_tn(pooled[j], d_mixed[:, cols])
            dpj = _dot_nt(d_mixed[:, cols], pw_ref[j])
            d_pooled.append(dpj)
            nxt_ref[0, 0:t_rows, cols] = dpj / cnts[j]
        lead_sums = _leading_window_sums(nxt_ref, pool_ref, t_rows)
        put_dp(P_X, jnp.concatenate([lead_sums[j] - d_pooled[j] for j in range(N_GROUPS)], axis=1))

        c_a = cur(C_A)
        sig_b = _sigmoid(cur(C_B))
        put_ext(1, hal(C_A) * _sigmoid(hal(C_B)), c_a * sig_b)
        cw = _lanes(cw_ref)
        xh, rstd, ln = _layer_norm_parts(cacc_ref[...], clg_ref[...], clb_ref[...])
        cgate = cur(C_GATE)
        dz_conv = dzp(1)
        ln_silu, ln_dsilu = _silu_pair(ln)
        cg_silu, cg_dsilu = _silu_pair(cgate)
        put_dp(C_GATE, dz_conv * ln_silu * cg_dsilu)
        d_ln = dz_conv * cg_silu * ln_dsilu
        g_clg[...] += _rowsum(d_ln * xh)
        g_clb[...] += _rowsum(d_ln)
        dxh = d_ln * clg_ref[...]
        dc = rstd * (dxh - _lanemean(dxh) - xh * _lanemean(dxh * xh))
        g_cb[...] += _rowsum(dc)
        nxt_ref[1, 0:t_rows, :] = dc
        _tap_grads(cw_acc, pad_ref, shift_ref, dc, ext_ref, 1, CONV_K, HALO - (CONV_K - 1), t_rows)
        dyg = _window_sum(nxt_ref, 1, lambda i: cw[CONV_K - 1 - i:CONV_K - i, :], CONV_K, 0, t_rows)
        put_dp(C_A, dyg * sig_b)
        put_dp(C_B, dyg * c_a * sig_b * (1.0 - sig_b))

        u = cur(G_U)
        ggate = cur(G_GATE)
        vxh, vrstd, v = _layer_norm_parts(cur(G_V), slg_ref[...], slb_ref[...])
        vb = v.astype(BF16)
        mask = _tril_mask()
        wt = [jnp.where(mask, sw_ref[g], 0.0) for g in range(N_GROUPS)]
        wt_b = [w.astype(BF16) for w in wt]
        wtt_b = [w.T.astype(BF16) for w in wt]
        dz_sgu = dzp(2)
        gg_silu, gg_dsilu = _silu_pair(ggate)
        d_sgu = dz_sgu * gg_silu
        d_sp = d_sgu * u
        d_spb = d_sp.astype(BF16)
        sp_rows, dv_rows = [], []
        for sub in range(t_rows // CHUNK):
            rows = slice(sub * CHUNK, (sub + 1) * CHUNK)
            sp_g, dv_g = [], []
            for g in range(N_GROUPS):
                cols = slice(g * GROUP_W, (g + 1) * GROUP_W)
                sp_g.append(_dot(wt_b[g], vb[rows, cols]))
                g_sw[g] += jnp.where(mask, _dot_nt(d_spb[rows, cols], vb[rows, cols]), 0.0)
                dv_g.append(_dot(wtt_b[g], d_spb[rows, cols]))
            sp_rows.append(jnp.concatenate(sp_g, axis=1) + sbias_ref[...])
            dv_rows.append(jnp.concatenate(dv_g, axis=1))
            sb_acc[...] += d_sp[rows, :]
        sp = jnp.concatenate(sp_rows, axis=0)
        dv = jnp.concatenate(dv_rows, axis=0)
        put_dp(G_GATE, dz_sgu * (u * sp) * gg_dsilu)
        put_dp(G_U, d_sgu * sp)
        g_slg[...] += _rowsum(dv * vxh)
        g_slb[...] += _rowsum(dv)
        dvx = dv * slg_ref[...]
        put_dp(G_V, vrstd * (dvx - _lanemean(dvx) - vxh * _lanemean(dvx * vxh)))

        s_b, s_c, s_x, sgate = cur(S_B), cur(S_C), cur(S_X), cur(S_GATE)
        put_ext(2, hal(S_C) * hal(S_X), s_c * s_x)
        scw = _lanes(scw_ref)
        cv = _window_sum(ext_ref, 2, lambda k: scw[k:k + 1, :], SHORT_K, HALO - (SHORT_K - 1), t_rows)
        dz_sc = dzp(3)
        sg_silu, sg_dsilu = _silu_pair(sgate)
        put_dp(S_GATE, dz_sc * (s_b * cv) * sg_dsilu)
        d_pre = dz_sc * sg_silu
        put_dp(S_B, d_pre * cv)
        dcv = d_pre * s_b
        nxt_ref[2, 0:t_rows, :] = dcv
        _tap_grads(scw_acc, pad_ref, shift_ref, dcv, ext_ref, 2, SHORT_K, HALO - (SHORT_K - 1), t_rows)
        du = _window_sum(nxt_ref, 2, lambda i: scw[SHORT_K - 1 - i:SHORT_K - i, :], SHORT_K, 0, t_rows)
        put_dp(S_C, du * s_x)
        put_dp(S_X, du * s_c)

        nxt_ref[:, t_rows:t_rows + HALO, :] = nxt_ref[:, 0:HALO, :]

        @pl.when(jnp.logical_and(b == nb - 1, r == nc - 1))
        def _():
            lane = lax.broadcasted_iota(jnp.int32, (CHUNK, GROUP_W), 1)
            out = jnp.zeros((CHUNK, GROUP_W), F32)
            for g in range(N_GROUPS):
                col = jnp.sum(sb_acc[:, g * GROUP_W:(g + 1) * GROUP_W], axis=1, keepdims=True)
                out = jnp.where(lane == g, col, out)
            g_sb[...] = out
            g_cw[...] = jnp.sum(cw_acc[...], axis=1)
            g_scw[...] = jnp.sum(scw_acc[...], axis=1)

    def acc_spec(shape):
        nd = len(shape)
        return pl.BlockSpec(shape, lambda b, r: (0,) * nd)

    acc_shapes = [
        (N_GROUPS, GROUP_W, GROUP_W),
        (1, BRANCH_W),
        (CONV_ROWS, BRANCH_W),
        (1, BRANCH_W),
        (1, BRANCH_W),
        (1, BRANCH_W),
        (1, BRANCH_W),
        (1, BRANCH_W),
        (N_GROUPS, CHUNK, CHUNK),
        (CHUNK, GROUP_W),
        (SHORT_ROWS, BRANCH_W),
    ]
    outs, rode = _host_call(
        body,
        name=f"mixers_bwd_l{layer}",
        grid=(nb, nc),
        in_specs=[
            pl.BlockSpec((None, t_rows, PIECE_COLS), lambda b, r: (b, nc - 1 - r, 0)),
            pl.BlockSpec((None, HALO, PIECE_COLS), lambda b, r: (b, jnp.maximum((nc - 1 - r) * hb - 1, 0), 0)),
            pl.BlockSpec((None, t_rows, N_BRANCH * BRANCH_W), lambda b, r: (b, nc - 1 - r, 0)),
            pl.BlockSpec((None, t_rows, BRANCH_W), lambda b, r: (b, nc - 1 - r, 0)),
        ] + _mixer_weight_specs() + [ANY],
        out_specs=[pl.BlockSpec((None, t_rows, PIECE_COLS), lambda b, r: (b, nc - 1 - r, 0))]
        + [acc_spec(s) for s in acc_shapes],
        out_shape=[jax.ShapeDtypeStruct(dproj3.shape, BF16)] + [jax.ShapeDtypeStruct(s, F32) for s in acc_shapes],
        scratch_shapes=[
            pltpu.VMEM((3, HALO + t_rows + 8, BRANCH_W), F32),
            pltpu.VMEM((3, t_rows + HALO, BRANCH_W), F32),
            pltpu.VMEM((CHUNK, BRANCH_W), F32),
            pltpu.VMEM((CONV_ROWS, 8, BRANCH_W), F32),
            pltpu.VMEM((SHORT_ROWS, 8, BRANCH_W), F32),
            pltpu.VMEM((t_rows + 16, BRANCH_W), F32),
            pltpu.VMEM((t_rows + 8, BRANCH_W), F32),
            pltpu.VMEM((t_rows + 32, BRANCH_W), F32),
        ],
        aliases={4 + 11: 0},
        args=[proj3, proj3, dz3, cacc3, *_mixer_weight_args(mw), dproj3],
        ride=ride,
    )
    return outs[0], dict(zip(MIXER_GRADS, outs[1:])), rode


def _merge_fwd(z2, proj2, x2, wbr_l, wo_l, layer, ride=None):
    n = x2.shape[0]
    tm = _tile(n, 1024)

    def body(z_ref, gate_ref, x_ref, wbr_ref, wo_ref, xn_ref, mg_ref, acc_ref):
        nbr = pl.program_id(1)
        bo = _dot(z_ref[...], wbr_ref[nbr])
        contrib = _sigmoid(gate_ref[...].astype(F32)) * bo

        @pl.when(nbr == 0)
        def _():
            acc_ref[...] = contrib

        @pl.when(nbr > 0)
        def _():
            acc_ref[...] += contrib

        @pl.when(nbr == N_BRANCH - 1)
        def _():
            mg = acc_ref[...].astype(BF16)
            mg_ref[...] = mg
            xn_ref[...] = x_ref[...] + _dot(mg, wo_ref[...])

    return _host_call(
        body,
        name=f"merge_fwd_l{layer}",
        grid=(n // tm, N_BRANCH),
        in_specs=[
            pl.BlockSpec((tm, BRANCH_W), lambda i, b: (i, b)),
            pl.BlockSpec((tm, D_MODEL), lambda i, b: (i, GATE_BLOCK0 + b)),
            pl.BlockSpec((tm, D_MODEL), lambda i, b: (i, 0)),
            pl.BlockSpec((N_BRANCH, BRANCH_W, D_MODEL), lambda i, b: (0, 0, 0)),
            pl.BlockSpec((D_MODEL, D_MODEL), lambda i, b: (0, 0)),
        ],
        out_specs=[
            pl.BlockSpec((tm, D_MODEL), lambda i, b: (i, 0)),
            pl.BlockSpec((tm, D_MODEL), lambda i, b: (i, 0)),
        ],
        out_shape=[jax.ShapeDtypeStruct((n, D_MODEL), F32), jax.ShapeDtypeStruct((n, D_MODEL), BF16)],
        scratch_shapes=[pltpu.VMEM((tm, D_MODEL), F32)],
        args=[z2, proj2, x2, wbr_l, wo_l],
        ride=ride,
    )


def _merge_fwd_head(z2, proj2, x2, wbr_l, wo_l, g_row, tgt2, layer):
    n = x2.shape[0]
    tm = _tile(n, 1024)

    def body(z_ref, gate_ref, x_ref, wbr_ref, wo_ref, g_ref, t_ref, dx_ref, mg_ref, loss_ref, dg_ref, acc_ref):
        i = pl.program_id(0)
        nbr = pl.program_id(1)
        bo = _dot(z_ref[...], wbr_ref[nbr])
        contrib = _sigmoid(gate_ref[...].astype(F32)) * bo

        @pl.when(jnp.logical_and(i == 0, nbr == 0))
        def _():
            loss_ref[...] = jnp.zeros_like(loss_ref)
            dg_ref[...] = jnp.zeros_like(dg_ref)

        @pl.when(nbr == 0)
        def _():
            acc_ref[...] = contrib

        @pl.when(nbr > 0)
        def _():
            acc_ref[...] += contrib

        @pl.when(nbr == N_BRANCH - 1)
        def _():
            mg = acc_ref[...].astype(BF16)
            mg_ref[...] = mg
            xv = x_ref[...] + _dot(mg, wo_ref[...])
            g = g_ref[...]
            r = lax.rsqrt(_lanemean(xv * xv) + RMS_EPS)
            xh = xv * r
            err = xh * g - t_ref[...]
            loss_ref[...] += 0.5 * jnp.sum(_lanemean(err * err), axis=0, keepdims=True)
            dy = err * (1.0 / D_MODEL)
            dg_ref[...] += _rowsum(dy * xh)
            dxh = dy * g
            dx_ref[...] = r * (dxh - xh * _lanemean(dxh * xh))

    outs, _ = _host_call(
        body,
        name=f"merge_fwd_head_l{layer}",
        grid=(n // tm, N_BRANCH),
        in_specs=[
            pl.BlockSpec((tm, BRANCH_W), lambda i, b: (i, b)),
            pl.BlockSpec((tm, D_MODEL), lambda i, b: (i, GATE_BLOCK0 + b)),
            pl.BlockSpec((tm, D_MODEL), lambda i, b: (i, 0)),
            pl.BlockSpec((N_BRANCH, BRANCH_W, D_MODEL), lambda i, b: (0, 0, 0)),
            pl.BlockSpec((D_MODEL, D_MODEL), lambda i, b: (0, 0)),
            pl.BlockSpec((1, D_MODEL), lambda i, b: (0, 0)),
            pl.BlockSpec((tm, D_MODEL), lambda i, b: (i, 0)),
        ],
        out_specs=[
            pl.BlockSpec((tm, D_MODEL), lambda i, b: (i, 0)),
            pl.BlockSpec((tm, D_MODEL), lambda i, b: (i, 0)),
            pl.BlockSpec((1, GROUP_W), lambda i, b: (0, 0)),
            pl.BlockSpec((1, D_MODEL), lambda i, b: (0, 0)),
        ],
        out_shape=[jax.ShapeDtypeStruct((n, D_MODEL), F32), jax.ShapeDtypeStruct((n, D_MODEL), BF16),
                   jax.ShapeDtypeStruct((1, GROUP_W), F32), jax.ShapeDtypeStruct((1, D_MODEL), F32)],
        scratch_shapes=[pltpu.VMEM((tm, D_MODEL), F32)],
        args=[z2, proj2, x2, wbr_l, wo_l, g_row, tgt2],
    )
    return outs


def _merge_bwd(dout2, merged2, z2, proj2, wbr_l, wo_l, layer, ride=None):
    n = dout2.shape[0]
    tm = _tile(n, 1024)
    nt = n // tm

    def body(do_ref, mg_ref, z_ref, gate_ref, wbr_ref, wo_ref, dz_ref, dg_ref, gwo_ref, gwbr_ref, dm_ref,
             awo_ref, awbr_ref):
        i = pl.program_id(0)
        nbr = pl.program_id(1)

        @pl.when(nbr == 0)
        def _():
            do_b = do_ref[...].astype(BF16)
            dm_ref[...] = _dot_nt(do_b, wo_ref[...])
            gw = _dot_tn(mg_ref[...], do_b)

            @pl.when(i == 0)
            def _():
                awo_ref[...] = gw

            @pl.when(i > 0)
            def _():
                awo_ref[...] += gw

        zt = z_ref[...]
        wbr = wbr_ref[nbr]
        bo = _dot(zt, wbr)
        gt = _sigmoid(gate_ref[...].astype(F32))
        dm = dm_ref[...]
        dbo = (dm * gt).astype(BF16)
        dg_ref[...] = (dm * bo * gt * (1.0 - gt)).astype(BF16)
        dz_ref[...] = _dot_nt(dbo, wbr).astype(BF16)
        gw = _dot_tn(zt, dbo)

        @pl.when(i == 0)
        def _():
            awbr_ref[nbr] = gw

        @pl.when(i > 0)
        def _():
            awbr_ref[nbr] += gw

        @pl.when(jnp.logical_and(i == nt - 1, nbr == N_BRANCH - 1))
        def _():
            pltpu.sync_copy(awo_ref, gwo_ref)
            pltpu.sync_copy(awbr_ref, gwbr_ref)

    return _host_call(
        body,
        name=f"merge_bwd_l{layer}",
        grid=(nt, N_BRANCH),
        in_specs=[
            pl.BlockSpec((tm, D_MODEL), lambda i, b: (i, 0)),
            pl.BlockSpec((tm, D_MODEL), lambda i, b: (i, 0)),
            pl.BlockSpec((tm, BRANCH_W), lambda i, b: (i, b)),
            pl.BlockSpec((tm, D_MODEL), lambda i, b: (i, GATE_BLOCK0 + b)),
            pl.BlockSpec((N_BRANCH, BRANCH_W, D_MODEL), lambda i, b: (0, 0, 0)),
            pl.BlockSpec((D_MODEL, D_MODEL), lambda i, b: (0, 0)),
        ],
        out_specs=[
            pl.BlockSpec((tm, BRANCH_W), lambda i, b: (i, b)),
            pl.BlockSpec((tm, D_MODEL), lambda i, b: (i, GATE_BLOCK0 + b)),
            ANY,
            ANY,
        ],
        out_shape=[
            jax.ShapeDtypeStruct((n, N_BRANCH * BRANCH_W), BF16),
            jax.ShapeDtypeStruct((n, IN_COLS), BF16),
            jax.ShapeDtypeStruct((D_MODEL, D_MODEL), F32),
            jax.ShapeDtypeStruct((N_BRANCH, BRANCH_W, D_MODEL), F32),
        ],
        scratch_shapes=[
            pltpu.VMEM((tm, D_MODEL), F32),
            pltpu.VMEM((D_MODEL, D_MODEL), F32),
            pltpu.VMEM((N_BRANCH, BRANCH_W, D_MODEL), F32),
        ],
        args=[dout2, merged2, z2, proj2, wbr_l, wo_l],
        ride=ride,
    )


def _inproj_bwd_x(dproj2, w_l, x2, g_row, dout2, layer, ride=None):
    n = x2.shape[0]
    tm = _tile(n, 1024)

    def body(dp_ref, w_ref, x_ref, g_ref, do_ref, dx_ref, dng_ref, dh_ref):
        i = pl.program_id(0)
        s = pl.program_id(1)
        part = _dot_nt(dp_ref[...], w_ref[...])

        @pl.when(s == 0)
        def _():
            dh_ref[...] = part

        @pl.when(s > 0)
        def _():
            dh_ref[...] += part

        @pl.when(jnp.logical_and(i == 0, s == 0))
        def _():
            dng_ref[...] = jnp.zeros_like(dng_ref)

        @pl.when(s == N_CHIPS - 1)
        def _():
            xv = x_ref[...]
            r = lax.rsqrt(_lanemean(xv * xv) + RMS_EPS)
            xh = xv * r
            dh = dh_ref[...]
            dng_ref[...] += _rowsum(dh * xh)
            dxh = dh * g_ref[...]
            dx_ref[...] = do_ref[...] + r * (dxh - xh * _lanemean(dxh * xh))

    return _host_call(
        body,
        name=f"inproj_bwd_x_l{layer}",
        grid=(n // tm, N_CHIPS),
        in_specs=[
            pl.BlockSpec((tm, SHARD_COLS), lambda i, s: (i, s)),
            pl.BlockSpec((None, D_MODEL, SHARD_COLS), lambda i, s: (s, 0, 0)),
            pl.BlockSpec((tm, D_MODEL), lambda i, s: (i, 0)),
            pl.BlockSpec((1, D_MODEL), lambda i, s: (0, 0)),
            pl.BlockSpec((tm, D_MODEL), lambda i, s: (i, 0)),
        ],
        out_specs=[
            pl.BlockSpec((tm, D_MODEL), lambda i, s: (i, 0)),
            pl.BlockSpec((1, D_MODEL), lambda i, s: (0, 0)),
        ],
        out_shape=[jax.ShapeDtypeStruct((n, D_MODEL), F32), jax.ShapeDtypeStruct((1, D_MODEL), F32)],
        scratch_shapes=[pltpu.VMEM((tm, D_MODEL), F32)],
        args=[dproj2, w_l, x2, g_row, dout2],
        ride=ride,
    )


def _inproj_bwd_w(h2, dproj2, layer, ride=None):
    n = h2.shape[0]
    tm = _tile(n, 2048)

    def body(h_ref, dp_ref, gw_ref):
        gw = _dot_tn(h_ref[...], dp_ref[...])

        @pl.when(pl.program_id(1) == 0)
        def _():
            gw_ref[...] = gw

        @pl.when(pl.program_id(1) > 0)
        def _():
            gw_ref[...] += gw

    outs, rode = _host_call(
        body,
        name=f"inproj_bwd_w_l{layer}",
        grid=(N_COL_TILES, n // tm),
        in_specs=[
            pl.BlockSpec((tm, D_MODEL), lambda s, i: (i, 0)),
            pl.BlockSpec((tm, COL_TILE), lambda s, i: (i, s)),
        ],
        out_specs=[pl.BlockSpec((None, D_MODEL, COL_TILE), lambda s, i: (s // 2, 0, s % 2))],
        out_shape=[jax.ShapeDtypeStruct((N_CHIPS, D_MODEL, SHARD_COLS), F32)],
        args=[h2, dproj2],
        ride=ride,
    )
    return outs[0], rode


def _adamw_math(w, g, m, v):
    m = ADAM_B1 * m + (1.0 - ADAM_B1) * g
    v = ADAM_B2 * v + (1.0 - ADAM_B2) * jnp.square(g)
    m_hat = m / (1.0 - ADAM_B1 ** ADAM_STEP)
    v_hat = v / (1.0 - ADAM_B2 ** ADAM_STEP)
    delta = -ADAM_LR * (m_hat / (jnp.sqrt(v_hat) + ADAM_EPS) + ADAM_WD * w)
    return delta, m, v


def _adamw_sharded(w, m, v, part, layer, prev, name, ride=None, after=None):
    assert prev is None or after is None
    _, rows, cols = w.shape
    tr = _tile(rows, max(16, (1 << 21) // (4 * cols) // 16 * 16), mult=16)

    def body(w_ref, m_ref, v_ref, p_ref, *rest):
        g_out, d_out, m_out, v_out = rest[-4:]
        g = ((p_ref[0].astype(F32) + p_ref[1].astype(F32)) + p_ref[2].astype(F32)) + p_ref[3].astype(F32)
        delta, m2, v2 = _adamw_math(w_ref[...], g, m_ref[...], v_ref[...])
        g_out[...] = g
        d_out[...] = delta
        m_out[...] = m2
        v_out[...] = v2

    spec = pl.BlockSpec((None, tr, cols), lambda i: (layer, i, 0))
    return _host_call(
        body,
        name=f"adamw_{name}_l{layer}",
        grid=(rows // tr,),
        in_specs=[spec, spec, spec, pl.BlockSpec((N_CHIPS, tr, cols), lambda i: (0, i, 0))]
        + ([ANY] * 4 if prev else []) + ([ANY] if after is not None else []),
        out_specs=[spec] * 4,
        out_shape=[jax.ShapeDtypeStruct(w.shape, F32)] * 4,
        args=[w, m, v, part] + (list(prev) if prev else []) + ([after] if after is not None else []),
        aliases={4 + k: k for k in range(4)} if prev else {},
        ride=ride,
    )


def _adamw_packed(w, m, v, g):
    rows = w.shape[0]
    tr = _tile(rows, rows // 2 if rows % 16 == 0 else rows)

    def body(w_ref, m_ref, v_ref, g_ref, d_out, m_out, v_out):
        delta, m2, v2 = _adamw_math(w_ref[...], g_ref[...], m_ref[...], v_ref[...])
        d_out[...] = delta
        m_out[...] = m2
        v_out[...] = v2

    spec = pl.BlockSpec((tr, GROUP_W), lambda i: (i, 0))
    return pl.pallas_call(
        body,
        name="adamw_small",
        grid=(rows // tr,),
        in_specs=[spec] * 4,
        out_specs=[spec] * 3,
        out_shape=[jax.ShapeDtypeStruct(w.shape, F32)] * 3,
        compiler_params=_params("arbitrary"),
    )(w, m, v, g)


SMALL = ["norm_g", "pool_w", "pool_scale", "conv_b", "conv_ln_g", "conv_ln_b", "sgu_ln_g", "sgu_ln_b", "sgu_w",
         "sgu_b", "final_g"]
WEIGHTS = ["norm_g", "w_in", "pool_w", "pool_scale", "conv_w", "conv_b", "conv_ln_g", "conv_ln_b", "sgu_ln_g",
           "sgu_ln_b", "sgu_w", "sgu_b", "sc_w", "w_branch", "w_o", "final_g"]
HALF_SHAPES = [(D_MODEL // 2, SHARD_COLS), (N_BRANCH * BRANCH_W // 2, BR_SHARD), (BR_SHARD // 2, D_MODEL),
               (CS_ROWS // 2, GROUP_W)]


def _pack_small(tree, last_rows=None):
    tail = jnp.zeros((8, GROUP_W), F32) if last_rows is None else last_rows
    return jnp.concatenate([tree[k].reshape(-1, GROUP_W) for k in SMALL] + [tail], axis=0)


def _unpack_small(packed, like):
    out, r = {}, 0
    for k in SMALL:
        nr = like[k].size // GROUP_W
        out[k] = packed[r:r + nr].reshape(like[k].shape)
        r += nr
    return out


def _pad_rows(a, rows):
    pad = [(0, 0)] * a.ndim
    pad[-2] = (0, rows - a.shape[-2])
    return jnp.pad(a, pad)


def _pack_cs(conv, short):
    return jnp.concatenate([_pad_rows(conv, CS_ROWS // 2), _pad_rows(short, CS_ROWS // 2)], axis=-2)


def _shard_major_rows(a):
    return a.reshape(a.shape[0], N_CHIPS, GROUP_W).transpose(1, 0, 2)


def kernel(x, norm_g, w_in, pool_w, pool_scale, conv_w, conv_b, conv_ln_g, conv_ln_b, sgu_ln_g, sgu_ln_b, sgu_w, sgu_b, sc_w, w_branch, w_o, final_g, loss_target, m_norm_g, m_w_in, m_pool_w, m_pool_scale, m_conv_w, m_conv_b, m_conv_ln_g, m_conv_ln_b, m_sgu_ln_g, m_sgu_ln_b, m_sgu_w, m_sgu_b, m_sc_w, m_w_branch, m_w_o, m_final_g, v_norm_g, v_w_in, v_pool_w, v_pool_scale, v_conv_w, v_conv_b, v_conv_ln_g, v_conv_ln_b, v_sgu_ln_g, v_sgu_ln_b, v_sgu_w, v_sgu_b, v_sc_w, v_w_branch, v_w_o, v_final_g):
    w = dict(norm_g=norm_g, w_in=w_in, pool_w=pool_w, pool_scale=pool_scale, conv_w=conv_w, conv_b=conv_b,
             conv_ln_g=conv_ln_g, conv_ln_b=conv_ln_b, sgu_ln_g=sgu_ln_g, sgu_ln_b=sgu_ln_b, sgu_w=sgu_w,
             sgu_b=sgu_b, sc_w=sc_w, w_branch=w_branch, w_o=w_o, final_g=final_g)
    mom = dict(norm_g=m_norm_g, w_in=m_w_in, pool_w=m_pool_w, pool_scale=m_pool_scale, conv_w=m_conv_w,
               conv_b=m_conv_b, conv_ln_g=m_conv_ln_g, conv_ln_b=m_conv_ln_b, sgu_ln_g=m_sgu_ln_g,
               sgu_ln_b=m_sgu_ln_b, sgu_w=m_sgu_w, sgu_b=m_sgu_b, sc_w=m_sc_w, w_branch=m_w_branch, w_o=m_w_o,
               final_g=m_final_g)
    var = dict(norm_g=v_norm_g, w_in=v_w_in, pool_w=v_pool_w, pool_scale=v_pool_scale, conv_w=v_conv_w,
               conv_b=v_conv_b, conv_ln_g=v_conv_ln_g, conv_ln_b=v_conv_ln_b, sgu_ln_g=v_sgu_ln_g,
               sgu_ln_b=v_sgu_ln_b, sgu_w=v_sgu_w, sgu_b=v_sgu_b, sc_w=v_sc_w, w_branch=v_w_branch, w_o=v_w_o,
               final_g=v_final_g)

    nb, seq, _ = x.shape
    n = nb * seq
    core = lax.axis_index("c").astype(jnp.int32).reshape(1)

    win_s = w_in.astype(BF16).reshape(DEPTH, 2, D_MODEL // 2, SHARD_COLS)
    wbr_s = w_branch.astype(BF16)
    wo_s = w_o.astype(BF16)
    cs_s = _pack_cs(conv_w, sc_w)

    (win_f0,) = _allgather_layer0([win_s])

    def layer_weights(win_f, wbr_f, wo_f, cs_f):
        cs_f = cs_f.reshape(N_CHIPS, CS_ROWS, GROUP_W)
        return dict(win=win_f.reshape(N_CHIPS, D_MODEL, SHARD_COLS),
                    wbr=wbr_f.reshape(N_CHIPS, N_BRANCH, BRANCH_W, BR_SHARD).transpose(1, 2, 0, 3).reshape(
                        N_BRANCH, BRANCH_W, D_MODEL),
                    wo=wo_f.reshape(D_MODEL, D_MODEL),
                    conv_w=cs_f[:, :CONV_ROWS], sc_w=cs_f[:, CS_ROWS // 2:CS_ROWS // 2 + SHORT_ROWS])

    def mixer_weights(l, gathered):
        row = lambda a: a[l].reshape(1, BRANCH_W)
        bias = jnp.repeat(jnp.swapaxes(sgu_b[l], 0, 1), GROUP_W, axis=1)
        return dict(pool_w=pool_w[l].astype(BF16), pool_scale=row(pool_scale), conv_w=gathered["conv_w"],
                    conv_b=row(conv_b), conv_ln_g=row(conv_ln_g), conv_ln_b=row(conv_ln_b), sgu_ln_g=row(sgu_ln_g),
                    sgu_ln_b=row(sgu_ln_b), sgu_w=sgu_w[l], sgu_bias=bias, sc_w=gathered["sc_w"])

    lw, mw = [None, None], [None, None]

    xs, projs, hs, zs, mgs, caccs = [x.reshape(n, D_MODEL)], [], [], [], [], []
    for l in range(DEPTH):
        first = l == 0
        win_l = win_f0.reshape(N_CHIPS, D_MODEL, SHARD_COLS) if first else lw[1]["win"]
        ride = _join(_gather_whole([wbr_s, wo_s, cs_s], 0), _gather_layer1_first_half(win_s)) if first else None
        (proj, h), rode = _inproj_fwd(xs[l], norm_g[l:l + 1], win_l, l, ride=ride)
        if first:
            lw[0] = layer_weights(win_f0, *rode[:3])
            mw[0] = mixer_weights(0, lw[0])
            win_f1 = rode[3]
        (z3, cacc), rode = _mixers_fwd(proj.reshape(nb, seq, IN_COLS), mw[l], l,
                                       ride=_gather_layer1_second_half(win_s, win_f1) if first else None)
        caccs.append(cacc)
        if first:
            win_f1 = rode[0]
        z = z3.reshape(n, N_BRANCH * BRANCH_W)
        if first:
            (x_new, merged), rode = _merge_fwd(z, proj, xs[l], lw[l]["wbr"], lw[l]["wo"], l,
                                               ride=_gather_whole([wbr_s, wo_s, cs_s], 1))
            lw[1] = layer_weights(win_f1, *rode)
            mw[1] = mixer_weights(1, lw[1])
            xs.append(x_new)
        else:
            dx, merged, loss_part, g_final = _merge_fwd_head(
                z, proj, xs[l], lw[l]["wbr"], lw[l]["wo"], final_g.reshape(1, D_MODEL),
                loss_target.reshape(n, D_MODEL), l)
        projs.append(proj)
        hs.append(h)
        zs.append(z)
        mgs.append(merged)

    def chip_major(gwbr):
        return gwbr.reshape(N_BRANCH, BRANCH_W, N_CHIPS, BR_SHARD).transpose(2, 0, 1, 3)

    def as_halves(arrays, first_index=0):
        return [g.reshape((N_CHIPS, 2) + s) for g, s in zip(arrays, HALF_SHAPES[first_index:])]

    def add_halves(grads, others, l, first_index=0):
        return [_add_halves(g, o, core, f"add_halves_l{l}_{first_index + i}")
                for i, (g, o) in enumerate(zip(grads, others))]

    per_layer, parts = {}, [None] * DEPTH
    (dz, dproj, gwo, gwbr), _ = _merge_bwd(dx, mgs[1], zs[1], projs[1], lw[1]["wbr"], lw[1]["wo"], 1)
    dproj3, gm, _ = _mixers_bwd(projs[1].reshape(nb, seq, IN_COLS), dz.reshape(nb, seq, N_BRANCH * BRANCH_W), caccs[1],
                                dproj.reshape(nb, seq, IN_COLS), mw[1], 1)
    dproj = dproj3.reshape(n, IN_COLS)
    gwin, _ = _inproj_bwd_w(hs[1], dproj, 1)
    gcs = _pack_cs(_shard_major_rows(gm["conv_w"]), _shard_major_rows(gm["sc_w"]))
    grads1 = as_halves([gwin, chip_major(gwbr), gwo, gcs])
    (dx, g_norm), others1 = _inproj_bwd_x(dproj, lw[1]["win"], xs[1], norm_g[1:2], dx, 1, ride=_swap_halves(grads1))
    gm["norm_g"] = g_norm
    per_layer[1] = gm
    sums1 = add_halves(grads1, others1, 1)
    (dz, dproj, gwo, gwbr), scattered1 = _merge_bwd(dx, mgs[0], zs[0], projs[0], lw[0]["wbr"], lw[0]["wo"], 0,
                                                    ride=_scatter_chip_sums(sums1))
    dproj3, gm, parts[1] = _mixers_bwd(projs[0].reshape(nb, seq, IN_COLS), dz.reshape(nb, seq, N_BRANCH * BRANCH_W),
                                       caccs[0], dproj.reshape(nb, seq, IN_COLS), mw[0], 0,
                                       ride=_share_halves(scattered1))
    dproj = dproj3.reshape(n, IN_COLS)
    gcs = _pack_cs(_shard_major_rows(gm["conv_w"]), _shard_major_rows(gm["sc_w"]))
    small0 = as_halves([chip_major(gwbr), gwo, gcs], first_index=1)

    per_layer[0] = dict(gm, norm_g=jnp.zeros((1, D_MODEL), F32))

    def stack(k, shape):
        return jnp.stack([per_layer[l][k] for l in range(DEPTH)]).reshape(shape)

    g_small = {"final_g": g_final.reshape(D_MODEL), "norm_g": stack("norm_g", (DEPTH, D_MODEL)),
               "pool_w": stack("pool_w", pool_w.shape), "sgu_w": stack("sgu_w", sgu_w.shape),
               "sgu_b": jnp.swapaxes(stack("sgu_b", (DEPTH, CHUNK, GROUP_W))[:, :, :N_GROUPS], 1, 2)}
    for k in ("pool_scale", "conv_b", "conv_ln_g", "conv_ln_b", "sgu_ln_g", "sgu_ln_b"):
        g_small[k] = stack(k, (DEPTH, BRANCH_W))
    small_part = _pack_small(g_small, jnp.broadcast_to(loss_part, (8, GROUP_W)))

    gwin, rode = _inproj_bwd_w(hs[0], dproj, 0, ride=_join(_swap_halves(small0), _swap_whole(small_part)))
    others_small0, small_other = rode[:3], rode[3]
    small_chip = _add2(small_part, small_other, "add_small_sibling")
    (gwin_h,) = as_halves([gwin])

    def flat(a, i):
        rows, cols = 2 * HALF_SHAPES[i][0], HALF_SHAPES[i][1]
        return a.reshape(a.shape[0], rows, cols)

    names4 = ["w_in", "w_branch", "w_o", "conv_sc"]

    def packed4(t):
        return [t["w_in"], t["w_branch"], t["w_o"], _pack_cs(t["conv_w"], t["sc_w"])]

    w4, m4, v4 = ([flat(a, i) for i, a in enumerate(packed4(t))] for t in (w, mom, var))

    def sharded_update(i, layer, prev, after=None):
        return _adamw_sharded(w4[i], m4[i], v4[i], flat(parts[layer][i], i), layer, prev, names4[i], after=after)

    send_sem, recv_sem, gwin_thru, land_thru, token = _swap_halves_start(gwin_h, "swap_halves_l0_w_in")
    updated = [sharded_update(i, 1, None, after=token if i == 0 else None)[0] for i in range(4)]
    gwin_h, other_win0 = _swap_halves_wait(send_sem, recv_sem, gwin_thru, land_thru, updated[0][0],
                                           "swap_halves_l0_w_in")
    sums0 = add_halves([gwin_h], [other_win0], 0) + add_halves(small0, others_small0, 0, first_index=1)
    (dx, g_norm0), rode = _inproj_bwd_x(dproj, lw[0]["win"], xs[0], norm_g[0:1], dx, 0,
                                        ride=_join(_scatter_chip_sums(sums0), _spread_chip_sums(small_chip)))
    scattered0, small_chips = rode[:4], rode[4]
    grad_x = dx.reshape(nb, seq, D_MODEL)

    g_norm0, parts[0] = _allreduce_small(g_norm0.reshape(D_MODEL // GROUP_W, GROUP_W), _share_halves(scattered0))
    g_packed = lax.dynamic_update_slice(_sum_chips(small_chips, "sum_small_chips"), g_norm0, (0, 0))
    loss = g_packed[g_packed.shape[0] - 8, 0]
    d_packed, m_packed, v_packed = _adamw_packed(_pack_small(w), _pack_small(mom), _pack_small(var), g_packed)
    grads, deltas, new_m, new_v = {}, {}, {}, {}
    for tree, packed in ((grads, g_packed), (deltas, d_packed), (new_m, m_packed), (new_v, v_packed)):
        tree.update(_unpack_small(packed, w))

    for i, name in enumerate(names4):
        res, _ = sharded_update(i, 0, updated[i])
        for tree, r in zip((grads, deltas, new_m, new_v), res):
            if name == "conv_sc":
                tree["conv_w"] = r[:, :CONV_K]
                tree["sc_w"] = r[:, CS_ROWS // 2:CS_ROWS // 2 + SHORT_K]
            else:
                tree[name] = r.reshape(w[name].shape)

    return (loss, grad_x, *[grads[k] for k in WEIGHTS], *[deltas[k] for k in WEIGHTS],
            *[new_m[k] for k in WEIGHTS], *[new_v[k] for k in WEIGHTS])
```

```python
import functools

import jax
import jax.numpy as jnp
from jax import lax
from jax.experimental import pallas as pl
from jax.experimental.pallas import tpu as pltpu

F32 = jnp.float32
BF16 = jnp.bfloat16

D_MODEL = 1024
DEPTH = 2
N_BRANCH = 4
BRANCH_W = 512
N_GROUPS = 4
GROUP_W = 128
POOL_WINDOWS = (2, 4, 8, 16)
CONV_K = 31
SHORT_K = 3
CHUNK = 128
N_PIECES = 12
PIECE_COLS = N_PIECES * BRANCH_W
IN_COLS = PIECE_COLS + N_BRANCH * D_MODEL
N_CHIPS = 4
SHARD_COLS = IN_COLS // N_CHIPS
BR_SHARD = D_MODEL // N_CHIPS
COL_TILE = SHARD_COLS // 2
N_COL_TILES = IN_COLS // COL_TILE
GATE_BLOCK0 = PIECE_COLS // D_MODEL
RMS_EPS = 1e-6
LN_EPS = 1e-5
HALO = 32
CONV_ROWS = 32
SHORT_ROWS = 8
CS_ROWS = 64

ADAM_LR = 0.001
ADAM_B1 = 0.9
ADAM_B2 = 0.999
ADAM_EPS = 1e-08
ADAM_WD = 0.01
ADAM_STEP = 10

VMEM_LIMIT = 60 * 1024 * 1024
MESH_ID = pl.DeviceIdType.MESH
ANY = pl.BlockSpec(memory_space=pl.ANY)
VMEM_WHOLE = pl.BlockSpec(memory_space=pltpu.VMEM)

(P_X, P_GATE, C_A, C_B, C_GATE, G_U, G_V, G_GATE, S_B, S_C, S_X, S_GATE) = range(N_PIECES)


def _params(*sem):
    return pltpu.CompilerParams(dimension_semantics=sem, vmem_limit_bytes=VMEM_LIMIT)


def _sigmoid(v):
    return 0.5 * jnp.tanh(0.5 * v) + 0.5


def _silu(v):
    return v * _sigmoid(v)


def _silu_pair(v):
    s = _sigmoid(v)
    return v * s, s * (1.0 + v * (1.0 - s))


def _dot(a, b):
    return jnp.dot(a, b, preferred_element_type=F32)


def _dot_nt(a, b):
    return lax.dot_general(a, b, (((1,), (1,)), ((), ())), preferred_element_type=F32)


def _dot_tn(a, b):
    return lax.dot_general(a, b, (((0,), (0,)), ((), ())), preferred_element_type=F32)


def _rowsum(v):
    return jnp.sum(v, axis=0, keepdims=True)


def _lanemean(v):
    return jnp.mean(v, axis=-1, keepdims=True)


def _tile(n, want, mult=8):
    t = max(1, min(n, want))
    while n % t or (t % mult and t != n):
        t -= 1
    return t


def _place():
    x, y, c = lax.axis_index("x"), lax.axis_index("y"), lax.axis_index("c")
    chip = 2 * x + y
    peers = [(1 - x, y), (x, 1 - y), (1 - x, 1 - y)]
    return x, y, c, chip, peers


def _remote(src, dst, send_sem, recv_sem, dev):
    return pltpu.make_async_remote_copy(src_ref=src, dst_ref=dst, send_sem=send_sem, recv_sem=recv_sem,
                                        device_id=dev, device_id_type=MESH_ID)


class _Exchange:
    def __init__(self, inputs, out_shapes, plan, n_remote, n_local=0, aliases=None):
        self.inputs = list(inputs)
        self.out_shapes = list(out_shapes)
        self.plan = plan
        self.n_remote = n_remote
        self.n_local = n_local
        self.aliases = dict(aliases or {})

    def copies(self, in_refs, out_refs, send_sems, recv_sems, loc_sems):
        remote, local = self.plan(in_refs, out_refs)
        assert len(remote) == self.n_remote and len(local) == self.n_local
        cps = [_remote(s, d, send_sems.at[t], recv_sems.at[t], dev) for t, (s, d, dev) in enumerate(remote)]
        cps += [pltpu.make_async_copy(s, d, loc_sems.at[t]) for t, (s, d) in enumerate(local)]
        return cps

    def sem_shapes(self):
        return [pltpu.SemaphoreType.DMA((max(self.n_remote, 1),)), pltpu.SemaphoreType.DMA((max(self.n_remote, 1),)),
                pltpu.SemaphoreType.DMA((max(self.n_local, 1),))]


def _exchange_call(name, ex):
    n_ci = len(ex.inputs)

    def body(*refs):
        cins, couts = refs[:n_ci], refs[n_ci:n_ci + len(ex.out_shapes)]
        cps = ex.copies(cins, couts, *refs[n_ci + len(ex.out_shapes):])
        for cp in cps:
            cp.start()
        for cp in cps:
            cp.wait()

    return pl.pallas_call(
        body,
        name=name,
        in_specs=[ANY] * n_ci,
        out_specs=[ANY] * len(ex.out_shapes),
        out_shape=ex.out_shapes,
        scratch_shapes=ex.sem_shapes(),
        input_output_aliases=ex.aliases,
    )(*ex.inputs)


def _host_call(body, *, name, grid, in_specs, out_specs, out_shape, args, scratch_shapes=(), aliases=None, ride=None):
    n_in, n_out, n_scr = len(in_specs), len(out_specs), len(scratch_shapes)
    sem = ("arbitrary",) * len(grid)
    aliases = dict(aliases or {})
    if ride is None:
        outs = pl.pallas_call(body, name=name, grid=grid, in_specs=list(in_specs), out_specs=list(out_specs),
                              out_shape=list(out_shape), scratch_shapes=list(scratch_shapes),
                              input_output_aliases=aliases, compiler_params=_params(*sem))(*args)
        return list(outs), []
    n_ci, n_co = len(ride.inputs), len(ride.out_shapes)

    def full_body(*refs):
        ins, cins = refs[:n_in], refs[n_in:n_in + n_ci]
        o0 = n_in + n_ci
        outs, couts = refs[o0:o0 + n_out], refs[o0 + n_out:o0 + n_out + n_co]
        s0 = o0 + n_out + n_co
        scr, sems = refs[s0:s0 + n_scr], refs[s0 + n_scr:]
        first = functools.reduce(jnp.logical_and, [pl.program_id(d) == 0 for d in range(len(grid))])
        last = functools.reduce(jnp.logical_and, [pl.program_id(d) == grid[d] - 1 for d in range(len(grid))])

        @pl.when(first)
        def _():
            for cp in ride.copies(cins, couts, *sems):
                cp.start()

        body(*ins, *outs, *scr)

        @pl.when(last)
        def _():
            for cp in ride.copies(cins, couts, *sems):
                cp.wait()

    for ci, co in ride.aliases.items():
        aliases[n_in + ci] = n_out + co
    outs = pl.pallas_call(
        full_body, name=name, grid=grid, in_specs=list(in_specs) + [ANY] * n_ci,
        out_specs=list(out_specs) + [ANY] * n_co, out_shape=list(out_shape) + ride.out_shapes,
        scratch_shapes=list(scratch_shapes) + ride.sem_shapes(), input_output_aliases=aliases,
        compiler_params=_params(*sem))(*args, *ride.inputs)
    return list(outs[:n_out]), list(outs[n_out:])


def _allgather_layer0(shards):
    na = len(shards)

    def body(*refs):
        ins, outs = refs[:na], refs[na:2 * na]
        send_sems, recv_sems, fsend_sems, frecv_sems, loc_sems = refs[2 * na:]
        x, y, c, chip, peers = _place()
        sib = (x, y, 1 - c)
        locs = [pltpu.make_async_copy(ins[a].at[0], outs[a].at[chip], loc_sems.at[a]) for a in range(na)]
        for cp in locs:
            cp.start()
        pending = []
        for k, (px, py) in enumerate(peers):
            for a in range(na):
                cp = _remote(ins[a].at[0, c], outs[a].at[chip, c], send_sems.at[k * na + a],
                             recv_sems.at[k * na + a], (px, py, c))
                cp.start()
                pending.append(cp)
        for k, (px, py) in enumerate(peers):
            pchip = 2 * px + py
            for a in range(na):
                slab = outs[a].at[pchip, c]
                _remote(slab, slab, send_sems.at[k * na + a], recv_sems.at[k * na + a], (px, py, c)).wait_recv()
                cp = _remote(slab, slab, fsend_sems.at[k * na + a], frecv_sems.at[k * na + a], sib)
                cp.start()
                pending.append(cp)
        for k, (px, py) in enumerate(peers):
            pchip = 2 * px + py
            for a in range(na):
                slab = outs[a].at[pchip, 1 - c]
                _remote(slab, slab, fsend_sems.at[k * na + a], frecv_sems.at[k * na + a], sib).wait_recv()
        for cp in pending:
            cp.wait_send()
        for cp in locs:
            cp.wait()

    return pl.pallas_call(
        body,
        name="allgather_layer0",
        in_specs=[ANY] * na,
        out_specs=[ANY] * na,
        out_shape=[jax.ShapeDtypeStruct((N_CHIPS,) + a.shape[1:], a.dtype) for a in shards],
        scratch_shapes=[pltpu.SemaphoreType.DMA((3 * na,))] * 4 + [pltpu.SemaphoreType.DMA((na,))],
    )(*shards)


def _gather_layer1_first_half(win_s):
    def plan(cins, couts):
        _, _, c, chip, peers = _place()
        (src,), (dst,) = cins, couts
        return ([(src.at[1, 0], dst.at[chip, 0], (px, py, c)) for px, py in peers],
                [(src.at[1], dst.at[chip])])

    return _Exchange([win_s], [jax.ShapeDtypeStruct((N_CHIPS,) + win_s.shape[1:], win_s.dtype)], plan, 3, 1)


def _gather_layer1_second_half(win_s, win_f1):
    def plan(cins, couts):
        _, _, c, chip, peers = _place()
        src, (dst,) = cins[0], couts
        return [(src.at[1, 1], dst.at[chip, 1], (px, py, c)) for px, py in peers], []

    return _Exchange([win_s, win_f1], [jax.ShapeDtypeStruct(win_f1.shape, win_f1.dtype)], plan, 3, 0, aliases={1: 0})


def _gather_whole(shards, layer):
    na = len(shards)

    def plan(cins, couts):
        _, _, c, chip, peers = _place()
        remote = [(cins[a].at[layer], couts[a].at[chip], (px, py, c)) for a in range(na) for px, py in peers]
        return remote, [(cins[a].at[layer], couts[a].at[chip]) for a in range(na)]

    return _Exchange(shards, [jax.ShapeDtypeStruct((N_CHIPS,) + a.shape[1:], a.dtype) for a in shards], plan,
                     3 * na, na)


def _join(a, b):
    n_in, n_out = len(a.inputs), len(a.out_shapes)

    def plan(cins, couts):
        remote_a, local_a = a.plan(cins[:n_in], couts[:n_out])
        remote_b, local_b = b.plan(cins[n_in:], couts[n_out:])
        return remote_a + remote_b, local_a + local_b

    aliases = dict(a.aliases)
    aliases.update({n_in + i: n_out + o for i, o in b.aliases.items()})
    return _Exchange(a.inputs + b.inputs, a.out_shapes + b.out_shapes, plan, a.n_remote + b.n_remote,
                     a.n_local + b.n_local, aliases)


def _swap_halves(grads):
    def plan(cins, couts):
        x, y, c, _, _ = _place()
        return [(g.at[:, 1 - c], r, (x, y, 1 - c)) for g, r in zip(cins, couts)], []

    return _Exchange(grads, [jax.ShapeDtypeStruct(g.shape[:1] + g.shape[2:], g.dtype) for g in grads], plan,
                     len(grads))


def _scatter_chip_sums(sums):
    na = len(sums)

    def plan(cins, couts):
        _, _, c, chip, peers = _place()
        remote = [(cins[a].at[2 * px + py], couts[a].at[chip, c], (px, py, c)) for a in range(na) for px, py in peers]
        return remote, [(cins[a].at[chip], couts[a].at[chip, c]) for a in range(na)]

    return _Exchange(sums, [jax.ShapeDtypeStruct((N_CHIPS, 2) + s.shape[1:], s.dtype) for s in sums], plan,
                     3 * na, na)


def _share_halves(parts):
    na = len(parts)

    def plan(cins, couts):
        x, y, c, _, _ = _place()
        return [(p.at[:, c], p.at[:, c], (x, y, 1 - c)) for p in couts], []

    return _Exchange(parts, [jax.ShapeDtypeStruct(p.shape, p.dtype) for p in parts], plan, na, 0,
                     aliases={a: a for a in range(na)})


def _allreduce_small(packed, ride):
    rows = packed.shape[0]
    n_ci, n_co = len(ride.inputs), len(ride.out_shapes)

    def body(*refs):
        sg_ref, cins = refs[0], refs[1:1 + n_ci]
        res_ref, couts = refs[1 + n_ci], refs[2 + n_ci:2 + n_ci + n_co]
        sib_ref, cs_ref, rem_ref, s1_send, s1_recv, s2_send, s2_recv = refs[2 + n_ci + n_co:9 + n_ci + n_co]
        riding = ride.copies(cins, couts, *refs[9 + n_ci + n_co:])
        for cp in riding:
            cp.start()
        x, y, c, chip, peers = _place()
        cp = _remote(sg_ref, sib_ref, s1_send, s1_recv, (x, y, 1 - c))
        cp.start()
        cp.wait()
        cs_ref[...] = sg_ref[...] + sib_ref[...]
        rem_ref[chip] = cs_ref[...]
        cps = [_remote(cs_ref, rem_ref.at[chip], s2_send.at[k], s2_recv.at[chip], (px, py, c))
               for k, (px, py) in enumerate(peers)]
        for cp in cps:
            cp.start()
        for k, (px, py) in enumerate(peers):
            pchip = 2 * px + py
            _remote(cs_ref, rem_ref.at[pchip], s2_send.at[k], s2_recv.at[pchip], (px, py, c)).wait_recv()
        for cp in cps:
            cp.wait_send()
        res_ref[...] = ((rem_ref[0] + rem_ref[1]) + rem_ref[2]) + rem_ref[3]
        for cp in riding:
            cp.wait()

    outs = pl.pallas_call(
        body,
        name="allreduce_small",
        in_specs=[VMEM_WHOLE] + [ANY] * n_ci,
        out_specs=[VMEM_WHOLE] + [ANY] * n_co,
        out_shape=[jax.ShapeDtypeStruct((rows, GROUP_W), F32)] + ride.out_shapes,
        scratch_shapes=[
            pltpu.VMEM((rows, GROUP_W), F32),
            pltpu.VMEM((rows, GROUP_W), F32),
            pltpu.VMEM((N_CHIPS, rows, GROUP_W), F32),
            pltpu.SemaphoreType.DMA,
            pltpu.SemaphoreType.DMA,
            pltpu.SemaphoreType.DMA((3,)),
            pltpu.SemaphoreType.DMA((N_CHIPS,)),
        ] + ride.sem_shapes(),
        input_output_aliases={1 + i: 1 + o for i, o in ride.aliases.items()},
        compiler_params=pltpu.CompilerParams(vmem_limit_bytes=VMEM_LIMIT),
    )(packed, *ride.inputs)
    return outs[0], list(outs[1:])


HBM_SPEC = pl.BlockSpec(memory_space=pltpu.HBM)
SEM_SPEC = pl.BlockSpec(memory_space=pltpu.SEMAPHORE)
SIDE_EFFECT = pltpu.SideEffectType.DATAFLOW_SIDE_EFFECTING


def _swap_halves_start(grad, name):
    land_shape = grad.shape[:1] + grad.shape[2:]

    def body(g_ref, land_ref, send_sem, recv_sem, g_thru, land_thru, token):
        del g_thru, land_thru
        x, y, c, _, _ = _place()
        _remote(g_ref.at[:, 1 - c], land_ref, send_sem, recv_sem, (x, y, 1 - c)).start()
        token[...] = jnp.zeros_like(token)

    return pl.pallas_call(
        body,
        name=name + "_start",
        out_shape=(pltpu.SemaphoreType.DMA(()), pltpu.SemaphoreType.DMA(()), pltpu.HBM(grad.shape, grad.dtype),
                   pltpu.HBM(land_shape, grad.dtype), jax.ShapeDtypeStruct((8, GROUP_W), F32)),
        in_specs=(HBM_SPEC, HBM_SPEC),
        out_specs=(SEM_SPEC, SEM_SPEC, HBM_SPEC, HBM_SPEC, VMEM_WHOLE),
        input_output_aliases={0: 2, 1: 3},
        compiler_params=pltpu.CompilerParams(has_side_effects=SIDE_EFFECT),
    )(pltpu.with_memory_space_constraint(grad, pltpu.HBM),
      pltpu.with_memory_space_constraint(lax.empty(land_shape, grad.dtype), pltpu.HBM))


def _swap_halves_wait(send_sem, recv_sem, g_thru, land_thru, after, name):
    def body(g_ref, land_ref, send_sem, recv_sem, after_ref, g_out, got_ref):
        del after_ref, g_out, got_ref
        x, y, c, _, _ = _place()
        cp = _remote(g_ref.at[:, 1 - c], land_ref, send_sem, recv_sem, (x, y, 1 - c))
        cp.wait_send()
        cp.wait_recv()

    return pl.pallas_call(
        body,
        name=name + "_wait",
        out_shape=(pltpu.HBM(g_thru.shape, g_thru.dtype), pltpu.HBM(land_thru.shape, land_thru.dtype)),
        in_specs=(HBM_SPEC, HBM_SPEC, SEM_SPEC, SEM_SPEC, ANY),
        out_specs=(HBM_SPEC, HBM_SPEC),
        input_output_aliases={0: 0, 1: 1},
        compiler_params=pltpu.CompilerParams(has_side_effects=SIDE_EFFECT),
    )(g_thru, land_thru, send_sem, recv_sem, after)


def _swap_whole(packed):
    def plan(cins, couts):
        x, y, c, _, _ = _place()
        return [(cins[0], couts[0], (x, y, 1 - c))], []

    return _Exchange([packed], [jax.ShapeDtypeStruct(packed.shape, packed.dtype)], plan, 1)


def _spread_chip_sums(chip_sum):
    def plan(cins, couts):
        _, _, c, chip, peers = _place()
        return ([(cins[0], couts[0].at[chip], (px, py, c)) for px, py in peers], [(cins[0], couts[0].at[chip])])

    return _Exchange([chip_sum], [jax.ShapeDtypeStruct((N_CHIPS,) + chip_sum.shape, chip_sum.dtype)], plan, 3, 1)


def _add2(a, b, name):
    def body(a_ref, b_ref, o_ref):
        o_ref[...] = a_ref[...] + b_ref[...]

    return pl.pallas_call(body, name=name, in_specs=[VMEM_WHOLE] * 2, out_specs=VMEM_WHOLE,
                          out_shape=jax.ShapeDtypeStruct(a.shape, F32))(a, b)


def _sum_chips(parts, name):
    def body(p_ref, o_ref):
        o_ref[...] = ((p_ref[0] + p_ref[1]) + p_ref[2]) + p_ref[3]

    return pl.pallas_call(body, name=name, in_specs=[VMEM_WHOLE], out_specs=VMEM_WHOLE,
                          out_shape=jax.ShapeDtypeStruct(parts.shape[1:], F32))(parts)


def _add_halves(grad, other, core, name):
    n_shards, _, rows, cols = grad.shape
    tr = _tile(rows, max(16, (1 << 22) // (4 * cols) // 16 * 16), mult=16)

    def body(core_ref, g_ref, o_ref, out_ref):
        del core_ref
        out_ref[...] = (g_ref[...] + o_ref[...]).astype(BF16)

    return pl.pallas_call(
        body,
        name=name,
        grid_spec=pltpu.PrefetchScalarGridSpec(
            num_scalar_prefetch=1,
            grid=(n_shards, rows // tr),
            in_specs=[
                pl.BlockSpec((None, None, tr, cols), lambda s, i, core_ref: (s, core_ref[0], i, 0)),
                pl.BlockSpec((None, tr, cols), lambda s, i, core_ref: (s, i, 0)),
            ],
            out_specs=pl.BlockSpec((None, tr, cols), lambda s, i, core_ref: (s, i, 0)),
        ),
        out_shape=jax.ShapeDtypeStruct((n_shards, rows, cols), BF16),
        compiler_params=_params("arbitrary", "arbitrary"),
    )(core, grad, other)


def _inproj_fwd(x2, g_row, w_l, layer, ride=None):
    n = x2.shape[0]
    tm = _tile(n, 2048)

    def body(x_ref, g_ref, w_ref, p_ref, h_ref):
        @pl.when(pl.program_id(1) == 0)
        def _():
            xv = x_ref[...]
            r = lax.rsqrt(_lanemean(xv * xv) + RMS_EPS)
            h_ref[...] = (xv * r * g_ref[...]).astype(BF16)

        p_ref[...] = _dot(h_ref[...], w_ref[...]).astype(BF16)

    return _host_call(
        body,
        name=f"inproj_fwd_l{layer}",
        grid=(n // tm, N_COL_TILES),
        in_specs=[
            pl.BlockSpec((tm, D_MODEL), lambda i, j: (i, 0)),
            pl.BlockSpec((1, D_MODEL), lambda i, j: (0, 0)),
            pl.BlockSpec((None, D_MODEL, COL_TILE), lambda i, j: (j // 2, 0, j % 2)),
        ],
        out_specs=[
            pl.BlockSpec((tm, COL_TILE), lambda i, j: (i, j)),
            pl.BlockSpec((tm, D_MODEL), lambda i, j: (i, 0)),
        ],
        out_shape=[jax.ShapeDtypeStruct((n, IN_COLS), BF16), jax.ShapeDtypeStruct((n, D_MODEL), BF16)],
        args=[x2, g_row, w_l],
        ride=ride,
    )


def _layer_norm_parts(v, g, b):
    mu = _lanemean(v)
    d = v - mu
    rstd = lax.rsqrt(_lanemean(d * d) + LN_EPS)
    xh = d * rstd
    return xh, rstd, xh * g + b


def _window_sum(ref, slot, weight_row, n_taps, base, t_rows):
    total = None
    for b in range(8):
        group = None
        for a in range((base + n_taps - 1) // 8 + 1):
            k = 8 * a + b - base
            if 0 <= k < n_taps:
                term = weight_row(k) * ref[slot, 8 * a:8 * a + t_rows + 8, :]
                group = term if group is None else group + term
        if group is not None:
            part = group[b:b + t_rows, :]
            total = part if total is None else total + part
    return total


def _tap_grads(acc_ref, pad_ref, shift_ref, dy, ref, slot, n_taps, base, t_rows):
    pad_ref[0:8, :] = jnp.zeros((8, BRANCH_W), F32)
    pad_ref[8:8 + t_rows, :] = dy
    pad_ref[8 + t_rows:16 + t_rows, :] = jnp.zeros((8, BRANCH_W), F32)
    for b in range(8):
        taps = [(a, 8 * a + b - base) for a in range((base + n_taps - 1) // 8 + 1) if 0 <= 8 * a + b - base < n_taps]
        if not taps:
            continue
        shift_ref[...] = pad_ref[pl.ds(8 - b, t_rows + 8), :]
        for a, k in taps:
            prod = shift_ref[...] * ref[slot, 8 * a:8 * a + t_rows + 8, :]
            acc_ref[k] += prod.reshape((t_rows + 8) // 8, 8, BRANCH_W).sum(axis=0)


def _trailing_window_sums(ext_ref, pool_ref, t_rows):
    lo, n = HALO - 16, t_rows + 16
    pool_ref[0:16, :] = jnp.zeros((16, BRANCH_W), F32)
    sums = []
    for j, win in enumerate(POOL_WINDOWS):
        cols = slice(j * GROUP_W, (j + 1) * GROUP_W)
        v = ext_ref[0, lo:lo + n, cols] + ext_ref[0, pl.ds(lo - 1, n), cols]
        width = 2
        while width < win:
            pool_ref[16:16 + n, cols] = v
            v = v + pool_ref[pl.ds(16 - width, n), cols]
            width *= 2
        sums.append(v[16:, :])
    return sums


def _leading_window_sums(nxt_ref, pool_ref, t_rows):
    n = t_rows + 16
    pool_ref[n:n + 16, :] = jnp.zeros((16, BRANCH_W), F32)
    sums = []
    for j, win in enumerate(POOL_WINDOWS):
        cols = slice(j * GROUP_W, (j + 1) * GROUP_W)
        v = nxt_ref[0, 0:n, cols] + nxt_ref[0, pl.ds(1, n), cols]
        width = 2
        while width < win:
            pool_ref[0:n, cols] = v
            v = v + pool_ref[pl.ds(width, n), cols]
            width *= 2
        sums.append(v[0:t_rows, :])
    return sums


def _tril_mask():
    r = lax.broadcasted_iota(jnp.int32, (CHUNK, CHUNK), 0)
    c = lax.broadcasted_iota(jnp.int32, (CHUNK, CHUNK), 1)
    return r >= c


def _lanes(refs4):
    return jnp.concatenate([refs4[g] for g in range(N_CHIPS)], axis=1)


def _mixer_weight_specs():
    def whole(shape):
        nd = len(shape)
        return pl.BlockSpec(shape, lambda b, c: (0,) * nd)

    return [
        whole((N_GROUPS, GROUP_W, GROUP_W)),
        whole((1, BRANCH_W)),
        whole((N_CHIPS, CONV_ROWS, GROUP_W)),
        whole((1, BRANCH_W)),
        whole((1, BRANCH_W)),
        whole((1, BRANCH_W)),
        whole((1, BRANCH_W)),
        whole((1, BRANCH_W)),
        whole((N_GROUPS, CHUNK, CHUNK)),
        whole((CHUNK, BRANCH_W)),
        whole((N_CHIPS, SHORT_ROWS, GROUP_W)),
    ]


def _mixer_weight_args(mw):
    return [mw["pool_w"], mw["pool_scale"], mw["conv_w"], mw["conv_b"], mw["conv_ln_g"], mw["conv_ln_b"],
            mw["sgu_ln_g"], mw["sgu_ln_b"], mw["sgu_w"], mw["sgu_bias"], mw["sc_w"]]


def _mix_merge_fwd(proj3, x3, mw, wbr_l, wo_l, layer, ride=None, head=None):
    nb, seq, _ = proj3.shape
    t_rows = _tile(seq, 512)
    nc = seq // t_rows
    hb = t_rows // HALO
    n_head = 2 if head else 0

    def body(*refs):
        (cur_ref, halo_ref, g01_ref, g23_ref, x_ref, pw_ref, ps_ref, cw_ref, cb_ref, clg_ref, clb_ref, slg_ref,
         slb_ref, sw_ref, sbias_ref, scw_ref, wbr_ref, wo_ref) = refs[:18]
        head_in = refs[18:18 + n_head]
        outs = refs[18 + n_head:-2]
        ext_ref, pool_ref = refs[-2:]
        z_ref, cacc_ref, mg_ref, xn_ref = outs[:4]
        b = pl.program_id(0)
        c = pl.program_id(1)
        has_prev = c > 0
        row = lax.broadcasted_iota(jnp.int32, (t_rows, 1), 0)
        tpos = (c * t_rows + row + 1).astype(F32)

        def cur(k):
            return cur_ref[:, k * BRANCH_W:(k + 1) * BRANCH_W].astype(F32)

        def hal(k):
            return halo_ref[:, k * BRANCH_W:(k + 1) * BRANCH_W].astype(F32)

        def put_ext(slot, halo_val, cur_val):
            ext_ref[slot, 0:HALO, :] = jnp.where(has_prev, halo_val, 0.0)
            ext_ref[slot, HALO:HALO + t_rows, :] = cur_val
            ext_ref[slot, HALO + t_rows:HALO + t_rows + 8, :] = jnp.zeros((8, BRANCH_W), F32)

        def gated(branch, z_val):
            zb = z_val.astype(BF16)
            z_ref[:, branch * BRANCH_W:(branch + 1) * BRANCH_W] = zb
            gate_ref = g01_ref if branch < 2 else g23_ref
            gate = gate_ref[:, (branch % 2) * D_MODEL:(branch % 2 + 1) * D_MODEL].astype(F32)
            return _sigmoid(gate) * _dot(zb, wbr_ref[branch])

        px = cur(P_X)
        put_ext(0, hal(P_X), px)
        mixed = []
        window_sums = _trailing_window_sums(ext_ref, pool_ref, t_rows)
        for j, win in enumerate(POOL_WINDOWS):
            cols = slice(j * GROUP_W, (j + 1) * GROUP_W)
            pooled = window_sums[j] / jnp.minimum(tpos, float(win)) - px[:, cols]
            mixed.append(_dot(pooled.astype(BF16), pw_ref[j]))
        acc = gated(0, jnp.concatenate(mixed, axis=1) * ps_ref[...] * _silu(cur(P_GATE)))

        put_ext(1, hal(C_A) * _sigmoid(hal(C_B)), cur(C_A) * _sigmoid(cur(C_B)))
        cw = _lanes(cw_ref)
        conv = cb_ref[...] + _window_sum(ext_ref, 1, lambda k: cw[k:k + 1, :], CONV_K, HALO - (CONV_K - 1), t_rows)
        cacc_ref[...] = conv
        _, _, ln = _layer_norm_parts(conv, clg_ref[...], clb_ref[...])
        acc = acc + gated(1, _silu(ln) * _silu(cur(C_GATE)))

        _, _, v = _layer_norm_parts(cur(G_V), slg_ref[...], slb_ref[...])
        vb = v.astype(BF16)
        mask = _tril_mask()
        wt = [jnp.where(mask, sw_ref[g], 0.0).astype(BF16) for g in range(N_GROUPS)]
        sp_rows = []
        for sub in range(t_rows // CHUNK):
            rows = slice(sub * CHUNK, (sub + 1) * CHUNK)
            sp_rows.append(jnp.concatenate(
                [_dot(wt[g], vb[rows, g * GROUP_W:(g + 1) * GROUP_W]) for g in range(N_GROUPS)], axis=1)
                + sbias_ref[...])
        sp = jnp.concatenate(sp_rows, axis=0)
        acc = acc + gated(2, cur(G_U) * sp * _silu(cur(G_GATE)))

        put_ext(2, hal(S_C) * hal(S_X), cur(S_C) * cur(S_X))
        scw = _lanes(scw_ref)
        cv = _window_sum(ext_ref, 2, lambda k: scw[k:k + 1, :], SHORT_K, HALO - (SHORT_K - 1), t_rows)
        acc = acc + gated(3, cur(S_B) * cv * _silu(cur(S_GATE)))

        mg = acc.astype(BF16)
        mg_ref[...] = mg
        xv = x_ref[...] + _dot(mg, wo_ref[...])
        if not head:
            xn_ref[...] = xv
            return
        g_ref, t_ref = head_in
        loss_ref, dg_ref = outs[4:]

        @pl.when(jnp.logical_and(b == 0, c == 0))
        def _():
            loss_ref[...] = jnp.zeros_like(loss_ref)
            dg_ref[...] = jnp.zeros_like(dg_ref)

        g = g_ref[...]
        r = lax.rsqrt(_lanemean(xv * xv) + RMS_EPS)
        xh = xv * r
        err = xh * g - t_ref[...]
        loss_ref[...] += 0.5 * jnp.sum(_lanemean(err * err), axis=0, keepdims=True)
        dy = err * (1.0 / D_MODEL)
        dg_ref[...] += _rowsum(dy * xh)
        dxh = dy * g
        xn_ref[...] = r * (dxh - xh * _lanemean(dxh * xh))

    def rows_spec(cols, col_block=0):
        return pl.BlockSpec((None, t_rows, cols), lambda b, c: (b, c, col_block))

    def whole(shape):
        nd = len(shape)
        return pl.BlockSpec(shape, lambda b, c: (0,) * nd)

    in_specs = [
        rows_spec(PIECE_COLS),
        pl.BlockSpec((None, HALO, PIECE_COLS), lambda b, c: (b, jnp.maximum(c * hb - 1, 0), 0)),
        rows_spec(2 * D_MODEL, PIECE_COLS // (2 * D_MODEL)),
        rows_spec(2 * D_MODEL, PIECE_COLS // (2 * D_MODEL) + 1),
        rows_spec(D_MODEL),
    ] + _mixer_weight_specs() + [whole((N_BRANCH, BRANCH_W, D_MODEL)), whole((D_MODEL, D_MODEL))]
    args = [proj3, proj3, proj3, proj3, x3, *_mixer_weight_args(mw), wbr_l, wo_l]
    out_specs = [rows_spec(N_BRANCH * BRANCH_W), rows_spec(BRANCH_W), rows_spec(D_MODEL), rows_spec(D_MODEL)]
    out_shape = [jax.ShapeDtypeStruct((nb, seq, N_BRANCH * BRANCH_W), BF16), jax.ShapeDtypeStruct((nb, seq, BRANCH_W), F32),
                 jax.ShapeDtypeStruct((nb, seq, D_MODEL), BF16), jax.ShapeDtypeStruct((nb, seq, D_MODEL), F32)]
    if head:
        in_specs += [whole((1, D_MODEL)), rows_spec(D_MODEL)]
        args += list(head)
        out_specs += [whole((1, GROUP_W)), whole((1, D_MODEL))]
        out_shape += [jax.ShapeDtypeStruct((1, GROUP_W), F32), jax.ShapeDtypeStruct((1, D_MODEL), F32)]
    return _host_call(
        body,
        name=f"mix_merge_fwd_l{layer}",
        grid=(nb, nc),
        in_specs=in_specs,
        out_specs=out_specs,
        out_shape=out_shape,
        scratch_shapes=[pltpu.VMEM((3, HALO + t_rows + 8, BRANCH_W), F32),
                        pltpu.VMEM((t_rows + 32, BRANCH_W), F32)],
        args=args,
        ride=ride,
    )


MIXER_GRADS = ["pool_w", "pool_scale", "conv_w", "conv_b", "conv_ln_g", "conv_ln_b", "sgu_ln_g", "sgu_ln_b", "sgu_w",
               "sgu_b", "sc_w"]


def _mixers_bwd(proj3, dz3, cacc3, dproj3, mw, layer, ride=None):
    nb, seq, _ = proj3.shape
    t_rows = _tile(seq, 512)
    nc = seq // t_rows
    hb = t_rows // HALO

    def body(cur_ref, halo_ref, dz_ref, cacc_ref, pw_ref, ps_ref, cw_ref, cb_ref, clg_ref, clb_ref, slg_ref, slb_ref,
             sw_ref, sbias_ref, scw_ref, dp_in_ref, dp_ref, g_pw, g_ps, g_cw, g_cb, g_clg, g_clb, g_slg,
             g_slb, g_sw, g_sb, g_scw, ext_ref, nxt_ref, sb_acc, cw_acc, scw_acc, pad_ref, shift_ref,
             pool_ref):
        del dp_in_ref
        b = pl.program_id(0)
        r = pl.program_id(1)
        c = nc - 1 - r
        has_prev = c > 0
        row = lax.broadcasted_iota(jnp.int32, (t_rows, 1), 0)
        tpos = (c * t_rows + row + 1).astype(F32)

        @pl.when(jnp.logical_and(b == 0, r == 0))
        def _():
            for ref in (g_pw, g_ps, g_cb, g_clg, g_clb, g_slg, g_slb, g_sw, sb_acc, cw_acc, scw_acc):
                ref[...] = jnp.zeros_like(ref)

        @pl.when(r == 0)
        def _():
            nxt_ref[:, t_rows:t_rows + HALO, :] = jnp.zeros((3, HALO, BRANCH_W), F32)

        def cur(k):
            return cur_ref[:, k * BRANCH_W:(k + 1) * BRANCH_W].astype(F32)

        def hal(k):
            return halo_ref[:, k * BRANCH_W:(k + 1) * BRANCH_W].astype(F32)

        def dzp(k):
            return dz_ref[:, k * BRANCH_W:(k + 1) * BRANCH_W].astype(F32)

        def put_dp(k, val):
            dp_ref[:, k * BRANCH_W:(k + 1) * BRANCH_W] = val.astype(BF16)

        def put_ext(slot, halo_val, cur_val):
            ext_ref[slot, 0:HALO, :] = jnp.where(has_prev, halo_val, 0.0)
            ext_ref[slot, HALO:HALO + t_rows, :] = cur_val
            ext_ref[slot, HALO + t_rows:HALO + t_rows + 8, :] = jnp.zeros((8, BRANCH_W), F32)

        px = cur(P_X)
        put_ext(0, hal(P_X), px)
        pgate = cur(P_GATE)
        dz_pool = dzp(0)
        ps = ps_ref[...]
        pooled, mixed, cnts = [], [], []
        window_sums = _trailing_window_sums(ext_ref, pool_ref, t_rows)
        for j, win in enumerate(POOL_WINDOWS):
            cols = slice(j * GROUP_W, (j + 1) * GROUP_W)
            cnt = jnp.minimum(tpos, float(win))
            pj = window_sums[j] / cnt - px[:, cols]
            cnts.append(cnt)
            pooled.append(pj.astype(BF16))
            mixed.append(_dot(pooled[j], pw_ref[j]))
        mixed = jnp.concatenate(mixed, axis=1)
        pg_silu, pg_dsilu = _silu_pair(pgate)
        put_dp(P_GATE, dz_pool * (mixed * ps) * pg_dsilu)
        d_out = dz_pool * pg_silu
        g_ps[...] += _rowsum(d_out * mixed)
        d_mixed = (d_out * ps).astype(BF16)
        d_pooled = []
        for j in range(N_GROUPS):
            cols = slice(j * GROUP_W, (j + 1) * GROUP_W)
            g_pw[j] += _dot_tn(pooled[j], d_mixed[:, cols])
            dpj = _dot_nt(d_mixed[:, cols], pw_ref[j])
            d_pooled.append(dpj)
            nxt_ref[0, 0:t_rows, cols] = dpj / cnts[j]
        lead_sums = _leading_window_sums(nxt_ref, pool_ref, t_rows)
        put_dp(P_X, jnp.concatenate([lead_sums[j] - d_pooled[j] for j in range(N_GROUPS)], axis=1))

        c_a = cur(C_A)
        sig_b = _sigmoid(cur(C_B))
        put_ext(1, hal(C_A) * _sigmoid(hal(C_B)), c_a * sig_b)
        cw = _lanes(cw_ref)
        xh, rstd, ln = _layer_norm_parts(cacc_ref[...], clg_ref[...], clb_ref[...])
        cgate = cur(C_GATE)
        dz_conv = dzp(1)
        ln_silu, ln_dsilu = _silu_pair(ln)
        cg_silu, cg_dsilu = _silu_pair(cgate)
        put_dp(C_GATE, dz_conv * ln_silu * cg_dsilu)
        d_ln = dz_conv * cg_silu * ln_dsilu
        g_clg[...] += _rowsum(d_ln * xh)
        g_clb[...] += _rowsum(d_ln)
        dxh = d_ln * clg_ref[...]
        dc = rstd * (dxh - _lanemean(dxh) - xh * _lanemean(dxh * xh))
        g_cb[...] += _rowsum(dc)
        nxt_ref[1, 0:t_rows, :] = dc
        _tap_grads(cw_acc, pad_ref, shift_ref, dc, ext_ref, 1, CONV_K, HALO - (CONV_K - 1), t_rows)
        dyg = _window_sum(nxt_ref, 1, lambda i: cw[CONV_K - 1 - i:CONV_K - i, :], CONV_K, 0, t_rows)
        put_dp(C_A, dyg * sig_b)
        put_dp(C_B, dyg * c_a * sig_b * (1.0 - sig_b))

        u = cur(G_U)
        ggate = cur(G_GATE)
        vxh, vrstd, v = _layer_norm_parts(cur(G_V), slg_ref[...], slb_ref[...])
        vb = v.astype(BF16)
        mask = _tril_mask()
        wt = [jnp.where(mask, sw_ref[g], 0.0) for g in range(N_GROUPS)]
        wt_b = [w.astype(BF16) for w in wt]
        wtt_b = [w.T.astype(BF16) for w in wt]
        dz_sgu = dzp(2)
        gg_silu, gg_dsilu = _silu_pair(ggate)
        d_sgu = dz_sgu * gg_silu
        d_sp = d_sgu * u
        d_spb = d_sp.astype(BF16)
        sp_rows, dv_rows = [], []
        for sub in range(t_rows // CHUNK):
            rows = slice(sub * CHUNK, (sub + 1) * CHUNK)
            sp_g, dv_g = [], []
            for g in range(N_GROUPS):
                cols = slice(g * GROUP_W, (g + 1) * GROUP_W)
                sp_g.append(_dot(wt_b[g], vb[rows, cols]))
                g_sw[g] += jnp.where(mask, _dot_nt(d_spb[rows, cols], vb[rows, cols]), 0.0)
                dv_g.append(_dot(wtt_b[g], d_spb[rows, cols]))
            sp_rows.append(jnp.concatenate(sp_g, axis=1) + sbias_ref[...])
            dv_rows.append(jnp.concatenate(dv_g, axis=1))
            sb_acc[...] += d_sp[rows, :]
        sp = jnp.concatenate(sp_rows, axis=0)
        dv = jnp.concatenate(dv_rows, axis=0)
        put_dp(G_GATE, dz_sgu * (u * sp) * gg_dsilu)
        put_dp(G_U, d_sgu * sp)
        g_slg[...] += _rowsum(dv * vxh)
        g_slb[...] += _rowsum(dv)
        dvx = dv * slg_ref[...]
        put_dp(G_V, vrstd * (dvx - _lanemean(dvx) - vxh * _lanemean(dvx * vxh)))

        s_b, s_c, s_x, sgate = cur(S_B), cur(S_C), cur(S_X), cur(S_GATE)
        put_ext(2, hal(S_C) * hal(S_X), s_c * s_x)
        scw = _lanes(scw_ref)
        cv = _window_sum(ext_ref, 2, lambda k: scw[k:k + 1, :], SHORT_K, HALO - (SHORT_K - 1), t_rows)
        dz_sc = dzp(3)
        sg_silu, sg_dsilu = _silu_pair(sgate)
        put_dp(S_GATE, dz_sc * (s_b * cv) * sg_dsilu)
        d_pre = dz_sc * sg_silu
        put_dp(S_B, d_pre * cv)
        dcv = d_pre * s_b
        nxt_ref[2, 0:t_rows, :] = dcv
        _tap_grads(scw_acc, pad_ref, shift_ref, dcv, ext_ref, 2, SHORT_K, HALO - (SHORT_K - 1), t_rows)
        du = _window_sum(nxt_ref, 2, lambda i: scw[SHORT_K - 1 - i:SHORT_K - i, :], SHORT_K, 0, t_rows)
        put_dp(S_C, du * s_x)
        put_dp(S_X, du * s_c)

        nxt_ref[:, t_rows:t_rows + HALO, :] = nxt_ref[:, 0:HALO, :]

        @pl.when(jnp.logical_and(b == nb - 1, r == nc - 1))
        def _():
            lane = lax.broadcasted_iota(jnp.int32, (CHUNK, GROUP_W), 1)
            out = jnp.zeros((CHUNK, GROUP_W), F32)
            for g in range(N_GROUPS):
                col = jnp.sum(sb_acc[:, g * GROUP_W:(g + 1) * GROUP_W], axis=1, keepdims=True)
                out = jnp.where(lane == g, col, out)
            g_sb[...] = out
            g_cw[...] = jnp.sum(cw_acc[...], axis=1)
            g_scw[...] = jnp.sum(scw_acc[...], axis=1)

    def acc_spec(shape):
        nd = len(shape)
        return pl.BlockSpec(shape, lambda b, r: (0,) * nd)

    acc_shapes = [
        (N_GROUPS, GROUP_W, GROUP_W),
        (1, BRANCH_W),
        (CONV_ROWS, BRANCH_W),
        (1, BRANCH_W),
        (1, BRANCH_W),
        (1, BRANCH_W),
        (1, BRANCH_W),
        (1, BRANCH_W),
        (N_GROUPS, CHUNK, CHUNK),
        (CHUNK, GROUP_W),
        (SHORT_ROWS, BRANCH_W),
    ]
    outs, rode = _host_call(
        body,
        name=f"mixers_bwd_l{layer}",
        grid=(nb, nc),
        in_specs=[
            pl.BlockSpec((None, t_rows, PIECE_COLS), lambda b, r: (b, nc - 1 - r, 0)),
            pl.BlockSpec((None, HALO, PIECE_COLS), lambda b, r: (b, jnp.maximum((nc - 1 - r) * hb - 1, 0), 0)),
            pl.BlockSpec((None, t_rows, N_BRANCH * BRANCH_W), lambda b, r: (b, nc - 1 - r, 0)),
            pl.BlockSpec((None, t_rows, BRANCH_W), lambda b, r: (b, nc - 1 - r, 0)),
        ] + _mixer_weight_specs() + [ANY],
        out_specs=[pl.BlockSpec((None, t_rows, PIECE_COLS), lambda b, r: (b, nc - 1 - r, 0))]
        + [acc_spec(s) for s in acc_shapes],
        out_shape=[jax.ShapeDtypeStruct(dproj3.shape, BF16)] + [jax.ShapeDtypeStruct(s, F32) for s in acc_shapes],
        scratch_shapes=[
            pltpu.VMEM((3, HALO + t_rows + 8, BRANCH_W), F32),
            pltpu.VMEM((3, t_rows + HALO, BRANCH_W), F32),
            pltpu.VMEM((CHUNK, BRANCH_W), F32),
            pltpu.VMEM((CONV_ROWS, 8, BRANCH_W), F32),
            pltpu.VMEM((SHORT_ROWS, 8, BRANCH_W), F32),
            pltpu.VMEM((t_rows + 16, BRANCH_W), F32),
            pltpu.VMEM((t_rows + 8, BRANCH_W), F32),
            pltpu.VMEM((t_rows + 32, BRANCH_W), F32),
        ],
        aliases={4 + 11: 0},
        args=[proj3, proj3, dz3, cacc3, *_mixer_weight_args(mw), dproj3],
        ride=ride,
    )
    return outs[0], dict(zip(MIXER_GRADS, outs[1:])), rode


def _merge_bwd(dout2, merged2, z2, proj2, wbr_l, wo_l, layer, ride=None):
    n = dout2.shape[0]
    tm = _tile(n, 1024)
    nt = n // tm

    def body(do_ref, mg_ref, z_ref, gate_ref, wbr_ref, wo_ref, dz_ref, dg_ref, gwo_ref, gwbr_ref, dm_ref,
             awo_ref, awbr_ref):
        i = pl.program_id(0)
        nbr = pl.program_id(1)

        @pl.when(nbr == 0)
        def _():
            do_b = do_ref[...].astype(BF16)
            dm_ref[...] = _dot_nt(do_b, wo_ref[...])
            gw = _dot_tn(mg_ref[...], do_b)

            @pl.when(i == 0)
            def _():
                awo_ref[...] = gw

            @pl.when(i > 0)
            def _():
                awo_ref[...] += gw

        zt = z_ref[...]
        wbr = wbr_ref[nbr]
        bo = _dot(zt, wbr)
        gt = _sigmoid(gate_ref[...].astype(F32))
        dm = dm_ref[...]
        dbo = (dm * gt).astype(BF16)
        dg_ref[...] = (dm * bo * gt * (1.0 - gt)).astype(BF16)
        dz_ref[...] = _dot_nt(dbo, wbr).astype(BF16)
        gw = _dot_tn(zt, dbo)

        @pl.when(i == 0)
        def _():
            awbr_ref[nbr] = gw

        @pl.when(i > 0)
        def _():
            awbr_ref[nbr] += gw

        @pl.when(jnp.logical_and(i == nt - 1, nbr == N_BRANCH - 1))
        def _():
            pltpu.sync_copy(awo_ref, gwo_ref)
            pltpu.sync_copy(awbr_ref, gwbr_ref)

    return _host_call(
        body,
        name=f"merge_bwd_l{layer}",
        grid=(nt, N_BRANCH),
        in_specs=[
            pl.BlockSpec((tm, D_MODEL), lambda i, b: (i, 0)),
            pl.BlockSpec((tm, D_MODEL), lambda i, b: (i, 0)),
            pl.BlockSpec((tm, BRANCH_W), lambda i, b: (i, b)),
            pl.BlockSpec((tm, D_MODEL), lambda i, b: (i, GATE_BLOCK0 + b)),
            pl.BlockSpec((N_BRANCH, BRANCH_W, D_MODEL), lambda i, b: (0, 0, 0)),
            pl.BlockSpec((D_MODEL, D_MODEL), lambda i, b: (0, 0)),
        ],
        out_specs=[
            pl.BlockSpec((tm, BRANCH_W), lambda i, b: (i, b)),
            pl.BlockSpec((tm, D_MODEL), lambda i, b: (i, GATE_BLOCK0 + b)),
            ANY,
            ANY,
        ],
        out_shape=[
            jax.ShapeDtypeStruct((n, N_BRANCH * BRANCH_W), BF16),
            jax.ShapeDtypeStruct((n, IN_COLS), BF16),
            jax.ShapeDtypeStruct((D_MODEL, D_MODEL), F32),
            jax.ShapeDtypeStruct((N_BRANCH, BRANCH_W, D_MODEL), F32),
        ],
        scratch_shapes=[
            pltpu.VMEM((tm, D_MODEL), F32),
            pltpu.VMEM((D_MODEL, D_MODEL), F32),
            pltpu.VMEM((N_BRANCH, BRANCH_W, D_MODEL), F32),
        ],
        args=[dout2, merged2, z2, proj2, wbr_l, wo_l],
        ride=ride,
    )


def _inproj_bwd_x(dproj2, w_l, x2, g_row, dout2, layer, ride=None):
    n = x2.shape[0]
    tm = _tile(n, 1024)

    def body(dp_ref, w_ref, x_ref, g_ref, do_ref, dx_ref, dng_ref, dh_ref):
        i = pl.program_id(0)
        s = pl.program_id(1)
        part = _dot_nt(dp_ref[...], w_ref[...])

        @pl.when(s == 0)
        def _():
            dh_ref[...] = part

        @pl.when(s > 0)
        def _():
            dh_ref[...] += part

        @pl.when(jnp.logical_and(i == 0, s == 0))
        def _():
            dng_ref[...] = jnp.zeros_like(dng_ref)

        @pl.when(s == N_CHIPS - 1)
        def _():
            xv = x_ref[...]
            r = lax.rsqrt(_lanemean(xv * xv) + RMS_EPS)
            xh = xv * r
            dh = dh_ref[...]
            dng_ref[...] += _rowsum(dh * xh)
            dxh = dh * g_ref[...]
            dx_ref[...] = do_ref[...] + r * (dxh - xh * _lanemean(dxh * xh))

    return _host_call(
        body,
        name=f"inproj_bwd_x_l{layer}",
        grid=(n // tm, N_CHIPS),
        in_specs=[
            pl.BlockSpec((tm, SHARD_COLS), lambda i, s: (i, s)),
            pl.BlockSpec((None, D_MODEL, SHARD_COLS), lambda i, s: (s, 0, 0)),
            pl.BlockSpec((tm, D_MODEL), lambda i, s: (i, 0)),
            pl.BlockSpec((1, D_MODEL), lambda i, s: (0, 0)),
            pl.BlockSpec((tm, D_MODEL), lambda i, s: (i, 0)),
        ],
        out_specs=[
            pl.BlockSpec((tm, D_MODEL), lambda i, s: (i, 0)),
            pl.BlockSpec((1, D_MODEL), lambda i, s: (0, 0)),
        ],
        out_shape=[jax.ShapeDtypeStruct((n, D_MODEL), F32), jax.ShapeDtypeStruct((1, D_MODEL), F32)],
        scratch_shapes=[pltpu.VMEM((tm, D_MODEL), F32)],
        args=[dproj2, w_l, x2, g_row, dout2],
        ride=ride,
    )


def _inproj_bwd_w(h2, dproj2, layer, ride=None):
    n = h2.shape[0]
    tm = _tile(n, 2048)

    def body(h_ref, dp_ref, gw_ref):
        gw = _dot_tn(h_ref[...], dp_ref[...])

        @pl.when(pl.program_id(1) == 0)
        def _():
            gw_ref[...] = gw

        @pl.when(pl.program_id(1) > 0)
        def _():
            gw_ref[...] += gw

    outs, rode = _host_call(
        body,
        name=f"inproj_bwd_w_l{layer}",
        grid=(N_COL_TILES, n // tm),
        in_specs=[
            pl.BlockSpec((tm, D_MODEL), lambda s, i: (i, 0)),
            pl.BlockSpec((tm, COL_TILE), lambda s, i: (i, s)),
        ],
        out_specs=[pl.BlockSpec((None, D_MODEL, COL_TILE), lambda s, i: (s // 2, 0, s % 2))],
        out_shape=[jax.ShapeDtypeStruct((N_CHIPS, D_MODEL, SHARD_COLS), F32)],
        args=[h2, dproj2],
        ride=ride,
    )
    return outs[0], rode


def _adamw_math(w, g, m, v):
    m = ADAM_B1 * m + (1.0 - ADAM_B1) * g
    v = ADAM_B2 * v + (1.0 - ADAM_B2) * jnp.square(g)
    m_hat = m / (1.0 - ADAM_B1 ** ADAM_STEP)
    v_hat = v / (1.0 - ADAM_B2 ** ADAM_STEP)
    delta = -ADAM_LR * (m_hat / (jnp.sqrt(v_hat) + ADAM_EPS) + ADAM_WD * w)
    return delta, m, v


def _adamw_sharded(w, m, v, part, layer, prev, name, ride=None, after=None):
    assert prev is None or after is None
    _, rows, cols = w.shape
    tr = _tile(rows, max(16, (1 << 21) // (4 * cols) // 16 * 16), mult=16)

    def body(w_ref, m_ref, v_ref, p_ref, *rest):
        g_out, d_out, m_out, v_out = rest[-4:]
        g = ((p_ref[0].astype(F32) + p_ref[1].astype(F32)) + p_ref[2].astype(F32)) + p_ref[3].astype(F32)
        delta, m2, v2 = _adamw_math(w_ref[...], g, m_ref[...], v_ref[...])
        g_out[...] = g
        d_out[...] = delta
        m_out[...] = m2
        v_out[...] = v2

    spec = pl.BlockSpec((None, tr, cols), lambda i: (layer, i, 0))
    return _host_call(
        body,
        name=f"adamw_{name}_l{layer}",
        grid=(rows // tr,),
        in_specs=[spec, spec, spec, pl.BlockSpec((N_CHIPS, tr, cols), lambda i: (0, i, 0))]
        + ([ANY] * 4 if prev else []) + ([ANY] if after is not None else []),
        out_specs=[spec] * 4,
        out_shape=[jax.ShapeDtypeStruct(w.shape, F32)] * 4,
        args=[w, m, v, part] + (list(prev) if prev else []) + ([after] if after is not None else []),
        aliases={4 + k: k for k in range(4)} if prev else {},
        ride=ride,
    )


def _adamw_packed(w, m, v, g):
    rows = w.shape[0]
    tr = _tile(rows, rows // 2 if rows % 16 == 0 else rows)

    def body(w_ref, m_ref, v_ref, g_ref, d_out, m_out, v_out):
        delta, m2, v2 = _adamw_math(w_ref[...], g_ref[...], m_ref[...], v_ref[...])
        d_out[...] = delta
        m_out[...] = m2
        v_out[...] = v2

    spec = pl.BlockSpec((tr, GROUP_W), lambda i: (i, 0))
    return pl.pallas_call(
        body,
        name="adamw_small",
        grid=(rows // tr,),
        in_specs=[spec] * 4,
        out_specs=[spec] * 3,
        out_shape=[jax.ShapeDtypeStruct(w.shape, F32)] * 3,
        compiler_params=_params("arbitrary"),
    )(w, m, v, g)


SMALL = ["norm_g", "pool_w", "pool_scale", "conv_b", "conv_ln_g", "conv_ln_b", "sgu_ln_g", "sgu_ln_b", "sgu_w",
         "sgu_b", "final_g"]
WEIGHTS = ["norm_g", "w_in", "pool_w", "pool_scale", "conv_w", "conv_b", "conv_ln_g", "conv_ln_b", "sgu_ln_g",
           "sgu_ln_b", "sgu_w", "sgu_b", "sc_w", "w_branch", "w_o", "final_g"]
HALF_SHAPES = [(D_MODEL // 2, SHARD_COLS), (N_BRANCH * BRANCH_W // 2, BR_SHARD), (BR_SHARD // 2, D_MODEL),
               (CS_ROWS // 2, GROUP_W)]


def _pack_small(tree, last_rows=None):
    tail = jnp.zeros((8, GROUP_W), F32) if last_rows is None else last_rows
    return jnp.concatenate([tree[k].reshape(-1, GROUP_W) for k in SMALL] + [tail], axis=0)


def _unpack_small(packed, like):
    out, r = {}, 0
    for k in SMALL:
        nr = like[k].size // GROUP_W
        out[k] = packed[r:r + nr].reshape(like[k].shape)
        r += nr
    return out


def _pad_rows(a, rows):
    pad = [(0, 0)] * a.ndim
    pad[-2] = (0, rows - a.shape[-2])
    return jnp.pad(a, pad)


def _pack_cs(conv, short):
    return jnp.concatenate([_pad_rows(conv, CS_ROWS // 2), _pad_rows(short, CS_ROWS // 2)], axis=-2)


def _shard_major_rows(a):
    return a.reshape(a.shape[0], N_CHIPS, GROUP_W).transpose(1, 0, 2)


def kernel(x, norm_g, w_in, pool_w, pool_scale, conv_w, conv_b, conv_ln_g, conv_ln_b, sgu_ln_g, sgu_ln_b, sgu_w, sgu_b, sc_w, w_branch, w_o, final_g, loss_target, m_norm_g, m_w_in, m_pool_w, m_pool_scale, m_conv_w, m_conv_b, m_conv_ln_g, m_conv_ln_b, m_sgu_ln_g, m_sgu_ln_b, m_sgu_w, m_sgu_b, m_sc_w, m_w_branch, m_w_o, m_final_g, v_norm_g, v_w_in, v_pool_w, v_pool_scale, v_conv_w, v_conv_b, v_conv_ln_g, v_conv_ln_b, v_sgu_ln_g, v_sgu_ln_b, v_sgu_w, v_sgu_b, v_sc_w, v_w_branch, v_w_o, v_final_g):
    w = dict(norm_g=norm_g, w_in=w_in, pool_w=pool_w, pool_scale=pool_scale, conv_w=conv_w, conv_b=conv_b,
             conv_ln_g=conv_ln_g, conv_ln_b=conv_ln_b, sgu_ln_g=sgu_ln_g, sgu_ln_b=sgu_ln_b, sgu_w=sgu_w,
             sgu_b=sgu_b, sc_w=sc_w, w_branch=w_branch, w_o=w_o, final_g=final_g)
    mom = dict(norm_g=m_norm_g, w_in=m_w_in, pool_w=m_pool_w, pool_scale=m_pool_scale, conv_w=m_conv_w,
               conv_b=m_conv_b, conv_ln_g=m_conv_ln_g, conv_ln_b=m_conv_ln_b, sgu_ln_g=m_sgu_ln_g,
               sgu_ln_b=m_sgu_ln_b, sgu_w=m_sgu_w, sgu_b=m_sgu_b, sc_w=m_sc_w, w_branch=m_w_branch, w_o=m_w_o,
               final_g=m_final_g)
    var = dict(norm_g=v_norm_g, w_in=v_w_in, pool_w=v_pool_w, pool_scale=v_pool_scale, conv_w=v_conv_w,
               conv_b=v_conv_b, conv_ln_g=v_conv_ln_g, conv_ln_b=v_conv_ln_b, sgu_ln_g=v_sgu_ln_g,
               sgu_ln_b=v_sgu_ln_b, sgu_w=v_sgu_w, sgu_b=v_sgu_b, sc_w=v_sc_w, w_branch=v_w_branch, w_o=v_w_o,
               final_g=v_final_g)

    nb, seq, _ = x.shape
    n = nb * seq
    core = lax.axis_index("c").astype(jnp.int32).reshape(1)

    win_s = w_in.astype(BF16).reshape(DEPTH, 2, D_MODEL // 2, SHARD_COLS)
    wbr_s = w_branch.astype(BF16)
    wo_s = w_o.astype(BF16)
    cs_s = _pack_cs(conv_w, sc_w)

    (win_f0,) = _allgather_layer0([win_s])

    def layer_weights(win_f, wbr_f, wo_f, cs_f):
        cs_f = cs_f.reshape(N_CHIPS, CS_ROWS, GROUP_W)
        return dict(win=win_f.reshape(N_CHIPS, D_MODEL, SHARD_COLS),
                    wbr=wbr_f.reshape(N_CHIPS, N_BRANCH, BRANCH_W, BR_SHARD).transpose(1, 2, 0, 3).reshape(
                        N_BRANCH, BRANCH_W, D_MODEL),
                    wo=wo_f.reshape(D_MODEL, D_MODEL),
                    conv_w=cs_f[:, :CONV_ROWS], sc_w=cs_f[:, CS_ROWS // 2:CS_ROWS // 2 + SHORT_ROWS])

    def mixer_weights(l, gathered):
        row = lambda a: a[l].reshape(1, BRANCH_W)
        bias = jnp.repeat(jnp.swapaxes(sgu_b[l], 0, 1), GROUP_W, axis=1)
        return dict(pool_w=pool_w[l].astype(BF16), pool_scale=row(pool_scale), conv_w=gathered["conv_w"],
                    conv_b=row(conv_b), conv_ln_g=row(conv_ln_g), conv_ln_b=row(conv_ln_b), sgu_ln_g=row(sgu_ln_g),
                    sgu_ln_b=row(sgu_ln_b), sgu_w=sgu_w[l], sgu_bias=bias, sc_w=gathered["sc_w"])

    lw, mw = [None, None], [None, None]

    xs, projs, hs, zs, mgs, caccs = [x.reshape(n, D_MODEL)], [], [], [], [], []
    for l in range(DEPTH):
        first = l == 0
        win_l = win_f0.reshape(N_CHIPS, D_MODEL, SHARD_COLS) if first else lw[1]["win"]
        ride = _join(_gather_whole([wbr_s, wo_s, cs_s], 0), _gather_layer1_first_half(win_s)) if first else None
        (proj, h), rode = _inproj_fwd(xs[l], norm_g[l:l + 1], win_l, l, ride=ride)
        if first:
            lw[0] = layer_weights(win_f0, *rode[:3])
            mw[0] = mixer_weights(0, lw[0])
            win_f1 = rode[3]
        proj3, x3 = proj.reshape(nb, seq, IN_COLS), xs[l].reshape(nb, seq, D_MODEL)
        if first:
            ride = _join(_gather_layer1_second_half(win_s, win_f1), _gather_whole([wbr_s, wo_s, cs_s], 1))
            (z3, cacc, mg3, xn3), rode = _mix_merge_fwd(proj3, x3, mw[0], lw[0]["wbr"], lw[0]["wo"], 0, ride=ride)
            lw[1] = layer_weights(*rode)
            mw[1] = mixer_weights(1, lw[1])
            xs.append(xn3.reshape(n, D_MODEL))
        else:
            (z3, cacc, mg3, dx3, loss_part, g_final), _ = _mix_merge_fwd(
                proj3, x3, mw[l], lw[l]["wbr"], lw[l]["wo"], l,
                head=(final_g.reshape(1, D_MODEL), loss_target))
            dx = dx3.reshape(n, D_MODEL)
        caccs.append(cacc)
        projs.append(proj)
        hs.append(h)
        zs.append(z3.reshape(n, N_BRANCH * BRANCH_W))
        mgs.append(mg3.reshape(n, D_MODEL))

    def chip_major(gwbr):
        return gwbr.reshape(N_BRANCH, BRANCH_W, N_CHIPS, BR_SHARD).transpose(2, 0, 1, 3)

    def as_halves(arrays, first_index=0):
        return [g.reshape((N_CHIPS, 2) + s) for g, s in zip(arrays, HALF_SHAPES[first_index:])]

    def add_halves(grads, others, l, first_index=0):
        return [_add_halves(g, o, core, f"add_halves_l{l}_{first_index + i}")
                for i, (g, o) in enumerate(zip(grads, others))]

    per_layer, parts = {}, [None] * DEPTH
    (dz, dproj, gwo, gwbr), _ = _merge_bwd(dx, mgs[1], zs[1], projs[1], lw[1]["wbr"], lw[1]["wo"], 1)
    dproj3, gm, _ = _mixers_bwd(projs[1].reshape(nb, seq, IN_COLS), dz.reshape(nb, seq, N_BRANCH * BRANCH_W), caccs[1],
                                dproj.reshape(nb, seq, IN_COLS), mw[1], 1)
    dproj = dproj3.reshape(n, IN_COLS)
    gwin, _ = _inproj_bwd_w(hs[1], dproj, 1)
    gcs = _pack_cs(_shard_major_rows(gm["conv_w"]), _shard_major_rows(gm["sc_w"]))
    grads1 = as_halves([gwin, chip_major(gwbr), gwo, gcs])
    (dx, g_norm), others1 = _inproj_bwd_x(dproj, lw[1]["win"], xs[1], norm_g[1:2], dx, 1, ride=_swap_halves(grads1))
    gm["norm_g"] = g_norm
    per_layer[1] = gm
    sums1 = add_halves(grads1, others1, 1)
    (dz, dproj, gwo, gwbr), scattered1 = _merge_bwd(dx, mgs[0], zs[0], projs[0], lw[0]["wbr"], lw[0]["wo"], 0,
                                                    ride=_scatter_chip_sums(sums1))
    dproj3, gm, parts[1] = _mixers_bwd(projs[0].reshape(nb, seq, IN_COLS), dz.reshape(nb, seq, N_BRANCH * BRANCH_W),
                                       caccs[0], dproj.reshape(nb, seq, IN_COLS), mw[0], 0,
                                       ride=_share_halves(scattered1))
    dproj = dproj3.reshape(n, IN_COLS)
    gcs = _pack_cs(_shard_major_rows(gm["conv_w"]), _shard_major_rows(gm["sc_w"]))
    small0 = as_halves([chip_major(gwbr), gwo, gcs], first_index=1)

    per_layer[0] = dict(gm, norm_g=jnp.zeros((1, D_MODEL), F32))

    def stack(k, shape):
        return jnp.stack([per_layer[l][k] for l in range(DEPTH)]).reshape(shape)

    g_small = {"final_g": g_final.reshape(D_MODEL), "norm_g": stack("norm_g", (DEPTH, D_MODEL)),
               "pool_w": stack("pool_w", pool_w.shape), "sgu_w": stack("sgu_w", sgu_w.shape),
               "sgu_b": jnp.swapaxes(stack("sgu_b", (DEPTH, CHUNK, GROUP_W))[:, :, :N_GROUPS], 1, 2)}
    for k in ("pool_scale", "conv_b", "conv_ln_g", "conv_ln_b", "sgu_ln_g", "sgu_ln_b"):
        g_small[k] = stack(k, (DEPTH, BRANCH_W))
    small_part = _pack_small(g_small, jnp.broadcast_to(loss_part, (8, GROUP_W)))

    gwin, rode = _inproj_bwd_w(hs[0], dproj, 0, ride=_join(_swap_halves(small0), _swap_whole(small_part)))
    others_small0, small_other = rode[:3], rode[3]
    small_chip = _add2(small_part, small_other, "add_small_sibling")
    (gwin_h,) = as_halves([gwin])

    def flat(a, i):
        rows, cols = 2 * HALF_SHAPES[i][0], HALF_SHAPES[i][1]
        return a.reshape(a.shape[0], rows, cols)

    names4 = ["w_in", "w_branch", "w_o", "conv_sc"]

    def packed4(t):
        return [t["w_in"], t["w_branch"], t["w_o"], _pack_cs(t["conv_w"], t["sc_w"])]

    w4, m4, v4 = ([flat(a, i) for i, a in enumerate(packed4(t))] for t in (w, mom, var))

    def sharded_update(i, layer, prev, after=None):
        return _adamw_sharded(w4[i], m4[i], v4[i], flat(parts[layer][i], i), layer, prev, names4[i], after=after)

    send_sem, recv_sem, gwin_thru, land_thru, token = _swap_halves_start(gwin_h, "swap_halves_l0_w_in")
    updated = [sharded_update(i, 1, None, after=token if i == 0 else None)[0] for i in range(4)]
    gwin_h, other_win0 = _swap_halves_wait(send_sem, recv_sem, gwin_thru, land_thru, updated[0][0],
                                           "swap_halves_l0_w_in")
    sums0 = add_halves([gwin_h], [other_win0], 0) + add_halves(small0, others_small0, 0, first_index=1)
    (dx, g_norm0), rode = _inproj_bwd_x(dproj, lw[0]["win"], xs[0], norm_g[0:1], dx, 0,
                                        ride=_join(_scatter_chip_sums(sums0), _spread_chip_sums(small_chip)))
    scattered0, small_chips = rode[:4], rode[4]
    grad_x = dx.reshape(nb, seq, D_MODEL)

    g_norm0, parts[0] = _allreduce_small(g_norm0.reshape(D_MODEL // GROUP_W, GROUP_W), _share_halves(scattered0))
    g_packed = lax.dynamic_update_slice(_sum_chips(small_chips, "sum_small_chips"), g_norm0, (0, 0))
    loss = g_packed[g_packed.shape[0] - 8, 0]
    d_packed, m_packed, v_packed = _adamw_packed(_pack_small(w), _pack_small(mom), _pack_small(var), g_packed)
    grads, deltas, new_m, new_v = {}, {}, {}, {}
    for tree, packed in ((grads, g_packed), (deltas, d_packed), (new_m, m_packed), (new_v, v_packed)):
        tree.update(_unpack_small(packed, w))

    for i, name in enumerate(names4):
        res, _ = sharded_update(i, 0, updated[i])
        for tree, r in zip((grads, deltas, new_m, new_v), res):
            if name == "conv_sc":
                tree["conv_w"] = r[:, :CONV_K]
                tree["sc_w"] = r[:, CS_ROWS // 2:CS_ROWS // 2 + SHORT_K]
            else:
                tree[name] = r.reshape(w[name].shape)

    return (loss, grad_x, *[grads[k] for k in WEIGHTS], *[deltas[k] for k in WEIGHTS],
            *[new_m[k] for k in WEIGHTS], *[new_v[k] for k in WEIGHTS])
```

```python
import functools

import jax
import jax.numpy as jnp
from jax import lax
from jax.experimental import pallas as pl
from jax.experimental.pallas import tpu as pltpu

F32 = jnp.float32
BF16 = jnp.bfloat16

D_MODEL = 1024
DEPTH = 2
N_BRANCH = 4
BRANCH_W = 512
N_GROUPS = 4
GROUP_W = 128
POOL_WINDOWS = (2, 4, 8, 16)
CONV_K = 31
SHORT_K = 3
CHUNK = 128
N_PIECES = 12
PIECE_COLS = N_PIECES * BRANCH_W
IN_COLS = PIECE_COLS + N_BRANCH * D_MODEL
N_CHIPS = 4
SHARD_COLS = IN_COLS // N_CHIPS
BR_SHARD = D_MODEL // N_CHIPS
COL_TILE = SHARD_COLS // 2
N_COL_TILES = IN_COLS // COL_TILE
GATE_BLOCK0 = PIECE_COLS // D_MODEL
RMS_EPS = 1e-6
LN_EPS = 1e-5
HALO = 32
CONV_ROWS = 32
SHORT_ROWS = 8
CS_ROWS = 64

ADAM_LR = 0.001
ADAM_B1 = 0.9
ADAM_B2 = 0.999
ADAM_EPS = 1e-08
ADAM_WD = 0.01
ADAM_STEP = 10

VMEM_LIMIT = 60 * 1024 * 1024
MESH_ID = pl.DeviceIdType.MESH
ANY = pl.BlockSpec(memory_space=pl.ANY)
VMEM_WHOLE = pl.BlockSpec(memory_space=pltpu.VMEM)

(P_X, P_GATE, C_A, C_B, C_GATE, G_U, G_V, G_GATE, S_B, S_C, S_X, S_GATE) = range(N_PIECES)


def _params(*sem):
    return pltpu.CompilerParams(dimension_semantics=sem, vmem_limit_bytes=VMEM_LIMIT)


def _sigmoid(v):
    return 0.5 * jnp.tanh(0.5 * v) + 0.5


def _silu(v):
    return v * _sigmoid(v)


def _silu_pair(v):
    s = _sigmoid(v)
    return v * s, s * (1.0 + v * (1.0 - s))


def _dot(a, b):
    return jnp.dot(a, b, preferred_element_type=F32)


def _dot_nt(a, b):
    return lax.dot_general(a, b, (((1,), (1,)), ((), ())), preferred_element_type=F32)


def _dot_tn(a, b):
    return lax.dot_general(a, b, (((0,), (0,)), ((), ())), preferred_element_type=F32)


def _rowsum(v):
    return jnp.sum(v, axis=0, keepdims=True)


def _lanemean(v):
    return jnp.mean(v, axis=-1, keepdims=True)


def _tile(n, want, mult=8):
    t = max(1, min(n, want))
    while n % t or (t % mult and t != n):
        t -= 1
    return t


def _place():
    x, y, c = lax.axis_index("x"), lax.axis_index("y"), lax.axis_index("c")
    chip = 2 * x + y
    peers = [(1 - x, y), (x, 1 - y), (1 - x, 1 - y)]
    return x, y, c, chip, peers


def _remote(src, dst, send_sem, recv_sem, dev):
    return pltpu.make_async_remote_copy(src_ref=src, dst_ref=dst, send_sem=send_sem, recv_sem=recv_sem,
                                        device_id=dev, device_id_type=MESH_ID)


class _Exchange:
    def __init__(self, inputs, out_shapes, plan, n_remote, n_local=0, aliases=None):
        self.inputs = list(inputs)
        self.out_shapes = list(out_shapes)
        self.plan = plan
        self.n_remote = n_remote
        self.n_local = n_local
        self.aliases = dict(aliases or {})

    def copies(self, in_refs, out_refs, send_sems, recv_sems, loc_sems):
        remote, local = self.plan(in_refs, out_refs)
        assert len(remote) == self.n_remote and len(local) == self.n_local
        cps = [_remote(s, d, send_sems.at[t], recv_sems.at[t], dev) for t, (s, d, dev) in enumerate(remote)]
        cps += [pltpu.make_async_copy(s, d, loc_sems.at[t]) for t, (s, d) in enumerate(local)]
        return cps

    def sem_shapes(self):
        return [pltpu.SemaphoreType.DMA((max(self.n_remote, 1),)), pltpu.SemaphoreType.DMA((max(self.n_remote, 1),)),
                pltpu.SemaphoreType.DMA((max(self.n_local, 1),))]


def _exchange_call(name, ex):
    n_ci = len(ex.inputs)

    def body(*refs):
        cins, couts = refs[:n_ci], refs[n_ci:n_ci + len(ex.out_shapes)]
        cps = ex.copies(cins, couts, *refs[n_ci + len(ex.out_shapes):])
        for cp in cps:
            cp.start()
        for cp in cps:
            cp.wait()

    return pl.pallas_call(
        body,
        name=name,
        in_specs=[ANY] * n_ci,
        out_specs=[ANY] * len(ex.out_shapes),
        out_shape=ex.out_shapes,
        scratch_shapes=ex.sem_shapes(),
        input_output_aliases=ex.aliases,
    )(*ex.inputs)


def _host_call(body, *, name, grid, in_specs, out_specs, out_shape, args, scratch_shapes=(), aliases=None, ride=None):
    n_in, n_out, n_scr = len(in_specs), len(out_specs), len(scratch_shapes)
    sem = ("arbitrary",) * len(grid)
    aliases = dict(aliases or {})
    if ride is None:
        outs = pl.pallas_call(body, name=name, grid=grid, in_specs=list(in_specs), out_specs=list(out_specs),
                              out_shape=list(out_shape), scratch_shapes=list(scratch_shapes),
                              input_output_aliases=aliases, compiler_params=_params(*sem))(*args)
        return list(outs), []
    n_ci, n_co = len(ride.inputs), len(ride.out_shapes)

    def full_body(*refs):
        ins, cins = refs[:n_in], refs[n_in:n_in + n_ci]
        o0 = n_in + n_ci
        outs, couts = refs[o0:o0 + n_out], refs[o0 + n_out:o0 + n_out + n_co]
        s0 = o0 + n_out + n_co
        scr, sems = refs[s0:s0 + n_scr], refs[s0 + n_scr:]
        first = functools.reduce(jnp.logical_and, [pl.program_id(d) == 0 for d in range(len(grid))])
        last = functools.reduce(jnp.logical_and, [pl.program_id(d) == grid[d] - 1 for d in range(len(grid))])

        @pl.when(first)
        def _():
            for cp in ride.copies(cins, couts, *sems):
                cp.start()

        body(*ins, *outs, *scr)

        @pl.when(last)
        def _():
            for cp in ride.copies(cins, couts, *sems):
                cp.wait()

    for ci, co in ride.aliases.items():
        aliases[n_in + ci] = n_out + co
    outs = pl.pallas_call(
        full_body, name=name, grid=grid, in_specs=list(in_specs) + [ANY] * n_ci,
        out_specs=list(out_specs) + [ANY] * n_co, out_shape=list(out_shape) + ride.out_shapes,
        scratch_shapes=list(scratch_shapes) + ride.sem_shapes(), input_output_aliases=aliases,
        compiler_params=_params(*sem))(*args, *ride.inputs)
    return list(outs[:n_out]), list(outs[n_out:])


def _allgather_layer0(shards):
    na = len(shards)

    def body(*refs):
        ins, outs = refs[:na], refs[na:2 * na]
        send_sems, recv_sems, fsend_sems, frecv_sems, loc_sems = refs[2 * na:]
        x, y, c, chip, peers = _place()
        sib = (x, y, 1 - c)
        locs = [pltpu.make_async_copy(ins[a].at[0], outs[a].at[chip], loc_sems.at[a]) for a in range(na)]
        for cp in locs:
            cp.start()
        pending = []
        for k, (px, py) in enumerate(peers):
            for a in range(na):
                cp = _remote(ins[a].at[0, c], outs[a].at[chip, c], send_sems.at[k * na + a],
                             recv_sems.at[k * na + a], (px, py, c))
                cp.start()
                pending.append(cp)
        for k, (px, py) in enumerate(peers):
            pchip = 2 * px + py
            for a in range(na):
                slab = outs[a].at[pchip, c]
                _remote(slab, slab, send_sems.at[k * na + a], recv_sems.at[k * na + a], (px, py, c)).wait_recv()
                cp = _remote(slab, slab, fsend_sems.at[k * na + a], frecv_sems.at[k * na + a], sib)
                cp.start()
                pending.append(cp)
        for k, (px, py) in enumerate(peers):
            pchip = 2 * px + py
            for a in range(na):
                slab = outs[a].at[pchip, 1 - c]
                _remote(slab, slab, fsend_sems.at[k * na + a], frecv_sems.at[k * na + a], sib).wait_recv()
        for cp in pending:
            cp.wait_send()
        for cp in locs:
            cp.wait()

    return pl.pallas_call(
        body,
        name="allgather_layer0",
        in_specs=[ANY] * na,
        out_specs=[ANY] * na,
        out_shape=[jax.ShapeDtypeStruct((N_CHIPS,) + a.shape[1:], a.dtype) for a in shards],
        scratch_shapes=[pltpu.SemaphoreType.DMA((3 * na,))] * 4 + [pltpu.SemaphoreType.DMA((na,))],
    )(*shards)


def _gather_layer1_halves(win_s):
    def plan(cins, couts):
        _, _, c, chip, peers = _place()
        (src,), (dst,) = cins, couts
        return ([(src.at[1, c], dst.at[chip, c], (px, py, c)) for px, py in peers], [(src.at[1], dst.at[chip])])

    return _Exchange([win_s], [jax.ShapeDtypeStruct((N_CHIPS,) + win_s.shape[1:], win_s.dtype)], plan, 3, 1)


def _forward_layer1_halves(win_f1):
    def plan(cins, couts):
        x, y, c, _, peers = _place()
        (dst,) = couts
        return [(dst.at[2 * px + py, c], dst.at[2 * px + py, c], (x, y, 1 - c)) for px, py in peers], []

    return _Exchange([win_f1], [jax.ShapeDtypeStruct(win_f1.shape, win_f1.dtype)], plan, 3, 0, aliases={0: 0})


def _gather_whole(shards, layer):
    na = len(shards)

    def plan(cins, couts):
        _, _, c, chip, peers = _place()
        remote = [(cins[a].at[layer], couts[a].at[chip], (px, py, c)) for a in range(na) for px, py in peers]
        return remote, [(cins[a].at[layer], couts[a].at[chip]) for a in range(na)]

    return _Exchange(shards, [jax.ShapeDtypeStruct((N_CHIPS,) + a.shape[1:], a.dtype) for a in shards], plan,
                     3 * na, na)


def _join(a, b):
    n_in, n_out = len(a.inputs), len(a.out_shapes)

    def plan(cins, couts):
        remote_a, local_a = a.plan(cins[:n_in], couts[:n_out])
        remote_b, local_b = b.plan(cins[n_in:], couts[n_out:])
        return remote_a + remote_b, local_a + local_b

    aliases = dict(a.aliases)
    aliases.update({n_in + i: n_out + o for i, o in b.aliases.items()})
    return _Exchange(a.inputs + b.inputs, a.out_shapes + b.out_shapes, plan, a.n_remote + b.n_remote,
                     a.n_local + b.n_local, aliases)


def _swap_halves(grads):
    def plan(cins, couts):
        x, y, c, _, _ = _place()
        return [(g.at[:, 1 - c], r, (x, y, 1 - c)) for g, r in zip(cins, couts)], []

    return _Exchange(grads, [jax.ShapeDtypeStruct(g.shape[:1] + g.shape[2:], g.dtype) for g in grads], plan,
                     len(grads))


def _scatter_chip_sums(sums):
    na = len(sums)

    def plan(cins, couts):
        _, _, c, chip, peers = _place()
        remote = [(cins[a].at[2 * px + py], couts[a].at[chip, c], (px, py, c)) for a in range(na) for px, py in peers]
        return remote, [(cins[a].at[chip], couts[a].at[chip, c]) for a in range(na)]

    return _Exchange(sums, [jax.ShapeDtypeStruct((N_CHIPS, 2) + s.shape[1:], s.dtype) for s in sums], plan,
                     3 * na, na)


def _share_halves(parts):
    na = len(parts)

    def plan(cins, couts):
        x, y, c, _, _ = _place()
        return [(p.at[:, c], p.at[:, c], (x, y, 1 - c)) for p in couts], []

    return _Exchange(parts, [jax.ShapeDtypeStruct(p.shape, p.dtype) for p in parts], plan, na, 0,
                     aliases={a: a for a in range(na)})


def _allreduce_small(packed, ride):
    rows = packed.shape[0]
    n_ci, n_co = len(ride.inputs), len(ride.out_shapes)

    def body(*refs):
        sg_ref, cins = refs[0], refs[1:1 + n_ci]
        res_ref, couts = refs[1 + n_ci], refs[2 + n_ci:2 + n_ci + n_co]
        sib_ref, cs_ref, rem_ref, s1_send, s1_recv, s2_send, s2_recv = refs[2 + n_ci + n_co:9 + n_ci + n_co]
        riding = ride.copies(cins, couts, *refs[9 + n_ci + n_co:])
        for cp in riding:
            cp.start()
        x, y, c, chip, peers = _place()
        cp = _remote(sg_ref, sib_ref, s1_send, s1_recv, (x, y, 1 - c))
        cp.start()
        cp.wait()
        cs_ref[...] = sg_ref[...] + sib_ref[...]
        rem_ref[chip] = cs_ref[...]
        cps = [_remote(cs_ref, rem_ref.at[chip], s2_send.at[k], s2_recv.at[chip], (px, py, c))
               for k, (px, py) in enumerate(peers)]
        for cp in cps:
            cp.start()
        for k, (px, py) in enumerate(peers):
            pchip = 2 * px + py
            _remote(cs_ref, rem_ref.at[pchip], s2_send.at[k], s2_recv.at[pchip], (px, py, c)).wait_recv()
        for cp in cps:
            cp.wait_send()
        res_ref[...] = ((rem_ref[0] + rem_ref[1]) + rem_ref[2]) + rem_ref[3]
        for cp in riding:
            cp.wait()

    outs = pl.pallas_call(
        body,
        name="allreduce_small",
        in_specs=[VMEM_WHOLE] + [ANY] * n_ci,
        out_specs=[VMEM_WHOLE] + [ANY] * n_co,
        out_shape=[jax.ShapeDtypeStruct((rows, GROUP_W), F32)] + ride.out_shapes,
        scratch_shapes=[
            pltpu.VMEM((rows, GROUP_W), F32),
            pltpu.VMEM((rows, GROUP_W), F32),
            pltpu.VMEM((N_CHIPS, rows, GROUP_W), F32),
            pltpu.SemaphoreType.DMA,
            pltpu.SemaphoreType.DMA,
            pltpu.SemaphoreType.DMA((3,)),
            pltpu.SemaphoreType.DMA((N_CHIPS,)),
        ] + ride.sem_shapes(),
        input_output_aliases={1 + i: 1 + o for i, o in ride.aliases.items()},
        compiler_params=pltpu.CompilerParams(vmem_limit_bytes=VMEM_LIMIT),
    )(packed, *ride.inputs)
    return outs[0], list(outs[1:])


HBM_SPEC = pl.BlockSpec(memory_space=pltpu.HBM)
SEM_SPEC = pl.BlockSpec(memory_space=pltpu.SEMAPHORE)
SIDE_EFFECT = pltpu.SideEffectType.DATAFLOW_SIDE_EFFECTING


def _swap_halves_start(grad, name):
    land_shape = grad.shape[:1] + grad.shape[2:]

    def body(g_ref, land_ref, send_sem, recv_sem, g_thru, land_thru, token):
        del g_thru, land_thru
        x, y, c, _, _ = _place()
        _remote(g_ref.at[:, 1 - c], land_ref, send_sem, recv_sem, (x, y, 1 - c)).start()
        token[...] = jnp.zeros_like(token)

    return pl.pallas_call(
        body,
        name=name + "_start",
        out_shape=(pltpu.SemaphoreType.DMA(()), pltpu.SemaphoreType.DMA(()), pltpu.HBM(grad.shape, grad.dtype),
                   pltpu.HBM(land_shape, grad.dtype), jax.ShapeDtypeStruct((8, GROUP_W), F32)),
        in_specs=(HBM_SPEC, HBM_SPEC),
        out_specs=(SEM_SPEC, SEM_SPEC, HBM_SPEC, HBM_SPEC, VMEM_WHOLE),
        input_output_aliases={0: 2, 1: 3},
        compiler_params=pltpu.CompilerParams(has_side_effects=SIDE_EFFECT),
    )(pltpu.with_memory_space_constraint(grad, pltpu.HBM),
      pltpu.with_memory_space_constraint(lax.empty(land_shape, grad.dtype), pltpu.HBM))


def _swap_halves_wait(send_sem, recv_sem, g_thru, land_thru, after, name):
    def body(g_ref, land_ref, send_sem, recv_sem, after_ref, g_out, got_ref):
        del after_ref, g_out, got_ref
        x, y, c, _, _ = _place()
        cp = _remote(g_ref.at[:, 1 - c], land_ref, send_sem, recv_sem, (x, y, 1 - c))
        cp.wait_send()
        cp.wait_recv()

    return pl.pallas_call(
        body,
        name=name + "_wait",
        out_shape=(pltpu.HBM(g_thru.shape, g_thru.dtype), pltpu.HBM(land_thru.shape, land_thru.dtype)),
        in_specs=(HBM_SPEC, HBM_SPEC, SEM_SPEC, SEM_SPEC, ANY),
        out_specs=(HBM_SPEC, HBM_SPEC),
        input_output_aliases={0: 0, 1: 1},
        compiler_params=pltpu.CompilerParams(has_side_effects=SIDE_EFFECT),
    )(g_thru, land_thru, send_sem, recv_sem, after)


def _swap_whole(packed):
    def plan(cins, couts):
        x, y, c, _, _ = _place()
        return [(cins[0], couts[0], (x, y, 1 - c))], []

    return _Exchange([packed], [jax.ShapeDtypeStruct(packed.shape, packed.dtype)], plan, 1)


def _spread_chip_sums(chip_sum):
    def plan(cins, couts):
        _, _, c, chip, peers = _place()
        return ([(cins[0], couts[0].at[chip], (px, py, c)) for px, py in peers], [(cins[0], couts[0].at[chip])])

    return _Exchange([chip_sum], [jax.ShapeDtypeStruct((N_CHIPS,) + chip_sum.shape, chip_sum.dtype)], plan, 3, 1)


def _add2(a, b, name):
    def body(a_ref, b_ref, o_ref):
        o_ref[...] = a_ref[...] + b_ref[...]

    return pl.pallas_call(body, name=name, in_specs=[VMEM_WHOLE] * 2, out_specs=VMEM_WHOLE,
                          out_shape=jax.ShapeDtypeStruct(a.shape, F32))(a, b)


def _sum_chips(parts, name):
    def body(p_ref, o_ref):
        o_ref[...] = ((p_ref[0] + p_ref[1]) + p_ref[2]) + p_ref[3]

    return pl.pallas_call(body, name=name, in_specs=[VMEM_WHOLE], out_specs=VMEM_WHOLE,
                          out_shape=jax.ShapeDtypeStruct(parts.shape[1:], F32))(parts)


def _add_halves(grad, other, core, name):
    n_shards, _, rows, cols = grad.shape
    tr = _tile(rows, max(16, (1 << 22) // (4 * cols) // 16 * 16), mult=16)

    def body(core_ref, g_ref, o_ref, out_ref):
        del core_ref
        out_ref[...] = (g_ref[...] + o_ref[...]).astype(BF16)

    return pl.pallas_call(
        body,
        name=name,
        grid_spec=pltpu.PrefetchScalarGridSpec(
            num_scalar_prefetch=1,
            grid=(n_shards, rows // tr),
            in_specs=[
                pl.BlockSpec((None, None, tr, cols), lambda s, i, core_ref: (s, core_ref[0], i, 0)),
                pl.BlockSpec((None, tr, cols), lambda s, i, core_ref: (s, i, 0)),
            ],
            out_specs=pl.BlockSpec((None, tr, cols), lambda s, i, core_ref: (s, i, 0)),
        ),
        out_shape=jax.ShapeDtypeStruct((n_shards, rows, cols), BF16),
        compiler_params=_params("arbitrary", "arbitrary"),
    )(core, grad, other)


def _inproj_fwd(x2, g_row, w_l, layer, ride=None):
    n = x2.shape[0]
    tm = _tile(n, 2048)

    def body(x_ref, g_ref, w_ref, p_ref, h_ref):
        @pl.when(pl.program_id(1) == 0)
        def _():
            xv = x_ref[...]
            r = lax.rsqrt(_lanemean(xv * xv) + RMS_EPS)
            h_ref[...] = (xv * r * g_ref[...]).astype(BF16)

        p_ref[...] = _dot(h_ref[...], w_ref[...]).astype(BF16)

    return _host_call(
        body,
        name=f"inproj_fwd_l{layer}",
        grid=(n // tm, N_COL_TILES),
        in_specs=[
            pl.BlockSpec((tm, D_MODEL), lambda i, j: (i, 0)),
            pl.BlockSpec((1, D_MODEL), lambda i, j: (0, 0)),
            pl.BlockSpec((None, D_MODEL, COL_TILE), lambda i, j: (j // 2, 0, j % 2)),
        ],
        out_specs=[
            pl.BlockSpec((tm, COL_TILE), lambda i, j: (i, j)),
            pl.BlockSpec((tm, D_MODEL), lambda i, j: (i, 0)),
        ],
        out_shape=[jax.ShapeDtypeStruct((n, IN_COLS), BF16), jax.ShapeDtypeStruct((n, D_MODEL), BF16)],
        args=[x2, g_row, w_l],
        ride=ride,
    )


def _layer_norm_parts(v, g, b):
    mu = _lanemean(v)
    d = v - mu
    rstd = lax.rsqrt(_lanemean(d * d) + LN_EPS)
    xh = d * rstd
    return xh, rstd, xh * g + b


def _window_sum(ref, slot, weight_row, n_taps, base, t_rows):
    total = None
    for b in range(8):
        group = None
        for a in range((base + n_taps - 1) // 8 + 1):
            k = 8 * a + b - base
            if 0 <= k < n_taps:
                term = weight_row(k) * ref[slot, 8 * a:8 * a + t_rows + 8, :]
                group = term if group is None else group + term
        if group is not None:
            part = group[b:b + t_rows, :]
            total = part if total is None else total + part
    return total


def _tap_grads(acc_ref, pad_ref, shift_ref, dy, ref, slot, n_taps, base, t_rows):
    pad_ref[0:8, :] = jnp.zeros((8, BRANCH_W), F32)
    pad_ref[8:8 + t_rows, :] = dy
    pad_ref[8 + t_rows:16 + t_rows, :] = jnp.zeros((8, BRANCH_W), F32)
    for b in range(8):
        taps = [(a, 8 * a + b - base) for a in range((base + n_taps - 1) // 8 + 1) if 0 <= 8 * a + b - base < n_taps]
        if not taps:
            continue
        shift_ref[...] = pad_ref[pl.ds(8 - b, t_rows + 8), :]
        for a, k in taps:
            prod = shift_ref[...] * ref[slot, 8 * a:8 * a + t_rows + 8, :]
            acc_ref[k] += prod.reshape((t_rows + 8) // 8, 8, BRANCH_W).sum(axis=0)


def _trailing_window_sums(ext_ref, pool_ref, t_rows):
    lo, n = HALO - 16, t_rows + 16
    pool_ref[0:16, :] = jnp.zeros((16, BRANCH_W), F32)
    sums = []
    for j, win in enumerate(POOL_WINDOWS):
        cols = slice(j * GROUP_W, (j + 1) * GROUP_W)
        v = ext_ref[0, lo:lo + n, cols] + ext_ref[0, pl.ds(lo - 1, n), cols]
        width = 2
        while width < win:
            pool_ref[16:16 + n, cols] = v
            v = v + pool_ref[pl.ds(16 - width, n), cols]
            width *= 2
        sums.append(v[16:, :])
    return sums


def _leading_window_sums(nxt_ref, pool_ref, t_rows):
    n = t_rows + 16
    pool_ref[n:n + 16, :] = jnp.zeros((16, BRANCH_W), F32)
    sums = []
    for j, win in enumerate(POOL_WINDOWS):
        cols = slice(j * GROUP_W, (j + 1) * GROUP_W)
        v = nxt_ref[0, 0:n, cols] + nxt_ref[0, pl.ds(1, n), cols]
        width = 2
        while width < win:
            pool_ref[0:n, cols] = v
            v = v + pool_ref[pl.ds(width, n), cols]
            width *= 2
        sums.append(v[0:t_rows, :])
    return sums


def _tril_mask():
    r = lax.broadcasted_iota(jnp.int32, (CHUNK, CHUNK), 0)
    c = lax.broadcasted_iota(jnp.int32, (CHUNK, CHUNK), 1)
    return r >= c


def _lanes(refs4):
    return jnp.concatenate([refs4[g] for g in range(N_CHIPS)], axis=1)


def _mixer_weight_specs():
    def whole(shape):
        nd = len(shape)
        return pl.BlockSpec(shape, lambda b, c: (0,) * nd)

    return [
        whole((N_GROUPS, GROUP_W, GROUP_W)),
        whole((1, BRANCH_W)),
        whole((N_CHIPS, CONV_ROWS, GROUP_W)),
        whole((1, BRANCH_W)),
        whole((1, BRANCH_W)),
        whole((1, BRANCH_W)),
        whole((1, BRANCH_W)),
        whole((1, BRANCH_W)),
        whole((N_GROUPS, CHUNK, CHUNK)),
        whole((CHUNK, BRANCH_W)),
        whole((N_CHIPS, SHORT_ROWS, GROUP_W)),
    ]


def _mixer_weight_args(mw):
    return [mw["pool_w"], mw["pool_scale"], mw["conv_w"], mw["conv_b"], mw["conv_ln_g"], mw["conv_ln_b"],
            mw["sgu_ln_g"], mw["sgu_ln_b"], mw["sgu_w"], mw["sgu_bias"], mw["sc_w"]]


def _mix_merge_fwd(proj3, x3, mw, wbr_l, wo_l, layer, ride=None, head=None):
    nb, seq, _ = proj3.shape
    t_rows = _tile(seq, 512)
    nc = seq // t_rows
    hb = t_rows // HALO
    n_head = 2 if head else 0

    def body(*refs):
        (cur_ref, halo_ref, g01_ref, g23_ref, x_ref, pw_ref, ps_ref, cw_ref, cb_ref, clg_ref, clb_ref, slg_ref,
         slb_ref, sw_ref, sbias_ref, scw_ref, wbr_ref, wo_ref) = refs[:18]
        head_in = refs[18:18 + n_head]
        outs = refs[18 + n_head:-2]
        ext_ref, pool_ref = refs[-2:]
        z_ref, cacc_ref, mg_ref, xn_ref = outs[:4]
        b = pl.program_id(0)
        c = pl.program_id(1)
        has_prev = c > 0
        row = lax.broadcasted_iota(jnp.int32, (t_rows, 1), 0)
        tpos = (c * t_rows + row + 1).astype(F32)

        def cur(k):
            return cur_ref[:, k * BRANCH_W:(k + 1) * BRANCH_W].astype(F32)

        def hal(k):
            return halo_ref[:, k * BRANCH_W:(k + 1) * BRANCH_W].astype(F32)

        def put_ext(slot, halo_val, cur_val):
            ext_ref[slot, 0:HALO, :] = jnp.where(has_prev, halo_val, 0.0)
            ext_ref[slot, HALO:HALO + t_rows, :] = cur_val
            ext_ref[slot, HALO + t_rows:HALO + t_rows + 8, :] = jnp.zeros((8, BRANCH_W), F32)

        def gated(branch, z_val):
            zb = z_val.astype(BF16)
            z_ref[:, branch * BRANCH_W:(branch + 1) * BRANCH_W] = zb
            gate_ref = g01_ref if branch < 2 else g23_ref
            gate = gate_ref[:, (branch % 2) * D_MODEL:(branch % 2 + 1) * D_MODEL].astype(F32)
            return _sigmoid(gate) * _dot(zb, wbr_ref[branch])

        px = cur(P_X)
        put_ext(0, hal(P_X), px)
        mixed = []
        window_sums = _trailing_window_sums(ext_ref, pool_ref, t_rows)
        for j, win in enumerate(POOL_WINDOWS):
            cols = slice(j * GROUP_W, (j + 1) * GROUP_W)
            pooled = window_sums[j] / jnp.minimum(tpos, float(win)) - px[:, cols]
            mixed.append(_dot(pooled.astype(BF16), pw_ref[j]))
        acc = gated(0, jnp.concatenate(mixed, axis=1) * ps_ref[...] * _silu(cur(P_GATE)))

        put_ext(1, hal(C_A) * _sigmoid(hal(C_B)), cur(C_A) * _sigmoid(cur(C_B)))
        cw = _lanes(cw_ref)
        conv = cb_ref[...] + _window_sum(ext_ref, 1, lambda k: cw[k:k + 1, :], CONV_K, HALO - (CONV_K - 1), t_rows)
        cacc_ref[...] = conv
        _, _, ln = _layer_norm_parts(conv, clg_ref[...], clb_ref[...])
        acc = acc + gated(1, _silu(ln) * _silu(cur(C_GATE)))

        _, _, v = _layer_norm_parts(cur(G_V), slg_ref[...], slb_ref[...])
        vb = v.astype(BF16)
        mask = _tril_mask()
        wt = [jnp.where(mask, sw_ref[g], 0.0).astype(BF16) for g in range(N_GROUPS)]
        sp_rows = []
        for sub in range(t_rows // CHUNK):
            rows = slice(sub * CHUNK, (sub + 1) * CHUNK)
            sp_rows.append(jnp.concatenate(
                [_dot(wt[g], vb[rows, g * GROUP_W:(g + 1) * GROUP_W]) for g in range(N_GROUPS)], axis=1)
                + sbias_ref[...])
        sp = jnp.concatenate(sp_rows, axis=0)
        acc = acc + gated(2, cur(G_U) * sp * _silu(cur(G_GATE)))

        put_ext(2, hal(S_C) * hal(S_X), cur(S_C) * cur(S_X))
        scw = _lanes(scw_ref)
        cv = _window_sum(ext_ref, 2, lambda k: scw[k:k + 1, :], SHORT_K, HALO - (SHORT_K - 1), t_rows)
        acc = acc + gated(3, cur(S_B) * cv * _silu(cur(S_GATE)))

        mg = acc.astype(BF16)
        mg_ref[...] = mg
        xv = x_ref[...] + _dot(mg, wo_ref[...])
        if not head:
            xn_ref[...] = xv
            return
        g_ref, t_ref = head_in
        loss_ref, dg_ref = outs[4:]

        @pl.when(jnp.logical_and(b == 0, c == 0))
        def _():
            loss_ref[...] = jnp.zeros_like(loss_ref)
            dg_ref[...] = jnp.zeros_like(dg_ref)

        g = g_ref[...]
        r = lax.rsqrt(_lanemean(xv * xv) + RMS_EPS)
        xh = xv * r
        err = xh * g - t_ref[...]
        loss_ref[...] += 0.5 * jnp.sum(_lanemean(err * err), axis=0, keepdims=True)
        dy = err * (1.0 / D_MODEL)
        dg_ref[...] += _rowsum(dy * xh)
        dxh = dy * g
        xn_ref[...] = r * (dxh - xh * _lanemean(dxh * xh))

    def rows_spec(cols, col_block=0):
        return pl.BlockSpec((None, t_rows, cols), lambda b, c: (b, c, col_block))

    def whole(shape):
        nd = len(shape)
        return pl.BlockSpec(shape, lambda b, c: (0,) * nd)

    in_specs = [
        rows_spec(PIECE_COLS),
        pl.BlockSpec((None, HALO, PIECE_COLS), lambda b, c: (b, jnp.maximum(c * hb - 1, 0), 0)),
        rows_spec(2 * D_MODEL, PIECE_COLS // (2 * D_MODEL)),
        rows_spec(2 * D_MODEL, PIECE_COLS // (2 * D_MODEL) + 1),
        rows_spec(D_MODEL),
    ] + _mixer_weight_specs() + [whole((N_BRANCH, BRANCH_W, D_MODEL)), whole((D_MODEL, D_MODEL))]
    args = [proj3, proj3, proj3, proj3, x3, *_mixer_weight_args(mw), wbr_l, wo_l]
    out_specs = [rows_spec(N_BRANCH * BRANCH_W), rows_spec(BRANCH_W), rows_spec(D_MODEL), rows_spec(D_MODEL)]
    out_shape = [jax.ShapeDtypeStruct((nb, seq, N_BRANCH * BRANCH_W), BF16), jax.ShapeDtypeStruct((nb, seq, BRANCH_W), F32),
                 jax.ShapeDtypeStruct((nb, seq, D_MODEL), BF16), jax.ShapeDtypeStruct((nb, seq, D_MODEL), F32)]
    if head:
        in_specs += [whole((1, D_MODEL)), rows_spec(D_MODEL)]
        args += list(head)
        out_specs += [whole((1, GROUP_W)), whole((1, D_MODEL))]
        out_shape += [jax.ShapeDtypeStruct((1, GROUP_W), F32), jax.ShapeDtypeStruct((1, D_MODEL), F32)]
    return _host_call(
        body,
        name=f"mix_merge_fwd_l{layer}",
        grid=(nb, nc),
        in_specs=in_specs,
        out_specs=out_specs,
        out_shape=out_shape,
        scratch_shapes=[pltpu.VMEM((3, HALO + t_rows + 8, BRANCH_W), F32),
                        pltpu.VMEM((t_rows + 32, BRANCH_W), F32)],
        args=args,
        ride=ride,
    )


MIXER_GRADS = ["pool_w", "pool_scale", "conv_w", "conv_b", "conv_ln_g", "conv_ln_b", "sgu_ln_g", "sgu_ln_b", "sgu_w",
               "sgu_b", "sc_w"]


def _mixers_bwd(proj3, dz3, cacc3, dproj3, mw, layer, ride=None):
    nb, seq, _ = proj3.shape
    t_rows = _tile(seq, 512)
    nc = seq // t_rows
    hb = t_rows // HALO

    def body(cur_ref, halo_ref, dz_ref, cacc_ref, pw_ref, ps_ref, cw_ref, cb_ref, clg_ref, clb_ref, slg_ref, slb_ref,
             sw_ref, sbias_ref, scw_ref, dp_in_ref, dp_ref, g_pw, g_ps, g_cw, g_cb, g_clg, g_clb, g_slg,
             g_slb, g_sw, g_sb, g_scw, ext_ref, nxt_ref, sb_acc, cw_acc, scw_acc, pad_ref, shift_ref,
             pool_ref):
        del dp_in_ref
        b = pl.program_id(0)
        r = pl.program_id(1)
        c = nc - 1 - r
        has_prev = c > 0
        row = lax.broadcasted_iota(jnp.int32, (t_rows, 1), 0)
        tpos = (c * t_rows + row + 1).astype(F32)

        @pl.when(jnp.logical_and(b == 0, r == 0))
        def _():
            for ref in (g_pw, g_ps, g_cb, g_clg, g_clb, g_slg, g_slb, g_sw, sb_acc, cw_acc, scw_acc):
                ref[...] = jnp.zeros_like(ref)

        @pl.when(r == 0)
        def _():
            nxt_ref[:, t_rows:t_rows + HALO, :] = jnp.zeros((3, HALO, BRANCH_W), F32)

        def cur(k):
            return cur_ref[:, k * BRANCH_W:(k + 1) * BRANCH_W].astype(F32)

        def hal(k):
            return halo_ref[:, k * BRANCH_W:(k + 1) * BRANCH_W].astype(F32)

        def dzp(k):
            return dz_ref[:, k * BRANCH_W:(k + 1) * BRANCH_W].astype(F32)

        def put_dp(k, val):
            dp_ref[:, k * BRANCH_W:(k + 1) * BRANCH_W] = val.astype(BF16)

        def put_ext(slot, halo_val, cur_val):
            ext_ref[slot, 0:HALO, :] = jnp.where(has_prev, halo_val, 0.0)
            ext_ref[slot, HALO:HALO + t_rows, :] = cur_val
            ext_ref[slot, HALO + t_rows:HALO + t_rows + 8, :] = jnp.zeros((8, BRANCH_W), F32)

        px = cur(P_X)
        put_ext(0, hal(P_X), px)
        pgate = cur(P_GATE)
        dz_pool = dzp(0)
        ps = ps_ref[...]
        pooled, mixed, cnts = [], [], []
        window_sums = _trailing_window_sums(ext_ref, pool_ref, t_rows)
        for j, win in enumerate(POOL_WINDOWS):
            cols = slice(j * GROUP_W, (j + 1) * GROUP_W)
            cnt = jnp.minimum(tpos, float(win))
            pj = window_sums[j] / cnt - px[:, cols]
            cnts.append(cnt)
            pooled.append(pj.astype(BF16))
            mixed.append(_dot(pooled[j], pw_ref[j]))
        mixed = jnp.concatenate(mixed, axis=1)
        pg_silu, pg_dsilu = _silu_pair(pgate)
        put_dp(P_GATE, dz_pool * (mixed * ps) * pg_dsilu)
        d_out = dz_pool * pg_silu
        g_ps[...] += _rowsum(d_out * mixed)
        d_mixed = (d_out * ps).astype(BF16)
        d_pooled = []
        for j in range(N_GROUPS):
            cols = slice(j * GROUP_W, (j + 1) * GROUP_W)
            g_pw[j] += _dot_tn(pooled[j], d_mixed[:, cols])
            dpj = _dot_nt(d_mixed[:, cols], pw_ref[j])
            d_pooled.append(dpj)
            nxt_ref[0, 0:t_rows, cols] = dpj / cnts[j]
        lead_sums = _leading_window_sums(nxt_ref, pool_ref, t_rows)
        put_dp(P_X, jnp.concatenate([lead_sums[j] - d_pooled[j] for j in range(N_GROUPS)], axis=1))

        c_a = cur(C_A)
        sig_b = _sigmoid(cur(C_B))
        put_ext(1, hal(C_A) * _sigmoid(hal(C_B)), c_a * sig_b)
        cw = _lanes(cw_ref)
        xh, rstd, ln = _layer_norm_parts(cacc_ref[...], clg_ref[...], clb_ref[...])
        cgate = cur(C_GATE)
        dz_conv = dzp(1)
        ln_silu, ln_dsilu = _silu_pair(ln)
        cg_silu, cg_dsilu = _silu_pair(cgate)
        put_dp(C_GATE, dz_conv * ln_silu * cg_dsilu)
        d_ln = dz_conv * cg_silu * ln_dsilu
        g_clg[...] += _rowsum(d_ln * xh)
        g_clb[...] += _rowsum(d_ln)
        dxh = d_ln * clg_ref[...]
        dc = rstd * (dxh - _lanemean(dxh) - xh * _lanemean(dxh * xh))
        g_cb[...] += _rowsum(dc)
        nxt_ref[1, 0:t_rows, :] = dc
        _tap_grads(cw_acc, pad_ref, shift_ref, dc, ext_ref, 1, CONV_K, HALO - (CONV_K - 1), t_rows)
        dyg = _window_sum(nxt_ref, 1, lambda i: cw[CONV_K - 1 - i:CONV_K - i, :], CONV_K, 0, t_rows)
        put_dp(C_A, dyg * sig_b)
        put_dp(C_B, dyg * c_a * sig_b * (1.0 - sig_b))

        u = cur(G_U)
        ggate = cur(G_GATE)
        vxh, vrstd, v = _layer_norm_parts(cur(G_V), slg_ref[...], slb_ref[...])
        vb = v.astype(BF16)
        mask = _tril_mask()
        wt = [jnp.where(mask, sw_ref[g], 0.0) for g in range(N_GROUPS)]
        wt_b = [w.astype(BF16) for w in wt]
        wtt_b = [w.T.astype(BF16) for w in wt]
        dz_sgu = dzp(2)
        gg_silu, gg_dsilu = _silu_pair(ggate)
        d_sgu = dz_sgu * gg_silu
        d_sp = d_sgu * u
        d_spb = d_sp.astype(BF16)
        sp_rows, dv_rows = [], []
        for sub in range(t_rows // CHUNK):
            rows = slice(sub * CHUNK, (sub + 1) * CHUNK)
            sp_g, dv_g = [], []
            for g in range(N_GROUPS):
                cols = slice(g * GROUP_W, (g + 1) * GROUP_W)
                sp_g.append(_dot(wt_b[g], vb[rows, cols]))
                g_sw[g] += jnp.where(mask, _dot_nt(d_spb[rows, cols], vb[rows, cols]), 0.0)
                dv_g.append(_dot(wtt_b[g], d_spb[rows, cols]))
            sp_rows.append(jnp.concatenate(sp_g, axis=1) + sbias_ref[...])
            dv_rows.append(jnp.concatenate(dv_g, axis=1))
            sb_acc[...] += d_sp[rows, :]
        sp = jnp.concatenate(sp_rows, axis=0)
        dv = jnp.concatenate(dv_rows, axis=0)
        put_dp(G_GATE, dz_sgu * (u * sp) * gg_dsilu)
        put_dp(G_U, d_sgu * sp)
        g_slg[...] += _rowsum(dv * vxh)
        g_slb[...] += _rowsum(dv)
        dvx = dv * slg_ref[...]
        put_dp(G_V, vrstd * (dvx - _lanemean(dvx) - vxh * _lanemean(dvx * vxh)))

        s_b, s_c, s_x, sgate = cur(S_B), cur(S_C), cur(S_X), cur(S_GATE)
        put_ext(2, hal(S_C) * hal(S_X), s_c * s_x)
        scw = _lanes(scw_ref)
        cv = _window_sum(ext_ref, 2, lambda k: scw[k:k + 1, :], SHORT_K, HALO - (SHORT_K - 1), t_rows)
        dz_sc = dzp(3)
        sg_silu, sg_dsilu = _silu_pair(sgate)
        put_dp(S_GATE, dz_sc * (s_b * cv) * sg_dsilu)
        d_pre = dz_sc * sg_silu
        put_dp(S_B, d_pre * cv)
        dcv = d_pre * s_b
        nxt_ref[2, 0:t_rows, :] = dcv
        _tap_grads(scw_acc, pad_ref, shift_ref, dcv, ext_ref, 2, SHORT_K, HALO - (SHORT_K - 1), t_rows)
        du = _window_sum(nxt_ref, 2, lambda i: scw[SHORT_K - 1 - i:SHORT_K - i, :], SHORT_K, 0, t_rows)
        put_dp(S_C, du * s_x)
        put_dp(S_X, du * s_c)

        nxt_ref[:, t_rows:t_rows + HALO, :] = nxt_ref[:, 0:HALO, :]

        @pl.when(jnp.logical_and(b == nb - 1, r == nc - 1))
        def _():
            lane = lax.broadcasted_iota(jnp.int32, (CHUNK, GROUP_W), 1)
            out = jnp.zeros((CHUNK, GROUP_W), F32)
            for g in range(N_GROUPS):
                col = jnp.sum(sb_acc[:, g * GROUP_W:(g + 1) * GROUP_W], axis=1, keepdims=True)
                out = jnp.where(lane == g, col, out)
            g_sb[...] = out
            g_cw[...] = jnp.sum(cw_acc[...], axis=1)
            g_scw[...] = jnp.sum(scw_acc[...], axis=1)

    def acc_spec(shape):
        nd = len(shape)
        return pl.BlockSpec(shape, lambda b, r: (0,) * nd)

    acc_shapes = [
        (N_GROUPS, GROUP_W, GROUP_W),
        (1, BRANCH_W),
        (CONV_ROWS, BRANCH_W),
        (1, BRANCH_W),
        (1, BRANCH_W),
        (1, BRANCH_W),
        (1, BRANCH_W),
        (1, BRANCH_W),
        (N_GROUPS, CHUNK, CHUNK),
        (CHUNK, GROUP_W),
        (SHORT_ROWS, BRANCH_W),
    ]
    outs, rode = _host_call(
        body,
        name=f"mixers_bwd_l{layer}",
        grid=(nb, nc),
        in_specs=[
            pl.BlockSpec((None, t_rows, PIECE_COLS), lambda b, r: (b, nc - 1 - r, 0)),
            pl.BlockSpec((None, HALO, PIECE_COLS), lambda b, r: (b, jnp.maximum((nc - 1 - r) * hb - 1, 0), 0)),
            pl.BlockSpec((None, t_rows, N_BRANCH * BRANCH_W), lambda b, r: (b, nc - 1 - r, 0)),
            pl.BlockSpec((None, t_rows, BRANCH_W), lambda b, r: (b, nc - 1 - r, 0)),
        ] + _mixer_weight_specs() + [ANY],
        out_specs=[pl.BlockSpec((None, t_rows, PIECE_COLS), lambda b, r: (b, nc - 1 - r, 0))]
        + [acc_spec(s) for s in acc_shapes],
        out_shape=[jax.ShapeDtypeStruct(dproj3.shape, BF16)] + [jax.ShapeDtypeStruct(s, F32) for s in acc_shapes],
        scratch_shapes=[
            pltpu.VMEM((3, HALO + t_rows + 8, BRANCH_W), F32),
            pltpu.VMEM((3, t_rows + HALO, BRANCH_W), F32),
            pltpu.VMEM((CHUNK, BRANCH_W), F32),
            pltpu.VMEM((CONV_ROWS, 8, BRANCH_W), F32),
            pltpu.VMEM((SHORT_ROWS, 8, BRANCH_W), F32),
            pltpu.VMEM((t_rows + 16, BRANCH_W), F32),
            pltpu.VMEM((t_rows + 8, BRANCH_W), F32),
            pltpu.VMEM((t_rows + 32, BRANCH_W), F32),
        ],
        aliases={4 + 11: 0},
        args=[proj3, proj3, dz3, cacc3, *_mixer_weight_args(mw), dproj3],
        ride=ride,
    )
    return outs[0], dict(zip(MIXER_GRADS, outs[1:])), rode


def _merge_bwd(dout2, merged2, z2, proj2, wbr_l, wo_l, layer, ride=None):
    n = dout2.shape[0]
    tm = _tile(n, 1024)
    nt = n // tm

    def body(do_ref, mg_ref, z_ref, gate_ref, wbr_ref, wo_ref, dz_ref, dg_ref, gwo_ref, gwbr_ref, dm_ref,
             awo_ref, awbr_ref):
        i = pl.program_id(0)
        nbr = pl.program_id(1)

        @pl.when(nbr == 0)
        def _():
            do_b = do_ref[...].astype(BF16)
            dm_ref[...] = _dot_nt(do_b, wo_ref[...])
            gw = _dot_tn(mg_ref[...], do_b)

            @pl.when(i == 0)
            def _():
                awo_ref[...] = gw

            @pl.when(i > 0)
            def _():
                awo_ref[...] += gw

        zt = z_ref[...]
        wbr = wbr_ref[nbr]
        bo = _dot(zt, wbr)
        gt = _sigmoid(gate_ref[...].astype(F32))
        dm = dm_ref[...]
        dbo = (dm * gt).astype(BF16)
        dg_ref[...] = (dm * bo * gt * (1.0 - gt)).astype(BF16)
        dz_ref[...] = _dot_nt(dbo, wbr).astype(BF16)
        gw = _dot_tn(zt, dbo)

        @pl.when(i == 0)
        def _():
            awbr_ref[nbr] = gw

        @pl.when(i > 0)
        def _():
            awbr_ref[nbr] += gw

        @pl.when(jnp.logical_and(i == nt - 1, nbr == N_BRANCH - 1))
        def _():
            pltpu.sync_copy(awo_ref, gwo_ref)
            pltpu.sync_copy(awbr_ref, gwbr_ref)

    return _host_call(
        body,
        name=f"merge_bwd_l{layer}",
        grid=(nt, N_BRANCH),
        in_specs=[
            pl.BlockSpec((tm, D_MODEL), lambda i, b: (i, 0)),
            pl.BlockSpec((tm, D_MODEL), lambda i, b: (i, 0)),
            pl.BlockSpec((tm, BRANCH_W), lambda i, b: (i, b)),
            pl.BlockSpec((tm, D_MODEL), lambda i, b: (i, GATE_BLOCK0 + b)),
            pl.BlockSpec((N_BRANCH, BRANCH_W, D_MODEL), lambda i, b: (0, 0, 0)),
            pl.BlockSpec((D_MODEL, D_MODEL), lambda i, b: (0, 0)),
        ],
        out_specs=[
            pl.BlockSpec((tm, BRANCH_W), lambda i, b: (i, b)),
            pl.BlockSpec((tm, D_MODEL), lambda i, b: (i, GATE_BLOCK0 + b)),
            ANY,
            ANY,
        ],
        out_shape=[
            jax.ShapeDtypeStruct((n, N_BRANCH * BRANCH_W), BF16),
            jax.ShapeDtypeStruct((n, IN_COLS), BF16),
            jax.ShapeDtypeStruct((D_MODEL, D_MODEL), F32),
            jax.ShapeDtypeStruct((N_BRANCH, BRANCH_W, D_MODEL), F32),
        ],
        scratch_shapes=[
            pltpu.VMEM((tm, D_MODEL), F32),
            pltpu.VMEM((D_MODEL, D_MODEL), F32),
            pltpu.VMEM((N_BRANCH, BRANCH_W, D_MODEL), F32),
        ],
        args=[dout2, merged2, z2, proj2, wbr_l, wo_l],
        ride=ride,
    )


def _inproj_bwd_x(dproj2, w_l, x2, g_row, dout2, layer, ride=None):
    n = x2.shape[0]
    tm = _tile(n, 1024)

    def body(dp_ref, w_ref, x_ref, g_ref, do_ref, dx_ref, dng_ref, dh_ref):
        i = pl.program_id(0)
        s = pl.program_id(1)
        part = _dot_nt(dp_ref[...], w_ref[...])

        @pl.when(s == 0)
        def _():
            dh_ref[...] = part

        @pl.when(s > 0)
        def _():
            dh_ref[...] += part

        @pl.when(jnp.logical_and(i == 0, s == 0))
        def _():
            dng_ref[...] = jnp.zeros_like(dng_ref)

        @pl.when(s == N_CHIPS - 1)
        def _():
            xv = x_ref[...]
            r = lax.rsqrt(_lanemean(xv * xv) + RMS_EPS)
            xh = xv * r
            dh = dh_ref[...]
            dng_ref[...] += _rowsum(dh * xh)
            dxh = dh * g_ref[...]
            dx_ref[...] = do_ref[...] + r * (dxh - xh * _lanemean(dxh * xh))

    return _host_call(
        body,
        name=f"inproj_bwd_x_l{layer}",
        grid=(n // tm, N_CHIPS),
        in_specs=[
            pl.BlockSpec((tm, SHARD_COLS), lambda i, s: (i, s)),
            pl.BlockSpec((None, D_MODEL, SHARD_COLS), lambda i, s: (s, 0, 0)),
            pl.BlockSpec((tm, D_MODEL), lambda i, s: (i, 0)),
            pl.BlockSpec((1, D_MODEL), lambda i, s: (0, 0)),
            pl.BlockSpec((tm, D_MODEL), lambda i, s: (i, 0)),
        ],
        out_specs=[
            pl.BlockSpec((tm, D_MODEL), lambda i, s: (i, 0)),
            pl.BlockSpec((1, D_MODEL), lambda i, s: (0, 0)),
        ],
        out_shape=[jax.ShapeDtypeStruct((n, D_MODEL), F32), jax.ShapeDtypeStruct((1, D_MODEL), F32)],
        scratch_shapes=[pltpu.VMEM((tm, D_MODEL), F32)],
        args=[dproj2, w_l, x2, g_row, dout2],
        ride=ride,
    )


def _inproj_bwd_w(h2, dproj2, layer, ride=None):
    n = h2.shape[0]
    tm = _tile(n, 2048)

    def body(h_ref, dp_ref, gw_ref):
        gw = _dot_tn(h_ref[...], dp_ref[...])

        @pl.when(pl.program_id(1) == 0)
        def _():
            gw_ref[...] = gw

        @pl.when(pl.program_id(1) > 0)
        def _():
            gw_ref[...] += gw

    outs, rode = _host_call(
        body,
        name=f"inproj_bwd_w_l{layer}",
        grid=(N_COL_TILES, n // tm),
        in_specs=[
            pl.BlockSpec((tm, D_MODEL), lambda s, i: (i, 0)),
            pl.BlockSpec((tm, COL_TILE), lambda s, i: (i, s)),
        ],
        out_specs=[pl.BlockSpec((None, D_MODEL, COL_TILE), lambda s, i: (s // 2, 0, s % 2))],
        out_shape=[jax.ShapeDtypeStruct((N_CHIPS, D_MODEL, SHARD_COLS), F32)],
        args=[h2, dproj2],
        ride=ride,
    )
    return outs[0], rode


def _adamw_math(w, g, m, v):
    m = ADAM_B1 * m + (1.0 - ADAM_B1) * g
    v = ADAM_B2 * v + (1.0 - ADAM_B2) * jnp.square(g)
    m_hat = m / (1.0 - ADAM_B1 ** ADAM_STEP)
    v_hat = v / (1.0 - ADAM_B2 ** ADAM_STEP)
    delta = -ADAM_LR * (m_hat / (jnp.sqrt(v_hat) + ADAM_EPS) + ADAM_WD * w)
    return delta, m, v


def _adamw_sharded(w, m, v, part, layer, prev, name, ride=None, after=None):
    assert prev is None or after is None
    _, rows, cols = w.shape
    tr = _tile(rows, max(16, (1 << 21) // (4 * cols) // 16 * 16), mult=16)

    def body(w_ref, m_ref, v_ref, p_ref, *rest):
        g_out, d_out, m_out, v_out = rest[-4:]
        g = ((p_ref[0].astype(F32) + p_ref[1].astype(F32)) + p_ref[2].astype(F32)) + p_ref[3].astype(F32)
        delta, m2, v2 = _adamw_math(w_ref[...], g, m_ref[...], v_ref[...])
        g_out[...] = g
        d_out[...] = delta
        m_out[...] = m2
        v_out[...] = v2

    spec = pl.BlockSpec((None, tr, cols), lambda i: (layer, i, 0))
    return _host_call(
        body,
        name=f"adamw_{name}_l{layer}",
        grid=(rows // tr,),
        in_specs=[spec, spec, spec, pl.BlockSpec((N_CHIPS, tr, cols), lambda i: (0, i, 0))]
        + ([ANY] * 4 if prev else []) + ([ANY] if after is not None else []),
        out_specs=[spec] * 4,
        out_shape=[jax.ShapeDtypeStruct(w.shape, F32)] * 4,
        args=[w, m, v, part] + (list(prev) if prev else []) + ([after] if after is not None else []),
        aliases={4 + k: k for k in range(4)} if prev else {},
        ride=ride,
    )


def _adamw_packed(w, m, v, g):
    rows = w.shape[0]
    tr = _tile(rows, rows // 2 if rows % 16 == 0 else rows)

    def body(w_ref, m_ref, v_ref, g_ref, d_out, m_out, v_out):
        delta, m2, v2 = _adamw_math(w_ref[...], g_ref[...], m_ref[...], v_ref[...])
        d_out[...] = delta
        m_out[...] = m2
        v_out[...] = v2

    spec = pl.BlockSpec((tr, GROUP_W), lambda i: (i, 0))
    return pl.pallas_call(
        body,
        name="adamw_small",
        grid=(rows // tr,),
        in_specs=[spec] * 4,
        out_specs=[spec] * 3,
        out_shape=[jax.ShapeDtypeStruct(w.shape, F32)] * 3,
        compiler_params=_params("arbitrary"),
    )(w, m, v, g)


SMALL = ["norm_g", "pool_w", "pool_scale", "conv_b", "conv_ln_g", "conv_ln_b", "sgu_ln_g", "sgu_ln_b", "sgu_w",
         "sgu_b", "final_g"]
WEIGHTS = ["norm_g", "w_in", "pool_w", "pool_scale", "conv_w", "conv_b", "conv_ln_g", "conv_ln_b", "sgu_ln_g",
           "sgu_ln_b", "sgu_w", "sgu_b", "sc_w", "w_branch", "w_o", "final_g"]
HALF_SHAPES = [(D_MODEL // 2, SHARD_COLS), (N_BRANCH * BRANCH_W // 2, BR_SHARD), (BR_SHARD // 2, D_MODEL),
               (CS_ROWS // 2, GROUP_W)]


def _pack_small(tree, last_rows=None):
    tail = jnp.zeros((8, GROUP_W), F32) if last_rows is None else last_rows
    return jnp.concatenate([tree[k].reshape(-1, GROUP_W) for k in SMALL] + [tail], axis=0)


def _unpack_small(packed, like):
    out, r = {}, 0
    for k in SMALL:
        nr = like[k].size // GROUP_W
        out[k] = packed[r:r + nr].reshape(like[k].shape)
        r += nr
    return out


def _pad_rows(a, rows):
    pad = [(0, 0)] * a.ndim
    pad[-2] = (0, rows - a.shape[-2])
    return jnp.pad(a, pad)


def _pack_cs(conv, short):
    return jnp.concatenate([_pad_rows(conv, CS_ROWS // 2), _pad_rows(short, CS_ROWS // 2)], axis=-2)


def _shard_major_rows(a):
    return a.reshape(a.shape[0], N_CHIPS, GROUP_W).transpose(1, 0, 2)


def kernel(x, norm_g, w_in, pool_w, pool_scale, conv_w, conv_b, conv_ln_g, conv_ln_b, sgu_ln_g, sgu_ln_b, sgu_w, sgu_b, sc_w, w_branch, w_o, final_g, loss_target, m_norm_g, m_w_in, m_pool_w, m_pool_scale, m_conv_w, m_conv_b, m_conv_ln_g, m_conv_ln_b, m_sgu_ln_g, m_sgu_ln_b, m_sgu_w, m_sgu_b, m_sc_w, m_w_branch, m_w_o, m_final_g, v_norm_g, v_w_in, v_pool_w, v_pool_scale, v_conv_w, v_conv_b, v_conv_ln_g, v_conv_ln_b, v_sgu_ln_g, v_sgu_ln_b, v_sgu_w, v_sgu_b, v_sc_w, v_w_branch, v_w_o, v_final_g):
    w = dict(norm_g=norm_g, w_in=w_in, pool_w=pool_w, pool_scale=pool_scale, conv_w=conv_w, conv_b=conv_b,
             conv_ln_g=conv_ln_g, conv_ln_b=conv_ln_b, sgu_ln_g=sgu_ln_g, sgu_ln_b=sgu_ln_b, sgu_w=sgu_w,
             sgu_b=sgu_b, sc_w=sc_w, w_branch=w_branch, w_o=w_o, final_g=final_g)
    mom = dict(norm_g=m_norm_g, w_in=m_w_in, pool_w=m_pool_w, pool_scale=m_pool_scale, conv_w=m_conv_w,
               conv_b=m_conv_b, conv_ln_g=m_conv_ln_g, conv_ln_b=m_conv_ln_b, sgu_ln_g=m_sgu_ln_g,
               sgu_ln_b=m_sgu_ln_b, sgu_w=m_sgu_w, sgu_b=m_sgu_b, sc_w=m_sc_w, w_branch=m_w_branch, w_o=m_w_o,
               final_g=m_final_g)
    var = dict(norm_g=v_norm_g, w_in=v_w_in, pool_w=v_pool_w, pool_scale=v_pool_scale, conv_w=v_conv_w,
               conv_b=v_conv_b, conv_ln_g=v_conv_ln_g, conv_ln_b=v_conv_ln_b, sgu_ln_g=v_sgu_ln_g,
               sgu_ln_b=v_sgu_ln_b, sgu_w=v_sgu_w, sgu_b=v_sgu_b, sc_w=v_sc_w, w_branch=v_w_branch, w_o=v_w_o,
               final_g=v_final_g)

    nb, seq, _ = x.shape
    n = nb * seq
    core = lax.axis_index("c").astype(jnp.int32).reshape(1)

    win_s = w_in.astype(BF16).reshape(DEPTH, 2, D_MODEL // 2, SHARD_COLS)
    wbr_s = w_branch.astype(BF16)
    wo_s = w_o.astype(BF16)
    cs_s = _pack_cs(conv_w, sc_w)

    (win_f0,) = _allgather_layer0([win_s])

    def layer_weights(win_f, wbr_f, wo_f, cs_f):
        cs_f = cs_f.reshape(N_CHIPS, CS_ROWS, GROUP_W)
        return dict(win=win_f.reshape(N_CHIPS, D_MODEL, SHARD_COLS),
                    wbr=wbr_f.reshape(N_CHIPS, N_BRANCH, BRANCH_W, BR_SHARD).transpose(1, 2, 0, 3).reshape(
                        N_BRANCH, BRANCH_W, D_MODEL),
                    wo=wo_f.reshape(D_MODEL, D_MODEL),
                    conv_w=cs_f[:, :CONV_ROWS], sc_w=cs_f[:, CS_ROWS // 2:CS_ROWS // 2 + SHORT_ROWS])

    def mixer_weights(l, gathered):
        row = lambda a: a[l].reshape(1, BRANCH_W)
        bias = jnp.repeat(jnp.swapaxes(sgu_b[l], 0, 1), GROUP_W, axis=1)
        return dict(pool_w=pool_w[l].astype(BF16), pool_scale=row(pool_scale), conv_w=gathered["conv_w"],
                    conv_b=row(conv_b), conv_ln_g=row(conv_ln_g), conv_ln_b=row(conv_ln_b), sgu_ln_g=row(sgu_ln_g),
                    sgu_ln_b=row(sgu_ln_b), sgu_w=sgu_w[l], sgu_bias=bias, sc_w=gathered["sc_w"])

    lw, mw = [None, None], [None, None]

    xs, projs, hs, zs, mgs, caccs = [x.reshape(n, D_MODEL)], [], [], [], [], []
    for l in range(DEPTH):
        first = l == 0
        win_l = win_f0.reshape(N_CHIPS, D_MODEL, SHARD_COLS) if first else lw[1]["win"]
        ride = _join(_gather_whole([wbr_s, wo_s, cs_s], 0), _gather_layer1_halves(win_s)) if first else None
        (proj, h), rode = _inproj_fwd(xs[l], norm_g[l:l + 1], win_l, l, ride=ride)
        if first:
            lw[0] = layer_weights(win_f0, *rode[:3])
            mw[0] = mixer_weights(0, lw[0])
            win_f1 = rode[3]
        proj3, x3 = proj.reshape(nb, seq, IN_COLS), xs[l].reshape(nb, seq, D_MODEL)
        if first:
            ride = _join(_forward_layer1_halves(win_f1), _gather_whole([wbr_s, wo_s, cs_s], 1))
            (z3, cacc, mg3, xn3), rode = _mix_merge_fwd(proj3, x3, mw[0], lw[0]["wbr"], lw[0]["wo"], 0, ride=ride)
            lw[1] = layer_weights(*rode)
            mw[1] = mixer_weights(1, lw[1])
            xs.append(xn3.reshape(n, D_MODEL))
        else:
            (z3, cacc, mg3, dx3, loss_part, g_final), _ = _mix_merge_fwd(
                proj3, x3, mw[l], lw[l]["wbr"], lw[l]["wo"], l,
                head=(final_g.reshape(1, D_MODEL), loss_target))
            dx = dx3.reshape(n, D_MODEL)
        caccs.append(cacc)
        projs.append(proj)
        hs.append(h)
        zs.append(z3.reshape(n, N_BRANCH * BRANCH_W))
        mgs.append(mg3.reshape(n, D_MODEL))

    def chip_major(gwbr):
        return gwbr.reshape(N_BRANCH, BRANCH_W, N_CHIPS, BR_SHARD).transpose(2, 0, 1, 3)

    def as_halves(arrays, first_index=0):
        return [g.reshape((N_CHIPS, 2) + s) for g, s in zip(arrays, HALF_SHAPES[first_index:])]

    def add_halves(grads, others, l, first_index=0):
        return [_add_halves(g, o, core, f"add_halves_l{l}_{first_index + i}")
                for i, (g, o) in enumerate(zip(grads, others))]

    per_layer, parts = {}, [None] * DEPTH
    (dz, dproj, gwo, gwbr), _ = _merge_bwd(dx, mgs[1], zs[1], projs[1], lw[1]["wbr"], lw[1]["wo"], 1)
    dproj3, gm, _ = _mixers_bwd(projs[1].reshape(nb, seq, IN_COLS), dz.reshape(nb, seq, N_BRANCH * BRANCH_W), caccs[1],
                                dproj.reshape(nb, seq, IN_COLS), mw[1], 1)
    dproj = dproj3.reshape(n, IN_COLS)
    gwin, _ = _inproj_bwd_w(hs[1], dproj, 1)
    gcs = _pack_cs(_shard_major_rows(gm["conv_w"]), _shard_major_rows(gm["sc_w"]))
    grads1 = as_halves([gwin, chip_major(gwbr), gwo, gcs])
    (dx, g_norm), others1 = _inproj_bwd_x(dproj, lw[1]["win"], xs[1], norm_g[1:2], dx, 1, ride=_swap_halves(grads1))
    gm["norm_g"] = g_norm
    per_layer[1] = gm
    sums1 = add_halves(grads1, others1, 1)
    (dz, dproj, gwo, gwbr), scattered1 = _merge_bwd(dx, mgs[0], zs[0], projs[0], lw[0]["wbr"], lw[0]["wo"], 0,
                                                    ride=_scatter_chip_sums(sums1))
    dproj3, gm, parts[1] = _mixers_bwd(projs[0].reshape(nb, seq, IN_COLS), dz.reshape(nb, seq, N_BRANCH * BRANCH_W),
                                       caccs[0], dproj.reshape(nb, seq, IN_COLS), mw[0], 0,
                                       ride=_share_halves(scattered1))
    dproj = dproj3.reshape(n, IN_COLS)
    gcs = _pack_cs(_shard_major_rows(gm["conv_w"]), _shard_major_rows(gm["sc_w"]))
    small0 = as_halves([chip_major(gwbr), gwo, gcs], first_index=1)

    per_layer[0] = dict(gm, norm_g=jnp.zeros((1, D_MODEL), F32))

    def stack(k, shape):
        return jnp.stack([per_layer[l][k] for l in range(DEPTH)]).reshape(shape)

    g_small = {"final_g": g_final.reshape(D_MODEL), "norm_g": stack("norm_g", (DEPTH, D_MODEL)),
               "pool_w": stack("pool_w", pool_w.shape), "sgu_w": stack("sgu_w", sgu_w.shape),
               "sgu_b": jnp.swapaxes(stack("sgu_b", (DEPTH, CHUNK, GROUP_W))[:, :, :N_GROUPS], 1, 2)}
    for k in ("pool_scale", "conv_b", "conv_ln_g", "conv_ln_b", "sgu_ln_g", "sgu_ln_b"):
        g_small[k] = stack(k, (DEPTH, BRANCH_W))
    small_part = _pack_small(g_small, jnp.broadcast_to(loss_part, (8, GROUP_W)))

    gwin, rode = _inproj_bwd_w(hs[0], dproj, 0, ride=_join(_swap_halves(small0), _swap_whole(small_part)))
    others_small0, small_other = rode[:3], rode[3]
    small_chip = _add2(small_part, small_other, "add_small_sibling")
    (gwin_h,) = as_halves([gwin])

    def flat(a, i):
        rows, cols = 2 * HALF_SHAPES[i][0], HALF_SHAPES[i][1]
        return a.reshape(a.shape[0], rows, cols)

    names4 = ["w_in", "w_branch", "w_o", "conv_sc"]

    def packed4(t):
        return [t["w_in"], t["w_branch"], t["w_o"], _pack_cs(t["conv_w"], t["sc_w"])]

    w4, m4, v4 = ([flat(a, i) for i, a in enumerate(packed4(t))] for t in (w, mom, var))

    def sharded_update(i, layer, prev, after=None):
        return _adamw_sharded(w4[i], m4[i], v4[i], flat(parts[layer][i], i), layer, prev, names4[i], after=after)

    send_sem, recv_sem, gwin_thru, land_thru, token = _swap_halves_start(gwin_h, "swap_halves_l0_w_in")
    updated = [sharded_update(i, 1, None, after=token if i == 0 else None)[0] for i in range(4)]
    gwin_h, other_win0 = _swap_halves_wait(send_sem, recv_sem, gwin_thru, land_thru, updated[0][0],
                                           "swap_halves_l0_w_in")
    sums0 = add_halves([gwin_h], [other_win0], 0) + add_halves(small0, others_small0, 0, first_index=1)
    (dx, g_norm0), rode = _inproj_bwd_x(dproj, lw[0]["win"], xs[0], norm_g[0:1], dx, 0,
                                        ride=_join(_scatter_chip_sums(sums0), _spread_chip_sums(small_chip)))
    scattered0, small_chips = rode[:4], rode[4]
    grad_x = dx.reshape(nb, seq, D_MODEL)

    g_norm0, parts[0] = _allreduce_small(g_norm0.reshape(D_MODEL // GROUP_W, GROUP_W), _share_halves(scattered0))
    g_packed = lax.dynamic_update_slice(_sum_chips(small_chips, "sum_small_chips"), g_norm0, (0, 0))
    loss = g_packed[g_packed.shape[0] - 8, 0]
    d_packed, m_packed, v_packed = _adamw_packed(_pack_small(w), _pack_small(mom), _pack_small(var), g_packed)
    grads, deltas, new_m, new_v = {}, {}, {}, {}
    for tree, packed in ((grads, g_packed), (deltas, d_packed), (new_m, m_packed), (new_v, v_packed)):
        tree.update(_unpack_small(packed, w))

    for i, name in enumerate(names4):
        res, _ = sharded_update(i, 0, updated[i])
        for tree, r in zip((grads, deltas, new_m, new_v), res):
            if name == "conv_sc":
                tree["conv_w"] = r[:, :CONV_K]
                tree["sc_w"] = r[:, CS_ROWS // 2:CS_ROWS // 2 + SHORT_K]
            else:
                tree[name] = r.reshape(w[name].shape)

    return (loss, grad_x, *[grads[k] for k in WEIGHTS], *[deltas[k] for k in WEIGHTS],
            *[new_m[k] for k in WEIGHTS], *[new_v[k] for k in WEIGHTS])
```

```python
import functools

import jax
import jax.numpy as jnp
from jax import lax
from jax.experimental import pallas as pl
from jax.experimental.pallas import tpu as pltpu

F32 = jnp.float32
BF16 = jnp.bfloat16

D_MODEL = 1024
DEPTH = 2
N_BRANCH = 4
BRANCH_W = 512
N_GROUPS = 4
GROUP_W = 128
POOL_WINDOWS = (2, 4, 8, 16)
CONV_K = 31
SHORT_K = 3
CHUNK = 128
N_PIECES = 12
PIECE_COLS = N_PIECES * BRANCH_W
IN_COLS = PIECE_COLS + N_BRANCH * D_MODEL
N_CHIPS = 4
SHARD_COLS = IN_COLS // N_CHIPS
BR_SHARD = D_MODEL // N_CHIPS
COL_TILE = SHARD_COLS // 2
N_COL_TILES = IN_COLS // COL_TILE
GATE_BLOCK0 = PIECE_COLS // D_MODEL
RMS_EPS = 1e-6
LN_EPS = 1e-5
HALO = 32
CONV_ROWS = 32
SHORT_ROWS = 8
CS_ROWS = 64

ADAM_LR = 0.001
ADAM_B1 = 0.9
ADAM_B2 = 0.999
ADAM_EPS = 1e-08
ADAM_WD = 0.01
ADAM_STEP = 10

VMEM_LIMIT = 60 * 1024 * 1024
MESH_ID = pl.DeviceIdType.MESH
ANY = pl.BlockSpec(memory_space=pl.ANY)
VMEM_WHOLE = pl.BlockSpec(memory_space=pltpu.VMEM)

(P_X, P_GATE, C_A, C_B, C_GATE, G_U, G_V, G_GATE, S_B, S_C, S_X, S_GATE) = range(N_PIECES)


def _params(*sem):
    return pltpu.CompilerParams(dimension_semantics=sem, vmem_limit_bytes=VMEM_LIMIT)


def _sigmoid(v):
    return 0.5 * jnp.tanh(0.5 * v) + 0.5


def _silu(v):
    return v * _sigmoid(v)


def _silu_pair(v):
    s = _sigmoid(v)
    return v * s, s * (1.0 + v * (1.0 - s))


def _dot(a, b):
    return jnp.dot(a, b, preferred_element_type=F32)


def _dot_nt(a, b):
    return lax.dot_general(a, b, (((1,), (1,)), ((), ())), preferred_element_type=F32)


def _dot_tn(a, b):
    return lax.dot_general(a, b, (((0,), (0,)), ((), ())), preferred_element_type=F32)


def _rowsum(v):
    return jnp.sum(v, axis=0, keepdims=True)


def _lanemean(v):
    return jnp.mean(v, axis=-1, keepdims=True)


def _tile(n, want, mult=8):
    t = max(1, min(n, want))
    while n % t or (t % mult and t != n):
        t -= 1
    return t


def _place():
    x, y, c = lax.axis_index("x"), lax.axis_index("y"), lax.axis_index("c")
    chip = 2 * x + y
    peers = [(1 - x, y), (x, 1 - y), (1 - x, 1 - y)]
    return x, y, c, chip, peers


def _remote(src, dst, send_sem, recv_sem, dev):
    return pltpu.make_async_remote_copy(src_ref=src, dst_ref=dst, send_sem=send_sem, recv_sem=recv_sem,
                                        device_id=dev, device_id_type=MESH_ID)


class _Exchange:
    def __init__(self, inputs, out_shapes, plan, n_remote, n_local=0, aliases=None):
        self.inputs = list(inputs)
        self.out_shapes = list(out_shapes)
        self.plan = plan
        self.n_remote = n_remote
        self.n_local = n_local
        self.aliases = dict(aliases or {})

    def copies(self, in_refs, out_refs, send_sems, recv_sems, loc_sems):
        remote, local = self.plan(in_refs, out_refs)
        assert len(remote) == self.n_remote and len(local) == self.n_local
        cps = [_remote(s, d, send_sems.at[t], recv_sems.at[t], dev) for t, (s, d, dev) in enumerate(remote)]
        cps += [pltpu.make_async_copy(s, d, loc_sems.at[t]) for t, (s, d) in enumerate(local)]
        return cps

    def sem_shapes(self):
        return [pltpu.SemaphoreType.DMA((max(self.n_remote, 1),)), pltpu.SemaphoreType.DMA((max(self.n_remote, 1),)),
                pltpu.SemaphoreType.DMA((max(self.n_local, 1),))]


def _exchange_call(name, ex):
    n_ci = len(ex.inputs)

    def body(*refs):
        cins, couts = refs[:n_ci], refs[n_ci:n_ci + len(ex.out_shapes)]
        cps = ex.copies(cins, couts, *refs[n_ci + len(ex.out_shapes):])
        for cp in cps:
            cp.start()
        for cp in cps:
            cp.wait()

    return pl.pallas_call(
        body,
        name=name,
        in_specs=[ANY] * n_ci,
        out_specs=[ANY] * len(ex.out_shapes),
        out_shape=ex.out_shapes,
        scratch_shapes=ex.sem_shapes(),
        input_output_aliases=ex.aliases,
    )(*ex.inputs)


def _host_call(body, *, name, grid, in_specs, out_specs, out_shape, args, scratch_shapes=(), aliases=None, ride=None):
    n_in, n_out, n_scr = len(in_specs), len(out_specs), len(scratch_shapes)
    sem = ("arbitrary",) * len(grid)
    aliases = dict(aliases or {})
    if ride is None:
        outs = pl.pallas_call(body, name=name, grid=grid, in_specs=list(in_specs), out_specs=list(out_specs),
                              out_shape=list(out_shape), scratch_shapes=list(scratch_shapes),
                              input_output_aliases=aliases, compiler_params=_params(*sem))(*args)
        return list(outs), []
    n_ci, n_co = len(ride.inputs), len(ride.out_shapes)

    def full_body(*refs):
        ins, cins = refs[:n_in], refs[n_in:n_in + n_ci]
        o0 = n_in + n_ci
        outs, couts = refs[o0:o0 + n_out], refs[o0 + n_out:o0 + n_out + n_co]
        s0 = o0 + n_out + n_co
        scr, sems = refs[s0:s0 + n_scr], refs[s0 + n_scr:]
        first = functools.reduce(jnp.logical_and, [pl.program_id(d) == 0 for d in range(len(grid))])
        last = functools.reduce(jnp.logical_and, [pl.program_id(d) == grid[d] - 1 for d in range(len(grid))])

        @pl.when(first)
        def _():
            for cp in ride.copies(cins, couts, *sems):
                cp.start()

        body(*ins, *outs, *scr)

        @pl.when(last)
        def _():
            for cp in ride.copies(cins, couts, *sems):
                cp.wait()

    for ci, co in ride.aliases.items():
        aliases[n_in + ci] = n_out + co
    outs = pl.pallas_call(
        full_body, name=name, grid=grid, in_specs=list(in_specs) + [ANY] * n_ci,
        out_specs=list(out_specs) + [ANY] * n_co, out_shape=list(out_shape) + ride.out_shapes,
        scratch_shapes=list(scratch_shapes) + ride.sem_shapes(), input_output_aliases=aliases,
        compiler_params=_params(*sem))(*args, *ride.inputs)
    return list(outs[:n_out]), list(outs[n_out:])


def _allgather_layer0(shards):
    na = len(shards)
    nq = shards[0].shape[2]
    units = [(k, a, q) for q in range(nq) for a in range(na) for k in range(3)]

    def body(*refs):
        ins, outs = refs[:na], refs[na:2 * na]
        send_sems, recv_sems, fsend_sems, frecv_sems, loc_sems = refs[2 * na:]
        x, y, c, chip, peers = _place()
        sib = (x, y, 1 - c)
        locs = [pltpu.make_async_copy(ins[a].at[0], outs[a].at[chip], loc_sems.at[a]) for a in range(na)]
        for cp in locs:
            cp.start()
        pending = []
        for t, (k, a, q) in enumerate(units):
            px, py = peers[k]
            cp = _remote(ins[a].at[0, c, q], outs[a].at[chip, c, q], send_sems.at[t], recv_sems.at[t], (px, py, c))
            cp.start()
            pending.append(cp)
        for t, (k, a, q) in enumerate(units):
            px, py = peers[k]
            slab = outs[a].at[2 * px + py, c, q]
            _remote(slab, slab, send_sems.at[t], recv_sems.at[t], (px, py, c)).wait_recv()
            cp = _remote(slab, slab, fsend_sems.at[t], frecv_sems.at[t], sib)
            cp.start()
            pending.append(cp)
        for t, (k, a, q) in enumerate(units):
            px, py = peers[k]
            slab = outs[a].at[2 * px + py, 1 - c, q]
            _remote(slab, slab, fsend_sems.at[t], frecv_sems.at[t], sib).wait_recv()
        for cp in pending:
            cp.wait_send()
        for cp in locs:
            cp.wait()

    return pl.pallas_call(
        body,
        name="allgather_layer0",
        in_specs=[ANY] * na,
        out_specs=[ANY] * na,
        out_shape=[jax.ShapeDtypeStruct((N_CHIPS,) + a.shape[1:], a.dtype) for a in shards],
        scratch_shapes=[pltpu.SemaphoreType.DMA((len(units),))] * 4 + [pltpu.SemaphoreType.DMA((na,))],
    )(*shards)


def _gather_layer1_halves(win_s):
    def plan(cins, couts):
        _, _, c, chip, peers = _place()
        (src,), (dst,) = cins, couts
        return ([(src.at[1, c], dst.at[chip, c], (px, py, c)) for px, py in peers], [(src.at[1], dst.at[chip])])

    return _Exchange([win_s], [jax.ShapeDtypeStruct((N_CHIPS,) + win_s.shape[1:], win_s.dtype)], plan, 3, 1)


def _forward_layer1_halves(win_f1):
    def plan(cins, couts):
        x, y, c, _, peers = _place()
        (dst,) = couts
        return [(dst.at[2 * px + py, c], dst.at[2 * px + py, c], (x, y, 1 - c)) for px, py in peers], []

    return _Exchange([win_f1], [jax.ShapeDtypeStruct(win_f1.shape, win_f1.dtype)], plan, 3, 0, aliases={0: 0})


def _gather_whole(shards, layer):
    na = len(shards)

    def plan(cins, couts):
        _, _, c, chip, peers = _place()
        remote = [(cins[a].at[layer], couts[a].at[chip], (px, py, c)) for a in range(na) for px, py in peers]
        return remote, [(cins[a].at[layer], couts[a].at[chip]) for a in range(na)]

    return _Exchange(shards, [jax.ShapeDtypeStruct((N_CHIPS,) + a.shape[1:], a.dtype) for a in shards], plan,
                     3 * na, na)


def _join(a, b):
    n_in, n_out = len(a.inputs), len(a.out_shapes)

    def plan(cins, couts):
        remote_a, local_a = a.plan(cins[:n_in], couts[:n_out])
        remote_b, local_b = b.plan(cins[n_in:], couts[n_out:])
        return remote_a + remote_b, local_a + local_b

    aliases = dict(a.aliases)
    aliases.update({n_in + i: n_out + o for i, o in b.aliases.items()})
    return _Exchange(a.inputs + b.inputs, a.out_shapes + b.out_shapes, plan, a.n_remote + b.n_remote,
                     a.n_local + b.n_local, aliases)


def _swap_halves(grads):
    def plan(cins, couts):
        x, y, c, _, _ = _place()
        return [(g.at[:, 1 - c], r, (x, y, 1 - c)) for g, r in zip(cins, couts)], []

    return _Exchange(grads, [jax.ShapeDtypeStruct(g.shape[:1] + g.shape[2:], g.dtype) for g in grads], plan,
                     len(grads))


def _scatter_chip_sums(sums):
    na = len(sums)

    def plan(cins, couts):
        _, _, c, chip, peers = _place()
        remote = [(cins[a].at[2 * px + py], couts[a].at[chip, c], (px, py, c)) for a in range(na) for px, py in peers]
        return remote, [(cins[a].at[chip], couts[a].at[chip, c]) for a in range(na)]

    return _Exchange(sums, [jax.ShapeDtypeStruct((N_CHIPS, 2) + s.shape[1:], s.dtype) for s in sums], plan,
                     3 * na, na)


def _share_halves(parts):
    na = len(parts)

    def plan(cins, couts):
        x, y, c, _, _ = _place()
        return [(p.at[:, c], p.at[:, c], (x, y, 1 - c)) for p in couts], []

    return _Exchange(parts, [jax.ShapeDtypeStruct(p.shape, p.dtype) for p in parts], plan, na, 0,
                     aliases={a: a for a in range(na)})


def _allreduce_small(packed, ride):
    rows = packed.shape[0]
    n_ci, n_co = len(ride.inputs), len(ride.out_shapes)

    def body(*refs):
        sg_ref, cins = refs[0], refs[1:1 + n_ci]
        res_ref, couts = refs[1 + n_ci], refs[2 + n_ci:2 + n_ci + n_co]
        sib_ref, cs_ref, rem_ref, s1_send, s1_recv, s2_send, s2_recv = refs[2 + n_ci + n_co:9 + n_ci + n_co]
        riding = ride.copies(cins, couts, *refs[9 + n_ci + n_co:])
        for cp in riding:
            cp.start()
        x, y, c, chip, peers = _place()
        cp = _remote(sg_ref, sib_ref, s1_send, s1_recv, (x, y, 1 - c))
        cp.start()
        cp.wait()
        cs_ref[...] = sg_ref[...] + sib_ref[...]
        rem_ref[chip] = cs_ref[...]
        cps = [_remote(cs_ref, rem_ref.at[chip], s2_send.at[k], s2_recv.at[chip], (px, py, c))
               for k, (px, py) in enumerate(peers)]
        for cp in cps:
            cp.start()
        for k, (px, py) in enumerate(peers):
            pchip = 2 * px + py
            _remote(cs_ref, rem_ref.at[pchip], s2_send.at[k], s2_recv.at[pchip], (px, py, c)).wait_recv()
        for cp in cps:
            cp.wait_send()
        res_ref[...] = ((rem_ref[0] + rem_ref[1]) + rem_ref[2]) + rem_ref[3]
        for cp in riding:
            cp.wait()

    outs = pl.pallas_call(
        body,
        name="allreduce_small",
        in_specs=[VMEM_WHOLE] + [ANY] * n_ci,
        out_specs=[VMEM_WHOLE] + [ANY] * n_co,
        out_shape=[jax.ShapeDtypeStruct((rows, GROUP_W), F32)] + ride.out_shapes,
        scratch_shapes=[
            pltpu.VMEM((rows, GROUP_W), F32),
            pltpu.VMEM((rows, GROUP_W), F32),
            pltpu.VMEM((N_CHIPS, rows, GROUP_W), F32),
            pltpu.SemaphoreType.DMA,
            pltpu.SemaphoreType.DMA,
            pltpu.SemaphoreType.DMA((3,)),
            pltpu.SemaphoreType.DMA((N_CHIPS,)),
        ] + ride.sem_shapes(),
        input_output_aliases={1 + i: 1 + o for i, o in ride.aliases.items()},
        compiler_params=pltpu.CompilerParams(vmem_limit_bytes=VMEM_LIMIT),
    )(packed, *ride.inputs)
    return outs[0], list(outs[1:])


HBM_SPEC = pl.BlockSpec(memory_space=pltpu.HBM)
SEM_SPEC = pl.BlockSpec(memory_space=pltpu.SEMAPHORE)
SIDE_EFFECT = pltpu.SideEffectType.DATAFLOW_SIDE_EFFECTING


def _swap_halves_start(grad, name):
    land_shape = grad.shape[:1] + grad.shape[2:]

    def body(g_ref, land_ref, send_sem, recv_sem, g_thru, land_thru, token):
        del g_thru, land_thru
        x, y, c, _, _ = _place()
        _remote(g_ref.at[:, 1 - c], land_ref, send_sem, recv_sem, (x, y, 1 - c)).start()
        token[...] = jnp.zeros_like(token)

    return pl.pallas_call(
        body,
        name=name + "_start",
        out_shape=(pltpu.SemaphoreType.DMA(()), pltpu.SemaphoreType.DMA(()), pltpu.HBM(grad.shape, grad.dtype),
                   pltpu.HBM(land_shape, grad.dtype), jax.ShapeDtypeStruct((8, GROUP_W), F32)),
        in_specs=(HBM_SPEC, HBM_SPEC),
        out_specs=(SEM_SPEC, SEM_SPEC, HBM_SPEC, HBM_SPEC, VMEM_WHOLE),
        input_output_aliases={0: 2, 1: 3},
        compiler_params=pltpu.CompilerParams(has_side_effects=SIDE_EFFECT),
    )(pltpu.with_memory_space_constraint(grad, pltpu.HBM),
      pltpu.with_memory_space_constraint(lax.empty(land_shape, grad.dtype), pltpu.HBM))


def _swap_halves_wait(send_sem, recv_sem, g_thru, land_thru, after, name):
    def body(g_ref, land_ref, send_sem, recv_sem, after_ref, g_out, got_ref):
        del after_ref, g_out, got_ref
        x, y, c, _, _ = _place()
        cp = _remote(g_ref.at[:, 1 - c], land_ref, send_sem, recv_sem, (x, y, 1 - c))
        cp.wait_send()
        cp.wait_recv()

    return pl.pallas_call(
        body,
        name=name + "_wait",
        out_shape=(pltpu.HBM(g_thru.shape, g_thru.dtype), pltpu.HBM(land_thru.shape, land_thru.dtype)),
        in_specs=(HBM_SPEC, HBM_SPEC, SEM_SPEC, SEM_SPEC, ANY),
        out_specs=(HBM_SPEC, HBM_SPEC),
        input_output_aliases={0: 0, 1: 1},
        compiler_params=pltpu.CompilerParams(has_side_effects=SIDE_EFFECT),
    )(g_thru, land_thru, send_sem, recv_sem, after)


def _swap_whole(packed):
    def plan(cins, couts):
        x, y, c, _, _ = _place()
        return [(cins[0], couts[0], (x, y, 1 - c))], []

    return _Exchange([packed], [jax.ShapeDtypeStruct(packed.shape, packed.dtype)], plan, 1)


def _spread_chip_sums(chip_sum):
    def plan(cins, couts):
        _, _, c, chip, peers = _place()
        return ([(cins[0], couts[0].at[chip], (px, py, c)) for px, py in peers], [(cins[0], couts[0].at[chip])])

    return _Exchange([chip_sum], [jax.ShapeDtypeStruct((N_CHIPS,) + chip_sum.shape, chip_sum.dtype)], plan, 3, 1)


def _add2(a, b, name):
    def body(a_ref, b_ref, o_ref):
        o_ref[...] = a_ref[...] + b_ref[...]

    return pl.pallas_call(body, name=name, in_specs=[VMEM_WHOLE] * 2, out_specs=VMEM_WHOLE,
                          out_shape=jax.ShapeDtypeStruct(a.shape, F32))(a, b)


def _sum_chips(parts, name):
    def body(p_ref, o_ref):
        o_ref[...] = ((p_ref[0] + p_ref[1]) + p_ref[2]) + p_ref[3]

    return pl.pallas_call(body, name=name, in_specs=[VMEM_WHOLE], out_specs=VMEM_WHOLE,
                          out_shape=jax.ShapeDtypeStruct(parts.shape[1:], F32))(parts)


def _add_halves(grad, other, core, name):
    n_shards, _, rows, cols = grad.shape
    tr = _tile(rows, max(16, (1 << 22) // (4 * cols) // 16 * 16), mult=16)

    def body(core_ref, g_ref, o_ref, out_ref):
        del core_ref
        out_ref[...] = (g_ref[...] + o_ref[...]).astype(BF16)

    return pl.pallas_call(
        body,
        name=name,
        grid_spec=pltpu.PrefetchScalarGridSpec(
            num_scalar_prefetch=1,
            grid=(n_shards, rows // tr),
            in_specs=[
                pl.BlockSpec((None, None, tr, cols), lambda s, i, core_ref: (s, core_ref[0], i, 0)),
                pl.BlockSpec((None, tr, cols), lambda s, i, core_ref: (s, i, 0)),
            ],
            out_specs=pl.BlockSpec((None, tr, cols), lambda s, i, core_ref: (s, i, 0)),
        ),
        out_shape=jax.ShapeDtypeStruct((n_shards, rows, cols), BF16),
        compiler_params=_params("arbitrary", "arbitrary"),
    )(core, grad, other)


def _inproj_fwd(x2, g_row, w_l, layer, ride=None):
    n = x2.shape[0]
    tm = _tile(n, 2048)

    def body(x_ref, g_ref, w_ref, p_ref, h_ref):
        @pl.when(pl.program_id(1) == 0)
        def _():
            xv = x_ref[...]
            r = lax.rsqrt(_lanemean(xv * xv) + RMS_EPS)
            h_ref[...] = (xv * r * g_ref[...]).astype(BF16)

        p_ref[...] = _dot(h_ref[...], w_ref[...]).astype(BF16)

    return _host_call(
        body,
        name=f"inproj_fwd_l{layer}",
        grid=(n // tm, N_COL_TILES),
        in_specs=[
            pl.BlockSpec((tm, D_MODEL), lambda i, j: (i, 0)),
            pl.BlockSpec((1, D_MODEL), lambda i, j: (0, 0)),
            pl.BlockSpec((None, D_MODEL, COL_TILE), lambda i, j: (j // 2, 0, j % 2)),
        ],
        out_specs=[
            pl.BlockSpec((tm, COL_TILE), lambda i, j: (i, j)),
            pl.BlockSpec((tm, D_MODEL), lambda i, j: (i, 0)),
        ],
        out_shape=[jax.ShapeDtypeStruct((n, IN_COLS), BF16), jax.ShapeDtypeStruct((n, D_MODEL), BF16)],
        args=[x2, g_row, w_l],
        ride=ride,
    )


def _layer_norm_parts(v, g, b):
    mu = _lanemean(v)
    d = v - mu
    rstd = lax.rsqrt(_lanemean(d * d) + LN_EPS)
    xh = d * rstd
    return xh, rstd, xh * g + b


def _window_sum(ref, slot, weight_row, n_taps, base, t_rows):
    total = None
    for b in range(8):
        group = None
        for a in range((base + n_taps - 1) // 8 + 1):
            k = 8 * a + b - base
            if 0 <= k < n_taps:
                term = weight_row(k) * ref[slot, 8 * a:8 * a + t_rows + 8, :]
                group = term if group is None else group + term
        if group is not None:
            part = group[b:b + t_rows, :]
            total = part if total is None else total + part
    return total


def _tap_grads(acc_ref, pad_ref, shift_ref, dy, ref, slot, n_taps, base, t_rows):
    pad_ref[0:8, :] = jnp.zeros((8, BRANCH_W), F32)
    pad_ref[8:8 + t_rows, :] = dy
    pad_ref[8 + t_rows:16 + t_rows, :] = jnp.zeros((8, BRANCH_W), F32)
    for b in range(8):
        taps = [(a, 8 * a + b - base) for a in range((base + n_taps - 1) // 8 + 1) if 0 <= 8 * a + b - base < n_taps]
        if not taps:
            continue
        shift_ref[...] = pad_ref[pl.ds(8 - b, t_rows + 8), :]
        for a, k in taps:
            prod = shift_ref[...] * ref[slot, 8 * a:8 * a + t_rows + 8, :]
            acc_ref[k] += prod.reshape((t_rows + 8) // 8, 8, BRANCH_W).sum(axis=0)


def _trailing_window_sums(ext_ref, pool_ref, t_rows):
    lo, n = HALO - 16, t_rows + 16
    pool_ref[0:16, :] = jnp.zeros((16, BRANCH_W), F32)
    sums = []
    for j, win in enumerate(POOL_WINDOWS):
        cols = slice(j * GROUP_W, (j + 1) * GROUP_W)
        v = ext_ref[0, lo:lo + n, cols] + ext_ref[0, pl.ds(lo - 1, n), cols]
        width = 2
        while width < win:
            pool_ref[16:16 + n, cols] = v
            v = v + pool_ref[pl.ds(16 - width, n), cols]
            width *= 2
        sums.append(v[16:, :])
    return sums


def _leading_window_sums(nxt_ref, pool_ref, t_rows):
    n = t_rows + 16
    pool_ref[n:n + 16, :] = jnp.zeros((16, BRANCH_W), F32)
    sums = []
    for j, win in enumerate(POOL_WINDOWS):
        cols = slice(j * GROUP_W, (j + 1) * GROUP_W)
        v = nxt_ref[0, 0:n, cols] + nxt_ref[0, pl.ds(1, n), cols]
        width = 2
        while width < win:
            pool_ref[0:n, cols] = v
            v = v + pool_ref[pl.ds(width, n), cols]
            width *= 2
        sums.append(v[0:t_rows, :])
    return sums


def _tril_mask():
    r = lax.broadcasted_iota(jnp.int32, (CHUNK, CHUNK), 0)
    c = lax.broadcasted_iota(jnp.int32, (CHUNK, CHUNK), 1)
    return r >= c


def _lanes(refs4):
    return jnp.concatenate([refs4[g] for g in range(N_CHIPS)], axis=1)


def _mixer_weight_specs():
    def whole(shape):
        nd = len(shape)
        return pl.BlockSpec(shape, lambda b, c: (0,) * nd)

    return [
        whole((N_GROUPS, GROUP_W, GROUP_W)),
        whole((1, BRANCH_W)),
        whole((N_CHIPS, CONV_ROWS, GROUP_W)),
        whole((1, BRANCH_W)),
        whole((1, BRANCH_W)),
        whole((1, BRANCH_W)),
        whole((1, BRANCH_W)),
        whole((1, BRANCH_W)),
        whole((N_GROUPS, CHUNK, CHUNK)),
        whole((CHUNK, BRANCH_W)),
        whole((N_CHIPS, SHORT_ROWS, GROUP_W)),
    ]


def _mixer_weight_args(mw):
    return [mw["pool_w"], mw["pool_scale"], mw["conv_w"], mw["conv_b"], mw["conv_ln_g"], mw["conv_ln_b"],
            mw["sgu_ln_g"], mw["sgu_ln_b"], mw["sgu_w"], mw["sgu_bias"], mw["sc_w"]]


def _mix_merge_fwd(proj3, x3, mw, wbr_l, wo_l, layer, ride=None, head=None):
    nb, seq, _ = proj3.shape
    t_rows = _tile(seq, 512)
    nc = seq // t_rows
    hb = t_rows // HALO
    n_head = 2 if head else 0

    def body(*refs):
        (cur_ref, halo_ref, g01_ref, g23_ref, x_ref, pw_ref, ps_ref, cw_ref, cb_ref, clg_ref, clb_ref, slg_ref,
         slb_ref, sw_ref, sbias_ref, scw_ref, wbr_ref, wo_ref) = refs[:18]
        head_in = refs[18:18 + n_head]
        outs = refs[18 + n_head:-2]
        ext_ref, pool_ref = refs[-2:]
        z_ref, cacc_ref, mg_ref, xn_ref = outs[:4]
        b = pl.program_id(0)
        c = pl.program_id(1)
        has_prev = c > 0
        row = lax.broadcasted_iota(jnp.int32, (t_rows, 1), 0)
        tpos = (c * t_rows + row + 1).astype(F32)

        def cur(k):
            return cur_ref[:, k * BRANCH_W:(k + 1) * BRANCH_W].astype(F32)

        def hal(k):
            return halo_ref[:, k * BRANCH_W:(k + 1) * BRANCH_W].astype(F32)

        def put_ext(slot, halo_val, cur_val):
            ext_ref[slot, 0:HALO, :] = jnp.where(has_prev, halo_val, 0.0)
            ext_ref[slot, HALO:HALO + t_rows, :] = cur_val
            ext_ref[slot, HALO + t_rows:HALO + t_rows + 8, :] = jnp.zeros((8, BRANCH_W), F32)

        def gated(branch, z_val):
            zb = z_val.astype(BF16)
            z_ref[:, branch * BRANCH_W:(branch + 1) * BRANCH_W] = zb
            gate_ref = g01_ref if branch < 2 else g23_ref
            gate = gate_ref[:, (branch % 2) * D_MODEL:(branch % 2 + 1) * D_MODEL].astype(F32)
            return _sigmoid(gate) * _dot(zb, wbr_ref[branch])

        px = cur(P_X)
        put_ext(0, hal(P_X), px)
        mixed = []
        window_sums = _trailing_window_sums(ext_ref, pool_ref, t_rows)
        for j, win in enumerate(POOL_WINDOWS):
            cols = slice(j * GROUP_W, (j + 1) * GROUP_W)
            pooled = window_sums[j] / jnp.minimum(tpos, float(win)) - px[:, cols]
            mixed.append(_dot(pooled.astype(BF16), pw_ref[j]))
        acc = gated(0, jnp.concatenate(mixed, axis=1) * ps_ref[...] * _silu(cur(P_GATE)))

        put_ext(1, hal(C_A) * _sigmoid(hal(C_B)), cur(C_A) * _sigmoid(cur(C_B)))
        cw = _lanes(cw_ref)
        conv = cb_ref[...] + _window_sum(ext_ref, 1, lambda k: cw[k:k + 1, :], CONV_K, HALO - (CONV_K - 1), t_rows)
        cacc_ref[...] = conv
        _, _, ln = _layer_norm_parts(conv, clg_ref[...], clb_ref[...])
        acc = acc + gated(1, _silu(ln) * _silu(cur(C_GATE)))

        _, _, v = _layer_norm_parts(cur(G_V), slg_ref[...], slb_ref[...])
        vb = v.astype(BF16)
        mask = _tril_mask()
        wt = [jnp.where(mask, sw_ref[g], 0.0).astype(BF16) for g in range(N_GROUPS)]
        sp_rows = []
        for sub in range(t_rows // CHUNK):
            rows = slice(sub * CHUNK, (sub + 1) * CHUNK)
            sp_rows.append(jnp.concatenate(
                [_dot(wt[g], vb[rows, g * GROUP_W:(g + 1) * GROUP_W]) for g in range(N_GROUPS)], axis=1)
                + sbias_ref[...])
        sp = jnp.concatenate(sp_rows, axis=0)
        acc = acc + gated(2, cur(G_U) * sp * _silu(cur(G_GATE)))

        put_ext(2, hal(S_C) * hal(S_X), cur(S_C) * cur(S_X))
        scw = _lanes(scw_ref)
        cv = _window_sum(ext_ref, 2, lambda k: scw[k:k + 1, :], SHORT_K, HALO - (SHORT_K - 1), t_rows)
        acc = acc + gated(3, cur(S_B) * cv * _silu(cur(S_GATE)))

        mg = acc.astype(BF16)
        mg_ref[...] = mg
        xv = x_ref[...] + _dot(mg, wo_ref[...])
        if not head:
            xn_ref[...] = xv
            return
        g_ref, t_ref = head_in
        loss_ref, dg_ref = outs[4:]

        @pl.when(jnp.logical_and(b == 0, c == 0))
        def _():
            loss_ref[...] = jnp.zeros_like(loss_ref)
            dg_ref[...] = jnp.zeros_like(dg_ref)

        g = g_ref[...]
        r = lax.rsqrt(_lanemean(xv * xv) + RMS_EPS)
        xh = xv * r
        err = xh * g - t_ref[...]
        loss_ref[...] += 0.5 * jnp.sum(_lanemean(err * err), axis=0, keepdims=True)
        dy = err * (1.0 / D_MODEL)
        dg_ref[...] += _rowsum(dy * xh)
        dxh = dy * g
        xn_ref[...] = r * (dxh - xh * _lanemean(dxh * xh))

    def rows_spec(cols, col_block=0):
        return pl.BlockSpec((None, t_rows, cols), lambda b, c: (b, c, col_block))

    def whole(shape):
        nd = len(shape)
        return pl.BlockSpec(shape, lambda b, c: (0,) * nd)

    in_specs = [
        rows_spec(PIECE_COLS),
        pl.BlockSpec((None, HALO, PIECE_COLS), lambda b, c: (b, jnp.maximum(c * hb - 1, 0), 0)),
        rows_spec(2 * D_MODEL, PIECE_COLS // (2 * D_MODEL)),
        rows_spec(2 * D_MODEL, PIECE_COLS // (2 * D_MODEL) + 1),
        rows_spec(D_MODEL),
    ] + _mixer_weight_specs() + [whole((N_BRANCH, BRANCH_W, D_MODEL)), whole((D_MODEL, D_MODEL))]
    args = [proj3, proj3, proj3, proj3, x3, *_mixer_weight_args(mw), wbr_l, wo_l]
    out_specs = [rows_spec(N_BRANCH * BRANCH_W), rows_spec(BRANCH_W), rows_spec(D_MODEL), rows_spec(D_MODEL)]
    out_shape = [jax.ShapeDtypeStruct((nb, seq, N_BRANCH * BRANCH_W), BF16), jax.ShapeDtypeStruct((nb, seq, BRANCH_W), F32),
                 jax.ShapeDtypeStruct((nb, seq, D_MODEL), BF16), jax.ShapeDtypeStruct((nb, seq, D_MODEL), F32)]
    if head:
        in_specs += [whole((1, D_MODEL)), rows_spec(D_MODEL)]
        args += list(head)
        out_specs += [whole((1, GROUP_W)), whole((1, D_MODEL))]
        out_shape += [jax.ShapeDtypeStruct((1, GROUP_W), F32), jax.ShapeDtypeStruct((1, D_MODEL), F32)]
    return _host_call(
        body,
        name=f"mix_merge_fwd_l{layer}",
        grid=(nb, nc),
        in_specs=in_specs,
        out_specs=out_specs,
        out_shape=out_shape,
        scratch_shapes=[pltpu.VMEM((3, HALO + t_rows + 8, BRANCH_W), F32),
                        pltpu.VMEM((t_rows + 32, BRANCH_W), F32)],
        args=args,
        ride=ride,
    )


MIXER_GRADS = ["pool_w", "pool_scale", "conv_w", "conv_b", "conv_ln_g", "conv_ln_b", "sgu_ln_g", "sgu_ln_b", "sgu_w",
               "sgu_b", "sc_w"]


def _mixers_bwd(proj3, dz3, cacc3, dproj3, mw, layer, ride=None):
    nb, seq, _ = proj3.shape
    t_rows = _tile(seq, 512)
    nc = seq // t_rows
    hb = t_rows // HALO

    def body(cur_ref, halo_ref, dz_ref, cacc_ref, pw_ref, ps_ref, cw_ref, cb_ref, clg_ref, clb_ref, slg_ref, slb_ref,
             sw_ref, sbias_ref, scw_ref, dp_in_ref, dp_ref, g_pw, g_ps, g_cw, g_cb, g_clg, g_clb, g_slg,
             g_slb, g_sw, g_sb, g_scw, ext_ref, nxt_ref, sb_acc, cw_acc, scw_acc, pad_ref, shift_ref,
             pool_ref):
        del dp_in_ref
        b = pl.program_id(0)
        r = pl.program_id(1)
        c = nc - 1 - r
        has_prev = c > 0
        row = lax.broadcasted_iota(jnp.int32, (t_rows, 1), 0)
        tpos = (c * t_rows + row + 1).astype(F32)

        @pl.when(jnp.logical_and(b == 0, r == 0))
        def _():
            for ref in (g_pw, g_ps, g_cb, g_clg, g_clb, g_slg, g_slb, g_sw, sb_acc, cw_acc, scw_acc):
                ref[...] = jnp.zeros_like(ref)

        @pl.when(r == 0)
        def _():
            nxt_ref[:, t_rows:t_rows + HALO, :] = jnp.zeros((3, HALO, BRANCH_W), F32)

        def cur(k):
            return cur_ref[:, k * BRANCH_W:(k + 1) * BRANCH_W].astype(F32)

        def hal(k):
            return halo_ref[:, k * BRANCH_W:(k + 1) * BRANCH_W].astype(F32)

        def dzp(k):
            return dz_ref[:, k * BRANCH_W:(k + 1) * BRANCH_W].astype(F32)

        def put_dp(k, val):
            dp_ref[:, k * BRANCH_W:(k + 1) * BRANCH_W] = val.astype(BF16)

        def put_ext(slot, halo_val, cur_val):
            ext_ref[slot, 0:HALO, :] = jnp.where(has_prev, halo_val, 0.0)
            ext_ref[slot, HALO:HALO + t_rows, :] = cur_val
            ext_ref[slot, HALO + t_rows:HALO + t_rows + 8, :] = jnp.zeros((8, BRANCH_W), F32)

        px = cur(P_X)
        put_ext(0, hal(P_X), px)
        pgate = cur(P_GATE)
        dz_pool = dzp(0)
        ps = ps_ref[...]
        pooled, mixed, cnts = [], [], []
        window_sums = _trailing_window_sums(ext_ref, pool_ref, t_rows)
        for j, win in enumerate(POOL_WINDOWS):
            cols = slice(j * GROUP_W, (j + 1) * GROUP_W)
            cnt = jnp.minimum(tpos, float(win))
            pj = window_sums[j] / cnt - px[:, cols]
            cnts.append(cnt)
            pooled.append(pj.astype(BF16))
            mixed.append(_dot(pooled[j], pw_ref[j]))
        mixed = jnp.concatenate(mixed, axis=1)
        pg_silu, pg_dsilu = _silu_pair(pgate)
        put_dp(P_GATE, dz_pool * (mixed * ps) * pg_dsilu)
        d_out = dz_pool * pg_silu
        g_ps[...] += _rowsum(d_out * mixed)
        d_mixed = (d_out * ps).astype(BF16)
        d_pooled = []
        for j in range(N_GROUPS):
            cols = slice(j * GROUP_W, (j + 1) * GROUP_W)
            g_pw[j] += _dot_tn(pooled[j], d_mixed[:, cols])
            dpj = _dot_nt(d_mixed[:, cols], pw_ref[j])
            d_pooled.append(dpj)
            nxt_ref[0, 0:t_rows, cols] = dpj / cnts[j]
        lead_sums = _leading_window_sums(nxt_ref, pool_ref, t_rows)
        put_dp(P_X, jnp.concatenate([lead_sums[j] - d_pooled[j] for j in range(N_GROUPS)], axis=1))

        c_a = cur(C_A)
        sig_b = _sigmoid(cur(C_B))
        put_ext(1, hal(C_A) * _sigmoid(hal(C_B)), c_a * sig_b)
        cw = _lanes(cw_ref)
        xh, rstd, ln = _layer_norm_parts(cacc_ref[...], clg_ref[...], clb_ref[...])
        cgate = cur(C_GATE)
        dz_conv = dzp(1)
        ln_silu, ln_dsilu = _silu_pair(ln)
        cg_silu, cg_dsilu = _silu_pair(cgate)
        put_dp(C_GATE, dz_conv * ln_silu * cg_dsilu)
        d_ln = dz_conv * cg_silu * ln_dsilu
        g_clg[...] += _rowsum(d_ln * xh)
        g_clb[...] += _rowsum(d_ln)
        dxh = d_ln * clg_ref[...]
        dc = rstd * (dxh - _lanemean(dxh) - xh * _lanemean(dxh * xh))
        g_cb[...] += _rowsum(dc)
        nxt_ref[1, 0:t_rows, :] = dc
        _tap_grads(cw_acc, pad_ref, shift_ref, dc, ext_ref, 1, CONV_K, HALO - (CONV_K - 1), t_rows)
        dyg = _window_sum(nxt_ref, 1, lambda i: cw[CONV_K - 1 - i:CONV_K - i, :], CONV_K, 0, t_rows)
        put_dp(C_A, dyg * sig_b)
        put_dp(C_B, dyg * c_a * sig_b * (1.0 - sig_b))

        u = cur(G_U)
        ggate = cur(G_GATE)
        vxh, vrstd, v = _layer_norm_parts(cur(G_V), slg_ref[...], slb_ref[...])
        vb = v.astype(BF16)
        mask = _tril_mask()
        wt = [jnp.where(mask, sw_ref[g], 0.0) for g in range(N_GROUPS)]
        wt_b = [w.astype(BF16) for w in wt]
        wtt_b = [w.T.astype(BF16) for w in wt]
        dz_sgu = dzp(2)
        gg_silu, gg_dsilu = _silu_pair(ggate)
        d_sgu = dz_sgu * gg_silu
        d_sp = d_sgu * u
        d_spb = d_sp.astype(BF16)
        sp_rows, dv_rows = [], []
        for sub in range(t_rows // CHUNK):
            rows = slice(sub * CHUNK, (sub + 1) * CHUNK)
            sp_g, dv_g = [], []
            for g in range(N_GROUPS):
                cols = slice(g * GROUP_W, (g + 1) * GROUP_W)
                sp_g.append(_dot(wt_b[g], vb[rows, cols]))
                g_sw[g] += jnp.where(mask, _dot_nt(d_spb[rows, cols], vb[rows, cols]), 0.0)
                dv_g.append(_dot(wtt_b[g], d_spb[rows, cols]))
            sp_rows.append(jnp.concatenate(sp_g, axis=1) + sbias_ref[...])
            dv_rows.append(jnp.concatenate(dv_g, axis=1))
            sb_acc[...] += d_sp[rows, :]
        sp = jnp.concatenate(sp_rows, axis=0)
        dv = jnp.concatenate(dv_rows, axis=0)
        put_dp(G_GATE, dz_sgu * (u * sp) * gg_dsilu)
        put_dp(G_U, d_sgu * sp)
        g_slg[...] += _rowsum(dv * vxh)
        g_slb[...] += _rowsum(dv)
        dvx = dv * slg_ref[...]
        put_dp(G_V, vrstd * (dvx - _lanemean(dvx) - vxh * _lanemean(dvx * vxh)))

        s_b, s_c, s_x, sgate = cur(S_B), cur(S_C), cur(S_X), cur(S_GATE)
        put_ext(2, hal(S_C) * hal(S_X), s_c * s_x)
        scw = _lanes(scw_ref)
        cv = _window_sum(ext_ref, 2, lambda k: scw[k:k + 1, :], SHORT_K, HALO - (SHORT_K - 1), t_rows)
        dz_sc = dzp(3)
        sg_silu, sg_dsilu = _silu_pair(sgate)
        put_dp(S_GATE, dz_sc * (s_b * cv) * sg_dsilu)
        d_pre = dz_sc * sg_silu
        put_dp(S_B, d_pre * cv)
        dcv = d_pre * s_b
        nxt_ref[2, 0:t_rows, :] = dcv
        _tap_grads(scw_acc, pad_ref, shift_ref, dcv, ext_ref, 2, SHORT_K, HALO - (SHORT_K - 1), t_rows)
        du = _window_sum(nxt_ref, 2, lambda i: scw[SHORT_K - 1 - i:SHORT_K - i, :], SHORT_K, 0, t_rows)
        put_dp(S_C, du * s_x)
        put_dp(S_X, du * s_c)

        nxt_ref[:, t_rows:t_rows + HALO, :] = nxt_ref[:, 0:HALO, :]

        @pl.when(jnp.logical_and(b == nb - 1, r == nc - 1))
        def _():
            lane = lax.broadcasted_iota(jnp.int32, (CHUNK, GROUP_W), 1)
            out = jnp.zeros((CHUNK, GROUP_W), F32)
            for g in range(N_GROUPS):
                col = jnp.sum(sb_acc[:, g * GROUP_W:(g + 1) * GROUP_W], axis=1, keepdims=True)
                out = jnp.where(lane == g, col, out)
            g_sb[...] = out
            g_cw[...] = jnp.sum(cw_acc[...], axis=1)
            g_scw[...] = jnp.sum(scw_acc[...], axis=1)

    def acc_spec(shape):
        nd = len(shape)
        return pl.BlockSpec(shape, lambda b, r: (0,) * nd)

    acc_shapes = [
        (N_GROUPS, GROUP_W, GROUP_W),
        (1, BRANCH_W),
        (CONV_ROWS, BRANCH_W),
        (1, BRANCH_W),
        (1, BRANCH_W),
        (1, BRANCH_W),
        (1, BRANCH_W),
        (1, BRANCH_W),
        (N_GROUPS, CHUNK, CHUNK),
        (CHUNK, GROUP_W),
        (SHORT_ROWS, BRANCH_W),
    ]
    outs, rode = _host_call(
        body,
        name=f"mixers_bwd_l{layer}",
        grid=(nb, nc),
        in_specs=[
            pl.BlockSpec((None, t_rows, PIECE_COLS), lambda b, r: (b, nc - 1 - r, 0)),
            pl.BlockSpec((None, HALO, PIECE_COLS), lambda b, r: (b, jnp.maximum((nc - 1 - r) * hb - 1, 0), 0)),
            pl.BlockSpec((None, t_rows, N_BRANCH * BRANCH_W), lambda b, r: (b, nc - 1 - r, 0)),
            pl.BlockSpec((None, t_rows, BRANCH_W), lambda b, r: (b, nc - 1 - r, 0)),
        ] + _mixer_weight_specs() + [ANY],
        out_specs=[pl.BlockSpec((None, t_rows, PIECE_COLS), lambda b, r: (b, nc - 1 - r, 0))]
        + [acc_spec(s) for s in acc_shapes],
        out_shape=[jax.ShapeDtypeStruct(dproj3.shape, BF16)] + [jax.ShapeDtypeStruct(s, F32) for s in acc_shapes],
        scratch_shapes=[
            pltpu.VMEM((3, HALO + t_rows + 8, BRANCH_W), F32),
            pltpu.VMEM((3, t_rows + HALO, BRANCH_W), F32),
            pltpu.VMEM((CHUNK, BRANCH_W), F32),
            pltpu.VMEM((CONV_ROWS, 8, BRANCH_W), F32),
            pltpu.VMEM((SHORT_ROWS, 8, BRANCH_W), F32),
            pltpu.VMEM((t_rows + 16, BRANCH_W), F32),
            pltpu.VMEM((t_rows + 8, BRANCH_W), F32),
            pltpu.VMEM((t_rows + 32, BRANCH_W), F32),
        ],
        aliases={4 + 11: 0},
        args=[proj3, proj3, dz3, cacc3, *_mixer_weight_args(mw), dproj3],
        ride=ride,
    )
    return outs[0], dict(zip(MIXER_GRADS, outs[1:])), rode


def _merge_bwd(dout2, merged2, z2, proj2, wbr_l, wo_l, layer, ride=None):
    n = dout2.shape[0]
    tm = _tile(n, 1024)
    nt = n // tm

    def body(do_ref, mg_ref, z_ref, gate_ref, wbr_ref, wo_ref, dz_ref, dg_ref, gwo_ref, gwbr_ref, dm_ref,
             awo_ref, awbr_ref):
        i = pl.program_id(0)
        nbr = pl.program_id(1)

        @pl.when(nbr == 0)
        def _():
            do_b = do_ref[...].astype(BF16)
            dm_ref[...] = _dot_nt(do_b, wo_ref[...])
            gw = _dot_tn(mg_ref[...], do_b)

            @pl.when(i == 0)
            def _():
                awo_ref[...] = gw

            @pl.when(i > 0)
            def _():
                awo_ref[...] += gw

        zt = z_ref[...]
        wbr = wbr_ref[nbr]
        bo = _dot(zt, wbr)
        gt = _sigmoid(gate_ref[...].astype(F32))
        dm = dm_ref[...]
        dbo = (dm * gt).astype(BF16)
        dg_ref[...] = (dm * bo * gt * (1.0 - gt)).astype(BF16)
        dz_ref[...] = _dot_nt(dbo, wbr).astype(BF16)
        gw = _dot_tn(zt, dbo)

        @pl.when(i == 0)
        def _():
            awbr_ref[nbr] = gw

        @pl.when(i > 0)
        def _():
            awbr_ref[nbr] += gw

        @pl.when(jnp.logical_and(i == nt - 1, nbr == N_BRANCH - 1))
        def _():
            pltpu.sync_copy(awo_ref, gwo_ref)
            pltpu.sync_copy(awbr_ref, gwbr_ref)

    return _host_call(
        body,
        name=f"merge_bwd_l{layer}",
        grid=(nt, N_BRANCH),
        in_specs=[
            pl.BlockSpec((tm, D_MODEL), lambda i, b: (i, 0)),
            pl.BlockSpec((tm, D_MODEL), lambda i, b: (i, 0)),
            pl.BlockSpec((tm, BRANCH_W), lambda i, b: (i, b)),
            pl.BlockSpec((tm, D_MODEL), lambda i, b: (i, GATE_BLOCK0 + b)),
            pl.BlockSpec((N_BRANCH, BRANCH_W, D_MODEL), lambda i, b: (0, 0, 0)),
            pl.BlockSpec((D_MODEL, D_MODEL), lambda i, b: (0, 0)),
        ],
        out_specs=[
            pl.BlockSpec((tm, BRANCH_W), lambda i, b: (i, b)),
            pl.BlockSpec((tm, D_MODEL), lambda i, b: (i, GATE_BLOCK0 + b)),
            ANY,
            ANY,
        ],
        out_shape=[
            jax.ShapeDtypeStruct((n, N_BRANCH * BRANCH_W), BF16),
            jax.ShapeDtypeStruct((n, IN_COLS), BF16),
            jax.ShapeDtypeStruct((D_MODEL, D_MODEL), F32),
            jax.ShapeDtypeStruct((N_BRANCH, BRANCH_W, D_MODEL), F32),
        ],
        scratch_shapes=[
            pltpu.VMEM((tm, D_MODEL), F32),
            pltpu.VMEM((D_MODEL, D_MODEL), F32),
            pltpu.VMEM((N_BRANCH, BRANCH_W, D_MODEL), F32),
        ],
        args=[dout2, merged2, z2, proj2, wbr_l, wo_l],
        ride=ride,
    )


def _inproj_bwd_x(dproj2, w_l, x2, g_row, dout2, layer, ride=None):
    n = x2.shape[0]
    tm = _tile(n, 1024)

    def body(dp_ref, w_ref, x_ref, g_ref, do_ref, dx_ref, dng_ref, dh_ref):
        i = pl.program_id(0)
        s = pl.program_id(1)
        part = _dot_nt(dp_ref[...], w_ref[...])

        @pl.when(s == 0)
        def _():
            dh_ref[...] = part

        @pl.when(s > 0)
        def _():
            dh_ref[...] += part

        @pl.when(jnp.logical_and(i == 0, s == 0))
        def _():
            dng_ref[...] = jnp.zeros_like(dng_ref)

        @pl.when(s == N_CHIPS - 1)
        def _():
            xv = x_ref[...]
            r = lax.rsqrt(_lanemean(xv * xv) + RMS_EPS)
            xh = xv * r
            dh = dh_ref[...]
            dng_ref[...] += _rowsum(dh * xh)
            dxh = dh * g_ref[...]
            dx_ref[...] = do_ref[...] + r * (dxh - xh * _lanemean(dxh * xh))

    return _host_call(
        body,
        name=f"inproj_bwd_x_l{layer}",
        grid=(n // tm, N_CHIPS),
        in_specs=[
            pl.BlockSpec((tm, SHARD_COLS), lambda i, s: (i, s)),
            pl.BlockSpec((None, D_MODEL, SHARD_COLS), lambda i, s: (s, 0, 0)),
            pl.BlockSpec((tm, D_MODEL), lambda i, s: (i, 0)),
            pl.BlockSpec((1, D_MODEL), lambda i, s: (0, 0)),
            pl.BlockSpec((tm, D_MODEL), lambda i, s: (i, 0)),
        ],
        out_specs=[
            pl.BlockSpec((tm, D_MODEL), lambda i, s: (i, 0)),
            pl.BlockSpec((1, D_MODEL), lambda i, s: (0, 0)),
        ],
        out_shape=[jax.ShapeDtypeStruct((n, D_MODEL), F32), jax.ShapeDtypeStruct((1, D_MODEL), F32)],
        scratch_shapes=[pltpu.VMEM((tm, D_MODEL), F32)],
        args=[dproj2, w_l, x2, g_row, dout2],
        ride=ride,
    )


def _inproj_bwd_w(h2, dproj2, layer, ride=None):
    n = h2.shape[0]
    tm = _tile(n, 2048)

    def body(h_ref, dp_ref, gw_ref):
        gw = _dot_tn(h_ref[...], dp_ref[...])

        @pl.when(pl.program_id(1) == 0)
        def _():
            gw_ref[...] = gw

        @pl.when(pl.program_id(1) > 0)
        def _():
            gw_ref[...] += gw

    outs, rode = _host_call(
        body,
        name=f"inproj_bwd_w_l{layer}",
        grid=(N_COL_TILES, n // tm),
        in_specs=[
            pl.BlockSpec((tm, D_MODEL), lambda s, i: (i, 0)),
            pl.BlockSpec((tm, COL_TILE), lambda s, i: (i, s)),
        ],
        out_specs=[pl.BlockSpec((None, D_MODEL, COL_TILE), lambda s, i: (s // 2, 0, s % 2))],
        out_shape=[jax.ShapeDtypeStruct((N_CHIPS, D_MODEL, SHARD_COLS), F32)],
        args=[h2, dproj2],
        ride=ride,
    )
    return outs[0], rode


def _adamw_math(w, g, m, v):
    m = ADAM_B1 * m + (1.0 - ADAM_B1) * g
    v = ADAM_B2 * v + (1.0 - ADAM_B2) * jnp.square(g)
    m_hat = m / (1.0 - ADAM_B1 ** ADAM_STEP)
    v_hat = v / (1.0 - ADAM_B2 ** ADAM_STEP)
    delta = -ADAM_LR * (m_hat / (jnp.sqrt(v_hat) + ADAM_EPS) + ADAM_WD * w)
    return delta, m, v


def _adamw_sharded(w, m, v, part, layer, prev, name, ride=None, after=None):
    assert prev is None or after is None
    _, rows, cols = w.shape
    tr = _tile(rows, max(16, (1 << 21) // (4 * cols) // 16 * 16), mult=16)

    def body(w_ref, m_ref, v_ref, p_ref, *rest):
        g_out, d_out, m_out, v_out = rest[-4:]
        g = ((p_ref[0].astype(F32) + p_ref[1].astype(F32)) + p_ref[2].astype(F32)) + p_ref[3].astype(F32)
        delta, m2, v2 = _adamw_math(w_ref[...], g, m_ref[...], v_ref[...])
        g_out[...] = g
        d_out[...] = delta
        m_out[...] = m2
        v_out[...] = v2

    spec = pl.BlockSpec((None, tr, cols), lambda i: (layer, i, 0))
    return _host_call(
        body,
        name=f"adamw_{name}_l{layer}",
        grid=(rows // tr,),
        in_specs=[spec, spec, spec, pl.BlockSpec((N_CHIPS, tr, cols), lambda i: (0, i, 0))]
        + ([ANY] * 4 if prev else []) + ([ANY] if after is not None else []),
        out_specs=[spec] * 4,
        out_shape=[jax.ShapeDtypeStruct(w.shape, F32)] * 4,
        args=[w, m, v, part] + (list(prev) if prev else []) + ([after] if after is not None else []),
        aliases={4 + k: k for k in range(4)} if prev else {},
        ride=ride,
    )


def _adamw_packed(w, m, v, g):
    rows = w.shape[0]
    tr = _tile(rows, rows // 2 if rows % 16 == 0 else rows)

    def body(w_ref, m_ref, v_ref, g_ref, d_out, m_out, v_out):
        delta, m2, v2 = _adamw_math(w_ref[...], g_ref[...], m_ref[...], v_ref[...])
        d_out[...] = delta
        m_out[...] = m2
        v_out[...] = v2

    spec = pl.BlockSpec((tr, GROUP_W), lambda i: (i, 0))
    return pl.pallas_call(
        body,
        name="adamw_small",
        grid=(rows // tr,),
        in_specs=[spec] * 4,
        out_specs=[spec] * 3,
        out_shape=[jax.ShapeDtypeStruct(w.shape, F32)] * 3,
        compiler_params=_params("arbitrary"),
    )(w, m, v, g)


SMALL = ["norm_g", "pool_w", "pool_scale", "conv_b", "conv_ln_g", "conv_ln_b", "sgu_ln_g", "sgu_ln_b", "sgu_w",
         "sgu_b", "final_g"]
WEIGHTS = ["norm_g", "w_in", "pool_w", "pool_scale", "conv_w", "conv_b", "conv_ln_g", "conv_ln_b", "sgu_ln_g",
           "sgu_ln_b", "sgu_w", "sgu_b", "sc_w", "w_branch", "w_o", "final_g"]
HALF_SHAPES = [(D_MODEL // 2, SHARD_COLS), (N_BRANCH * BRANCH_W // 2, BR_SHARD), (BR_SHARD // 2, D_MODEL),
               (CS_ROWS // 2, GROUP_W)]


def _pack_small(tree, last_rows=None):
    tail = jnp.zeros((8, GROUP_W), F32) if last_rows is None else last_rows
    return jnp.concatenate([tree[k].reshape(-1, GROUP_W) for k in SMALL] + [tail], axis=0)


def _unpack_small(packed, like):
    out, r = {}, 0
    for k in SMALL:
        nr = like[k].size // GROUP_W
        out[k] = packed[r:r + nr].reshape(like[k].shape)
        r += nr
    return out


def _pad_rows(a, rows):
    pad = [(0, 0)] * a.ndim
    pad[-2] = (0, rows - a.shape[-2])
    return jnp.pad(a, pad)


def _pack_cs(conv, short):
    return jnp.concatenate([_pad_rows(conv, CS_ROWS // 2), _pad_rows(short, CS_ROWS // 2)], axis=-2)


def _shard_major_rows(a):
    return a.reshape(a.shape[0], N_CHIPS, GROUP_W).transpose(1, 0, 2)


def kernel(x, norm_g, w_in, pool_w, pool_scale, conv_w, conv_b, conv_ln_g, conv_ln_b, sgu_ln_g, sgu_ln_b, sgu_w, sgu_b, sc_w, w_branch, w_o, final_g, loss_target, m_norm_g, m_w_in, m_pool_w, m_pool_scale, m_conv_w, m_conv_b, m_conv_ln_g, m_conv_ln_b, m_sgu_ln_g, m_sgu_ln_b, m_sgu_w, m_sgu_b, m_sc_w, m_w_branch, m_w_o, m_final_g, v_norm_g, v_w_in, v_pool_w, v_pool_scale, v_conv_w, v_conv_b, v_conv_ln_g, v_conv_ln_b, v_sgu_ln_g, v_sgu_ln_b, v_sgu_w, v_sgu_b, v_sc_w, v_w_branch, v_w_o, v_final_g):
    w = dict(norm_g=norm_g, w_in=w_in, pool_w=pool_w, pool_scale=pool_scale, conv_w=conv_w, conv_b=conv_b,
             conv_ln_g=conv_ln_g, conv_ln_b=conv_ln_b, sgu_ln_g=sgu_ln_g, sgu_ln_b=sgu_ln_b, sgu_w=sgu_w,
             sgu_b=sgu_b, sc_w=sc_w, w_branch=w_branch, w_o=w_o, final_g=final_g)
    mom = dict(norm_g=m_norm_g, w_in=m_w_in, pool_w=m_pool_w, pool_scale=m_pool_scale, conv_w=m_conv_w,
               conv_b=m_conv_b, conv_ln_g=m_conv_ln_g, conv_ln_b=m_conv_ln_b, sgu_ln_g=m_sgu_ln_g,
               sgu_ln_b=m_sgu_ln_b, sgu_w=m_sgu_w, sgu_b=m_sgu_b, sc_w=m_sc_w, w_branch=m_w_branch, w_o=m_w_o,
               final_g=m_final_g)
    var = dict(norm_g=v_norm_g, w_in=v_w_in, pool_w=v_pool_w, pool_scale=v_pool_scale, conv_w=v_conv_w,
               conv_b=v_conv_b, conv_ln_g=v_conv_ln_g, conv_ln_b=v_conv_ln_b, sgu_ln_g=v_sgu_ln_g,
               sgu_ln_b=v_sgu_ln_b, sgu_w=v_sgu_w, sgu_b=v_sgu_b, sc_w=v_sc_w, w_branch=v_w_branch, w_o=v_w_o,
               final_g=v_final_g)

    nb, seq, _ = x.shape
    n = nb * seq
    core = lax.axis_index("c").astype(jnp.int32).reshape(1)

    win_s = w_in.astype(BF16).reshape(DEPTH, 2, D_MODEL // 2, SHARD_COLS)
    wbr_s = w_branch.astype(BF16)
    wo_s = w_o.astype(BF16)
    cs_s = _pack_cs(conv_w, sc_w)

    (win_f0,) = _allgather_layer0([win_s.reshape(DEPTH, 2, 4, D_MODEL // 8, SHARD_COLS)])

    def layer_weights(win_f, wbr_f, wo_f, cs_f):
        cs_f = cs_f.reshape(N_CHIPS, CS_ROWS, GROUP_W)
        return dict(win=win_f.reshape(N_CHIPS, D_MODEL, SHARD_COLS),
                    wbr=wbr_f.reshape(N_CHIPS, N_BRANCH, BRANCH_W, BR_SHARD).transpose(1, 2, 0, 3).reshape(
                        N_BRANCH, BRANCH_W, D_MODEL),
                    wo=wo_f.reshape(D_MODEL, D_MODEL),
                    conv_w=cs_f[:, :CONV_ROWS], sc_w=cs_f[:, CS_ROWS // 2:CS_ROWS // 2 + SHORT_ROWS])

    def mixer_weights(l, gathered):
        row = lambda a: a[l].reshape(1, BRANCH_W)
        bias = jnp.repeat(jnp.swapaxes(sgu_b[l], 0, 1), GROUP_W, axis=1)
        return dict(pool_w=pool_w[l].astype(BF16), pool_scale=row(pool_scale), conv_w=gathered["conv_w"],
                    conv_b=row(conv_b), conv_ln_g=row(conv_ln_g), conv_ln_b=row(conv_ln_b), sgu_ln_g=row(sgu_ln_g),
                    sgu_ln_b=row(sgu_ln_b), sgu_w=sgu_w[l], sgu_bias=bias, sc_w=gathered["sc_w"])

    lw, mw = [None, None], [None, None]

    xs, projs, hs, zs, mgs, caccs = [x.reshape(n, D_MODEL)], [], [], [], [], []
    for l in range(DEPTH):
        first = l == 0
        win_l = win_f0.reshape(N_CHIPS, D_MODEL, SHARD_COLS) if first else lw[1]["win"]
        ride = _join(_gather_whole([wbr_s, wo_s, cs_s], 0), _gather_layer1_halves(win_s)) if first else None
        (proj, h), rode = _inproj_fwd(xs[l], norm_g[l:l + 1], win_l, l, ride=ride)
        if first:
            lw[0] = layer_weights(win_f0, *rode[:3])
            mw[0] = mixer_weights(0, lw[0])
            win_f1 = rode[3]
        proj3, x3 = proj.reshape(nb, seq, IN_COLS), xs[l].reshape(nb, seq, D_MODEL)
        if first:
            ride = _join(_forward_layer1_halves(win_f1), _gather_whole([wbr_s, wo_s, cs_s], 1))
            (z3, cacc, mg3, xn3), rode = _mix_merge_fwd(proj3, x3, mw[0], lw[0]["wbr"], lw[0]["wo"], 0, ride=ride)
            lw[1] = layer_weights(*rode)
            mw[1] = mixer_weights(1, lw[1])
            xs.append(xn3.reshape(n, D_MODEL))
        else:
            (z3, cacc, mg3, dx3, loss_part, g_final), _ = _mix_merge_fwd(
                proj3, x3, mw[l], lw[l]["wbr"], lw[l]["wo"], l,
                head=(final_g.reshape(1, D_MODEL), loss_target))
            dx = dx3.reshape(n, D_MODEL)
        caccs.append(cacc)
        projs.append(proj)
        hs.append(h)
        zs.append(z3.reshape(n, N_BRANCH * BRANCH_W))
        mgs.append(mg3.reshape(n, D_MODEL))

    def chip_major(gwbr):
        return gwbr.reshape(N_BRANCH, BRANCH_W, N_CHIPS, BR_SHARD).transpose(2, 0, 1, 3)

    def as_halves(arrays, first_index=0):
        return [g.reshape((N_CHIPS, 2) + s) for g, s in zip(arrays, HALF_SHAPES[first_index:])]

    def add_halves(grads, others, l, first_index=0):
        return [_add_halves(g, o, core, f"add_halves_l{l}_{first_index + i}")
                for i, (g, o) in enumerate(zip(grads, others))]

    per_layer, parts = {}, [None] * DEPTH
    (dz, dproj, gwo, gwbr), _ = _merge_bwd(dx, mgs[1], zs[1], projs[1], lw[1]["wbr"], lw[1]["wo"], 1)
    dproj3, gm, _ = _mixers_bwd(projs[1].reshape(nb, seq, IN_COLS), dz.reshape(nb, seq, N_BRANCH * BRANCH_W), caccs[1],
                                dproj.reshape(nb, seq, IN_COLS), mw[1], 1)
    dproj = dproj3.reshape(n, IN_COLS)
    gwin, _ = _inproj_bwd_w(hs[1], dproj, 1)
    gcs = _pack_cs(_shard_major_rows(gm["conv_w"]), _shard_major_rows(gm["sc_w"]))
    grads1 = as_halves([gwin, chip_major(gwbr), gwo, gcs])
    (dx, g_norm), others1 = _inproj_bwd_x(dproj, lw[1]["win"], xs[1], norm_g[1:2], dx, 1, ride=_swap_halves(grads1))
    gm["norm_g"] = g_norm
    per_layer[1] = gm
    sums1 = add_halves(grads1, others1, 1)
    (dz, dproj, gwo, gwbr), scattered1 = _merge_bwd(dx, mgs[0], zs[0], projs[0], lw[0]["wbr"], lw[0]["wo"], 0,
                                                    ride=_scatter_chip_sums(sums1))
    dproj3, gm, parts[1] = _mixers_bwd(projs[0].reshape(nb, seq, IN_COLS), dz.reshape(nb, seq, N_BRANCH * BRANCH_W),
                                       caccs[0], dproj.reshape(nb, seq, IN_COLS), mw[0], 0,
                                       ride=_share_halves(scattered1))
    dproj = dproj3.reshape(n, IN_COLS)
    gcs = _pack_cs(_shard_major_rows(gm["conv_w"]), _shard_major_rows(gm["sc_w"]))
    small0 = as_halves([chip_major(gwbr), gwo, gcs], first_index=1)

    per_layer[0] = dict(gm, norm_g=jnp.zeros((1, D_MODEL), F32))

    def stack(k, shape):
        return jnp.stack([per_layer[l][k] for l in range(DEPTH)]).reshape(shape)

    g_small = {"final_g": g_final.reshape(D_MODEL), "norm_g": stack("norm_g", (DEPTH, D_MODEL)),
               "pool_w": stack("pool_w", pool_w.shape), "sgu_w": stack("sgu_w", sgu_w.shape),
               "sgu_b": jnp.swapaxes(stack("sgu_b", (DEPTH, CHUNK, GROUP_W))[:, :, :N_GROUPS], 1, 2)}
    for k in ("pool_scale", "conv_b", "conv_ln_g", "conv_ln_b", "sgu_ln_g", "sgu_ln_b"):
        g_small[k] = stack(k, (DEPTH, BRANCH_W))
    small_part = _pack_small(g_small, jnp.broadcast_to(loss_part, (8, GROUP_W)))

    gwin, rode = _inproj_bwd_w(hs[0], dproj, 0, ride=_join(_swap_halves(small0), _swap_whole(small_part)))
    others_small0, small_other = rode[:3], rode[3]
    small_chip = _add2(small_part, small_other, "add_small_sibling")
    (gwin_h,) = as_halves([gwin])

    def flat(a, i):
        rows, cols = 2 * HALF_SHAPES[i][0], HALF_SHAPES[i][1]
        return a.reshape(a.shape[0], rows, cols)

    names4 = ["w_in", "w_branch", "w_o", "conv_sc"]

    def packed4(t):
        return [t["w_in"], t["w_branch"], t["w_o"], _pack_cs(t["conv_w"], t["sc_w"])]

    w4, m4, v4 = ([flat(a, i) for i, a in enumerate(packed4(t))] for t in (w, mom, var))

    def sharded_update(i, layer, prev, after=None):
        return _adamw_sharded(w4[i], m4[i], v4[i], flat(parts[layer][i], i), layer, prev, names4[i], after=after)

    send_sem, recv_sem, gwin_thru, land_thru, token = _swap_halves_start(gwin_h, "swap_halves_l0_w_in")
    updated = [sharded_update(i, 1, None, after=token if i == 0 else None)[0] for i in range(4)]
    gwin_h, other_win0 = _swap_halves_wait(send_sem, recv_sem, gwin_thru, land_thru, updated[0][0],
                                           "swap_halves_l0_w_in")
    sums0 = add_halves([gwin_h], [other_win0], 0) + add_halves(small0, others_small0, 0, first_index=1)
    (dx, g_norm0), rode = _inproj_bwd_x(dproj, lw[0]["win"], xs[0], norm_g[0:1], dx, 0,
                                        ride=_join(_scatter_chip_sums(sums0), _spread_chip_sums(small_chip)))
    scattered0, small_chips = rode[:4], rode[4]
    grad_x = dx.reshape(nb, seq, D_MODEL)

    g_norm0, parts[0] = _allreduce_small(g_norm0.reshape(D_MODEL // GROUP_W, GROUP_W), _share_halves(scattered0))
    g_packed = lax.dynamic_update_slice(_sum_chips(small_chips, "sum_small_chips"), g_norm0, (0, 0))
    loss = g_packed[g_packed.shape[0] - 8, 0]
    d_packed, m_packed, v_packed = _adamw_packed(_pack_small(w), _pack_small(mom), _pack_small(var), g_packed)
    grads, deltas, new_m, new_v = {}, {}, {}, {}
    for tree, packed in ((grads, g_packed), (deltas, d_packed), (new_m, m_packed), (new_v, v_packed)):
        tree.update(_unpack_small(packed, w))

    for i, name in enumerate(names4):
        res, _ = sharded_update(i, 0, updated[i])
        for tree, r in zip((grads, deltas, new_m, new_v), res):
            if name == "conv_sc":
                tree["conv_w"] = r[:, :CONV_K]
                tree["sc_w"] = r[:, CS_ROWS // 2:CS_ROWS // 2 + SHORT_K]
            else:
                tree[name] = r.reshape(w[name].shape)

    return (loss, grad_x, *[grads[k] for k in WEIGHTS], *[deltas[k] for k in WEIGHTS],
            *[new_m[k] for k in WEIGHTS], *[new_v[k] for k in WEIGHTS])
```

```python
import functools

import jax
import jax.numpy as jnp
from jax import lax
from jax.experimental import pallas as pl
from jax.experimental.pallas import tpu as pltpu

F32 = jnp.float32
BF16 = jnp.bfloat16

D_MODEL = 1024
DEPTH = 2
N_BRANCH = 4
BRANCH_W = 512
N_GROUPS = 4
GROUP_W = 128
POOL_WINDOWS = (2, 4, 8, 16)
CONV_K = 31
SHORT_K = 3
CHUNK = 128
N_PIECES = 12
PIECE_COLS = N_PIECES * BRANCH_W
IN_COLS = PIECE_COLS + N_BRANCH * D_MODEL
N_CHIPS = 4
SHARD_COLS = IN_COLS // N_CHIPS
BR_SHARD = D_MODEL // N_CHIPS
COL_TILE = SHARD_COLS // 2
N_COL_TILES = IN_COLS // COL_TILE
GATE_BLOCK0 = PIECE_COLS // D_MODEL
RMS_EPS = 1e-6
LN_EPS = 1e-5
HALO = 32
CONV_ROWS = 32
SHORT_ROWS = 8
CS_ROWS = 64

ADAM_LR = 0.001
ADAM_B1 = 0.9
ADAM_B2 = 0.999
ADAM_EPS = 1e-08
ADAM_WD = 0.01
ADAM_STEP = 10

VMEM_LIMIT = 60 * 1024 * 1024
MESH_ID = pl.DeviceIdType.MESH
ANY = pl.BlockSpec(memory_space=pl.ANY)
VMEM_WHOLE = pl.BlockSpec(memory_space=pltpu.VMEM)

(P_X, P_GATE, C_A, C_B, C_GATE, G_U, G_V, G_GATE, S_B, S_C, S_X, S_GATE) = range(N_PIECES)


def _params(*sem):
    return pltpu.CompilerParams(dimension_semantics=sem, vmem_limit_bytes=VMEM_LIMIT)


def _sigmoid(v):
    return 0.5 * jnp.tanh(0.5 * v) + 0.5


def _silu(v):
    return v * _sigmoid(v)


def _silu_pair(v):
    s = _sigmoid(v)
    return v * s, s * (1.0 + v * (1.0 - s))


def _dot(a, b):
    return jnp.dot(a, b, preferred_element_type=F32)


def _dot_nt(a, b):
    return lax.dot_general(a, b, (((1,), (1,)), ((), ())), preferred_element_type=F32)


def _dot_tn(a, b):
    return lax.dot_general(a, b, (((0,), (0,)), ((), ())), preferred_element_type=F32)


def _rowsum(v):
    return jnp.sum(v, axis=0, keepdims=True)


def _lanemean(v):
    return jnp.mean(v, axis=-1, keepdims=True)


def _tile(n, want, mult=8):
    t = max(1, min(n, want))
    while n % t or (t % mult and t != n):
        t -= 1
    return t


def _place():
    x, y, c = lax.axis_index("x"), lax.axis_index("y"), lax.axis_index("c")
    chip = 2 * x + y
    peers = [(1 - x, y), (x, 1 - y), (1 - x, 1 - y)]
    return x, y, c, chip, peers


def _remote(src, dst, send_sem, recv_sem, dev):
    return pltpu.make_async_remote_copy(src_ref=src, dst_ref=dst, send_sem=send_sem, recv_sem=recv_sem,
                                        device_id=dev, device_id_type=MESH_ID)


class _Exchange:
    def __init__(self, inputs, out_shapes, plan, n_remote, n_local=0, aliases=None):
        self.inputs = list(inputs)
        self.out_shapes = list(out_shapes)
        self.plan = plan
        self.n_remote = n_remote
        self.n_local = n_local
        self.aliases = dict(aliases or {})

    def copies(self, in_refs, out_refs, send_sems, recv_sems, loc_sems):
        remote, local = self.plan(in_refs, out_refs)
        assert len(remote) == self.n_remote and len(local) == self.n_local
        cps = [_remote(s, d, send_sems.at[t], recv_sems.at[t], dev) for t, (s, d, dev) in enumerate(remote)]
        cps += [pltpu.make_async_copy(s, d, loc_sems.at[t]) for t, (s, d) in enumerate(local)]
        return cps

    def sem_shapes(self):
        return [pltpu.SemaphoreType.DMA((max(self.n_remote, 1),)), pltpu.SemaphoreType.DMA((max(self.n_remote, 1),)),
                pltpu.SemaphoreType.DMA((max(self.n_local, 1),))]


def _exchange_call(name, ex):
    n_ci = len(ex.inputs)

    def body(*refs):
        cins, couts = refs[:n_ci], refs[n_ci:n_ci + len(ex.out_shapes)]
        cps = ex.copies(cins, couts, *refs[n_ci + len(ex.out_shapes):])
        for cp in cps:
            cp.start()
        for cp in cps:
            cp.wait()

    return pl.pallas_call(
        body,
        name=name,
        in_specs=[ANY] * n_ci,
        out_specs=[ANY] * len(ex.out_shapes),
        out_shape=ex.out_shapes,
        scratch_shapes=ex.sem_shapes(),
        input_output_aliases=ex.aliases,
    )(*ex.inputs)


def _host_call(body, *, name, grid, in_specs, out_specs, out_shape, args, scratch_shapes=(), aliases=None, ride=None):
    n_in, n_out, n_scr = len(in_specs), len(out_specs), len(scratch_shapes)
    sem = ("arbitrary",) * len(grid)
    aliases = dict(aliases or {})
    if ride is None:
        outs = pl.pallas_call(body, name=name, grid=grid, in_specs=list(in_specs), out_specs=list(out_specs),
                              out_shape=list(out_shape), scratch_shapes=list(scratch_shapes),
                              input_output_aliases=aliases, compiler_params=_params(*sem))(*args)
        return list(outs), []
    n_ci, n_co = len(ride.inputs), len(ride.out_shapes)

    def full_body(*refs):
        ins, cins = refs[:n_in], refs[n_in:n_in + n_ci]
        o0 = n_in + n_ci
        outs, couts = refs[o0:o0 + n_out], refs[o0 + n_out:o0 + n_out + n_co]
        s0 = o0 + n_out + n_co
        scr, sems = refs[s0:s0 + n_scr], refs[s0 + n_scr:]
        first = functools.reduce(jnp.logical_and, [pl.program_id(d) == 0 for d in range(len(grid))])
        last = functools.reduce(jnp.logical_and, [pl.program_id(d) == grid[d] - 1 for d in range(len(grid))])

        @pl.when(first)
        def _():
            for cp in ride.copies(cins, couts, *sems):
                cp.start()

        body(*ins, *outs, *scr)

        @pl.when(last)
        def _():
            for cp in ride.copies(cins, couts, *sems):
                cp.wait()

    for ci, co in ride.aliases.items():
        aliases[n_in + ci] = n_out + co
    outs = pl.pallas_call(
        full_body, name=name, grid=grid, in_specs=list(in_specs) + [ANY] * n_ci,
        out_specs=list(out_specs) + [ANY] * n_co, out_shape=list(out_shape) + ride.out_shapes,
        scratch_shapes=list(scratch_shapes) + ride.sem_shapes(), input_output_aliases=aliases,
        compiler_params=_params(*sem))(*args, *ride.inputs)
    return list(outs[:n_out]), list(outs[n_out:])


def _allgather_layer0(shards):
    na = len(shards)

    def body(*refs):
        ins, outs = refs[:na], refs[na:2 * na]
        send_sems, recv_sems, fsend_sems, frecv_sems, loc_sems = refs[2 * na:]
        x, y, c, chip, peers = _place()
        sib = (x, y, 1 - c)
        locs = [pltpu.make_async_copy(ins[a].at[0], outs[a].at[chip], loc_sems.at[a]) for a in range(na)]
        for cp in locs:
            cp.start()
        pending = []
        for k, (px, py) in enumerate(peers):
            for a in range(na):
                cp = _remote(ins[a].at[0, c], outs[a].at[chip, c], send_sems.at[k * na + a],
                             recv_sems.at[k * na + a], (px, py, c))
                cp.start()
                pending.append(cp)
        for k, (px, py) in enumerate(peers):
            pchip = 2 * px + py
            for a in range(na):
                slab = outs[a].at[pchip, c]
                _remote(slab, slab, send_sems.at[k * na + a], recv_sems.at[k * na + a], (px, py, c)).wait_recv()
                cp = _remote(slab, slab, fsend_sems.at[k * na + a], frecv_sems.at[k * na + a], sib)
                cp.start()
                pending.append(cp)
        for k, (px, py) in enumerate(peers):
            pchip = 2 * px + py
            for a in range(na):
                slab = outs[a].at[pchip, 1 - c]
                _remote(slab, slab, fsend_sems.at[k * na + a], frecv_sems.at[k * na + a], sib).wait_recv()
        for cp in pending:
            cp.wait_send()
        for cp in locs:
            cp.wait()

    return pl.pallas_call(
        body,
        name="allgather_layer0",
        in_specs=[ANY] * na,
        out_specs=[ANY] * na,
        out_shape=[jax.ShapeDtypeStruct((N_CHIPS,) + a.shape[1:], a.dtype) for a in shards],
        scratch_shapes=[pltpu.SemaphoreType.DMA((3 * na,))] * 4 + [pltpu.SemaphoreType.DMA((na,))],
    )(*shards)


def _gather_layer1_halves(win_s):
    def plan(cins, couts):
        _, _, c, chip, peers = _place()
        (src,), (dst,) = cins, couts
        return ([(src.at[1, c], dst.at[chip, c], (px, py, c)) for px, py in peers], [(src.at[1], dst.at[chip])])

    return _Exchange([win_s], [jax.ShapeDtypeStruct((N_CHIPS,) + win_s.shape[1:], win_s.dtype)], plan, 3, 1)


def _forward_layer1_halves(win_f1):
    def plan(cins, couts):
        x, y, c, _, peers = _place()
        (dst,) = couts
        return [(dst.at[2 * px + py, c], dst.at[2 * px + py, c], (x, y, 1 - c)) for px, py in peers], []

    return _Exchange([win_f1], [jax.ShapeDtypeStruct(win_f1.shape, win_f1.dtype)], plan, 3, 0, aliases={0: 0})


def _gather_whole(shards, layer):
    na = len(shards)

    def plan(cins, couts):
        _, _, c, chip, peers = _place()
        remote = [(cins[a].at[layer], couts[a].at[chip], (px, py, c)) for a in range(na) for px, py in peers]
        return remote, [(cins[a].at[layer], couts[a].at[chip]) for a in range(na)]

    return _Exchange(shards, [jax.ShapeDtypeStruct((N_CHIPS,) + a.shape[1:], a.dtype) for a in shards], plan,
                     3 * na, na)


def _join(a, b):
    n_in, n_out = len(a.inputs), len(a.out_shapes)

    def plan(cins, couts):
        remote_a, local_a = a.plan(cins[:n_in], couts[:n_out])
        remote_b, local_b = b.plan(cins[n_in:], couts[n_out:])
        return remote_a + remote_b, local_a + local_b

    aliases = dict(a.aliases)
    aliases.update({n_in + i: n_out + o for i, o in b.aliases.items()})
    return _Exchange(a.inputs + b.inputs, a.out_shapes + b.out_shapes, plan, a.n_remote + b.n_remote,
                     a.n_local + b.n_local, aliases)


def _swap_halves(grads):
    def plan(cins, couts):
        x, y, c, _, _ = _place()
        return [(g.at[:, 1 - c], r, (x, y, 1 - c)) for g, r in zip(cins, couts)], []

    return _Exchange(grads, [jax.ShapeDtypeStruct(g.shape[:1] + g.shape[2:], g.dtype) for g in grads], plan,
                     len(grads))


def _scatter_chip_sums(sums):
    na = len(sums)

    def plan(cins, couts):
        _, _, c, chip, peers = _place()
        remote = [(cins[a].at[2 * px + py], couts[a].at[chip, c], (px, py, c)) for a in range(na) for px, py in peers]
        return remote, [(cins[a].at[chip], couts[a].at[chip, c]) for a in range(na)]

    return _Exchange(sums, [jax.ShapeDtypeStruct((N_CHIPS, 2) + s.shape[1:], s.dtype) for s in sums], plan,
                     3 * na, na)


def _share_halves(parts):
    na = len(parts)

    def plan(cins, couts):
        x, y, c, _, _ = _place()
        return [(p.at[:, c], p.at[:, c], (x, y, 1 - c)) for p in couts], []

    return _Exchange(parts, [jax.ShapeDtypeStruct(p.shape, p.dtype) for p in parts], plan, na, 0,
                     aliases={a: a for a in range(na)})


def _allreduce_small(packed, ride):
    rows = packed.shape[0]
    n_ci, n_co = len(ride.inputs), len(ride.out_shapes)

    def body(*refs):
        sg_ref, cins = refs[0], refs[1:1 + n_ci]
        res_ref, couts = refs[1 + n_ci], refs[2 + n_ci:2 + n_ci + n_co]
        sib_ref, cs_ref, rem_ref, s1_send, s1_recv, s2_send, s2_recv = refs[2 + n_ci + n_co:9 + n_ci + n_co]
        riding = ride.copies(cins, couts, *refs[9 + n_ci + n_co:])
        for cp in riding:
            cp.start()
        x, y, c, chip, peers = _place()
        cp = _remote(sg_ref, sib_ref, s1_send, s1_recv, (x, y, 1 - c))
        cp.start()
        cp.wait()
        cs_ref[...] = sg_ref[...] + sib_ref[...]
        rem_ref[chip] = cs_ref[...]
        cps = [_remote(cs_ref, rem_ref.at[chip], s2_send.at[k], s2_recv.at[chip], (px, py, c))
               for k, (px, py) in enumerate(peers)]
        for cp in cps:
            cp.start()
        for k, (px, py) in enumerate(peers):
            pchip = 2 * px + py
            _remote(cs_ref, rem_ref.at[pchip], s2_send.at[k], s2_recv.at[pchip], (px, py, c)).wait_recv()
        for cp in cps:
            cp.wait_send()
        res_ref[...] = ((rem_ref[0] + rem_ref[1]) + rem_ref[2]) + rem_ref[3]
        for cp in riding:
            cp.wait()

    outs = pl.pallas_call(
        body,
        name="allreduce_small",
        in_specs=[VMEM_WHOLE] + [ANY] * n_ci,
        out_specs=[VMEM_WHOLE] + [ANY] * n_co,
        out_shape=[jax.ShapeDtypeStruct((rows, GROUP_W), F32)] + ride.out_shapes,
        scratch_shapes=[
            pltpu.VMEM((rows, GROUP_W), F32),
            pltpu.VMEM((rows, GROUP_W), F32),
            pltpu.VMEM((N_CHIPS, rows, GROUP_W), F32),
            pltpu.SemaphoreType.DMA,
            pltpu.SemaphoreType.DMA,
            pltpu.SemaphoreType.DMA((3,)),
            pltpu.SemaphoreType.DMA((N_CHIPS,)),
        ] + ride.sem_shapes(),
        input_output_aliases={1 + i: 1 + o for i, o in ride.aliases.items()},
        compiler_params=pltpu.CompilerParams(vmem_limit_bytes=VMEM_LIMIT),
    )(packed, *ride.inputs)
    return outs[0], list(outs[1:])


HBM_SPEC = pl.BlockSpec(memory_space=pltpu.HBM)
SEM_SPEC = pl.BlockSpec(memory_space=pltpu.SEMAPHORE)
SIDE_EFFECT = pltpu.SideEffectType.DATAFLOW_SIDE_EFFECTING


def _swap_halves_start(grad, name):
    land_shape = grad.shape[:1] + grad.shape[2:]

    def body(g_ref, land_ref, send_sem, recv_sem, g_thru, land_thru, token):
        del g_thru, land_thru
        x, y, c, _, _ = _place()
        _remote(g_ref.at[:, 1 - c], land_ref, send_sem, recv_sem, (x, y, 1 - c)).start()
        token[...] = jnp.zeros_like(token)

    return pl.pallas_call(
        body,
        name=name + "_start",
        out_shape=(pltpu.SemaphoreType.DMA(()), pltpu.SemaphoreType.DMA(()), pltpu.HBM(grad.shape, grad.dtype),
                   pltpu.HBM(land_shape, grad.dtype), jax.ShapeDtypeStruct((8, GROUP_W), F32)),
        in_specs=(HBM_SPEC, HBM_SPEC),
        out_specs=(SEM_SPEC, SEM_SPEC, HBM_SPEC, HBM_SPEC, VMEM_WHOLE),
        input_output_aliases={0: 2, 1: 3},
        compiler_params=pltpu.CompilerParams(has_side_effects=SIDE_EFFECT),
    )(pltpu.with_memory_space_constraint(grad, pltpu.HBM),
      pltpu.with_memory_space_constraint(lax.empty(land_shape, grad.dtype), pltpu.HBM))


def _swap_halves_wait(send_sem, recv_sem, g_thru, land_thru, after, name):
    def body(g_ref, land_ref, send_sem, recv_sem, after_ref, g_out, got_ref):
        del after_ref, g_out, got_ref
        x, y, c, _, _ = _place()
        cp = _remote(g_ref.at[:, 1 - c], land_ref, send_sem, recv_sem, (x, y, 1 - c))
        cp.wait_send()
        cp.wait_recv()

    return pl.pallas_call(
        body,
        name=name + "_wait",
        out_shape=(pltpu.HBM(g_thru.shape, g_thru.dtype), pltpu.HBM(land_thru.shape, land_thru.dtype)),
        in_specs=(HBM_SPEC, HBM_SPEC, SEM_SPEC, SEM_SPEC, ANY),
        out_specs=(HBM_SPEC, HBM_SPEC),
        input_output_aliases={0: 0, 1: 1},
        compiler_params=pltpu.CompilerParams(has_side_effects=SIDE_EFFECT),
    )(g_thru, land_thru, send_sem, recv_sem, after)


def _swap_whole(packed):
    def plan(cins, couts):
        x, y, c, _, _ = _place()
        return [(cins[0], couts[0], (x, y, 1 - c))], []

    return _Exchange([packed], [jax.ShapeDtypeStruct(packed.shape, packed.dtype)], plan, 1)


def _spread_chip_sums(chip_sum):
    def plan(cins, couts):
        _, _, c, chip, peers = _place()
        return ([(cins[0], couts[0].at[chip], (px, py, c)) for px, py in peers], [(cins[0], couts[0].at[chip])])

    return _Exchange([chip_sum], [jax.ShapeDtypeStruct((N_CHIPS,) + chip_sum.shape, chip_sum.dtype)], plan, 3, 1)


def _add2(a, b, name):
    def body(a_ref, b_ref, o_ref):
        o_ref[...] = a_ref[...] + b_ref[...]

    return pl.pallas_call(body, name=name, in_specs=[VMEM_WHOLE] * 2, out_specs=VMEM_WHOLE,
                          out_shape=jax.ShapeDtypeStruct(a.shape, F32))(a, b)


def _sum_chips(parts, name):
    def body(p_ref, o_ref):
        o_ref[...] = ((p_ref[0] + p_ref[1]) + p_ref[2]) + p_ref[3]

    return pl.pallas_call(body, name=name, in_specs=[VMEM_WHOLE], out_specs=VMEM_WHOLE,
                          out_shape=jax.ShapeDtypeStruct(parts.shape[1:], F32))(parts)


def _add_halves(grad, other, core, name):
    n_shards, _, rows, cols = grad.shape
    tr = _tile(rows, max(16, (1 << 22) // (4 * cols) // 16 * 16), mult=16)

    def body(core_ref, g_ref, o_ref, out_ref):
        del core_ref
        out_ref[...] = (g_ref[...] + o_ref[...]).astype(BF16)

    return pl.pallas_call(
        body,
        name=name,
        grid_spec=pltpu.PrefetchScalarGridSpec(
            num_scalar_prefetch=1,
            grid=(n_shards, rows // tr),
            in_specs=[
                pl.BlockSpec((None, None, tr, cols), lambda s, i, core_ref: (s, core_ref[0], i, 0)),
                pl.BlockSpec((None, tr, cols), lambda s, i, core_ref: (s, i, 0)),
            ],
            out_specs=pl.BlockSpec((None, tr, cols), lambda s, i, core_ref: (s, i, 0)),
        ),
        out_shape=jax.ShapeDtypeStruct((n_shards, rows, cols), BF16),
        compiler_params=_params("arbitrary", "arbitrary"),
    )(core, grad, other)


def _inproj_fwd(x2, g_row, w_l, layer, ride=None):
    n = x2.shape[0]
    tm = _tile(n, 2048)

    def body(x_ref, g_ref, w_ref, p_ref, h_ref):
        @pl.when(pl.program_id(1) == 0)
        def _():
            xv = x_ref[...]
            r = lax.rsqrt(_lanemean(xv * xv) + RMS_EPS)
            h_ref[...] = (xv * r * g_ref[...]).astype(BF16)

        p_ref[...] = _dot(h_ref[...], w_ref[...]).astype(BF16)

    return _host_call(
        body,
        name=f"inproj_fwd_l{layer}",
        grid=(n // tm, N_COL_TILES),
        in_specs=[
            pl.BlockSpec((tm, D_MODEL), lambda i, j: (i, 0)),
            pl.BlockSpec((1, D_MODEL), lambda i, j: (0, 0)),
            pl.BlockSpec((None, D_MODEL, COL_TILE), lambda i, j: (j // 2, 0, j % 2)),
        ],
        out_specs=[
            pl.BlockSpec((tm, COL_TILE), lambda i, j: (i, j)),
            pl.BlockSpec((tm, D_MODEL), lambda i, j: (i, 0)),
        ],
        out_shape=[jax.ShapeDtypeStruct((n, IN_COLS), BF16), jax.ShapeDtypeStruct((n, D_MODEL), BF16)],
        args=[x2, g_row, w_l],
        ride=ride,
    )


def _layer_norm_parts(v, g, b):
    mu = _lanemean(v)
    d = v - mu
    rstd = lax.rsqrt(_lanemean(d * d) + LN_EPS)
    xh = d * rstd
    return xh, rstd, xh * g + b


def _window_sum(ref, slot, weight_row, n_taps, base, t_rows):
    total = None
    for b in range(8):
        group = None
        for a in range((base + n_taps - 1) // 8 + 1):
            k = 8 * a + b - base
            if 0 <= k < n_taps:
                term = weight_row(k) * ref[slot, 8 * a:8 * a + t_rows + 8, :]
                group = term if group is None else group + term
        if group is not None:
            part = group[b:b + t_rows, :]
            total = part if total is None else total + part
    return total


def _tap_grads(acc_ref, pad_ref, shift_ref, dy, ref, slot, n_taps, base, t_rows):
    pad_ref[0:8, :] = jnp.zeros((8, BRANCH_W), F32)
    pad_ref[8:8 + t_rows, :] = dy
    pad_ref[8 + t_rows:16 + t_rows, :] = jnp.zeros((8, BRANCH_W), F32)
    for b in range(8):
        taps = [(a, 8 * a + b - base) for a in range((base + n_taps - 1) // 8 + 1) if 0 <= 8 * a + b - base < n_taps]
        if not taps:
            continue
        shift_ref[...] = pad_ref[pl.ds(8 - b, t_rows + 8), :]
        for a, k in taps:
            prod = shift_ref[...] * ref[slot, 8 * a:8 * a + t_rows + 8, :]
            acc_ref[k] += prod.reshape((t_rows + 8) // 8, 8, BRANCH_W).sum(axis=0)


def _trailing_window_sums(ext_ref, pool_ref, t_rows):
    lo, n = HALO - 16, t_rows + 16
    pool_ref[0:16, :] = jnp.zeros((16, BRANCH_W), F32)
    sums = []
    for j, win in enumerate(POOL_WINDOWS):
        cols = slice(j * GROUP_W, (j + 1) * GROUP_W)
        v = ext_ref[0, lo:lo + n, cols] + ext_ref[0, pl.ds(lo - 1, n), cols]
        width = 2
        while width < win:
            pool_ref[16:16 + n, cols] = v
            v = v + pool_ref[pl.ds(16 - width, n), cols]
            width *= 2
        sums.append(v[16:, :])
    return sums


def _leading_window_sums(nxt_ref, pool_ref, t_rows):
    n = t_rows + 16
    pool_ref[n:n + 16, :] = jnp.zeros((16, BRANCH_W), F32)
    sums = []
    for j, win in enumerate(POOL_WINDOWS):
        cols = slice(j * GROUP_W, (j + 1) * GROUP_W)
        v = nxt_ref[0, 0:n, cols] + nxt_ref[0, pl.ds(1, n), cols]
        width = 2
        while width < win:
            pool_ref[0:n, cols] = v
            v = v + pool_ref[pl.ds(width, n), cols]
            width *= 2
        sums.append(v[0:t_rows, :])
    return sums


def _tril_mask():
    r = lax.broadcasted_iota(jnp.int32, (CHUNK, CHUNK), 0)
    c = lax.broadcasted_iota(jnp.int32, (CHUNK, CHUNK), 1)
    return r >= c


def _lanes(refs4):
    return jnp.concatenate([refs4[g] for g in range(N_CHIPS)], axis=1)


def _mixer_weight_specs():
    def whole(shape):
        nd = len(shape)
        return pl.BlockSpec(shape, lambda b, c: (0,) * nd)

    return [
        whole((N_GROUPS, GROUP_W, GROUP_W)),
        whole((1, BRANCH_W)),
        whole((N_CHIPS, CONV_ROWS, GROUP_W)),
        whole((1, BRANCH_W)),
        whole((1, BRANCH_W)),
        whole((1, BRANCH_W)),
        whole((1, BRANCH_W)),
        whole((1, BRANCH_W)),
        whole((N_GROUPS, CHUNK, CHUNK)),
        whole((CHUNK, BRANCH_W)),
        whole((N_CHIPS, SHORT_ROWS, GROUP_W)),
    ]


def _mixer_weight_args(mw):
    return [mw["pool_w"], mw["pool_scale"], mw["conv_w"], mw["conv_b"], mw["conv_ln_g"], mw["conv_ln_b"],
            mw["sgu_ln_g"], mw["sgu_ln_b"], mw["sgu_w"], mw["sgu_bias"], mw["sc_w"]]


def _mix_merge_fwd(proj3, x3, mw, wbr_l, wo_l, layer, ride=None, head=None):
    nb, seq, _ = proj3.shape
    t_rows = _tile(seq, 512)
    nc = seq // t_rows
    hb = t_rows // HALO
    n_head = 2 if head else 0

    def body(*refs):
        (cur_ref, halo_ref, g01_ref, g23_ref, x_ref, pw_ref, ps_ref, cw_ref, cb_ref, clg_ref, clb_ref, slg_ref,
         slb_ref, sw_ref, sbias_ref, scw_ref, wbr_ref, wo_ref) = refs[:18]
        head_in = refs[18:18 + n_head]
        outs = refs[18 + n_head:-2]
        ext_ref, pool_ref = refs[-2:]
        z_ref, cacc_ref, mg_ref, xn_ref = outs[:4]
        b = pl.program_id(0)
        c = pl.program_id(1)
        has_prev = c > 0
        row = lax.broadcasted_iota(jnp.int32, (t_rows, 1), 0)
        tpos = (c * t_rows + row + 1).astype(F32)

        def cur(k):
            return cur_ref[:, k * BRANCH_W:(k + 1) * BRANCH_W].astype(F32)

        def hal(k):
            return halo_ref[:, k * BRANCH_W:(k + 1) * BRANCH_W].astype(F32)

        def put_ext(slot, halo_val, cur_val):
            ext_ref[slot, 0:HALO, :] = jnp.where(has_prev, halo_val, 0.0)
            ext_ref[slot, HALO:HALO + t_rows, :] = cur_val
            ext_ref[slot, HALO + t_rows:HALO + t_rows + 8, :] = jnp.zeros((8, BRANCH_W), F32)

        def gated(branch, z_val):
            zb = z_val.astype(BF16)
            z_ref[:, branch * BRANCH_W:(branch + 1) * BRANCH_W] = zb
            gate_ref = g01_ref if branch < 2 else g23_ref
            gate = gate_ref[:, (branch % 2) * D_MODEL:(branch % 2 + 1) * D_MODEL].astype(F32)
            return _sigmoid(gate) * _dot(zb, wbr_ref[branch])

        px = cur(P_X)
        put_ext(0, hal(P_X), px)
        mixed = []
        window_sums = _trailing_window_sums(ext_ref, pool_ref, t_rows)
        for j, win in enumerate(POOL_WINDOWS):
            cols = slice(j * GROUP_W, (j + 1) * GROUP_W)
            pooled = window_sums[j] / jnp.minimum(tpos, float(win)) - px[:, cols]
            mixed.append(_dot(pooled.astype(BF16), pw_ref[j]))
        acc = gated(0, jnp.concatenate(mixed, axis=1) * ps_ref[...] * _silu(cur(P_GATE)))

        put_ext(1, hal(C_A) * _sigmoid(hal(C_B)), cur(C_A) * _sigmoid(cur(C_B)))
        cw = _lanes(cw_ref)
        conv = cb_ref[...] + _window_sum(ext_ref, 1, lambda k: cw[k:k + 1, :], CONV_K, HALO - (CONV_K - 1), t_rows)
        cacc_ref[...] = conv
        _, _, ln = _layer_norm_parts(conv, clg_ref[...], clb_ref[...])
        acc = acc + gated(1, _silu(ln) * _silu(cur(C_GATE)))

        _, _, v = _layer_norm_parts(cur(G_V), slg_ref[...], slb_ref[...])
        vb = v.astype(BF16)
        mask = _tril_mask()
        wt = [jnp.where(mask, sw_ref[g], 0.0).astype(BF16) for g in range(N_GROUPS)]
        sp_rows = []
        for sub in range(t_rows // CHUNK):
            rows = slice(sub * CHUNK, (sub + 1) * CHUNK)
            sp_rows.append(jnp.concatenate(
                [_dot(wt[g], vb[rows, g * GROUP_W:(g + 1) * GROUP_W]) for g in range(N_GROUPS)], axis=1)
                + sbias_ref[...])
        sp = jnp.concatenate(sp_rows, axis=0)
        acc = acc + gated(2, cur(G_U) * sp * _silu(cur(G_GATE)))

        put_ext(2, hal(S_C) * hal(S_X), cur(S_C) * cur(S_X))
        scw = _lanes(scw_ref)
        cv = _window_sum(ext_ref, 2, lambda k: scw[k:k + 1, :], SHORT_K, HALO - (SHORT_K - 1), t_rows)
        acc = acc + gated(3, cur(S_B) * cv * _silu(cur(S_GATE)))

        mg = acc.astype(BF16)
        mg_ref[...] = mg
        xv = x_ref[...] + _dot(mg, wo_ref[...])
        if not head:
            xn_ref[...] = xv
            return
        g_ref, t_ref = head_in
        loss_ref, dg_ref = outs[4:]

        @pl.when(jnp.logical_and(b == 0, c == 0))
        def _():
            loss_ref[...] = jnp.zeros_like(loss_ref)
            dg_ref[...] = jnp.zeros_like(dg_ref)

        g = g_ref[...]
        r = lax.rsqrt(_lanemean(xv * xv) + RMS_EPS)
        xh = xv * r
        err = xh * g - t_ref[...]
        loss_ref[...] += 0.5 * jnp.sum(_lanemean(err * err), axis=0, keepdims=True)
        dy = err * (1.0 / D_MODEL)
        dg_ref[...] += _rowsum(dy * xh)
        dxh = dy * g
        xn_ref[...] = r * (dxh - xh * _lanemean(dxh * xh))

    def rows_spec(cols, col_block=0):
        return pl.BlockSpec((None, t_rows, cols), lambda b, c: (b, c, col_block))

    def whole(shape):
        nd = len(shape)
        return pl.BlockSpec(shape, lambda b, c: (0,) * nd)

    in_specs = [
        rows_spec(PIECE_COLS),
        pl.BlockSpec((None, HALO, PIECE_COLS), lambda b, c: (b, jnp.maximum(c * hb - 1, 0), 0)),
        rows_spec(2 * D_MODEL, PIECE_COLS // (2 * D_MODEL)),
        rows_spec(2 * D_MODEL, PIECE_COLS // (2 * D_MODEL) + 1),
        rows_spec(D_MODEL),
    ] + _mixer_weight_specs() + [whole((N_BRANCH, BRANCH_W, D_MODEL)), whole((D_MODEL, D_MODEL))]
    args = [proj3, proj3, proj3, proj3, x3, *_mixer_weight_args(mw), wbr_l, wo_l]
    out_specs = [rows_spec(N_BRANCH * BRANCH_W), rows_spec(BRANCH_W), rows_spec(D_MODEL), rows_spec(D_MODEL)]
    out_shape = [jax.ShapeDtypeStruct((nb, seq, N_BRANCH * BRANCH_W), BF16), jax.ShapeDtypeStruct((nb, seq, BRANCH_W), F32),
                 jax.ShapeDtypeStruct((nb, seq, D_MODEL), BF16), jax.ShapeDtypeStruct((nb, seq, D_MODEL), F32)]
    if head:
        in_specs += [whole((1, D_MODEL)), rows_spec(D_MODEL)]
        args += list(head)
        out_specs += [whole((1, GROUP_W)), whole((1, D_MODEL))]
        out_shape += [jax.ShapeDtypeStruct((1, GROUP_W), F32), jax.ShapeDtypeStruct((1, D_MODEL), F32)]
    return _host_call(
        body,
        name=f"mix_merge_fwd_l{layer}",
        grid=(nb, nc),
        in_specs=in_specs,
        out_specs=out_specs,
        out_shape=out_shape,
        scratch_shapes=[pltpu.VMEM((3, HALO + t_rows + 8, BRANCH_W), F32),
                        pltpu.VMEM((t_rows + 32, BRANCH_W), F32)],
        args=args,
        ride=ride,
    )


MIXER_GRADS = ["pool_w", "pool_scale", "conv_w", "conv_b", "conv_ln_g", "conv_ln_b", "sgu_ln_g", "sgu_ln_b", "sgu_w",
               "sgu_b", "sc_w"]


def _mixers_bwd(proj3, dz3, cacc3, dproj3, mw, layer, ride=None):
    nb, seq, _ = proj3.shape
    t_rows = _tile(seq, 512)
    nc = seq // t_rows
    hb = t_rows // HALO

    def body(cur_ref, halo_ref, dz_ref, cacc_ref, pw_ref, ps_ref, cw_ref, cb_ref, clg_ref, clb_ref, slg_ref, slb_ref,
             sw_ref, sbias_ref, scw_ref, dp_in_ref, dp_ref, g_pw, g_ps, g_cw, g_cb, g_clg, g_clb, g_slg,
             g_slb, g_sw, g_sb, g_scw, ext_ref, nxt_ref, sb_acc, cw_acc, scw_acc, pad_ref, shift_ref,
             pool_ref):
        del dp_in_ref
        b = pl.program_id(0)
        r = pl.program_id(1)
        c = nc - 1 - r
        has_prev = c > 0
        row = lax.broadcasted_iota(jnp.int32, (t_rows, 1), 0)
        tpos = (c * t_rows + row + 1).astype(F32)

        @pl.when(jnp.logical_and(b == 0, r == 0))
        def _():
            for ref in (g_pw, g_ps, g_cb, g_clg, g_clb, g_slg, g_slb, g_sw, sb_acc, cw_acc, scw_acc):
                ref[...] = jnp.zeros_like(ref)

        @pl.when(r == 0)
        def _():
            nxt_ref[:, t_rows:t_rows + HALO, :] = jnp.zeros((3, HALO, BRANCH_W), F32)

        def cur(k):
            return cur_ref[:, k * BRANCH_W:(k + 1) * BRANCH_W].astype(F32)

        def hal(k):
            return halo_ref[:, k * BRANCH_W:(k + 1) * BRANCH_W].astype(F32)

        def dzp(k):
            return dz_ref[:, k * BRANCH_W:(k + 1) * BRANCH_W].astype(F32)

        def put_dp(k, val):
            dp_ref[:, k * BRANCH_W:(k + 1) * BRANCH_W] = val.astype(BF16)

        def put_ext(slot, halo_val, cur_val):
            ext_ref[slot, 0:HALO, :] = jnp.where(has_prev, halo_val, 0.0)
            ext_ref[slot, HALO:HALO + t_rows, :] = cur_val
            ext_ref[slot, HALO + t_rows:HALO + t_rows + 8, :] = jnp.zeros((8, BRANCH_W), F32)

        px = cur(P_X)
        put_ext(0, hal(P_X), px)
        pgate = cur(P_GATE)
        dz_pool = dzp(0)
        ps = ps_ref[...]
        pooled, mixed, cnts = [], [], []
        window_sums = _trailing_window_sums(ext_ref, pool_ref, t_rows)
        for j, win in enumerate(POOL_WINDOWS):
            cols = slice(j * GROUP_W, (j + 1) * GROUP_W)
            cnt = jnp.minimum(tpos, float(win))
            pj = window_sums[j] / cnt - px[:, cols]
            cnts.append(cnt)
            pooled.append(pj.astype(BF16))
            mixed.append(_dot(pooled[j], pw_ref[j]))
        mixed = jnp.concatenate(mixed, axis=1)
        pg_silu, pg_dsilu = _silu_pair(pgate)
        put_dp(P_GATE, dz_pool * (mixed * ps) * pg_dsilu)
        d_out = dz_pool * pg_silu
        g_ps[...] += _rowsum(d_out * mixed)
        d_mixed = (d_out * ps).astype(BF16)
        d_pooled = []
        for j in range(N_GROUPS):
            cols = slice(j * GROUP_W, (j + 1) * GROUP_W)
            g_pw[j] += _dot_tn(pooled[j], d_mixed[:, cols])
            dpj = _dot_nt(d_mixed[:, cols], pw_ref[j])
            d_pooled.append(dpj)
            nxt_ref[0, 0:t_rows, cols] = dpj / cnts[j]
        lead_sums = _leading_window_sums(nxt_ref, pool_ref, t_rows)
        put_dp(P_X, jnp.concatenate([lead_sums[j] - d_pooled[j] for j in range(N_GROUPS)], axis=1))

        c_a = cur(C_A)
        sig_b = _sigmoid(cur(C_B))
        put_ext(1, hal(C_A) * _sigmoid(hal(C_B)), c_a * sig_b)
        cw = _lanes(cw_ref)
        xh, rstd, ln = _layer_norm_parts(cacc_ref[...], clg_ref[...], clb_ref[...])
        cgate = cur(C_GATE)
        dz_conv = dzp(1)
        ln_silu, ln_dsilu = _silu_pair(ln)
        cg_silu, cg_dsilu = _silu_pair(cgate)
        put_dp(C_GATE, dz_conv * ln_silu * cg_dsilu)
        d_ln = dz_conv * cg_silu * ln_dsilu
        g_clg[...] += _rowsum(d_ln * xh)
        g_clb[...] += _rowsum(d_ln)
        dxh = d_ln * clg_ref[...]
        dc = rstd * (dxh - _lanemean(dxh) - xh * _lanemean(dxh * xh))
        g_cb[...] += _rowsum(dc)
        nxt_ref[1, 0:t_rows, :] = dc
        _tap_grads(cw_acc, pad_ref, shift_ref, dc, ext_ref, 1, CONV_K, HALO - (CONV_K - 1), t_rows)
        dyg = _window_sum(nxt_ref, 1, lambda i: cw[CONV_K - 1 - i:CONV_K - i, :], CONV_K, 0, t_rows)
        put_dp(C_A, dyg * sig_b)
        put_dp(C_B, dyg * c_a * sig_b * (1.0 - sig_b))

        u = cur(G_U)
        ggate = cur(G_GATE)
        vxh, vrstd, v = _layer_norm_parts(cur(G_V), slg_ref[...], slb_ref[...])
        vb = v.astype(BF16)
        mask = _tril_mask()
        wt = [jnp.where(mask, sw_ref[g], 0.0) for g in range(N_GROUPS)]
        wt_b = [w.astype(BF16) for w in wt]
        wtt_b = [w.T.astype(BF16) for w in wt]
        dz_sgu = dzp(2)
        gg_silu, gg_dsilu = _silu_pair(ggate)
        d_sgu = dz_sgu * gg_silu
        d_sp = d_sgu * u
        d_spb = d_sp.astype(BF16)
        sp_rows, dv_rows = [], []
        for sub in range(t_rows // CHUNK):
            rows = slice(sub * CHUNK, (sub + 1) * CHUNK)
            sp_g, dv_g = [], []
            for g in range(N_GROUPS):
                cols = slice(g * GROUP_W, (g + 1) * GROUP_W)
                sp_g.append(_dot(wt_b[g], vb[rows, cols]))
                g_sw[g] += jnp.where(mask, _dot_nt(d_spb[rows, cols], vb[rows, cols]), 0.0)
                dv_g.append(_dot(wtt_b[g], d_spb[rows, cols]))
            sp_rows.append(jnp.concatenate(sp_g, axis=1) + sbias_ref[...])
            dv_rows.append(jnp.concatenate(dv_g, axis=1))
            sb_acc[...] += d_sp[rows, :]
        sp = jnp.concatenate(sp_rows, axis=0)
        dv = jnp.concatenate(dv_rows, axis=0)
        put_dp(G_GATE, dz_sgu * (u * sp) * gg_dsilu)
        put_dp(G_U, d_sgu * sp)
        g_slg[...] += _rowsum(dv * vxh)
        g_slb[...] += _rowsum(dv)
        dvx = dv * slg_ref[...]
        put_dp(G_V, vrstd * (dvx - _lanemean(dvx) - vxh * _lanemean(dvx * vxh)))

        s_b, s_c, s_x, sgate = cur(S_B), cur(S_C), cur(S_X), cur(S_GATE)
        put_ext(2, hal(S_C) * hal(S_X), s_c * s_x)
        scw = _lanes(scw_ref)
        cv = _window_sum(ext_ref, 2, lambda k: scw[k:k + 1, :], SHORT_K, HALO - (SHORT_K - 1), t_rows)
        dz_sc = dzp(3)
        sg_silu, sg_dsilu = _silu_pair(sgate)
        put_dp(S_GATE, dz_sc * (s_b * cv) * sg_dsilu)
        d_pre = dz_sc * sg_silu
        put_dp(S_B, d_pre * cv)
        dcv = d_pre * s_b
        nxt_ref[2, 0:t_rows, :] = dcv
        _tap_grads(scw_acc, pad_ref, shift_ref, dcv, ext_ref, 2, SHORT_K, HALO - (SHORT_K - 1), t_rows)
        du = _window_sum(nxt_ref, 2, lambda i: scw[SHORT_K - 1 - i:SHORT_K - i, :], SHORT_K, 0, t_rows)
        put_dp(S_C, du * s_x)
        put_dp(S_X, du * s_c)

        nxt_ref[:, t_rows:t_rows + HALO, :] = nxt_ref[:, 0:HALO, :]

        @pl.when(jnp.logical_and(b == nb - 1, r == nc - 1))
        def _():
            lane = lax.broadcasted_iota(jnp.int32, (CHUNK, GROUP_W), 1)
            out = jnp.zeros((CHUNK, GROUP_W), F32)
            for g in range(N_GROUPS):
                col = jnp.sum(sb_acc[:, g * GROUP_W:(g + 1) * GROUP_W], axis=1, keepdims=True)
                out = jnp.where(lane == g, col, out)
            g_sb[...] = out
            g_cw[...] = jnp.sum(cw_acc[...], axis=1)
            g_scw[...] = jnp.sum(scw_acc[...], axis=1)

    def acc_spec(shape):
        nd = len(shape)
        return pl.BlockSpec(shape, lambda b, r: (0,) * nd)

    acc_shapes = [
        (N_GROUPS, GROUP_W, GROUP_W),
        (1, BRANCH_W),
        (CONV_ROWS, BRANCH_W),
        (1, BRANCH_W),
        (1, BRANCH_W),
        (1, BRANCH_W),
        (1, BRANCH_W),
        (1, BRANCH_W),
        (N_GROUPS, CHUNK, CHUNK),
        (CHUNK, GROUP_W),
        (SHORT_ROWS, BRANCH_W),
    ]
    outs, rode = _host_call(
        body,
        name=f"mixers_bwd_l{layer}",
        grid=(nb, nc),
        in_specs=[
            pl.BlockSpec((None, t_rows, PIECE_COLS), lambda b, r: (b, nc - 1 - r, 0)),
            pl.BlockSpec((None, HALO, PIECE_COLS), lambda b, r: (b, jnp.maximum((nc - 1 - r) * hb - 1, 0), 0)),
            pl.BlockSpec((None, t_rows, N_BRANCH * BRANCH_W), lambda b, r: (b, nc - 1 - r, 0)),
            pl.BlockSpec((None, t_rows, BRANCH_W), lambda b, r: (b, nc - 1 - r, 0)),
        ] + _mixer_weight_specs() + [ANY],
        out_specs=[pl.BlockSpec((None, t_rows, PIECE_COLS), lambda b, r: (b, nc - 1 - r, 0))]
        + [acc_spec(s) for s in acc_shapes],
        out_shape=[jax.ShapeDtypeStruct(dproj3.shape, BF16)] + [jax.ShapeDtypeStruct(s, F32) for s in acc_shapes],
        scratch_shapes=[
            pltpu.VMEM((3, HALO + t_rows + 8, BRANCH_W), F32),
            pltpu.VMEM((3, t_rows + HALO, BRANCH_W), F32),
            pltpu.VMEM((CHUNK, BRANCH_W), F32),
            pltpu.VMEM((CONV_ROWS, 8, BRANCH_W), F32),
            pltpu.VMEM((SHORT_ROWS, 8, BRANCH_W), F32),
            pltpu.VMEM((t_rows + 16, BRANCH_W), F32),
            pltpu.VMEM((t_rows + 8, BRANCH_W), F32),
            pltpu.VMEM((t_rows + 32, BRANCH_W), F32),
        ],
        aliases={4 + 11: 0},
        args=[proj3, proj3, dz3, cacc3, *_mixer_weight_args(mw), dproj3],
        ride=ride,
    )
    return outs[0], dict(zip(MIXER_GRADS, outs[1:])), rode


def _merge_bwd(dout2, merged2, z2, proj2, wbr_l, wo_l, layer, ride=None):
    n = dout2.shape[0]
    tm = _tile(n, 1024)
    nt = n // tm

    def body(do_ref, mg_ref, z_ref, gate_ref, wbr_ref, wo_ref, dz_ref, dg_ref, gwo_ref, gwbr_ref, dm_ref,
             awo_ref, awbr_ref):
        i = pl.program_id(0)
        nbr = pl.program_id(1)

        @pl.when(nbr == 0)
        def _():
            do_b = do_ref[...].astype(BF16)
            dm_ref[...] = _dot_nt(do_b, wo_ref[...])
            gw = _dot_tn(mg_ref[...], do_b)

            @pl.when(i == 0)
            def _():
                awo_ref[...] = gw

            @pl.when(i > 0)
            def _():
                awo_ref[...] += gw

        zt = z_ref[...]
        wbr = wbr_ref[nbr]
        bo = _dot(zt, wbr)
        gt = _sigmoid(gate_ref[...].astype(F32))
        dm = dm_ref[...]
        dbo = (dm * gt).astype(BF16)
        dg_ref[...] = (dm * bo * gt * (1.0 - gt)).astype(BF16)
        dz_ref[...] = _dot_nt(dbo, wbr).astype(BF16)
        gw = _dot_tn(zt, dbo)

        @pl.when(i == 0)
        def _():
            awbr_ref[nbr] = gw

        @pl.when(i > 0)
        def _():
            awbr_ref[nbr] += gw

        @pl.when(jnp.logical_and(i == nt - 1, nbr == N_BRANCH - 1))
        def _():
            pltpu.sync_copy(awo_ref, gwo_ref)
            pltpu.sync_copy(awbr_ref, gwbr_ref)

    return _host_call(
        body,
        name=f"merge_bwd_l{layer}",
        grid=(nt, N_BRANCH),
        in_specs=[
            pl.BlockSpec((tm, D_MODEL), lambda i, b: (i, 0)),
            pl.BlockSpec((tm, D_MODEL), lambda i, b: (i, 0)),
            pl.BlockSpec((tm, BRANCH_W), lambda i, b: (i, b)),
            pl.BlockSpec((tm, D_MODEL), lambda i, b: (i, GATE_BLOCK0 + b)),
            pl.BlockSpec((N_BRANCH, BRANCH_W, D_MODEL), lambda i, b: (0, 0, 0)),
            pl.BlockSpec((D_MODEL, D_MODEL), lambda i, b: (0, 0)),
        ],
        out_specs=[
            pl.BlockSpec((tm, BRANCH_W), lambda i, b: (i, b)),
            pl.BlockSpec((tm, D_MODEL), lambda i, b: (i, GATE_BLOCK0 + b)),
            ANY,
            ANY,
        ],
        out_shape=[
            jax.ShapeDtypeStruct((n, N_BRANCH * BRANCH_W), BF16),
            jax.ShapeDtypeStruct((n, IN_COLS), BF16),
            jax.ShapeDtypeStruct((D_MODEL, D_MODEL), F32),
            jax.ShapeDtypeStruct((N_BRANCH, BRANCH_W, D_MODEL), F32),
        ],
        scratch_shapes=[
            pltpu.VMEM((tm, D_MODEL), F32),
            pltpu.VMEM((D_MODEL, D_MODEL), F32),
            pltpu.VMEM((N_BRANCH, BRANCH_W, D_MODEL), F32),
        ],
        args=[dout2, merged2, z2, proj2, wbr_l, wo_l],
        ride=ride,
    )


def _inproj_bwd_x(dproj2, w_l, x2, g_row, dout2, layer, ride=None):
    n = x2.shape[0]
    tm = _tile(n, 1024)

    def body(dp_ref, w_ref, x_ref, g_ref, do_ref, dx_ref, dng_ref, dh_ref):
        i = pl.program_id(0)
        s = pl.program_id(1)
        part = _dot_nt(dp_ref[...], w_ref[...])

        @pl.when(s == 0)
        def _():
            dh_ref[...] = part

        @pl.when(s > 0)
        def _():
            dh_ref[...] += part

        @pl.when(jnp.logical_and(i == 0, s == 0))
        def _():
            dng_ref[...] = jnp.zeros_like(dng_ref)

        @pl.when(s == N_CHIPS - 1)
        def _():
            xv = x_ref[...]
            r = lax.rsqrt(_lanemean(xv * xv) + RMS_EPS)
            xh = xv * r
            dh = dh_ref[...]
            dng_ref[...] += _rowsum(dh * xh)
            dxh = dh * g_ref[...]
            dx_ref[...] = do_ref[...] + r * (dxh - xh * _lanemean(dxh * xh))

    return _host_call(
        body,
        name=f"inproj_bwd_x_l{layer}",
        grid=(n // tm, N_CHIPS),
        in_specs=[
            pl.BlockSpec((tm, SHARD_COLS), lambda i, s: (i, s)),
            pl.BlockSpec((None, D_MODEL, SHARD_COLS), lambda i, s: (s, 0, 0)),
            pl.BlockSpec((tm, D_MODEL), lambda i, s: (i, 0)),
            pl.BlockSpec((1, D_MODEL), lambda i, s: (0, 0)),
            pl.BlockSpec((tm, D_MODEL), lambda i, s: (i, 0)),
        ],
        out_specs=[
            pl.BlockSpec((tm, D_MODEL), lambda i, s: (i, 0)),
            pl.BlockSpec((1, D_MODEL), lambda i, s: (0, 0)),
        ],
        out_shape=[jax.ShapeDtypeStruct((n, D_MODEL), F32), jax.ShapeDtypeStruct((1, D_MODEL), F32)],
        scratch_shapes=[pltpu.VMEM((tm, D_MODEL), F32)],
        args=[dproj2, w_l, x2, g_row, dout2],
        ride=ride,
    )


def _inproj_bwd_w(h2, dproj2, layer, ride=None):
    n = h2.shape[0]
    tm = _tile(n, 2048)
    ni = n // tm
    total = N_COL_TILES * ni
    slots = 3

    def body(h_ref, dp_hbm, gw_ref, ring_ref, ring_sems):
        i = pl.program_id(1)
        t = pl.program_id(0) * ni + i

        def fetch(step):
            rows = pl.ds(pl.multiple_of((step % ni) * tm, tm), tm)
            cols = pl.ds(pl.multiple_of((step // ni) * COL_TILE, COL_TILE), COL_TILE)
            slot = step % slots
            return pltpu.make_async_copy(dp_hbm.at[rows, cols], ring_ref.at[slot], ring_sems.at[slot])

        @pl.when(t == 0)
        def _():
            fetch(t).start()
            fetch(t + 1).start()

        @pl.when(t + 2 < total)
        def _():
            fetch(t + 2).start()

        fetch(t).wait()
        gw = _dot_tn(h_ref[...], ring_ref[t % slots])

        @pl.when(i == 0)
        def _():
            gw_ref[...] = gw

        @pl.when(i > 0)
        def _():
            gw_ref[...] += gw

    outs, rode = _host_call(
        body,
        name=f"inproj_bwd_w_l{layer}",
        grid=(N_COL_TILES, ni),
        in_specs=[pl.BlockSpec((tm, D_MODEL), lambda s, i: (i, 0)), ANY],
        out_specs=[pl.BlockSpec((None, D_MODEL, COL_TILE), lambda s, i: (s // 2, 0, s % 2))],
        out_shape=[jax.ShapeDtypeStruct((N_CHIPS, D_MODEL, SHARD_COLS), F32)],
        scratch_shapes=[pltpu.VMEM((slots, tm, COL_TILE), BF16), pltpu.SemaphoreType.DMA((slots,))],
        args=[h2, dproj2],
        ride=ride,
    )
    return outs[0], rode


def _adamw_math(w, g, m, v):
    m = ADAM_B1 * m + (1.0 - ADAM_B1) * g
    v = ADAM_B2 * v + (1.0 - ADAM_B2) * jnp.square(g)
    m_hat = m / (1.0 - ADAM_B1 ** ADAM_STEP)
    v_hat = v / (1.0 - ADAM_B2 ** ADAM_STEP)
    delta = -ADAM_LR * (m_hat / (jnp.sqrt(v_hat) + ADAM_EPS) + ADAM_WD * w)
    return delta, m, v


def _adamw_sharded(w, m, v, part, layer, prev, name, ride=None, after=None):
    assert prev is None or after is None
    _, rows, cols = w.shape
    tr = _tile(rows, max(16, (1 << 21) // (4 * cols) // 16 * 16), mult=16)

    def body(w_ref, m_ref, v_ref, p_ref, *rest):
        g_out, d_out, m_out, v_out = rest[-4:]
        g = ((p_ref[0].astype(F32) + p_ref[1].astype(F32)) + p_ref[2].astype(F32)) + p_ref[3].astype(F32)
        delta, m2, v2 = _adamw_math(w_ref[...], g, m_ref[...], v_ref[...])
        g_out[...] = g
        d_out[...] = delta
        m_out[...] = m2
        v_out[...] = v2

    spec = pl.BlockSpec((None, tr, cols), lambda i: (layer, i, 0))
    return _host_call(
        body,
        name=f"adamw_{name}_l{layer}",
        grid=(rows // tr,),
        in_specs=[spec, spec, spec, pl.BlockSpec((N_CHIPS, tr, cols), lambda i: (0, i, 0))]
        + ([ANY] * 4 if prev else []) + ([ANY] if after is not None else []),
        out_specs=[spec] * 4,
        out_shape=[jax.ShapeDtypeStruct(w.shape, F32)] * 4,
        args=[w, m, v, part] + (list(prev) if prev else []) + ([after] if after is not None else []),
        aliases={4 + k: k for k in range(4)} if prev else {},
        ride=ride,
    )


def _adamw_packed(w, m, v, g):
    rows = w.shape[0]
    tr = _tile(rows, rows // 2 if rows % 16 == 0 else rows)

    def body(w_ref, m_ref, v_ref, g_ref, d_out, m_out, v_out):
        delta, m2, v2 = _adamw_math(w_ref[...], g_ref[...], m_ref[...], v_ref[...])
        d_out[...] = delta
        m_out[...] = m2
        v_out[...] = v2

    spec = pl.BlockSpec((tr, GROUP_W), lambda i: (i, 0))
    return pl.pallas_call(
        body,
        name="adamw_small",
        grid=(rows // tr,),
        in_specs=[spec] * 4,
        out_specs=[spec] * 3,
        out_shape=[jax.ShapeDtypeStruct(w.shape, F32)] * 3,
        compiler_params=_params("arbitrary"),
    )(w, m, v, g)


SMALL = ["norm_g", "pool_w", "pool_scale", "conv_b", "conv_ln_g", "conv_ln_b", "sgu_ln_g", "sgu_ln_b", "sgu_w",
         "sgu_b", "final_g"]
WEIGHTS = ["norm_g", "w_in", "pool_w", "pool_scale", "conv_w", "conv_b", "conv_ln_g", "conv_ln_b", "sgu_ln_g",
           "sgu_ln_b", "sgu_w", "sgu_b", "sc_w", "w_branch", "w_o", "final_g"]
HALF_SHAPES = [(D_MODEL // 2, SHARD_COLS), (N_BRANCH * BRANCH_W // 2, BR_SHARD), (BR_SHARD // 2, D_MODEL),
               (CS_ROWS // 2, GROUP_W)]


def _pack_small(tree, last_rows=None):
    tail = jnp.zeros((8, GROUP_W), F32) if last_rows is None else last_rows
    return jnp.concatenate([tree[k].reshape(-1, GROUP_W) for k in SMALL] + [tail], axis=0)


def _unpack_small(packed, like):
    out, r = {}, 0
    for k in SMALL:
        nr = like[k].size // GROUP_W
        out[k] = packed[r:r + nr].reshape(like[k].shape)
        r += nr
    return out


def _pad_rows(a, rows):
    pad = [(0, 0)] * a.ndim
    pad[-2] = (0, rows - a.shape[-2])
    return jnp.pad(a, pad)


def _pack_cs(conv, short):
    return jnp.concatenate([_pad_rows(conv, CS_ROWS // 2), _pad_rows(short, CS_ROWS // 2)], axis=-2)


def _shard_major_rows(a):
    return a.reshape(a.shape[0], N_CHIPS, GROUP_W).transpose(1, 0, 2)


def kernel(x, norm_g, w_in, pool_w, pool_scale, conv_w, conv_b, conv_ln_g, conv_ln_b, sgu_ln_g, sgu_ln_b, sgu_w, sgu_b, sc_w, w_branch, w_o, final_g, loss_target, m_norm_g, m_w_in, m_pool_w, m_pool_scale, m_conv_w, m_conv_b, m_conv_ln_g, m_conv_ln_b, m_sgu_ln_g, m_sgu_ln_b, m_sgu_w, m_sgu_b, m_sc_w, m_w_branch, m_w_o, m_final_g, v_norm_g, v_w_in, v_pool_w, v_pool_scale, v_conv_w, v_conv_b, v_conv_ln_g, v_conv_ln_b, v_sgu_ln_g, v_sgu_ln_b, v_sgu_w, v_sgu_b, v_sc_w, v_w_branch, v_w_o, v_final_g):
    w = dict(norm_g=norm_g, w_in=w_in, pool_w=pool_w, pool_scale=pool_scale, conv_w=conv_w, conv_b=conv_b,
             conv_ln_g=conv_ln_g, conv_ln_b=conv_ln_b, sgu_ln_g=sgu_ln_g, sgu_ln_b=sgu_ln_b, sgu_w=sgu_w,
             sgu_b=sgu_b, sc_w=sc_w, w_branch=w_branch, w_o=w_o, final_g=final_g)
    mom = dict(norm_g=m_norm_g, w_in=m_w_in, pool_w=m_pool_w, pool_scale=m_pool_scale, conv_w=m_conv_w,
               conv_b=m_conv_b, conv_ln_g=m_conv_ln_g, conv_ln_b=m_conv_ln_b, sgu_ln_g=m_sgu_ln_g,
               sgu_ln_b=m_sgu_ln_b, sgu_w=m_sgu_w, sgu_b=m_sgu_b, sc_w=m_sc_w, w_branch=m_w_branch, w_o=m_w_o,
               final_g=m_final_g)
    var = dict(norm_g=v_norm_g, w_in=v_w_in, pool_w=v_pool_w, pool_scale=v_pool_scale, conv_w=v_conv_w,
               conv_b=v_conv_b, conv_ln_g=v_conv_ln_g, conv_ln_b=v_conv_ln_b, sgu_ln_g=v_sgu_ln_g,
               sgu_ln_b=v_sgu_ln_b, sgu_w=v_sgu_w, sgu_b=v_sgu_b, sc_w=v_sc_w, w_branch=v_w_branch, w_o=v_w_o,
               final_g=v_final_g)

    nb, seq, _ = x.shape
    n = nb * seq
    core = lax.axis_index("c").astype(jnp.int32).reshape(1)

    win_s = w_in.astype(BF16).reshape(DEPTH, 2, D_MODEL // 2, SHARD_COLS)
    wbr_s = w_branch.astype(BF16)
    wo_s = w_o.astype(BF16)
    cs_s = _pack_cs(conv_w, sc_w)

    (win_f0,) = _allgather_layer0([win_s])

    def layer_weights(win_f, wbr_f, wo_f, cs_f):
        cs_f = cs_f.reshape(N_CHIPS, CS_ROWS, GROUP_W)
        return dict(win=win_f.reshape(N_CHIPS, D_MODEL, SHARD_COLS),
                    wbr=wbr_f.reshape(N_CHIPS, N_BRANCH, BRANCH_W, BR_SHARD).transpose(1, 2, 0, 3).reshape(
                        N_BRANCH, BRANCH_W, D_MODEL),
                    wo=wo_f.reshape(D_MODEL, D_MODEL),
                    conv_w=cs_f[:, :CONV_ROWS], sc_w=cs_f[:, CS_ROWS // 2:CS_ROWS // 2 + SHORT_ROWS])

    def mixer_weights(l, gathered):
        row = lambda a: a[l].reshape(1, BRANCH_W)
        bias = jnp.repeat(jnp.swapaxes(sgu_b[l], 0, 1), GROUP_W, axis=1)
        return dict(pool_w=pool_w[l].astype(BF16), pool_scale=row(pool_scale), conv_w=gathered["conv_w"],
                    conv_b=row(conv_b), conv_ln_g=row(conv_ln_g), conv_ln_b=row(conv_ln_b), sgu_ln_g=row(sgu_ln_g),
                    sgu_ln_b=row(sgu_ln_b), sgu_w=sgu_w[l], sgu_bias=bias, sc_w=gathered["sc_w"])

    lw, mw = [None, None], [None, None]

    xs, projs, hs, zs, mgs, caccs = [x.reshape(n, D_MODEL)], [], [], [], [], []
    for l in range(DEPTH):
        first = l == 0
        win_l = win_f0.reshape(N_CHIPS, D_MODEL, SHARD_COLS) if first else lw[1]["win"]
        ride = _join(_gather_whole([wbr_s, wo_s, cs_s], 0), _gather_layer1_halves(win_s)) if first else None
        (proj, h), rode = _inproj_fwd(xs[l], norm_g[l:l + 1], win_l, l, ride=ride)
        if first:
            lw[0] = layer_weights(win_f0, *rode[:3])
            mw[0] = mixer_weights(0, lw[0])
            win_f1 = rode[3]
        proj3, x3 = proj.reshape(nb, seq, IN_COLS), xs[l].reshape(nb, seq, D_MODEL)
        if first:
            ride = _join(_forward_layer1_halves(win_f1), _gather_whole([wbr_s, wo_s, cs_s], 1))
            (z3, cacc, mg3, xn3), rode = _mix_merge_fwd(proj3, x3, mw[0], lw[0]["wbr"], lw[0]["wo"], 0, ride=ride)
            lw[1] = layer_weights(*rode)
            mw[1] = mixer_weights(1, lw[1])
            xs.append(xn3.reshape(n, D_MODEL))
        else:
            (z3, cacc, mg3, dx3, loss_part, g_final), _ = _mix_merge_fwd(
                proj3, x3, mw[l], lw[l]["wbr"], lw[l]["wo"], l,
                head=(final_g.reshape(1, D_MODEL), loss_target))
            dx = dx3.reshape(n, D_MODEL)
        caccs.append(cacc)
        projs.append(proj)
        hs.append(h)
        zs.append(z3.reshape(n, N_BRANCH * BRANCH_W))
        mgs.append(mg3.reshape(n, D_MODEL))

    def chip_major(gwbr):
        return gwbr.reshape(N_BRANCH, BRANCH_W, N_CHIPS, BR_SHARD).transpose(2, 0, 1, 3)

    def as_halves(arrays, first_index=0):
        return [g.reshape((N_CHIPS, 2) + s) for g, s in zip(arrays, HALF_SHAPES[first_index:])]

    def add_halves(grads, others, l, first_index=0):
        return [_add_halves(g, o, core, f"add_halves_l{l}_{first_index + i}")
                for i, (g, o) in enumerate(zip(grads, others))]

    per_layer, parts = {}, [None] * DEPTH
    (dz, dproj, gwo, gwbr), _ = _merge_bwd(dx, mgs[1], zs[1], projs[1], lw[1]["wbr"], lw[1]["wo"], 1)
    dproj3, gm, _ = _mixers_bwd(projs[1].reshape(nb, seq, IN_COLS), dz.reshape(nb, seq, N_BRANCH * BRANCH_W), caccs[1],
                                dproj.reshape(nb, seq, IN_COLS), mw[1], 1)
    dproj = dproj3.reshape(n, IN_COLS)
    gwin, _ = _inproj_bwd_w(hs[1], dproj, 1)
    gcs = _pack_cs(_shard_major_rows(gm["conv_w"]), _shard_major_rows(gm["sc_w"]))
    grads1 = as_halves([gwin, chip_major(gwbr), gwo, gcs])
    (dx, g_norm), others1 = _inproj_bwd_x(dproj, lw[1]["win"], xs[1], norm_g[1:2], dx, 1, ride=_swap_halves(grads1))
    gm["norm_g"] = g_norm
    per_layer[1] = gm
    sums1 = add_halves(grads1, others1, 1)
    (dz, dproj, gwo, gwbr), scattered1 = _merge_bwd(dx, mgs[0], zs[0], projs[0], lw[0]["wbr"], lw[0]["wo"], 0,
                                                    ride=_scatter_chip_sums(sums1))
    dproj3, gm, parts[1] = _mixers_bwd(projs[0].reshape(nb, seq, IN_COLS), dz.reshape(nb, seq, N_BRANCH * BRANCH_W),
                                       caccs[0], dproj.reshape(nb, seq, IN_COLS), mw[0], 0,
                                       ride=_share_halves(scattered1))
    dproj = dproj3.reshape(n, IN_COLS)
    gcs = _pack_cs(_shard_major_rows(gm["conv_w"]), _shard_major_rows(gm["sc_w"]))
    small0 = as_halves([chip_major(gwbr), gwo, gcs], first_index=1)

    per_layer[0] = dict(gm, norm_g=jnp.zeros((1, D_MODEL), F32))

    def stack(k, shape):
        return jnp.stack([per_layer[l][k] for l in range(DEPTH)]).reshape(shape)

    g_small = {"final_g": g_final.reshape(D_MODEL), "norm_g": stack("norm_g", (DEPTH, D_MODEL)),
               "pool_w": stack("pool_w", pool_w.shape), "sgu_w": stack("sgu_w", sgu_w.shape),
               "sgu_b": jnp.swapaxes(stack("sgu_b", (DEPTH, CHUNK, GROUP_W))[:, :, :N_GROUPS], 1, 2)}
    for k in ("pool_scale", "conv_b", "conv_ln_g", "conv_ln_b", "sgu_ln_g", "sgu_ln_b"):
        g_small[k] = stack(k, (DEPTH, BRANCH_W))
    small_part = _pack_small(g_small, jnp.broadcast_to(loss_part, (8, GROUP_W)))

    gwin, rode = _inproj_bwd_w(hs[0], dproj, 0, ride=_join(_swap_halves(small0), _swap_whole(small_part)))
    others_small0, small_other = rode[:3], rode[3]
    small_chip = _add2(small_part, small_other, "add_small_sibling")
    (gwin_h,) = as_halves([gwin])

    def flat(a, i):
        rows, cols = 2 * HALF_SHAPES[i][0], HALF_SHAPES[i][1]
        return a.reshape(a.shape[0], rows, cols)

    names4 = ["w_in", "w_branch", "w_o", "conv_sc"]

    def packed4(t):
        return [t["w_in"], t["w_branch"], t["w_o"], _pack_cs(t["conv_w"], t["sc_w"])]

    w4, m4, v4 = ([flat(a, i) for i, a in enumerate(packed4(t))] for t in (w, mom, var))

    def sharded_update(i, layer, prev, after=None):
        return _adamw_sharded(w4[i], m4[i], v4[i], flat(parts[layer][i], i), layer, prev, names4[i], after=after)

    send_sem, recv_sem, gwin_thru, land_thru, token = _swap_halves_start(gwin_h, "swap_halves_l0_w_in")
    updated = [sharded_update(i, 1, None, after=token if i == 0 else None)[0] for i in range(4)]
    gwin_h, other_win0 = _swap_halves_wait(send_sem, recv_sem, gwin_thru, land_thru, updated[0][0],
                                           "swap_halves_l0_w_in")
    sums0 = add_halves([gwin_h], [other_win0], 0) + add_halves(small0, others_small0, 0, first_index=1)
    (dx, g_norm0), rode = _inproj_bwd_x(dproj, lw[0]["win"], xs[0], norm_g[0:1], dx, 0,
                                        ride=_join(_scatter_chip_sums(sums0), _spread_chip_sums(small_chip)))
    scattered0, small_chips = rode[:4], rode[4]
    grad_x = dx.reshape(nb, seq, D_MODEL)

    g_norm0, parts[0] = _allreduce_small(g_norm0.reshape(D_MODEL // GROUP_W, GROUP_W), _share_halves(scattered0))
    g_packed = lax.dynamic_update_slice(_sum_chips(small_chips, "sum_small_chips"), g_norm0, (0, 0))
    loss = g_packed[g_packed.shape[0] - 8, 0]
    d_packed, m_packed, v_packed = _adamw_packed(_pack_small(w), _pack_small(mom), _pack_small(var), g_packed)
    grads, deltas, new_m, new_v = {}, {}, {}, {}
    for tree, packed in ((grads, g_packed), (deltas, d_packed), (new_m, m_packed), (new_v, v_packed)):
        tree.update(_unpack_small(packed, w))

    for i, name in enumerate(names4):
        res, _ = sharded_update(i, 0, updated[i])
        for tree, r in zip((grads, deltas, new_m, new_v), res):
            if name == "conv_sc":
                tree["conv_w"] = r[:, :CONV_K]
                tree["sc_w"] = r[:, CS_ROWS // 2:CS_ROWS // 2 + SHORT_K]
            else:
                tree[name] = r.reshape(w[name].shape)

    return (loss, grad_x, *[grads[k] for k in WEIGHTS], *[deltas[k] for k in WEIGHTS],
            *[new_m[k] for k in WEIGHTS], *[new_v[k] for k in WEIGHTS])
```
